```python
import jax, jax.numpy as jnp
from jax import lax
import numpy as np

D_MODEL = 1024
BATCH = 16
SEQ = 4096
DEPTH = 2

D_PLE = 256
D_SSM = 256
D_DN = 512
D_SG = 256
D_MIX = D_SSM + D_DN + D_SG
SSM_GROUP = 16
SSM_GROUPS = D_SSM // SSM_GROUP
SSM_STATE = 64
DN_HEADS = 4
DN_HEAD_DIM = D_DN // DN_HEADS
DN_CONV = 4
DN_CHUNK = 64
SG_HEADS = 4
SG_HEAD_DIM = D_SG // SG_HEADS
SG_CHUNK = 128
EPS = 1e-6
SPLITS = (D_SSM, D_SSM, 3 * D_DN, DN_HEADS, DN_HEADS, D_DN, D_SG, D_SG, D_SG)
D_IN = 2 * D_SSM + 4 * D_DN + 2 * DN_HEADS + 3 * D_SG

kernel_name = "hymba_style_s5_gdn_gmlp_ple"


def rms_norm(x, g):
    xf = x.astype(jnp.float32)
    y = xf * lax.rsqrt(jnp.mean(xf * xf, axis=-1, keepdims=True) + EPS)
    return (y * g.astype(jnp.float32)).astype(x.dtype)


def layer_norm(x, g, b):
    xf = x.astype(jnp.float32)
    mu = jnp.mean(xf, axis=-1, keepdims=True)
    xc = xf - mu
    y = xc * lax.rsqrt(jnp.mean(xc * xc, axis=-1, keepdims=True) + EPS)
    return (y * g.astype(jnp.float32) + b.astype(jnp.float32)).astype(x.dtype)


def l2_normalize(x):
    return x * lax.rsqrt(jnp.sum(x * x, axis=-1, keepdims=True) + EPS)


def split_cols(z):
    idx = np.cumsum(np.array(SPLITS))[:-1].tolist()
    return jnp.split(z, idx, axis=-1)


def complex_linear_combine(e1, e2):
    a1r, a1i, b1r, b1i = e1
    a2r, a2i, b2r, b2i = e2
    ar = a2r * a1r - a2i * a1i
    ai = a2r * a1i + a2i * a1r
    br = a2r * b1r - a2i * b1i + b2r
    bi = a2r * b1i + a2i * b1r + b2i
    return (ar, ai, br, bi)


def s5_branch(u, a_re, a_im, b_re, b_im, c_re, c_im, d_skip, log_step, w_glu, b_glu):
    bsz, seq, _ = u.shape
    f32 = jnp.float32
    uf = u.astype(f32).reshape(bsz, seq, SSM_GROUPS, SSM_GROUP)
    step = jnp.exp(log_step.astype(f32))[:, None]
    ar, ai = a_re.astype(f32), a_im.astype(f32)
    mag = jnp.exp(ar * step)
    lam_re = mag * jnp.cos(ai * step)
    lam_im = mag * jnp.sin(ai * step)
    den = ar * ar + ai * ai
    nr, ni = lam_re - 1.0, lam_im
    f_re = (nr * ar + ni * ai) / den
    f_im = (ni * ar - nr * ai) / den
    br, bi = b_re.astype(f32), b_im.astype(f32)
    bbar_re = f_re[..., None] * br - f_im[..., None] * bi
    bbar_im = f_re[..., None] * bi + f_im[..., None] * br
    bu_re = jnp.einsum('bsgc,gnc->bsgn', uf, bbar_re)
    bu_im = jnp.einsum('bsgc,gnc->bsgn', uf, bbar_im)
    lam_re_s = jnp.broadcast_to(lam_re, (1, seq, SSM_GROUPS, SSM_STATE))
    lam_im_s = jnp.broadcast_to(lam_im, (1, seq, SSM_GROUPS, SSM_STATE))
    _, _, h_re, h_im = lax.associative_scan(
        complex_linear_combine, (lam_re_s, lam_im_s, bu_re, bu_im), axis=1)
    y = (jnp.einsum('bsgn,gcn->bsgc', h_re, c_re.astype(f32))
         - jnp.einsum('bsgn,gcn->bsgc', h_im, c_im.astype(f32))
         + d_skip.astype(f32) * uf)
    y = jax.nn.gelu(y.reshape(bsz, seq, D_SSM))
    y = y * jax.nn.sigmoid(jnp.einsum('bse,ef->bsf', y, w_glu.astype(f32)) + b_glu.astype(f32))
    return y.astype(u.dtype)


def causal_depthwise_conv(x, w):
    k_len, ch = w.shape
    return lax.conv_general_dilated(
        x, w[:, None, :], window_strides=(1,), padding=[(k_len - 1, 0)],
        dimension_numbers=('NWC', 'WIO', 'NWC'), feature_group_count=ch)


def gated_deltanet_branch(qkv, a_in, b_in, conv_w, a_log, dt_bias, norm_g):
    bsz, seq, _ = qkv.shape
    f32 = jnp.float32
    H, Dh, C = DN_HEADS, DN_HEAD_DIM, DN_CHUNK
    nc = seq // C
    qkv = jax.nn.silu(causal_depthwise_conv(qkv.astype(f32), conv_w.astype(f32)))
    q, k, v = jnp.split(qkv, 3, axis=-1)

    def to_chunks(t):
        return t.reshape(bsz, nc, C, H, Dh).transpose(0, 3, 1, 2, 4)

    def head_scalars(t):
        return t.reshape(bsz, nc, C, H).transpose(0, 3, 1, 2)

    q = l2_normalize(to_chunks(q)) * (Dh ** -0.5)
    k = l2_normalize(to_chunks(k))
    v = to_chunks(v)
    beta = head_scalars(jax.nn.sigmoid(b_in.astype(f32)))
    g = head_scalars(-jnp.exp(a_log.astype(f32)) * jax.nn.softplus(a_in.astype(f32) + dt_bias.astype(f32)))
    gc = jnp.cumsum(g, axis=-1)
    causal = jnp.tril(jnp.ones((C, C), dtype=bool))
    strict = jnp.tril(jnp.ones((C, C), dtype=bool), k=-1)
    diff = gc[..., :, None] - gc[..., None, :]
    decay = jnp.where(causal, jnp.exp(jnp.where(causal, diff, 0.0)), 0.0)
    kb = k * beta[..., None]
    vb = v * beta[..., None]
    m = jnp.where(strict, jnp.einsum('bhnid,bhnjd->bhnij', kb, k) * decay, 0.0)
    rhs = jnp.concatenate([vb, kb * jnp.exp(gc)[..., None]], axis=-1)
    sol = lax.linalg.triangular_solve(m, rhs, left_side=True, lower=True, unit_diagonal=True)
    value, k_cd = sol[..., :Dh], sol[..., Dh:]
    attn = jnp.einsum('bhnid,bhnjd->bhnij', q, k) * decay
    q_dec = q * jnp.exp(gc)[..., None]
    k_dec = k * jnp.exp(gc[..., -1:] - gc)[..., None]
    last = jnp.exp(gc[..., -1])
    xs = (jnp.moveaxis(value, 2, 0), jnp.moveaxis(k_cd, 2, 0), jnp.moveaxis(attn, 2, 0),
          jnp.moveaxis(q_dec, 2, 0), jnp.moveaxis(k_dec, 2, 0), jnp.moveaxis(last, 2, 0))

    def chunk_step(state, inp):
        val, kcd, att, qd, kd, dl = inp
        v_new = val - jnp.einsum('bhcd,bhde->bhce', kcd, state)
        o = jnp.einsum('bhcd,bhde->bhce', qd, state) + jnp.einsum('bhij,bhje->bhie', att, v_new)
        state = state * dl[..., None, None] + jnp.einsum('bhcd,bhce->bhde', kd, v_new)
        return state, o

    s0 = jnp.zeros((bsz, H, Dh, Dh), f32)
    _, o = lax.scan(chunk_step, s0, xs)
    o = o.transpose(1, 0, 3, 2, 4).reshape(bsz, seq, H, Dh)
    o = rms_norm(o, norm_g)
    return o.reshape(bsz, seq, D_DN).astype(a_in.dtype)


def spatial_gating_branch(u, v, ln_g, ln_b, w_sp, b_sp):
    bsz, seq, _ = u.shape
    nch = seq // SG_CHUNK
    u = jax.nn.gelu(u)
    v = layer_norm(jax.nn.gelu(v), ln_g, ln_b)
    vh = v.reshape(bsz, nch, SG_CHUNK, SG_HEADS, SG_HEAD_DIM)
    causal = jnp.tril(jnp.ones((SG_CHUNK, SG_CHUNK), dtype=bool))
    w = jnp.where(causal, w_sp, 0.0)
    s = jnp.einsum('hts,bnshc->bnthc', w, vh) + jnp.transpose(b_sp)[:, :, None]
    return u * s.reshape(bsz, seq, D_SG)


def _fwd_setup_inputs(seed: int = 0) -> dict:
    key = jax.random.key(seed)
    ks = jax.random.split(key, 32)
    f32 = jnp.float32
    L, D = DEPTH, D_MODEL
    G, N, Cg = SSM_GROUPS, SSM_STATE, SSM_GROUP

    def nrm(k, shape, scale):
        return scale * jax.random.normal(k, shape, f32)

    x = jax.random.normal(ks[0], (BATCH, SEQ, D), f32)
    p = jax.random.normal(ks[1], (DEPTH, BATCH, SEQ, D_PLE), f32)
    norm_g = 1.0 + nrm(ks[2], (L, D), 0.02)
    w_in = nrm(ks[3], (L, D, D_IN), D ** -0.5)
    ssm_a_re = -0.5 + nrm(ks[4], (L, G, N), 0.01)
    ssm_a_im = jnp.pi * jnp.arange(N, dtype=f32) + nrm(ks[5], (L, G, N), 0.01)
    ssm_b_re = nrm(ks[6], (L, G, N, Cg), Cg ** -0.5)
    ssm_b_im = nrm(ks[7], (L, G, N, Cg), Cg ** -0.5)
    ssm_c_re = nrm(ks[8], (L, G, Cg, N), N ** -0.5)
    ssm_c_im = nrm(ks[9], (L, G, Cg, N), N ** -0.5)
    ssm_d = nrm(ks[10], (L, G, Cg), 1.0)
    ssm_log_step = jax.random.uniform(ks[11], (L, G), f32, np.log(1e-3), np.log(1e-1))
    ssm_w_glu = nrm(ks[12], (L, D_SSM, D_SSM), D_SSM ** -0.5)
    ssm_b_glu = nrm(ks[13], (L, D_SSM), 0.01)
    dn_conv_w = nrm(ks[14], (L, DN_CONV, 3 * D_DN), DN_CONV ** -0.5)
    dn_a_log = jnp.log(jax.random.uniform(ks[15], (L, DN_HEADS), f32, 1.0, 16.0))
    dt = jnp.exp(jax.random.uniform(ks[16], (L, DN_HEADS), f32, np.log(1e-3), np.log(1e-1)))
    dn_dt_bias = dt + jnp.log(-jnp.expm1(-dt))
    dn_norm_g = 1.0 + nrm(ks[17], (L, DN_HEAD_DIM), 0.02)
    sg_ln_g = 1.0 + nrm(ks[18], (L, D_SG), 0.02)
    sg_ln_b = nrm(ks[19], (L, D_SG), 0.01)
    sg_w = nrm(ks[20], (L, SG_HEADS, SG_CHUNK, SG_CHUNK), SG_CHUNK ** -0.5)
    sg_b = 1.0 + nrm(ks[21], (L, SG_HEADS, SG_CHUNK), 0.02)
    w_out = nrm(ks[22], (L, D_MIX, D), D_MIX ** -0.5)
    ple_norm_g = 1.0 + nrm(ks[23], (L, D), 0.02)
    w_ple_gate = nrm(ks[24], (L, D, D), D ** -0.5)
    w_ple = nrm(ks[25], (L, D_PLE, D), D_PLE ** -0.5)
    final_norm_g = 1.0 + nrm(ks[26], (D,), 0.02)
    return {"x": x, "p": p, "norm_g": norm_g, "w_in": w_in,
            "ssm_a_re": ssm_a_re, "ssm_a_im": ssm_a_im, "ssm_b_re": ssm_b_re, "ssm_b_im": ssm_b_im,
            "ssm_c_re": ssm_c_re, "ssm_c_im": ssm_c_im, "ssm_d": ssm_d, "ssm_log_step": ssm_log_step,
            "ssm_w_glu": ssm_w_glu, "ssm_b_glu": ssm_b_glu,
            "dn_conv_w": dn_conv_w, "dn_a_log": dn_a_log, "dn_dt_bias": dn_dt_bias, "dn_norm_g": dn_norm_g,
            "sg_ln_g": sg_ln_g, "sg_ln_b": sg_ln_b, "sg_w": sg_w, "sg_b": sg_b,
            "w_out": w_out, "ple_norm_g": ple_norm_g, "w_ple_gate": w_ple_gate, "w_ple": w_ple,
            "final_norm_g": final_norm_g}


def _fwd_reference(x, p, norm_g, w_in, ssm_a_re, ssm_a_im, ssm_b_re, ssm_b_im, ssm_c_re, ssm_c_im,
              ssm_d, ssm_log_step, ssm_w_glu, ssm_b_glu, dn_conv_w, dn_a_log, dn_dt_bias, dn_norm_g,
              sg_ln_g, sg_ln_b, sg_w, sg_b, w_out, ple_norm_g, w_ple_gate, w_ple, final_norm_g):
    for i in range(DEPTH):
        h = rms_norm(x, norm_g[i])
        z = jnp.einsum('bsd,de->bse', h, w_in[i])
        u_ssm, g_ssm, qkv, a_dn, b_dn, g_dn, u_sg, v_sg, g_sg = split_cols(z)
        y_ssm = s5_branch(u_ssm, ssm_a_re[i], ssm_a_im[i], ssm_b_re[i], ssm_b_im[i],
                          ssm_c_re[i], ssm_c_im[i], ssm_d[i], ssm_log_step[i],
                          ssm_w_glu[i], ssm_b_glu[i]) * jax.nn.silu(g_ssm)
        y_dn = gated_deltanet_branch(qkv, a_dn, b_dn, dn_conv_w[i], dn_a_log[i], dn_dt_bias[i],
                                     dn_norm_g[i]) * jax.nn.silu(g_dn)
        y_sg = spatial_gating_branch(u_sg, v_sg, sg_ln_g[i], sg_ln_b[i], sg_w[i], sg_b[i]) * jax.nn.silu(g_sg)
        y = jnp.concatenate([y_ssm, y_dn, y_sg], axis=-1)
        x = x + jnp.einsum('bse,ed->bsd', y, w_out[i])
        gate = jax.nn.sigmoid(jnp.einsum('bsd,de->bse', rms_norm(x, ple_norm_g[i]), w_ple_gate[i]))
        x = x + gate * jnp.einsum('bsk,kd->bsd', p[i], w_ple[i])
    return rms_norm(x, final_norm_g)


import jax as _jax
import jax.numpy as _jnp

TWIN_FORMAT = 'train_step'
FWD_PARAMS = ['x', 'p', 'norm_g', 'w_in', 'ssm_a_re', 'ssm_a_im', 'ssm_b_re', 'ssm_b_im', 'ssm_c_re', 'ssm_c_im', 'ssm_d', 'ssm_log_step', 'ssm_w_glu', 'ssm_b_glu', 'dn_conv_w', 'dn_a_log', 'dn_dt_bias', 'dn_norm_g', 'sg_ln_g', 'sg_ln_b', 'sg_w', 'sg_b', 'w_out', 'ple_norm_g', 'w_ple_gate', 'w_ple', 'final_norm_g']
TWIN_WEIGHTS = ['norm_g', 'w_in', 'ssm_a_re', 'ssm_a_im', 'ssm_b_re', 'ssm_b_im', 'ssm_c_re', 'ssm_c_im', 'ssm_d', 'ssm_log_step', 'ssm_w_glu', 'ssm_b_glu', 'dn_conv_w', 'dn_a_log', 'dn_dt_bias', 'dn_norm_g', 'sg_ln_g', 'sg_ln_b', 'sg_w', 'sg_b', 'w_out', 'ple_norm_g', 'w_ple_gate', 'w_ple', 'final_norm_g']
TWIN_DIFF_INPUT = 'x'
TWIN_INPUTS = ['x', 'p', 'norm_g', 'w_in', 'ssm_a_re', 'ssm_a_im', 'ssm_b_re', 'ssm_b_im', 'ssm_c_re', 'ssm_c_im', 'ssm_d', 'ssm_log_step', 'ssm_w_glu', 'ssm_b_glu', 'dn_conv_w', 'dn_a_log', 'dn_dt_bias', 'dn_norm_g', 'sg_ln_g', 'sg_ln_b', 'sg_w', 'sg_b', 'w_out', 'ple_norm_g', 'w_ple_gate', 'w_ple', 'final_norm_g', 'loss_target', 'm_norm_g', 'm_w_in', 'm_ssm_a_re', 'm_ssm_a_im', 'm_ssm_b_re', 'm_ssm_b_im', 'm_ssm_c_re', 'm_ssm_c_im', 'm_ssm_d', 'm_ssm_log_step', 'm_ssm_w_glu', 'm_ssm_b_glu', 'm_dn_conv_w', 'm_dn_a_log', 'm_dn_dt_bias', 'm_dn_norm_g', 'm_sg_ln_g', 'm_sg_ln_b', 'm_sg_w', 'm_sg_b', 'm_w_out', 'm_ple_norm_g', 'm_w_ple_gate', 'm_w_ple', 'm_final_norm_g', 'v_norm_g', 'v_w_in', 'v_ssm_a_re', 'v_ssm_a_im', 'v_ssm_b_re', 'v_ssm_b_im', 'v_ssm_c_re', 'v_ssm_c_im', 'v_ssm_d', 'v_ssm_log_step', 'v_ssm_w_glu', 'v_ssm_b_glu', 'v_dn_conv_w', 'v_dn_a_log', 'v_dn_dt_bias', 'v_dn_norm_g', 'v_sg_ln_g', 'v_sg_ln_b', 'v_sg_w', 'v_sg_b', 'v_w_out', 'v_ple_norm_g', 'v_w_ple_gate', 'v_w_ple', 'v_final_norm_g']
TWIN_OUTPUTS = ['loss', 'grad_x', 'grad_norm_g', 'grad_w_in', 'grad_ssm_a_re', 'grad_ssm_a_im', 'grad_ssm_b_re', 'grad_ssm_b_im', 'grad_ssm_c_re', 'grad_ssm_c_im', 'grad_ssm_d', 'grad_ssm_log_step', 'grad_ssm_w_glu', 'grad_ssm_b_glu', 'grad_dn_conv_w', 'grad_dn_a_log', 'grad_dn_dt_bias', 'grad_dn_norm_g', 'grad_sg_ln_g', 'grad_sg_ln_b', 'grad_sg_w', 'grad_sg_b', 'grad_w_out', 'grad_ple_norm_g', 'grad_w_ple_gate', 'grad_w_ple', 'grad_final_norm_g', 'delta_norm_g', 'delta_w_in', 'delta_ssm_a_re', 'delta_ssm_a_im', 'delta_ssm_b_re', 'delta_ssm_b_im', 'delta_ssm_c_re', 'delta_ssm_c_im', 'delta_ssm_d', 'delta_ssm_log_step', 'delta_ssm_w_glu', 'delta_ssm_b_glu', 'delta_dn_conv_w', 'delta_dn_a_log', 'delta_dn_dt_bias', 'delta_dn_norm_g', 'delta_sg_ln_g', 'delta_sg_ln_b', 'delta_sg_w', 'delta_sg_b', 'delta_w_out', 'delta_ple_norm_g', 'delta_w_ple_gate', 'delta_w_ple', 'delta_final_norm_g', 'new_m_norm_g', 'new_m_w_in', 'new_m_ssm_a_re', 'new_m_ssm_a_im', 'new_m_ssm_b_re', 'new_m_ssm_b_im', 'new_m_ssm_c_re', 'new_m_ssm_c_im', 'new_m_ssm_d', 'new_m_ssm_log_step', 'new_m_ssm_w_glu', 'new_m_ssm_b_glu', 'new_m_dn_conv_w', 'new_m_dn_a_log', 'new_m_dn_dt_bias', 'new_m_dn_norm_g', 'new_m_sg_ln_g', 'new_m_sg_ln_b', 'new_m_sg_w', 'new_m_sg_b', 'new_m_w_out', 'new_m_ple_norm_g', 'new_m_w_ple_gate', 'new_m_w_ple', 'new_m_final_norm_g', 'new_v_norm_g', 'new_v_w_in', 'new_v_ssm_a_re', 'new_v_ssm_a_im', 'new_v_ssm_b_re', 'new_v_ssm_b_im', 'new_v_ssm_c_re', 'new_v_ssm_c_im', 'new_v_ssm_d', 'new_v_ssm_log_step', 'new_v_ssm_w_glu', 'new_v_ssm_b_glu', 'new_v_dn_conv_w', 'new_v_dn_a_log', 'new_v_dn_dt_bias', 'new_v_dn_norm_g', 'new_v_sg_ln_g', 'new_v_sg_ln_b', 'new_v_sg_w', 'new_v_sg_b', 'new_v_w_out', 'new_v_ple_norm_g', 'new_v_w_ple_gate', 'new_v_w_ple', 'new_v_final_norm_g']
TWIN_LEAF_KINDS = {'loss': 'loss', 'grad_x': 'grad_x', 'grad_norm_g': 'grad_w', 'grad_w_in': 'grad_w', 'grad_ssm_a_re': 'grad_w', 'grad_ssm_a_im': 'grad_w', 'grad_ssm_b_re': 'grad_w', 'grad_ssm_b_im': 'grad_w', 'grad_ssm_c_re': 'grad_w', 'grad_ssm_c_im': 'grad_w', 'grad_ssm_d': 'grad_w', 'grad_ssm_log_step': 'grad_w', 'grad_ssm_w_glu': 'grad_w', 'grad_ssm_b_glu': 'grad_w', 'grad_dn_conv_w': 'grad_w', 'grad_dn_a_log': 'grad_w', 'grad_dn_dt_bias': 'grad_w', 'grad_dn_norm_g': 'grad_w', 'grad_sg_ln_g': 'grad_w', 'grad_sg_ln_b': 'grad_w', 'grad_sg_w': 'grad_w', 'grad_sg_b': 'grad_w', 'grad_w_out': 'grad_w', 'grad_ple_norm_g': 'grad_w', 'grad_w_ple_gate': 'grad_w', 'grad_w_ple': 'grad_w', 'grad_final_norm_g': 'grad_w', 'delta_norm_g': 'delta_w', 'delta_w_in': 'delta_w', 'delta_ssm_a_re': 'delta_w', 'delta_ssm_a_im': 'delta_w', 'delta_ssm_b_re': 'delta_w', 'delta_ssm_b_im': 'delta_w', 'delta_ssm_c_re': 'delta_w', 'delta_ssm_c_im': 'delta_w', 'delta_ssm_d': 'delta_w', 'delta_ssm_log_step': 'delta_w', 'delta_ssm_w_glu': 'delta_w', 'delta_ssm_b_glu': 'delta_w', 'delta_dn_conv_w': 'delta_w', 'delta_dn_a_log': 'delta_w', 'delta_dn_dt_bias': 'delta_w', 'delta_dn_norm_g': 'delta_w', 'delta_sg_ln_g': 'delta_w', 'delta_sg_ln_b': 'delta_w', 'delta_sg_w': 'delta_w', 'delta_sg_b': 'delta_w', 'delta_w_out': 'delta_w', 'delta_ple_norm_g': 'delta_w', 'delta_w_ple_gate': 'delta_w', 'delta_w_ple': 'delta_w', 'delta_final_norm_g': 'delta_w', 'new_m_norm_g': 'new_m', 'new_m_w_in': 'new_m', 'new_m_ssm_a_re': 'new_m', 'new_m_ssm_a_im': 'new_m', 'new_m_ssm_b_re': 'new_m', 'new_m_ssm_b_im': 'new_m', 'new_m_ssm_c_re': 'new_m', 'new_m_ssm_c_im': 'new_m', 'new_m_ssm_d': 'new_m', 'new_m_ssm_log_step': 'new_m', 'new_m_ssm_w_glu': 'new_m', 'new_m_ssm_b_glu': 'new_m', 'new_m_dn_conv_w': 'new_m', 'new_m_dn_a_log': 'new_m', 'new_m_dn_dt_bias': 'new_m', 'new_m_dn_norm_g': 'new_m', 'new_m_sg_ln_g': 'new_m', 'new_m_sg_ln_b': 'new_m', 'new_m_sg_w': 'new_m', 'new_m_sg_b': 'new_m', 'new_m_w_out': 'new_m', 'new_m_ple_norm_g': 'new_m', 'new_m_w_ple_gate': 'new_m', 'new_m_w_ple': 'new_m', 'new_m_final_norm_g': 'new_m', 'new_v_norm_g': 'new_v', 'new_v_w_in': 'new_v', 'new_v_ssm_a_re': 'new_v', 'new_v_ssm_a_im': 'new_v', 'new_v_ssm_b_re': 'new_v', 'new_v_ssm_b_im': 'new_v', 'new_v_ssm_c_re': 'new_v', 'new_v_ssm_c_im': 'new_v', 'new_v_ssm_d': 'new_v', 'new_v_ssm_log_step': 'new_v', 'new_v_ssm_w_glu': 'new_v', 'new_v_ssm_b_glu': 'new_v', 'new_v_dn_conv_w': 'new_v', 'new_v_dn_a_log': 'new_v', 'new_v_dn_dt_bias': 'new_v', 'new_v_dn_norm_g': 'new_v', 'new_v_sg_ln_g': 'new_v', 'new_v_sg_ln_b': 'new_v', 'new_v_sg_w': 'new_v', 'new_v_sg_b': 'new_v', 'new_v_w_out': 'new_v', 'new_v_ple_norm_g': 'new_v', 'new_v_w_ple_gate': 'new_v', 'new_v_w_ple': 'new_v', 'new_v_final_norm_g': 'new_v'}


def _forward(args):
    return _fwd_reference(*[args[k] for k in FWD_PARAMS])


def _output_shape():
    out = _jax.eval_shape(lambda: _forward(_fwd_setup_inputs(0)))
    return out.shape, out.dtype

N_MICROBATCH = 1
ADAM_LR = 0.001
ADAM_B1 = 0.9
ADAM_B2 = 0.999
ADAM_EPS = 1e-08
ADAM_WD = 0.01
ADAM_STEP = 10
PER_EXAMPLE_BATCH_AXIS = {'x': 0, 'p': 1, 'loss_target': 0}
SHARED_INPUTS = []
_WEIGHT_DTYPES = {'norm_g': _jnp.float32, 'w_in': _jnp.float32, 'ssm_a_re': _jnp.float32, 'ssm_a_im': _jnp.float32, 'ssm_b_re': _jnp.float32, 'ssm_b_im': _jnp.float32, 'ssm_c_re': _jnp.float32, 'ssm_c_im': _jnp.float32, 'ssm_d': _jnp.float32, 'ssm_log_step': _jnp.float32, 'ssm_w_glu': _jnp.float32, 'ssm_b_glu': _jnp.float32, 'dn_conv_w': _jnp.float32, 'dn_a_log': _jnp.float32, 'dn_dt_bias': _jnp.float32, 'dn_norm_g': _jnp.float32, 'sg_ln_g': _jnp.float32, 'sg_ln_b': _jnp.float32, 'sg_w': _jnp.float32, 'sg_b': _jnp.float32, 'w_out': _jnp.float32, 'ple_norm_g': _jnp.float32, 'w_ple_gate': _jnp.float32, 'w_ple': _jnp.float32, 'final_norm_g': _jnp.float32}
MOMENT_SCALE = {'norm_g': 1.737666e-01, 'w_in': 8.884642e-02, 'ssm_a_re': 5.354228e-03, 'ssm_a_im': 6.846537e-03, 'ssm_b_re': 2.227799e-03, 'ssm_b_im': 2.335433e-03, 'ssm_c_re': 4.604434e-03, 'ssm_c_im': 4.713971e-03, 'ssm_d': 5.053756e-02, 'ssm_log_step': 2.365449e+00, 'ssm_w_glu': 1.421757e-02, 'ssm_b_glu': 2.152519e-02, 'dn_conv_w': 8.979318e-02, 'dn_a_log': 4.400050e-01, 'dn_dt_bias': 4.279982e-01, 'dn_norm_g': 3.247583e-01, 'sg_ln_g': 5.584841e-02, 'sg_ln_b': 5.508123e-02, 'sg_w': 3.827593e-02, 'sg_b': 5.250367e-02, 'w_out': 9.744917e-02, 'ple_norm_g': 4.477829e-02, 'w_ple_gate': 4.135320e-02, 'w_ple': 1.054150e-01, 'final_norm_g': 6.395536e+01}


def _to_microbatches(a, axis):
    t = _jnp.moveaxis(a, axis, 0)
    t = t.reshape((N_MICROBATCH, t.shape[0] // N_MICROBATCH) + t.shape[1:])
    return _jnp.moveaxis(t, 1, axis + 1)


def setup_inputs(seed: int = 0) -> dict:
    inp = _fwd_setup_inputs(seed)
    key = _jax.random.fold_in(_jax.random.key(seed), 7919)
    shape, _ = _output_shape()
    out = dict(inp)
    out["loss_target"] = _jax.random.normal(_jax.random.fold_in(key, 0), shape, _jnp.float32)
    for i, name in enumerate(TWIN_WEIGHTS):
        w = inp[name].astype(_jnp.float32)
        if MOMENT_SCALE is None:
            s = _jnp.sqrt(_jnp.mean(_jnp.square(w)) + 1e-30)
        else:
            s = MOMENT_SCALE[name]
        km, kv = _jax.random.split(_jax.random.fold_in(key, i + 1))
        out[name] = w
        out["m_" + name] = s * _jax.random.normal(km, w.shape, _jnp.float32)
        out["v_" + name] = (s * s) * _jax.random.uniform(kv, w.shape, _jnp.float32, 0.5, 1.5)
    if N_MICROBATCH > 1:
        for name, axis in PER_EXAMPLE_BATCH_AXIS.items():
            out[name] = _to_microbatches(out[name], axis)
    return {'x': out['x'], 'p': out['p'], 'norm_g': out['norm_g'], 'w_in': out['w_in'], 'ssm_a_re': out['ssm_a_re'], 'ssm_a_im': out['ssm_a_im'], 'ssm_b_re': out['ssm_b_re'], 'ssm_b_im': out['ssm_b_im'], 'ssm_c_re': out['ssm_c_re'], 'ssm_c_im': out['ssm_c_im'], 'ssm_d': out['ssm_d'], 'ssm_log_step': out['ssm_log_step'], 'ssm_w_glu': out['ssm_w_glu'], 'ssm_b_glu': out['ssm_b_glu'], 'dn_conv_w': out['dn_conv_w'], 'dn_a_log': out['dn_a_log'], 'dn_dt_bias': out['dn_dt_bias'], 'dn_norm_g': out['dn_norm_g'], 'sg_ln_g': out['sg_ln_g'], 'sg_ln_b': out['sg_ln_b'], 'sg_w': out['sg_w'], 'sg_b': out['sg_b'], 'w_out': out['w_out'], 'ple_norm_g': out['ple_norm_g'], 'w_ple_gate': out['w_ple_gate'], 'w_ple': out['w_ple'], 'final_norm_g': out['final_norm_g'], 'loss_target': out['loss_target'], 'm_norm_g': out['m_norm_g'], 'm_w_in': out['m_w_in'], 'm_ssm_a_re': out['m_ssm_a_re'], 'm_ssm_a_im': out['m_ssm_a_im'], 'm_ssm_b_re': out['m_ssm_b_re'], 'm_ssm_b_im': out['m_ssm_b_im'], 'm_ssm_c_re': out['m_ssm_c_re'], 'm_ssm_c_im': out['m_ssm_c_im'], 'm_ssm_d': out['m_ssm_d'], 'm_ssm_log_step': out['m_ssm_log_step'], 'm_ssm_w_glu': out['m_ssm_w_glu'], 'm_ssm_b_glu': out['m_ssm_b_glu'], 'm_dn_conv_w': out['m_dn_conv_w'], 'm_dn_a_log': out['m_dn_a_log'], 'm_dn_dt_bias': out['m_dn_dt_bias'], 'm_dn_norm_g': out['m_dn_norm_g'], 'm_sg_ln_g': out['m_sg_ln_g'], 'm_sg_ln_b': out['m_sg_ln_b'], 'm_sg_w': out['m_sg_w'], 'm_sg_b': out['m_sg_b'], 'm_w_out': out['m_w_out'], 'm_ple_norm_g': out['m_ple_norm_g'], 'm_w_ple_gate': out['m_w_ple_gate'], 'm_w_ple': out['m_w_ple'], 'm_final_norm_g': out['m_final_norm_g'], 'v_norm_g': out['v_norm_g'], 'v_w_in': out['v_w_in'], 'v_ssm_a_re': out['v_ssm_a_re'], 'v_ssm_a_im': out['v_ssm_a_im'], 'v_ssm_b_re': out['v_ssm_b_re'], 'v_ssm_b_im': out['v_ssm_b_im'], 'v_ssm_c_re': out['v_ssm_c_re'], 'v_ssm_c_im': out['v_ssm_c_im'], 'v_ssm_d': out['v_ssm_d'], 'v_ssm_log_step': out['v_ssm_log_step'], 'v_ssm_w_glu': out['v_ssm_w_glu'], 'v_ssm_b_glu': out['v_ssm_b_glu'], 'v_dn_conv_w': out['v_dn_conv_w'], 'v_dn_a_log': out['v_dn_a_log'], 'v_dn_dt_bias': out['v_dn_dt_bias'], 'v_dn_norm_g': out['v_dn_norm_g'], 'v_sg_ln_g': out['v_sg_ln_g'], 'v_sg_ln_b': out['v_sg_ln_b'], 'v_sg_w': out['v_sg_w'], 'v_sg_b': out['v_sg_b'], 'v_w_out': out['v_w_out'], 'v_ple_norm_g': out['v_ple_norm_g'], 'v_w_ple_gate': out['v_w_ple_gate'], 'v_w_ple': out['v_w_ple'], 'v_final_norm_g': out['v_final_norm_g']}


def _loss(weights, diff, rest, loss_target):
    with _jax.named_scope("forward"):
        args = {**rest, TWIN_DIFF_INPUT: diff, **{k: w.astype(_WEIGHT_DTYPES[k]) for k, w in weights.items()}}
        y = _forward(args)
    with _jax.named_scope("loss_head"):
        err = _jnp.square(y.astype(_jnp.float32) - loss_target)
        return 0.5 * _jnp.sum(_jnp.mean(err, axis=-1)) if err.ndim else 0.5 * err


def _adamw(w, g, m, v):
    m = ADAM_B1 * m + (1.0 - ADAM_B1) * g
    v = ADAM_B2 * v + (1.0 - ADAM_B2) * _jnp.square(g)
    m_hat = m / (1.0 - ADAM_B1 ** ADAM_STEP)
    v_hat = v / (1.0 - ADAM_B2 ** ADAM_STEP)
    delta = -ADAM_LR * (m_hat / (_jnp.sqrt(v_hat) + ADAM_EPS) + ADAM_WD * w)
    return delta, m, v


def reference(x, p, norm_g, w_in, ssm_a_re, ssm_a_im, ssm_b_re, ssm_b_im, ssm_c_re, ssm_c_im, ssm_d, ssm_log_step, ssm_w_glu, ssm_b_glu, dn_conv_w, dn_a_log, dn_dt_bias, dn_norm_g, sg_ln_g, sg_ln_b, sg_w, sg_b, w_out, ple_norm_g, w_ple_gate, w_ple, final_norm_g, loss_target, m_norm_g, m_w_in, m_ssm_a_re, m_ssm_a_im, m_ssm_b_re, m_ssm_b_im, m_ssm_c_re, m_ssm_c_im, m_ssm_d, m_ssm_log_step, m_ssm_w_glu, m_ssm_b_glu, m_dn_conv_w, m_dn_a_log, m_dn_dt_bias, m_dn_norm_g, m_sg_ln_g, m_sg_ln_b, m_sg_w, m_sg_b, m_w_out, m_ple_norm_g, m_w_ple_gate, m_w_ple, m_final_norm_g, v_norm_g, v_w_in, v_ssm_a_re, v_ssm_a_im, v_ssm_b_re, v_ssm_b_im, v_ssm_c_re, v_ssm_c_im, v_ssm_d, v_ssm_log_step, v_ssm_w_glu, v_ssm_b_glu, v_dn_conv_w, v_dn_a_log, v_dn_dt_bias, v_dn_norm_g, v_sg_ln_g, v_sg_ln_b, v_sg_w, v_sg_b, v_w_out, v_ple_norm_g, v_w_ple_gate, v_w_ple, v_final_norm_g):
    given = dict(x=x, p=p, norm_g=norm_g, w_in=w_in, ssm_a_re=ssm_a_re, ssm_a_im=ssm_a_im, ssm_b_re=ssm_b_re, ssm_b_im=ssm_b_im, ssm_c_re=ssm_c_re, ssm_c_im=ssm_c_im, ssm_d=ssm_d, ssm_log_step=ssm_log_step, ssm_w_glu=ssm_w_glu, ssm_b_glu=ssm_b_glu, dn_conv_w=dn_conv_w, dn_a_log=dn_a_log, dn_dt_bias=dn_dt_bias, dn_norm_g=dn_norm_g, sg_ln_g=sg_ln_g, sg_ln_b=sg_ln_b, sg_w=sg_w, sg_b=sg_b, w_out=w_out, ple_norm_g=ple_norm_g, w_ple_gate=w_ple_gate, w_ple=w_ple, final_norm_g=final_norm_g, loss_target=loss_target, m_norm_g=m_norm_g, m_w_in=m_w_in, m_ssm_a_re=m_ssm_a_re, m_ssm_a_im=m_ssm_a_im, m_ssm_b_re=m_ssm_b_re, m_ssm_b_im=m_ssm_b_im, m_ssm_c_re=m_ssm_c_re, m_ssm_c_im=m_ssm_c_im, m_ssm_d=m_ssm_d, m_ssm_log_step=m_ssm_log_step, m_ssm_w_glu=m_ssm_w_glu, m_ssm_b_glu=m_ssm_b_glu, m_dn_conv_w=m_dn_conv_w, m_dn_a_log=m_dn_a_log, m_dn_dt_bias=m_dn_dt_bias, m_dn_norm_g=m_dn_norm_g, m_sg_ln_g=m_sg_ln_g, m_sg_ln_b=m_sg_ln_b, m_sg_w=m_sg_w, m_sg_b=m_sg_b, m_w_out=m_w_out, m_ple_norm_g=m_ple_norm_g, m_w_ple_gate=m_w_ple_gate, m_w_ple=m_w_ple, m_final_norm_g=m_final_norm_g, v_norm_g=v_norm_g, v_w_in=v_w_in, v_ssm_a_re=v_ssm_a_re, v_ssm_a_im=v_ssm_a_im, v_ssm_b_re=v_ssm_b_re, v_ssm_b_im=v_ssm_b_im, v_ssm_c_re=v_ssm_c_re, v_ssm_c_im=v_ssm_c_im, v_ssm_d=v_ssm_d, v_ssm_log_step=v_ssm_log_step, v_ssm_w_glu=v_ssm_w_glu, v_ssm_b_glu=v_ssm_b_glu, v_dn_conv_w=v_dn_conv_w, v_dn_a_log=v_dn_a_log, v_dn_dt_bias=v_dn_dt_bias, v_dn_norm_g=v_dn_norm_g, v_sg_ln_g=v_sg_ln_g, v_sg_ln_b=v_sg_ln_b, v_sg_w=v_sg_w, v_sg_b=v_sg_b, v_w_out=v_w_out, v_ple_norm_g=v_ple_norm_g, v_w_ple_gate=v_w_ple_gate, v_w_ple=v_w_ple, v_final_norm_g=v_final_norm_g)
    weights = {n: given[n] for n in TWIN_WEIGHTS}
    shared = {n: given[n] for n in SHARED_INPUTS}
    per_example = {n: given[n] for n in ['x', 'p']}
    grad_fn = _jax.value_and_grad(_loss, argnums=(0, 1))

    def one_microbatch(ex, loss_target):
        ex = dict(ex)
        diff = ex.pop(TWIN_DIFF_INPUT)
        return grad_fn(weights, diff, {**shared, **ex}, loss_target)

    if N_MICROBATCH == 1:
        loss, (grad_w, grad_x) = one_microbatch(per_example, given["loss_target"])
    else:
        def body(carry, xs):
            loss_sum, grad_sum = carry
            l_k, (gw_k, gx_k) = one_microbatch(xs[0], xs[1])
            with _jax.named_scope("update"):
                return (loss_sum + l_k, _jax.tree.map(_jnp.add, grad_sum, gw_k)), gx_k

        init = (_jnp.zeros((), _jnp.float32), _jax.tree.map(_jnp.zeros_like, weights))
        (loss, grad_w), grad_x = _jax.lax.scan(body, init, (per_example, given["loss_target"]))
    with _jax.named_scope("update"):
        delta_w, new_m, new_v = {}, {}, {}
        for n in TWIN_WEIGHTS:
            delta_w[n], new_m[n], new_v[n] = _adamw(weights[n], grad_w[n], given["m_" + n], given["v_" + n])
    return (loss, grad_x, *[grad_w[n] for n in TWIN_WEIGHTS], *[delta_w[n] for n in TWIN_WEIGHTS],
            *[new_m[n] for n in TWIN_WEIGHTS], *[new_v[n] for n in TWIN_WEIGHTS])
```

```python
import functools

import jax
import jax.numpy as jnp
from jax import lax
from jax.experimental import pallas as pl
from jax.experimental.pallas import tpu as pltpu

f32 = jnp.float32
bf16 = jnp.bfloat16

_MXU = bf16
_COMM = bf16
HIGHEST = lax.Precision.HIGHEST

D_MODEL = 1024
DEPTH = 2
D_PLE = 256
D_SSM = 256
D_DN = 512
D_SG = 256
SSM_GROUPS = 16
SSM_GROUP = 16
SSM_STATE = 64
N_STATE = SSM_GROUPS * SSM_STATE
DN_HEADS = 4
DN_HEAD_DIM = 128
DN_CONV = 4
DN_CHUNK = 64
SG_HEADS = 4
SG_HEAD_DIM = 64
SG_CHUNK = 128
S5_CHUNK = 128
EPS = 1e-6
D_IN = 3336
D_IN_PAD = 3456
LANE = 128

ADAM_LR = 0.001
ADAM_B1 = 0.9
ADAM_B2 = 0.999
ADAM_EPS = 1e-08
ADAM_WD = 0.01
ADAM_STEP = 10

N_CHIPS = 4
N_DEV = 8
MESH_AXES = ("x", "y", "c")

Z_COLS = ((0, 512), (512, 2048), (2048, 2560), (2560, 3328), (3328, 3456))

SHARDED = (("w_in", 2), ("ssm_w_glu", 1), ("dn_conv_w", 2), ("w_out", 1), ("w_ple_gate", 1), ("w_ple", 2))
REPLICATED = ("norm_g", "ssm_a_re", "ssm_a_im", "ssm_b_re", "ssm_b_im", "ssm_c_re", "ssm_c_im", "ssm_d",
              "ssm_log_step", "ssm_b_glu", "dn_a_log", "dn_dt_bias", "dn_norm_g", "sg_ln_g", "sg_ln_b", "sg_w",
              "sg_b", "ple_norm_g", "final_norm_g")
WEIGHTS = ("norm_g", "w_in", "ssm_a_re", "ssm_a_im", "ssm_b_re", "ssm_b_im", "ssm_c_re", "ssm_c_im", "ssm_d",
           "ssm_log_step", "ssm_w_glu", "ssm_b_glu", "dn_conv_w", "dn_a_log", "dn_dt_bias", "dn_norm_g", "sg_ln_g",
           "sg_ln_b", "sg_w", "sg_b", "w_out", "ple_norm_g", "w_ple_gate", "w_ple", "final_norm_g")

VMEM_BIG = 56 * 1024 * 1024


def _mm(a, b):
    return jnp.dot(a.astype(_MXU), b.astype(_MXU), preferred_element_type=f32)


def _mm_nt(a, b):
    return lax.dot_general(a.astype(_MXU), b.astype(_MXU), (((1,), (1,)), ((), ())), preferred_element_type=f32)


def _mm_tn(a, b):
    return lax.dot_general(a.astype(_MXU), b.astype(_MXU), (((0,), (0,)), ((), ())), preferred_element_type=f32)


@jax.custom_vjp
def bdot(a, b):
    return _mm(a, b)


def _bdot_fwd(a, b):
    return _mm(a, b), (a, b)


def _bdot_bwd(res, g):
    a, b = res
    return _mm_nt(g, b).astype(a.dtype), _mm_tn(a, g).astype(b.dtype)


bdot.defvjp(_bdot_fwd, _bdot_bwd)


@jax.custom_vjp
def bdot_nt(a, b):
    return _mm_nt(a, b)


def _bdot_nt_fwd(a, b):
    return _mm_nt(a, b), (a, b)


def _bdot_nt_bwd(res, g):
    a, b = res
    return _mm(g, b).astype(a.dtype), _mm_tn(g, a).astype(b.dtype)


bdot_nt.defvjp(_bdot_nt_fwd, _bdot_nt_bwd)


@jax.custom_vjp
def bdot_tn(a, b):
    return _mm_tn(a, b)


def _bdot_tn_fwd(a, b):
    return _mm_tn(a, b), (a, b)


def _bdot_tn_bwd(res, g):
    a, b = res
    return _mm_nt(b, g).astype(a.dtype), _mm(a, g).astype(b.dtype)


bdot_tn.defvjp(_bdot_tn_fwd, _bdot_tn_bwd)


def hdot(a, b):
    return jnp.dot(a, b, precision=HIGHEST, preferred_element_type=f32)


@functools.partial(jax.custom_vjp, nondiff_argnums=(1,))
def roll_rows(x, k):
    return pltpu.roll(x, k, 0)


def _roll_rows_fwd(x, k):
    return pltpu.roll(x, k, 0), None


def _roll_rows_bwd(k, _, g):
    return (pltpu.roll(g, g.shape[0] - k, 0),)


roll_rows.defvjp(_roll_rows_fwd, _roll_rows_bwd)


def _row_ids(shape):
    return lax.broadcasted_iota(jnp.int32, shape, 0)


def _shift_rows(x, d):
    return jnp.where(_row_ids(x.shape) >= d, roll_rows(x, d), 0.0)


def _rowsel(x, i):
    return jnp.sum(jnp.where(_row_ids(x.shape) == i, x, 0.0), axis=0, keepdims=True)


def _rms(x, g):
    return x * lax.rsqrt(jnp.mean(x * x, axis=-1, keepdims=True) + EPS) * g


def _layer_norm(x, g, b):
    mu = jnp.mean(x, axis=-1, keepdims=True)
    xc = x - mu
    return xc * lax.rsqrt(jnp.mean(xc * xc, axis=-1, keepdims=True) + EPS) * g + b


def _s5_prep(are, aim, ls, bre, bim):
    step = jnp.exp(ls)
    mag = jnp.exp(are * step)
    lr = mag * jnp.cos(aim * step)
    li = mag * jnp.sin(aim * step)
    den = are * are + aim * aim
    nr, ni = lr - 1.0, li
    fr = (nr * are + ni * aim) / den
    fi = (ni * are - nr * aim) / den
    bbr = fr * bre - fi * bim
    bbi = fr * bim + fi * bre
    pr = jnp.broadcast_to(lr, (S5_CHUNK, N_STATE))
    pi = jnp.broadcast_to(li, (S5_CHUNK, N_STATE))
    rows8 = _row_ids((8, N_STATE))
    qr = jnp.zeros((8, N_STATE), f32)
    qi = jnp.zeros((8, N_STATE), f32)
    d, k = 1, 0
    while d < S5_CHUNK:
        qr = qr + jnp.where(rows8 == k, _rowsel(pr, d - 1), 0.0)
        qi = qi + jnp.where(rows8 == k, _rowsel(pi, d - 1), 0.0)
        keep = _row_ids(pr.shape) >= d
        sr, si = roll_rows(pr, d), roll_rows(pi, d)
        pr, pi = jnp.where(keep, pr * sr - pi * si, pr), jnp.where(keep, pr * si + pi * sr, pi)
        d, k = 2 * d, k + 1
    return pr, pi, qr, qi, bbr, bbi


def _s5_chunk(u, gate, hr, hi, pr, pi, qr, qi, bbr, bbi, cr, ci, dr, wglu, bglu):
    n = u.shape[0]
    xr = bdot(u, bbr)
    xi = bdot(u, bbi)
    d, k = 1, 0
    while d < n:
        lr, li = _rowsel(qr, k), _rowsel(qi, k)
        sr, si = _shift_rows(xr, d), _shift_rows(xi, d)
        xr, xi = xr + lr * sr - li * si, xi + lr * si + li * sr
        d, k = 2 * d, k + 1
    xr, xi = xr + pr * hr - pi * hi, xi + pr * hi + pi * hr
    y = bdot(xr, cr) - bdot(xi, ci) + dr * u
    y = jax.nn.gelu(y)
    y = y * jax.nn.sigmoid(bdot(y, wglu) + bglu)
    return y * jax.nn.silu(gate), _rowsel(xr, n - 1), _rowsel(xi, n - 1)


def _dn_pre(xc, xp, w0, w1, w2, w3, is_start, col):
    xp = jnp.where(is_start, 0.0, xp)
    rows = _row_ids(xc.shape)
    acc = w3 * xc
    for d, w in ((1, w2), (2, w1), (3, w0)):
        acc = acc + w * jnp.where(rows >= d, roll_rows(xc, d), roll_rows(xp, d))
    y = jax.nn.silu(acc)
    nrm = y * lax.rsqrt(jnp.sum(y * y, axis=-1, keepdims=True) + EPS)
    nrm = nrm * jnp.where(col < DN_HEADS, DN_HEAD_DIM ** -0.5, 1.0)
    return jnp.where(col < 2 * DN_HEADS, nrm, y)


def _dn_chunk(qs, ks, vs, ab, ggs, states, alog, dtb, ng):
    c = DN_CHUNK
    ri = lax.broadcasted_iota(jnp.int32, (c, c), 0)
    ci = lax.broadcasted_iota(jnp.int32, (c, c), 1)
    causal, strict = ri >= ci, ri > ci
    eye = (ri == ci).astype(f32)
    graw = -jnp.exp(alog) * jax.nn.softplus(ab + dtb)
    gcum = hdot(causal.astype(f32), graw)
    gcum_t = gcum.T
    sig = jax.nn.sigmoid(ab)
    outs, new_states = [], []
    for h in range(DN_HEADS):
        q, k, v, st = qs[h], ks[h], vs[h], states[h]
        gc = gcum[:, h:h + 1]
        gcr = gcum_t[h:h + 1, :]
        beta = sig[:, DN_HEADS + h:DN_HEADS + h + 1]
        decay = jnp.where(causal, jnp.exp(jnp.where(causal, gc - gcr, 0.0)), 0.0)
        kb = k * beta
        vb = v * beta
        neg_m = -jnp.where(strict, bdot_nt(kb, k) * decay, 0.0)
        inv = eye + neg_m
        pw = neg_m
        for _ in range(5):
            pw = hdot(pw, pw)
            inv = inv + hdot(inv, pw)
        egc = jnp.exp(gc)
        value = hdot(inv, vb)
        k_cd = hdot(inv, kb * egc)
        attn = bdot_nt(q, k) * decay
        gl = gc[c - 1:c, :]
        v_new = value - bdot(k_cd, st)
        o = bdot(q * egc, st) + bdot(attn, v_new)
        new_states.append(st * jnp.exp(gl) + bdot_tn(k * jnp.exp(gl - gc), v_new))
        outs.append(_rms(o, ng) * jax.nn.silu(ggs[h]))
    return outs, new_states


def _sg_chunk(u, v, gate, lng, lnb, ws, bt):
    n = SG_CHUNK
    ug = jax.nn.gelu(u)
    vn = _layer_norm(jax.nn.gelu(v), lng, lnb)
    causal = lax.broadcasted_iota(jnp.int32, (n, n), 0) >= lax.broadcasted_iota(jnp.int32, (n, n), 1)
    lane = lax.broadcasted_iota(jnp.int32, (n, D_SG), 1)
    s = jnp.zeros((n, D_SG), f32)
    for h in range(SG_HEADS):
        t = bdot(jnp.where(causal, ws[h], 0.0), vn) + bt[:, h:h + 1]
        s = s + jnp.where((lane >= h * SG_HEAD_DIM) & (lane < (h + 1) * SG_HEAD_DIM), t, 0.0)
    return ug * s * jax.nn.silu(gate)


def _cp(n_grid, vmem=None):
    return pltpu.CompilerParams(dimension_semantics=("arbitrary",) * n_grid, vmem_limit_bytes=vmem)


def _full(shape):
    nd = len(shape)
    return pl.BlockSpec(tuple(shape), lambda *_: (0,) * nd)


def _rows(tm, ncol):
    return pl.BlockSpec((tm, ncol), lambda i: (i, 0))


def _sds(shape, dtype=f32):
    return jax.ShapeDtypeStruct(tuple(shape), dtype)


def _acc(ref, val, first):
    @pl.when(first)
    def _():
        ref[...] = val

    @pl.when(jnp.logical_not(first))
    def _():
        ref[...] += val


def in_fwd(x, g, w, name):
    t, tm = x.shape[0], 256

    def body(x_ref, g_ref, w_ref, h_ref, *z_refs):
        h = _rms(x_ref[...], g_ref[...]).astype(_MXU)
        h_ref[...] = h
        for z_ref, (a, b) in zip(z_refs, Z_COLS):
            z_ref[...] = jnp.dot(h, w_ref[:, a:b], preferred_element_type=f32)

    widths = [b - a for a, b in Z_COLS]
    return pl.pallas_call(
        body, name=name, grid=(t // tm,),
        in_specs=[_rows(tm, D_MODEL), _full((1, D_MODEL)), _full((D_MODEL, D_IN_PAD))],
        out_specs=[_rows(tm, D_MODEL)] + [_rows(tm, n) for n in widths],
        out_shape=[_sds((t, D_MODEL), _MXU)] + [_sds((t, n)) for n in widths],
        compiler_params=_cp(1, VMEM_BIG),
    )(x, g, w)


def in_bwd(x, g, w, dzs, dres, name):
    t, tm = x.shape[0], 256
    widths = [b - a for a, b in Z_COLS]

    def body(x_ref, g_ref, w_ref, dres_ref, *rest):
        dz_refs, (dx_ref, dg_ref) = rest[:5], rest[5:]
        dh = jnp.zeros((tm, D_MODEL), f32)
        for dz_ref, (a, b) in zip(dz_refs, Z_COLS):
            dh = dh + _mm_nt(dz_ref[...], w_ref[:, a:b])
        _, vj = jax.vjp(_rms, x_ref[...], g_ref[...])
        dx, dg = vj(dh)
        dx_ref[...] = dres_ref[...] + dx
        _acc(dg_ref, dg, pl.program_id(0) == 0)

    return pl.pallas_call(
        body, name=name, grid=(t // tm,),
        in_specs=[_rows(tm, D_MODEL), _full((1, D_MODEL)), _full((D_MODEL, D_IN_PAD)), _rows(tm, D_MODEL)]
        + [_rows(tm, n) for n in widths],
        out_specs=[_rows(tm, D_MODEL), _full((1, D_MODEL))],
        out_shape=[_sds((t, D_MODEL)), _sds((1, D_MODEL))],
        compiler_params=_cp(1, VMEM_BIG),
    )(x, g, w, dres, *dzs)


def wgrad(a, g, name):
    t, k = a.shape
    n = g.shape[1]
    tm = 512
    tn = n if n <= 768 else (768 if n % 768 == 0 else 512)

    def body(a_ref, g_ref, o_ref):
        _acc(o_ref, _mm_tn(a_ref[...], g_ref[...]), pl.program_id(1) == 0)

    return pl.pallas_call(
        body, name=name, grid=(n // tn, t // tm),
        in_specs=[pl.BlockSpec((tm, k), lambda j, i: (i, 0)), pl.BlockSpec((tm, tn), lambda j, i: (i, j))],
        out_specs=pl.BlockSpec((k, tn), lambda j, i: (0, j)),
        out_shape=_sds((k, n)),
        compiler_params=_cp(2, VMEM_BIG),
    )(a, g)


def post_fwd(x, ys, yd, yg, p, wout, pg, wgate, wple, name):
    t, tm = x.shape[0], 256

    def body(x_ref, ys_ref, yd_ref, yg_ref, p_ref, wout_ref, pg_ref, wgate_ref, wple_ref,
             x2_ref, x1_ref, y_ref, hn_ref):
        y = jnp.concatenate([ys_ref[...], yd_ref[...], yg_ref[...]], axis=1).astype(_MXU)
        y_ref[...] = y
        x1 = x_ref[...] + jnp.dot(y, wout_ref[...], preferred_element_type=f32)
        x1_ref[...] = x1
        hn = _rms(x1, pg_ref[...]).astype(_MXU)
        hn_ref[...] = hn
        gp = jnp.dot(hn, wgate_ref[...], preferred_element_type=f32)
        pp = _mm(p_ref[...], wple_ref[...])
        x2_ref[...] = x1 + jax.nn.sigmoid(gp) * pp

    return pl.pallas_call(
        body, name=name, grid=(t // tm,),
        in_specs=[_rows(tm, D_MODEL), _rows(tm, D_SSM), _rows(tm, D_DN), _rows(tm, D_SG), _rows(tm, D_PLE),
                  _full((D_MODEL, D_MODEL)), _full((1, D_MODEL)), _full((D_MODEL, D_MODEL)), _full((D_PLE, D_MODEL))],
        out_specs=[_rows(tm, D_MODEL)] * 4,
        out_shape=[_sds((t, D_MODEL)), _sds((t, D_MODEL)), _sds((t, D_MODEL), _MXU), _sds((t, D_MODEL), _MXU)],
        compiler_params=_cp(1, VMEM_BIG),
    )(x, ys, yd, yg, p, wout, pg, wgate, wple)


def post_bwd(dx2, x1, hn, p, wout, pg, wgate, wple, name):
    t, tm = dx2.shape[0], 256

    def body(dx2_ref, x1_ref, hn_ref, p_ref, wout_ref, pg_ref, wgate_ref, wple_ref,
             dx1_ref, dgp_ref, dpp_ref, dys_ref, dyd_ref, dyg_ref, dpg_ref):
        dx2 = dx2_ref[...]
        gp = jnp.dot(hn_ref[...], wgate_ref[...], preferred_element_type=f32)
        pp = _mm(p_ref[...], wple_ref[...])
        sg = jax.nn.sigmoid(gp)
        dpp_ref[...] = (dx2 * sg).astype(_MXU)
        dgp = (dx2 * pp * sg * (1.0 - sg)).astype(_MXU)
        dgp_ref[...] = dgp
        dhn = _mm_nt(dgp, wgate_ref[...])
        _, vj = jax.vjp(_rms, x1_ref[...], pg_ref[...])
        dx1n, dpg = vj(dhn)
        dx1 = dx2 + dx1n
        dx1_ref[...] = dx1
        dy = _mm_nt(dx1, wout_ref[...])
        dys_ref[...] = dy[:, :D_SSM]
        dyd_ref[...] = dy[:, D_SSM:D_SSM + D_DN]
        dyg_ref[...] = dy[:, D_SSM + D_DN:]
        _acc(dpg_ref, dpg, pl.program_id(0) == 0)

    return pl.pallas_call(
        body, name=name, grid=(t // tm,),
        in_specs=[_rows(tm, D_MODEL), _rows(tm, D_MODEL), _rows(tm, D_MODEL), _rows(tm, D_PLE),
                  _full((D_MODEL, D_MODEL)), _full((1, D_MODEL)), _full((D_MODEL, D_MODEL)), _full((D_PLE, D_MODEL))],
        out_specs=[_rows(tm, D_MODEL), _rows(tm, D_MODEL), _rows(tm, D_MODEL), _rows(tm, D_SSM), _rows(tm, D_DN),
                   _rows(tm, D_SG), _full((1, D_MODEL))],
        out_shape=[_sds((t, D_MODEL)), _sds((t, D_MODEL), _MXU), _sds((t, D_MODEL), _MXU), _sds((t, D_SSM)),
                   _sds((t, D_DN)), _sds((t, D_SG)), _sds((1, D_MODEL))],
        compiler_params=_cp(1, VMEM_BIG),
    )(dx2, x1, hn, p, wout, pg, wgate, wple)


def loss_fwd_bwd(x, fg, target, name):
    t, tm = x.shape[0], 512

    def body(x_ref, fg_ref, t_ref, loss_ref, dx_ref, dfg_ref):
        def f(xv, gv):
            err = _rms(xv, gv) - t_ref[...]
            return 0.5 * jnp.sum(jnp.mean(err * err, axis=-1))

        val, vj = jax.vjp(f, x_ref[...], fg_ref[...])
        dx, dfg = vj(jnp.ones((), f32))
        dx_ref[...] = dx
        first = pl.program_id(0) == 0
        _acc(dfg_ref, dfg, first)
        _acc(loss_ref, jnp.full((8, LANE), val, f32), first)

    return pl.pallas_call(
        body, name=name, grid=(t // tm,),
        in_specs=[_rows(tm, D_MODEL), _full((1, D_MODEL)), _rows(tm, D_MODEL)],
        out_specs=[_full((8, LANE)), _rows(tm, D_MODEL), _full((1, D_MODEL))],
        out_shape=[_sds((8, LANE)), _sds((t, D_MODEL)), _sds((1, D_MODEL))],
        compiler_params=_cp(1),
    )(x, fg, target)


def s5_prep_fwd(are, aim, ls, bre, bim, name):
    def body(are_ref, aim_ref, ls_ref, bre_ref, bim_ref, *outs):
        vals = _s5_prep(are_ref[...], aim_ref[...], ls_ref[...], bre_ref[...], bim_ref[...])
        for o, v in zip(outs, vals):
            o[...] = v

    shapes = [(S5_CHUNK, N_STATE)] * 2 + [(8, N_STATE)] * 2 + [(D_SSM, N_STATE)] * 2
    return pl.pallas_call(body, name=name, out_shape=[_sds(s) for s in shapes])(are, aim, ls, bre, bim)


def s5_prep_bwd(are, aim, ls, bre, bim, cts, name):
    def body(are_ref, aim_ref, ls_ref, bre_ref, bim_ref, *rest):
        ct_refs, outs = rest[:6], rest[6:]
        _, vj = jax.vjp(_s5_prep, are_ref[...], aim_ref[...], ls_ref[...], bre_ref[...], bim_ref[...])
        for o, v in zip(outs, vj(tuple(r[...] for r in ct_refs))):
            o[...] = v

    shapes = [(1, N_STATE)] * 3 + [(D_SSM, N_STATE)] * 2
    return pl.pallas_call(body, name=name, out_shape=[_sds(s) for s in shapes])(are, aim, ls, bre, bim, *cts)


_S5_PARAM_SHAPES = ((S5_CHUNK, N_STATE), (S5_CHUNK, N_STATE), (8, N_STATE), (8, N_STATE), (D_SSM, N_STATE),
                    (D_SSM, N_STATE), (N_STATE, D_SSM), (N_STATE, D_SSM), (1, D_SSM), (D_SSM, D_SSM), (1, D_SSM))


def s5_fwd(z, params, nb, name):
    t = z.shape[0]
    nc = t // nb // S5_CHUNK

    def body(z_ref, *rest):
        p_refs, (y_ref, hs_ref, hr_s, hi_s) = rest[:11], rest[11:]

        @pl.when(pl.program_id(1) == 0)
        def _():
            hr_s[...] = jnp.zeros_like(hr_s)
            hi_s[...] = jnp.zeros_like(hi_s)

        hr, hi = hr_s[...], hi_s[...]
        hs_ref[0, :, :N_STATE] = hr
        hs_ref[0, :, N_STATE:] = hi
        y, nhr, nhi = _s5_chunk(z_ref[:, :D_SSM], z_ref[:, D_SSM:], hr, hi, *[r[...] for r in p_refs])
        y_ref[...] = y
        hr_s[...] = nhr
        hi_s[...] = nhi

    return pl.pallas_call(
        body, name=name, grid=(nb, nc),
        in_specs=[pl.BlockSpec((S5_CHUNK, 2 * D_SSM), lambda b, c: (b * nc + c, 0))]
        + [_full(s) for s in _S5_PARAM_SHAPES],
        out_specs=[pl.BlockSpec((S5_CHUNK, D_SSM), lambda b, c: (b * nc + c, 0)),
                   pl.BlockSpec((1, 1, 2 * N_STATE), lambda b, c: (b * nc + c, 0, 0))],
        out_shape=[_sds((t, D_SSM)), _sds((nb * nc, 1, 2 * N_STATE))],
        scratch_shapes=[pltpu.VMEM((1, N_STATE), f32), pltpu.VMEM((1, N_STATE), f32)],
        compiler_params=_cp(2, VMEM_BIG),
    )(z, *params)


def s5_bwd(z, params, hs, dy, nb, name):
    t = z.shape[0]
    nc = t // nb // S5_CHUNK

    def body(z_ref, hs_ref, dy_ref, *rest):
        p_refs, dz_ref, dp_refs, (dhr_s, dhi_s) = rest[:11], rest[11], rest[12:23], rest[23:]

        @pl.when(pl.program_id(1) == 0)
        def _():
            dhr_s[...] = jnp.zeros_like(dhr_s)
            dhi_s[...] = jnp.zeros_like(dhi_s)

        prim = (z_ref[:, :D_SSM], z_ref[:, D_SSM:], hs_ref[0, :, :N_STATE], hs_ref[0, :, N_STATE:]) + tuple(
            r[...] for r in p_refs)
        _, vj = jax.vjp(_s5_chunk, *prim)
        cts = vj((dy_ref[...], dhr_s[...], dhi_s[...]))
        dz_ref[:, :D_SSM] = cts[0]
        dz_ref[:, D_SSM:] = cts[1]
        dhr_s[...] = cts[2]
        dhi_s[...] = cts[3]
        first = (pl.program_id(0) == 0) & (pl.program_id(1) == 0)
        for r, v in zip(dp_refs, cts[4:]):
            _acc(r, v, first)

    rev = lambda b, c: (b * nc + nc - 1 - c, 0)
    return pl.pallas_call(
        body, name=name, grid=(nb, nc),
        in_specs=[pl.BlockSpec((S5_CHUNK, 2 * D_SSM), rev),
                  pl.BlockSpec((1, 1, 2 * N_STATE), lambda b, c: (b * nc + nc - 1 - c, 0, 0)),
                  pl.BlockSpec((S5_CHUNK, D_SSM), rev)] + [_full(s) for s in _S5_PARAM_SHAPES],
        out_specs=[pl.BlockSpec((S5_CHUNK, 2 * D_SSM), rev)] + [_full(s) for s in _S5_PARAM_SHAPES],
        out_shape=[_sds((t, 2 * D_SSM))] + [_sds(s) for s in _S5_PARAM_SHAPES],
        scratch_shapes=[pltpu.VMEM((1, N_STATE), f32), pltpu.VMEM((1, N_STATE), f32)],
        compiler_params=_cp(2, VMEM_BIG),
    )(z, hs, dy, *params)


DN_PRE_ROWS = 512
DN_COLS = 3 * D_DN // LANE


def dn_pre_fwd(zq, convw, seq, name):
    t, tb = zq.shape[0], DN_PRE_ROWS
    per_seq = seq // tb

    def body(xc_ref, xp_ref, w_ref, o_ref):
        i, j = pl.program_id(1), pl.program_id(0)
        o_ref[...] = _dn_pre(xc_ref[...], xp_ref[...], w_ref[0:1, :], w_ref[1:2, :], w_ref[2:3, :], w_ref[3:4, :],
                             i % per_seq == 0, j)

    return pl.pallas_call(
        body, name=name, grid=(DN_COLS, t // tb),
        in_specs=[pl.BlockSpec((tb, LANE), lambda j, i: (i, j)),
                  pl.BlockSpec((tb, LANE), lambda j, i: (jnp.maximum(i - 1, 0), j)),
                  pl.BlockSpec((DN_CONV, LANE), lambda j, i: (0, j))],
        out_specs=pl.BlockSpec((tb, LANE), lambda j, i: (i, j)),
        out_shape=_sds((t, 3 * D_DN)),
        compiler_params=_cp(2),
    )(zq, zq, convw)


def dn_pre_bwd(zq, convw, dqkv, seq, name):
    t, tb = zq.shape[0], DN_PRE_ROWS
    nrow = t // tb
    per_seq = seq // tb

    def body(xc_ref, xp_ref, w_ref, d_ref, dx_ref, dw_ref, carry):
        j, step = pl.program_id(0), pl.program_id(1)
        i = nrow - 1 - step

        @pl.when(step == 0)
        def _():
            carry[...] = jnp.zeros_like(carry)

        fn = functools.partial(_dn_pre, is_start=i % per_seq == 0, col=j)
        _, vj = jax.vjp(fn, xc_ref[...], xp_ref[...], w_ref[0:1, :], w_ref[1:2, :], w_ref[2:3, :], w_ref[3:4, :])
        dxc, dxp, dw0, dw1, dw2, dw3 = vj(d_ref[...])
        dx_ref[...] = dxc + carry[...]
        carry[...] = dxp
        for k, dw in enumerate((dw0, dw1, dw2, dw3)):
            @pl.when(step == 0)
            def _():
                dw_ref[k:k + 1, :] = dw

            @pl.when(step != 0)
            def _():
                dw_ref[k:k + 1, :] += dw

    return pl.pallas_call(
        body, name=name, grid=(DN_COLS, nrow),
        in_specs=[pl.BlockSpec((tb, LANE), lambda j, s: (nrow - 1 - s, j)),
                  pl.BlockSpec((tb, LANE), lambda j, s: (jnp.maximum(nrow - 2 - s, 0), j)),
                  pl.BlockSpec((DN_CONV, LANE), lambda j, s: (0, j)),
                  pl.BlockSpec((tb, LANE), lambda j, s: (nrow - 1 - s, j))],
        out_specs=[pl.BlockSpec((tb, LANE), lambda j, s: (nrow - 1 - s, j)),
                   pl.BlockSpec((DN_CONV, LANE), lambda j, s: (0, j))],
        out_shape=[_sds((t, 3 * D_DN)), _sds((DN_CONV, 3 * D_DN))],
        scratch_shapes=[pltpu.VMEM((tb, LANE), f32)],
        compiler_params=_cp(2),
    )(zq, zq, convw, dqkv)


def _dn_heads(ref, base=0):
    return [ref[:, base + h * DN_HEAD_DIM:base + (h + 1) * DN_HEAD_DIM] for h in range(DN_HEADS)]


def dn_fwd(qkv, ab, gg, alog, dtb, ng, nb, name):
    t = qkv.shape[0]
    nc = t // nb // DN_CHUNK
    c = DN_CHUNK

    def body(qkv_ref, ab_ref, gg_ref, alog_ref, dtb_ref, ng_ref, y_ref, ss_ref, st):
        @pl.when(pl.program_id(1) == 0)
        def _():
            st[...] = jnp.zeros_like(st)

        states = [st[h] for h in range(DN_HEADS)]
        for h in range(DN_HEADS):
            ss_ref[0, h] = states[h]
        outs, new = _dn_chunk(_dn_heads(qkv_ref), _dn_heads(qkv_ref, D_DN), _dn_heads(qkv_ref, 2 * D_DN), ab_ref[...],
                              _dn_heads(gg_ref), states, alog_ref[...], dtb_ref[...], ng_ref[...])
        for h in range(DN_HEADS):
            y_ref[:, h * DN_HEAD_DIM:(h + 1) * DN_HEAD_DIM] = outs[h]
            st[h] = new[h]

    blk = lambda b, k: (b * nc + k, 0)
    return pl.pallas_call(
        body, name=name, grid=(nb, nc),
        in_specs=[pl.BlockSpec((c, 3 * D_DN), blk), pl.BlockSpec((c, LANE), blk), pl.BlockSpec((c, D_DN), blk),
                  _full((1, LANE)), _full((1, LANE)), _full((1, LANE))],
        out_specs=[pl.BlockSpec((c, D_DN), blk),
                   pl.BlockSpec((1, DN_HEADS, DN_HEAD_DIM, DN_HEAD_DIM), lambda b, k: (b * nc + k, 0, 0, 0))],
        out_shape=[_sds((t, D_DN)), _sds((nb * nc, DN_HEADS, DN_HEAD_DIM, DN_HEAD_DIM))],
        scratch_shapes=[pltpu.VMEM((DN_HEADS, DN_HEAD_DIM, DN_HEAD_DIM), f32)],
        compiler_params=_cp(2),
    )(qkv, ab, gg, alog, dtb, ng)


def dn_bwd(qkv, ab, gg, alog, dtb, ng, ss, dy, nb, name):
    t = qkv.shape[0]
    nc = t // nb // DN_CHUNK
    c = DN_CHUNK

    def body(qkv_ref, ab_ref, gg_ref, alog_ref, dtb_ref, ng_ref, ss_ref, dy_ref,
             dqkv_ref, dab_ref, dgg_ref, dalog_ref, ddtb_ref, dng_ref, dst):
        @pl.when(pl.program_id(1) == 0)
        def _():
            dst[...] = jnp.zeros_like(dst)

        prim = (_dn_heads(qkv_ref), _dn_heads(qkv_ref, D_DN), _dn_heads(qkv_ref, 2 * D_DN), ab_ref[...],
                _dn_heads(gg_ref), [ss_ref[0, h] for h in range(DN_HEADS)], alog_ref[...], dtb_ref[...], ng_ref[...])
        _, vj = jax.vjp(_dn_chunk, *prim)
        dq, dk, dv, dab, dgg, dstates, dalog, ddtb, dng = vj((_dn_heads(dy_ref), [dst[h] for h in range(DN_HEADS)]))
        for h in range(DN_HEADS):
            lo, hi = h * DN_HEAD_DIM, (h + 1) * DN_HEAD_DIM
            dqkv_ref[:, lo:hi] = dq[h]
            dqkv_ref[:, D_DN + lo:D_DN + hi] = dk[h]
            dqkv_ref[:, 2 * D_DN + lo:2 * D_DN + hi] = dv[h]
            dgg_ref[:, lo:hi] = dgg[h]
            dst[h] = dstates[h]
        dab_ref[...] = dab
        first = (pl.program_id(0) == 0) & (pl.program_id(1) == 0)
        _acc(dalog_ref, dalog, first)
        _acc(ddtb_ref, ddtb, first)
        _acc(dng_ref, dng, first)

    rev = lambda b, k: (b * nc + nc - 1 - k, 0)
    return pl.pallas_call(
        body, name=name, grid=(nb, nc),
        in_specs=[pl.BlockSpec((c, 3 * D_DN), rev), pl.BlockSpec((c, LANE), rev), pl.BlockSpec((c, D_DN), rev),
                  _full((1, LANE)), _full((1, LANE)), _full((1, LANE)),
                  pl.BlockSpec((1, DN_HEADS, DN_HEAD_DIM, DN_HEAD_DIM), lambda b, k: (b * nc + nc - 1 - k, 0, 0, 0)),
                  pl.BlockSpec((c, D_DN), rev)],
        out_specs=[pl.BlockSpec((c, 3 * D_DN), rev), pl.BlockSpec((c, LANE), rev), pl.BlockSpec((c, D_DN), rev),
                   _full((1, LANE)), _full((1, LANE)), _full((1, LANE))],
        out_shape=[_sds((t, 3 * D_DN)), _sds((t, LANE)), _sds((t, D_DN)), _sds((1, LANE)), _sds((1, LANE)),
                   _sds((1, LANE))],
        scratch_shapes=[pltpu.VMEM((DN_HEADS, DN_HEAD_DIM, DN_HEAD_DIM), f32)],
        compiler_params=_cp(2),
    )(qkv, ab, gg, alog, dtb, ng, ss, dy)


SG_ROWS = 512


def sg_fwd(z, lng, lnb, w, bt, name):
    t = z.shape[0]

    def body(z_ref, lng_ref, lnb_ref, w_ref, bt_ref, y_ref):
        ws = [w_ref[h] for h in range(SG_HEADS)]
        for k in range(SG_ROWS // SG_CHUNK):
            r = pl.ds(k * SG_CHUNK, SG_CHUNK)
            y_ref[r, :] = _sg_chunk(z_ref[r, :D_SG], z_ref[r, D_SG:2 * D_SG], z_ref[r, 2 * D_SG:], lng_ref[...],
                                    lnb_ref[...], ws, bt_ref[...])

    return pl.pallas_call(
        body, name=name, grid=(t // SG_ROWS,),
        in_specs=[_rows(SG_ROWS, 3 * D_SG), _full((1, D_SG)), _full((1, D_SG)),
                  _full((SG_HEADS, SG_CHUNK, SG_CHUNK)), _full((SG_CHUNK, LANE))],
        out_specs=_rows(SG_ROWS, D_SG),
        out_shape=_sds((t, D_SG)),
        compiler_params=_cp(1),
    )(z, lng, lnb, w, bt)


def sg_bwd(z, lng, lnb, w, bt, dy, name):
    t = z.shape[0]

    def body(z_ref, lng_ref, lnb_ref, w_ref, bt_ref, dy_ref, dz_ref, dlng_ref, dlnb_ref, dw_ref, dbt_ref):
        ws = [w_ref[h] for h in range(SG_HEADS)]
        tot = None
        for k in range(SG_ROWS // SG_CHUNK):
            r = pl.ds(k * SG_CHUNK, SG_CHUNK)
            _, vj = jax.vjp(_sg_chunk, z_ref[r, :D_SG], z_ref[r, D_SG:2 * D_SG], z_ref[r, 2 * D_SG:], lng_ref[...],
                            lnb_ref[...], ws, bt_ref[...])
            du, dv, dgate, dlng, dlnb, dws, dbt = vj(dy_ref[r, :])
            dz_ref[r, :D_SG] = du
            dz_ref[r, D_SG:2 * D_SG] = dv
            dz_ref[r, 2 * D_SG:] = dgate
            part = [dlng, dlnb, dbt] + list(dws)
            tot = part if tot is None else [a + b for a, b in zip(tot, part)]
        first = pl.program_id(0) == 0
        _acc(dlng_ref, tot[0], first)
        _acc(dlnb_ref, tot[1], first)
        _acc(dbt_ref, tot[2], first)
        for h in range(SG_HEADS):
            @pl.when(first)
            def _():
                dw_ref[h] = tot[3 + h]

            @pl.when(jnp.logical_not(first))
            def _():
                dw_ref[h] += tot[3 + h]

    return pl.pallas_call(
        body, name=name, grid=(t // SG_ROWS,),
        in_specs=[_rows(SG_ROWS, 3 * D_SG), _full((1, D_SG)), _full((1, D_SG)),
                  _full((SG_HEADS, SG_CHUNK, SG_CHUNK)), _full((SG_CHUNK, LANE)), _rows(SG_ROWS, D_SG)],
        out_specs=[_rows(SG_ROWS, 3 * D_SG), _full((1, D_SG)), _full((1, D_SG)),
                   _full((SG_HEADS, SG_CHUNK, SG_CHUNK)), _full((SG_CHUNK, LANE))],
        out_shape=[_sds((t, 3 * D_SG)), _sds((1, D_SG)), _sds((1, D_SG)), _sds((SG_HEADS, SG_CHUNK, SG_CHUNK)),
                   _sds((SG_CHUNK, LANE))],
        compiler_params=_cp(1),
    )(z, lng, lnb, w, bt, dy)


def sum_adamw(recv, w, m, v, name):
    r = w.shape[0]
    tr = 256

    def body(recv_ref, w_ref, m_ref, v_ref, g_ref, d_ref, nm_ref, nv_ref):
        g = recv_ref[0].astype(f32)
        for k in range(1, N_DEV):
            g = g + recv_ref[k].astype(f32)
        wv = w_ref[...]
        nm = ADAM_B1 * m_ref[...] + (1.0 - ADAM_B1) * g
        nv = ADAM_B2 * v_ref[...] + (1.0 - ADAM_B2) * jnp.square(g)
        m_hat = nm / (1.0 - ADAM_B1 ** ADAM_STEP)
        v_hat = nv / (1.0 - ADAM_B2 ** ADAM_STEP)
        g_ref[...] = g
        d_ref[...] = -ADAM_LR * (m_hat / (jnp.sqrt(v_hat) + ADAM_EPS) + ADAM_WD * wv)
        nm_ref[...] = nm
        nv_ref[...] = nv

    return pl.pallas_call(
        body, name=name, grid=(r // tr,),
        in_specs=[pl.BlockSpec((N_DEV, tr, LANE), lambda i: (0, i, 0))] + [_rows(tr, LANE)] * 3,
        out_specs=[_rows(tr, LANE)] * 4,
        out_shape=[_sds((r, LANE))] * 4,
        compiler_params=_cp(1),
    )(recv, w, m, v)


_ANY = pl.BlockSpec(memory_space=pl.ANY)
_MESH = pl.DeviceIdType.MESH


def _flip(v, bit):
    return 1 - v if bit else v


def gather_weights(wa, wb, name):
    rels = ((1, 0), (0, 1), (1, 1))

    def body(a_ref, b_ref, oa_ref, ob_ref, send_sems, recv_sems, local_sems):
        x, y, c = lax.axis_index("x"), lax.axis_index("y"), lax.axis_index("c")
        mine = 2 * x + y
        local = [pltpu.make_async_copy(a_ref, oa_ref.at[mine], local_sems.at[0]),
                 pltpu.make_async_copy(b_ref, ob_ref.at[mine], local_sems.at[1])]
        for cp in local:
            cp.start()
        sends, recvs = [], []
        for r, (fx, fy) in enumerate(rels):
            px, py = _flip(x, fx), _flip(y, fy)
            for n, (src, dst) in enumerate(((a_ref, oa_ref), (b_ref, ob_ref))):
                k = 2 * r + n
                sends.append(pltpu.make_async_remote_copy(
                    src_ref=src, dst_ref=dst.at[mine], send_sem=send_sems.at[k], recv_sem=recv_sems.at[k],
                    device_id=(px, py, c), device_id_type=_MESH))
                recvs.append(pltpu.make_async_remote_copy(
                    src_ref=src, dst_ref=dst.at[2 * px + py], send_sem=send_sems.at[k], recv_sem=recv_sems.at[k],
                    device_id=(px, py, c), device_id_type=_MESH))
        for cp in sends:
            cp.start()
        for cp in recvs:
            cp.wait_recv()
        for cp in sends:
            cp.wait_send()
        for cp in local:
            cp.wait()

    return pl.pallas_call(
        body, name=name, in_specs=[_ANY, _ANY], out_specs=[_ANY, _ANY],
        out_shape=[_sds((N_CHIPS,) + wa.shape, wa.dtype), _sds((N_CHIPS,) + wb.shape, wb.dtype)],
        scratch_shapes=[pltpu.SemaphoreType.DMA((6,)), pltpu.SemaphoreType.DMA((6,)), pltpu.SemaphoreType.DMA((2,))],
    )(wa, wb)


def exchange_grads(ga, gb, name):
    rels = [(fx, fy, fc) for fx in (0, 1) for fy in (0, 1) for fc in (0, 1)][1:]

    def body(a_ref, b_ref, oa_ref, ob_ref, send_sems, recv_sems, local_sems):
        x, y, c = lax.axis_index("x"), lax.axis_index("y"), lax.axis_index("c")
        me = 4 * x + 2 * y + c
        local = [pltpu.make_async_copy(a_ref.at[2 * x + y], oa_ref.at[me], local_sems.at[0]),
                 pltpu.make_async_copy(b_ref, ob_ref.at[me], local_sems.at[1])]
        for cp in local:
            cp.start()
        sends, recvs = [], []
        for r, (fx, fy, fc) in enumerate(rels):
            px, py, pc = _flip(x, fx), _flip(y, fy), _flip(c, fc)
            peer = 4 * px + 2 * py + pc
            for n, (src, dst) in enumerate(((a_ref.at[2 * px + py], oa_ref), (b_ref, ob_ref))):
                k = 2 * r + n
                sends.append(pltpu.make_async_remote_copy(
                    src_ref=src, dst_ref=dst.at[me], send_sem=send_sems.at[k], recv_sem=recv_sems.at[k],
                    device_id=(px, py, pc), device_id_type=_MESH))
                recvs.append(pltpu.make_async_remote_copy(
                    src_ref=src, dst_ref=dst.at[peer], send_sem=send_sems.at[k], recv_sem=recv_sems.at[k],
                    device_id=(px, py, pc), device_id_type=_MESH))
        for cp in sends:
            cp.start()
        for cp in recvs:
            cp.wait_recv()
        for cp in sends:
            cp.wait_send()
        for cp in local:
            cp.wait()

    return pl.pallas_call(
        body, name=name, in_specs=[_ANY, _ANY], out_specs=[_ANY, _ANY],
        out_shape=[_sds((N_DEV,) + ga.shape[1:], ga.dtype), _sds((N_DEV,) + gb.shape, gb.dtype)],
        scratch_shapes=[pltpu.SemaphoreType.DMA((14,)), pltpu.SemaphoreType.DMA((14,)), pltpu.SemaphoreType.DMA((2,))],
    )(ga, gb)


def _pack_rows(parts, mult, dtype):
    flat = jnp.concatenate([a.reshape(-1) for a in parts]).astype(dtype)
    rows = -(-flat.shape[0] // (LANE * mult)) * mult
    return jnp.pad(flat, (0, rows * LANE - flat.shape[0])).reshape(rows, LANE)


def _unpack_rows(buf, shapes):
    flat = buf.reshape(-1)
    out, off = [], 0
    for s in shapes:
        n = 1
        for d in s:
            n *= d
        out.append(flat[off:off + n].reshape(s))
        off += n
    return out


def _permute_cols(w):
    pad = jnp.zeros(w.shape[:-1] + (D_IN_PAD - D_IN,), w.dtype)
    return jnp.concatenate([w[..., :2048], w[..., 2056:], w[..., 2048:2056], pad], axis=-1)


def _unpermute_cols(w):
    return jnp.concatenate([w[..., :2048], w[..., 3328:3336], w[..., 2048:3328]], axis=-1)


def _expand_b(b):
    eye = jnp.eye(SSM_GROUPS, dtype=b.dtype)
    return jnp.einsum("gnc,gh->gchn", b, eye).reshape(D_SSM, N_STATE)


def _extract_b(e):
    return jnp.einsum("gcgn->gnc", e.reshape(SSM_GROUPS, SSM_GROUP, SSM_GROUPS, SSM_STATE))


def _expand_c(c):
    eye = jnp.eye(SSM_GROUPS, dtype=c.dtype)
    return jnp.einsum("gcn,gh->gnhc", c, eye).reshape(N_STATE, D_SSM)


def _extract_c(e):
    return jnp.einsum("gngc->gcn", e.reshape(SSM_GROUPS, SSM_STATE, SSM_GROUPS, SSM_GROUP))


def _lane_row(v):
    return jnp.pad(v, (0, LANE - v.shape[0])).reshape(1, LANE)


def _layer_params(w, l):
    return dict(
        norm_g=w["norm_g"][l][None], win=w["w_in_perm"][l], wout=w["w_out"][l].astype(_MXU),
        pg=w["ple_norm_g"][l][None], wgate=w["w_ple_gate"][l].astype(_MXU), wple=w["w_ple"][l].astype(_MXU),
        are=w["ssm_a_re"][l].reshape(1, N_STATE), aim=w["ssm_a_im"][l].reshape(1, N_STATE),
        ls=jnp.repeat(w["ssm_log_step"][l], SSM_STATE).reshape(1, N_STATE),
        bre=_expand_b(w["ssm_b_re"][l]), bim=_expand_b(w["ssm_b_im"][l]),
        cr=_expand_c(w["ssm_c_re"][l]), ci=_expand_c(w["ssm_c_im"][l]),
        dr=w["ssm_d"][l].reshape(1, D_SSM), wglu=w["ssm_w_glu"][l].astype(f32), bglu=w["ssm_b_glu"][l][None],
        convw=w["dn_conv_w"][l], alog=_lane_row(w["dn_a_log"][l]), dtb=_lane_row(w["dn_dt_bias"][l]),
        ng=w["dn_norm_g"][l][None],
        lng=w["sg_ln_g"][l][None], lnb=w["sg_ln_b"][l][None], sgw=w["sg_w"][l],
        bt=jnp.pad(w["sg_b"][l].T, ((0, 0), (0, LANE - SG_HEADS))),
    )


def _layer_fwd(x, p, lp, nb, tag):
    seq = x.shape[0] // nb
    h, zs, zq, zg, zsg, zab = in_fwd(x, lp["norm_g"], lp["win"], f"in_fwd{tag}")
    prep = s5_prep_fwd(lp["are"], lp["aim"], lp["ls"], lp["bre"], lp["bim"], f"s5_prep_fwd{tag}")
    s5p = tuple(prep) + (lp["cr"], lp["ci"], lp["dr"], lp["wglu"], lp["bglu"])
    ys, hs = s5_fwd(zs, s5p, nb, f"s5_fwd{tag}")
    qkv = dn_pre_fwd(zq, lp["convw"], seq, f"dn_pre_fwd{tag}")
    yd, ss = dn_fwd(qkv, zab, zg, lp["alog"], lp["dtb"], lp["ng"], nb, f"dn_fwd{tag}")
    yg = sg_fwd(zsg, lp["lng"], lp["lnb"], lp["sgw"], lp["bt"], f"sg_fwd{tag}")
    x2, x1, y, hn = post_fwd(x, ys, yd, yg, p, lp["wout"], lp["pg"], lp["wgate"], lp["wple"], f"post_fwd{tag}")
    saved = dict(x=x, h=h, zs=zs, zq=zq, zg=zg, zsg=zsg, zab=zab, s5p=s5p, hs=hs, qkv=qkv, ss=ss, x1=x1, y=y, hn=hn, p=p)
    return x2, saved


def _layer_bwd(dx2, sv, lp, nb, tag):
    seq = dx2.shape[0] // nb
    dx1, dgp, dpp, dys, dyd, dyg, dpg = post_bwd(dx2, sv["x1"], sv["hn"], sv["p"], lp["wout"], lp["pg"], lp["wgate"],
                                                 lp["wple"], f"post_bwd{tag}")
    g = {}
    g["w_out"] = wgrad(sv["y"], dx1, f"wgrad_out{tag}")
    g["w_ple_gate"] = wgrad(sv["hn"], dgp, f"wgrad_gate{tag}")
    g["w_ple"] = wgrad(sv["p"], dpp, f"wgrad_ple{tag}")
    g["ple_norm_g"] = dpg[0]
    dzsg, dlng, dlnb, dsgw, dbt = sg_bwd(sv["zsg"], lp["lng"], lp["lnb"], lp["sgw"], lp["bt"], dyg, f"sg_bwd{tag}")
    g["sg_ln_g"], g["sg_ln_b"], g["sg_w"], g["sg_b"] = dlng[0], dlnb[0], dsgw, dbt[:, :SG_HEADS].T
    dqkv, dzab, dzg, dalog, ddtb, dng = dn_bwd(sv["qkv"], sv["zab"], sv["zg"], lp["alog"], lp["dtb"], lp["ng"], sv["ss"],
                                               dyd, nb, f"dn_bwd{tag}")
    dzq, dconv = dn_pre_bwd(sv["zq"], lp["convw"], dqkv, seq, f"dn_pre_bwd{tag}")
    g["dn_conv_w"], g["dn_a_log"], g["dn_dt_bias"], g["dn_norm_g"] = dconv, dalog[0, :DN_HEADS], ddtb[0, :DN_HEADS], dng[0]
    s5out = s5_bwd(sv["zs"], sv["s5p"], sv["hs"], dys, nb, f"s5_bwd{tag}")
    dzs, dprep, (dcr, dci, ddr, dwglu, dbglu) = s5out[0], s5out[1:7], s5out[7:]
    dare, daim, dls, dbre, dbim = s5_prep_bwd(lp["are"], lp["aim"], lp["ls"], lp["bre"], lp["bim"], dprep,
                                              f"s5_prep_bwd{tag}")
    g["ssm_a_re"] = dare.reshape(SSM_GROUPS, SSM_STATE)
    g["ssm_a_im"] = daim.reshape(SSM_GROUPS, SSM_STATE)
    g["ssm_log_step"] = dls.reshape(SSM_GROUPS, SSM_STATE).sum(axis=1)
    g["ssm_b_re"], g["ssm_b_im"] = _extract_b(dbre), _extract_b(dbim)
    g["ssm_c_re"], g["ssm_c_im"] = _extract_c(dcr), _extract_c(dci)
    g["ssm_d"] = ddr.reshape(SSM_GROUPS, SSM_GROUP)
    g["ssm_w_glu"], g["ssm_b_glu"] = dwglu, dbglu[0]
    dzs_all = (dzs, dzq, dzg, dzsg, dzab)
    dx, dng_in = in_bwd(sv["x"], lp["norm_g"], lp["win"], dzs_all, dx1, f"in_bwd{tag}")
    dwin = jnp.concatenate([wgrad(sv["h"], dz, f"wgrad_in{k}{tag}") for k, dz in enumerate(dzs_all)], axis=1)
    g["w_in"] = _unpermute_cols(dwin)
    g["norm_g"] = dng_in[0]
    return dx, g


def _local_step(x, p, target, w, nb):
    w = dict(w)
    w["w_in_perm"] = _permute_cols(w["w_in"]).astype(_MXU)
    lps = [_layer_params(w, l) for l in range(DEPTH)]
    saved = []
    for l in range(DEPTH):
        x, sv = _layer_fwd(x, p[l], lps[l], nb, f"_l{l}")
        saved.append(sv)
    loss_blk, dx, dfg = loss_fwd_bwd(x, w["final_norm_g"][None], target, "loss")
    grads = [None] * DEPTH
    for l in reversed(range(DEPTH)):
        dx, grads[l] = _layer_bwd(dx, saved[l], lps[l], nb, f"_l{l}")
    out = {k: jnp.stack([grads[l][k] for l in range(DEPTH)]) for k in grads[0]}
    out["final_norm_g"] = dfg[0]
    return loss_blk[0, 0], dx, out


def _split4(a, axis):
    n = a.shape[axis] // N_CHIPS
    shape = a.shape[:axis] + (N_CHIPS, n) + a.shape[axis + 1:]
    return jnp.moveaxis(a.reshape(shape), axis, 0)


def _join4(parts, axis):
    moved = jnp.moveaxis(parts, 0, axis)
    shape = moved.shape[:axis] + (moved.shape[axis] * moved.shape[axis + 1],) + moved.shape[axis + 2:]
    return moved.reshape(shape)


def kernel(x, p, norm_g, w_in, ssm_a_re, ssm_a_im, ssm_b_re, ssm_b_im, ssm_c_re, ssm_c_im, ssm_d, ssm_log_step, ssm_w_glu, ssm_b_glu, dn_conv_w, dn_a_log, dn_dt_bias, dn_norm_g, sg_ln_g, sg_ln_b, sg_w, sg_b, w_out, ple_norm_g, w_ple_gate, w_ple, final_norm_g, loss_target, m_norm_g, m_w_in, m_ssm_a_re, m_ssm_a_im, m_ssm_b_re, m_ssm_b_im, m_ssm_c_re, m_ssm_c_im, m_ssm_d, m_ssm_log_step, m_ssm_w_glu, m_ssm_b_glu, m_dn_conv_w, m_dn_a_log, m_dn_dt_bias, m_dn_norm_g, m_sg_ln_g, m_sg_ln_b, m_sg_w, m_sg_b, m_w_out, m_ple_norm_g, m_w_ple_gate, m_w_ple, m_final_norm_g, v_norm_g, v_w_in, v_ssm_a_re, v_ssm_a_im, v_ssm_b_re, v_ssm_b_im, v_ssm_c_re, v_ssm_c_im, v_ssm_d, v_ssm_log_step, v_ssm_w_glu, v_ssm_b_glu, v_dn_conv_w, v_dn_a_log, v_dn_dt_bias, v_dn_norm_g, v_sg_ln_g, v_sg_ln_b, v_sg_w, v_sg_b, v_w_out, v_ple_norm_g, v_w_ple_gate, v_w_ple, v_final_norm_g):
    args = locals()
    w = {n: args[n] for n in WEIGHTS}
    m = {n: args["m_" + n] for n in WEIGHTS}
    v = {n: args["v_" + n] for n in WEIGHTS}
    nb, seq = x.shape[0], x.shape[1]
    t = nb * seq

    sh_names = [n for n, _ in SHARDED]
    sh_axes = dict(SHARDED)
    sh_shapes = [w[n].shape for n in sh_names]
    wa = _pack_rows([w[n] for n in sh_names], 256, _COMM)
    wb = _pack_rows([w["dn_conv_w"]], 8, f32)
    ga, gb = gather_weights(wa, wb, "gather_weights")
    full = {n: w[n] for n in REPLICATED}
    per_chip = [_unpack_rows(ga[q], sh_shapes) for q in range(N_CHIPS)]
    for k, n in enumerate(sh_names):
        full[n] = _join4(jnp.stack([per_chip[q][k] for q in range(N_CHIPS)]), sh_axes[n])
    full["dn_conv_w"] = _join4(jnp.stack([_unpack_rows(gb[q], [w["dn_conv_w"].shape])[0] for q in range(N_CHIPS)]), 2)

    loss_local, dx, grads = _local_step(x.reshape(t, D_MODEL), p.reshape(DEPTH, t, D_PLE),
                                        loss_target.reshape(t, D_MODEL), full, nb)

    parts = [_split4(grads[n], sh_axes[n]) for n in sh_names]
    send_a = jnp.stack([_pack_rows([pt[q] for pt in parts], 256, _COMM) for q in range(N_CHIPS)])
    send_b = _pack_rows([grads[n] for n in REPLICATED], 256, f32)
    recv_a, recv_b = exchange_grads(send_a, send_b, "exchange_grads")

    def pack(d, names):
        return _pack_rows([d[n] for n in names], 256, f32)

    res_a = sum_adamw(recv_a, pack(w, sh_names), pack(m, sh_names), pack(v, sh_names), "adamw_sharded")
    res_b = sum_adamw(recv_b, pack(w, REPLICATED), pack(m, REPLICATED), pack(v, REPLICATED), "adamw_replicated")
    rep_shapes = [w[n].shape for n in REPLICATED]
    outs = []
    for ra, rb in zip(res_a, res_b):
        d = dict(zip(sh_names, _unpack_rows(ra, sh_shapes)))
        d.update(zip(REPLICATED, _unpack_rows(rb, rep_shapes)))
        outs.append(d)

    loss = lax.psum(loss_local, MESH_AXES)
    return (loss, dx.reshape(nb, seq, D_MODEL), *[outs[0][n] for n in WEIGHTS], *[outs[1][n] for n in WEIGHTS],
            *[outs[2][n] for n in WEIGHTS], *[outs[3][n] for n in WEIGHTS])
```

```python
import functools

import jax
import jax.numpy as jnp
from jax import lax
from jax.experimental import pallas as pl
from jax.experimental.pallas import tpu as pltpu

f32 = jnp.float32
bf16 = jnp.bfloat16

_MXU = bf16
_COMM = bf16
HIGHEST = lax.Precision.HIGHEST

D_MODEL = 1024
DEPTH = 2
D_PLE = 256
D_SSM = 256
D_DN = 512
D_SG = 256
SSM_GROUPS = 16
SSM_GROUP = 16
SSM_STATE = 64
N_STATE = SSM_GROUPS * SSM_STATE
DN_HEADS = 4
DN_HEAD_DIM = 128
DN_CONV = 4
DN_CHUNK = 64
SG_HEADS = 4
SG_HEAD_DIM = 64
SG_CHUNK = 128
S5_CHUNK = 128
EPS = 1e-6
D_IN = 3336
D_IN_PAD = 3456
LANE = 128

ADAM_LR = 0.001
ADAM_B1 = 0.9
ADAM_B2 = 0.999
ADAM_EPS = 1e-08
ADAM_WD = 0.01
ADAM_STEP = 10

N_CHIPS = 4
N_DEV = 8
MESH_AXES = ("x", "y", "c")

Z_COLS = ((0, 512), (512, 2048), (2048, 2560), (2560, 3328), (3328, 3456))

SHARDED = (("w_in", 2), ("ssm_w_glu", 1), ("dn_conv_w", 2), ("w_out", 1), ("w_ple_gate", 1), ("w_ple", 2))
REPLICATED = ("norm_g", "ssm_a_re", "ssm_a_im", "ssm_b_re", "ssm_b_im", "ssm_c_re", "ssm_c_im", "ssm_d",
              "ssm_log_step", "ssm_b_glu", "dn_a_log", "dn_dt_bias", "dn_norm_g", "sg_ln_g", "sg_ln_b", "sg_w",
              "sg_b", "ple_norm_g", "final_norm_g")
WEIGHTS = ("norm_g", "w_in", "ssm_a_re", "ssm_a_im", "ssm_b_re", "ssm_b_im", "ssm_c_re", "ssm_c_im", "ssm_d",
           "ssm_log_step", "ssm_w_glu", "ssm_b_glu", "dn_conv_w", "dn_a_log", "dn_dt_bias", "dn_norm_g", "sg_ln_g",
           "sg_ln_b", "sg_w", "sg_b", "w_out", "ple_norm_g", "w_ple_gate", "w_ple", "final_norm_g")

VMEM_BIG = 56 * 1024 * 1024


def _mm(a, b):
    return jnp.dot(a.astype(_MXU), b.astype(_MXU), preferred_element_type=f32)


def _mm_nt(a, b):
    return lax.dot_general(a.astype(_MXU), b.astype(_MXU), (((1,), (1,)), ((), ())), preferred_element_type=f32)


def _mm_tn(a, b):
    return lax.dot_general(a.astype(_MXU), b.astype(_MXU), (((0,), (0,)), ((), ())), preferred_element_type=f32)


@jax.custom_vjp
def bdot(a, b):
    return _mm(a, b)


def _bdot_fwd(a, b):
    return _mm(a, b), (a, b)


def _bdot_bwd(res, g):
    a, b = res
    return _mm_nt(g, b).astype(a.dtype), _mm_tn(a, g).astype(b.dtype)


bdot.defvjp(_bdot_fwd, _bdot_bwd)


@jax.custom_vjp
def bdot_nt(a, b):
    return _mm_nt(a, b)


def _bdot_nt_fwd(a, b):
    return _mm_nt(a, b), (a, b)


def _bdot_nt_bwd(res, g):
    a, b = res
    return _mm(g, b).astype(a.dtype), _mm_tn(g, a).astype(b.dtype)


bdot_nt.defvjp(_bdot_nt_fwd, _bdot_nt_bwd)


@jax.custom_vjp
def bdot_tn(a, b):
    return _mm_tn(a, b)


def _bdot_tn_fwd(a, b):
    return _mm_tn(a, b), (a, b)


def _bdot_tn_bwd(res, g):
    a, b = res
    return _mm_nt(b, g).astype(a.dtype), _mm(a, g).astype(b.dtype)


bdot_tn.defvjp(_bdot_tn_fwd, _bdot_tn_bwd)


def hdot(a, b):
    return jnp.dot(a, b, precision=HIGHEST, preferred_element_type=f32)


@functools.partial(jax.custom_vjp, nondiff_argnums=(1,))
def roll_rows(x, k):
    return pltpu.roll(x, k, 0)


def _roll_rows_fwd(x, k):
    return pltpu.roll(x, k, 0), None


def _roll_rows_bwd(k, _, g):
    return (pltpu.roll(g, g.shape[0] - k, 0),)


roll_rows.defvjp(_roll_rows_fwd, _roll_rows_bwd)


def _row_ids(shape):
    return lax.broadcasted_iota(jnp.int32, shape, 0)


def _shift_rows(x, d):
    return jnp.where(_row_ids(x.shape) >= d, roll_rows(x, d), 0.0)


def _rowsel(x, i):
    return jnp.sum(jnp.where(_row_ids(x.shape) == i, x, 0.0), axis=0, keepdims=True)


def _rms(x, g):
    return x * lax.rsqrt(jnp.mean(x * x, axis=-1, keepdims=True) + EPS) * g


def _layer_norm(x, g, b):
    mu = jnp.mean(x, axis=-1, keepdims=True)
    xc = x - mu
    return xc * lax.rsqrt(jnp.mean(xc * xc, axis=-1, keepdims=True) + EPS) * g + b


def _s5_prep(are, aim, ls, bre, bim):
    step = jnp.exp(ls)
    mag = jnp.exp(are * step)
    lr = mag * jnp.cos(aim * step)
    li = mag * jnp.sin(aim * step)
    den = are * are + aim * aim
    nr, ni = lr - 1.0, li
    fr = (nr * are + ni * aim) / den
    fi = (ni * are - nr * aim) / den
    bbr = fr * bre - fi * bim
    bbi = fr * bim + fi * bre
    pr = jnp.broadcast_to(lr, (S5_CHUNK, N_STATE))
    pi = jnp.broadcast_to(li, (S5_CHUNK, N_STATE))
    rows8 = _row_ids((8, N_STATE))
    qr = jnp.zeros((8, N_STATE), f32)
    qi = jnp.zeros((8, N_STATE), f32)
    d, k = 1, 0
    while d < S5_CHUNK:
        qr = qr + jnp.where(rows8 == k, _rowsel(pr, d - 1), 0.0)
        qi = qi + jnp.where(rows8 == k, _rowsel(pi, d - 1), 0.0)
        keep = _row_ids(pr.shape) >= d
        sr, si = roll_rows(pr, d), roll_rows(pi, d)
        pr, pi = jnp.where(keep, pr * sr - pi * si, pr), jnp.where(keep, pr * si + pi * sr, pi)
        d, k = 2 * d, k + 1
    return pr, pi, qr, qi, bbr, bbi


def _s5_chunk(u, gate, hr, hi, pr, pi, qr, qi, bbr, bbi, cr, ci, dr, wglu, bglu):
    n = u.shape[0]
    xr = bdot(u, bbr)
    xi = bdot(u, bbi)
    d, k = 1, 0
    while d < n:
        lr, li = _rowsel(qr, k), _rowsel(qi, k)
        sr, si = _shift_rows(xr, d), _shift_rows(xi, d)
        xr, xi = xr + lr * sr - li * si, xi + lr * si + li * sr
        d, k = 2 * d, k + 1
    xr, xi = xr + pr * hr - pi * hi, xi + pr * hi + pi * hr
    y = bdot(xr, cr) - bdot(xi, ci) + dr * u
    y = jax.nn.gelu(y)
    y = y * jax.nn.sigmoid(bdot(y, wglu) + bglu)
    return y * jax.nn.silu(gate), _rowsel(xr, n - 1), _rowsel(xi, n - 1)


def _dn_pre(xc, xp, w0, w1, w2, w3, is_start, col):
    xp = jnp.where(is_start, 0.0, xp)
    rows = _row_ids(xc.shape)
    acc = w3 * xc
    for d, w in ((1, w2), (2, w1), (3, w0)):
        acc = acc + w * jnp.where(rows >= d, roll_rows(xc, d), roll_rows(xp, d))
    y = jax.nn.silu(acc)
    nrm = y * lax.rsqrt(jnp.sum(y * y, axis=-1, keepdims=True) + EPS)
    nrm = nrm * jnp.where(col < DN_HEADS, DN_HEAD_DIM ** -0.5, 1.0)
    return jnp.where(col < 2 * DN_HEADS, nrm, y)


def _dn_local(qs, ks, vs, ab, alog, dtb):
    c = DN_CHUNK
    ri = lax.broadcasted_iota(jnp.int32, (c, c), 0)
    ci = lax.broadcasted_iota(jnp.int32, (c, c), 1)
    causal, strict = ri >= ci, ri > ci
    eye = (ri == ci).astype(f32)
    graw = -jnp.exp(alog) * jax.nn.softplus(ab + dtb)
    gcum = hdot(causal.astype(f32), graw)
    gcum_t = gcum.T
    sig = jax.nn.sigmoid(ab)
    values, k_cds, attns, q_decs, k_decs = [], [], [], [], []
    for h in range(DN_HEADS):
        q, k, v = qs[h], ks[h], vs[h]
        gc = gcum[:, h:h + 1]
        gcr = gcum_t[h:h + 1, :]
        beta = sig[:, DN_HEADS + h:DN_HEADS + h + 1]
        decay = jnp.where(causal, jnp.exp(jnp.where(causal, gc - gcr, 0.0)), 0.0)
        kb = k * beta
        neg_m = -jnp.where(strict, bdot_nt(kb, k) * decay, 0.0)
        inv = eye + neg_m
        pw = neg_m
        for _ in range(5):
            pw = hdot(pw, pw)
            inv = inv + hdot(inv, pw)
        egc = jnp.exp(gc)
        values.append(hdot(inv, v * beta))
        k_cds.append(hdot(inv, kb * egc))
        attns.append(bdot_nt(q, k) * decay)
        q_decs.append(q * egc)
        k_decs.append(k * jnp.exp(gc[c - 1:c, :] - gc))
    return values, k_cds, attns, q_decs, k_decs, jnp.exp(gcum[c - 1:c, :])


def _dn_step(value, k_cd, attn, q_dec, k_dec, last, gg, st, ng):
    v_new = value - bdot(k_cd, st)
    o = bdot(q_dec, st) + bdot(attn, v_new)
    return _rms(o, ng) * jax.nn.silu(gg), st * last + bdot_tn(k_dec, v_new)


def _sg_chunk(u, v, gate, lng, lnb, ws, bt):
    n = SG_CHUNK
    ug = jax.nn.gelu(u)
    vn = _layer_norm(jax.nn.gelu(v), lng, lnb)
    causal = lax.broadcasted_iota(jnp.int32, (n, n), 0) >= lax.broadcasted_iota(jnp.int32, (n, n), 1)
    lane = lax.broadcasted_iota(jnp.int32, (n, D_SG), 1)
    s = jnp.zeros((n, D_SG), f32)
    for h in range(SG_HEADS):
        t = bdot(jnp.where(causal, ws[h], 0.0), vn) + bt[:, h:h + 1]
        s = s + jnp.where((lane >= h * SG_HEAD_DIM) & (lane < (h + 1) * SG_HEAD_DIM), t, 0.0)
    return ug * s * jax.nn.silu(gate)


def _cp(n_grid, vmem=None):
    return pltpu.CompilerParams(dimension_semantics=("arbitrary",) * n_grid, vmem_limit_bytes=vmem)


def _full(shape):
    nd = len(shape)
    return pl.BlockSpec(tuple(shape), lambda *_: (0,) * nd)


def _rows(tm, ncol):
    return pl.BlockSpec((tm, ncol), lambda i: (i, 0))


def _sds(shape, dtype=f32):
    return jax.ShapeDtypeStruct(tuple(shape), dtype)


def _acc(ref, val, first):
    @pl.when(first)
    def _():
        ref[...] = val

    @pl.when(jnp.logical_not(first))
    def _():
        ref[...] += val


def in_fwd(x, g, w, name):
    t, tm = x.shape[0], 256

    def body(x_ref, g_ref, w_ref, h_ref, *z_refs):
        h = _rms(x_ref[...], g_ref[...]).astype(_MXU)
        h_ref[...] = h
        for z_ref, (a, b) in zip(z_refs, Z_COLS):
            z_ref[...] = jnp.dot(h, w_ref[:, a:b], preferred_element_type=f32)

    widths = [b - a for a, b in Z_COLS]
    return pl.pallas_call(
        body, name=name, grid=(t // tm,),
        in_specs=[_rows(tm, D_MODEL), _full((1, D_MODEL)), _full((D_MODEL, D_IN_PAD))],
        out_specs=[_rows(tm, D_MODEL)] + [_rows(tm, n) for n in widths],
        out_shape=[_sds((t, D_MODEL), _MXU)] + [_sds((t, n)) for n in widths],
        compiler_params=_cp(1, VMEM_BIG),
    )(x, g, w)


def in_bwd(x, g, w, dzs, dres, name):
    t, tm = x.shape[0], 256
    widths = [b - a for a, b in Z_COLS]

    def body(x_ref, g_ref, w_ref, dres_ref, *rest):
        dz_refs, (dx_ref, dg_ref) = rest[:5], rest[5:]
        dh = jnp.zeros((tm, D_MODEL), f32)
        for dz_ref, (a, b) in zip(dz_refs, Z_COLS):
            dh = dh + _mm_nt(dz_ref[...], w_ref[:, a:b])
        _, vj = jax.vjp(_rms, x_ref[...], g_ref[...])
        dx, dg = vj(dh)
        dx_ref[...] = dres_ref[...] + dx
        _acc(dg_ref, dg, pl.program_id(0) == 0)

    return pl.pallas_call(
        body, name=name, grid=(t // tm,),
        in_specs=[_rows(tm, D_MODEL), _full((1, D_MODEL)), _full((D_MODEL, D_IN_PAD)), _rows(tm, D_MODEL)]
        + [_rows(tm, n) for n in widths],
        out_specs=[_rows(tm, D_MODEL), _full((1, D_MODEL))],
        out_shape=[_sds((t, D_MODEL)), _sds((1, D_MODEL))],
        compiler_params=_cp(1, VMEM_BIG),
    )(x, g, w, dres, *dzs)


def wgrad(a, g, name):
    t, k = a.shape
    n = g.shape[1]
    tm = 512
    tn = n if n <= 768 else (768 if n % 768 == 0 else 512)

    def body(a_ref, g_ref, o_ref):
        _acc(o_ref, _mm_tn(a_ref[...], g_ref[...]), pl.program_id(1) == 0)

    return pl.pallas_call(
        body, name=name, grid=(n // tn, t // tm),
        in_specs=[pl.BlockSpec((tm, k), lambda j, i: (i, 0)), pl.BlockSpec((tm, tn), lambda j, i: (i, j))],
        out_specs=pl.BlockSpec((k, tn), lambda j, i: (0, j)),
        out_shape=_sds((k, n)),
        compiler_params=_cp(2, VMEM_BIG),
    )(a, g)


def post_fwd(x, ys, yd, yg, p, wout, pg, wgate, wple, name):
    t, tm = x.shape[0], 256

    def body(x_ref, ys_ref, yd_ref, yg_ref, p_ref, wout_ref, pg_ref, wgate_ref, wple_ref,
             x2_ref, x1_ref, y_ref, hn_ref):
        y = jnp.concatenate([ys_ref[...], yd_ref[...], yg_ref[...]], axis=1).astype(_MXU)
        y_ref[...] = y
        x1 = x_ref[...] + jnp.dot(y, wout_ref[...], preferred_element_type=f32)
        x1_ref[...] = x1
        hn = _rms(x1, pg_ref[...]).astype(_MXU)
        hn_ref[...] = hn
        gp = jnp.dot(hn, wgate_ref[...], preferred_element_type=f32)
        pp = _mm(p_ref[...], wple_ref[...])
        x2_ref[...] = x1 + jax.nn.sigmoid(gp) * pp

    return pl.pallas_call(
        body, name=name, grid=(t // tm,),
        in_specs=[_rows(tm, D_MODEL), _rows(tm, D_SSM), _rows(tm, D_DN), _rows(tm, D_SG), _rows(tm, D_PLE),
                  _full((D_MODEL, D_MODEL)), _full((1, D_MODEL)), _full((D_MODEL, D_MODEL)), _full((D_PLE, D_MODEL))],
        out_specs=[_rows(tm, D_MODEL)] * 4,
        out_shape=[_sds((t, D_MODEL)), _sds((t, D_MODEL)), _sds((t, D_MODEL), _MXU), _sds((t, D_MODEL), _MXU)],
        compiler_params=_cp(1, VMEM_BIG),
    )(x, ys, yd, yg, p, wout, pg, wgate, wple)


def post_bwd(dx2, x1, hn, p, wout, pg, wgate, wple, name):
    t, tm = dx2.shape[0], 256

    def body(dx2_ref, x1_ref, hn_ref, p_ref, wout_ref, pg_ref, wgate_ref, wple_ref,
             dx1_ref, dgp_ref, dpp_ref, dys_ref, dyd_ref, dyg_ref, dpg_ref):
        dx2 = dx2_ref[...]
        gp = jnp.dot(hn_ref[...], wgate_ref[...], preferred_element_type=f32)
        pp = _mm(p_ref[...], wple_ref[...])
        sg = jax.nn.sigmoid(gp)
        dpp_ref[...] = (dx2 * sg).astype(_MXU)
        dgp = (dx2 * pp * sg * (1.0 - sg)).astype(_MXU)
        dgp_ref[...] = dgp
        dhn = _mm_nt(dgp, wgate_ref[...])
        _, vj = jax.vjp(_rms, x1_ref[...], pg_ref[...])
        dx1n, dpg = vj(dhn)
        dx1 = dx2 + dx1n
        dx1_ref[...] = dx1
        dy = _mm_nt(dx1, wout_ref[...])
        dys_ref[...] = dy[:, :D_SSM]
        dyd_ref[...] = dy[:, D_SSM:D_SSM + D_DN]
        dyg_ref[...] = dy[:, D_SSM + D_DN:]
        _acc(dpg_ref, dpg, pl.program_id(0) == 0)

    return pl.pallas_call(
        body, name=name, grid=(t // tm,),
        in_specs=[_rows(tm, D_MODEL), _rows(tm, D_MODEL), _rows(tm, D_MODEL), _rows(tm, D_PLE),
                  _full((D_MODEL, D_MODEL)), _full((1, D_MODEL)), _full((D_MODEL, D_MODEL)), _full((D_PLE, D_MODEL))],
        out_specs=[_rows(tm, D_MODEL), _rows(tm, D_MODEL), _rows(tm, D_MODEL), _rows(tm, D_SSM), _rows(tm, D_DN),
                   _rows(tm, D_SG), _full((1, D_MODEL))],
        out_shape=[_sds((t, D_MODEL)), _sds((t, D_MODEL), _MXU), _sds((t, D_MODEL), _MXU), _sds((t, D_SSM)),
                   _sds((t, D_DN)), _sds((t, D_SG)), _sds((1, D_MODEL))],
        compiler_params=_cp(1, VMEM_BIG),
    )(dx2, x1, hn, p, wout, pg, wgate, wple)


def loss_fwd_bwd(x, fg, target, name):
    t, tm = x.shape[0], 512

    def body(x_ref, fg_ref, t_ref, loss_ref, dx_ref, dfg_ref):
        def f(xv, gv):
            err = _rms(xv, gv) - t_ref[...]
            return 0.5 * jnp.sum(jnp.mean(err * err, axis=-1))

        val, vj = jax.vjp(f, x_ref[...], fg_ref[...])
        dx, dfg = vj(jnp.ones((), f32))
        dx_ref[...] = dx
        first = pl.program_id(0) == 0
        _acc(dfg_ref, dfg, first)
        _acc(loss_ref, jnp.full((8, LANE), val, f32), first)

    return pl.pallas_call(
        body, name=name, grid=(t // tm,),
        in_specs=[_rows(tm, D_MODEL), _full((1, D_MODEL)), _rows(tm, D_MODEL)],
        out_specs=[_full((8, LANE)), _rows(tm, D_MODEL), _full((1, D_MODEL))],
        out_shape=[_sds((8, LANE)), _sds((t, D_MODEL)), _sds((1, D_MODEL))],
        compiler_params=_cp(1),
    )(x, fg, target)


def s5_prep_fwd(are, aim, ls, bre, bim, name):
    def body(are_ref, aim_ref, ls_ref, bre_ref, bim_ref, *outs):
        vals = _s5_prep(are_ref[...], aim_ref[...], ls_ref[...], bre_ref[...], bim_ref[...])
        for o, v in zip(outs, vals):
            o[...] = v

    shapes = [(S5_CHUNK, N_STATE)] * 2 + [(8, N_STATE)] * 2 + [(D_SSM, N_STATE)] * 2
    return pl.pallas_call(body, name=name, out_shape=[_sds(s) for s in shapes])(are, aim, ls, bre, bim)


def s5_prep_bwd(are, aim, ls, bre, bim, cts, name):
    def body(are_ref, aim_ref, ls_ref, bre_ref, bim_ref, *rest):
        ct_refs, outs = rest[:6], rest[6:]
        _, vj = jax.vjp(_s5_prep, are_ref[...], aim_ref[...], ls_ref[...], bre_ref[...], bim_ref[...])
        for o, v in zip(outs, vj(tuple(r[...] for r in ct_refs))):
            o[...] = v

    shapes = [(1, N_STATE)] * 3 + [(D_SSM, N_STATE)] * 2
    return pl.pallas_call(body, name=name, out_shape=[_sds(s) for s in shapes])(are, aim, ls, bre, bim, *cts)


_S5_PARAM_SHAPES = ((S5_CHUNK, N_STATE), (S5_CHUNK, N_STATE), (8, N_STATE), (8, N_STATE), (D_SSM, N_STATE),
                    (D_SSM, N_STATE), (N_STATE, D_SSM), (N_STATE, D_SSM), (1, D_SSM), (D_SSM, D_SSM), (1, D_SSM))


def s5_fwd(z, params, nb, name):
    t = z.shape[0]
    nc = t // nb // S5_CHUNK

    def body(z_ref, *rest):
        p_refs, (y_ref, hs_ref, hr_s, hi_s) = rest[:11], rest[11:]

        @pl.when(pl.program_id(1) == 0)
        def _():
            hr_s[...] = jnp.zeros_like(hr_s)
            hi_s[...] = jnp.zeros_like(hi_s)

        hr, hi = hr_s[...], hi_s[...]
        hs_ref[0, :, :N_STATE] = hr
        hs_ref[0, :, N_STATE:] = hi
        y, nhr, nhi = _s5_chunk(z_ref[:, :D_SSM], z_ref[:, D_SSM:], hr, hi, *[r[...] for r in p_refs])
        y_ref[...] = y
        hr_s[...] = nhr
        hi_s[...] = nhi

    return pl.pallas_call(
        body, name=name, grid=(nb, nc),
        in_specs=[pl.BlockSpec((S5_CHUNK, 2 * D_SSM), lambda b, c: (b * nc + c, 0))]
        + [_full(s) for s in _S5_PARAM_SHAPES],
        out_specs=[pl.BlockSpec((S5_CHUNK, D_SSM), lambda b, c: (b * nc + c, 0)),
                   pl.BlockSpec((1, 1, 2 * N_STATE), lambda b, c: (b * nc + c, 0, 0))],
        out_shape=[_sds((t, D_SSM)), _sds((nb * nc, 1, 2 * N_STATE))],
        scratch_shapes=[pltpu.VMEM((1, N_STATE), f32), pltpu.VMEM((1, N_STATE), f32)],
        compiler_params=_cp(2, VMEM_BIG),
    )(z, *params)


def s5_bwd(z, params, hs, dy, nb, name):
    t = z.shape[0]
    nc = t // nb // S5_CHUNK

    def body(z_ref, hs_ref, dy_ref, *rest):
        p_refs, dz_ref, dp_refs, (dhr_s, dhi_s) = rest[:11], rest[11], rest[12:23], rest[23:]

        @pl.when(pl.program_id(1) == 0)
        def _():
            dhr_s[...] = jnp.zeros_like(dhr_s)
            dhi_s[...] = jnp.zeros_like(dhi_s)

        prim = (z_ref[:, :D_SSM], z_ref[:, D_SSM:], hs_ref[0, :, :N_STATE], hs_ref[0, :, N_STATE:]) + tuple(
            r[...] for r in p_refs)
        _, vj = jax.vjp(_s5_chunk, *prim)
        cts = vj((dy_ref[...], dhr_s[...], dhi_s[...]))
        dz_ref[:, :D_SSM] = cts[0]
        dz_ref[:, D_SSM:] = cts[1]
        dhr_s[...] = cts[2]
        dhi_s[...] = cts[3]
        first = (pl.program_id(0) == 0) & (pl.program_id(1) == 0)
        for r, v in zip(dp_refs, cts[4:]):
            _acc(r, v, first)

    rev = lambda b, c: (b * nc + nc - 1 - c, 0)
    return pl.pallas_call(
        body, name=name, grid=(nb, nc),
        in_specs=[pl.BlockSpec((S5_CHUNK, 2 * D_SSM), rev),
                  pl.BlockSpec((1, 1, 2 * N_STATE), lambda b, c: (b * nc + nc - 1 - c, 0, 0)),
                  pl.BlockSpec((S5_CHUNK, D_SSM), rev)] + [_full(s) for s in _S5_PARAM_SHAPES],
        out_specs=[pl.BlockSpec((S5_CHUNK, 2 * D_SSM), rev)] + [_full(s) for s in _S5_PARAM_SHAPES],
        out_shape=[_sds((t, 2 * D_SSM))] + [_sds(s) for s in _S5_PARAM_SHAPES],
        scratch_shapes=[pltpu.VMEM((1, N_STATE), f32), pltpu.VMEM((1, N_STATE), f32)],
        compiler_params=_cp(2, VMEM_BIG),
    )(z, hs, dy, *params)


DN_PRE_ROWS = 512
DN_COLS = 3 * D_DN // LANE


def dn_pre_fwd(zq, convw, seq, name):
    t, tb = zq.shape[0], DN_PRE_ROWS
    per_seq = seq // tb

    def body(xc_ref, xp_ref, w_ref, o_ref):
        i, j = pl.program_id(1), pl.program_id(0)
        o_ref[...] = _dn_pre(xc_ref[...], xp_ref[...], w_ref[0:1, :], w_ref[1:2, :], w_ref[2:3, :], w_ref[3:4, :],
                             i % per_seq == 0, j)

    return pl.pallas_call(
        body, name=name, grid=(DN_COLS, t // tb),
        in_specs=[pl.BlockSpec((tb, LANE), lambda j, i: (i, j)),
                  pl.BlockSpec((tb, LANE), lambda j, i: (jnp.maximum(i - 1, 0), j)),
                  pl.BlockSpec((DN_CONV, LANE), lambda j, i: (0, j))],
        out_specs=pl.BlockSpec((tb, LANE), lambda j, i: (i, j)),
        out_shape=_sds((t, 3 * D_DN)),
        compiler_params=_cp(2),
    )(zq, zq, convw)


def dn_pre_bwd(zq, convw, dqkv, seq, name):
    t, tb = zq.shape[0], DN_PRE_ROWS
    nrow = t // tb
    per_seq = seq // tb

    def body(xc_ref, xp_ref, w_ref, d_ref, dx_ref, dw_ref, carry):
        j, step = pl.program_id(0), pl.program_id(1)
        i = nrow - 1 - step

        @pl.when(step == 0)
        def _():
            carry[...] = jnp.zeros_like(carry)

        fn = functools.partial(_dn_pre, is_start=i % per_seq == 0, col=j)
        _, vj = jax.vjp(fn, xc_ref[...], xp_ref[...], w_ref[0:1, :], w_ref[1:2, :], w_ref[2:3, :], w_ref[3:4, :])
        dxc, dxp, dw0, dw1, dw2, dw3 = vj(d_ref[...])
        dx_ref[...] = dxc + carry[...]
        carry[...] = dxp
        for k, dw in enumerate((dw0, dw1, dw2, dw3)):
            @pl.when(step == 0)
            def _():
                dw_ref[k:k + 1, :] = dw

            @pl.when(step != 0)
            def _():
                dw_ref[k:k + 1, :] += dw

    return pl.pallas_call(
        body, name=name, grid=(DN_COLS, nrow),
        in_specs=[pl.BlockSpec((tb, LANE), lambda j, s: (nrow - 1 - s, j)),
                  pl.BlockSpec((tb, LANE), lambda j, s: (jnp.maximum(nrow - 2 - s, 0), j)),
                  pl.BlockSpec((DN_CONV, LANE), lambda j, s: (0, j)),
                  pl.BlockSpec((tb, LANE), lambda j, s: (nrow - 1 - s, j))],
        out_specs=[pl.BlockSpec((tb, LANE), lambda j, s: (nrow - 1 - s, j)),
                   pl.BlockSpec((DN_CONV, LANE), lambda j, s: (0, j))],
        out_shape=[_sds((t, 3 * D_DN)), _sds((DN_CONV, 3 * D_DN))],
        scratch_shapes=[pltpu.VMEM((tb, LANE), f32)],
        compiler_params=_cp(2),
    )(zq, zq, convw, dqkv)


DN_LOCAL_CHUNKS = 2
DN_ATTN = DN_HEADS * DN_CHUNK


def _dn_heads(ref, rows, base=0):
    return [ref[rows, base + h * DN_HEAD_DIM:base + (h + 1) * DN_HEAD_DIM] for h in range(DN_HEADS)]


def dn_local_fwd(qkv, ab, alog, dtb, name):
    t = qkv.shape[0]
    c, n = DN_CHUNK, DN_LOCAL_CHUNKS

    def body(qkv_ref, ab_ref, alog_ref, dtb_ref, val_ref, kcd_ref, attn_ref, qd_ref, kd_ref, el_ref):
        for j in range(n):
            r = pl.ds(j * c, c)
            vals, kcds, attns, qds, kds, el = _dn_local(_dn_heads(qkv_ref, r), _dn_heads(qkv_ref, r, D_DN),
                                                        _dn_heads(qkv_ref, r, 2 * D_DN), ab_ref[r, :], alog_ref[...],
                                                        dtb_ref[...])
            for h in range(DN_HEADS):
                lo, hi = h * DN_HEAD_DIM, (h + 1) * DN_HEAD_DIM
                val_ref[r, lo:hi] = vals[h]
                kcd_ref[r, lo:hi] = kcds[h]
                qd_ref[r, lo:hi] = qds[h]
                kd_ref[r, lo:hi] = kds[h]
                attn_ref[r, h * c:(h + 1) * c] = attns[h]
            el_ref[j] = el

    wide = _rows(n * c, D_DN)
    return pl.pallas_call(
        body, name=name, grid=(t // (n * c),),
        in_specs=[_rows(n * c, 3 * D_DN), _rows(n * c, LANE), _full((1, LANE)), _full((1, LANE))],
        out_specs=[wide, wide, _rows(n * c, DN_ATTN), wide, wide, pl.BlockSpec((n, 1, LANE), lambda i: (i, 0, 0))],
        out_shape=[_sds((t, D_DN)), _sds((t, D_DN)), _sds((t, DN_ATTN)), _sds((t, D_DN)), _sds((t, D_DN)),
                   _sds((t // c, 1, LANE))],
        compiler_params=_cp(1),
    )(qkv, ab, alog, dtb)


def dn_local_bwd(qkv, ab, alog, dtb, cts, name):
    t = qkv.shape[0]
    c, n = DN_CHUNK, DN_LOCAL_CHUNKS

    def body(qkv_ref, ab_ref, alog_ref, dtb_ref, dval_ref, dkcd_ref, dattn_ref, dqd_ref, dkd_ref, del_ref,
             dqkv_ref, dab_ref, dalog_ref, ddtb_ref):
        tot = None
        for j in range(n):
            r = pl.ds(j * c, c)
            _, vj = jax.vjp(_dn_local, _dn_heads(qkv_ref, r), _dn_heads(qkv_ref, r, D_DN),
                            _dn_heads(qkv_ref, r, 2 * D_DN), ab_ref[r, :], alog_ref[...], dtb_ref[...])
            dattn = [dattn_ref[r, h * c:(h + 1) * c] for h in range(DN_HEADS)]
            dq, dk, dv, dab, dalog, ddtb = vj((_dn_heads(dval_ref, r), _dn_heads(dkcd_ref, r), dattn,
                                               _dn_heads(dqd_ref, r), _dn_heads(dkd_ref, r), del_ref[j]))
            for h in range(DN_HEADS):
                lo, hi = h * DN_HEAD_DIM, (h + 1) * DN_HEAD_DIM
                dqkv_ref[r, lo:hi] = dq[h]
                dqkv_ref[r, D_DN + lo:D_DN + hi] = dk[h]
                dqkv_ref[r, 2 * D_DN + lo:2 * D_DN + hi] = dv[h]
            dab_ref[r, :] = dab
            tot = [dalog, ddtb] if tot is None else [tot[0] + dalog, tot[1] + ddtb]
        first = pl.program_id(0) == 0
        _acc(dalog_ref, tot[0], first)
        _acc(ddtb_ref, tot[1], first)

    wide = _rows(n * c, D_DN)
    return pl.pallas_call(
        body, name=name, grid=(t // (n * c),),
        in_specs=[_rows(n * c, 3 * D_DN), _rows(n * c, LANE), _full((1, LANE)), _full((1, LANE)),
                  wide, wide, _rows(n * c, DN_ATTN), wide, wide, pl.BlockSpec((n, 1, LANE), lambda i: (i, 0, 0))],
        out_specs=[_rows(n * c, 3 * D_DN), _rows(n * c, LANE), _full((1, LANE)), _full((1, LANE))],
        out_shape=[_sds((t, 3 * D_DN)), _sds((t, LANE)), _sds((1, LANE)), _sds((1, LANE))],
        compiler_params=_cp(1),
    )(qkv, ab, alog, dtb, *cts)


def _seq_view(a, nb):
    return a.reshape((nb, a.shape[0] // nb) + a.shape[1:])


def dn_scan_fwd(loc, gg, ng, nb, name):
    val, kcd, attn, qd, kd, el = loc
    t = val.shape[0]
    c = DN_CHUNK
    nc = t // nb // c
    ns = nb * DN_HEADS

    def body(val_ref, kcd_ref, attn_ref, qd_ref, kd_ref, el_ref, gg_ref, ng_ref, y_ref, ss_ref, st):
        @pl.when(pl.program_id(0) == 0)
        def _():
            st[...] = jnp.zeros_like(st)

        for b in range(nb):
            for h in range(DN_HEADS):
                i = b * DN_HEADS + h
                cols = slice(h * DN_HEAD_DIM, (h + 1) * DN_HEAD_DIM)
                s = st[i]
                ss_ref[0, i] = s
                y, s_new = _dn_step(val_ref[b, :, cols], kcd_ref[b, :, cols], attn_ref[b, :, h * c:(h + 1) * c],
                                    qd_ref[b, :, cols], kd_ref[b, :, cols], el_ref[b, 0, :, h:h + 1],
                                    gg_ref[b, :, cols], s, ng_ref[...])
                y_ref[b, :, cols] = y
                st[i] = s_new

    def blk(w):
        return pl.BlockSpec((nb, c, w), lambda k: (0, k, 0))

    el_spec = pl.BlockSpec((nb, 1, 1, LANE), lambda k: (0, k, 0, 0))
    y, ss = pl.pallas_call(
        body, name=name, grid=(nc,),
        in_specs=[blk(D_DN), blk(D_DN), blk(DN_ATTN), blk(D_DN), blk(D_DN), el_spec, blk(D_DN), _full((1, LANE))],
        out_specs=[blk(D_DN), pl.BlockSpec((1, ns, DN_HEAD_DIM, DN_HEAD_DIM), lambda k: (k, 0, 0, 0))],
        out_shape=[_sds((nb, t // nb, D_DN)), _sds((nc, ns, DN_HEAD_DIM, DN_HEAD_DIM))],
        scratch_shapes=[pltpu.VMEM((ns, DN_HEAD_DIM, DN_HEAD_DIM), f32)],
        compiler_params=_cp(1),
    )(_seq_view(val, nb), _seq_view(kcd, nb), _seq_view(attn, nb), _seq_view(qd, nb), _seq_view(kd, nb),
      el.reshape(nb, nc, 1, LANE), _seq_view(gg, nb), ng)
    return y.reshape(t, D_DN), ss


def dn_scan_bwd(loc, gg, ng, ss, dy, nb, name):
    val, kcd, attn, qd, kd, el = loc
    t = val.shape[0]
    c = DN_CHUNK
    nc = t // nb // c
    ns = nb * DN_HEADS

    def body(val_ref, kcd_ref, attn_ref, qd_ref, kd_ref, el_ref, gg_ref, ng_ref, ss_ref, dy_ref,
             dval_ref, dkcd_ref, dattn_ref, dqd_ref, dkd_ref, del_ref, dgg_ref, dng_ref, dst):
        @pl.when(pl.program_id(0) == 0)
        def _():
            dst[...] = jnp.zeros_like(dst)

        lane = lax.broadcasted_iota(jnp.int32, (1, LANE), 1)
        dng_tot = jnp.zeros((1, LANE), f32)
        for b in range(nb):
            del_row = jnp.zeros((1, LANE), f32)
            for h in range(DN_HEADS):
                i = b * DN_HEADS + h
                cols = slice(h * DN_HEAD_DIM, (h + 1) * DN_HEAD_DIM)
                _, vj = jax.vjp(_dn_step, val_ref[b, :, cols], kcd_ref[b, :, cols], attn_ref[b, :, h * c:(h + 1) * c],
                                qd_ref[b, :, cols], kd_ref[b, :, cols], el_ref[b, 0, :, h:h + 1], gg_ref[b, :, cols],
                                ss_ref[0, i], ng_ref[...])
                dval, dkcd, dattn, dqd, dkd, dlast, dgg, ds, dng = vj((dy_ref[b, :, cols], dst[i]))
                dval_ref[b, :, cols] = dval
                dkcd_ref[b, :, cols] = dkcd
                dattn_ref[b, :, h * c:(h + 1) * c] = dattn
                dqd_ref[b, :, cols] = dqd
                dkd_ref[b, :, cols] = dkd
                dgg_ref[b, :, cols] = dgg
                dst[i] = ds
                del_row = del_row + jnp.where(lane == h, dlast, 0.0)
                dng_tot = dng_tot + dng
            del_ref[b, 0] = del_row
        _acc(dng_ref, dng_tot, pl.program_id(0) == 0)

    def blk(w):
        return pl.BlockSpec((nb, c, w), lambda k: (0, nc - 1 - k, 0))

    el_spec = pl.BlockSpec((nb, 1, 1, LANE), lambda k: (0, nc - 1 - k, 0, 0))
    outs = pl.pallas_call(
        body, name=name, grid=(nc,),
        in_specs=[blk(D_DN), blk(D_DN), blk(DN_ATTN), blk(D_DN), blk(D_DN), el_spec, blk(D_DN), _full((1, LANE)),
                  pl.BlockSpec((1, ns, DN_HEAD_DIM, DN_HEAD_DIM), lambda k: (nc - 1 - k, 0, 0, 0)), blk(D_DN)],
        out_specs=[blk(D_DN), blk(D_DN), blk(DN_ATTN), blk(D_DN), blk(D_DN), el_spec, blk(D_DN), _full((1, LANE))],
        out_shape=[_sds((nb, t // nb, D_DN)), _sds((nb, t // nb, D_DN)), _sds((nb, t // nb, DN_ATTN)),
                   _sds((nb, t // nb, D_DN)), _sds((nb, t // nb, D_DN)), _sds((nb, nc, 1, LANE)),
                   _sds((nb, t // nb, D_DN)), _sds((1, LANE))],
        scratch_shapes=[pltpu.VMEM((ns, DN_HEAD_DIM, DN_HEAD_DIM), f32)],
        compiler_params=_cp(1),
    )(_seq_view(val, nb), _seq_view(kcd, nb), _seq_view(attn, nb), _seq_view(qd, nb), _seq_view(kd, nb),
      el.reshape(nb, nc, 1, LANE), _seq_view(gg, nb), ng, ss, _seq_view(dy, nb))
    dloc = [o.reshape((t,) + o.shape[2:]) for o in outs[:5]] + [outs[5].reshape(t // c, 1, LANE)]
    return dloc, outs[6].reshape(t, D_DN), outs[7]


SG_ROWS = 512


def sg_fwd(z, lng, lnb, w, bt, name):
    t = z.shape[0]

    def body(z_ref, lng_ref, lnb_ref, w_ref, bt_ref, y_ref):
        ws = [w_ref[h] for h in range(SG_HEADS)]
        for k in range(SG_ROWS // SG_CHUNK):
            r = pl.ds(k * SG_CHUNK, SG_CHUNK)
            y_ref[r, :] = _sg_chunk(z_ref[r, :D_SG], z_ref[r, D_SG:2 * D_SG], z_ref[r, 2 * D_SG:], lng_ref[...],
                                    lnb_ref[...], ws, bt_ref[...])

    return pl.pallas_call(
        body, name=name, grid=(t // SG_ROWS,),
        in_specs=[_rows(SG_ROWS, 3 * D_SG), _full((1, D_SG)), _full((1, D_SG)),
                  _full((SG_HEADS, SG_CHUNK, SG_CHUNK)), _full((SG_CHUNK, LANE))],
        out_specs=_rows(SG_ROWS, D_SG),
        out_shape=_sds((t, D_SG)),
        compiler_params=_cp(1),
    )(z, lng, lnb, w, bt)


def sg_bwd(z, lng, lnb, w, bt, dy, name):
    t = z.shape[0]

    def body(z_ref, lng_ref, lnb_ref, w_ref, bt_ref, dy_ref, dz_ref, dlng_ref, dlnb_ref, dw_ref, dbt_ref):
        ws = [w_ref[h] for h in range(SG_HEADS)]
        tot = None
        for k in range(SG_ROWS // SG_CHUNK):
            r = pl.ds(k * SG_CHUNK, SG_CHUNK)
            _, vj = jax.vjp(_sg_chunk, z_ref[r, :D_SG], z_ref[r, D_SG:2 * D_SG], z_ref[r, 2 * D_SG:], lng_ref[...],
                            lnb_ref[...], ws, bt_ref[...])
            du, dv, dgate, dlng, dlnb, dws, dbt = vj(dy_ref[r, :])
            dz_ref[r, :D_SG] = du
            dz_ref[r, D_SG:2 * D_SG] = dv
            dz_ref[r, 2 * D_SG:] = dgate
            part = [dlng, dlnb, dbt] + list(dws)
            tot = part if tot is None else [a + b for a, b in zip(tot, part)]
        first = pl.program_id(0) == 0
        _acc(dlng_ref, tot[0], first)
        _acc(dlnb_ref, tot[1], first)
        _acc(dbt_ref, tot[2], first)
        for h in range(SG_HEADS):
            @pl.when(first)
            def _():
                dw_ref[h] = tot[3 + h]

            @pl.when(jnp.logical_not(first))
            def _():
                dw_ref[h] += tot[3 + h]

    return pl.pallas_call(
        body, name=name, grid=(t // SG_ROWS,),
        in_specs=[_rows(SG_ROWS, 3 * D_SG), _full((1, D_SG)), _full((1, D_SG)),
                  _full((SG_HEADS, SG_CHUNK, SG_CHUNK)), _full((SG_CHUNK, LANE)), _rows(SG_ROWS, D_SG)],
        out_specs=[_rows(SG_ROWS, 3 * D_SG), _full((1, D_SG)), _full((1, D_SG)),
                   _full((SG_HEADS, SG_CHUNK, SG_CHUNK)), _full((SG_CHUNK, LANE))],
        out_shape=[_sds((t, 3 * D_SG)), _sds((1, D_SG)), _sds((1, D_SG)), _sds((SG_HEADS, SG_CHUNK, SG_CHUNK)),
                   _sds((SG_CHUNK, LANE))],
        compiler_params=_cp(1),
    )(z, lng, lnb, w, bt, dy)


def sum_adamw(recv, w, m, v, name):
    r = w.shape[0]
    tr = 256

    def body(recv_ref, w_ref, m_ref, v_ref, g_ref, d_ref, nm_ref, nv_ref):
        g = recv_ref[0].astype(f32)
        for k in range(1, N_DEV):
            g = g + recv_ref[k].astype(f32)
        wv = w_ref[...]
        nm = ADAM_B1 * m_ref[...] + (1.0 - ADAM_B1) * g
        nv = ADAM_B2 * v_ref[...] + (1.0 - ADAM_B2) * jnp.square(g)
        m_hat = nm / (1.0 - ADAM_B1 ** ADAM_STEP)
        v_hat = nv / (1.0 - ADAM_B2 ** ADAM_STEP)
        g_ref[...] = g
        d_ref[...] = -ADAM_LR * (m_hat / (jnp.sqrt(v_hat) + ADAM_EPS) + ADAM_WD * wv)
        nm_ref[...] = nm
        nv_ref[...] = nv

    return pl.pallas_call(
        body, name=name, grid=(r // tr,),
        in_specs=[pl.BlockSpec((N_DEV, tr, LANE), lambda i: (0, i, 0))] + [_rows(tr, LANE)] * 3,
        out_specs=[_rows(tr, LANE)] * 4,
        out_shape=[_sds((r, LANE))] * 4,
        compiler_params=_cp(1),
    )(recv, w, m, v)


_ANY = pl.BlockSpec(memory_space=pl.ANY)
_MESH = pl.DeviceIdType.MESH


def _flip(v, bit):
    return 1 - v if bit else v


def gather_weights(wa, wb, name):
    rels = ((1, 0), (0, 1), (1, 1))

    def body(a_ref, b_ref, oa_ref, ob_ref, send_sems, recv_sems, local_sems):
        x, y, c = lax.axis_index("x"), lax.axis_index("y"), lax.axis_index("c")
        mine = 2 * x + y
        local = [pltpu.make_async_copy(a_ref, oa_ref.at[mine], local_sems.at[0]),
                 pltpu.make_async_copy(b_ref, ob_ref.at[mine], local_sems.at[1])]
        for cp in local:
            cp.start()
        sends, recvs = [], []
        for r, (fx, fy) in enumerate(rels):
            px, py = _flip(x, fx), _flip(y, fy)
            for n, (src, dst) in enumerate(((a_ref, oa_ref), (b_ref, ob_ref))):
                k = 2 * r + n
                sends.append(pltpu.make_async_remote_copy(
                    src_ref=src, dst_ref=dst.at[mine], send_sem=send_sems.at[k], recv_sem=recv_sems.at[k],
                    device_id=(px, py, c), device_id_type=_MESH))
                recvs.append(pltpu.make_async_remote_copy(
                    src_ref=src, dst_ref=dst.at[2 * px + py], send_sem=send_sems.at[k], recv_sem=recv_sems.at[k],
                    device_id=(px, py, c), device_id_type=_MESH))
        for cp in sends:
            cp.start()
        for cp in recvs:
            cp.wait_recv()
        for cp in sends:
            cp.wait_send()
        for cp in local:
            cp.wait()

    return pl.pallas_call(
        body, name=name, in_specs=[_ANY, _ANY], out_specs=[_ANY, _ANY],
        out_shape=[_sds((N_CHIPS,) + wa.shape, wa.dtype), _sds((N_CHIPS,) + wb.shape, wb.dtype)],
        scratch_shapes=[pltpu.SemaphoreType.DMA((6,)), pltpu.SemaphoreType.DMA((6,)), pltpu.SemaphoreType.DMA((2,))],
    )(wa, wb)


def exchange_grads(ga, gb, name):
    rels = [(fx, fy, fc) for fx in (0, 1) for fy in (0, 1) for fc in (0, 1)][1:]

    def body(a_ref, b_ref, oa_ref, ob_ref, send_sems, recv_sems, local_sems):
        x, y, c = lax.axis_index("x"), lax.axis_index("y"), lax.axis_index("c")
        me = 4 * x + 2 * y + c
        local = [pltpu.make_async_copy(a_ref.at[2 * x + y], oa_ref.at[me], local_sems.at[0]),
                 pltpu.make_async_copy(b_ref, ob_ref.at[me], local_sems.at[1])]
        for cp in local:
            cp.start()
        sends, recvs = [], []
        for r, (fx, fy, fc) in enumerate(rels):
            px, py, pc = _flip(x, fx), _flip(y, fy), _flip(c, fc)
            peer = 4 * px + 2 * py + pc
            for n, (src, dst) in enumerate(((a_ref.at[2 * px + py], oa_ref), (b_ref, ob_ref))):
                k = 2 * r + n
                sends.append(pltpu.make_async_remote_copy(
                    src_ref=src, dst_ref=dst.at[me], send_sem=send_sems.at[k], recv_sem=recv_sems.at[k],
                    device_id=(px, py, pc), device_id_type=_MESH))
                recvs.append(pltpu.make_async_remote_copy(
                    src_ref=src, dst_ref=dst.at[peer], send_sem=send_sems.at[k], recv_sem=recv_sems.at[k],
                    device_id=(px, py, pc), device_id_type=_MESH))
        for cp in sends:
            cp.start()
        for cp in recvs:
            cp.wait_recv()
        for cp in sends:
            cp.wait_send()
        for cp in local:
            cp.wait()

    return pl.pallas_call(
        body, name=name, in_specs=[_ANY, _ANY], out_specs=[_ANY, _ANY],
        out_shape=[_sds((N_DEV,) + ga.shape[1:], ga.dtype), _sds((N_DEV,) + gb.shape, gb.dtype)],
        scratch_shapes=[pltpu.SemaphoreType.DMA((14,)), pltpu.SemaphoreType.DMA((14,)), pltpu.SemaphoreType.DMA((2,))],
    )(ga, gb)


def _pack_rows(parts, mult, dtype):
    flat = jnp.concatenate([a.reshape(-1) for a in parts]).astype(dtype)
    rows = -(-flat.shape[0] // (LANE * mult)) * mult
    return jnp.pad(flat, (0, rows * LANE - flat.shape[0])).reshape(rows, LANE)


def _unpack_rows(buf, shapes):
    flat = buf.reshape(-1)
    out, off = [], 0
    for s in shapes:
        n = 1
        for d in s:
            n *= d
        out.append(flat[off:off + n].reshape(s))
        off += n
    return out


def _permute_cols(w):
    pad = jnp.zeros(w.shape[:-1] + (D_IN_PAD - D_IN,), w.dtype)
    return jnp.concatenate([w[..., :2048], w[..., 2056:], w[..., 2048:2056], pad], axis=-1)


def _unpermute_cols(w):
    return jnp.concatenate([w[..., :2048], w[..., 3328:3336], w[..., 2048:3328]], axis=-1)


def _expand_b(b):
    eye = jnp.eye(SSM_GROUPS, dtype=b.dtype)
    return jnp.einsum("gnc,gh->gchn", b, eye).reshape(D_SSM, N_STATE)


def _extract_b(e):
    return jnp.einsum("gcgn->gnc", e.reshape(SSM_GROUPS, SSM_GROUP, SSM_GROUPS, SSM_STATE))


def _expand_c(c):
    eye = jnp.eye(SSM_GROUPS, dtype=c.dtype)
    return jnp.einsum("gcn,gh->gnhc", c, eye).reshape(N_STATE, D_SSM)


def _extract_c(e):
    return jnp.einsum("gngc->gcn", e.reshape(SSM_GROUPS, SSM_STATE, SSM_GROUPS, SSM_GROUP))


def _lane_row(v):
    return jnp.pad(v, (0, LANE - v.shape[0])).reshape(1, LANE)


def _layer_params(w, l):
    return dict(
        norm_g=w["norm_g"][l][None], win=w["w_in_perm"][l], wout=w["w_out"][l].astype(_MXU),
        pg=w["ple_norm_g"][l][None], wgate=w["w_ple_gate"][l].astype(_MXU), wple=w["w_ple"][l].astype(_MXU),
        are=w["ssm_a_re"][l].reshape(1, N_STATE), aim=w["ssm_a_im"][l].reshape(1, N_STATE),
        ls=jnp.repeat(w["ssm_log_step"][l], SSM_STATE).reshape(1, N_STATE),
        bre=_expand_b(w["ssm_b_re"][l]), bim=_expand_b(w["ssm_b_im"][l]),
        cr=_expand_c(w["ssm_c_re"][l]), ci=_expand_c(w["ssm_c_im"][l]),
        dr=w["ssm_d"][l].reshape(1, D_SSM), wglu=w["ssm_w_glu"][l].astype(f32), bglu=w["ssm_b_glu"][l][None],
        convw=w["dn_conv_w"][l], alog=_lane_row(w["dn_a_log"][l]), dtb=_lane_row(w["dn_dt_bias"][l]),
        ng=w["dn_norm_g"][l][None],
        lng=w["sg_ln_g"][l][None], lnb=w["sg_ln_b"][l][None], sgw=w["sg_w"][l],
        bt=jnp.pad(w["sg_b"][l].T, ((0, 0), (0, LANE - SG_HEADS))),
    )


def _layer_fwd(x, p, lp, nb, tag):
    seq = x.shape[0] // nb
    h, zs, zq, zg, zsg, zab = in_fwd(x, lp["norm_g"], lp["win"], f"in_fwd{tag}")
    prep = s5_prep_fwd(lp["are"], lp["aim"], lp["ls"], lp["bre"], lp["bim"], f"s5_prep_fwd{tag}")
    s5p = tuple(prep) + (lp["cr"], lp["ci"], lp["dr"], lp["wglu"], lp["bglu"])
    ys, hs = s5_fwd(zs, s5p, nb, f"s5_fwd{tag}")
    qkv = dn_pre_fwd(zq, lp["convw"], seq, f"dn_pre_fwd{tag}")
    loc = dn_local_fwd(qkv, zab, lp["alog"], lp["dtb"], f"dn_local_fwd{tag}")
    yd, ss = dn_scan_fwd(loc, zg, lp["ng"], nb, f"dn_scan_fwd{tag}")
    yg = sg_fwd(zsg, lp["lng"], lp["lnb"], lp["sgw"], lp["bt"], f"sg_fwd{tag}")
    x2, x1, y, hn = post_fwd(x, ys, yd, yg, p, lp["wout"], lp["pg"], lp["wgate"], lp["wple"], f"post_fwd{tag}")
    saved = dict(x=x, h=h, zs=zs, zq=zq, zg=zg, zsg=zsg, zab=zab, s5p=s5p, hs=hs, qkv=qkv, loc=loc, ss=ss, x1=x1, y=y, hn=hn, p=p)
    return x2, saved


def _layer_bwd(dx2, sv, lp, nb, tag):
    seq = dx2.shape[0] // nb
    dx1, dgp, dpp, dys, dyd, dyg, dpg = post_bwd(dx2, sv["x1"], sv["hn"], sv["p"], lp["wout"], lp["pg"], lp["wgate"],
                                                 lp["wple"], f"post_bwd{tag}")
    g = {}
    g["w_out"] = wgrad(sv["y"], dx1, f"wgrad_out{tag}")
    g["w_ple_gate"] = wgrad(sv["hn"], dgp, f"wgrad_gate{tag}")
    g["w_ple"] = wgrad(sv["p"], dpp, f"wgrad_ple{tag}")
    g["ple_norm_g"] = dpg[0]
    dzsg, dlng, dlnb, dsgw, dbt = sg_bwd(sv["zsg"], lp["lng"], lp["lnb"], lp["sgw"], lp["bt"], dyg, f"sg_bwd{tag}")
    g["sg_ln_g"], g["sg_ln_b"], g["sg_w"], g["sg_b"] = dlng[0], dlnb[0], dsgw, dbt[:, :SG_HEADS].T
    dloc, dzg, dng = dn_scan_bwd(sv["loc"], sv["zg"], lp["ng"], sv["ss"], dyd, nb, f"dn_scan_bwd{tag}")
    dqkv, dzab, dalog, ddtb = dn_local_bwd(sv["qkv"], sv["zab"], lp["alog"], lp["dtb"], dloc, f"dn_local_bwd{tag}")
    dzq, dconv = dn_pre_bwd(sv["zq"], lp["convw"], dqkv, seq, f"dn_pre_bwd{tag}")
    g["dn_conv_w"], g["dn_a_log"], g["dn_dt_bias"], g["dn_norm_g"] = dconv, dalog[0, :DN_HEADS], ddtb[0, :DN_HEADS], dng[0]
    s5out = s5_bwd(sv["zs"], sv["s5p"], sv["hs"], dys, nb, f"s5_bwd{tag}")
    dzs, dprep, (dcr, dci, ddr, dwglu, dbglu) = s5out[0], s5out[1:7], s5out[7:]
    dare, daim, dls, dbre, dbim = s5_prep_bwd(lp["are"], lp["aim"], lp["ls"], lp["bre"], lp["bim"], dprep,
                                              f"s5_prep_bwd{tag}")
    g["ssm_a_re"] = dare.reshape(SSM_GROUPS, SSM_STATE)
    g["ssm_a_im"] = daim.reshape(SSM_GROUPS, SSM_STATE)
    g["ssm_log_step"] = dls.reshape(SSM_GROUPS, SSM_STATE).sum(axis=1)
    g["ssm_b_re"], g["ssm_b_im"] = _extract_b(dbre), _extract_b(dbim)
    g["ssm_c_re"], g["ssm_c_im"] = _extract_c(dcr), _extract_c(dci)
    g["ssm_d"] = ddr.reshape(SSM_GROUPS, SSM_GROUP)
    g["ssm_w_glu"], g["ssm_b_glu"] = dwglu, dbglu[0]
    dzs_all = (dzs, dzq, dzg, dzsg, dzab)
    dx, dng_in = in_bwd(sv["x"], lp["norm_g"], lp["win"], dzs_all, dx1, f"in_bwd{tag}")
    dwin = jnp.concatenate([wgrad(sv["h"], dz, f"wgrad_in{k}{tag}") for k, dz in enumerate(dzs_all)], axis=1)
    g["w_in"] = _unpermute_cols(dwin)
    g["norm_g"] = dng_in[0]
    return dx, g


def _local_step(x, p, target, w, nb):
    w = dict(w)
    w["w_in_perm"] = _permute_cols(w["w_in"]).astype(_MXU)
    lps = [_layer_params(w, l) for l in range(DEPTH)]
    saved = []
    for l in range(DEPTH):
        x, sv = _layer_fwd(x, p[l], lps[l], nb, f"_l{l}")
        saved.append(sv)
    loss_blk, dx, dfg = loss_fwd_bwd(x, w["final_norm_g"][None], target, "loss")
    grads = [None] * DEPTH
    for l in reversed(range(DEPTH)):
        dx, grads[l] = _layer_bwd(dx, saved[l], lps[l], nb, f"_l{l}")
    out = {k: jnp.stack([grads[l][k] for l in range(DEPTH)]) for k in grads[0]}
    out["final_norm_g"] = dfg[0]
    return loss_blk[0, 0], dx, out


def _split4(a, axis):
    n = a.shape[axis] // N_CHIPS
    shape = a.shape[:axis] + (N_CHIPS, n) + a.shape[axis + 1:]
    return jnp.moveaxis(a.reshape(shape), axis, 0)


def _join4(parts, axis):
    moved = jnp.moveaxis(parts, 0, axis)
    shape = moved.shape[:axis] + (moved.shape[axis] * moved.shape[axis + 1],) + moved.shape[axis + 2:]
    return moved.reshape(shape)


def kernel(x, p, norm_g, w_in, ssm_a_re, ssm_a_im, ssm_b_re, ssm_b_im, ssm_c_re, ssm_c_im, ssm_d, ssm_log_step, ssm_w_glu, ssm_b_glu, dn_conv_w, dn_a_log, dn_dt_bias, dn_norm_g, sg_ln_g, sg_ln_b, sg_w, sg_b, w_out, ple_norm_g, w_ple_gate, w_ple, final_norm_g, loss_target, m_norm_g, m_w_in, m_ssm_a_re, m_ssm_a_im, m_ssm_b_re, m_ssm_b_im, m_ssm_c_re, m_ssm_c_im, m_ssm_d, m_ssm_log_step, m_ssm_w_glu, m_ssm_b_glu, m_dn_conv_w, m_dn_a_log, m_dn_dt_bias, m_dn_norm_g, m_sg_ln_g, m_sg_ln_b, m_sg_w, m_sg_b, m_w_out, m_ple_norm_g, m_w_ple_gate, m_w_ple, m_final_norm_g, v_norm_g, v_w_in, v_ssm_a_re, v_ssm_a_im, v_ssm_b_re, v_ssm_b_im, v_ssm_c_re, v_ssm_c_im, v_ssm_d, v_ssm_log_step, v_ssm_w_glu, v_ssm_b_glu, v_dn_conv_w, v_dn_a_log, v_dn_dt_bias, v_dn_norm_g, v_sg_ln_g, v_sg_ln_b, v_sg_w, v_sg_b, v_w_out, v_ple_norm_g, v_w_ple_gate, v_w_ple, v_final_norm_g):
    args = locals()
    w = {n: args[n] for n in WEIGHTS}
    m = {n: args["m_" + n] for n in WEIGHTS}
    v = {n: args["v_" + n] for n in WEIGHTS}
    nb, seq = x.shape[0], x.shape[1]
    t = nb * seq

    sh_names = [n for n, _ in SHARDED]
    sh_axes = dict(SHARDED)
    sh_shapes = [w[n].shape for n in sh_names]
    wa = _pack_rows([w[n] for n in sh_names], 256, _COMM)
    wb = _pack_rows([w["dn_conv_w"]], 8, f32)
    ga, gb = gather_weights(wa, wb, "gather_weights")
    full = {n: w[n] for n in REPLICATED}
    per_chip = [_unpack_rows(ga[q], sh_shapes) for q in range(N_CHIPS)]
    for k, n in enumerate(sh_names):
        full[n] = _join4(jnp.stack([per_chip[q][k] for q in range(N_CHIPS)]), sh_axes[n])
    full["dn_conv_w"] = _join4(jnp.stack([_unpack_rows(gb[q], [w["dn_conv_w"].shape])[0] for q in range(N_CHIPS)]), 2)

    loss_local, dx, grads = _local_step(x.reshape(t, D_MODEL), p.reshape(DEPTH, t, D_PLE),
                                        loss_target.reshape(t, D_MODEL), full, nb)

    parts = [_split4(grads[n], sh_axes[n]) for n in sh_names]
    send_a = jnp.stack([_pack_rows([pt[q] for pt in parts], 256, _COMM) for q in range(N_CHIPS)])
    send_b = _pack_rows([grads[n] for n in REPLICATED], 256, f32)
    recv_a, recv_b = exchange_grads(send_a, send_b, "exchange_grads")

    def pack(d, names):
        return _pack_rows([d[n] for n in names], 256, f32)

    res_a = sum_adamw(recv_a, pack(w, sh_names), pack(m, sh_names), pack(v, sh_names), "adamw_sharded")
    res_b = sum_adamw(recv_b, pack(w, REPLICATED), pack(m, REPLICATED), pack(v, REPLICATED), "adamw_replicated")
    rep_shapes = [w[n].shape for n in REPLICATED]
    outs = []
    for ra, rb in zip(res_a, res_b):
        d = dict(zip(sh_names, _unpack_rows(ra, sh_shapes)))
        d.update(zip(REPLICATED, _unpack_rows(rb, rep_shapes)))
        outs.append(d)

    loss = lax.psum(loss_local, MESH_AXES)
    return (loss, dx.reshape(nb, seq, D_MODEL), *[outs[0][n] for n in WEIGHTS], *[outs[1][n] for n in WEIGHTS],
            *[outs[2][n] for n in WEIGHTS], *[outs[3][n] for n in WEIGHTS])
```

```python
import functools

import jax
import jax.numpy as jnp
from jax import lax
from jax.experimental import pallas as pl
from jax.experimental.pallas import tpu as pltpu

f32 = jnp.float32
bf16 = jnp.bfloat16

_MXU = bf16
_COMM = bf16
HIGH = lax.Precision.HIGH

D_MODEL = 1024
DEPTH = 2
D_PLE = 256
D_SSM = 256
D_DN = 512
D_SG = 256
SSM_GROUPS = 16
SSM_GROUP = 16
SSM_STATE = 64
N_STATE = SSM_GROUPS * SSM_STATE
DN_HEADS = 4
DN_HEAD_DIM = 128
DN_CONV = 4
DN_CHUNK = 64
SG_HEADS = 4
SG_HEAD_DIM = 64
SG_CHUNK = 128
S5_CHUNK = 128
EPS = 1e-6
D_IN = 3336
D_IN_PAD = 3456
LANE = 128

ADAM_LR = 0.001
ADAM_B1 = 0.9
ADAM_B2 = 0.999
ADAM_EPS = 1e-08
ADAM_WD = 0.01
ADAM_STEP = 10

N_CHIPS = 4
N_DEV = 8
MESH_AXES = ("x", "y", "c")

Z_COLS = ((0, 512), (512, 2048), (2048, 2560), (2560, 3328), (3328, 3456))

SHARDED = (("w_in", 2), ("ssm_w_glu", 1), ("dn_conv_w", 2), ("w_out", 1), ("w_ple_gate", 1), ("w_ple", 2))
REPLICATED = ("norm_g", "ssm_a_re", "ssm_a_im", "ssm_b_re", "ssm_b_im", "ssm_c_re", "ssm_c_im", "ssm_d",
              "ssm_log_step", "ssm_b_glu", "dn_a_log", "dn_dt_bias", "dn_norm_g", "sg_ln_g", "sg_ln_b", "sg_w",
              "sg_b", "ple_norm_g", "final_norm_g")
WEIGHTS = ("norm_g", "w_in", "ssm_a_re", "ssm_a_im", "ssm_b_re", "ssm_b_im", "ssm_c_re", "ssm_c_im", "ssm_d",
           "ssm_log_step", "ssm_w_glu", "ssm_b_glu", "dn_conv_w", "dn_a_log", "dn_dt_bias", "dn_norm_g", "sg_ln_g",
           "sg_ln_b", "sg_w", "sg_b", "w_out", "ple_norm_g", "w_ple_gate", "w_ple", "final_norm_g")

VMEM_BIG = 56 * 1024 * 1024


def _mm(a, b):
    return jnp.dot(a.astype(_MXU), b.astype(_MXU), preferred_element_type=f32)


def _mm_nt(a, b):
    return lax.dot_general(a.astype(_MXU), b.astype(_MXU), (((1,), (1,)), ((), ())), preferred_element_type=f32)


def _mm_tn(a, b):
    return lax.dot_general(a.astype(_MXU), b.astype(_MXU), (((0,), (0,)), ((), ())), preferred_element_type=f32)


@jax.custom_vjp
def bdot(a, b):
    return _mm(a, b)


def _bdot_fwd(a, b):
    return _mm(a, b), (a, b)


def _bdot_bwd(res, g):
    a, b = res
    return _mm_nt(g, b).astype(a.dtype), _mm_tn(a, g).astype(b.dtype)


bdot.defvjp(_bdot_fwd, _bdot_bwd)


@jax.custom_vjp
def bdot_nt(a, b):
    return _mm_nt(a, b)


def _bdot_nt_fwd(a, b):
    return _mm_nt(a, b), (a, b)


def _bdot_nt_bwd(res, g):
    a, b = res
    return _mm(g, b).astype(a.dtype), _mm_tn(g, a).astype(b.dtype)


bdot_nt.defvjp(_bdot_nt_fwd, _bdot_nt_bwd)


@jax.custom_vjp
def bdot_tn(a, b):
    return _mm_tn(a, b)


def _bdot_tn_fwd(a, b):
    return _mm_tn(a, b), (a, b)


def _bdot_tn_bwd(res, g):
    a, b = res
    return _mm_nt(b, g).astype(a.dtype), _mm(a, g).astype(b.dtype)


bdot_tn.defvjp(_bdot_tn_fwd, _bdot_tn_bwd)


def hdot(a, b):
    return jnp.dot(a, b, precision=HIGH, preferred_element_type=f32)


@functools.partial(jax.custom_vjp, nondiff_argnums=(1,))
def roll_rows(x, k):
    return pltpu.roll(x, k, 0)


def _roll_rows_fwd(x, k):
    return pltpu.roll(x, k, 0), None


def _roll_rows_bwd(k, _, g):
    return (pltpu.roll(g, g.shape[0] - k, 0),)


roll_rows.defvjp(_roll_rows_fwd, _roll_rows_bwd)


def _row_ids(shape):
    return lax.broadcasted_iota(jnp.int32, shape, 0)


def _shift_rows(x, d):
    return jnp.where(_row_ids(x.shape) >= d, roll_rows(x, d), 0.0)


def _rowsel(x, i):
    return jnp.sum(jnp.where(_row_ids(x.shape) == i, x, 0.0), axis=0, keepdims=True)


def _rms(x, g):
    return x * lax.rsqrt(jnp.mean(x * x, axis=-1, keepdims=True) + EPS) * g


def _layer_norm(x, g, b):
    mu = jnp.mean(x, axis=-1, keepdims=True)
    xc = x - mu
    return xc * lax.rsqrt(jnp.mean(xc * xc, axis=-1, keepdims=True) + EPS) * g + b


def _s5_prep(are, aim, ls, bre, bim):
    step = jnp.exp(ls)
    mag = jnp.exp(are * step)
    lr = mag * jnp.cos(aim * step)
    li = mag * jnp.sin(aim * step)
    den = are * are + aim * aim
    nr, ni = lr - 1.0, li
    fr = (nr * are + ni * aim) / den
    fi = (ni * are - nr * aim) / den
    bbr = fr * bre - fi * bim
    bbi = fr * bim + fi * bre
    pr = jnp.broadcast_to(lr, (S5_CHUNK, N_STATE))
    pi = jnp.broadcast_to(li, (S5_CHUNK, N_STATE))
    rows8 = _row_ids((8, N_STATE))
    qr = jnp.zeros((8, N_STATE), f32)
    qi = jnp.zeros((8, N_STATE), f32)
    d, k = 1, 0
    while d < S5_CHUNK:
        qr = qr + jnp.where(rows8 == k, _rowsel(pr, d - 1), 0.0)
        qi = qi + jnp.where(rows8 == k, _rowsel(pi, d - 1), 0.0)
        keep = _row_ids(pr.shape) >= d
        sr, si = roll_rows(pr, d), roll_rows(pi, d)
        pr, pi = jnp.where(keep, pr * sr - pi * si, pr), jnp.where(keep, pr * si + pi * sr, pi)
        d, k = 2 * d, k + 1
    return pr, pi, qr, qi, bbr, bbi


def _s5_chunk(u, gate, hr, hi, pr, pi, qr, qi, bbr, bbi, cr, ci, dr, wglu, bglu):
    n = u.shape[0]
    xr = bdot(u, bbr)
    xi = bdot(u, bbi)
    d, k = 1, 0
    while d < n:
        lr, li = _rowsel(qr, k), _rowsel(qi, k)
        sr, si = _shift_rows(xr, d), _shift_rows(xi, d)
        xr, xi = xr + lr * sr - li * si, xi + lr * si + li * sr
        d, k = 2 * d, k + 1
    xr, xi = xr + pr * hr - pi * hi, xi + pr * hi + pi * hr
    y = bdot(xr, cr) - bdot(xi, ci) + dr * u
    y = jax.nn.gelu(y)
    y = y * jax.nn.sigmoid(bdot(y, wglu) + bglu)
    return y * jax.nn.silu(gate), _rowsel(xr, n - 1), _rowsel(xi, n - 1)


def _dn_pre(xc, xp, w0, w1, w2, w3, is_start, col):
    xp = jnp.where(is_start, 0.0, xp)
    rows = _row_ids(xc.shape)
    acc = w3 * xc
    for d, w in ((1, w2), (2, w1), (3, w0)):
        acc = acc + w * jnp.where(rows >= d, roll_rows(xc, d), roll_rows(xp, d))
    y = jax.nn.silu(acc)
    nrm = y * lax.rsqrt(jnp.sum(y * y, axis=-1, keepdims=True) + EPS)
    nrm = nrm * jnp.where(col < DN_HEADS, DN_HEAD_DIM ** -0.5, 1.0)
    return jnp.where(col < 2 * DN_HEADS, nrm, y)


def _dn_local(qs, ks, vs, abs_, alog, dtb):
    c = DN_CHUNK
    ri = lax.broadcasted_iota(jnp.int32, (c, c), 0)
    ci = lax.broadcasted_iota(jnp.int32, (c, c), 1)
    causal, strict = ri >= ci, ri > ci
    eye = (ri == ci).astype(f32)
    tril = causal.astype(f32)
    gcums = [hdot(tril, -jnp.exp(alog) * jax.nn.softplus(ab + dtb)) for ab in abs_]
    gcum_ts = [g.T for g in gcums]
    sigs = [jax.nn.sigmoid(ab) for ab in abs_]
    chains = [(j, h) for j in range(len(abs_)) for h in range(DN_HEADS)]
    gc = [gcums[j][:, h:h + 1] for j, h in chains]
    decay = [jnp.where(causal, jnp.exp(jnp.where(causal, gc[n] - gcum_ts[j][h:h + 1, :], 0.0)), 0.0)
             for n, (j, h) in enumerate(chains)]
    beta = [sigs[j][:, DN_HEADS + h:DN_HEADS + h + 1] for j, h in chains]
    kb = [ks[j][h] * beta[n] for n, (j, h) in enumerate(chains)]
    pw = [-jnp.where(strict, bdot_nt(kb[n], ks[j][h]) * decay[n], 0.0) for n, (j, h) in enumerate(chains)]
    inv = [eye + m for m in pw]
    for _ in range(5):
        pw = [hdot(m, m) for m in pw]
        inv = [a + hdot(a, m) for a, m in zip(inv, pw)]
    egc = [jnp.exp(g) for g in gc]
    values = [hdot(inv[n], vs[j][h] * beta[n]) for n, (j, h) in enumerate(chains)]
    k_cds = [hdot(inv[n], kb[n] * egc[n]) for n in range(len(chains))]
    attns = [bdot_nt(qs[j][h], ks[j][h]) * decay[n] for n, (j, h) in enumerate(chains)]
    q_decs = [qs[j][h] * egc[n] for n, (j, h) in enumerate(chains)]
    k_decs = [ks[j][h] * jnp.exp(gc[n][c - 1:c, :] - gc[n]) for n, (j, h) in enumerate(chains)]

    def nest(flat):
        return [flat[j * DN_HEADS:(j + 1) * DN_HEADS] for j in range(len(abs_))]

    return nest(values), nest(k_cds), nest(attns), nest(q_decs), nest(k_decs), [jnp.exp(g[c - 1:c, :]) for g in gcums]


def _dn_step(values, k_cds, attns, q_decs, k_decs, lasts, ggs, sts, ng):
    v_new = [v - bdot(kc, st) for v, kc, st in zip(values, k_cds, sts)]
    o = [bdot(qd, st) for qd, st in zip(q_decs, sts)]
    o = [a + bdot(at, vn) for a, at, vn in zip(o, attns, v_new)]
    new = [st * la + bdot_tn(kd, vn) for st, la, kd, vn in zip(sts, lasts, k_decs, v_new)]
    return [_rms(a, ng) * jax.nn.silu(g) for a, g in zip(o, ggs)], new


def _sg_chunk(u, v, gate, lng, lnb, ws, bt):
    n = SG_CHUNK
    ug = jax.nn.gelu(u)
    vn = _layer_norm(jax.nn.gelu(v), lng, lnb)
    causal = lax.broadcasted_iota(jnp.int32, (n, n), 0) >= lax.broadcasted_iota(jnp.int32, (n, n), 1)
    lane = lax.broadcasted_iota(jnp.int32, (n, D_SG), 1)
    s = jnp.zeros((n, D_SG), f32)
    for h in range(SG_HEADS):
        t = bdot(jnp.where(causal, ws[h], 0.0), vn) + bt[:, h:h + 1]
        s = s + jnp.where((lane >= h * SG_HEAD_DIM) & (lane < (h + 1) * SG_HEAD_DIM), t, 0.0)
    return ug * s * jax.nn.silu(gate)


def _cp(n_grid, vmem=None):
    return pltpu.CompilerParams(dimension_semantics=("arbitrary",) * n_grid, vmem_limit_bytes=vmem)


def _full(shape):
    nd = len(shape)
    return pl.BlockSpec(tuple(shape), lambda *_: (0,) * nd)


def _rows(tm, ncol):
    return pl.BlockSpec((tm, ncol), lambda i: (i, 0))


def _sds(shape, dtype=f32):
    return jax.ShapeDtypeStruct(tuple(shape), dtype)


def _acc(ref, val, first):
    @pl.when(first)
    def _():
        ref[...] = val

    @pl.when(jnp.logical_not(first))
    def _():
        ref[...] += val


def in_fwd(x, g, w, name):
    t, tm = x.shape[0], 256

    def body(x_ref, g_ref, w_ref, h_ref, *z_refs):
        h = _rms(x_ref[...], g_ref[...]).astype(_MXU)
        h_ref[...] = h
        for z_ref, (a, b) in zip(z_refs, Z_COLS):
            z_ref[...] = jnp.dot(h, w_ref[:, a:b], preferred_element_type=f32)

    widths = [b - a for a, b in Z_COLS]
    return pl.pallas_call(
        body, name=name, grid=(t // tm,),
        in_specs=[_rows(tm, D_MODEL), _full((1, D_MODEL)), _full((D_MODEL, D_IN_PAD))],
        out_specs=[_rows(tm, D_MODEL)] + [_rows(tm, n) for n in widths],
        out_shape=[_sds((t, D_MODEL), _MXU)] + [_sds((t, n)) for n in widths],
        compiler_params=_cp(1, VMEM_BIG),
    )(x, g, w)


def in_bwd(x, g, w, dzs, dres, name):
    t, tm = x.shape[0], 256
    widths = [b - a for a, b in Z_COLS]

    def body(x_ref, g_ref, w_ref, dres_ref, *rest):
        dz_refs, (dx_ref, dg_ref) = rest[:5], rest[5:]
        dh = jnp.zeros((tm, D_MODEL), f32)
        for dz_ref, (a, b) in zip(dz_refs, Z_COLS):
            dh = dh + _mm_nt(dz_ref[...], w_ref[:, a:b])
        _, vj = jax.vjp(_rms, x_ref[...], g_ref[...])
        dx, dg = vj(dh)
        dx_ref[...] = dres_ref[...] + dx
        _acc(dg_ref, dg, pl.program_id(0) == 0)

    return pl.pallas_call(
        body, name=name, grid=(t // tm,),
        in_specs=[_rows(tm, D_MODEL), _full((1, D_MODEL)), _full((D_MODEL, D_IN_PAD)), _rows(tm, D_MODEL)]
        + [_rows(tm, n) for n in widths],
        out_specs=[_rows(tm, D_MODEL), _full((1, D_MODEL))],
        out_shape=[_sds((t, D_MODEL)), _sds((1, D_MODEL))],
        compiler_params=_cp(1, VMEM_BIG),
    )(x, g, w, dres, *dzs)


def wgrad(a, g, name):
    t, k = a.shape
    n = g.shape[1]
    tm = 512
    tn = n if n <= 768 else (768 if n % 768 == 0 else 512)

    def body(a_ref, g_ref, o_ref):
        _acc(o_ref, _mm_tn(a_ref[...], g_ref[...]), pl.program_id(1) == 0)

    return pl.pallas_call(
        body, name=name, grid=(n // tn, t // tm),
        in_specs=[pl.BlockSpec((tm, k), lambda j, i: (i, 0)), pl.BlockSpec((tm, tn), lambda j, i: (i, j))],
        out_specs=pl.BlockSpec((k, tn), lambda j, i: (0, j)),
        out_shape=_sds((k, n)),
        compiler_params=_cp(2, VMEM_BIG),
    )(a, g)


def post_fwd(x, ys, yd, yg, p, wout, pg, wgate, wple, name):
    t, tm = x.shape[0], 256

    def body(x_ref, ys_ref, yd_ref, yg_ref, p_ref, wout_ref, pg_ref, wgate_ref, wple_ref,
             x2_ref, x1_ref, y_ref, hn_ref):
        y = jnp.concatenate([ys_ref[...], yd_ref[...], yg_ref[...]], axis=1).astype(_MXU)
        y_ref[...] = y
        x1 = x_ref[...] + jnp.dot(y, wout_ref[...], preferred_element_type=f32)
        x1_ref[...] = x1
        hn = _rms(x1, pg_ref[...]).astype(_MXU)
        hn_ref[...] = hn
        gp = jnp.dot(hn, wgate_ref[...], preferred_element_type=f32)
        pp = _mm(p_ref[...], wple_ref[...])
        x2_ref[...] = x1 + jax.nn.sigmoid(gp) * pp

    return pl.pallas_call(
        body, name=name, grid=(t // tm,),
        in_specs=[_rows(tm, D_MODEL), _rows(tm, D_SSM), _rows(tm, D_DN), _rows(tm, D_SG), _rows(tm, D_PLE),
                  _full((D_MODEL, D_MODEL)), _full((1, D_MODEL)), _full((D_MODEL, D_MODEL)), _full((D_PLE, D_MODEL))],
        out_specs=[_rows(tm, D_MODEL)] * 4,
        out_shape=[_sds((t, D_MODEL)), _sds((t, D_MODEL)), _sds((t, D_MODEL), _MXU), _sds((t, D_MODEL), _MXU)],
        compiler_params=_cp(1, VMEM_BIG),
    )(x, ys, yd, yg, p, wout, pg, wgate, wple)


def post_bwd(dx2, x1, hn, p, wout, pg, wgate, wple, name):
    t, tm = dx2.shape[0], 256

    def body(dx2_ref, x1_ref, hn_ref, p_ref, wout_ref, pg_ref, wgate_ref, wple_ref,
             dx1_ref, dgp_ref, dpp_ref, dys_ref, dyd_ref, dyg_ref, dpg_ref):
        dx2 = dx2_ref[...]
        gp = jnp.dot(hn_ref[...], wgate_ref[...], preferred_element_type=f32)
        pp = _mm(p_ref[...], wple_ref[...])
        sg = jax.nn.sigmoid(gp)
        dpp_ref[...] = (dx2 * sg).astype(_MXU)
        dgp = (dx2 * pp * sg * (1.0 - sg)).astype(_MXU)
        dgp_ref[...] = dgp
        dhn = _mm_nt(dgp, wgate_ref[...])
        _, vj = jax.vjp(_rms, x1_ref[...], pg_ref[...])
        dx1n, dpg = vj(dhn)
        dx1 = dx2 + dx1n
        dx1_ref[...] = dx1
        dy = _mm_nt(dx1, wout_ref[...])
        dys_ref[...] = dy[:, :D_SSM]
        dyd_ref[...] = dy[:, D_SSM:D_SSM + D_DN]
        dyg_ref[...] = dy[:, D_SSM + D_DN:]
        _acc(dpg_ref, dpg, pl.program_id(0) == 0)

    return pl.pallas_call(
        body, name=name, grid=(t // tm,),
        in_specs=[_rows(tm, D_MODEL), _rows(tm, D_MODEL), _rows(tm, D_MODEL), _rows(tm, D_PLE),
                  _full((D_MODEL, D_MODEL)), _full((1, D_MODEL)), _full((D_MODEL, D_MODEL)), _full((D_PLE, D_MODEL))],
        out_specs=[_rows(tm, D_MODEL), _rows(tm, D_MODEL), _rows(tm, D_MODEL), _rows(tm, D_SSM), _rows(tm, D_DN),
                   _rows(tm, D_SG), _full((1, D_MODEL))],
        out_shape=[_sds((t, D_MODEL)), _sds((t, D_MODEL), _MXU), _sds((t, D_MODEL), _MXU), _sds((t, D_SSM)),
                   _sds((t, D_DN)), _sds((t, D_SG)), _sds((1, D_MODEL))],
        compiler_params=_cp(1, VMEM_BIG),
    )(dx2, x1, hn, p, wout, pg, wgate, wple)


def loss_fwd_bwd(x, fg, target, name):
    t, tm = x.shape[0], 512

    def body(x_ref, fg_ref, t_ref, loss_ref, dx_ref, dfg_ref):
        def f(xv, gv):
            err = _rms(xv, gv) - t_ref[...]
            return 0.5 * jnp.sum(jnp.mean(err * err, axis=-1))

        val, vj = jax.vjp(f, x_ref[...], fg_ref[...])
        dx, dfg = vj(jnp.ones((), f32))
        dx_ref[...] = dx
        first = pl.program_id(0) == 0
        _acc(dfg_ref, dfg, first)
        _acc(loss_ref, jnp.full((8, LANE), val, f32), first)

    return pl.pallas_call(
        body, name=name, grid=(t // tm,),
        in_specs=[_rows(tm, D_MODEL), _full((1, D_MODEL)), _rows(tm, D_MODEL)],
        out_specs=[_full((8, LANE)), _rows(tm, D_MODEL), _full((1, D_MODEL))],
        out_shape=[_sds((8, LANE)), _sds((t, D_MODEL)), _sds((1, D_MODEL))],
        compiler_params=_cp(1),
    )(x, fg, target)


def s5_prep_fwd(are, aim, ls, bre, bim, name):
    def body(are_ref, aim_ref, ls_ref, bre_ref, bim_ref, *outs):
        vals = _s5_prep(are_ref[...], aim_ref[...], ls_ref[...], bre_ref[...], bim_ref[...])
        for o, v in zip(outs, vals):
            o[...] = v

    shapes = [(S5_CHUNK, N_STATE)] * 2 + [(8, N_STATE)] * 2 + [(D_SSM, N_STATE)] * 2
    return pl.pallas_call(body, name=name, out_shape=[_sds(s) for s in shapes])(are, aim, ls, bre, bim)


def s5_prep_bwd(are, aim, ls, bre, bim, cts, name):
    def body(are_ref, aim_ref, ls_ref, bre_ref, bim_ref, *rest):
        ct_refs, outs = rest[:6], rest[6:]
        _, vj = jax.vjp(_s5_prep, are_ref[...], aim_ref[...], ls_ref[...], bre_ref[...], bim_ref[...])
        for o, v in zip(outs, vj(tuple(r[...] for r in ct_refs))):
            o[...] = v

    shapes = [(1, N_STATE)] * 3 + [(D_SSM, N_STATE)] * 2
    return pl.pallas_call(body, name=name, out_shape=[_sds(s) for s in shapes])(are, aim, ls, bre, bim, *cts)


_S5_PARAM_SHAPES = ((S5_CHUNK, N_STATE), (S5_CHUNK, N_STATE), (8, N_STATE), (8, N_STATE), (D_SSM, N_STATE),
                    (D_SSM, N_STATE), (N_STATE, D_SSM), (N_STATE, D_SSM), (1, D_SSM), (D_SSM, D_SSM), (1, D_SSM))


def s5_fwd(z, params, nb, name):
    t = z.shape[0]
    nc = t // nb // S5_CHUNK

    def body(z_ref, *rest):
        p_refs, (y_ref, hs_ref, hr_s, hi_s) = rest[:11], rest[11:]

        @pl.when(pl.program_id(1) == 0)
        def _():
            hr_s[...] = jnp.zeros_like(hr_s)
            hi_s[...] = jnp.zeros_like(hi_s)

        hr, hi = hr_s[...], hi_s[...]
        hs_ref[0, :, :N_STATE] = hr
        hs_ref[0, :, N_STATE:] = hi
        y, nhr, nhi = _s5_chunk(z_ref[:, :D_SSM], z_ref[:, D_SSM:], hr, hi, *[r[...] for r in p_refs])
        y_ref[...] = y
        hr_s[...] = nhr
        hi_s[...] = nhi

    return pl.pallas_call(
        body, name=name, grid=(nb, nc),
        in_specs=[pl.BlockSpec((S5_CHUNK, 2 * D_SSM), lambda b, c: (b * nc + c, 0))]
        + [_full(s) for s in _S5_PARAM_SHAPES],
        out_specs=[pl.BlockSpec((S5_CHUNK, D_SSM), lambda b, c: (b * nc + c, 0)),
                   pl.BlockSpec((1, 1, 2 * N_STATE), lambda b, c: (b * nc + c, 0, 0))],
        out_shape=[_sds((t, D_SSM)), _sds((nb * nc, 1, 2 * N_STATE))],
        scratch_shapes=[pltpu.VMEM((1, N_STATE), f32), pltpu.VMEM((1, N_STATE), f32)],
        compiler_params=_cp(2, VMEM_BIG),
    )(z, *params)


def s5_bwd(z, params, hs, dy, nb, name):
    t = z.shape[0]
    nc = t // nb // S5_CHUNK

    def body(z_ref, hs_ref, dy_ref, *rest):
        p_refs, dz_ref, dp_refs, (dhr_s, dhi_s) = rest[:11], rest[11], rest[12:23], rest[23:]

        @pl.when(pl.program_id(1) == 0)
        def _():
            dhr_s[...] = jnp.zeros_like(dhr_s)
            dhi_s[...] = jnp.zeros_like(dhi_s)

        prim = (z_ref[:, :D_SSM], z_ref[:, D_SSM:], hs_ref[0, :, :N_STATE], hs_ref[0, :, N_STATE:]) + tuple(
            r[...] for r in p_refs)
        _, vj = jax.vjp(_s5_chunk, *prim)
        cts = vj((dy_ref[...], dhr_s[...], dhi_s[...]))
        dz_ref[:, :D_SSM] = cts[0]
        dz_ref[:, D_SSM:] = cts[1]
        dhr_s[...] = cts[2]
        dhi_s[...] = cts[3]
        first = (pl.program_id(0) == 0) & (pl.program_id(1) == 0)
        for r, v in zip(dp_refs, cts[4:]):
            _acc(r, v, first)

    rev = lambda b, c: (b * nc + nc - 1 - c, 0)
    return pl.pallas_call(
        body, name=name, grid=(nb, nc),
        in_specs=[pl.BlockSpec((S5_CHUNK, 2 * D_SSM), rev),
                  pl.BlockSpec((1, 1, 2 * N_STATE), lambda b, c: (b * nc + nc - 1 - c, 0, 0)),
                  pl.BlockSpec((S5_CHUNK, D_SSM), rev)] + [_full(s) for s in _S5_PARAM_SHAPES],
        out_specs=[pl.BlockSpec((S5_CHUNK, 2 * D_SSM), rev)] + [_full(s) for s in _S5_PARAM_SHAPES],
        out_shape=[_sds((t, 2 * D_SSM))] + [_sds(s) for s in _S5_PARAM_SHAPES],
        scratch_shapes=[pltpu.VMEM((1, N_STATE), f32), pltpu.VMEM((1, N_STATE), f32)],
        compiler_params=_cp(2, VMEM_BIG),
    )(z, hs, dy, *params)


DN_PRE_ROWS = 512
DN_COLS = 3 * D_DN // LANE


def dn_pre_fwd(zq, convw, seq, name):
    t, tb = zq.shape[0], DN_PRE_ROWS
    per_seq = seq // tb

    def body(xc_ref, xp_ref, w_ref, o_ref):
        i, j = pl.program_id(1), pl.program_id(0)
        o_ref[...] = _dn_pre(xc_ref[...], xp_ref[...], w_ref[0:1, :], w_ref[1:2, :], w_ref[2:3, :], w_ref[3:4, :],
                             i % per_seq == 0, j)

    return pl.pallas_call(
        body, name=name, grid=(DN_COLS, t // tb),
        in_specs=[pl.BlockSpec((tb, LANE), lambda j, i: (i, j)),
                  pl.BlockSpec((tb, LANE), lambda j, i: (jnp.maximum(i - 1, 0), j)),
                  pl.BlockSpec((DN_CONV, LANE), lambda j, i: (0, j))],
        out_specs=pl.BlockSpec((tb, LANE), lambda j, i: (i, j)),
        out_shape=_sds((t, 3 * D_DN)),
        compiler_params=_cp(2),
    )(zq, zq, convw)


def dn_pre_bwd(zq, convw, dqkv, seq, name):
    t, tb = zq.shape[0], DN_PRE_ROWS
    nrow = t // tb
    per_seq = seq // tb

    def body(xc_ref, xp_ref, w_ref, d_ref, dx_ref, dw_ref, carry):
        j, step = pl.program_id(0), pl.program_id(1)
        i = nrow - 1 - step

        @pl.when(step == 0)
        def _():
            carry[...] = jnp.zeros_like(carry)

        fn = functools.partial(_dn_pre, is_start=i % per_seq == 0, col=j)
        _, vj = jax.vjp(fn, xc_ref[...], xp_ref[...], w_ref[0:1, :], w_ref[1:2, :], w_ref[2:3, :], w_ref[3:4, :])
        dxc, dxp, dw0, dw1, dw2, dw3 = vj(d_ref[...])
        dx_ref[...] = dxc + carry[...]
        carry[...] = dxp
        for k, dw in enumerate((dw0, dw1, dw2, dw3)):
            @pl.when(step == 0)
            def _():
                dw_ref[k:k + 1, :] = dw

            @pl.when(step != 0)
            def _():
                dw_ref[k:k + 1, :] += dw

    return pl.pallas_call(
        body, name=name, grid=(DN_COLS, nrow),
        in_specs=[pl.BlockSpec((tb, LANE), lambda j, s: (nrow - 1 - s, j)),
                  pl.BlockSpec((tb, LANE), lambda j, s: (jnp.maximum(nrow - 2 - s, 0), j)),
                  pl.BlockSpec((DN_CONV, LANE), lambda j, s: (0, j)),
                  pl.BlockSpec((tb, LANE), lambda j, s: (nrow - 1 - s, j))],
        out_specs=[pl.BlockSpec((tb, LANE), lambda j, s: (nrow - 1 - s, j)),
                   pl.BlockSpec((DN_CONV, LANE), lambda j, s: (0, j))],
        out_shape=[_sds((t, 3 * D_DN)), _sds((DN_CONV, 3 * D_DN))],
        scratch_shapes=[pltpu.VMEM((tb, LANE), f32)],
        compiler_params=_cp(2),
    )(zq, zq, convw, dqkv)


DN_LOCAL_CHUNKS = 2
DN_ATTN = DN_HEADS * DN_CHUNK


def _dn_heads(ref, rows, base=0):
    return [ref[rows, base + h * DN_HEAD_DIM:base + (h + 1) * DN_HEAD_DIM] for h in range(DN_HEADS)]


def dn_local_fwd(qkv, ab, alog, dtb, name):
    t = qkv.shape[0]
    c, n = DN_CHUNK, DN_LOCAL_CHUNKS

    def body(qkv_ref, ab_ref, alog_ref, dtb_ref, val_ref, kcd_ref, attn_ref, qd_ref, kd_ref, el_ref):
        rows = [pl.ds(j * c, c) for j in range(n)]
        vals, kcds, attns, qds, kds, els = _dn_local(
            [_dn_heads(qkv_ref, r) for r in rows], [_dn_heads(qkv_ref, r, D_DN) for r in rows],
            [_dn_heads(qkv_ref, r, 2 * D_DN) for r in rows], [ab_ref[r, :] for r in rows], alog_ref[...], dtb_ref[...])
        for j, r in enumerate(rows):
            for h in range(DN_HEADS):
                lo, hi = h * DN_HEAD_DIM, (h + 1) * DN_HEAD_DIM
                val_ref[r, lo:hi] = vals[j][h]
                kcd_ref[r, lo:hi] = kcds[j][h]
                qd_ref[r, lo:hi] = qds[j][h]
                kd_ref[r, lo:hi] = kds[j][h]
                attn_ref[r, h * c:(h + 1) * c] = attns[j][h]
            el_ref[j] = els[j]

    wide = _rows(n * c, D_DN)
    return pl.pallas_call(
        body, name=name, grid=(t // (n * c),),
        in_specs=[_rows(n * c, 3 * D_DN), _rows(n * c, LANE), _full((1, LANE)), _full((1, LANE))],
        out_specs=[wide, wide, _rows(n * c, DN_ATTN), wide, wide, pl.BlockSpec((n, 1, LANE), lambda i: (i, 0, 0))],
        out_shape=[_sds((t, D_DN)), _sds((t, D_DN)), _sds((t, DN_ATTN)), _sds((t, D_DN)), _sds((t, D_DN)),
                   _sds((t // c, 1, LANE))],
        compiler_params=_cp(1),
    )(qkv, ab, alog, dtb)


def dn_local_bwd(qkv, ab, alog, dtb, cts, name):
    t = qkv.shape[0]
    c, n = DN_CHUNK, DN_LOCAL_CHUNKS

    def body(qkv_ref, ab_ref, alog_ref, dtb_ref, dval_ref, dkcd_ref, dattn_ref, dqd_ref, dkd_ref, del_ref,
             dqkv_ref, dab_ref, dalog_ref, ddtb_ref):
        rows = [pl.ds(j * c, c) for j in range(n)]
        _, vj = jax.vjp(_dn_local, [_dn_heads(qkv_ref, r) for r in rows], [_dn_heads(qkv_ref, r, D_DN) for r in rows],
                        [_dn_heads(qkv_ref, r, 2 * D_DN) for r in rows], [ab_ref[r, :] for r in rows], alog_ref[...],
                        dtb_ref[...])
        dattn = [[dattn_ref[r, h * c:(h + 1) * c] for h in range(DN_HEADS)] for r in rows]
        dq, dk, dv, dab, dalog, ddtb = vj(([_dn_heads(dval_ref, r) for r in rows], [_dn_heads(dkcd_ref, r) for r in rows],
                                           dattn, [_dn_heads(dqd_ref, r) for r in rows],
                                           [_dn_heads(dkd_ref, r) for r in rows], [del_ref[j] for j in range(n)]))
        for j, r in enumerate(rows):
            for h in range(DN_HEADS):
                lo, hi = h * DN_HEAD_DIM, (h + 1) * DN_HEAD_DIM
                dqkv_ref[r, lo:hi] = dq[j][h]
                dqkv_ref[r, D_DN + lo:D_DN + hi] = dk[j][h]
                dqkv_ref[r, 2 * D_DN + lo:2 * D_DN + hi] = dv[j][h]
            dab_ref[r, :] = dab[j]
        first = pl.program_id(0) == 0
        _acc(dalog_ref, dalog, first)
        _acc(ddtb_ref, ddtb, first)

    wide = _rows(n * c, D_DN)
    return pl.pallas_call(
        body, name=name, grid=(t // (n * c),),
        in_specs=[_rows(n * c, 3 * D_DN), _rows(n * c, LANE), _full((1, LANE)), _full((1, LANE)),
                  wide, wide, _rows(n * c, DN_ATTN), wide, wide, pl.BlockSpec((n, 1, LANE), lambda i: (i, 0, 0))],
        out_specs=[_rows(n * c, 3 * D_DN), _rows(n * c, LANE), _full((1, LANE)), _full((1, LANE))],
        out_shape=[_sds((t, 3 * D_DN)), _sds((t, LANE)), _sds((1, LANE)), _sds((1, LANE))],
        compiler_params=_cp(1),
    )(qkv, ab, alog, dtb, *cts)


def _seq_view(a, nb):
    return a.reshape((nb, a.shape[0] // nb) + a.shape[1:])


def _dn_chains(nb):
    return [(b, h) for b in range(nb) for h in range(DN_HEADS)]


def _dn_step_operands(val_ref, kcd_ref, attn_ref, qd_ref, kd_ref, el_ref, gg_ref, nb):
    chains = _dn_chains(nb)
    c = DN_CHUNK

    def wide(ref):
        return [ref[b, :, h * DN_HEAD_DIM:(h + 1) * DN_HEAD_DIM] for b, h in chains]

    return (wide(val_ref), wide(kcd_ref), [attn_ref[b, :, h * c:(h + 1) * c] for b, h in chains], wide(qd_ref),
            wide(kd_ref), [el_ref[b, 0, :, h:h + 1] for b, h in chains], wide(gg_ref))


def dn_scan_fwd(loc, gg, ng, nb, name):
    val, kcd, attn, qd, kd, el = loc
    t = val.shape[0]
    c = DN_CHUNK
    nc = t // nb // c
    ns = nb * DN_HEADS

    def body(val_ref, kcd_ref, attn_ref, qd_ref, kd_ref, el_ref, gg_ref, ng_ref, y_ref, ss_ref, st):
        @pl.when(pl.program_id(0) == 0)
        def _():
            st[...] = jnp.zeros_like(st)

        sts = [st[i] for i in range(ns)]
        for i in range(ns):
            ss_ref[0, i] = sts[i]
        ys, new = _dn_step(*_dn_step_operands(val_ref, kcd_ref, attn_ref, qd_ref, kd_ref, el_ref, gg_ref, nb), sts,
                           ng_ref[...])
        for i, (b, h) in enumerate(_dn_chains(nb)):
            y_ref[b, :, h * DN_HEAD_DIM:(h + 1) * DN_HEAD_DIM] = ys[i]
            st[i] = new[i]

    def blk(w):
        return pl.BlockSpec((nb, c, w), lambda k: (0, k, 0))

    el_spec = pl.BlockSpec((nb, 1, 1, LANE), lambda k: (0, k, 0, 0))
    y, ss = pl.pallas_call(
        body, name=name, grid=(nc,),
        in_specs=[blk(D_DN), blk(D_DN), blk(DN_ATTN), blk(D_DN), blk(D_DN), el_spec, blk(D_DN), _full((1, LANE))],
        out_specs=[blk(D_DN), pl.BlockSpec((1, ns, DN_HEAD_DIM, DN_HEAD_DIM), lambda k: (k, 0, 0, 0))],
        out_shape=[_sds((nb, t // nb, D_DN)), _sds((nc, ns, DN_HEAD_DIM, DN_HEAD_DIM))],
        scratch_shapes=[pltpu.VMEM((ns, DN_HEAD_DIM, DN_HEAD_DIM), f32)],
        compiler_params=_cp(1),
    )(_seq_view(val, nb), _seq_view(kcd, nb), _seq_view(attn, nb), _seq_view(qd, nb), _seq_view(kd, nb),
      el.reshape(nb, nc, 1, LANE), _seq_view(gg, nb), ng)
    return y.reshape(t, D_DN), ss


def dn_scan_bwd(loc, gg, ng, ss, dy, nb, name):
    val, kcd, attn, qd, kd, el = loc
    t = val.shape[0]
    c = DN_CHUNK
    nc = t // nb // c
    ns = nb * DN_HEADS

    def body(val_ref, kcd_ref, attn_ref, qd_ref, kd_ref, el_ref, gg_ref, ng_ref, ss_ref, dy_ref,
             dval_ref, dkcd_ref, dattn_ref, dqd_ref, dkd_ref, del_ref, dgg_ref, dng_ref, dst):
        @pl.when(pl.program_id(0) == 0)
        def _():
            dst[...] = jnp.zeros_like(dst)

        lane = lax.broadcasted_iota(jnp.int32, (1, LANE), 1)
        chains = _dn_chains(nb)
        _, vj = jax.vjp(_dn_step, *_dn_step_operands(val_ref, kcd_ref, attn_ref, qd_ref, kd_ref, el_ref, gg_ref, nb),
                        [ss_ref[0, i] for i in range(ns)], ng_ref[...])
        dys = [dy_ref[b, :, h * DN_HEAD_DIM:(h + 1) * DN_HEAD_DIM] for b, h in chains]
        dval, dkcd, dattn, dqd, dkd, dlast, dgg, ds, dng = vj((dys, [dst[i] for i in range(ns)]))
        del_rows = [jnp.zeros((1, LANE), f32) for _ in range(nb)]
        for i, (b, h) in enumerate(chains):
            cols = slice(h * DN_HEAD_DIM, (h + 1) * DN_HEAD_DIM)
            dval_ref[b, :, cols] = dval[i]
            dkcd_ref[b, :, cols] = dkcd[i]
            dattn_ref[b, :, h * c:(h + 1) * c] = dattn[i]
            dqd_ref[b, :, cols] = dqd[i]
            dkd_ref[b, :, cols] = dkd[i]
            dgg_ref[b, :, cols] = dgg[i]
            dst[i] = ds[i]
            del_rows[b] = del_rows[b] + jnp.where(lane == h, dlast[i], 0.0)
        for b in range(nb):
            del_ref[b, 0] = del_rows[b]
        _acc(dng_ref, dng, pl.program_id(0) == 0)

    def blk(w):
        return pl.BlockSpec((nb, c, w), lambda k: (0, nc - 1 - k, 0))

    el_spec = pl.BlockSpec((nb, 1, 1, LANE), lambda k: (0, nc - 1 - k, 0, 0))
    outs = pl.pallas_call(
        body, name=name, grid=(nc,),
        in_specs=[blk(D_DN), blk(D_DN), blk(DN_ATTN), blk(D_DN), blk(D_DN), el_spec, blk(D_DN), _full((1, LANE)),
                  pl.BlockSpec((1, ns, DN_HEAD_DIM, DN_HEAD_DIM), lambda k: (nc - 1 - k, 0, 0, 0)), blk(D_DN)],
        out_specs=[blk(D_DN), blk(D_DN), blk(DN_ATTN), blk(D_DN), blk(D_DN), el_spec, blk(D_DN), _full((1, LANE))],
        out_shape=[_sds((nb, t // nb, D_DN)), _sds((nb, t // nb, D_DN)), _sds((nb, t // nb, DN_ATTN)),
                   _sds((nb, t // nb, D_DN)), _sds((nb, t // nb, D_DN)), _sds((nb, nc, 1, LANE)),
                   _sds((nb, t // nb, D_DN)), _sds((1, LANE))],
        scratch_shapes=[pltpu.VMEM((ns, DN_HEAD_DIM, DN_HEAD_DIM), f32)],
        compiler_params=_cp(1),
    )(_seq_view(val, nb), _seq_view(kcd, nb), _seq_view(attn, nb), _seq_view(qd, nb), _seq_view(kd, nb),
      el.reshape(nb, nc, 1, LANE), _seq_view(gg, nb), ng, ss, _seq_view(dy, nb))
    dloc = [o.reshape((t,) + o.shape[2:]) for o in outs[:5]] + [outs[5].reshape(t // c, 1, LANE)]
    return dloc, outs[6].reshape(t, D_DN), outs[7]


SG_ROWS = 512


def sg_fwd(z, lng, lnb, w, bt, name):
    t = z.shape[0]

    def body(z_ref, lng_ref, lnb_ref, w_ref, bt_ref, y_ref):
        ws = [w_ref[h] for h in range(SG_HEADS)]
        for k in range(SG_ROWS // SG_CHUNK):
            r = pl.ds(k * SG_CHUNK, SG_CHUNK)
            y_ref[r, :] = _sg_chunk(z_ref[r, :D_SG], z_ref[r, D_SG:2 * D_SG], z_ref[r, 2 * D_SG:], lng_ref[...],
                                    lnb_ref[...], ws, bt_ref[...])

    return pl.pallas_call(
        body, name=name, grid=(t // SG_ROWS,),
        in_specs=[_rows(SG_ROWS, 3 * D_SG), _full((1, D_SG)), _full((1, D_SG)),
                  _full((SG_HEADS, SG_CHUNK, SG_CHUNK)), _full((SG_CHUNK, LANE))],
        out_specs=_rows(SG_ROWS, D_SG),
        out_shape=_sds((t, D_SG)),
        compiler_params=_cp(1),
    )(z, lng, lnb, w, bt)


def sg_bwd(z, lng, lnb, w, bt, dy, name):
    t = z.shape[0]

    def body(z_ref, lng_ref, lnb_ref, w_ref, bt_ref, dy_ref, dz_ref, dlng_ref, dlnb_ref, dw_ref, dbt_ref):
        ws = [w_ref[h] for h in range(SG_HEADS)]
        tot = None
        for k in range(SG_ROWS // SG_CHUNK):
            r = pl.ds(k * SG_CHUNK, SG_CHUNK)
            _, vj = jax.vjp(_sg_chunk, z_ref[r, :D_SG], z_ref[r, D_SG:2 * D_SG], z_ref[r, 2 * D_SG:], lng_ref[...],
                            lnb_ref[...], ws, bt_ref[...])
            du, dv, dgate, dlng, dlnb, dws, dbt = vj(dy_ref[r, :])
            dz_ref[r, :D_SG] = du
            dz_ref[r, D_SG:2 * D_SG] = dv
            dz_ref[r, 2 * D_SG:] = dgate
            part = [dlng, dlnb, dbt] + list(dws)
            tot = part if tot is None else [a + b for a, b in zip(tot, part)]
        first = pl.program_id(0) == 0
        _acc(dlng_ref, tot[0], first)
        _acc(dlnb_ref, tot[1], first)
        _acc(dbt_ref, tot[2], first)
        for h in range(SG_HEADS):
            @pl.when(first)
            def _():
                dw_ref[h] = tot[3 + h]

            @pl.when(jnp.logical_not(first))
            def _():
                dw_ref[h] += tot[3 + h]

    return pl.pallas_call(
        body, name=name, grid=(t // SG_ROWS,),
        in_specs=[_rows(SG_ROWS, 3 * D_SG), _full((1, D_SG)), _full((1, D_SG)),
                  _full((SG_HEADS, SG_CHUNK, SG_CHUNK)), _full((SG_CHUNK, LANE)), _rows(SG_ROWS, D_SG)],
        out_specs=[_rows(SG_ROWS, 3 * D_SG), _full((1, D_SG)), _full((1, D_SG)),
                   _full((SG_HEADS, SG_CHUNK, SG_CHUNK)), _full((SG_CHUNK, LANE))],
        out_shape=[_sds((t, 3 * D_SG)), _sds((1, D_SG)), _sds((1, D_SG)), _sds((SG_HEADS, SG_CHUNK, SG_CHUNK)),
                   _sds((SG_CHUNK, LANE))],
        compiler_params=_cp(1),
    )(z, lng, lnb, w, bt, dy)


def sum_adamw(recv, w, m, v, name):
    r = w.shape[0]
    tr = 256

    def body(recv_ref, w_ref, m_ref, v_ref, g_ref, d_ref, nm_ref, nv_ref):
        g = recv_ref[0].astype(f32)
        for k in range(1, N_DEV):
            g = g + recv_ref[k].astype(f32)
        wv = w_ref[...]
        nm = ADAM_B1 * m_ref[...] + (1.0 - ADAM_B1) * g
        nv = ADAM_B2 * v_ref[...] + (1.0 - ADAM_B2) * jnp.square(g)
        m_hat = nm / (1.0 - ADAM_B1 ** ADAM_STEP)
        v_hat = nv / (1.0 - ADAM_B2 ** ADAM_STEP)
        g_ref[...] = g
        d_ref[...] = -ADAM_LR * (m_hat / (jnp.sqrt(v_hat) + ADAM_EPS) + ADAM_WD * wv)
        nm_ref[...] = nm
        nv_ref[...] = nv

    return pl.pallas_call(
        body, name=name, grid=(r // tr,),
        in_specs=[pl.BlockSpec((N_DEV, tr, LANE), lambda i: (0, i, 0))] + [_rows(tr, LANE)] * 3,
        out_specs=[_rows(tr, LANE)] * 4,
        out_shape=[_sds((r, LANE))] * 4,
        compiler_params=_cp(1),
    )(recv, w, m, v)


_ANY = pl.BlockSpec(memory_space=pl.ANY)
_MESH = pl.DeviceIdType.MESH


def _flip(v, bit):
    return 1 - v if bit else v


def gather_weights(wa, wb, name):
    rels = ((1, 0), (0, 1), (1, 1))

    def body(a_ref, b_ref, oa_ref, ob_ref, send_sems, recv_sems, local_sems):
        x, y, c = lax.axis_index("x"), lax.axis_index("y"), lax.axis_index("c")
        mine = 2 * x + y
        local = [pltpu.make_async_copy(a_ref, oa_ref.at[mine], local_sems.at[0]),
                 pltpu.make_async_copy(b_ref, ob_ref.at[mine], local_sems.at[1])]
        for cp in local:
            cp.start()
        sends, recvs = [], []
        for r, (fx, fy) in enumerate(rels):
            px, py = _flip(x, fx), _flip(y, fy)
            for n, (src, dst) in enumerate(((a_ref, oa_ref), (b_ref, ob_ref))):
                k = 2 * r + n
                sends.append(pltpu.make_async_remote_copy(
                    src_ref=src, dst_ref=dst.at[mine], send_sem=send_sems.at[k], recv_sem=recv_sems.at[k],
                    device_id=(px, py, c), device_id_type=_MESH))
                recvs.append(pltpu.make_async_remote_copy(
                    src_ref=src, dst_ref=dst.at[2 * px + py], send_sem=send_sems.at[k], recv_sem=recv_sems.at[k],
                    device_id=(px, py, c), device_id_type=_MESH))
        for cp in sends:
            cp.start()
        for cp in recvs:
            cp.wait_recv()
        for cp in sends:
            cp.wait_send()
        for cp in local:
            cp.wait()

    return pl.pallas_call(
        body, name=name, in_specs=[_ANY, _ANY], out_specs=[_ANY, _ANY],
        out_shape=[_sds((N_CHIPS,) + wa.shape, wa.dtype), _sds((N_CHIPS,) + wb.shape, wb.dtype)],
        scratch_shapes=[pltpu.SemaphoreType.DMA((6,)), pltpu.SemaphoreType.DMA((6,)), pltpu.SemaphoreType.DMA((2,))],
    )(wa, wb)


def exchange_grads(ga, gb, name):
    rels = [(fx, fy, fc) for fx in (0, 1) for fy in (0, 1) for fc in (0, 1)][1:]

    def body(a_ref, b_ref, oa_ref, ob_ref, send_sems, recv_sems, local_sems):
        x, y, c = lax.axis_index("x"), lax.axis_index("y"), lax.axis_index("c")
        me = 4 * x + 2 * y + c
        local = [pltpu.make_async_copy(a_ref.at[2 * x + y], oa_ref.at[me], local_sems.at[0]),
                 pltpu.make_async_copy(b_ref, ob_ref.at[me], local_sems.at[1])]
        for cp in local:
            cp.start()
        sends, recvs = [], []
        for r, (fx, fy, fc) in enumerate(rels):
            px, py, pc = _flip(x, fx), _flip(y, fy), _flip(c, fc)
            peer = 4 * px + 2 * py + pc
            for n, (src, dst) in enumerate(((a_ref.at[2 * px + py], oa_ref), (b_ref, ob_ref))):
                k = 2 * r + n
                sends.append(pltpu.make_async_remote_copy(
                    src_ref=src, dst_ref=dst.at[me], send_sem=send_sems.at[k], recv_sem=recv_sems.at[k],
                    device_id=(px, py, pc), device_id_type=_MESH))
                recvs.append(pltpu.make_async_remote_copy(
                    src_ref=src, dst_ref=dst.at[peer], send_sem=send_sems.at[k], recv_sem=recv_sems.at[k],
                    device_id=(px, py, pc), device_id_type=_MESH))
        for cp in sends:
            cp.start()
        for cp in recvs:
            cp.wait_recv()
        for cp in sends:
            cp.wait_send()
        for cp in local:
            cp.wait()

    return pl.pallas_call(
        body, name=name, in_specs=[_ANY, _ANY], out_specs=[_ANY, _ANY],
        out_shape=[_sds((N_DEV,) + ga.shape[1:], ga.dtype), _sds((N_DEV,) + gb.shape, gb.dtype)],
        scratch_shapes=[pltpu.SemaphoreType.DMA((14,)), pltpu.SemaphoreType.DMA((14,)), pltpu.SemaphoreType.DMA((2,))],
    )(ga, gb)


def _pack_rows(parts, mult, dtype):
    blocks = []
    for a in parts:
        flat = a.reshape(-1).astype(dtype)
        rows = -(-flat.shape[0] // LANE)
        blocks.append(jnp.pad(flat, (0, rows * LANE - flat.shape[0])).reshape(rows, LANE))
    buf = jnp.concatenate(blocks, axis=0)
    return jnp.pad(buf, ((0, -buf.shape[0] % mult), (0, 0)))


def _unpack_rows(buf, shapes):
    out, row = [], 0
    for s in shapes:
        n = 1
        for d in s:
            n *= d
        rows = -(-n // LANE)
        out.append(buf[row:row + rows].reshape(-1)[:n].reshape(s))
        row += rows
    return out


def _permute_cols(w):
    pad = jnp.zeros(w.shape[:-1] + (D_IN_PAD - D_IN,), w.dtype)
    return jnp.concatenate([w[..., :2048], w[..., 2056:], w[..., 2048:2056], pad], axis=-1)


def _unpermute_cols(w):
    return jnp.concatenate([w[..., :2048], w[..., 3328:3336], w[..., 2048:3328]], axis=-1)


def _expand_b(b):
    eye = jnp.eye(SSM_GROUPS, dtype=b.dtype)
    return jnp.einsum("gnc,gh->gchn", b, eye).reshape(D_SSM, N_STATE)


def _extract_b(e):
    return jnp.einsum("gcgn->gnc", e.reshape(SSM_GROUPS, SSM_GROUP, SSM_GROUPS, SSM_STATE))


def _expand_c(c):
    eye = jnp.eye(SSM_GROUPS, dtype=c.dtype)
    return jnp.einsum("gcn,gh->gnhc", c, eye).reshape(N_STATE, D_SSM)


def _extract_c(e):
    return jnp.einsum("gngc->gcn", e.reshape(SSM_GROUPS, SSM_STATE, SSM_GROUPS, SSM_GROUP))


def _lane_row(v):
    return jnp.pad(v, (0, LANE - v.shape[0])).reshape(1, LANE)


def _layer_params(w, l):
    return dict(
        norm_g=w["norm_g"][l][None], win=w["w_in_perm"][l], wout=w["w_out"][l].astype(_MXU),
        pg=w["ple_norm_g"][l][None], wgate=w["w_ple_gate"][l].astype(_MXU), wple=w["w_ple"][l].astype(_MXU),
        are=w["ssm_a_re"][l].reshape(1, N_STATE), aim=w["ssm_a_im"][l].reshape(1, N_STATE),
        ls=jnp.repeat(w["ssm_log_step"][l], SSM_STATE).reshape(1, N_STATE),
        bre=_expand_b(w["ssm_b_re"][l]), bim=_expand_b(w["ssm_b_im"][l]),
        cr=_expand_c(w["ssm_c_re"][l]), ci=_expand_c(w["ssm_c_im"][l]),
        dr=w["ssm_d"][l].reshape(1, D_SSM), wglu=w["ssm_w_glu"][l].astype(f32), bglu=w["ssm_b_glu"][l][None],
        convw=w["dn_conv_w"][l], alog=_lane_row(w["dn_a_log"][l]), dtb=_lane_row(w["dn_dt_bias"][l]),
        ng=w["dn_norm_g"][l][None],
        lng=w["sg_ln_g"][l][None], lnb=w["sg_ln_b"][l][None], sgw=w["sg_w"][l],
        bt=jnp.pad(w["sg_b"][l].T, ((0, 0), (0, LANE - SG_HEADS))),
    )


def _layer_fwd(x, p, lp, nb, tag):
    seq = x.shape[0] // nb
    h, zs, zq, zg, zsg, zab = in_fwd(x, lp["norm_g"], lp["win"], f"in_fwd{tag}")
    prep = s5_prep_fwd(lp["are"], lp["aim"], lp["ls"], lp["bre"], lp["bim"], f"s5_prep_fwd{tag}")
    s5p = tuple(prep) + (lp["cr"], lp["ci"], lp["dr"], lp["wglu"], lp["bglu"])
    ys, hs = s5_fwd(zs, s5p, nb, f"s5_fwd{tag}")
    qkv = dn_pre_fwd(zq, lp["convw"], seq, f"dn_pre_fwd{tag}")
    loc = dn_local_fwd(qkv, zab, lp["alog"], lp["dtb"], f"dn_local_fwd{tag}")
    yd, ss = dn_scan_fwd(loc, zg, lp["ng"], nb, f"dn_scan_fwd{tag}")
    yg = sg_fwd(zsg, lp["lng"], lp["lnb"], lp["sgw"], lp["bt"], f"sg_fwd{tag}")
    x2, x1, y, hn = post_fwd(x, ys, yd, yg, p, lp["wout"], lp["pg"], lp["wgate"], lp["wple"], f"post_fwd{tag}")
    saved = dict(x=x, h=h, zs=zs, zq=zq, zg=zg, zsg=zsg, zab=zab, s5p=s5p, hs=hs, qkv=qkv, loc=loc, ss=ss, x1=x1, y=y, hn=hn, p=p)
    return x2, saved


def _layer_bwd(dx2, sv, lp, nb, tag):
    seq = dx2.shape[0] // nb
    dx1, dgp, dpp, dys, dyd, dyg, dpg = post_bwd(dx2, sv["x1"], sv["hn"], sv["p"], lp["wout"], lp["pg"], lp["wgate"],
                                                 lp["wple"], f"post_bwd{tag}")
    g = {}
    g["w_out"] = wgrad(sv["y"], dx1, f"wgrad_out{tag}")
    g["w_ple_gate"] = wgrad(sv["hn"], dgp, f"wgrad_gate{tag}")
    g["w_ple"] = wgrad(sv["p"], dpp, f"wgrad_ple{tag}")
    g["ple_norm_g"] = dpg[0]
    dzsg, dlng, dlnb, dsgw, dbt = sg_bwd(sv["zsg"], lp["lng"], lp["lnb"], lp["sgw"], lp["bt"], dyg, f"sg_bwd{tag}")
    g["sg_ln_g"], g["sg_ln_b"], g["sg_w"], g["sg_b"] = dlng[0], dlnb[0], dsgw, dbt[:, :SG_HEADS].T
    dloc, dzg, dng = dn_scan_bwd(sv["loc"], sv["zg"], lp["ng"], sv["ss"], dyd, nb, f"dn_scan_bwd{tag}")
    dqkv, dzab, dalog, ddtb = dn_local_bwd(sv["qkv"], sv["zab"], lp["alog"], lp["dtb"], dloc, f"dn_local_bwd{tag}")
    dzq, dconv = dn_pre_bwd(sv["zq"], lp["convw"], dqkv, seq, f"dn_pre_bwd{tag}")
    g["dn_conv_w"], g["dn_a_log"], g["dn_dt_bias"], g["dn_norm_g"] = dconv, dalog[0, :DN_HEADS], ddtb[0, :DN_HEADS], dng[0]
    s5out = s5_bwd(sv["zs"], sv["s5p"], sv["hs"], dys, nb, f"s5_bwd{tag}")
    dzs, dprep, (dcr, dci, ddr, dwglu, dbglu) = s5out[0], s5out[1:7], s5out[7:]
    dare, daim, dls, dbre, dbim = s5_prep_bwd(lp["are"], lp["aim"], lp["ls"], lp["bre"], lp["bim"], dprep,
                                              f"s5_prep_bwd{tag}")
    g["ssm_a_re"] = dare.reshape(SSM_GROUPS, SSM_STATE)
    g["ssm_a_im"] = daim.reshape(SSM_GROUPS, SSM_STATE)
    g["ssm_log_step"] = dls.reshape(SSM_GROUPS, SSM_STATE).sum(axis=1)
    g["ssm_b_re"], g["ssm_b_im"] = _extract_b(dbre), _extract_b(dbim)
    g["ssm_c_re"], g["ssm_c_im"] = _extract_c(dcr), _extract_c(dci)
    g["ssm_d"] = ddr.reshape(SSM_GROUPS, SSM_GROUP)
    g["ssm_w_glu"], g["ssm_b_glu"] = dwglu, dbglu[0]
    dzs_all = (dzs, dzq, dzg, dzsg, dzab)
    dx, dng_in = in_bwd(sv["x"], lp["norm_g"], lp["win"], dzs_all, dx1, f"in_bwd{tag}")
    dwin = jnp.concatenate([wgrad(sv["h"], dz, f"wgrad_in{k}{tag}") for k, dz in enumerate(dzs_all)], axis=1)
    g["w_in"] = _unpermute_cols(dwin)
    g["norm_g"] = dng_in[0]
    return dx, g


def _local_step(x, p, target, w, nb):
    w = dict(w)
    w["w_in_perm"] = _permute_cols(w["w_in"]).astype(_MXU)
    lps = [_layer_params(w, l) for l in range(DEPTH)]
    saved = []
    for l in range(DEPTH):
        x, sv = _layer_fwd(x, p[l], lps[l], nb, f"_l{l}")
        saved.append(sv)
    loss_blk, dx, dfg = loss_fwd_bwd(x, w["final_norm_g"][None], target, "loss")
    grads = [None] * DEPTH
    for l in reversed(range(DEPTH)):
        dx, grads[l] = _layer_bwd(dx, saved[l], lps[l], nb, f"_l{l}")
    out = {k: jnp.stack([grads[l][k] for l in range(DEPTH)]) for k in grads[0]}
    out["final_norm_g"] = dfg[0]
    return loss_blk[0, 0], dx, out


def _split4(a, axis):
    n = a.shape[axis] // N_CHIPS
    shape = a.shape[:axis] + (N_CHIPS, n) + a.shape[axis + 1:]
    return jnp.moveaxis(a.reshape(shape), axis, 0)


def _join4(parts, axis):
    moved = jnp.moveaxis(parts, 0, axis)
    shape = moved.shape[:axis] + (moved.shape[axis] * moved.shape[axis + 1],) + moved.shape[axis + 2:]
    return moved.reshape(shape)


def kernel(x, p, norm_g, w_in, ssm_a_re, ssm_a_im, ssm_b_re, ssm_b_im, ssm_c_re, ssm_c_im, ssm_d, ssm_log_step, ssm_w_glu, ssm_b_glu, dn_conv_w, dn_a_log, dn_dt_bias, dn_norm_g, sg_ln_g, sg_ln_b, sg_w, sg_b, w_out, ple_norm_g, w_ple_gate, w_ple, final_norm_g, loss_target, m_norm_g, m_w_in, m_ssm_a_re, m_ssm_a_im, m_ssm_b_re, m_ssm_b_im, m_ssm_c_re, m_ssm_c_im, m_ssm_d, m_ssm_log_step, m_ssm_w_glu, m_ssm_b_glu, m_dn_conv_w, m_dn_a_log, m_dn_dt_bias, m_dn_norm_g, m_sg_ln_g, m_sg_ln_b, m_sg_w, m_sg_b, m_w_out, m_ple_norm_g, m_w_ple_gate, m_w_ple, m_final_norm_g, v_norm_g, v_w_in, v_ssm_a_re, v_ssm_a_im, v_ssm_b_re, v_ssm_b_im, v_ssm_c_re, v_ssm_c_im, v_ssm_d, v_ssm_log_step, v_ssm_w_glu, v_ssm_b_glu, v_dn_conv_w, v_dn_a_log, v_dn_dt_bias, v_dn_norm_g, v_sg_ln_g, v_sg_ln_b, v_sg_w, v_sg_b, v_w_out, v_ple_norm_g, v_w_ple_gate, v_w_ple, v_final_norm_g):
    args = locals()
    w = {n: args[n] for n in WEIGHTS}
    m = {n: args["m_" + n] for n in WEIGHTS}
    v = {n: args["v_" + n] for n in WEIGHTS}
    nb, seq = x.shape[0], x.shape[1]
    t = nb * seq

    sh_names = [n for n, _ in SHARDED]
    sh_axes = dict(SHARDED)
    sh_shapes = [w[n].shape for n in sh_names]
    wa = _pack_rows([w[n] for n in sh_names], 256, _COMM)
    wb = _pack_rows([w["dn_conv_w"]], 8, f32)
    ga, gb = gather_weights(wa, wb, "gather_weights")
    full = {n: w[n] for n in REPLICATED}
    per_chip = [_unpack_rows(ga[q], sh_shapes) for q in range(N_CHIPS)]
    for k, n in enumerate(sh_names):
        full[n] = _join4(jnp.stack([per_chip[q][k] for q in range(N_CHIPS)]), sh_axes[n])
    full["dn_conv_w"] = _join4(jnp.stack([_unpack_rows(gb[q], [w["dn_conv_w"].shape])[0] for q in range(N_CHIPS)]), 2)

    loss_local, dx, grads = _local_step(x.reshape(t, D_MODEL), p.reshape(DEPTH, t, D_PLE),
                                        loss_target.reshape(t, D_MODEL), full, nb)

    parts = [_split4(grads[n], sh_axes[n]) for n in sh_names]
    send_a = jnp.stack([_pack_rows([pt[q] for pt in parts], 256, _COMM) for q in range(N_CHIPS)])
    send_b = _pack_rows([grads[n] for n in REPLICATED], 256, f32)
    recv_a, recv_b = exchange_grads(send_a, send_b, "exchange_grads")

    def pack(d, names):
        return _pack_rows([d[n] for n in names], 256, f32)

    res_a = sum_adamw(recv_a, pack(w, sh_names), pack(m, sh_names), pack(v, sh_names), "adamw_sharded")
    res_b = sum_adamw(recv_b, pack(w, REPLICATED), pack(m, REPLICATED), pack(v, REPLICATED), "adamw_replicated")
    rep_shapes = [w[n].shape for n in REPLICATED]
    outs = []
    for ra, rb in zip(res_a, res_b):
        d = dict(zip(sh_names, _unpack_rows(ra, sh_shapes)))
        d.update(zip(REPLICATED, _unpack_rows(rb, rep_shapes)))
        outs.append(d)

    loss = lax.psum(loss_local, MESH_AXES)
    return (loss, dx.reshape(nb, seq, D_MODEL), *[outs[0][n] for n in WEIGHTS], *[outs[1][n] for n in WEIGHTS],
            *[outs[2][n] for n in WEIGHTS], *[outs[3][n] for n in WEIGHTS])
```

```python
import functools

import jax
import jax.numpy as jnp
from jax import lax
from jax.experimental import pallas as pl
from jax.experimental.pallas import tpu as pltpu

f32 = jnp.float32
bf16 = jnp.bfloat16

_MXU = bf16
_COMM = bf16
HIGH = lax.Precision.HIGH

D_MODEL = 1024
DEPTH = 2
D_PLE = 256
D_SSM = 256
D_DN = 512
D_SG = 256
SSM_GROUPS = 16
SSM_GROUP = 16
SSM_STATE = 64
N_STATE = SSM_GROUPS * SSM_STATE
DN_HEADS = 4
DN_HEAD_DIM = 128
DN_CONV = 4
DN_CHUNK = 64
SG_HEADS = 4
SG_HEAD_DIM = 64
SG_CHUNK = 128
S5_CHUNK = 128
EPS = 1e-6
D_IN = 3336
D_IN_PAD = 3456
LANE = 128

ADAM_LR = 0.001
ADAM_B1 = 0.9
ADAM_B2 = 0.999
ADAM_EPS = 1e-08
ADAM_WD = 0.01
ADAM_STEP = 10

N_CHIPS = 4
N_DEV = 8
MESH_AXES = ("x", "y", "c")

Z_COLS = ((0, 512), (512, 2048), (2048, 2560), (2560, 3328), (3328, 3456))

GROUP_COLS = ((0, 512), (512, 2048), (2056, 2568), (2568, 3336), (2048, 2056))
SHARD_COLS = D_IN // 4

SHARDED = (("w_in", ("slot",)), ("ssm_w_glu", ("win", 0, 64)), ("dn_conv_w", ("win", 1, 384)),
           ("w_out", ("win", 0, 256)), ("w_ple_gate", ("win", 0, 256)), ("w_ple", ("win", 1, 256)))
REPLICATED = ("norm_g", "ssm_a_re", "ssm_a_im", "ssm_b_re", "ssm_b_im", "ssm_c_re", "ssm_c_im", "ssm_d",
              "ssm_log_step", "ssm_b_glu", "dn_a_log", "dn_dt_bias", "dn_norm_g", "sg_ln_g", "sg_ln_b", "sg_w",
              "sg_b", "ple_norm_g", "final_norm_g")
WEIGHTS = ("norm_g", "w_in", "ssm_a_re", "ssm_a_im", "ssm_b_re", "ssm_b_im", "ssm_c_re", "ssm_c_im", "ssm_d",
           "ssm_log_step", "ssm_w_glu", "ssm_b_glu", "dn_conv_w", "dn_a_log", "dn_dt_bias", "dn_norm_g", "sg_ln_g",
           "sg_ln_b", "sg_w", "sg_b", "w_out", "ple_norm_g", "w_ple_gate", "w_ple", "final_norm_g")

VMEM_BIG = 56 * 1024 * 1024


def _mm(a, b):
    return jnp.dot(a.astype(_MXU), b.astype(_MXU), preferred_element_type=f32)


def _mm_nt(a, b):
    return lax.dot_general(a.astype(_MXU), b.astype(_MXU), (((1,), (1,)), ((), ())), preferred_element_type=f32)


def _mm_tn(a, b):
    return lax.dot_general(a.astype(_MXU), b.astype(_MXU), (((0,), (0,)), ((), ())), preferred_element_type=f32)


@jax.custom_vjp
def bdot(a, b):
    return _mm(a, b)


def _bdot_fwd(a, b):
    return _mm(a, b), (a, b)


def _bdot_bwd(res, g):
    a, b = res
    return _mm_nt(g, b).astype(a.dtype), _mm_tn(a, g).astype(b.dtype)


bdot.defvjp(_bdot_fwd, _bdot_bwd)


@jax.custom_vjp
def bdot_nt(a, b):
    return _mm_nt(a, b)


def _bdot_nt_fwd(a, b):
    return _mm_nt(a, b), (a, b)


def _bdot_nt_bwd(res, g):
    a, b = res
    return _mm(g, b).astype(a.dtype), _mm_tn(g, a).astype(b.dtype)


bdot_nt.defvjp(_bdot_nt_fwd, _bdot_nt_bwd)


@jax.custom_vjp
def bdot_tn(a, b):
    return _mm_tn(a, b)


def _bdot_tn_fwd(a, b):
    return _mm_tn(a, b), (a, b)


def _bdot_tn_bwd(res, g):
    a, b = res
    return _mm_nt(b, g).astype(a.dtype), _mm(a, g).astype(b.dtype)


bdot_tn.defvjp(_bdot_tn_fwd, _bdot_tn_bwd)


def hdot(a, b):
    return jnp.dot(a, b, precision=HIGH, preferred_element_type=f32)


@functools.partial(jax.custom_vjp, nondiff_argnums=(1,))
def roll_rows(x, k):
    return pltpu.roll(x, k, 0)


def _roll_rows_fwd(x, k):
    return pltpu.roll(x, k, 0), None


def _roll_rows_bwd(k, _, g):
    return (pltpu.roll(g, g.shape[0] - k, 0),)


roll_rows.defvjp(_roll_rows_fwd, _roll_rows_bwd)


def _row_ids(shape):
    return lax.broadcasted_iota(jnp.int32, shape, 0)


def _shift_rows(x, d):
    return jnp.where(_row_ids(x.shape) >= d, roll_rows(x, d), 0.0)


def _rowsel(x, i):
    return jnp.sum(jnp.where(_row_ids(x.shape) == i, x, 0.0), axis=0, keepdims=True)


def _rms(x, g):
    return x * lax.rsqrt(jnp.mean(x * x, axis=-1, keepdims=True) + EPS) * g


def _layer_norm(x, g, b):
    mu = jnp.mean(x, axis=-1, keepdims=True)
    xc = x - mu
    return xc * lax.rsqrt(jnp.mean(xc * xc, axis=-1, keepdims=True) + EPS) * g + b


def _s5_prep(are, aim, ls, bre, bim):
    step = jnp.exp(ls)
    mag = jnp.exp(are * step)
    lr = mag * jnp.cos(aim * step)
    li = mag * jnp.sin(aim * step)
    den = are * are + aim * aim
    nr, ni = lr - 1.0, li
    fr = (nr * are + ni * aim) / den
    fi = (ni * are - nr * aim) / den
    bbr = fr * bre - fi * bim
    bbi = fr * bim + fi * bre
    pr = jnp.broadcast_to(lr, (S5_CHUNK, N_STATE))
    pi = jnp.broadcast_to(li, (S5_CHUNK, N_STATE))
    rows8 = _row_ids((8, N_STATE))
    qr = jnp.zeros((8, N_STATE), f32)
    qi = jnp.zeros((8, N_STATE), f32)
    d, k = 1, 0
    while d < S5_CHUNK:
        qr = qr + jnp.where(rows8 == k, _rowsel(pr, d - 1), 0.0)
        qi = qi + jnp.where(rows8 == k, _rowsel(pi, d - 1), 0.0)
        keep = _row_ids(pr.shape) >= d
        sr, si = roll_rows(pr, d), roll_rows(pi, d)
        pr, pi = jnp.where(keep, pr * sr - pi * si, pr), jnp.where(keep, pr * si + pi * sr, pi)
        d, k = 2 * d, k + 1
    return pr, pi, qr, qi, bbr, bbi


def _s5_chunk(u, gate, hr, hi, pr, pi, qr, qi, bbr, bbi, cr, ci, dr, wglu, bglu):
    n = u.shape[0]
    xr = bdot(u, bbr)
    xi = bdot(u, bbi)
    d, k = 1, 0
    while d < n:
        lr, li = _rowsel(qr, k), _rowsel(qi, k)
        sr, si = _shift_rows(xr, d), _shift_rows(xi, d)
        xr, xi = xr + lr * sr - li * si, xi + lr * si + li * sr
        d, k = 2 * d, k + 1
    xr, xi = xr + pr * hr - pi * hi, xi + pr * hi + pi * hr
    y = bdot(xr, cr) - bdot(xi, ci) + dr * u
    y = jax.nn.gelu(y)
    y = y * jax.nn.sigmoid(bdot(y, wglu) + bglu)
    return y * jax.nn.silu(gate), _rowsel(xr, n - 1), _rowsel(xi, n - 1)


def _dn_pre(xc, xp, w0, w1, w2, w3, is_start, col):
    xp = jnp.where(is_start, 0.0, xp)
    rows = _row_ids(xc.shape)
    acc = w3 * xc
    for d, w in ((1, w2), (2, w1), (3, w0)):
        acc = acc + w * jnp.where(rows >= d, roll_rows(xc, d), roll_rows(xp, d))
    y = jax.nn.silu(acc)
    nrm = y * lax.rsqrt(jnp.sum(y * y, axis=-1, keepdims=True) + EPS)
    nrm = nrm * jnp.where(col < DN_HEADS, DN_HEAD_DIM ** -0.5, 1.0)
    return jnp.where(col < 2 * DN_HEADS, nrm, y)


def _dn_local(qs, ks, vs, abs_, alog, dtb):
    c = DN_CHUNK
    ri = lax.broadcasted_iota(jnp.int32, (c, c), 0)
    ci = lax.broadcasted_iota(jnp.int32, (c, c), 1)
    causal, strict = ri >= ci, ri > ci
    eye = (ri == ci).astype(f32)
    tril = causal.astype(f32)
    gcums = [hdot(tril, -jnp.exp(alog) * jax.nn.softplus(ab + dtb)) for ab in abs_]
    gcum_ts = [g.T for g in gcums]
    sigs = [jax.nn.sigmoid(ab) for ab in abs_]
    chains = [(j, h) for j in range(len(abs_)) for h in range(DN_HEADS)]
    gc = [gcums[j][:, h:h + 1] for j, h in chains]
    decay = [jnp.where(causal, jnp.exp(jnp.where(causal, gc[n] - gcum_ts[j][h:h + 1, :], 0.0)), 0.0)
             for n, (j, h) in enumerate(chains)]
    beta = [sigs[j][:, DN_HEADS + h:DN_HEADS + h + 1] for j, h in chains]
    kb = [ks[j][h] * beta[n] for n, (j, h) in enumerate(chains)]
    pw = [-jnp.where(strict, bdot_nt(kb[n], ks[j][h]) * decay[n], 0.0) for n, (j, h) in enumerate(chains)]
    inv = [eye + m for m in pw]
    for _ in range(5):
        pw = [hdot(m, m) for m in pw]
        inv = [a + hdot(a, m) for a, m in zip(inv, pw)]
    egc = [jnp.exp(g) for g in gc]
    values = [hdot(inv[n], vs[j][h] * beta[n]) for n, (j, h) in enumerate(chains)]
    k_cds = [hdot(inv[n], kb[n] * egc[n]) for n in range(len(chains))]
    attns = [bdot_nt(qs[j][h], ks[j][h]) * decay[n] for n, (j, h) in enumerate(chains)]
    q_decs = [qs[j][h] * egc[n] for n, (j, h) in enumerate(chains)]
    k_decs = [ks[j][h] * jnp.exp(gc[n][c - 1:c, :] - gc[n]) for n, (j, h) in enumerate(chains)]

    def nest(flat):
        return [flat[j * DN_HEADS:(j + 1) * DN_HEADS] for j in range(len(abs_))]

    return nest(values), nest(k_cds), nest(attns), nest(q_decs), nest(k_decs), [jnp.exp(g[c - 1:c, :]) for g in gcums]


def _dn_step(values, k_cds, attns, q_decs, k_decs, lasts, ggs, sts, ng):
    v_new = [v - bdot(kc, st) for v, kc, st in zip(values, k_cds, sts)]
    o = [bdot(qd, st) for qd, st in zip(q_decs, sts)]
    o = [a + bdot(at, vn) for a, at, vn in zip(o, attns, v_new)]
    new = [st * la + bdot_tn(kd, vn) for st, la, kd, vn in zip(sts, lasts, k_decs, v_new)]
    return [_rms(a, ng) * jax.nn.silu(g) for a, g in zip(o, ggs)], new


def _sg_chunk(u, v, gate, lng, lnb, ws, bt):
    n = SG_CHUNK
    ug = jax.nn.gelu(u)
    vn = _layer_norm(jax.nn.gelu(v), lng, lnb)
    causal = lax.broadcasted_iota(jnp.int32, (n, n), 0) >= lax.broadcasted_iota(jnp.int32, (n, n), 1)
    lane = lax.broadcasted_iota(jnp.int32, (n, D_SG), 1)
    s = jnp.zeros((n, D_SG), f32)
    for h in range(SG_HEADS):
        t = bdot(jnp.where(causal, ws[h], 0.0), vn) + bt[:, h:h + 1]
        s = s + jnp.where((lane >= h * SG_HEAD_DIM) & (lane < (h + 1) * SG_HEAD_DIM), t, 0.0)
    return ug * s * jax.nn.silu(gate)


def _cp(n_grid, vmem=None):
    return pltpu.CompilerParams(dimension_semantics=("arbitrary",) * n_grid, vmem_limit_bytes=vmem)


def _full(shape):
    nd = len(shape)
    return pl.BlockSpec(tuple(shape), lambda *_: (0,) * nd)


def _rows(tm, ncol):
    return pl.BlockSpec((tm, ncol), lambda i: (i, 0))


def _sds(shape, dtype=f32):
    return jax.ShapeDtypeStruct(tuple(shape), dtype)


def _acc(ref, val, first):
    @pl.when(first)
    def _():
        ref[...] = val

    @pl.when(jnp.logical_not(first))
    def _():
        ref[...] += val


def in_fwd(x, g, w, name):
    t, tm = x.shape[0], 256

    def body(x_ref, g_ref, w_ref, h_ref, *z_refs):
        h = _rms(x_ref[...], g_ref[...]).astype(_MXU)
        h_ref[...] = h
        for z_ref, (a, b) in zip(z_refs, Z_COLS):
            z_ref[...] = jnp.dot(h, w_ref[:, a:b], preferred_element_type=f32)

    widths = [b - a for a, b in Z_COLS]
    return pl.pallas_call(
        body, name=name, grid=(t // tm,),
        in_specs=[_rows(tm, D_MODEL), _full((1, D_MODEL)), _full((D_MODEL, D_IN_PAD))],
        out_specs=[_rows(tm, D_MODEL)] + [_rows(tm, n) for n in widths],
        out_shape=[_sds((t, D_MODEL), _MXU)] + [_sds((t, n)) for n in widths],
        compiler_params=_cp(1, VMEM_BIG),
    )(x, g, w)


def in_bwd(x, g, w, dzs, dres, name):
    t, tm = x.shape[0], 256
    widths = [b - a for a, b in Z_COLS]

    def body(x_ref, g_ref, w_ref, dres_ref, *rest):
        dz_refs, (dx_ref, dg_ref) = rest[:5], rest[5:]
        dh = jnp.zeros((tm, D_MODEL), f32)
        for dz_ref, (a, b) in zip(dz_refs, Z_COLS):
            dh = dh + _mm_nt(dz_ref[...], w_ref[:, a:b])
        _, vj = jax.vjp(_rms, x_ref[...], g_ref[...])
        dx, dg = vj(dh)
        dx_ref[...] = dres_ref[...] + dx
        _acc(dg_ref, dg, pl.program_id(0) == 0)

    return pl.pallas_call(
        body, name=name, grid=(t // tm,),
        in_specs=[_rows(tm, D_MODEL), _full((1, D_MODEL)), _full((D_MODEL, D_IN_PAD)), _rows(tm, D_MODEL)]
        + [_rows(tm, n) for n in widths],
        out_specs=[_rows(tm, D_MODEL), _full((1, D_MODEL))],
        out_shape=[_sds((t, D_MODEL)), _sds((1, D_MODEL))],
        compiler_params=_cp(1, VMEM_BIG),
    )(x, g, w, dres, *dzs)


def wgrad(a, g, name):
    t, k = a.shape
    n = g.shape[1]
    tm = 512
    tn = n if n <= 768 else (768 if n % 768 == 0 else 512)
    steps = t // tm

    def body(a_ref, g_ref, o_ref, acc):
        i = pl.program_id(1)
        _acc(acc, _mm_tn(a_ref[...], g_ref[...]), i == 0)

        @pl.when(i == steps - 1)
        def _():
            o_ref[...] = acc[...].astype(o_ref.dtype)

    return pl.pallas_call(
        body, name=name, grid=(n // tn, steps),
        in_specs=[pl.BlockSpec((tm, k), lambda j, i: (i, 0)), pl.BlockSpec((tm, tn), lambda j, i: (i, j))],
        out_specs=pl.BlockSpec((k, tn), lambda j, i: (0, j)),
        out_shape=_sds((k, n), _COMM),
        scratch_shapes=[pltpu.VMEM((k, tn), f32)],
        compiler_params=_cp(2, VMEM_BIG),
    )(a, g)


def post_fwd(x, ys, yd, yg, p, wout, pg, wgate, wple, name):
    t, tm = x.shape[0], 256

    def body(x_ref, ys_ref, yd_ref, yg_ref, p_ref, wout_ref, pg_ref, wgate_ref, wple_ref,
             x2_ref, x1_ref, y_ref, hn_ref):
        y = jnp.concatenate([ys_ref[...], yd_ref[...], yg_ref[...]], axis=1).astype(_MXU)
        y_ref[...] = y
        x1 = x_ref[...] + jnp.dot(y, wout_ref[...], preferred_element_type=f32)
        x1_ref[...] = x1
        hn = _rms(x1, pg_ref[...]).astype(_MXU)
        hn_ref[...] = hn
        gp = jnp.dot(hn, wgate_ref[...], preferred_element_type=f32)
        pp = _mm(p_ref[...], wple_ref[...])
        x2_ref[...] = x1 + jax.nn.sigmoid(gp) * pp

    return pl.pallas_call(
        body, name=name, grid=(t // tm,),
        in_specs=[_rows(tm, D_MODEL), _rows(tm, D_SSM), _rows(tm, D_DN), _rows(tm, D_SG), _rows(tm, D_PLE),
                  _full((D_MODEL, D_MODEL)), _full((1, D_MODEL)), _full((D_MODEL, D_MODEL)), _full((D_PLE, D_MODEL))],
        out_specs=[_rows(tm, D_MODEL)] * 4,
        out_shape=[_sds((t, D_MODEL)), _sds((t, D_MODEL)), _sds((t, D_MODEL), _MXU), _sds((t, D_MODEL), _MXU)],
        compiler_params=_cp(1, VMEM_BIG),
    )(x, ys, yd, yg, p, wout, pg, wgate, wple)


def post_bwd(dx2, x1, hn, p, wout, pg, wgate, wple, name):
    t, tm = dx2.shape[0], 256

    def body(dx2_ref, x1_ref, hn_ref, p_ref, wout_ref, pg_ref, wgate_ref, wple_ref,
             dx1_ref, dgp_ref, dpp_ref, dys_ref, dyd_ref, dyg_ref, dpg_ref):
        dx2 = dx2_ref[...]
        gp = jnp.dot(hn_ref[...], wgate_ref[...], preferred_element_type=f32)
        pp = _mm(p_ref[...], wple_ref[...])
        sg = jax.nn.sigmoid(gp)
        dpp_ref[...] = (dx2 * sg).astype(_MXU)
        dgp = (dx2 * pp * sg * (1.0 - sg)).astype(_MXU)
        dgp_ref[...] = dgp
        dhn = _mm_nt(dgp, wgate_ref[...])
        _, vj = jax.vjp(_rms, x1_ref[...], pg_ref[...])
        dx1n, dpg = vj(dhn)
        dx1 = dx2 + dx1n
        dx1_ref[...] = dx1
        dy = _mm_nt(dx1, wout_ref[...])
        dys_ref[...] = dy[:, :D_SSM]
        dyd_ref[...] = dy[:, D_SSM:D_SSM + D_DN]
        dyg_ref[...] = dy[:, D_SSM + D_DN:]
        _acc(dpg_ref, dpg, pl.program_id(0) == 0)

    return pl.pallas_call(
        body, name=name, grid=(t // tm,),
        in_specs=[_rows(tm, D_MODEL), _rows(tm, D_MODEL), _rows(tm, D_MODEL), _rows(tm, D_PLE),
                  _full((D_MODEL, D_MODEL)), _full((1, D_MODEL)), _full((D_MODEL, D_MODEL)), _full((D_PLE, D_MODEL))],
        out_specs=[_rows(tm, D_MODEL), _rows(tm, D_MODEL), _rows(tm, D_MODEL), _rows(tm, D_SSM), _rows(tm, D_DN),
                   _rows(tm, D_SG), _full((1, D_MODEL))],
        out_shape=[_sds((t, D_MODEL)), _sds((t, D_MODEL), _MXU), _sds((t, D_MODEL), _MXU), _sds((t, D_SSM)),
                   _sds((t, D_DN)), _sds((t, D_SG)), _sds((1, D_MODEL))],
        compiler_params=_cp(1, VMEM_BIG),
    )(dx2, x1, hn, p, wout, pg, wgate, wple)


def loss_fwd_bwd(x, fg, target, name):
    t, tm = x.shape[0], 512

    def body(x_ref, fg_ref, t_ref, loss_ref, dx_ref, dfg_ref):
        def f(xv, gv):
            err = _rms(xv, gv) - t_ref[...]
            return 0.5 * jnp.sum(jnp.mean(err * err, axis=-1))

        val, vj = jax.vjp(f, x_ref[...], fg_ref[...])
        dx, dfg = vj(jnp.ones((), f32))
        dx_ref[...] = dx
        first = pl.program_id(0) == 0
        _acc(dfg_ref, dfg, first)
        _acc(loss_ref, jnp.full((8, LANE), val, f32), first)

    return pl.pallas_call(
        body, name=name, grid=(t // tm,),
        in_specs=[_rows(tm, D_MODEL), _full((1, D_MODEL)), _rows(tm, D_MODEL)],
        out_specs=[_full((8, LANE)), _rows(tm, D_MODEL), _full((1, D_MODEL))],
        out_shape=[_sds((8, LANE)), _sds((t, D_MODEL)), _sds((1, D_MODEL))],
        compiler_params=_cp(1),
    )(x, fg, target)


def s5_prep_fwd(are, aim, ls, bre, bim, name):
    def body(are_ref, aim_ref, ls_ref, bre_ref, bim_ref, *outs):
        vals = _s5_prep(are_ref[...], aim_ref[...], ls_ref[...], bre_ref[...], bim_ref[...])
        for o, v in zip(outs, vals):
            o[...] = v

    shapes = [(S5_CHUNK, N_STATE)] * 2 + [(8, N_STATE)] * 2 + [(D_SSM, N_STATE)] * 2
    return pl.pallas_call(body, name=name, out_shape=[_sds(s) for s in shapes])(are, aim, ls, bre, bim)


def s5_prep_bwd(are, aim, ls, bre, bim, cts, name):
    def body(are_ref, aim_ref, ls_ref, bre_ref, bim_ref, *rest):
        ct_refs, outs = rest[:6], rest[6:]
        _, vj = jax.vjp(_s5_prep, are_ref[...], aim_ref[...], ls_ref[...], bre_ref[...], bim_ref[...])
        for o, v in zip(outs, vj(tuple(r[...] for r in ct_refs))):
            o[...] = v

    shapes = [(1, N_STATE)] * 3 + [(D_SSM, N_STATE)] * 2
    return pl.pallas_call(body, name=name, out_shape=[_sds(s) for s in shapes])(are, aim, ls, bre, bim, *cts)


_S5_PARAM_SHAPES = ((S5_CHUNK, N_STATE), (S5_CHUNK, N_STATE), (8, N_STATE), (8, N_STATE), (D_SSM, N_STATE),
                    (D_SSM, N_STATE), (N_STATE, D_SSM), (N_STATE, D_SSM), (1, D_SSM), (D_SSM, D_SSM), (1, D_SSM))


def s5_fwd(z, params, nb, name):
    t = z.shape[0]
    nc = t // nb // S5_CHUNK

    def body(z_ref, *rest):
        p_refs, (y_ref, hs_ref, hr_s, hi_s) = rest[:11], rest[11:]

        @pl.when(pl.program_id(1) == 0)
        def _():
            hr_s[...] = jnp.zeros_like(hr_s)
            hi_s[...] = jnp.zeros_like(hi_s)

        hr, hi = hr_s[...], hi_s[...]
        hs_ref[0, :, :N_STATE] = hr
        hs_ref[0, :, N_STATE:] = hi
        y, nhr, nhi = _s5_chunk(z_ref[:, :D_SSM], z_ref[:, D_SSM:], hr, hi, *[r[...] for r in p_refs])
        y_ref[...] = y
        hr_s[...] = nhr
        hi_s[...] = nhi

    return pl.pallas_call(
        body, name=name, grid=(nb, nc),
        in_specs=[pl.BlockSpec((S5_CHUNK, 2 * D_SSM), lambda b, c: (b * nc + c, 0))]
        + [_full(s) for s in _S5_PARAM_SHAPES],
        out_specs=[pl.BlockSpec((S5_CHUNK, D_SSM), lambda b, c: (b * nc + c, 0)),
                   pl.BlockSpec((1, 1, 2 * N_STATE), lambda b, c: (b * nc + c, 0, 0))],
        out_shape=[_sds((t, D_SSM)), _sds((nb * nc, 1, 2 * N_STATE))],
        scratch_shapes=[pltpu.VMEM((1, N_STATE), f32), pltpu.VMEM((1, N_STATE), f32)],
        compiler_params=_cp(2, VMEM_BIG),
    )(z, *params)


def s5_bwd(z, params, hs, dy, nb, name):
    t = z.shape[0]
    nc = t // nb // S5_CHUNK

    def body(z_ref, hs_ref, dy_ref, *rest):
        p_refs, dz_ref, dp_refs, (dhr_s, dhi_s) = rest[:11], rest[11], rest[12:23], rest[23:]

        @pl.when(pl.program_id(1) == 0)
        def _():
            dhr_s[...] = jnp.zeros_like(dhr_s)
            dhi_s[...] = jnp.zeros_like(dhi_s)

        prim = (z_ref[:, :D_SSM], z_ref[:, D_SSM:], hs_ref[0, :, :N_STATE], hs_ref[0, :, N_STATE:]) + tuple(
            r[...] for r in p_refs)
        _, vj = jax.vjp(_s5_chunk, *prim)
        cts = vj((dy_ref[...], dhr_s[...], dhi_s[...]))
        dz_ref[:, :D_SSM] = cts[0]
        dz_ref[:, D_SSM:] = cts[1]
        dhr_s[...] = cts[2]
        dhi_s[...] = cts[3]
        first = (pl.program_id(0) == 0) & (pl.program_id(1) == 0)
        for r, v in zip(dp_refs, cts[4:]):
            _acc(r, v, first)

    rev = lambda b, c: (b * nc + nc - 1 - c, 0)
    return pl.pallas_call(
        body, name=name, grid=(nb, nc),
        in_specs=[pl.BlockSpec((S5_CHUNK, 2 * D_SSM), rev),
                  pl.BlockSpec((1, 1, 2 * N_STATE), lambda b, c: (b * nc + nc - 1 - c, 0, 0)),
                  pl.BlockSpec((S5_CHUNK, D_SSM), rev)] + [_full(s) for s in _S5_PARAM_SHAPES],
        out_specs=[pl.BlockSpec((S5_CHUNK, 2 * D_SSM), rev)] + [_full(s) for s in _S5_PARAM_SHAPES],
        out_shape=[_sds((t, 2 * D_SSM))] + [_sds(s) for s in _S5_PARAM_SHAPES],
        scratch_shapes=[pltpu.VMEM((1, N_STATE), f32), pltpu.VMEM((1, N_STATE), f32)],
        compiler_params=_cp(2, VMEM_BIG),
    )(z, hs, dy, *params)


DN_PRE_ROWS = 256
DN_COLS = 3 * D_DN // LANE


def dn_pre_fwd(zq, convw, seq, name):
    t, tb = zq.shape[0], DN_PRE_ROWS
    per_seq = seq // tb

    def body(xc_ref, xp_ref, w_ref, o_ref):
        is_start = pl.program_id(0) % per_seq == 0
        for j in range(DN_COLS):
            cols = slice(j * LANE, (j + 1) * LANE)
            o_ref[:, cols] = _dn_pre(xc_ref[:, cols], xp_ref[:, cols], w_ref[0:1, cols], w_ref[1:2, cols],
                                     w_ref[2:3, cols], w_ref[3:4, cols], is_start, j)

    return pl.pallas_call(
        body, name=name, grid=(t // tb,),
        in_specs=[_rows(tb, 3 * D_DN), pl.BlockSpec((tb, 3 * D_DN), lambda i: (jnp.maximum(i - 1, 0), 0)),
                  _full((DN_CONV, 3 * D_DN))],
        out_specs=_rows(tb, 3 * D_DN),
        out_shape=_sds((t, 3 * D_DN)),
        compiler_params=_cp(1, VMEM_BIG),
    )(zq, zq, convw)


def dn_pre_bwd(zq, convw, dqkv, seq, name):
    t, tb = zq.shape[0], DN_PRE_ROWS
    nrow = t // tb
    per_seq = seq // tb

    def body(xc_ref, xp_ref, w_ref, d_ref, dx_ref, dw_ref, carry):
        step = pl.program_id(0)
        i = nrow - 1 - step

        @pl.when(step == 0)
        def _():
            carry[...] = jnp.zeros_like(carry)

        for j in range(DN_COLS):
            cols = slice(j * LANE, (j + 1) * LANE)
            fn = functools.partial(_dn_pre, is_start=i % per_seq == 0, col=j)
            _, vj = jax.vjp(fn, xc_ref[:, cols], xp_ref[:, cols], w_ref[0:1, cols], w_ref[1:2, cols],
                            w_ref[2:3, cols], w_ref[3:4, cols])
            dxc, dxp, dw0, dw1, dw2, dw3 = vj(d_ref[:, cols])
            dx_ref[:, cols] = dxc + carry[:, cols]
            carry[:, cols] = dxp
            for k, dw in enumerate((dw0, dw1, dw2, dw3)):
                @pl.when(step == 0)
                def _():
                    dw_ref[k:k + 1, cols] = dw

                @pl.when(step != 0)
                def _():
                    dw_ref[k:k + 1, cols] += dw

    rev = lambda s: (nrow - 1 - s, 0)
    return pl.pallas_call(
        body, name=name, grid=(nrow,),
        in_specs=[pl.BlockSpec((tb, 3 * D_DN), rev),
                  pl.BlockSpec((tb, 3 * D_DN), lambda s: (jnp.maximum(nrow - 2 - s, 0), 0)),
                  _full((DN_CONV, 3 * D_DN)), pl.BlockSpec((tb, 3 * D_DN), rev)],
        out_specs=[pl.BlockSpec((tb, 3 * D_DN), rev), _full((DN_CONV, 3 * D_DN))],
        out_shape=[_sds((t, 3 * D_DN)), _sds((DN_CONV, 3 * D_DN))],
        scratch_shapes=[pltpu.VMEM((tb, 3 * D_DN), f32)],
        compiler_params=_cp(1, VMEM_BIG),
    )(zq, zq, convw, dqkv)


DN_LOCAL_CHUNKS = 2
DN_ATTN = DN_HEADS * DN_CHUNK


def _dn_heads(ref, rows, base=0):
    return [ref[rows, base + h * DN_HEAD_DIM:base + (h + 1) * DN_HEAD_DIM] for h in range(DN_HEADS)]


def dn_local_fwd(qkv, ab, alog, dtb, name):
    t = qkv.shape[0]
    c, n = DN_CHUNK, DN_LOCAL_CHUNKS

    def body(qkv_ref, ab_ref, alog_ref, dtb_ref, val_ref, kcd_ref, attn_ref, qd_ref, kd_ref, el_ref):
        rows = [pl.ds(j * c, c) for j in range(n)]
        vals, kcds, attns, qds, kds, els = _dn_local(
            [_dn_heads(qkv_ref, r) for r in rows], [_dn_heads(qkv_ref, r, D_DN) for r in rows],
            [_dn_heads(qkv_ref, r, 2 * D_DN) for r in rows], [ab_ref[r, :] for r in rows], alog_ref[...], dtb_ref[...])
        for j, r in enumerate(rows):
            for h in range(DN_HEADS):
                lo, hi = h * DN_HEAD_DIM, (h + 1) * DN_HEAD_DIM
                val_ref[r, lo:hi] = vals[j][h]
                kcd_ref[r, lo:hi] = kcds[j][h]
                qd_ref[r, lo:hi] = qds[j][h]
                kd_ref[r, lo:hi] = kds[j][h]
                attn_ref[r, h * c:(h + 1) * c] = attns[j][h]
            el_ref[j] = els[j]

    wide = _rows(n * c, D_DN)
    return pl.pallas_call(
        body, name=name, grid=(t // (n * c),),
        in_specs=[_rows(n * c, 3 * D_DN), _rows(n * c, LANE), _full((1, LANE)), _full((1, LANE))],
        out_specs=[wide, wide, _rows(n * c, DN_ATTN), wide, wide, pl.BlockSpec((n, 1, LANE), lambda i: (i, 0, 0))],
        out_shape=[_sds((t, D_DN)), _sds((t, D_DN)), _sds((t, DN_ATTN)), _sds((t, D_DN)), _sds((t, D_DN)),
                   _sds((t // c, 1, LANE))],
        compiler_params=_cp(1),
    )(qkv, ab, alog, dtb)


def dn_local_bwd(qkv, ab, alog, dtb, cts, name):
    t = qkv.shape[0]
    c, n = DN_CHUNK, DN_LOCAL_CHUNKS

    def body(qkv_ref, ab_ref, alog_ref, dtb_ref, dval_ref, dkcd_ref, dattn_ref, dqd_ref, dkd_ref, del_ref,
             dqkv_ref, dab_ref, dalog_ref, ddtb_ref):
        rows = [pl.ds(j * c, c) for j in range(n)]
        _, vj = jax.vjp(_dn_local, [_dn_heads(qkv_ref, r) for r in rows], [_dn_heads(qkv_ref, r, D_DN) for r in rows],
                        [_dn_heads(qkv_ref, r, 2 * D_DN) for r in rows], [ab_ref[r, :] for r in rows], alog_ref[...],
                        dtb_ref[...])
        dattn = [[dattn_ref[r, h * c:(h + 1) * c] for h in range(DN_HEADS)] for r in rows]
        dq, dk, dv, dab, dalog, ddtb = vj(([_dn_heads(dval_ref, r) for r in rows], [_dn_heads(dkcd_ref, r) for r in rows],
                                           dattn, [_dn_heads(dqd_ref, r) for r in rows],
                                           [_dn_heads(dkd_ref, r) for r in rows], [del_ref[j] for j in range(n)]))
        for j, r in enumerate(rows):
            for h in range(DN_HEADS):
                lo, hi = h * DN_HEAD_DIM, (h + 1) * DN_HEAD_DIM
                dqkv_ref[r, lo:hi] = dq[j][h]
                dqkv_ref[r, D_DN + lo:D_DN + hi] = dk[j][h]
                dqkv_ref[r, 2 * D_DN + lo:2 * D_DN + hi] = dv[j][h]
            dab_ref[r, :] = dab[j]
        first = pl.program_id(0) == 0
        _acc(dalog_ref, dalog, first)
        _acc(ddtb_ref, ddtb, first)

    wide = _rows(n * c, D_DN)
    return pl.pallas_call(
        body, name=name, grid=(t // (n * c),),
        in_specs=[_rows(n * c, 3 * D_DN), _rows(n * c, LANE), _full((1, LANE)), _full((1, LANE)),
                  wide, wide, _rows(n * c, DN_ATTN), wide, wide, pl.BlockSpec((n, 1, LANE), lambda i: (i, 0, 0))],
        out_specs=[_rows(n * c, 3 * D_DN), _rows(n * c, LANE), _full((1, LANE)), _full((1, LANE))],
        out_shape=[_sds((t, 3 * D_DN)), _sds((t, LANE)), _sds((1, LANE)), _sds((1, LANE))],
        compiler_params=_cp(1),
    )(qkv, ab, alog, dtb, *cts)


def _seq_view(a, nb):
    return a.reshape((nb, a.shape[0] // nb) + a.shape[1:])


def _dn_chains(nb):
    return [(b, h) for b in range(nb) for h in range(DN_HEADS)]


def _dn_step_operands(val_ref, kcd_ref, attn_ref, qd_ref, kd_ref, el_ref, gg_ref, nb):
    chains = _dn_chains(nb)
    c = DN_CHUNK

    def wide(ref):
        return [ref[b, :, h * DN_HEAD_DIM:(h + 1) * DN_HEAD_DIM] for b, h in chains]

    return (wide(val_ref), wide(kcd_ref), [attn_ref[b, :, h * c:(h + 1) * c] for b, h in chains], wide(qd_ref),
            wide(kd_ref), [el_ref[b, 0, :, h:h + 1] for b, h in chains], wide(gg_ref))


def dn_scan_fwd(loc, gg, ng, nb, name):
    val, kcd, attn, qd, kd, el = loc
    t = val.shape[0]
    c = DN_CHUNK
    nc = t // nb // c
    ns = nb * DN_HEADS

    def body(val_ref, kcd_ref, attn_ref, qd_ref, kd_ref, el_ref, gg_ref, ng_ref, y_ref, ss_ref, st):
        @pl.when(pl.program_id(0) == 0)
        def _():
            st[...] = jnp.zeros_like(st)

        sts = [st[i] for i in range(ns)]
        for i in range(ns):
            ss_ref[0, i] = sts[i]
        ys, new = _dn_step(*_dn_step_operands(val_ref, kcd_ref, attn_ref, qd_ref, kd_ref, el_ref, gg_ref, nb), sts,
                           ng_ref[...])
        for i, (b, h) in enumerate(_dn_chains(nb)):
            y_ref[b, :, h * DN_HEAD_DIM:(h + 1) * DN_HEAD_DIM] = ys[i]
            st[i] = new[i]

    def blk(w):
        return pl.BlockSpec((nb, c, w), lambda k: (0, k, 0))

    el_spec = pl.BlockSpec((nb, 1, 1, LANE), lambda k: (0, k, 0, 0))
    y, ss = pl.pallas_call(
        body, name=name, grid=(nc,),
        in_specs=[blk(D_DN), blk(D_DN), blk(DN_ATTN), blk(D_DN), blk(D_DN), el_spec, blk(D_DN), _full((1, LANE))],
        out_specs=[blk(D_DN), pl.BlockSpec((1, ns, DN_HEAD_DIM, DN_HEAD_DIM), lambda k: (k, 0, 0, 0))],
        out_shape=[_sds((nb, t // nb, D_DN)), _sds((nc, ns, DN_HEAD_DIM, DN_HEAD_DIM))],
        scratch_shapes=[pltpu.VMEM((ns, DN_HEAD_DIM, DN_HEAD_DIM), f32)],
        compiler_params=_cp(1),
    )(_seq_view(val, nb), _seq_view(kcd, nb), _seq_view(attn, nb), _seq_view(qd, nb), _seq_view(kd, nb),
      el.reshape(nb, nc, 1, LANE), _seq_view(gg, nb), ng)
    return y.reshape(t, D_DN), ss


def dn_scan_bwd(loc, gg, ng, ss, dy, nb, name):
    val, kcd, attn, qd, kd, el = loc
    t = val.shape[0]
    c = DN_CHUNK
    nc = t // nb // c
    ns = nb * DN_HEADS

    def body(val_ref, kcd_ref, attn_ref, qd_ref, kd_ref, el_ref, gg_ref, ng_ref, ss_ref, dy_ref,
             dval_ref, dkcd_ref, dattn_ref, dqd_ref, dkd_ref, del_ref, dgg_ref, dng_ref, dst):
        @pl.when(pl.program_id(0) == 0)
        def _():
            dst[...] = jnp.zeros_like(dst)

        lane = lax.broadcasted_iota(jnp.int32, (1, LANE), 1)
        chains = _dn_chains(nb)
        _, vj = jax.vjp(_dn_step, *_dn_step_operands(val_ref, kcd_ref, attn_ref, qd_ref, kd_ref, el_ref, gg_ref, nb),
                        [ss_ref[0, i] for i in range(ns)], ng_ref[...])
        dys = [dy_ref[b, :, h * DN_HEAD_DIM:(h + 1) * DN_HEAD_DIM] for b, h in chains]
        dval, dkcd, dattn, dqd, dkd, dlast, dgg, ds, dng = vj((dys, [dst[i] for i in range(ns)]))
        del_rows = [jnp.zeros((1, LANE), f32) for _ in range(nb)]
        for i, (b, h) in enumerate(chains):
            cols = slice(h * DN_HEAD_DIM, (h + 1) * DN_HEAD_DIM)
            dval_ref[b, :, cols] = dval[i]
            dkcd_ref[b, :, cols] = dkcd[i]
            dattn_ref[b, :, h * c:(h + 1) * c] = dattn[i]
            dqd_ref[b, :, cols] = dqd[i]
            dkd_ref[b, :, cols] = dkd[i]
            dgg_ref[b, :, cols] = dgg[i]
            dst[i] = ds[i]
            del_rows[b] = del_rows[b] + jnp.where(lane == h, dlast[i], 0.0)
        for b in range(nb):
            del_ref[b, 0] = del_rows[b]
        _acc(dng_ref, dng, pl.program_id(0) == 0)

    def blk(w):
        return pl.BlockSpec((nb, c, w), lambda k: (0, nc - 1 - k, 0))

    el_spec = pl.BlockSpec((nb, 1, 1, LANE), lambda k: (0, nc - 1 - k, 0, 0))
    outs = pl.pallas_call(
        body, name=name, grid=(nc,),
        in_specs=[blk(D_DN), blk(D_DN), blk(DN_ATTN), blk(D_DN), blk(D_DN), el_spec, blk(D_DN), _full((1, LANE)),
                  pl.BlockSpec((1, ns, DN_HEAD_DIM, DN_HEAD_DIM), lambda k: (nc - 1 - k, 0, 0, 0)), blk(D_DN)],
        out_specs=[blk(D_DN), blk(D_DN), blk(DN_ATTN), blk(D_DN), blk(D_DN), el_spec, blk(D_DN), _full((1, LANE))],
        out_shape=[_sds((nb, t // nb, D_DN)), _sds((nb, t // nb, D_DN)), _sds((nb, t // nb, DN_ATTN)),
                   _sds((nb, t // nb, D_DN)), _sds((nb, t // nb, D_DN)), _sds((nb, nc, 1, LANE)),
                   _sds((nb, t // nb, D_DN)), _sds((1, LANE))],
        scratch_shapes=[pltpu.VMEM((ns, DN_HEAD_DIM, DN_HEAD_DIM), f32)],
        compiler_params=_cp(1),
    )(_seq_view(val, nb), _seq_view(kcd, nb), _seq_view(attn, nb), _seq_view(qd, nb), _seq_view(kd, nb),
      el.reshape(nb, nc, 1, LANE), _seq_view(gg, nb), ng, ss, _seq_view(dy, nb))
    dloc = [o.reshape((t,) + o.shape[2:]) for o in outs[:5]] + [outs[5].reshape(t // c, 1, LANE)]
    return dloc, outs[6].reshape(t, D_DN), outs[7]


SG_ROWS = 512


def sg_fwd(z, lng, lnb, w, bt, name):
    t = z.shape[0]

    def body(z_ref, lng_ref, lnb_ref, w_ref, bt_ref, y_ref):
        ws = [w_ref[h] for h in range(SG_HEADS)]
        for k in range(SG_ROWS // SG_CHUNK):
            r = pl.ds(k * SG_CHUNK, SG_CHUNK)
            y_ref[r, :] = _sg_chunk(z_ref[r, :D_SG], z_ref[r, D_SG:2 * D_SG], z_ref[r, 2 * D_SG:], lng_ref[...],
                                    lnb_ref[...], ws, bt_ref[...])

    return pl.pallas_call(
        body, name=name, grid=(t // SG_ROWS,),
        in_specs=[_rows(SG_ROWS, 3 * D_SG), _full((1, D_SG)), _full((1, D_SG)),
                  _full((SG_HEADS, SG_CHUNK, SG_CHUNK)), _full((SG_CHUNK, LANE))],
        out_specs=_rows(SG_ROWS, D_SG),
        out_shape=_sds((t, D_SG)),
        compiler_params=_cp(1),
    )(z, lng, lnb, w, bt)


def sg_bwd(z, lng, lnb, w, bt, dy, name):
    t = z.shape[0]

    def body(z_ref, lng_ref, lnb_ref, w_ref, bt_ref, dy_ref, dz_ref, dlng_ref, dlnb_ref, dw_ref, dbt_ref):
        ws = [w_ref[h] for h in range(SG_HEADS)]
        tot = None
        for k in range(SG_ROWS // SG_CHUNK):
            r = pl.ds(k * SG_CHUNK, SG_CHUNK)
            _, vj = jax.vjp(_sg_chunk, z_ref[r, :D_SG], z_ref[r, D_SG:2 * D_SG], z_ref[r, 2 * D_SG:], lng_ref[...],
                            lnb_ref[...], ws, bt_ref[...])
            du, dv, dgate, dlng, dlnb, dws, dbt = vj(dy_ref[r, :])
            dz_ref[r, :D_SG] = du
            dz_ref[r, D_SG:2 * D_SG] = dv
            dz_ref[r, 2 * D_SG:] = dgate
            part = [dlng, dlnb, dbt] + list(dws)
            tot = part if tot is None else [a + b for a, b in zip(tot, part)]
        first = pl.program_id(0) == 0
        _acc(dlng_ref, tot[0], first)
        _acc(dlnb_ref, tot[1], first)
        _acc(dbt_ref, tot[2], first)
        for h in range(SG_HEADS):
            @pl.when(first)
            def _():
                dw_ref[h] = tot[3 + h]

            @pl.when(jnp.logical_not(first))
            def _():
                dw_ref[h] += tot[3 + h]

    return pl.pallas_call(
        body, name=name, grid=(t // SG_ROWS,),
        in_specs=[_rows(SG_ROWS, 3 * D_SG), _full((1, D_SG)), _full((1, D_SG)),
                  _full((SG_HEADS, SG_CHUNK, SG_CHUNK)), _full((SG_CHUNK, LANE)), _rows(SG_ROWS, D_SG)],
        out_specs=[_rows(SG_ROWS, 3 * D_SG), _full((1, D_SG)), _full((1, D_SG)),
                   _full((SG_HEADS, SG_CHUNK, SG_CHUNK)), _full((SG_CHUNK, LANE))],
        out_shape=[_sds((t, 3 * D_SG)), _sds((1, D_SG)), _sds((1, D_SG)), _sds((SG_HEADS, SG_CHUNK, SG_CHUNK)),
                   _sds((SG_CHUNK, LANE))],
        compiler_params=_cp(1),
    )(z, lng, lnb, w, bt, dy)


def add_pairs(a_list, b_list, name):
    n = len(a_list)

    def body(*refs):
        for a_ref, b_ref, o_ref in zip(refs[:n], refs[n:2 * n], refs[2 * n:]):
            o_ref[...] = (a_ref[...].astype(f32) + b_ref[...].astype(f32)).astype(o_ref.dtype)

    return pl.pallas_call(
        body, name=name, out_shape=[_sds(a.shape, a.dtype) for a in a_list],
        compiler_params=pltpu.CompilerParams(vmem_limit_bytes=VMEM_BIG),
    )(*a_list, *b_list)


def sum_adamw(recv, w, m, v, name):
    r, c = w.shape
    tr = 256 if r % 256 == 0 else r

    def body(recv_ref, w_ref, m_ref, v_ref, g_ref, d_ref, nm_ref, nv_ref):
        g = recv_ref[0].astype(f32)
        for k in range(1, N_CHIPS):
            g = g + recv_ref[k].astype(f32)
        wv = w_ref[...]
        nm = ADAM_B1 * m_ref[...] + (1.0 - ADAM_B1) * g
        nv = ADAM_B2 * v_ref[...] + (1.0 - ADAM_B2) * jnp.square(g)
        m_hat = nm / (1.0 - ADAM_B1 ** ADAM_STEP)
        v_hat = nv / (1.0 - ADAM_B2 ** ADAM_STEP)
        g_ref[...] = g
        d_ref[...] = -ADAM_LR * (m_hat / (jnp.sqrt(v_hat) + ADAM_EPS) + ADAM_WD * wv)
        nm_ref[...] = nm
        nv_ref[...] = nv

    return pl.pallas_call(
        body, name=name, grid=(r // tr,),
        in_specs=[pl.BlockSpec((N_CHIPS, tr, c), lambda i: (0, i, 0))] + [_rows(tr, c)] * 3,
        out_specs=[_rows(tr, c)] * 4,
        out_shape=[_sds((r, c))] * 4,
        compiler_params=_cp(1, VMEM_BIG),
    )(recv, w, m, v)


_ANY = pl.BlockSpec(memory_space=pl.ANY)
_MESH = pl.DeviceIdType.MESH


def _flip(v, bit):
    return 1 - v if bit else v


_CHIP_RELS = ((1, 0), (0, 1), (1, 1))


def _piece(ref, kind, q):
    if kind[0] == "slot":
        return ref.at[q]
    if kind[0] == "all":
        return ref
    _, axis, n = kind
    return ref.at[(slice(None),) * axis + (pl.ds(q * n, n),)]


def _piece_shape(shape, kind):
    if kind[0] == "slot":
        return tuple(shape[1:])
    if kind[0] == "all":
        return tuple(shape)
    _, axis, n = kind
    return tuple(shape[:axis]) + (n,) + tuple(shape[axis + 1:])


def gather_weights(shards, kinds, name):
    n = len(shards)

    def out_shape(s, kind):
        if kind[0] == "slot":
            return (N_CHIPS,) + tuple(s.shape)
        _, axis, w = kind
        return tuple(s.shape[:axis + 1]) + (N_CHIPS * w,) + tuple(s.shape[axis + 2:])

    def place(o_ref, kind, q, layer):
        if kind[0] == "slot":
            return o_ref.at[q, layer]
        return _piece(o_ref.at[layer], kind, q)

    def place_all(o_ref, kind, q):
        if kind[0] == "slot":
            return o_ref.at[q]
        _, axis, w = kind
        return o_ref.at[(slice(None),) * (axis + 1) + (pl.ds(q * w, w),)]

    def body(*refs):
        s_refs, o_refs = refs[:n], refs[n:2 * n]
        send_sems, recv_sems, fwd_send_sems, fwd_recv_sems, local_sems = refs[2 * n:]
        x, y, c = lax.axis_index("x"), lax.axis_index("y"), lax.axis_index("c")
        mine = 2 * x + y
        local = [pltpu.make_async_copy(s_refs[k], place_all(o_refs[k], kinds[k], mine), local_sems.at[k])
                 for k in range(n)]
        for cp in local:
            cp.start()
        sends, arrivals, forwards, fwd_arrivals = [], [], [], []
        for r, (fx, fy) in enumerate(_CHIP_RELS):
            px, py = _flip(x, fx), _flip(y, fy)
            peer = 2 * px + py
            for k in range(n):
                s = r * n + k
                sends.append(pltpu.make_async_remote_copy(
                    src_ref=s_refs[k].at[c], dst_ref=place(o_refs[k], kinds[k], mine, c), send_sem=send_sems.at[s],
                    recv_sem=recv_sems.at[s], device_id=(px, py, c), device_id_type=_MESH))
                arrivals.append(pltpu.make_async_remote_copy(
                    src_ref=s_refs[k].at[c], dst_ref=place(o_refs[k], kinds[k], peer, c), send_sem=send_sems.at[s],
                    recv_sem=recv_sems.at[s], device_id=(px, py, c), device_id_type=_MESH))
                block = place(o_refs[k], kinds[k], peer, c)
                forwards.append(pltpu.make_async_remote_copy(
                    src_ref=block, dst_ref=block, send_sem=fwd_send_sems.at[s], recv_sem=fwd_recv_sems.at[s],
                    device_id=(x, y, 1 - c), device_id_type=_MESH))
                other = place(o_refs[k], kinds[k], peer, 1 - c)
                fwd_arrivals.append(pltpu.make_async_remote_copy(
                    src_ref=other, dst_ref=other, send_sem=fwd_send_sems.at[s], recv_sem=fwd_recv_sems.at[s],
                    device_id=(x, y, 1 - c), device_id_type=_MESH))
        for cp in sends:
            cp.start()
        for arrived, fwd in zip(arrivals, forwards):
            arrived.wait_recv()
            fwd.start()
        for cp in fwd_arrivals:
            cp.wait_recv()
        for cp in sends + forwards:
            cp.wait_send()
        for cp in local:
            cp.wait()

    m = len(_CHIP_RELS) * n
    return pl.pallas_call(
        body, name=name, in_specs=[_ANY] * n, out_specs=[_ANY] * n,
        out_shape=[_sds(out_shape(s, k), s.dtype) for s, k in zip(shards, kinds)],
        scratch_shapes=[pltpu.SemaphoreType.DMA((m,)), pltpu.SemaphoreType.DMA((m,)), pltpu.SemaphoreType.DMA((m,)),
                        pltpu.SemaphoreType.DMA((m,)), pltpu.SemaphoreType.DMA((n,))],
    )(*shards)


def exchange_halves(gs, name):
    n = len(gs)

    def body(*refs):
        g_refs, mine_refs, got_refs = refs[:n], refs[n:2 * n], refs[2 * n:3 * n]
        send_sems, recv_sems, local_sems = refs[3 * n:]
        x, y, c = lax.axis_index("x"), lax.axis_index("y"), lax.axis_index("c")
        local = [pltpu.make_async_copy(g_refs[k].at[c], mine_refs[k], local_sems.at[k]) for k in range(n)]
        swaps = [pltpu.make_async_remote_copy(
            src_ref=g_refs[k].at[1 - c], dst_ref=got_refs[k], send_sem=send_sems.at[k], recv_sem=recv_sems.at[k],
            device_id=(x, y, 1 - c), device_id_type=_MESH) for k in range(n)]
        for cp in local + swaps:
            cp.start()
        for cp in swaps + local:
            cp.wait()

    halves = [_sds(g.shape[1:], g.dtype) for g in gs]
    outs = pl.pallas_call(
        body, name=name, in_specs=[_ANY] * n, out_specs=[_ANY] * (2 * n), out_shape=halves + halves,
        scratch_shapes=[pltpu.SemaphoreType.DMA((n,)), pltpu.SemaphoreType.DMA((n,)), pltpu.SemaphoreType.DMA((n,))],
    )(*gs)
    return outs[:n], outs[n:]


def reduce_to_chips(ts, kinds, states, name):
    n, ns = len(ts), len(states)

    def body(*refs):
        t_refs, s_refs = refs[:n], refs[n:n + ns]
        o_refs, so_refs = refs[n + ns:2 * n + ns], refs[2 * n + ns:2 * n + 2 * ns]
        send_sems, recv_sems, local_sems = refs[2 * n + 2 * ns:]
        x, y, c = lax.axis_index("x"), lax.axis_index("y"), lax.axis_index("c")
        mine = 2 * x + y
        local = [pltpu.make_async_copy(_piece(t_refs[k], kinds[k], mine), o_refs[k].at[mine], local_sems.at[k])
                 for k in range(n)]
        local += [pltpu.make_async_copy(s_refs[k].at[c], so_refs[k], local_sems.at[n + k]) for k in range(ns)]
        for cp in local:
            cp.start()
        sends, arrivals = [], []
        for r, (fx, fy) in enumerate(_CHIP_RELS):
            px, py = _flip(x, fx), _flip(y, fy)
            peer = 2 * px + py
            for k in range(n):
                s = r * n + k
                sends.append(pltpu.make_async_remote_copy(
                    src_ref=_piece(t_refs[k], kinds[k], peer), dst_ref=o_refs[k].at[mine], send_sem=send_sems.at[s],
                    recv_sem=recv_sems.at[s], device_id=(px, py, c), device_id_type=_MESH))
                arrivals.append(pltpu.make_async_remote_copy(
                    src_ref=_piece(t_refs[k], kinds[k], peer), dst_ref=o_refs[k].at[peer], send_sem=send_sems.at[s],
                    recv_sem=recv_sems.at[s], device_id=(px, py, c), device_id_type=_MESH))
        for cp in sends:
            cp.start()
        for cp in arrivals:
            cp.wait_recv()
        for cp in sends:
            cp.wait_send()
        for cp in local:
            cp.wait()

    m = len(_CHIP_RELS) * n
    outs = pl.pallas_call(
        body, name=name, in_specs=[_ANY] * (n + ns), out_specs=[_ANY] * (n + ns),
        out_shape=[_sds((N_CHIPS,) + _piece_shape(t.shape, k), t.dtype) for t, k in zip(ts, kinds)]
        + [_sds(s.shape[1:], s.dtype) for s in states],
        scratch_shapes=[pltpu.SemaphoreType.DMA((m,)), pltpu.SemaphoreType.DMA((m,)),
                        pltpu.SemaphoreType.DMA((n + ns,))],
    )(*ts, *states)
    return outs[:n], outs[n:]


def share_halves(rs, name):
    n = len(rs)

    def body(*refs):
        r_refs, o_refs = refs[:n], refs[n:2 * n]
        send_sems, recv_sems, local_sems = refs[2 * n:]
        x, y, c = lax.axis_index("x"), lax.axis_index("y"), lax.axis_index("c")
        local = [pltpu.make_async_copy(r_refs[k], o_refs[k].at[c], local_sems.at[k]) for k in range(n)]
        swaps = [pltpu.make_async_remote_copy(
            src_ref=r_refs[k], dst_ref=o_refs[k].at[c], send_sem=send_sems.at[k], recv_sem=recv_sems.at[k],
            device_id=(x, y, 1 - c), device_id_type=_MESH) for k in range(n)]
        arrivals = [pltpu.make_async_remote_copy(
            src_ref=r_refs[k], dst_ref=o_refs[k].at[1 - c], send_sem=send_sems.at[k], recv_sem=recv_sems.at[k],
            device_id=(x, y, 1 - c), device_id_type=_MESH) for k in range(n)]
        for cp in local + swaps:
            cp.start()
        for cp in arrivals:
            cp.wait_recv()
        for cp in swaps:
            cp.wait_send()
        for cp in local:
            cp.wait()

    return pl.pallas_call(
        body, name=name, in_specs=[_ANY] * n, out_specs=[_ANY] * n,
        out_shape=[_sds((2,) + r.shape, r.dtype) for r in rs],
        scratch_shapes=[pltpu.SemaphoreType.DMA((n,)), pltpu.SemaphoreType.DMA((n,)), pltpu.SemaphoreType.DMA((n,))],
    )(*rs)


def _pack_rows(parts, mult, dtype):
    blocks = []
    for a in parts:
        flat = a.reshape(-1).astype(dtype)
        rows = -(-flat.shape[0] // LANE)
        blocks.append(jnp.pad(flat, (0, rows * LANE - flat.shape[0])).reshape(rows, LANE))
    buf = jnp.concatenate(blocks, axis=0)
    return jnp.pad(buf, ((0, -buf.shape[0] % mult), (0, 0)))


def _unpack_rows(buf, shapes):
    out, row = [], 0
    for s in shapes:
        n = 1
        for d in s:
            n *= d
        rows = -(-n // LANE)
        out.append(buf[row:row + rows].reshape(-1)[:n].reshape(s))
        row += rows
    return out


def _permuted_from_shards(shards):
    parts = []
    for lo, hi in GROUP_COLS:
        for q in range(N_CHIPS):
            a, b = max(lo, q * SHARD_COLS), min(hi, (q + 1) * SHARD_COLS)
            if a < b:
                parts.append(shards[q][..., a - q * SHARD_COLS:b - q * SHARD_COLS])
    pad = jnp.zeros(shards[0].shape[:-1] + (D_IN_PAD - D_IN,), shards[0].dtype)
    return jnp.concatenate(parts + [pad], axis=-1)


def _shards_from_groups(groups):
    in_order = sorted(range(len(GROUP_COLS)), key=lambda j: GROUP_COLS[j][0])
    shards = []
    for q in range(N_CHIPS):
        parts = []
        for j in in_order:
            lo, hi = GROUP_COLS[j]
            a, b = max(lo, q * SHARD_COLS), min(hi, (q + 1) * SHARD_COLS)
            if a < b:
                parts.append(groups[j][..., a - lo:b - lo])
        shards.append(jnp.concatenate(parts, axis=-1))
    return shards


def _expand_b(b):
    eye = jnp.eye(SSM_GROUPS, dtype=b.dtype)
    return jnp.einsum("gnc,gh->gchn", b, eye).reshape(D_SSM, N_STATE)


def _extract_b(e):
    return jnp.einsum("gcgn->gnc", e.reshape(SSM_GROUPS, SSM_GROUP, SSM_GROUPS, SSM_STATE))


def _expand_c(c):
    eye = jnp.eye(SSM_GROUPS, dtype=c.dtype)
    return jnp.einsum("gcn,gh->gnhc", c, eye).reshape(N_STATE, D_SSM)


def _extract_c(e):
    return jnp.einsum("gngc->gcn", e.reshape(SSM_GROUPS, SSM_STATE, SSM_GROUPS, SSM_GROUP))


def _lane_row(v):
    return jnp.pad(v, (0, LANE - v.shape[0])).reshape(1, LANE)


def _layer_params(w, l):
    return dict(
        norm_g=w["norm_g"][l][None], win=w["w_in_perm"][l], wout=w["w_out"][l].astype(_MXU),
        pg=w["ple_norm_g"][l][None], wgate=w["w_ple_gate"][l].astype(_MXU), wple=w["w_ple"][l].astype(_MXU),
        are=w["ssm_a_re"][l].reshape(1, N_STATE), aim=w["ssm_a_im"][l].reshape(1, N_STATE),
        ls=jnp.repeat(w["ssm_log_step"][l], SSM_STATE).reshape(1, N_STATE),
        bre=_expand_b(w["ssm_b_re"][l]), bim=_expand_b(w["ssm_b_im"][l]),
        cr=_expand_c(w["ssm_c_re"][l]), ci=_expand_c(w["ssm_c_im"][l]),
        dr=w["ssm_d"][l].reshape(1, D_SSM), wglu=w["ssm_w_glu"][l].astype(f32), bglu=w["ssm_b_glu"][l][None],
        convw=w["dn_conv_w"][l], alog=_lane_row(w["dn_a_log"][l]), dtb=_lane_row(w["dn_dt_bias"][l]),
        ng=w["dn_norm_g"][l][None],
        lng=w["sg_ln_g"][l][None], lnb=w["sg_ln_b"][l][None], sgw=w["sg_w"][l],
        bt=jnp.pad(w["sg_b"][l].T, ((0, 0), (0, LANE - SG_HEADS))),
    )


def _layer_fwd(x, p, lp, nb, tag):
    seq = x.shape[0] // nb
    h, zs, zq, zg, zsg, zab = in_fwd(x, lp["norm_g"], lp["win"], f"in_fwd{tag}")
    prep = s5_prep_fwd(lp["are"], lp["aim"], lp["ls"], lp["bre"], lp["bim"], f"s5_prep_fwd{tag}")
    s5p = tuple(prep) + (lp["cr"], lp["ci"], lp["dr"], lp["wglu"], lp["bglu"])
    ys, hs = s5_fwd(zs, s5p, nb, f"s5_fwd{tag}")
    qkv = dn_pre_fwd(zq, lp["convw"], seq, f"dn_pre_fwd{tag}")
    loc = dn_local_fwd(qkv, zab, lp["alog"], lp["dtb"], f"dn_local_fwd{tag}")
    yd, ss = dn_scan_fwd(loc, zg, lp["ng"], nb, f"dn_scan_fwd{tag}")
    yg = sg_fwd(zsg, lp["lng"], lp["lnb"], lp["sgw"], lp["bt"], f"sg_fwd{tag}")
    x2, x1, y, hn = post_fwd(x, ys, yd, yg, p, lp["wout"], lp["pg"], lp["wgate"], lp["wple"], f"post_fwd{tag}")
    saved = dict(x=x, h=h, zs=zs, zq=zq, zg=zg, zsg=zsg, zab=zab, s5p=s5p, hs=hs, qkv=qkv, loc=loc, ss=ss, x1=x1, y=y, hn=hn, p=p)
    return x2, saved


def _layer_bwd(dx2, sv, lp, nb, tag):
    seq = dx2.shape[0] // nb
    dx1, dgp, dpp, dys, dyd, dyg, dpg = post_bwd(dx2, sv["x1"], sv["hn"], sv["p"], lp["wout"], lp["pg"], lp["wgate"],
                                                 lp["wple"], f"post_bwd{tag}")
    g = {}
    g["w_out"] = wgrad(sv["y"], dx1, f"wgrad_out{tag}")
    g["w_ple_gate"] = wgrad(sv["hn"], dgp, f"wgrad_gate{tag}")
    g["w_ple"] = wgrad(sv["p"], dpp, f"wgrad_ple{tag}")
    g["ple_norm_g"] = dpg[0]
    dzsg, dlng, dlnb, dsgw, dbt = sg_bwd(sv["zsg"], lp["lng"], lp["lnb"], lp["sgw"], lp["bt"], dyg, f"sg_bwd{tag}")
    g["sg_ln_g"], g["sg_ln_b"], g["sg_w"], g["sg_b"] = dlng[0], dlnb[0], dsgw, dbt[:, :SG_HEADS].T
    dloc, dzg, dng = dn_scan_bwd(sv["loc"], sv["zg"], lp["ng"], sv["ss"], dyd, nb, f"dn_scan_bwd{tag}")
    dqkv, dzab, dalog, ddtb = dn_local_bwd(sv["qkv"], sv["zab"], lp["alog"], lp["dtb"], dloc, f"dn_local_bwd{tag}")
    dzq, dconv = dn_pre_bwd(sv["zq"], lp["convw"], dqkv, seq, f"dn_pre_bwd{tag}")
    g["dn_conv_w"], g["dn_a_log"], g["dn_dt_bias"], g["dn_norm_g"] = dconv, dalog[0, :DN_HEADS], ddtb[0, :DN_HEADS], dng[0]
    s5out = s5_bwd(sv["zs"], sv["s5p"], sv["hs"], dys, nb, f"s5_bwd{tag}")
    dzs, dprep, (dcr, dci, ddr, dwglu, dbglu) = s5out[0], s5out[1:7], s5out[7:]
    dare, daim, dls, dbre, dbim = s5_prep_bwd(lp["are"], lp["aim"], lp["ls"], lp["bre"], lp["bim"], dprep,
                                              f"s5_prep_bwd{tag}")
    g["ssm_a_re"] = dare.reshape(SSM_GROUPS, SSM_STATE)
    g["ssm_a_im"] = daim.reshape(SSM_GROUPS, SSM_STATE)
    g["ssm_log_step"] = dls.reshape(SSM_GROUPS, SSM_STATE).sum(axis=1)
    g["ssm_b_re"], g["ssm_b_im"] = _extract_b(dbre), _extract_b(dbim)
    g["ssm_c_re"], g["ssm_c_im"] = _extract_c(dcr), _extract_c(dci)
    g["ssm_d"] = ddr.reshape(SSM_GROUPS, SSM_GROUP)
    g["ssm_w_glu"], g["ssm_b_glu"] = dwglu, dbglu[0]
    dzs_all = (dzs, dzq, dzg, dzsg, dzab)
    dx, dng_in = in_bwd(sv["x"], lp["norm_g"], lp["win"], dzs_all, dx1, f"in_bwd{tag}")
    g["w_in_pieces"] = [wgrad(sv["h"], dz, f"wgrad_in{k}{tag}") for k, dz in enumerate(dzs_all)]
    g["norm_g"] = dng_in[0]
    return dx, g


def _local_step(x, p, target, w, nb):
    lps = [_layer_params(w, l) for l in range(DEPTH)]
    saved = []
    for l in range(DEPTH):
        x, sv = _layer_fwd(x, p[l], lps[l], nb, f"_l{l}")
        saved.append(sv)
    loss_blk, dx, dfg = loss_fwd_bwd(x, w["final_norm_g"][None], target, "loss")
    grads = [None] * DEPTH
    for l in reversed(range(DEPTH)):
        dx, grads[l] = _layer_bwd(dx, saved[l], lps[l], nb, f"_l{l}")
    out = {k: jnp.stack([grads[l][k] for l in range(DEPTH)]) for k in grads[0] if k != "w_in_pieces"}
    out["w_in_pieces"] = [grads[l]["w_in_pieces"] for l in range(DEPTH)]
    out["final_norm_g"] = dfg[0]
    return loss_blk[0, 0], dx, out


def kernel(x, p, norm_g, w_in, ssm_a_re, ssm_a_im, ssm_b_re, ssm_b_im, ssm_c_re, ssm_c_im, ssm_d, ssm_log_step, ssm_w_glu, ssm_b_glu, dn_conv_w, dn_a_log, dn_dt_bias, dn_norm_g, sg_ln_g, sg_ln_b, sg_w, sg_b, w_out, ple_norm_g, w_ple_gate, w_ple, final_norm_g, loss_target, m_norm_g, m_w_in, m_ssm_a_re, m_ssm_a_im, m_ssm_b_re, m_ssm_b_im, m_ssm_c_re, m_ssm_c_im, m_ssm_d, m_ssm_log_step, m_ssm_w_glu, m_ssm_b_glu, m_dn_conv_w, m_dn_a_log, m_dn_dt_bias, m_dn_norm_g, m_sg_ln_g, m_sg_ln_b, m_sg_w, m_sg_b, m_w_out, m_ple_norm_g, m_w_ple_gate, m_w_ple, m_final_norm_g, v_norm_g, v_w_in, v_ssm_a_re, v_ssm_a_im, v_ssm_b_re, v_ssm_b_im, v_ssm_c_re, v_ssm_c_im, v_ssm_d, v_ssm_log_step, v_ssm_w_glu, v_ssm_b_glu, v_dn_conv_w, v_dn_a_log, v_dn_dt_bias, v_dn_norm_g, v_sg_ln_g, v_sg_ln_b, v_sg_w, v_sg_b, v_w_out, v_ple_norm_g, v_w_ple_gate, v_w_ple, v_final_norm_g):
    args = locals()
    w = {n: args[n] for n in WEIGHTS}
    m = {n: args["m_" + n] for n in WEIGHTS}
    v = {n: args["v_" + n] for n in WEIGHTS}
    nb, seq = x.shape[0], x.shape[1]
    t = nb * seq

    full = _gather_full(w)
    loss_local, dx, grads = _local_step(x.reshape(t, D_MODEL), p.reshape(DEPTH, t, D_PLE),
                                        loss_target.reshape(t, D_MODEL), full, nb)
    outs = _reduce_and_update(grads, w, m, v)
    loss = lax.psum(loss_local, MESH_AXES)
    return (loss, dx.reshape(nb, seq, D_MODEL), *[outs[0][n] for n in WEIGHTS], *[outs[1][n] for n in WEIGHTS],
            *[outs[2][n] for n in WEIGHTS], *[outs[3][n] for n in WEIGHTS])


def _gather_full(w):
    sh_names = [n for n, _ in SHARDED]
    gathered = gather_weights([w[n] if n == "dn_conv_w" else w[n].astype(_COMM) for n in sh_names],
                              [k for _, k in SHARDED], "gather_weights")
    full = {n: w[n] for n in REPLICATED}
    full.update(zip(sh_names, gathered))
    slots = full.pop("w_in")
    full["w_in_perm"] = _permuted_from_shards([slots[q] for q in range(N_CHIPS)]).astype(_MXU)
    return full


def _reduce_and_update(grads, w, m, v):
    sh_names = [n for n, _ in SHARDED]
    sh_kinds = [k for _, k in SHARDED]

    def pack_small(d):
        buf = _pack_rows([d[n] for n in REPLICATED], 512, f32)
        return buf.reshape(2, buf.shape[0] // 2, LANE)

    grads["w_in"] = jnp.stack([jnp.stack(_shards_from_groups(pieces)) for pieces in grads["w_in_pieces"]])
    gs = [grads[n] if n == "dn_conv_w" else grads[n].astype(_COMM) for n in sh_names] + [pack_small(grads)]
    kinds = sh_kinds + [("all",)]
    mine, got = exchange_halves(gs, "exchange_halves")
    sums = add_pairs(mine, got, "add_halves")
    states = []
    for n in sh_names:
        states += [w[n], m[n], v[n]]
    states += [pack_small(w), pack_small(m), pack_small(v)]
    parts, halves = reduce_to_chips(sums, kinds, states, "reduce_to_chips")
    results = []
    for k, n in enumerate(sh_names + ["replicated"]):
        results += sum_adamw(parts[k], halves[3 * k], halves[3 * k + 1], halves[3 * k + 2], f"adamw_{n}")
    shared = share_halves(results, "share_halves")
    rep_shapes = [w[n].shape for n in REPLICATED]
    outs = []
    for j in range(4):
        d = {n: shared[4 * k + j] for k, n in enumerate(sh_names)}
        small = shared[4 * len(sh_names) + j]
        d.update(zip(REPLICATED, _unpack_rows(small.reshape(-1, LANE), rep_shapes)))
        outs.append(d)
    return outs
```

```python
import functools

import jax
import jax.numpy as jnp
from jax import lax
from jax.experimental import pallas as pl
from jax.experimental.pallas import tpu as pltpu

f32 = jnp.float32
bf16 = jnp.bfloat16

_MXU = bf16
_COMM = bf16
HIGH = lax.Precision.HIGH

D_MODEL = 1024
DEPTH = 2
D_PLE = 256
D_SSM = 256
D_DN = 512
D_SG = 256
SSM_GROUPS = 16
SSM_GROUP = 16
SSM_STATE = 64
N_STATE = SSM_GROUPS * SSM_STATE
DN_HEADS = 4
DN_HEAD_DIM = 128
DN_CONV = 4
DN_CHUNK = 64
SG_HEADS = 4
SG_HEAD_DIM = 64
SG_CHUNK = 128
S5_CHUNK = 128
EPS = 1e-6
D_IN = 3336
D_IN_PAD = 3456
LANE = 128

ADAM_LR = 0.001
ADAM_B1 = 0.9
ADAM_B2 = 0.999
ADAM_EPS = 1e-08
ADAM_WD = 0.01
ADAM_STEP = 10

N_CHIPS = 4
N_DEV = 8
MESH_AXES = ("x", "y", "c")

Z_COLS = ((0, 512), (512, 2048), (2048, 2560), (2560, 3328), (3328, 3456))

GROUP_COLS = ((0, 512), (512, 2048), (2056, 2568), (2568, 3336), (2048, 2056))
SHARD_COLS = D_IN // 4

SHARDED = (("w_in", ("slot",)), ("ssm_w_glu", ("win", 0, 64)), ("dn_conv_w", ("win", 1, 384)),
           ("w_out", ("win", 0, 256)), ("w_ple_gate", ("win", 0, 256)), ("w_ple", ("win", 1, 256)))
REPLICATED = ("norm_g", "ssm_a_re", "ssm_a_im", "ssm_b_re", "ssm_b_im", "ssm_c_re", "ssm_c_im", "ssm_d",
              "ssm_log_step", "ssm_b_glu", "dn_a_log", "dn_dt_bias", "dn_norm_g", "sg_ln_g", "sg_ln_b", "sg_w",
              "sg_b", "ple_norm_g", "final_norm_g")
WEIGHTS = ("norm_g", "w_in", "ssm_a_re", "ssm_a_im", "ssm_b_re", "ssm_b_im", "ssm_c_re", "ssm_c_im", "ssm_d",
           "ssm_log_step", "ssm_w_glu", "ssm_b_glu", "dn_conv_w", "dn_a_log", "dn_dt_bias", "dn_norm_g", "sg_ln_g",
           "sg_ln_b", "sg_w", "sg_b", "w_out", "ple_norm_g", "w_ple_gate", "w_ple", "final_norm_g")

VMEM_BIG = 56 * 1024 * 1024


def _mm(a, b):
    return jnp.dot(a.astype(_MXU), b.astype(_MXU), preferred_element_type=f32)


def _mm_nt(a, b):
    return lax.dot_general(a.astype(_MXU), b.astype(_MXU), (((1,), (1,)), ((), ())), preferred_element_type=f32)


def _mm_tn(a, b):
    return lax.dot_general(a.astype(_MXU), b.astype(_MXU), (((0,), (0,)), ((), ())), preferred_element_type=f32)


@jax.custom_vjp
def bdot(a, b):
    return _mm(a, b)


def _bdot_fwd(a, b):
    return _mm(a, b), (a, b)


def _bdot_bwd(res, g):
    a, b = res
    return _mm_nt(g, b).astype(a.dtype), _mm_tn(a, g).astype(b.dtype)


bdot.defvjp(_bdot_fwd, _bdot_bwd)


@jax.custom_vjp
def bdot_nt(a, b):
    return _mm_nt(a, b)


def _bdot_nt_fwd(a, b):
    return _mm_nt(a, b), (a, b)


def _bdot_nt_bwd(res, g):
    a, b = res
    return _mm(g, b).astype(a.dtype), _mm_tn(g, a).astype(b.dtype)


bdot_nt.defvjp(_bdot_nt_fwd, _bdot_nt_bwd)


@jax.custom_vjp
def bdot_tn(a, b):
    return _mm_tn(a, b)


def _bdot_tn_fwd(a, b):
    return _mm_tn(a, b), (a, b)


def _bdot_tn_bwd(res, g):
    a, b = res
    return _mm_nt(b, g).astype(a.dtype), _mm(a, g).astype(b.dtype)


bdot_tn.defvjp(_bdot_tn_fwd, _bdot_tn_bwd)


def hdot(a, b):
    return jnp.dot(a, b, precision=HIGH, preferred_element_type=f32)


@functools.partial(jax.custom_vjp, nondiff_argnums=(1,))
def roll_rows(x, k):
    return pltpu.roll(x, k, 0)


def _roll_rows_fwd(x, k):
    return pltpu.roll(x, k, 0), None


def _roll_rows_bwd(k, _, g):
    return (pltpu.roll(g, g.shape[0] - k, 0),)


roll_rows.defvjp(_roll_rows_fwd, _roll_rows_bwd)


def _row_ids(shape):
    return lax.broadcasted_iota(jnp.int32, shape, 0)


def _shift_rows(x, d):
    return jnp.where(_row_ids(x.shape) >= d, roll_rows(x, d), 0.0)


def _rowsel(x, i):
    return jnp.sum(jnp.where(_row_ids(x.shape) == i, x, 0.0), axis=0, keepdims=True)


def _rms(x, g):
    return x * lax.rsqrt(jnp.mean(x * x, axis=-1, keepdims=True) + EPS) * g


def _layer_norm(x, g, b):
    mu = jnp.mean(x, axis=-1, keepdims=True)
    xc = x - mu
    return xc * lax.rsqrt(jnp.mean(xc * xc, axis=-1, keepdims=True) + EPS) * g + b


def _s5_prep(are, aim, ls, bre, bim):
    step = jnp.exp(ls)
    mag = jnp.exp(are * step)
    lr = mag * jnp.cos(aim * step)
    li = mag * jnp.sin(aim * step)
    den = are * are + aim * aim
    nr, ni = lr - 1.0, li
    fr = (nr * are + ni * aim) / den
    fi = (ni * are - nr * aim) / den
    bbr = fr * bre - fi * bim
    bbi = fr * bim + fi * bre
    pr = jnp.broadcast_to(lr, (S5_CHUNK, N_STATE))
    pi = jnp.broadcast_to(li, (S5_CHUNK, N_STATE))
    rows8 = _row_ids((8, N_STATE))
    qr = jnp.zeros((8, N_STATE), f32)
    qi = jnp.zeros((8, N_STATE), f32)
    d, k = 1, 0
    while d < S5_CHUNK:
        qr = qr + jnp.where(rows8 == k, _rowsel(pr, d - 1), 0.0)
        qi = qi + jnp.where(rows8 == k, _rowsel(pi, d - 1), 0.0)
        keep = _row_ids(pr.shape) >= d
        sr, si = roll_rows(pr, d), roll_rows(pi, d)
        pr, pi = jnp.where(keep, pr * sr - pi * si, pr), jnp.where(keep, pr * si + pi * sr, pi)
        d, k = 2 * d, k + 1
    return pr, pi, qr, qi, bbr, bbi


def _s5_chunk(u, gate, hr, hi, pr, pi, qr, qi, bbr, bbi, cr, ci, dr, wglu, bglu):
    n = u.shape[0]
    xr = bdot(u, bbr)
    xi = bdot(u, bbi)
    d, k = 1, 0
    while d < n:
        lr, li = _rowsel(qr, k), _rowsel(qi, k)
        sr, si = _shift_rows(xr, d), _shift_rows(xi, d)
        xr, xi = xr + lr * sr - li * si, xi + lr * si + li * sr
        d, k = 2 * d, k + 1
    xr, xi = xr + pr * hr - pi * hi, xi + pr * hi + pi * hr
    y = bdot(xr, cr) - bdot(xi, ci) + dr * u
    y = jax.nn.gelu(y)
    y = y * jax.nn.sigmoid(bdot(y, wglu) + bglu)
    return y * jax.nn.silu(gate), _rowsel(xr, n - 1), _rowsel(xi, n - 1)


def _dn_pre(xc, xp, w0, w1, w2, w3, is_start, col):
    xp = jnp.where(is_start, 0.0, xp)
    rows = _row_ids(xc.shape)
    acc = w3 * xc
    for d, w in ((1, w2), (2, w1), (3, w0)):
        acc = acc + w * jnp.where(rows >= d, roll_rows(xc, d), roll_rows(xp, d))
    y = jax.nn.silu(acc)
    nrm = y * lax.rsqrt(jnp.sum(y * y, axis=-1, keepdims=True) + EPS)
    nrm = nrm * jnp.where(col < DN_HEADS, DN_HEAD_DIM ** -0.5, 1.0)
    return jnp.where(col < 2 * DN_HEADS, nrm, y)


def _dn_local(qs, ks, vs, abs_, alog, dtb):
    c = DN_CHUNK
    ri = lax.broadcasted_iota(jnp.int32, (c, c), 0)
    ci = lax.broadcasted_iota(jnp.int32, (c, c), 1)
    causal, strict = ri >= ci, ri > ci
    eye = (ri == ci).astype(f32)
    tril = causal.astype(f32)
    gcums = [hdot(tril, -jnp.exp(alog) * jax.nn.softplus(ab + dtb)) for ab in abs_]
    gcum_ts = [g.T for g in gcums]
    sigs = [jax.nn.sigmoid(ab) for ab in abs_]
    chains = [(j, h) for j in range(len(abs_)) for h in range(DN_HEADS)]
    gc = [gcums[j][:, h:h + 1] for j, h in chains]
    decay = [jnp.where(causal, jnp.exp(jnp.where(causal, gc[n] - gcum_ts[j][h:h + 1, :], 0.0)), 0.0)
             for n, (j, h) in enumerate(chains)]
    beta = [sigs[j][:, DN_HEADS + h:DN_HEADS + h + 1] for j, h in chains]
    kb = [ks[j][h] * beta[n] for n, (j, h) in enumerate(chains)]
    pw = [-jnp.where(strict, bdot_nt(kb[n], ks[j][h]) * decay[n], 0.0) for n, (j, h) in enumerate(chains)]
    inv = [eye + m for m in pw]
    for _ in range(5):
        pw = [hdot(m, m) for m in pw]
        inv = [a + hdot(a, m) for a, m in zip(inv, pw)]
    egc = [jnp.exp(g) for g in gc]
    values = [hdot(inv[n], vs[j][h] * beta[n]) for n, (j, h) in enumerate(chains)]
    k_cds = [hdot(inv[n], kb[n] * egc[n]) for n in range(len(chains))]
    attns = [bdot_nt(qs[j][h], ks[j][h]) * decay[n] for n, (j, h) in enumerate(chains)]
    q_decs = [qs[j][h] * egc[n] for n, (j, h) in enumerate(chains)]
    k_decs = [ks[j][h] * jnp.exp(gc[n][c - 1:c, :] - gc[n]) for n, (j, h) in enumerate(chains)]

    def nest(flat):
        return [flat[j * DN_HEADS:(j + 1) * DN_HEADS] for j in range(len(abs_))]

    return nest(values), nest(k_cds), nest(attns), nest(q_decs), nest(k_decs), [jnp.exp(g[c - 1:c, :]) for g in gcums]


def _dn_step(values, k_cds, attns, q_decs, k_decs, lasts, ggs, sts, ng):
    v_new = [v - bdot(kc, st) for v, kc, st in zip(values, k_cds, sts)]
    o = [bdot(qd, st) for qd, st in zip(q_decs, sts)]
    o = [a + bdot(at, vn) for a, at, vn in zip(o, attns, v_new)]
    new = [st * la + bdot_tn(kd, vn) for st, la, kd, vn in zip(sts, lasts, k_decs, v_new)]
    return [_rms(a, ng) * jax.nn.silu(g) for a, g in zip(o, ggs)], new


def _sg_chunk(u, v, gate, lng, lnb, ws, bt):
    n = SG_CHUNK
    ug = jax.nn.gelu(u)
    vn = _layer_norm(jax.nn.gelu(v), lng, lnb)
    causal = lax.broadcasted_iota(jnp.int32, (n, n), 0) >= lax.broadcasted_iota(jnp.int32, (n, n), 1)
    lane = lax.broadcasted_iota(jnp.int32, (n, D_SG), 1)
    s = jnp.zeros((n, D_SG), f32)
    for h in range(SG_HEADS):
        t = bdot(jnp.where(causal, ws[h], 0.0), vn) + bt[:, h:h + 1]
        s = s + jnp.where((lane >= h * SG_HEAD_DIM) & (lane < (h + 1) * SG_HEAD_DIM), t, 0.0)
    return ug * s * jax.nn.silu(gate)


def _cp(n_grid, vmem=None):
    return pltpu.CompilerParams(dimension_semantics=("arbitrary",) * n_grid, vmem_limit_bytes=vmem)


def _full(shape):
    nd = len(shape)
    return pl.BlockSpec(tuple(shape), lambda *_: (0,) * nd)


def _rows(tm, ncol):
    return pl.BlockSpec((tm, ncol), lambda i: (i, 0))


def _sds(shape, dtype=f32):
    return jax.ShapeDtypeStruct(tuple(shape), dtype)


def _acc(ref, val, first):
    @pl.when(first)
    def _():
        ref[...] = val

    @pl.when(jnp.logical_not(first))
    def _():
        ref[...] += val


def in_fwd(x, g, w, name):
    t, tm = x.shape[0], 256

    def body(x_ref, g_ref, w_ref, h_ref, *z_refs):
        h = _rms(x_ref[...], g_ref[...]).astype(_MXU)
        h_ref[...] = h
        for z_ref, (a, b) in zip(z_refs, Z_COLS):
            z_ref[...] = jnp.dot(h, w_ref[:, a:b], preferred_element_type=f32)

    widths = [b - a for a, b in Z_COLS]
    return pl.pallas_call(
        body, name=name, grid=(t // tm,),
        in_specs=[_rows(tm, D_MODEL), _full((1, D_MODEL)), _full((D_MODEL, D_IN_PAD))],
        out_specs=[_rows(tm, D_MODEL)] + [_rows(tm, n) for n in widths],
        out_shape=[_sds((t, D_MODEL), _MXU)] + [_sds((t, n)) for n in widths],
        compiler_params=_cp(1, VMEM_BIG),
    )(x, g, w)


def in_bwd(x, g, w, dzs, dres, name):
    t, tm = x.shape[0], 256
    widths = [b - a for a, b in Z_COLS]

    def body(x_ref, g_ref, w_ref, dres_ref, *rest):
        dz_refs, (dx_ref, dg_ref) = rest[:5], rest[5:]
        dh = jnp.zeros((tm, D_MODEL), f32)
        for dz_ref, (a, b) in zip(dz_refs, Z_COLS):
            dh = dh + _mm_nt(dz_ref[...], w_ref[:, a:b])
        _, vj = jax.vjp(_rms, x_ref[...], g_ref[...])
        dx, dg = vj(dh)
        dx_ref[...] = dres_ref[...] + dx
        _acc(dg_ref, dg, pl.program_id(0) == 0)

    return pl.pallas_call(
        body, name=name, grid=(t // tm,),
        in_specs=[_rows(tm, D_MODEL), _full((1, D_MODEL)), _full((D_MODEL, D_IN_PAD)), _rows(tm, D_MODEL)]
        + [_rows(tm, n) for n in widths],
        out_specs=[_rows(tm, D_MODEL), _full((1, D_MODEL))],
        out_shape=[_sds((t, D_MODEL)), _sds((1, D_MODEL))],
        compiler_params=_cp(1, VMEM_BIG),
    )(x, g, w, dres, *dzs)


def wgrad(a, g, name):
    t, k = a.shape
    n = g.shape[1]
    tm = 512
    tn = n if n <= 768 else (768 if n % 768 == 0 else 512)
    steps = t // tm

    def body(a_ref, g_ref, o_ref, acc):
        i = pl.program_id(1)
        _acc(acc, _mm_tn(a_ref[...], g_ref[...]), i == 0)

        @pl.when(i == steps - 1)
        def _():
            o_ref[...] = acc[...].astype(o_ref.dtype)

    return pl.pallas_call(
        body, name=name, grid=(n // tn, steps),
        in_specs=[pl.BlockSpec((tm, k), lambda j, i: (i, 0)), pl.BlockSpec((tm, tn), lambda j, i: (i, j))],
        out_specs=pl.BlockSpec((k, tn), lambda j, i: (0, j)),
        out_shape=_sds((k, n), _COMM),
        scratch_shapes=[pltpu.VMEM((k, tn), f32)],
        compiler_params=_cp(2, VMEM_BIG),
    )(a, g)


def post_fwd(x, ys, yd, yg, p, wout, pg, wgate, wple, name):
    t, tm = x.shape[0], 256

    def body(x_ref, ys_ref, yd_ref, yg_ref, p_ref, wout_ref, pg_ref, wgate_ref, wple_ref,
             x2_ref, x1_ref, y_ref, hn_ref):
        y = jnp.concatenate([ys_ref[...], yd_ref[...], yg_ref[...]], axis=1).astype(_MXU)
        y_ref[...] = y
        x1 = x_ref[...] + jnp.dot(y, wout_ref[...], preferred_element_type=f32)
        x1_ref[...] = x1
        hn = _rms(x1, pg_ref[...]).astype(_MXU)
        hn_ref[...] = hn
        gp = jnp.dot(hn, wgate_ref[...], preferred_element_type=f32)
        pp = _mm(p_ref[...], wple_ref[...])
        x2_ref[...] = x1 + jax.nn.sigmoid(gp) * pp

    return pl.pallas_call(
        body, name=name, grid=(t // tm,),
        in_specs=[_rows(tm, D_MODEL), _rows(tm, D_SSM), _rows(tm, D_DN), _rows(tm, D_SG), _rows(tm, D_PLE),
                  _full((D_MODEL, D_MODEL)), _full((1, D_MODEL)), _full((D_MODEL, D_MODEL)), _full((D_PLE, D_MODEL))],
        out_specs=[_rows(tm, D_MODEL)] * 4,
        out_shape=[_sds((t, D_MODEL)), _sds((t, D_MODEL)), _sds((t, D_MODEL), _MXU), _sds((t, D_MODEL), _MXU)],
        compiler_params=_cp(1, VMEM_BIG),
    )(x, ys, yd, yg, p, wout, pg, wgate, wple)


def post_bwd(dx2, x1, hn, p, wout, pg, wgate, wple, name):
    t, tm = dx2.shape[0], 256

    def body(dx2_ref, x1_ref, hn_ref, p_ref, wout_ref, pg_ref, wgate_ref, wple_ref,
             dx1_ref, dgp_ref, dpp_ref, dys_ref, dyd_ref, dyg_ref, dpg_ref):
        dx2 = dx2_ref[...]
        gp = jnp.dot(hn_ref[...], wgate_ref[...], preferred_element_type=f32)
        pp = _mm(p_ref[...], wple_ref[...])
        sg = jax.nn.sigmoid(gp)
        dpp_ref[...] = (dx2 * sg).astype(_MXU)
        dgp = (dx2 * pp * sg * (1.0 - sg)).astype(_MXU)
        dgp_ref[...] = dgp
        dhn = _mm_nt(dgp, wgate_ref[...])
        _, vj = jax.vjp(_rms, x1_ref[...], pg_ref[...])
        dx1n, dpg = vj(dhn)
        dx1 = dx2 + dx1n
        dx1_ref[...] = dx1
        dy = _mm_nt(dx1, wout_ref[...])
        dys_ref[...] = dy[:, :D_SSM]
        dyd_ref[...] = dy[:, D_SSM:D_SSM + D_DN]
        dyg_ref[...] = dy[:, D_SSM + D_DN:]
        _acc(dpg_ref, dpg, pl.program_id(0) == 0)

    return pl.pallas_call(
        body, name=name, grid=(t // tm,),
        in_specs=[_rows(tm, D_MODEL), _rows(tm, D_MODEL), _rows(tm, D_MODEL), _rows(tm, D_PLE),
                  _full((D_MODEL, D_MODEL)), _full((1, D_MODEL)), _full((D_MODEL, D_MODEL)), _full((D_PLE, D_MODEL))],
        out_specs=[_rows(tm, D_MODEL), _rows(tm, D_MODEL), _rows(tm, D_MODEL), _rows(tm, D_SSM), _rows(tm, D_DN),
                   _rows(tm, D_SG), _full((1, D_MODEL))],
        out_shape=[_sds((t, D_MODEL)), _sds((t, D_MODEL), _MXU), _sds((t, D_MODEL), _MXU), _sds((t, D_SSM)),
                   _sds((t, D_DN)), _sds((t, D_SG)), _sds((1, D_MODEL))],
        compiler_params=_cp(1, VMEM_BIG),
    )(dx2, x1, hn, p, wout, pg, wgate, wple)


def loss_fwd_bwd(x, fg, target, name):
    t, tm = x.shape[0], 512

    def body(x_ref, fg_ref, t_ref, loss_ref, dx_ref, dfg_ref):
        def f(xv, gv):
            err = _rms(xv, gv) - t_ref[...]
            return 0.5 * jnp.sum(jnp.mean(err * err, axis=-1))

        val, vj = jax.vjp(f, x_ref[...], fg_ref[...])
        dx, dfg = vj(jnp.ones((), f32))
        dx_ref[...] = dx
        first = pl.program_id(0) == 0
        _acc(dfg_ref, dfg, first)
        _acc(loss_ref, jnp.full((8, LANE), val, f32), first)

    return pl.pallas_call(
        body, name=name, grid=(t // tm,),
        in_specs=[_rows(tm, D_MODEL), _full((1, D_MODEL)), _rows(tm, D_MODEL)],
        out_specs=[_full((8, LANE)), _rows(tm, D_MODEL), _full((1, D_MODEL))],
        out_shape=[_sds((8, LANE)), _sds((t, D_MODEL)), _sds((1, D_MODEL))],
        compiler_params=_cp(1),
    )(x, fg, target)


def s5_prep_fwd(are, aim, ls, bre, bim, name):
    def body(are_ref, aim_ref, ls_ref, bre_ref, bim_ref, *outs):
        vals = _s5_prep(are_ref[...], aim_ref[...], ls_ref[...], bre_ref[...], bim_ref[...])
        for o, v in zip(outs, vals):
            o[...] = v

    shapes = [(S5_CHUNK, N_STATE)] * 2 + [(8, N_STATE)] * 2 + [(D_SSM, N_STATE)] * 2
    return pl.pallas_call(body, name=name, out_shape=[_sds(s) for s in shapes])(are, aim, ls, bre, bim)


def s5_prep_bwd(are, aim, ls, bre, bim, cts, name):
    def body(are_ref, aim_ref, ls_ref, bre_ref, bim_ref, *rest):
        ct_refs, outs = rest[:6], rest[6:]
        _, vj = jax.vjp(_s5_prep, are_ref[...], aim_ref[...], ls_ref[...], bre_ref[...], bim_ref[...])
        for o, v in zip(outs, vj(tuple(r[...] for r in ct_refs))):
            o[...] = v

    shapes = [(1, N_STATE)] * 3 + [(D_SSM, N_STATE)] * 2
    return pl.pallas_call(body, name=name, out_shape=[_sds(s) for s in shapes])(are, aim, ls, bre, bim, *cts)


_S5_PARAM_SHAPES = ((S5_CHUNK, N_STATE), (S5_CHUNK, N_STATE), (8, N_STATE), (8, N_STATE), (D_SSM, N_STATE),
                    (D_SSM, N_STATE), (N_STATE, D_SSM), (N_STATE, D_SSM), (1, D_SSM), (D_SSM, D_SSM), (1, D_SSM))


def s5_fwd(z, params, nb, name):
    t = z.shape[0]
    nc = t // nb // S5_CHUNK

    def body(z_ref, *rest):
        p_refs, (y_ref, hs_ref, hr_s, hi_s) = rest[:11], rest[11:]

        @pl.when(pl.program_id(1) == 0)
        def _():
            hr_s[...] = jnp.zeros_like(hr_s)
            hi_s[...] = jnp.zeros_like(hi_s)

        hr, hi = hr_s[...], hi_s[...]
        hs_ref[0, :, :N_STATE] = hr
        hs_ref[0, :, N_STATE:] = hi
        y, nhr, nhi = _s5_chunk(z_ref[:, :D_SSM], z_ref[:, D_SSM:], hr, hi, *[r[...] for r in p_refs])
        y_ref[...] = y
        hr_s[...] = nhr
        hi_s[...] = nhi

    return pl.pallas_call(
        body, name=name, grid=(nb, nc),
        in_specs=[pl.BlockSpec((S5_CHUNK, 2 * D_SSM), lambda b, c: (b * nc + c, 0))]
        + [_full(s) for s in _S5_PARAM_SHAPES],
        out_specs=[pl.BlockSpec((S5_CHUNK, D_SSM), lambda b, c: (b * nc + c, 0)),
                   pl.BlockSpec((1, 1, 2 * N_STATE), lambda b, c: (b * nc + c, 0, 0))],
        out_shape=[_sds((t, D_SSM)), _sds((nb * nc, 1, 2 * N_STATE))],
        scratch_shapes=[pltpu.VMEM((1, N_STATE), f32), pltpu.VMEM((1, N_STATE), f32)],
        compiler_params=_cp(2, VMEM_BIG),
    )(z, *params)


def s5_bwd(z, params, hs, dy, nb, name):
    t = z.shape[0]
    nc = t // nb // S5_CHUNK

    def body(z_ref, hs_ref, dy_ref, *rest):
        p_refs, dz_ref, dp_refs, (dhr_s, dhi_s) = rest[:11], rest[11], rest[12:23], rest[23:]

        @pl.when(pl.program_id(1) == 0)
        def _():
            dhr_s[...] = jnp.zeros_like(dhr_s)
            dhi_s[...] = jnp.zeros_like(dhi_s)

        prim = (z_ref[:, :D_SSM], z_ref[:, D_SSM:], hs_ref[0, :, :N_STATE], hs_ref[0, :, N_STATE:]) + tuple(
            r[...] for r in p_refs)
        _, vj = jax.vjp(_s5_chunk, *prim)
        cts = vj((dy_ref[...], dhr_s[...], dhi_s[...]))
        dz_ref[:, :D_SSM] = cts[0]
        dz_ref[:, D_SSM:] = cts[1]
        dhr_s[...] = cts[2]
        dhi_s[...] = cts[3]
        first = (pl.program_id(0) == 0) & (pl.program_id(1) == 0)
        for r, v in zip(dp_refs, cts[4:]):
            _acc(r, v, first)

    rev = lambda b, c: (b * nc + nc - 1 - c, 0)
    return pl.pallas_call(
        body, name=name, grid=(nb, nc),
        in_specs=[pl.BlockSpec((S5_CHUNK, 2 * D_SSM), rev),
                  pl.BlockSpec((1, 1, 2 * N_STATE), lambda b, c: (b * nc + nc - 1 - c, 0, 0)),
                  pl.BlockSpec((S5_CHUNK, D_SSM), rev)] + [_full(s) for s in _S5_PARAM_SHAPES],
        out_specs=[pl.BlockSpec((S5_CHUNK, 2 * D_SSM), rev)] + [_full(s) for s in _S5_PARAM_SHAPES],
        out_shape=[_sds((t, 2 * D_SSM))] + [_sds(s) for s in _S5_PARAM_SHAPES],
        scratch_shapes=[pltpu.VMEM((1, N_STATE), f32), pltpu.VMEM((1, N_STATE), f32)],
        compiler_params=_cp(2, VMEM_BIG),
    )(z, hs, dy, *params)


DN_PRE_ROWS = 256
DN_COLS = 3 * D_DN // LANE


def dn_pre_fwd(zq, convw, seq, name):
    t, tb = zq.shape[0], DN_PRE_ROWS
    per_seq = seq // tb

    def body(xc_ref, xp_ref, w_ref, o_ref):
        is_start = pl.program_id(0) % per_seq == 0
        for j in range(DN_COLS):
            cols = slice(j * LANE, (j + 1) * LANE)
            o_ref[:, cols] = _dn_pre(xc_ref[:, cols], xp_ref[:, cols], w_ref[0:1, cols], w_ref[1:2, cols],
                                     w_ref[2:3, cols], w_ref[3:4, cols], is_start, j)

    return pl.pallas_call(
        body, name=name, grid=(t // tb,),
        in_specs=[_rows(tb, 3 * D_DN), pl.BlockSpec((tb, 3 * D_DN), lambda i: (jnp.maximum(i - 1, 0), 0)),
                  _full((DN_CONV, 3 * D_DN))],
        out_specs=_rows(tb, 3 * D_DN),
        out_shape=_sds((t, 3 * D_DN)),
        compiler_params=_cp(1, VMEM_BIG),
    )(zq, zq, convw)


def dn_pre_bwd(zq, convw, dqkv, seq, name):
    t, tb = zq.shape[0], DN_PRE_ROWS
    nrow = t // tb
    per_seq = seq // tb

    def body(xc_ref, xp_ref, w_ref, d_ref, dx_ref, dw_ref, carry):
        step = pl.program_id(0)
        i = nrow - 1 - step

        @pl.when(step == 0)
        def _():
            carry[...] = jnp.zeros_like(carry)

        for j in range(DN_COLS):
            cols = slice(j * LANE, (j + 1) * LANE)
            fn = functools.partial(_dn_pre, is_start=i % per_seq == 0, col=j)
            _, vj = jax.vjp(fn, xc_ref[:, cols], xp_ref[:, cols], w_ref[0:1, cols], w_ref[1:2, cols],
                            w_ref[2:3, cols], w_ref[3:4, cols])
            dxc, dxp, dw0, dw1, dw2, dw3 = vj(d_ref[:, cols])
            dx_ref[:, cols] = dxc + carry[:, cols]
            carry[:, cols] = dxp
            for k, dw in enumerate((dw0, dw1, dw2, dw3)):
                @pl.when(step == 0)
                def _():
                    dw_ref[k:k + 1, cols] = dw

                @pl.when(step != 0)
                def _():
                    dw_ref[k:k + 1, cols] += dw

    rev = lambda s: (nrow - 1 - s, 0)
    return pl.pallas_call(
        body, name=name, grid=(nrow,),
        in_specs=[pl.BlockSpec((tb, 3 * D_DN), rev),
                  pl.BlockSpec((tb, 3 * D_DN), lambda s: (jnp.maximum(nrow - 2 - s, 0), 0)),
                  _full((DN_CONV, 3 * D_DN)), pl.BlockSpec((tb, 3 * D_DN), rev)],
        out_specs=[pl.BlockSpec((tb, 3 * D_DN), rev), _full((DN_CONV, 3 * D_DN))],
        out_shape=[_sds((t, 3 * D_DN)), _sds((DN_CONV, 3 * D_DN))],
        scratch_shapes=[pltpu.VMEM((tb, 3 * D_DN), f32)],
        compiler_params=_cp(1, VMEM_BIG),
    )(zq, zq, convw, dqkv)


DN_LOCAL_CHUNKS = 2
DN_ATTN = DN_HEADS * DN_CHUNK


def _dn_heads(ref, rows, base=0):
    return [ref[rows, base + h * DN_HEAD_DIM:base + (h + 1) * DN_HEAD_DIM] for h in range(DN_HEADS)]


def dn_local_fwd(qkv, ab, alog, dtb, name):
    t = qkv.shape[0]
    c, n = DN_CHUNK, DN_LOCAL_CHUNKS

    def body(qkv_ref, ab_ref, alog_ref, dtb_ref, val_ref, kcd_ref, attn_ref, qd_ref, kd_ref, el_ref):
        rows = [pl.ds(j * c, c) for j in range(n)]
        vals, kcds, attns, qds, kds, els = _dn_local(
            [_dn_heads(qkv_ref, r) for r in rows], [_dn_heads(qkv_ref, r, D_DN) for r in rows],
            [_dn_heads(qkv_ref, r, 2 * D_DN) for r in rows], [ab_ref[r, :] for r in rows], alog_ref[...], dtb_ref[...])
        for j, r in enumerate(rows):
            for h in range(DN_HEADS):
                lo, hi = h * DN_HEAD_DIM, (h + 1) * DN_HEAD_DIM
                val_ref[r, lo:hi] = vals[j][h]
                kcd_ref[r, lo:hi] = kcds[j][h]
                qd_ref[r, lo:hi] = qds[j][h]
                kd_ref[r, lo:hi] = kds[j][h]
                attn_ref[r, h * c:(h + 1) * c] = attns[j][h]
            el_ref[j] = els[j]

    wide = _rows(n * c, D_DN)
    return pl.pallas_call(
        body, name=name, grid=(t // (n * c),),
        in_specs=[_rows(n * c, 3 * D_DN), _rows(n * c, LANE), _full((1, LANE)), _full((1, LANE))],
        out_specs=[wide, wide, _rows(n * c, DN_ATTN), wide, wide, pl.BlockSpec((n, 1, LANE), lambda i: (i, 0, 0))],
        out_shape=[_sds((t, D_DN)), _sds((t, D_DN)), _sds((t, DN_ATTN)), _sds((t, D_DN)), _sds((t, D_DN)),
                   _sds((t // c, 1, LANE))],
        compiler_params=_cp(1),
    )(qkv, ab, alog, dtb)


def dn_local_bwd(qkv, ab, alog, dtb, cts, name):
    t = qkv.shape[0]
    c, n = DN_CHUNK, DN_LOCAL_CHUNKS

    def body(qkv_ref, ab_ref, alog_ref, dtb_ref, dval_ref, dkcd_ref, dattn_ref, dqd_ref, dkd_ref, del_ref,
             dqkv_ref, dab_ref, dalog_ref, ddtb_ref):
        rows = [pl.ds(j * c, c) for j in range(n)]
        _, vj = jax.vjp(_dn_local, [_dn_heads(qkv_ref, r) for r in rows], [_dn_heads(qkv_ref, r, D_DN) for r in rows],
                        [_dn_heads(qkv_ref, r, 2 * D_DN) for r in rows], [ab_ref[r, :] for r in rows], alog_ref[...],
                        dtb_ref[...])
        dattn = [[dattn_ref[r, h * c:(h + 1) * c] for h in range(DN_HEADS)] for r in rows]
        dq, dk, dv, dab, dalog, ddtb = vj(([_dn_heads(dval_ref, r) for r in rows], [_dn_heads(dkcd_ref, r) for r in rows],
                                           dattn, [_dn_heads(dqd_ref, r) for r in rows],
                                           [_dn_heads(dkd_ref, r) for r in rows], [del_ref[j] for j in range(n)]))
        for j, r in enumerate(rows):
            for h in range(DN_HEADS):
                lo, hi = h * DN_HEAD_DIM, (h + 1) * DN_HEAD_DIM
                dqkv_ref[r, lo:hi] = dq[j][h]
                dqkv_ref[r, D_DN + lo:D_DN + hi] = dk[j][h]
                dqkv_ref[r, 2 * D_DN + lo:2 * D_DN + hi] = dv[j][h]
            dab_ref[r, :] = dab[j]
        first = pl.program_id(0) == 0
        _acc(dalog_ref, dalog, first)
        _acc(ddtb_ref, ddtb, first)

    wide = _rows(n * c, D_DN)
    return pl.pallas_call(
        body, name=name, grid=(t // (n * c),),
        in_specs=[_rows(n * c, 3 * D_DN), _rows(n * c, LANE), _full((1, LANE)), _full((1, LANE)),
                  wide, wide, _rows(n * c, DN_ATTN), wide, wide, pl.BlockSpec((n, 1, LANE), lambda i: (i, 0, 0))],
        out_specs=[_rows(n * c, 3 * D_DN), _rows(n * c, LANE), _full((1, LANE)), _full((1, LANE))],
        out_shape=[_sds((t, 3 * D_DN)), _sds((t, LANE)), _sds((1, LANE)), _sds((1, LANE))],
        compiler_params=_cp(1),
    )(qkv, ab, alog, dtb, *cts)


def _seq_view(a, nb):
    return a.reshape((nb, a.shape[0] // nb) + a.shape[1:])


def _dn_chains(nb):
    return [(b, h) for b in range(nb) for h in range(DN_HEADS)]


def _dn_step_operands(val_ref, kcd_ref, attn_ref, qd_ref, kd_ref, el_ref, gg_ref, nb):
    chains = _dn_chains(nb)
    c = DN_CHUNK

    def wide(ref):
        return [ref[b, :, h * DN_HEAD_DIM:(h + 1) * DN_HEAD_DIM] for b, h in chains]

    return (wide(val_ref), wide(kcd_ref), [attn_ref[b, :, h * c:(h + 1) * c] for b, h in chains], wide(qd_ref),
            wide(kd_ref), [el_ref[b, 0, :, h:h + 1] for b, h in chains], wide(gg_ref))


def dn_scan_fwd(loc, gg, ng, nb, name):
    val, kcd, attn, qd, kd, el = loc
    t = val.shape[0]
    c = DN_CHUNK
    nc = t // nb // c
    ns = nb * DN_HEADS

    def body(val_ref, kcd_ref, attn_ref, qd_ref, kd_ref, el_ref, gg_ref, ng_ref, y_ref, ss_ref, st):
        @pl.when(pl.program_id(0) == 0)
        def _():
            st[...] = jnp.zeros_like(st)

        sts = [st[i] for i in range(ns)]
        for i in range(ns):
            ss_ref[0, i] = sts[i]
        ys, new = _dn_step(*_dn_step_operands(val_ref, kcd_ref, attn_ref, qd_ref, kd_ref, el_ref, gg_ref, nb), sts,
                           ng_ref[...])
        for i, (b, h) in enumerate(_dn_chains(nb)):
            y_ref[b, :, h * DN_HEAD_DIM:(h + 1) * DN_HEAD_DIM] = ys[i]
            st[i] = new[i]

    def blk(w):
        return pl.BlockSpec((nb, c, w), lambda k: (0, k, 0))

    el_spec = pl.BlockSpec((nb, 1, 1, LANE), lambda k: (0, k, 0, 0))
    y, ss = pl.pallas_call(
        body, name=name, grid=(nc,),
        in_specs=[blk(D_DN), blk(D_DN), blk(DN_ATTN), blk(D_DN), blk(D_DN), el_spec, blk(D_DN), _full((1, LANE))],
        out_specs=[blk(D_DN), pl.BlockSpec((1, ns, DN_HEAD_DIM, DN_HEAD_DIM), lambda k: (k, 0, 0, 0))],
        out_shape=[_sds((nb, t // nb, D_DN)), _sds((nc, ns, DN_HEAD_DIM, DN_HEAD_DIM))],
        scratch_shapes=[pltpu.VMEM((ns, DN_HEAD_DIM, DN_HEAD_DIM), f32)],
        compiler_params=_cp(1),
    )(_seq_view(val, nb), _seq_view(kcd, nb), _seq_view(attn, nb), _seq_view(qd, nb), _seq_view(kd, nb),
      el.reshape(nb, nc, 1, LANE), _seq_view(gg, nb), ng)
    return y.reshape(t, D_DN), ss


def dn_scan_bwd(loc, gg, ng, ss, dy, nb, name):
    val, kcd, attn, qd, kd, el = loc
    t = val.shape[0]
    c = DN_CHUNK
    nc = t // nb // c
    ns = nb * DN_HEADS

    def body(val_ref, kcd_ref, attn_ref, qd_ref, kd_ref, el_ref, gg_ref, ng_ref, ss_ref, dy_ref,
             dval_ref, dkcd_ref, dattn_ref, dqd_ref, dkd_ref, del_ref, dgg_ref, dng_ref, dst):
        @pl.when(pl.program_id(0) == 0)
        def _():
            dst[...] = jnp.zeros_like(dst)

        lane = lax.broadcasted_iota(jnp.int32, (1, LANE), 1)
        chains = _dn_chains(nb)
        _, vj = jax.vjp(_dn_step, *_dn_step_operands(val_ref, kcd_ref, attn_ref, qd_ref, kd_ref, el_ref, gg_ref, nb),
                        [ss_ref[0, i] for i in range(ns)], ng_ref[...])
        dys = [dy_ref[b, :, h * DN_HEAD_DIM:(h + 1) * DN_HEAD_DIM] for b, h in chains]
        dval, dkcd, dattn, dqd, dkd, dlast, dgg, ds, dng = vj((dys, [dst[i] for i in range(ns)]))
        del_rows = [jnp.zeros((1, LANE), f32) for _ in range(nb)]
        for i, (b, h) in enumerate(chains):
            cols = slice(h * DN_HEAD_DIM, (h + 1) * DN_HEAD_DIM)
            dval_ref[b, :, cols] = dval[i]
            dkcd_ref[b, :, cols] = dkcd[i]
            dattn_ref[b, :, h * c:(h + 1) * c] = dattn[i]
            dqd_ref[b, :, cols] = dqd[i]
            dkd_ref[b, :, cols] = dkd[i]
            dgg_ref[b, :, cols] = dgg[i]
            dst[i] = ds[i]
            del_rows[b] = del_rows[b] + jnp.where(lane == h, dlast[i], 0.0)
        for b in range(nb):
            del_ref[b, 0] = del_rows[b]
        _acc(dng_ref, dng, pl.program_id(0) == 0)

    def blk(w):
        return pl.BlockSpec((nb, c, w), lambda k: (0, nc - 1 - k, 0))

    el_spec = pl.BlockSpec((nb, 1, 1, LANE), lambda k: (0, nc - 1 - k, 0, 0))
    outs = pl.pallas_call(
        body, name=name, grid=(nc,),
        in_specs=[blk(D_DN), blk(D_DN), blk(DN_ATTN), blk(D_DN), blk(D_DN), el_spec, blk(D_DN), _full((1, LANE)),
                  pl.BlockSpec((1, ns, DN_HEAD_DIM, DN_HEAD_DIM), lambda k: (nc - 1 - k, 0, 0, 0)), blk(D_DN)],
        out_specs=[blk(D_DN), blk(D_DN), blk(DN_ATTN), blk(D_DN), blk(D_DN), el_spec, blk(D_DN), _full((1, LANE))],
        out_shape=[_sds((nb, t // nb, D_DN)), _sds((nb, t // nb, D_DN)), _sds((nb, t // nb, DN_ATTN)),
                   _sds((nb, t // nb, D_DN)), _sds((nb, t // nb, D_DN)), _sds((nb, nc, 1, LANE)),
                   _sds((nb, t // nb, D_DN)), _sds((1, LANE))],
        scratch_shapes=[pltpu.VMEM((ns, DN_HEAD_DIM, DN_HEAD_DIM), f32)],
        compiler_params=_cp(1),
    )(_seq_view(val, nb), _seq_view(kcd, nb), _seq_view(attn, nb), _seq_view(qd, nb), _seq_view(kd, nb),
      el.reshape(nb, nc, 1, LANE), _seq_view(gg, nb), ng, ss, _seq_view(dy, nb))
    dloc = [o.reshape((t,) + o.shape[2:]) for o in outs[:5]] + [outs[5].reshape(t // c, 1, LANE)]
    return dloc, outs[6].reshape(t, D_DN), outs[7]


SG_ROWS = 512


def sg_fwd(z, lng, lnb, w, bt, name):
    t = z.shape[0]

    def body(z_ref, lng_ref, lnb_ref, w_ref, bt_ref, y_ref):
        ws = [w_ref[h] for h in range(SG_HEADS)]
        for k in range(SG_ROWS // SG_CHUNK):
            r = pl.ds(k * SG_CHUNK, SG_CHUNK)
            y_ref[r, :] = _sg_chunk(z_ref[r, :D_SG], z_ref[r, D_SG:2 * D_SG], z_ref[r, 2 * D_SG:], lng_ref[...],
                                    lnb_ref[...], ws, bt_ref[...])

    return pl.pallas_call(
        body, name=name, grid=(t // SG_ROWS,),
        in_specs=[_rows(SG_ROWS, 3 * D_SG), _full((1, D_SG)), _full((1, D_SG)),
                  _full((SG_HEADS, SG_CHUNK, SG_CHUNK)), _full((SG_CHUNK, LANE))],
        out_specs=_rows(SG_ROWS, D_SG),
        out_shape=_sds((t, D_SG)),
        compiler_params=_cp(1),
    )(z, lng, lnb, w, bt)


def sg_bwd(z, lng, lnb, w, bt, dy, name):
    t = z.shape[0]

    def body(z_ref, lng_ref, lnb_ref, w_ref, bt_ref, dy_ref, dz_ref, dlng_ref, dlnb_ref, dw_ref, dbt_ref):
        ws = [w_ref[h] for h in range(SG_HEADS)]
        tot = None
        for k in range(SG_ROWS // SG_CHUNK):
            r = pl.ds(k * SG_CHUNK, SG_CHUNK)
            _, vj = jax.vjp(_sg_chunk, z_ref[r, :D_SG], z_ref[r, D_SG:2 * D_SG], z_ref[r, 2 * D_SG:], lng_ref[...],
                            lnb_ref[...], ws, bt_ref[...])
            du, dv, dgate, dlng, dlnb, dws, dbt = vj(dy_ref[r, :])
            dz_ref[r, :D_SG] = du
            dz_ref[r, D_SG:2 * D_SG] = dv
            dz_ref[r, 2 * D_SG:] = dgate
            part = [dlng, dlnb, dbt] + list(dws)
            tot = part if tot is None else [a + b for a, b in zip(tot, part)]
        first = pl.program_id(0) == 0
        _acc(dlng_ref, tot[0], first)
        _acc(dlnb_ref, tot[1], first)
        _acc(dbt_ref, tot[2], first)
        for h in range(SG_HEADS):
            @pl.when(first)
            def _():
                dw_ref[h] = tot[3 + h]

            @pl.when(jnp.logical_not(first))
            def _():
                dw_ref[h] += tot[3 + h]

    return pl.pallas_call(
        body, name=name, grid=(t // SG_ROWS,),
        in_specs=[_rows(SG_ROWS, 3 * D_SG), _full((1, D_SG)), _full((1, D_SG)),
                  _full((SG_HEADS, SG_CHUNK, SG_CHUNK)), _full((SG_CHUNK, LANE)), _rows(SG_ROWS, D_SG)],
        out_specs=[_rows(SG_ROWS, 3 * D_SG), _full((1, D_SG)), _full((1, D_SG)),
                   _full((SG_HEADS, SG_CHUNK, SG_CHUNK)), _full((SG_CHUNK, LANE))],
        out_shape=[_sds((t, 3 * D_SG)), _sds((1, D_SG)), _sds((1, D_SG)), _sds((SG_HEADS, SG_CHUNK, SG_CHUNK)),
                   _sds((SG_CHUNK, LANE))],
        compiler_params=_cp(1),
    )(z, lng, lnb, w, bt, dy)


def add_pairs(a_list, b_list, name):
    n = len(a_list)

    def body(*refs):
        for a_ref, b_ref, o_ref in zip(refs[:n], refs[n:2 * n], refs[2 * n:]):
            o_ref[...] = (a_ref[...].astype(f32) + b_ref[...].astype(f32)).astype(o_ref.dtype)

    return pl.pallas_call(
        body, name=name, out_shape=[_sds(a.shape, a.dtype) for a in a_list],
        compiler_params=pltpu.CompilerParams(vmem_limit_bytes=VMEM_BIG),
    )(*a_list, *b_list)


def sum_adamw(half, recv, w, m, v, name):
    _, r, c = w.shape
    tr = 256 if r % 256 == 0 else r

    def body(half_ref, recv_ref, w_ref, m_ref, v_ref, g_ref, d_ref, nm_ref, nv_ref):
        g = recv_ref[0].astype(f32)
        for k in range(1, N_CHIPS):
            g = g + recv_ref[k].astype(f32)
        wv = w_ref[...]
        nm = ADAM_B1 * m_ref[...] + (1.0 - ADAM_B1) * g
        nv = ADAM_B2 * v_ref[...] + (1.0 - ADAM_B2) * jnp.square(g)
        m_hat = nm / (1.0 - ADAM_B1 ** ADAM_STEP)
        v_hat = nv / (1.0 - ADAM_B2 ** ADAM_STEP)
        g_ref[...] = g
        d_ref[...] = -ADAM_LR * (m_hat / (jnp.sqrt(v_hat) + ADAM_EPS) + ADAM_WD * wv)
        nm_ref[...] = nm
        nv_ref[...] = nv

    own = pl.BlockSpec((None, tr, c), lambda i, h: (h[0], i, 0))
    return pl.pallas_call(
        body, name=name,
        grid_spec=pltpu.PrefetchScalarGridSpec(
            num_scalar_prefetch=1, grid=(r // tr,),
            in_specs=[pl.BlockSpec((N_CHIPS, tr, c), lambda i, h: (0, i, 0)), own, own, own], out_specs=[own] * 4),
        out_shape=[_sds((2, r, c))] * 4,
        compiler_params=_cp(1, VMEM_BIG),
    )(half, recv, w, m, v)


_ANY = pl.BlockSpec(memory_space=pl.ANY)
_MESH = pl.DeviceIdType.MESH


def _flip(v, bit):
    return 1 - v if bit else v


_CHIP_RELS = ((1, 0), (0, 1), (1, 1))


def _piece(ref, kind, q):
    if kind[0] == "slot":
        return ref.at[q]
    if kind[0] == "all":
        return ref
    _, axis, n = kind
    return ref.at[(slice(None),) * axis + (pl.ds(q * n, n),)]


def _piece_shape(shape, kind):
    if kind[0] == "slot":
        return tuple(shape[1:])
    if kind[0] == "all":
        return tuple(shape)
    _, axis, n = kind
    return tuple(shape[:axis]) + (n,) + tuple(shape[axis + 1:])


def gather_weights(shards, kinds, name):
    n = len(shards)

    def out_shape(s, kind):
        if kind[0] == "slot":
            return (N_CHIPS,) + tuple(s.shape)
        _, axis, w = kind
        return tuple(s.shape[:axis + 1]) + (N_CHIPS * w,) + tuple(s.shape[axis + 2:])

    def place(o_ref, kind, q, layer):
        if kind[0] == "slot":
            return o_ref.at[q, layer]
        return _piece(o_ref.at[layer], kind, q)

    def body(*refs):
        s_refs, o_refs = refs[:n], refs[n:2 * n]
        send_sems, recv_sems, fwd_send_sems, fwd_recv_sems = refs[2 * n:]
        x, y, c = lax.axis_index("x"), lax.axis_index("y"), lax.axis_index("c")
        mine = 2 * x + y
        sends, arrivals, forwards, fwd_arrivals = [], [], [], []
        for r, (fx, fy) in enumerate(_CHIP_RELS):
            px, py = _flip(x, fx), _flip(y, fy)
            peer = 2 * px + py
            for k in range(n):
                s = r * n + k
                sends.append(pltpu.make_async_remote_copy(
                    src_ref=s_refs[k].at[c], dst_ref=place(o_refs[k], kinds[k], mine, c), send_sem=send_sems.at[s],
                    recv_sem=recv_sems.at[s], device_id=(px, py, c), device_id_type=_MESH))
                arrivals.append(pltpu.make_async_remote_copy(
                    src_ref=s_refs[k].at[c], dst_ref=place(o_refs[k], kinds[k], peer, c), send_sem=send_sems.at[s],
                    recv_sem=recv_sems.at[s], device_id=(px, py, c), device_id_type=_MESH))
                block = place(o_refs[k], kinds[k], peer, c)
                forwards.append(pltpu.make_async_remote_copy(
                    src_ref=block, dst_ref=block, send_sem=fwd_send_sems.at[s], recv_sem=fwd_recv_sems.at[s],
                    device_id=(x, y, 1 - c), device_id_type=_MESH))
                other = place(o_refs[k], kinds[k], peer, 1 - c)
                fwd_arrivals.append(pltpu.make_async_remote_copy(
                    src_ref=other, dst_ref=other, send_sem=fwd_send_sems.at[s], recv_sem=fwd_recv_sems.at[s],
                    device_id=(x, y, 1 - c), device_id_type=_MESH))
        for cp in sends:
            cp.start()
        for arrived, fwd in zip(arrivals, forwards):
            arrived.wait_recv()
            fwd.start()
        for cp in fwd_arrivals:
            cp.wait_recv()
        for cp in sends + forwards:
            cp.wait_send()

    m = len(_CHIP_RELS) * n
    return pl.pallas_call(
        body, name=name, in_specs=[_ANY] * n, out_specs=[_ANY] * n,
        out_shape=[_sds(out_shape(s, k), s.dtype) for s, k in zip(shards, kinds)],
        scratch_shapes=[pltpu.SemaphoreType.DMA((m,))] * 4,
    )(*shards)


def exchange_halves(gs, name):
    n = len(gs)

    def body(*refs):
        g_refs, got_refs, (send_sems, recv_sems) = refs[:n], refs[n:2 * n], refs[2 * n:]
        x, y, c = lax.axis_index("x"), lax.axis_index("y"), lax.axis_index("c")
        swaps = [pltpu.make_async_remote_copy(
            src_ref=g_refs[k].at[1 - c], dst_ref=got_refs[k], send_sem=send_sems.at[k], recv_sem=recv_sems.at[k],
            device_id=(x, y, 1 - c), device_id_type=_MESH) for k in range(n)]
        for cp in swaps:
            cp.start()
        for cp in swaps:
            cp.wait()

    return pl.pallas_call(
        body, name=name, in_specs=[_ANY] * n, out_specs=[_ANY] * n,
        out_shape=[_sds(g.shape[1:], g.dtype) for g in gs],
        scratch_shapes=[pltpu.SemaphoreType.DMA((n,)), pltpu.SemaphoreType.DMA((n,))],
    )(*gs)


def reduce_to_chips(ts, kinds, name):
    n = len(ts)

    def body(*refs):
        t_refs, o_refs, (send_sems, recv_sems) = refs[:n], refs[n:2 * n], refs[2 * n:]
        x, y, c = lax.axis_index("x"), lax.axis_index("y"), lax.axis_index("c")
        mine = 2 * x + y
        sends, arrivals = [], []
        for r, (fx, fy) in enumerate(_CHIP_RELS):
            px, py = _flip(x, fx), _flip(y, fy)
            peer = 2 * px + py
            for k in range(n):
                s = r * n + k
                sends.append(pltpu.make_async_remote_copy(
                    src_ref=_piece(t_refs[k], kinds[k], peer), dst_ref=o_refs[k].at[mine], send_sem=send_sems.at[s],
                    recv_sem=recv_sems.at[s], device_id=(px, py, c), device_id_type=_MESH))
                arrivals.append(pltpu.make_async_remote_copy(
                    src_ref=_piece(t_refs[k], kinds[k], peer), dst_ref=o_refs[k].at[peer], send_sem=send_sems.at[s],
                    recv_sem=recv_sems.at[s], device_id=(px, py, c), device_id_type=_MESH))
        for cp in sends:
            cp.start()
        for cp in arrivals:
            cp.wait_recv()
        for cp in sends:
            cp.wait_send()

    m = len(_CHIP_RELS) * n
    return pl.pallas_call(
        body, name=name, in_specs=[_ANY] * n, out_specs=[_ANY] * n,
        out_shape=[_sds((N_CHIPS,) + _piece_shape(t.shape, k), t.dtype) for t, k in zip(ts, kinds)],
        scratch_shapes=[pltpu.SemaphoreType.DMA((m,)), pltpu.SemaphoreType.DMA((m,))],
    )(*ts)


def share_halves(rs, name):
    n = len(rs)

    def body(*refs):
        o_refs, (send_sems, recv_sems) = refs[n:2 * n], refs[2 * n:]
        x, y, c = lax.axis_index("x"), lax.axis_index("y"), lax.axis_index("c")
        swaps = [pltpu.make_async_remote_copy(
            src_ref=o_refs[k].at[c], dst_ref=o_refs[k].at[c], send_sem=send_sems.at[k], recv_sem=recv_sems.at[k],
            device_id=(x, y, 1 - c), device_id_type=_MESH) for k in range(n)]
        arrivals = [pltpu.make_async_remote_copy(
            src_ref=o_refs[k].at[c], dst_ref=o_refs[k].at[1 - c], send_sem=send_sems.at[k], recv_sem=recv_sems.at[k],
            device_id=(x, y, 1 - c), device_id_type=_MESH) for k in range(n)]
        for cp in swaps:
            cp.start()
        for cp in arrivals:
            cp.wait_recv()
        for cp in swaps:
            cp.wait_send()

    return pl.pallas_call(
        body, name=name, in_specs=[_ANY] * n, out_specs=[_ANY] * n,
        out_shape=[_sds(r.shape, r.dtype) for r in rs], input_output_aliases={k: k for k in range(n)},
        scratch_shapes=[pltpu.SemaphoreType.DMA((n,)), pltpu.SemaphoreType.DMA((n,))],
    )(*rs)


def _pack_rows(parts, mult, dtype):
    blocks = []
    for a in parts:
        flat = a.reshape(-1).astype(dtype)
        rows = -(-flat.shape[0] // LANE)
        blocks.append(jnp.pad(flat, (0, rows * LANE - flat.shape[0])).reshape(rows, LANE))
    buf = jnp.concatenate(blocks, axis=0)
    return jnp.pad(buf, ((0, -buf.shape[0] % mult), (0, 0)))


def _unpack_rows(buf, shapes):
    out, row = [], 0
    for s in shapes:
        n = 1
        for d in s:
            n *= d
        rows = -(-n // LANE)
        out.append(buf[row:row + rows].reshape(-1)[:n].reshape(s))
        row += rows
    return out


def _permuted_from_shards(shards):
    parts = []
    for lo, hi in GROUP_COLS:
        for q in range(N_CHIPS):
            a, b = max(lo, q * SHARD_COLS), min(hi, (q + 1) * SHARD_COLS)
            if a < b:
                parts.append(shards[q][..., a - q * SHARD_COLS:b - q * SHARD_COLS])
    pad = jnp.zeros(shards[0].shape[:-1] + (D_IN_PAD - D_IN,), shards[0].dtype)
    return jnp.concatenate(parts + [pad], axis=-1)


def _shards_from_groups(groups):
    in_order = sorted(range(len(GROUP_COLS)), key=lambda j: GROUP_COLS[j][0])
    shards = []
    for q in range(N_CHIPS):
        parts = []
        for j in in_order:
            lo, hi = GROUP_COLS[j]
            a, b = max(lo, q * SHARD_COLS), min(hi, (q + 1) * SHARD_COLS)
            if a < b:
                parts.append(groups[j][..., a - lo:b - lo])
        shards.append(jnp.concatenate(parts, axis=-1))
    return shards


def _expand_b(b):
    eye = jnp.eye(SSM_GROUPS, dtype=b.dtype)
    return jnp.einsum("gnc,gh->gchn", b, eye).reshape(D_SSM, N_STATE)


def _extract_b(e):
    return jnp.einsum("gcgn->gnc", e.reshape(SSM_GROUPS, SSM_GROUP, SSM_GROUPS, SSM_STATE))


def _expand_c(c):
    eye = jnp.eye(SSM_GROUPS, dtype=c.dtype)
    return jnp.einsum("gcn,gh->gnhc", c, eye).reshape(N_STATE, D_SSM)


def _extract_c(e):
    return jnp.einsum("gngc->gcn", e.reshape(SSM_GROUPS, SSM_STATE, SSM_GROUPS, SSM_GROUP))


def _lane_row(v):
    return jnp.pad(v, (0, LANE - v.shape[0])).reshape(1, LANE)


def _layer_params(w, l):
    return dict(
        norm_g=w["norm_g"][l][None], win=w["w_in_perm"][l], wout=w["w_out"][l].astype(_MXU),
        pg=w["ple_norm_g"][l][None], wgate=w["w_ple_gate"][l].astype(_MXU), wple=w["w_ple"][l].astype(_MXU),
        are=w["ssm_a_re"][l].reshape(1, N_STATE), aim=w["ssm_a_im"][l].reshape(1, N_STATE),
        ls=jnp.repeat(w["ssm_log_step"][l], SSM_STATE).reshape(1, N_STATE),
        bre=_expand_b(w["ssm_b_re"][l]), bim=_expand_b(w["ssm_b_im"][l]),
        cr=_expand_c(w["ssm_c_re"][l]), ci=_expand_c(w["ssm_c_im"][l]),
        dr=w["ssm_d"][l].reshape(1, D_SSM), wglu=w["ssm_w_glu"][l].astype(f32), bglu=w["ssm_b_glu"][l][None],
        convw=w["dn_conv_w"][l], alog=_lane_row(w["dn_a_log"][l]), dtb=_lane_row(w["dn_dt_bias"][l]),
        ng=w["dn_norm_g"][l][None],
        lng=w["sg_ln_g"][l][None], lnb=w["sg_ln_b"][l][None], sgw=w["sg_w"][l],
        bt=jnp.pad(w["sg_b"][l].T, ((0, 0), (0, LANE - SG_HEADS))),
    )


def _layer_fwd(x, p, lp, nb, tag):
    seq = x.shape[0] // nb
    h, zs, zq, zg, zsg, zab = in_fwd(x, lp["norm_g"], lp["win"], f"in_fwd{tag}")
    prep = s5_prep_fwd(lp["are"], lp["aim"], lp["ls"], lp["bre"], lp["bim"], f"s5_prep_fwd{tag}")
    s5p = tuple(prep) + (lp["cr"], lp["ci"], lp["dr"], lp["wglu"], lp["bglu"])
    ys, hs = s5_fwd(zs, s5p, nb, f"s5_fwd{tag}")
    qkv = dn_pre_fwd(zq, lp["convw"], seq, f"dn_pre_fwd{tag}")
    loc = dn_local_fwd(qkv, zab, lp["alog"], lp["dtb"], f"dn_local_fwd{tag}")
    yd, ss = dn_scan_fwd(loc, zg, lp["ng"], nb, f"dn_scan_fwd{tag}")
    yg = sg_fwd(zsg, lp["lng"], lp["lnb"], lp["sgw"], lp["bt"], f"sg_fwd{tag}")
    x2, x1, y, hn = post_fwd(x, ys, yd, yg, p, lp["wout"], lp["pg"], lp["wgate"], lp["wple"], f"post_fwd{tag}")
    saved = dict(x=x, h=h, zs=zs, zq=zq, zg=zg, zsg=zsg, zab=zab, s5p=s5p, hs=hs, qkv=qkv, loc=loc, ss=ss, x1=x1, y=y, hn=hn, p=p)
    return x2, saved


def _layer_bwd(dx2, sv, lp, nb, tag):
    seq = dx2.shape[0] // nb
    dx1, dgp, dpp, dys, dyd, dyg, dpg = post_bwd(dx2, sv["x1"], sv["hn"], sv["p"], lp["wout"], lp["pg"], lp["wgate"],
                                                 lp["wple"], f"post_bwd{tag}")
    g = {}
    g["w_out"] = wgrad(sv["y"], dx1, f"wgrad_out{tag}")
    g["w_ple_gate"] = wgrad(sv["hn"], dgp, f"wgrad_gate{tag}")
    g["w_ple"] = wgrad(sv["p"], dpp, f"wgrad_ple{tag}")
    g["ple_norm_g"] = dpg[0]
    dzsg, dlng, dlnb, dsgw, dbt = sg_bwd(sv["zsg"], lp["lng"], lp["lnb"], lp["sgw"], lp["bt"], dyg, f"sg_bwd{tag}")
    g["sg_ln_g"], g["sg_ln_b"], g["sg_w"], g["sg_b"] = dlng[0], dlnb[0], dsgw, dbt[:, :SG_HEADS].T
    dloc, dzg, dng = dn_scan_bwd(sv["loc"], sv["zg"], lp["ng"], sv["ss"], dyd, nb, f"dn_scan_bwd{tag}")
    dqkv, dzab, dalog, ddtb = dn_local_bwd(sv["qkv"], sv["zab"], lp["alog"], lp["dtb"], dloc, f"dn_local_bwd{tag}")
    dzq, dconv = dn_pre_bwd(sv["zq"], lp["convw"], dqkv, seq, f"dn_pre_bwd{tag}")
    g["dn_conv_w"], g["dn_a_log"], g["dn_dt_bias"], g["dn_norm_g"] = dconv, dalog[0, :DN_HEADS], ddtb[0, :DN_HEADS], dng[0]
    s5out = s5_bwd(sv["zs"], sv["s5p"], sv["hs"], dys, nb, f"s5_bwd{tag}")
    dzs, dprep, (dcr, dci, ddr, dwglu, dbglu) = s5out[0], s5out[1:7], s5out[7:]
    dare, daim, dls, dbre, dbim = s5_prep_bwd(lp["are"], lp["aim"], lp["ls"], lp["bre"], lp["bim"], dprep,
                                              f"s5_prep_bwd{tag}")
    g["ssm_a_re"] = dare.reshape(SSM_GROUPS, SSM_STATE)
    g["ssm_a_im"] = daim.reshape(SSM_GROUPS, SSM_STATE)
    g["ssm_log_step"] = dls.reshape(SSM_GROUPS, SSM_STATE).sum(axis=1)
    g["ssm_b_re"], g["ssm_b_im"] = _extract_b(dbre), _extract_b(dbim)
    g["ssm_c_re"], g["ssm_c_im"] = _extract_c(dcr), _extract_c(dci)
    g["ssm_d"] = ddr.reshape(SSM_GROUPS, SSM_GROUP)
    g["ssm_w_glu"], g["ssm_b_glu"] = dwglu, dbglu[0]
    dzs_all = (dzs, dzq, dzg, dzsg, dzab)
    dx, dng_in = in_bwd(sv["x"], lp["norm_g"], lp["win"], dzs_all, dx1, f"in_bwd{tag}")
    g["w_in_pieces"] = [wgrad(sv["h"], dz, f"wgrad_in{k}{tag}") for k, dz in enumerate(dzs_all)]
    g["norm_g"] = dng_in[0]
    return dx, g


def _local_step(x, p, target, w, nb):
    lps = [_layer_params(w, l) for l in range(DEPTH)]
    saved = []
    for l in range(DEPTH):
        x, sv = _layer_fwd(x, p[l], lps[l], nb, f"_l{l}")
        saved.append(sv)
    loss_blk, dx, dfg = loss_fwd_bwd(x, w["final_norm_g"][None], target, "loss")
    grads = [None] * DEPTH
    for l in reversed(range(DEPTH)):
        dx, grads[l] = _layer_bwd(dx, saved[l], lps[l], nb, f"_l{l}")
    out = {k: jnp.stack([grads[l][k] for l in range(DEPTH)]) for k in grads[0] if k != "w_in_pieces"}
    out["w_in_pieces"] = [grads[l]["w_in_pieces"] for l in range(DEPTH)]
    out["final_norm_g"] = dfg[0]
    return loss_blk[0, 0], dx, out


def kernel(x, p, norm_g, w_in, ssm_a_re, ssm_a_im, ssm_b_re, ssm_b_im, ssm_c_re, ssm_c_im, ssm_d, ssm_log_step, ssm_w_glu, ssm_b_glu, dn_conv_w, dn_a_log, dn_dt_bias, dn_norm_g, sg_ln_g, sg_ln_b, sg_w, sg_b, w_out, ple_norm_g, w_ple_gate, w_ple, final_norm_g, loss_target, m_norm_g, m_w_in, m_ssm_a_re, m_ssm_a_im, m_ssm_b_re, m_ssm_b_im, m_ssm_c_re, m_ssm_c_im, m_ssm_d, m_ssm_log_step, m_ssm_w_glu, m_ssm_b_glu, m_dn_conv_w, m_dn_a_log, m_dn_dt_bias, m_dn_norm_g, m_sg_ln_g, m_sg_ln_b, m_sg_w, m_sg_b, m_w_out, m_ple_norm_g, m_w_ple_gate, m_w_ple, m_final_norm_g, v_norm_g, v_w_in, v_ssm_a_re, v_ssm_a_im, v_ssm_b_re, v_ssm_b_im, v_ssm_c_re, v_ssm_c_im, v_ssm_d, v_ssm_log_step, v_ssm_w_glu, v_ssm_b_glu, v_dn_conv_w, v_dn_a_log, v_dn_dt_bias, v_dn_norm_g, v_sg_ln_g, v_sg_ln_b, v_sg_w, v_sg_b, v_w_out, v_ple_norm_g, v_w_ple_gate, v_w_ple, v_final_norm_g):
    args = locals()
    w = {n: args[n] for n in WEIGHTS}
    m = {n: args["m_" + n] for n in WEIGHTS}
    v = {n: args["v_" + n] for n in WEIGHTS}
    nb, seq = x.shape[0], x.shape[1]
    t = nb * seq

    full = _gather_full(w)
    loss_local, dx, grads = _local_step(x.reshape(t, D_MODEL), p.reshape(DEPTH, t, D_PLE),
                                        loss_target.reshape(t, D_MODEL), full, nb)
    outs = _reduce_and_update(grads, w, m, v)
    loss = lax.psum(loss_local, MESH_AXES)
    return (loss, dx.reshape(nb, seq, D_MODEL), *[outs[0][n] for n in WEIGHTS], *[outs[1][n] for n in WEIGHTS],
            *[outs[2][n] for n in WEIGHTS], *[outs[3][n] for n in WEIGHTS])


def _gather_full(w):
    sh_names = [n for n, _ in SHARDED]
    shards = [w[n] if n == "dn_conv_w" else w[n].astype(_COMM) for n in sh_names]
    gathered = gather_weights(shards, [k for _, k in SHARDED], "gather_weights")
    chip = 2 * lax.axis_index("x") + lax.axis_index("y")
    full = {n: w[n] for n in REPLICATED}
    for (n, kind), shard, got in zip(SHARDED, shards, gathered):
        if kind[0] == "slot":
            full[n] = lax.dynamic_update_index_in_dim(got, shard, chip, 0)
        else:
            full[n] = lax.dynamic_update_slice_in_dim(got, shard, chip * kind[2], axis=kind[1] + 1)
    slots = full.pop("w_in")
    full["w_in_perm"] = _permuted_from_shards([slots[q] for q in range(N_CHIPS)]).astype(_MXU)
    return full


def _reduce_and_update(grads, w, m, v):
    sh_names = [n for n, _ in SHARDED]
    sh_kinds = [k for _, k in SHARDED]

    def pack_small(d):
        buf = _pack_rows([d[n] for n in REPLICATED], 512, f32)
        return buf.reshape(2, buf.shape[0] // 2, LANE)

    grads["w_in"] = jnp.stack([jnp.stack(_shards_from_groups(pieces)) for pieces in grads["w_in_pieces"]])
    gs = [grads[n] if n == "dn_conv_w" else grads[n].astype(_COMM) for n in sh_names] + [pack_small(grads)]
    kinds = sh_kinds + [("all",)]
    core = lax.axis_index("c")
    chip = 2 * lax.axis_index("x") + lax.axis_index("y")
    got = exchange_halves(gs, "exchange_halves")
    sums = add_pairs([lax.dynamic_index_in_dim(g, core, 0, keepdims=False) for g in gs], got, "add_halves")
    parts = list(reduce_to_chips(sums, kinds, "reduce_to_chips"))
    for k, (kind, total) in enumerate(zip(kinds, sums)):
        if kind[0] == "slot":
            own = lax.dynamic_index_in_dim(total, chip, 0, keepdims=False)
        elif kind[0] == "win":
            own = lax.dynamic_slice_in_dim(total, chip * kind[2], kind[2], axis=kind[1])
        else:
            own = total
        parts[k] = lax.dynamic_update_index_in_dim(parts[k], own, chip, 0)
    states = [(w[n], m[n], v[n]) for n in sh_names] + [(pack_small(w), pack_small(m), pack_small(v))]
    half = core.astype(jnp.int32).reshape(1)
    results = []
    for n, part, (wn, mn, vn) in zip(sh_names + ["replicated"], parts, states):
        results += sum_adamw(half, part, wn, mn, vn, f"adamw_{n}")
    shared = share_halves(results, "share_halves")
    rep_shapes = [w[n].shape for n in REPLICATED]
    outs = []
    for j in range(4):
        d = {n: shared[4 * k + j] for k, n in enumerate(sh_names)}
        small = shared[4 * len(sh_names) + j]
        d.update(zip(REPLICATED, _unpack_rows(small.reshape(-1, LANE), rep_shapes)))
        outs.append(d)
    return outs
```

```python
import functools

import jax
import jax.numpy as jnp
from jax import lax
from jax.experimental import pallas as pl
from jax.experimental.pallas import tpu as pltpu

f32 = jnp.float32
bf16 = jnp.bfloat16

_MXU = bf16
_COMM = bf16
HIGH = lax.Precision.HIGH

D_MODEL = 1024
DEPTH = 2
D_PLE = 256
D_SSM = 256
D_DN = 512
D_SG = 256
SSM_GROUPS = 16
SSM_GROUP = 16
SSM_STATE = 64
N_STATE = SSM_GROUPS * SSM_STATE
DN_HEADS = 4
DN_HEAD_DIM = 128
DN_CONV = 4
DN_CHUNK = 64
SG_HEADS = 4
SG_HEAD_DIM = 64
SG_CHUNK = 128
S5_CHUNK = 256
S5_GROUP_ROWS = 8
EPS = 1e-6
D_IN = 3336
D_IN_PAD = 3456
LANE = 128

ADAM_LR = 0.001
ADAM_B1 = 0.9
ADAM_B2 = 0.999
ADAM_EPS = 1e-08
ADAM_WD = 0.01
ADAM_STEP = 10

N_CHIPS = 4
N_DEV = 8
MESH_AXES = ("x", "y", "c")

Z_COLS = ((0, 512), (512, 2048), (2048, 2560), (2560, 3328), (3328, 3456))

GROUP_COLS = ((0, 512), (512, 2048), (2056, 2568), (2568, 3336), (2048, 2056))
SHARD_COLS = D_IN // 4

SHARDED = (("w_in", ("slot",)), ("ssm_w_glu", ("win", 0, 64)), ("dn_conv_w", ("win", 1, 384)),
           ("w_out", ("win", 0, 256)), ("w_ple_gate", ("win", 0, 256)), ("w_ple", ("win", 1, 256)))
REPLICATED = ("norm_g", "ssm_a_re", "ssm_a_im", "ssm_b_re", "ssm_b_im", "ssm_c_re", "ssm_c_im", "ssm_d",
              "ssm_log_step", "ssm_b_glu", "dn_a_log", "dn_dt_bias", "dn_norm_g", "sg_ln_g", "sg_ln_b", "sg_w",
              "sg_b", "ple_norm_g", "final_norm_g")
WEIGHTS = ("norm_g", "w_in", "ssm_a_re", "ssm_a_im", "ssm_b_re", "ssm_b_im", "ssm_c_re", "ssm_c_im", "ssm_d",
           "ssm_log_step", "ssm_w_glu", "ssm_b_glu", "dn_conv_w", "dn_a_log", "dn_dt_bias", "dn_norm_g", "sg_ln_g",
           "sg_ln_b", "sg_w", "sg_b", "w_out", "ple_norm_g", "w_ple_gate", "w_ple", "final_norm_g")

VMEM_BIG = 56 * 1024 * 1024


def _mm(a, b):
    return jnp.dot(a.astype(_MXU), b.astype(_MXU), preferred_element_type=f32)


def _mm_nt(a, b):
    return lax.dot_general(a.astype(_MXU), b.astype(_MXU), (((1,), (1,)), ((), ())), preferred_element_type=f32)


def _mm_tn(a, b):
    return lax.dot_general(a.astype(_MXU), b.astype(_MXU), (((0,), (0,)), ((), ())), preferred_element_type=f32)


@jax.custom_vjp
def bdot(a, b):
    return _mm(a, b)


def _bdot_fwd(a, b):
    return _mm(a, b), (a, b)


def _bdot_bwd(res, g):
    a, b = res
    return _mm_nt(g, b).astype(a.dtype), _mm_tn(a, g).astype(b.dtype)


bdot.defvjp(_bdot_fwd, _bdot_bwd)


@jax.custom_vjp
def bdot_nt(a, b):
    return _mm_nt(a, b)


def _bdot_nt_fwd(a, b):
    return _mm_nt(a, b), (a, b)


def _bdot_nt_bwd(res, g):
    a, b = res
    return _mm(g, b).astype(a.dtype), _mm_tn(g, a).astype(b.dtype)


bdot_nt.defvjp(_bdot_nt_fwd, _bdot_nt_bwd)


@jax.custom_vjp
def bdot_tn(a, b):
    return _mm_tn(a, b)


def _bdot_tn_fwd(a, b):
    return _mm_tn(a, b), (a, b)


def _bdot_tn_bwd(res, g):
    a, b = res
    return _mm_nt(b, g).astype(a.dtype), _mm(a, g).astype(b.dtype)


bdot_tn.defvjp(_bdot_tn_fwd, _bdot_tn_bwd)


def hdot(a, b):
    return jnp.dot(a, b, precision=HIGH, preferred_element_type=f32)


def _unit_lower_inverses(ms):
    n = ms[0].shape[0]
    eye = (lax.broadcasted_iota(jnp.int32, (n, n), 0) == lax.broadcasted_iota(jnp.int32, (n, n), 1)).astype(f32)
    pw = [-m for m in ms]
    inv = [eye + p for p in pw]
    for _ in range(n.bit_length() - 2):
        pw = [hdot(p, p) for p in pw]
        inv = [a + hdot(a, p) for a, p in zip(inv, pw)]
    return inv


@jax.custom_vjp
def solve_unit_lower(ms, rhs):
    return [hdot(a, r) for a, r in zip(_unit_lower_inverses(ms), rhs)]


def _solve_unit_lower_fwd(ms, rhs):
    inv = _unit_lower_inverses(ms)
    xs = [hdot(a, r) for a, r in zip(inv, rhs)]
    return xs, (inv, xs)


def _solve_unit_lower_bwd(res, gs):
    inv, xs = res
    d_rhs = [lax.dot_general(a, g, (((0,), (0,)), ((), ())), precision=HIGH, preferred_element_type=f32)
             for a, g in zip(inv, gs)]
    d_ms = [-lax.dot_general(d, x, (((1,), (1,)), ((), ())), precision=HIGH, preferred_element_type=f32)
            for d, x in zip(d_rhs, xs)]
    return d_ms, d_rhs


solve_unit_lower.defvjp(_solve_unit_lower_fwd, _solve_unit_lower_bwd)


@functools.partial(jax.custom_vjp, nondiff_argnums=(1,))
def roll_rows(x, k):
    return pltpu.roll(x, k, 0)


def _roll_rows_fwd(x, k):
    return pltpu.roll(x, k, 0), None


def _roll_rows_bwd(k, _, g):
    return (pltpu.roll(g, g.shape[0] - k, 0),)


roll_rows.defvjp(_roll_rows_fwd, _roll_rows_bwd)


def _row_ids(shape):
    return lax.broadcasted_iota(jnp.int32, shape, 0)


def _rms(x, g):
    return x * lax.rsqrt(jnp.mean(x * x, axis=-1, keepdims=True) + EPS) * g


def _layer_norm(x, g, b):
    mu = jnp.mean(x, axis=-1, keepdims=True)
    xc = x - mu
    return xc * lax.rsqrt(jnp.mean(xc * xc, axis=-1, keepdims=True) + EPS) * g + b


def _s5_prep(are, aim, ls, bre, bim):
    step = jnp.exp(ls)
    mag = jnp.exp(are * step)
    lr = mag * jnp.cos(aim * step)
    li = mag * jnp.sin(aim * step)
    den = are * are + aim * aim
    nr, ni = lr - 1.0, li
    fr = (nr * are + ni * aim) / den
    fi = (ni * are - nr * aim) / den
    bbr = fr * bre - fi * bim
    bbi = fr * bim + fi * bre
    pr = jnp.broadcast_to(lr, (S5_GROUP_ROWS, N_STATE))
    pi = jnp.broadcast_to(li, (S5_GROUP_ROWS, N_STATE))
    d = 1
    while d < S5_GROUP_ROWS:
        keep = _row_ids(pr.shape) >= d
        sr, si = roll_rows(pr, d), roll_rows(pi, d)
        pr, pi = jnp.where(keep, pr * sr - pi * si, pr), jnp.where(keep, pr * si + pi * sr, pi)
        d *= 2
    return pr, pi, bbr, bbi


def _s5_chunk(u, gate, hr, hi, pr, pi, bbr, bbi, cr, ci, dr, wglu, bglu):
    n, grp = u.shape[0], S5_GROUP_ROWS
    xr = bdot(u, bbr)
    xi = bdot(u, bbi)
    sub = _row_ids(xr.shape) % grp
    d = 1
    while d < grp:
        lr, li = pr[d - 1:d], pi[d - 1:d]
        sr = jnp.where(sub >= d, roll_rows(xr, d), 0.0)
        si = jnp.where(sub >= d, roll_rows(xi, d), 0.0)
        xr, xi = xr + lr * sr - li * si, xi + lr * si + li * sr
        d *= 2
    outs_r, outs_i = [], []
    for g in range(n // grp):
        gr, gi = xr[g * grp:(g + 1) * grp], xi[g * grp:(g + 1) * grp]
        gr, gi = gr + pr * hr - pi * hi, gi + pr * hi + pi * hr
        hr, hi = gr[grp - 1:grp], gi[grp - 1:grp]
        outs_r.append(gr)
        outs_i.append(gi)
    xr = jnp.concatenate(outs_r, axis=0)
    xi = jnp.concatenate(outs_i, axis=0)
    y = bdot(xr, cr) - bdot(xi, ci) + dr * u
    y = jax.nn.gelu(y)
    y = y * jax.nn.sigmoid(bdot(y, wglu) + bglu)
    return y * jax.nn.silu(gate), hr, hi


def _dn_pre(xc, xp, w0, w1, w2, w3, is_start, col):
    xp = jnp.where(is_start, 0.0, xp)
    rows = _row_ids(xc.shape)
    acc = w3 * xc
    for d, w in ((1, w2), (2, w1), (3, w0)):
        acc = acc + w * jnp.where(rows >= d, roll_rows(xc, d), roll_rows(xp, d))
    y = jax.nn.silu(acc)
    nrm = y * lax.rsqrt(jnp.sum(y * y, axis=-1, keepdims=True) + EPS)
    nrm = nrm * jnp.where(col < DN_HEADS, DN_HEAD_DIM ** -0.5, 1.0)
    return jnp.where(col < 2 * DN_HEADS, nrm, y)


def _dn_local(qs, ks, vs, abs_, alog, dtb):
    c = DN_CHUNK
    ri = lax.broadcasted_iota(jnp.int32, (c, c), 0)
    ci = lax.broadcasted_iota(jnp.int32, (c, c), 1)
    causal, strict = ri >= ci, ri > ci
    tril = causal.astype(f32)
    gcums = [hdot(tril, -jnp.exp(alog) * jax.nn.softplus(ab + dtb)) for ab in abs_]
    gcum_ts = [g.T for g in gcums]
    sigs = [jax.nn.sigmoid(ab) for ab in abs_]
    chains = [(j, h) for j in range(len(abs_)) for h in range(DN_HEADS)]
    gc = [gcums[j][:, h:h + 1] for j, h in chains]
    decay = [jnp.where(causal, jnp.exp(jnp.where(causal, gc[n] - gcum_ts[j][h:h + 1, :], 0.0)), 0.0)
             for n, (j, h) in enumerate(chains)]
    beta = [sigs[j][:, DN_HEADS + h:DN_HEADS + h + 1] for j, h in chains]
    kb = [ks[j][h] * beta[n] for n, (j, h) in enumerate(chains)]
    ms = [jnp.where(strict, bdot_nt(kb[n], ks[j][h]) * decay[n], 0.0) for n, (j, h) in enumerate(chains)]
    egc = [jnp.exp(g) for g in gc]
    rhs = [jnp.concatenate([vs[j][h] * beta[n], kb[n] * egc[n]], axis=1) for n, (j, h) in enumerate(chains)]
    sol = solve_unit_lower(ms, rhs)
    values = [s[:, :DN_HEAD_DIM] for s in sol]
    k_cds = [s[:, DN_HEAD_DIM:] for s in sol]
    attns = [bdot_nt(qs[j][h], ks[j][h]) * decay[n] for n, (j, h) in enumerate(chains)]
    q_decs = [qs[j][h] * egc[n] for n, (j, h) in enumerate(chains)]
    k_decs = [ks[j][h] * jnp.exp(gc[n][c - 1:c, :] - gc[n]) for n, (j, h) in enumerate(chains)]

    def nest(flat):
        return [flat[j * DN_HEADS:(j + 1) * DN_HEADS] for j in range(len(abs_))]

    return nest(values), nest(k_cds), nest(attns), nest(q_decs), nest(k_decs), [jnp.exp(g[c - 1:c, :]) for g in gcums]


def _dn_step(values, k_cds, attns, q_decs, k_decs, lasts, ggs, sts, ng):
    v_new = [v - bdot(kc, st) for v, kc, st in zip(values, k_cds, sts)]
    o = [bdot(qd, st) for qd, st in zip(q_decs, sts)]
    o = [a + bdot(at, vn) for a, at, vn in zip(o, attns, v_new)]
    new = [st * la + bdot_tn(kd, vn) for st, la, kd, vn in zip(sts, lasts, k_decs, v_new)]
    return [_rms(a, ng) * jax.nn.silu(g) for a, g in zip(o, ggs)], new


def _sg_chunk(u, v, gate, lng, lnb, ws, bt):
    n = SG_CHUNK
    ug = jax.nn.gelu(u)
    vn = _layer_norm(jax.nn.gelu(v), lng, lnb)
    causal = lax.broadcasted_iota(jnp.int32, (n, n), 0) >= lax.broadcasted_iota(jnp.int32, (n, n), 1)
    lane = lax.broadcasted_iota(jnp.int32, (n, D_SG), 1)
    s = jnp.zeros((n, D_SG), f32)
    for h in range(SG_HEADS):
        t = bdot(jnp.where(causal, ws[h], 0.0), vn) + bt[:, h:h + 1]
        s = s + jnp.where((lane >= h * SG_HEAD_DIM) & (lane < (h + 1) * SG_HEAD_DIM), t, 0.0)
    return ug * s * jax.nn.silu(gate)


def _cp(n_grid, vmem=None):
    return pltpu.CompilerParams(dimension_semantics=("arbitrary",) * n_grid, vmem_limit_bytes=vmem)


def _full(shape):
    nd = len(shape)
    return pl.BlockSpec(tuple(shape), lambda *_: (0,) * nd)


def _rows(tm, ncol):
    return pl.BlockSpec((tm, ncol), lambda i: (i, 0))


def _sds(shape, dtype=f32):
    return jax.ShapeDtypeStruct(tuple(shape), dtype)


def _acc(ref, val, first):
    @pl.when(first)
    def _():
        ref[...] = val

    @pl.when(jnp.logical_not(first))
    def _():
        ref[...] += val


def in_fwd(x, g, w, name):
    t, tm = x.shape[0], 256

    def body(x_ref, g_ref, w_ref, h_ref, *z_refs):
        h = _rms(x_ref[...], g_ref[...]).astype(_MXU)
        h_ref[...] = h
        for z_ref, (a, b) in zip(z_refs, Z_COLS):
            z_ref[...] = jnp.dot(h, w_ref[:, a:b], preferred_element_type=f32)

    widths = [b - a for a, b in Z_COLS]
    return pl.pallas_call(
        body, name=name, grid=(t // tm,),
        in_specs=[_rows(tm, D_MODEL), _full((1, D_MODEL)), _full((D_MODEL, D_IN_PAD))],
        out_specs=[_rows(tm, D_MODEL)] + [_rows(tm, n) for n in widths],
        out_shape=[_sds((t, D_MODEL), _MXU)] + [_sds((t, n)) for n in widths],
        compiler_params=_cp(1, VMEM_BIG),
    )(x, g, w)


def in_bwd(x, g, w, dzs, dres, name):
    t, tm = x.shape[0], 256
    widths = [b - a for a, b in Z_COLS]

    def body(x_ref, g_ref, w_ref, dres_ref, *rest):
        dz_refs, (dx_ref, dg_ref) = rest[:5], rest[5:]
        dh = jnp.zeros((tm, D_MODEL), f32)
        for dz_ref, (a, b) in zip(dz_refs, Z_COLS):
            dh = dh + _mm_nt(dz_ref[...], w_ref[:, a:b])
        _, vj = jax.vjp(_rms, x_ref[...], g_ref[...])
        dx, dg = vj(dh)
        dx_ref[...] = dres_ref[...] + dx
        _acc(dg_ref, dg, pl.program_id(0) == 0)

    return pl.pallas_call(
        body, name=name, grid=(t // tm,),
        in_specs=[_rows(tm, D_MODEL), _full((1, D_MODEL)), _full((D_MODEL, D_IN_PAD)), _rows(tm, D_MODEL)]
        + [_rows(tm, n) for n in widths],
        out_specs=[_rows(tm, D_MODEL), _full((1, D_MODEL))],
        out_shape=[_sds((t, D_MODEL)), _sds((1, D_MODEL))],
        compiler_params=_cp(1, VMEM_BIG),
    )(x, g, w, dres, *dzs)


def wgrad(a, g, name):
    t, k = a.shape
    n = g.shape[1]
    tm = 512
    tn = n if n <= 768 else (768 if n % 768 == 0 else 512)
    steps = t // tm

    def body(a_ref, g_ref, o_ref, acc):
        i = pl.program_id(1)
        _acc(acc, _mm_tn(a_ref[...], g_ref[...]), i == 0)

        @pl.when(i == steps - 1)
        def _():
            o_ref[...] = acc[...].astype(o_ref.dtype)

    return pl.pallas_call(
        body, name=name, grid=(n // tn, steps),
        in_specs=[pl.BlockSpec((tm, k), lambda j, i: (i, 0)), pl.BlockSpec((tm, tn), lambda j, i: (i, j))],
        out_specs=pl.BlockSpec((k, tn), lambda j, i: (0, j)),
        out_shape=_sds((k, n), _COMM),
        scratch_shapes=[pltpu.VMEM((k, tn), f32)],
        compiler_params=_cp(2, VMEM_BIG),
    )(a, g)


def post_fwd(x, ys, yd, yg, p, wout, pg, wgate, wple, name):
    t, tm = x.shape[0], 256

    def body(x_ref, ys_ref, yd_ref, yg_ref, p_ref, wout_ref, pg_ref, wgate_ref, wple_ref,
             x2_ref, x1_ref, y_ref, hn_ref):
        y = jnp.concatenate([ys_ref[...], yd_ref[...], yg_ref[...]], axis=1).astype(_MXU)
        y_ref[...] = y
        x1 = x_ref[...] + jnp.dot(y, wout_ref[...], preferred_element_type=f32)
        x1_ref[...] = x1
        hn = _rms(x1, pg_ref[...]).astype(_MXU)
        hn_ref[...] = hn
        gp = jnp.dot(hn, wgate_ref[...], preferred_element_type=f32)
        pp = _mm(p_ref[...], wple_ref[...])
        x2_ref[...] = x1 + jax.nn.sigmoid(gp) * pp

    return pl.pallas_call(
        body, name=name, grid=(t // tm,),
        in_specs=[_rows(tm, D_MODEL), _rows(tm, D_SSM), _rows(tm, D_DN), _rows(tm, D_SG), _rows(tm, D_PLE),
                  _full((D_MODEL, D_MODEL)), _full((1, D_MODEL)), _full((D_MODEL, D_MODEL)), _full((D_PLE, D_MODEL))],
        out_specs=[_rows(tm, D_MODEL)] * 4,
        out_shape=[_sds((t, D_MODEL)), _sds((t, D_MODEL)), _sds((t, D_MODEL), _MXU), _sds((t, D_MODEL), _MXU)],
        compiler_params=_cp(1, VMEM_BIG),
    )(x, ys, yd, yg, p, wout, pg, wgate, wple)


def post_bwd(dx2, x1, hn, p, wout, pg, wgate, wple, name):
    t, tm = dx2.shape[0], 256

    def body(dx2_ref, x1_ref, hn_ref, p_ref, wout_ref, pg_ref, wgate_ref, wple_ref,
             dx1_ref, dgp_ref, dpp_ref, dys_ref, dyd_ref, dyg_ref, dpg_ref):
        dx2 = dx2_ref[...]
        gp = jnp.dot(hn_ref[...], wgate_ref[...], preferred_element_type=f32)
        pp = _mm(p_ref[...], wple_ref[...])
        sg = jax.nn.sigmoid(gp)
        dpp_ref[...] = (dx2 * sg).astype(_MXU)
        dgp = (dx2 * pp * sg * (1.0 - sg)).astype(_MXU)
        dgp_ref[...] = dgp
        dhn = _mm_nt(dgp, wgate_ref[...])
        _, vj = jax.vjp(_rms, x1_ref[...], pg_ref[...])
        dx1n, dpg = vj(dhn)
        dx1 = dx2 + dx1n
        dx1_ref[...] = dx1
        dy = _mm_nt(dx1, wout_ref[...])
        dys_ref[...] = dy[:, :D_SSM]
        dyd_ref[...] = dy[:, D_SSM:D_SSM + D_DN]
        dyg_ref[...] = dy[:, D_SSM + D_DN:]
        _acc(dpg_ref, dpg, pl.program_id(0) == 0)

    return pl.pallas_call(
        body, name=name, grid=(t // tm,),
        in_specs=[_rows(tm, D_MODEL), _rows(tm, D_MODEL), _rows(tm, D_MODEL), _rows(tm, D_PLE),
                  _full((D_MODEL, D_MODEL)), _full((1, D_MODEL)), _full((D_MODEL, D_MODEL)), _full((D_PLE, D_MODEL))],
        out_specs=[_rows(tm, D_MODEL), _rows(tm, D_MODEL), _rows(tm, D_MODEL), _rows(tm, D_SSM), _rows(tm, D_DN),
                   _rows(tm, D_SG), _full((1, D_MODEL))],
        out_shape=[_sds((t, D_MODEL)), _sds((t, D_MODEL), _MXU), _sds((t, D_MODEL), _MXU), _sds((t, D_SSM)),
                   _sds((t, D_DN)), _sds((t, D_SG)), _sds((1, D_MODEL))],
        compiler_params=_cp(1, VMEM_BIG),
    )(dx2, x1, hn, p, wout, pg, wgate, wple)


def loss_fwd_bwd(x, fg, target, name):
    t, tm = x.shape[0], 512

    def body(x_ref, fg_ref, t_ref, loss_ref, dx_ref, dfg_ref):
        def f(xv, gv):
            err = _rms(xv, gv) - t_ref[...]
            return 0.5 * jnp.sum(jnp.mean(err * err, axis=-1))

        val, vj = jax.vjp(f, x_ref[...], fg_ref[...])
        dx, dfg = vj(jnp.ones((), f32))
        dx_ref[...] = dx
        first = pl.program_id(0) == 0
        _acc(dfg_ref, dfg, first)
        _acc(loss_ref, jnp.full((8, LANE), val, f32), first)

    return pl.pallas_call(
        body, name=name, grid=(t // tm,),
        in_specs=[_rows(tm, D_MODEL), _full((1, D_MODEL)), _rows(tm, D_MODEL)],
        out_specs=[_full((8, LANE)), _rows(tm, D_MODEL), _full((1, D_MODEL))],
        out_shape=[_sds((8, LANE)), _sds((t, D_MODEL)), _sds((1, D_MODEL))],
        compiler_params=_cp(1),
    )(x, fg, target)


S5_PREPARED = 4
_S5_PARAM_SHAPES = ((S5_GROUP_ROWS, N_STATE), (S5_GROUP_ROWS, N_STATE), (D_SSM, N_STATE), (D_SSM, N_STATE),
                    (N_STATE, D_SSM), (N_STATE, D_SSM), (1, D_SSM), (D_SSM, D_SSM), (1, D_SSM))

def s5_prep_fwd(are, aim, ls, bre, bim, name):
    def body(are_ref, aim_ref, ls_ref, bre_ref, bim_ref, *outs):
        vals = _s5_prep(are_ref[...], aim_ref[...], ls_ref[...], bre_ref[...], bim_ref[...])
        for o, v in zip(outs, vals):
            o[...] = v

    return pl.pallas_call(body, name=name, out_shape=[_sds(s) for s in _S5_PARAM_SHAPES[:S5_PREPARED]])(
        are, aim, ls, bre, bim)


def s5_prep_bwd(are, aim, ls, bre, bim, cts, name):
    def body(are_ref, aim_ref, ls_ref, bre_ref, bim_ref, *rest):
        ct_refs, outs = rest[:S5_PREPARED], rest[S5_PREPARED:]
        _, vj = jax.vjp(_s5_prep, are_ref[...], aim_ref[...], ls_ref[...], bre_ref[...], bim_ref[...])
        for o, v in zip(outs, vj(tuple(r[...] for r in ct_refs))):
            o[...] = v

    shapes = [(1, N_STATE)] * 3 + [(D_SSM, N_STATE)] * 2
    return pl.pallas_call(body, name=name, out_shape=[_sds(s) for s in shapes])(are, aim, ls, bre, bim, *cts)


def s5_fwd(z, params, nb, name):
    t = z.shape[0]
    nc = t // nb // S5_CHUNK
    npar = len(_S5_PARAM_SHAPES)

    def body(z_ref, *rest):
        p_refs, (y_ref, hs_ref, hr_s, hi_s) = rest[:npar], rest[npar:]

        @pl.when(pl.program_id(1) == 0)
        def _():
            hr_s[...] = jnp.zeros_like(hr_s)
            hi_s[...] = jnp.zeros_like(hi_s)

        hr, hi = hr_s[...], hi_s[...]
        hs_ref[0, :, :N_STATE] = hr
        hs_ref[0, :, N_STATE:] = hi
        y, nhr, nhi = _s5_chunk(z_ref[:, :D_SSM], z_ref[:, D_SSM:], hr, hi, *[r[...] for r in p_refs])
        y_ref[...] = y
        hr_s[...] = nhr
        hi_s[...] = nhi

    return pl.pallas_call(
        body, name=name, grid=(nb, nc),
        in_specs=[pl.BlockSpec((S5_CHUNK, 2 * D_SSM), lambda b, c: (b * nc + c, 0))]
        + [_full(s) for s in _S5_PARAM_SHAPES],
        out_specs=[pl.BlockSpec((S5_CHUNK, D_SSM), lambda b, c: (b * nc + c, 0)),
                   pl.BlockSpec((1, 1, 2 * N_STATE), lambda b, c: (b * nc + c, 0, 0))],
        out_shape=[_sds((t, D_SSM)), _sds((nb * nc, 1, 2 * N_STATE))],
        scratch_shapes=[pltpu.VMEM((1, N_STATE), f32), pltpu.VMEM((1, N_STATE), f32)],
        compiler_params=_cp(2, VMEM_BIG),
    )(z, *params)


def s5_bwd(z, params, hs, dy, nb, name):
    t = z.shape[0]
    nc = t // nb // S5_CHUNK
    npar = len(_S5_PARAM_SHAPES)

    def body(z_ref, hs_ref, dy_ref, *rest):
        p_refs, dz_ref, dp_refs, (dhr_s, dhi_s) = rest[:npar], rest[npar], rest[npar + 1:2 * npar + 1], rest[2 * npar + 1:]

        @pl.when(pl.program_id(1) == 0)
        def _():
            dhr_s[...] = jnp.zeros_like(dhr_s)
            dhi_s[...] = jnp.zeros_like(dhi_s)

        prim = (z_ref[:, :D_SSM], z_ref[:, D_SSM:], hs_ref[0, :, :N_STATE], hs_ref[0, :, N_STATE:]) + tuple(
            r[...] for r in p_refs)
        _, vj = jax.vjp(_s5_chunk, *prim)
        cts = vj((dy_ref[...], dhr_s[...], dhi_s[...]))
        dz_ref[:, :D_SSM] = cts[0]
        dz_ref[:, D_SSM:] = cts[1]
        dhr_s[...] = cts[2]
        dhi_s[...] = cts[3]
        first = (pl.program_id(0) == 0) & (pl.program_id(1) == 0)
        for r, v in zip(dp_refs, cts[4:]):
            _acc(r, v, first)

    rev = lambda b, c: (b * nc + nc - 1 - c, 0)
    return pl.pallas_call(
        body, name=name, grid=(nb, nc),
        in_specs=[pl.BlockSpec((S5_CHUNK, 2 * D_SSM), rev),
                  pl.BlockSpec((1, 1, 2 * N_STATE), lambda b, c: (b * nc + nc - 1 - c, 0, 0)),
                  pl.BlockSpec((S5_CHUNK, D_SSM), rev)] + [_full(s) for s in _S5_PARAM_SHAPES],
        out_specs=[pl.BlockSpec((S5_CHUNK, 2 * D_SSM), rev)] + [_full(s) for s in _S5_PARAM_SHAPES],
        out_shape=[_sds((t, 2 * D_SSM))] + [_sds(s) for s in _S5_PARAM_SHAPES],
        scratch_shapes=[pltpu.VMEM((1, N_STATE), f32), pltpu.VMEM((1, N_STATE), f32)],
        compiler_params=_cp(2, VMEM_BIG),
    )(z, hs, dy, *params)


DN_PRE_ROWS = 256
DN_COLS = 3 * D_DN // LANE


def dn_pre_fwd(zq, convw, seq, name):
    t, tb = zq.shape[0], DN_PRE_ROWS
    per_seq = seq // tb

    def body(xc_ref, xp_ref, w_ref, o_ref):
        is_start = pl.program_id(0) % per_seq == 0
        for j in range(DN_COLS):
            cols = slice(j * LANE, (j + 1) * LANE)
            o_ref[:, cols] = _dn_pre(xc_ref[:, cols], xp_ref[:, cols], w_ref[0:1, cols], w_ref[1:2, cols],
                                     w_ref[2:3, cols], w_ref[3:4, cols], is_start, j)

    return pl.pallas_call(
        body, name=name, grid=(t // tb,),
        in_specs=[_rows(tb, 3 * D_DN), pl.BlockSpec((tb, 3 * D_DN), lambda i: (jnp.maximum(i - 1, 0), 0)),
                  _full((DN_CONV, 3 * D_DN))],
        out_specs=_rows(tb, 3 * D_DN),
        out_shape=_sds((t, 3 * D_DN)),
        compiler_params=_cp(1, VMEM_BIG),
    )(zq, zq, convw)


def dn_pre_bwd(zq, convw, dqkv, seq, name):
    t, tb = zq.shape[0], DN_PRE_ROWS
    nrow = t // tb
    per_seq = seq // tb

    def body(xc_ref, xp_ref, w_ref, d_ref, dx_ref, dw_ref, carry):
        step = pl.program_id(0)
        i = nrow - 1 - step

        @pl.when(step == 0)
        def _():
            carry[...] = jnp.zeros_like(carry)

        for j in range(DN_COLS):
            cols = slice(j * LANE, (j + 1) * LANE)
            fn = functools.partial(_dn_pre, is_start=i % per_seq == 0, col=j)
            _, vj = jax.vjp(fn, xc_ref[:, cols], xp_ref[:, cols], w_ref[0:1, cols], w_ref[1:2, cols],
                            w_ref[2:3, cols], w_ref[3:4, cols])
            dxc, dxp, dw0, dw1, dw2, dw3 = vj(d_ref[:, cols])
            dx_ref[:, cols] = dxc + carry[:, cols]
            carry[:, cols] = dxp
            for k, dw in enumerate((dw0, dw1, dw2, dw3)):
                @pl.when(step == 0)
                def _():
                    dw_ref[k:k + 1, cols] = dw

                @pl.when(step != 0)
                def _():
                    dw_ref[k:k + 1, cols] += dw

    rev = lambda s: (nrow - 1 - s, 0)
    return pl.pallas_call(
        body, name=name, grid=(nrow,),
        in_specs=[pl.BlockSpec((tb, 3 * D_DN), rev),
                  pl.BlockSpec((tb, 3 * D_DN), lambda s: (jnp.maximum(nrow - 2 - s, 0), 0)),
                  _full((DN_CONV, 3 * D_DN)), pl.BlockSpec((tb, 3 * D_DN), rev)],
        out_specs=[pl.BlockSpec((tb, 3 * D_DN), rev), _full((DN_CONV, 3 * D_DN))],
        out_shape=[_sds((t, 3 * D_DN)), _sds((DN_CONV, 3 * D_DN))],
        scratch_shapes=[pltpu.VMEM((tb, 3 * D_DN), f32)],
        compiler_params=_cp(1, VMEM_BIG),
    )(zq, zq, convw, dqkv)


DN_LOCAL_CHUNKS = 2
DN_ATTN = DN_HEADS * DN_CHUNK


def _dn_heads(ref, rows, base=0):
    return [ref[rows, base + h * DN_HEAD_DIM:base + (h + 1) * DN_HEAD_DIM] for h in range(DN_HEADS)]


def dn_local_fwd(qkv, ab, alog, dtb, name):
    t = qkv.shape[0]
    c, n = DN_CHUNK, DN_LOCAL_CHUNKS

    def body(qkv_ref, ab_ref, alog_ref, dtb_ref, val_ref, kcd_ref, attn_ref, qd_ref, kd_ref, el_ref):
        rows = [pl.ds(j * c, c) for j in range(n)]
        vals, kcds, attns, qds, kds, els = _dn_local(
            [_dn_heads(qkv_ref, r) for r in rows], [_dn_heads(qkv_ref, r, D_DN) for r in rows],
            [_dn_heads(qkv_ref, r, 2 * D_DN) for r in rows], [ab_ref[r, :] for r in rows], alog_ref[...], dtb_ref[...])
        for j, r in enumerate(rows):
            for h in range(DN_HEADS):
                lo, hi = h * DN_HEAD_DIM, (h + 1) * DN_HEAD_DIM
                val_ref[r, lo:hi] = vals[j][h]
                kcd_ref[r, lo:hi] = kcds[j][h]
                qd_ref[r, lo:hi] = qds[j][h]
                kd_ref[r, lo:hi] = kds[j][h]
                attn_ref[r, h * c:(h + 1) * c] = attns[j][h]
            el_ref[j] = els[j]

    wide = _rows(n * c, D_DN)
    return pl.pallas_call(
        body, name=name, grid=(t // (n * c),),
        in_specs=[_rows(n * c, 3 * D_DN), _rows(n * c, LANE), _full((1, LANE)), _full((1, LANE))],
        out_specs=[wide, wide, _rows(n * c, DN_ATTN), wide, wide, pl.BlockSpec((n, 1, LANE), lambda i: (i, 0, 0))],
        out_shape=[_sds((t, D_DN)), _sds((t, D_DN)), _sds((t, DN_ATTN)), _sds((t, D_DN)), _sds((t, D_DN)),
                   _sds((t // c, 1, LANE))],
        compiler_params=_cp(1),
    )(qkv, ab, alog, dtb)


def dn_local_bwd(qkv, ab, alog, dtb, cts, name):
    t = qkv.shape[0]
    c, n = DN_CHUNK, DN_LOCAL_CHUNKS

    def body(qkv_ref, ab_ref, alog_ref, dtb_ref, dval_ref, dkcd_ref, dattn_ref, dqd_ref, dkd_ref, del_ref,
             dqkv_ref, dab_ref, dalog_ref, ddtb_ref):
        rows = [pl.ds(j * c, c) for j in range(n)]
        _, vj = jax.vjp(_dn_local, [_dn_heads(qkv_ref, r) for r in rows], [_dn_heads(qkv_ref, r, D_DN) for r in rows],
                        [_dn_heads(qkv_ref, r, 2 * D_DN) for r in rows], [ab_ref[r, :] for r in rows], alog_ref[...],
                        dtb_ref[...])
        dattn = [[dattn_ref[r, h * c:(h + 1) * c] for h in range(DN_HEADS)] for r in rows]
        dq, dk, dv, dab, dalog, ddtb = vj(([_dn_heads(dval_ref, r) for r in rows], [_dn_heads(dkcd_ref, r) for r in rows],
                                           dattn, [_dn_heads(dqd_ref, r) for r in rows],
                                           [_dn_heads(dkd_ref, r) for r in rows], [del_ref[j] for j in range(n)]))
        for j, r in enumerate(rows):
            for h in range(DN_HEADS):
                lo, hi = h * DN_HEAD_DIM, (h + 1) * DN_HEAD_DIM
                dqkv_ref[r, lo:hi] = dq[j][h]
                dqkv_ref[r, D_DN + lo:D_DN + hi] = dk[j][h]
                dqkv_ref[r, 2 * D_DN + lo:2 * D_DN + hi] = dv[j][h]
            dab_ref[r, :] = dab[j]
        first = pl.program_id(0) == 0
        _acc(dalog_ref, dalog, first)
        _acc(ddtb_ref, ddtb, first)

    wide = _rows(n * c, D_DN)
    return pl.pallas_call(
        body, name=name, grid=(t // (n * c),),
        in_specs=[_rows(n * c, 3 * D_DN), _rows(n * c, LANE), _full((1, LANE)), _full((1, LANE)),
                  wide, wide, _rows(n * c, DN_ATTN), wide, wide, pl.BlockSpec((n, 1, LANE), lambda i: (i, 0, 0))],
        out_specs=[_rows(n * c, 3 * D_DN), _rows(n * c, LANE), _full((1, LANE)), _full((1, LANE))],
        out_shape=[_sds((t, 3 * D_DN)), _sds((t, LANE)), _sds((1, LANE)), _sds((1, LANE))],
        compiler_params=_cp(1),
    )(qkv, ab, alog, dtb, *cts)


def _seq_view(a, nb):
    return a.reshape((nb, a.shape[0] // nb) + a.shape[1:])


def _dn_chains(nb):
    return [(b, h) for b in range(nb) for h in range(DN_HEADS)]


def _dn_step_operands(val_ref, kcd_ref, attn_ref, qd_ref, kd_ref, el_ref, gg_ref, nb):
    chains = _dn_chains(nb)
    c = DN_CHUNK

    def wide(ref):
        return [ref[b, :, h * DN_HEAD_DIM:(h + 1) * DN_HEAD_DIM] for b, h in chains]

    return (wide(val_ref), wide(kcd_ref), [attn_ref[b, :, h * c:(h + 1) * c] for b, h in chains], wide(qd_ref),
            wide(kd_ref), [el_ref[b, 0, :, h:h + 1] for b, h in chains], wide(gg_ref))


def dn_scan_fwd(loc, gg, ng, nb, name):
    val, kcd, attn, qd, kd, el = loc
    t = val.shape[0]
    c = DN_CHUNK
    nc = t // nb // c
    ns = nb * DN_HEADS

    def body(val_ref, kcd_ref, attn_ref, qd_ref, kd_ref, el_ref, gg_ref, ng_ref, y_ref, ss_ref, st):
        @pl.when(pl.program_id(0) == 0)
        def _():
            st[...] = jnp.zeros_like(st)

        sts = [st[i] for i in range(ns)]
        for i in range(ns):
            ss_ref[0, i] = sts[i]
        ys, new = _dn_step(*_dn_step_operands(val_ref, kcd_ref, attn_ref, qd_ref, kd_ref, el_ref, gg_ref, nb), sts,
                           ng_ref[...])
        for i, (b, h) in enumerate(_dn_chains(nb)):
            y_ref[b, :, h * DN_HEAD_DIM:(h + 1) * DN_HEAD_DIM] = ys[i]
            st[i] = new[i]

    def blk(w):
        return pl.BlockSpec((nb, c, w), lambda k: (0, k, 0))

    el_spec = pl.BlockSpec((nb, 1, 1, LANE), lambda k: (0, k, 0, 0))
    y, ss = pl.pallas_call(
        body, name=name, grid=(nc,),
        in_specs=[blk(D_DN), blk(D_DN), blk(DN_ATTN), blk(D_DN), blk(D_DN), el_spec, blk(D_DN), _full((1, LANE))],
        out_specs=[blk(D_DN), pl.BlockSpec((1, ns, DN_HEAD_DIM, DN_HEAD_DIM), lambda k: (k, 0, 0, 0))],
        out_shape=[_sds((nb, t // nb, D_DN)), _sds((nc, ns, DN_HEAD_DIM, DN_HEAD_DIM))],
        scratch_shapes=[pltpu.VMEM((ns, DN_HEAD_DIM, DN_HEAD_DIM), f32)],
        compiler_params=_cp(1),
    )(_seq_view(val, nb), _seq_view(kcd, nb), _seq_view(attn, nb), _seq_view(qd, nb), _seq_view(kd, nb),
      el.reshape(nb, nc, 1, LANE), _seq_view(gg, nb), ng)
    return y.reshape(t, D_DN), ss


def dn_scan_bwd(loc, gg, ng, ss, dy, nb, name):
    val, kcd, attn, qd, kd, el = loc
    t = val.shape[0]
    c = DN_CHUNK
    nc = t // nb // c
    ns = nb * DN_HEADS

    def body(val_ref, kcd_ref, attn_ref, qd_ref, kd_ref, el_ref, gg_ref, ng_ref, ss_ref, dy_ref,
             dval_ref, dkcd_ref, dattn_ref, dqd_ref, dkd_ref, del_ref, dgg_ref, dng_ref, dst):
        @pl.when(pl.program_id(0) == 0)
        def _():
            dst[...] = jnp.zeros_like(dst)

        lane = lax.broadcasted_iota(jnp.int32, (1, LANE), 1)
        chains = _dn_chains(nb)
        _, vj = jax.vjp(_dn_step, *_dn_step_operands(val_ref, kcd_ref, attn_ref, qd_ref, kd_ref, el_ref, gg_ref, nb),
                        [ss_ref[0, i] for i in range(ns)], ng_ref[...])
        dys = [dy_ref[b, :, h * DN_HEAD_DIM:(h + 1) * DN_HEAD_DIM] for b, h in chains]
        dval, dkcd, dattn, dqd, dkd, dlast, dgg, ds, dng = vj((dys, [dst[i] for i in range(ns)]))
        del_rows = [jnp.zeros((1, LANE), f32) for _ in range(nb)]
        for i, (b, h) in enumerate(chains):
            cols = slice(h * DN_HEAD_DIM, (h + 1) * DN_HEAD_DIM)
            dval_ref[b, :, cols] = dval[i]
            dkcd_ref[b, :, cols] = dkcd[i]
            dattn_ref[b, :, h * c:(h + 1) * c] = dattn[i]
            dqd_ref[b, :, cols] = dqd[i]
            dkd_ref[b, :, cols] = dkd[i]
            dgg_ref[b, :, cols] = dgg[i]
            dst[i] = ds[i]
            del_rows[b] = del_rows[b] + jnp.where(lane == h, dlast[i], 0.0)
        for b in range(nb):
            del_ref[b, 0] = del_rows[b]
        _acc(dng_ref, dng, pl.program_id(0) == 0)

    def blk(w):
        return pl.BlockSpec((nb, c, w), lambda k: (0, nc - 1 - k, 0))

    el_spec = pl.BlockSpec((nb, 1, 1, LANE), lambda k: (0, nc - 1 - k, 0, 0))
    outs = pl.pallas_call(
        body, name=name, grid=(nc,),
        in_specs=[blk(D_DN), blk(D_DN), blk(DN_ATTN), blk(D_DN), blk(D_DN), el_spec, blk(D_DN), _full((1, LANE)),
                  pl.BlockSpec((1, ns, DN_HEAD_DIM, DN_HEAD_DIM), lambda k: (nc - 1 - k, 0, 0, 0)), blk(D_DN)],
        out_specs=[blk(D_DN), blk(D_DN), blk(DN_ATTN), blk(D_DN), blk(D_DN), el_spec, blk(D_DN), _full((1, LANE))],
        out_shape=[_sds((nb, t // nb, D_DN)), _sds((nb, t // nb, D_DN)), _sds((nb, t // nb, DN_ATTN)),
                   _sds((nb, t // nb, D_DN)), _sds((nb, t // nb, D_DN)), _sds((nb, nc, 1, LANE)),
                   _sds((nb, t // nb, D_DN)), _sds((1, LANE))],
        scratch_shapes=[pltpu.VMEM((ns, DN_HEAD_DIM, DN_HEAD_DIM), f32)],
        compiler_params=_cp(1),
    )(_seq_view(val, nb), _seq_view(kcd, nb), _seq_view(attn, nb), _seq_view(qd, nb), _seq_view(kd, nb),
      el.reshape(nb, nc, 1, LANE), _seq_view(gg, nb), ng, ss, _seq_view(dy, nb))
    dloc = [o.reshape((t,) + o.shape[2:]) for o in outs[:5]] + [outs[5].reshape(t // c, 1, LANE)]
    return dloc, outs[6].reshape(t, D_DN), outs[7]


SG_ROWS = 512


def sg_fwd(z, lng, lnb, w, bt, name):
    t = z.shape[0]

    def body(z_ref, lng_ref, lnb_ref, w_ref, bt_ref, y_ref):
        ws = [w_ref[h] for h in range(SG_HEADS)]
        for k in range(SG_ROWS // SG_CHUNK):
            r = pl.ds(k * SG_CHUNK, SG_CHUNK)
            y_ref[r, :] = _sg_chunk(z_ref[r, :D_SG], z_ref[r, D_SG:2 * D_SG], z_ref[r, 2 * D_SG:], lng_ref[...],
                                    lnb_ref[...], ws, bt_ref[...])

    return pl.pallas_call(
        body, name=name, grid=(t // SG_ROWS,),
        in_specs=[_rows(SG_ROWS, 3 * D_SG), _full((1, D_SG)), _full((1, D_SG)),
                  _full((SG_HEADS, SG_CHUNK, SG_CHUNK)), _full((SG_CHUNK, LANE))],
        out_specs=_rows(SG_ROWS, D_SG),
        out_shape=_sds((t, D_SG)),
        compiler_params=_cp(1),
    )(z, lng, lnb, w, bt)


def sg_bwd(z, lng, lnb, w, bt, dy, name):
    t = z.shape[0]

    def body(z_ref, lng_ref, lnb_ref, w_ref, bt_ref, dy_ref, dz_ref, dlng_ref, dlnb_ref, dw_ref, dbt_ref):
        ws = [w_ref[h] for h in range(SG_HEADS)]
        tot = None
        for k in range(SG_ROWS // SG_CHUNK):
            r = pl.ds(k * SG_CHUNK, SG_CHUNK)
            _, vj = jax.vjp(_sg_chunk, z_ref[r, :D_SG], z_ref[r, D_SG:2 * D_SG], z_ref[r, 2 * D_SG:], lng_ref[...],
                            lnb_ref[...], ws, bt_ref[...])
            du, dv, dgate, dlng, dlnb, dws, dbt = vj(dy_ref[r, :])
            dz_ref[r, :D_SG] = du
            dz_ref[r, D_SG:2 * D_SG] = dv
            dz_ref[r, 2 * D_SG:] = dgate
            part = [dlng, dlnb, dbt] + list(dws)
            tot = part if tot is None else [a + b for a, b in zip(tot, part)]
        first = pl.program_id(0) == 0
        _acc(dlng_ref, tot[0], first)
        _acc(dlnb_ref, tot[1], first)
        _acc(dbt_ref, tot[2], first)
        for h in range(SG_HEADS):
            @pl.when(first)
            def _():
                dw_ref[h] = tot[3 + h]

            @pl.when(jnp.logical_not(first))
            def _():
                dw_ref[h] += tot[3 + h]

    return pl.pallas_call(
        body, name=name, grid=(t // SG_ROWS,),
        in_specs=[_rows(SG_ROWS, 3 * D_SG), _full((1, D_SG)), _full((1, D_SG)),
                  _full((SG_HEADS, SG_CHUNK, SG_CHUNK)), _full((SG_CHUNK, LANE)), _rows(SG_ROWS, D_SG)],
        out_specs=[_rows(SG_ROWS, 3 * D_SG), _full((1, D_SG)), _full((1, D_SG)),
                   _full((SG_HEADS, SG_CHUNK, SG_CHUNK)), _full((SG_CHUNK, LANE))],
        out_shape=[_sds((t, 3 * D_SG)), _sds((1, D_SG)), _sds((1, D_SG)), _sds((SG_HEADS, SG_CHUNK, SG_CHUNK)),
                   _sds((SG_CHUNK, LANE))],
        compiler_params=_cp(1),
    )(z, lng, lnb, w, bt, dy)


def add_pairs(a_list, b_list, name):
    n = len(a_list)

    def body(*refs):
        for a_ref, b_ref, o_ref in zip(refs[:n], refs[n:2 * n], refs[2 * n:]):
            o_ref[...] = (a_ref[...].astype(f32) + b_ref[...].astype(f32)).astype(o_ref.dtype)

    return pl.pallas_call(
        body, name=name, out_shape=[_sds(a.shape, a.dtype) for a in a_list],
        compiler_params=pltpu.CompilerParams(vmem_limit_bytes=VMEM_BIG),
    )(*a_list, *b_list)


def sum_adamw(half, recv, w, m, v, name):
    _, r, c = w.shape
    tr = 256 if r % 256 == 0 else r

    def body(half_ref, recv_ref, w_ref, m_ref, v_ref, g_ref, d_ref, nm_ref, nv_ref):
        g = recv_ref[0].astype(f32)
        for k in range(1, N_CHIPS):
            g = g + recv_ref[k].astype(f32)
        wv = w_ref[...]
        nm = ADAM_B1 * m_ref[...] + (1.0 - ADAM_B1) * g
        nv = ADAM_B2 * v_ref[...] + (1.0 - ADAM_B2) * jnp.square(g)
        m_hat = nm / (1.0 - ADAM_B1 ** ADAM_STEP)
        v_hat = nv / (1.0 - ADAM_B2 ** ADAM_STEP)
        g_ref[...] = g
        d_ref[...] = -ADAM_LR * (m_hat / (jnp.sqrt(v_hat) + ADAM_EPS) + ADAM_WD * wv)
        nm_ref[...] = nm
        nv_ref[...] = nv

    own = pl.BlockSpec((None, tr, c), lambda i, h: (h[0], i, 0))
    return pl.pallas_call(
        body, name=name,
        grid_spec=pltpu.PrefetchScalarGridSpec(
            num_scalar_prefetch=1, grid=(r // tr,),
            in_specs=[pl.BlockSpec((N_CHIPS, tr, c), lambda i, h: (0, i, 0)), own, own, own], out_specs=[own] * 4),
        out_shape=[_sds((2, r, c))] * 4,
        compiler_params=_cp(1, VMEM_BIG),
    )(half, recv, w, m, v)


_ANY = pl.BlockSpec(memory_space=pl.ANY)
_MESH = pl.DeviceIdType.MESH


def _flip(v, bit):
    return 1 - v if bit else v


_CHIP_RELS = ((1, 0), (0, 1), (1, 1))


def _piece(ref, kind, q):
    if kind[0] == "slot":
        return ref.at[q]
    if kind[0] == "all":
        return ref
    _, axis, n = kind
    return ref.at[(slice(None),) * axis + (pl.ds(q * n, n),)]


def _piece_shape(shape, kind):
    if kind[0] == "slot":
        return tuple(shape[1:])
    if kind[0] == "all":
        return tuple(shape)
    _, axis, n = kind
    return tuple(shape[:axis]) + (n,) + tuple(shape[axis + 1:])


def gather_weights(shards, kinds, name):
    n = len(shards)

    def out_shape(s, kind):
        if kind[0] == "slot":
            return (N_CHIPS,) + tuple(s.shape)
        _, axis, w = kind
        return tuple(s.shape[:axis + 1]) + (N_CHIPS * w,) + tuple(s.shape[axis + 2:])

    def place(o_ref, kind, q, layer):
        if kind[0] == "slot":
            return o_ref.at[q, layer]
        return _piece(o_ref.at[layer], kind, q)

    def body(*refs):
        s_refs, o_refs = refs[:n], refs[n:2 * n]
        send_sems, recv_sems, fwd_send_sems, fwd_recv_sems = refs[2 * n:]
        x, y, c = lax.axis_index("x"), lax.axis_index("y"), lax.axis_index("c")
        mine = 2 * x + y
        sends, arrivals, forwards, fwd_arrivals = [], [], [], []
        for r, (fx, fy) in enumerate(_CHIP_RELS):
            px, py = _flip(x, fx), _flip(y, fy)
            peer = 2 * px + py
            for k in range(n):
                s = r * n + k
                sends.append(pltpu.make_async_remote_copy(
                    src_ref=s_refs[k].at[c], dst_ref=place(o_refs[k], kinds[k], mine, c), send_sem=send_sems.at[s],
                    recv_sem=recv_sems.at[s], device_id=(px, py, c), device_id_type=_MESH))
                arrivals.append(pltpu.make_async_remote_copy(
                    src_ref=s_refs[k].at[c], dst_ref=place(o_refs[k], kinds[k], peer, c), send_sem=send_sems.at[s],
                    recv_sem=recv_sems.at[s], device_id=(px, py, c), device_id_type=_MESH))
                block = place(o_refs[k], kinds[k], peer, c)
                forwards.append(pltpu.make_async_remote_copy(
                    src_ref=block, dst_ref=block, send_sem=fwd_send_sems.at[s], recv_sem=fwd_recv_sems.at[s],
                    device_id=(x, y, 1 - c), device_id_type=_MESH))
                other = place(o_refs[k], kinds[k], peer, 1 - c)
                fwd_arrivals.append(pltpu.make_async_remote_copy(
                    src_ref=other, dst_ref=other, send_sem=fwd_send_sems.at[s], recv_sem=fwd_recv_sems.at[s],
                    device_id=(x, y, 1 - c), device_id_type=_MESH))
        for cp in sends:
            cp.start()
        for arrived, fwd in zip(arrivals, forwards):
            arrived.wait_recv()
            fwd.start()
        for cp in fwd_arrivals:
            cp.wait_recv()
        for cp in sends + forwards:
            cp.wait_send()

    m = len(_CHIP_RELS) * n
    return pl.pallas_call(
        body, name=name, in_specs=[_ANY] * n, out_specs=[_ANY] * n,
        out_shape=[_sds(out_shape(s, k), s.dtype) for s, k in zip(shards, kinds)],
        scratch_shapes=[pltpu.SemaphoreType.DMA((m,))] * 4,
    )(*shards)


def exchange_halves(gs, name):
    n = len(gs)

    def body(*refs):
        g_refs, got_refs, (send_sems, recv_sems) = refs[:n], refs[n:2 * n], refs[2 * n:]
        x, y, c = lax.axis_index("x"), lax.axis_index("y"), lax.axis_index("c")
        swaps = [pltpu.make_async_remote_copy(
            src_ref=g_refs[k].at[1 - c], dst_ref=got_refs[k], send_sem=send_sems.at[k], recv_sem=recv_sems.at[k],
            device_id=(x, y, 1 - c), device_id_type=_MESH) for k in range(n)]
        for cp in swaps:
            cp.start()
        for cp in swaps:
            cp.wait()

    return pl.pallas_call(
        body, name=name, in_specs=[_ANY] * n, out_specs=[_ANY] * n,
        out_shape=[_sds(g.shape[1:], g.dtype) for g in gs],
        scratch_shapes=[pltpu.SemaphoreType.DMA((n,)), pltpu.SemaphoreType.DMA((n,))],
    )(*gs)


def reduce_to_chips(ts, kinds, name):
    n = len(ts)

    def body(*refs):
        t_refs, o_refs, (send_sems, recv_sems) = refs[:n], refs[n:2 * n], refs[2 * n:]
        x, y, c = lax.axis_index("x"), lax.axis_index("y"), lax.axis_index("c")
        mine = 2 * x + y
        sends, arrivals = [], []
        for r, (fx, fy) in enumerate(_CHIP_RELS):
            px, py = _flip(x, fx), _flip(y, fy)
            peer = 2 * px + py
            for k in range(n):
                s = r * n + k
                sends.append(pltpu.make_async_remote_copy(
                    src_ref=_piece(t_refs[k], kinds[k], peer), dst_ref=o_refs[k].at[mine], send_sem=send_sems.at[s],
                    recv_sem=recv_sems.at[s], device_id=(px, py, c), device_id_type=_MESH))
                arrivals.append(pltpu.make_async_remote_copy(
                    src_ref=_piece(t_refs[k], kinds[k], peer), dst_ref=o_refs[k].at[peer], send_sem=send_sems.at[s],
                    recv_sem=recv_sems.at[s], device_id=(px, py, c), device_id_type=_MESH))
        for cp in sends:
            cp.start()
        for cp in arrivals:
            cp.wait_recv()
        for cp in sends:
            cp.wait_send()

    m = len(_CHIP_RELS) * n
    return pl.pallas_call(
        body, name=name, in_specs=[_ANY] * n, out_specs=[_ANY] * n,
        out_shape=[_sds((N_CHIPS,) + _piece_shape(t.shape, k), t.dtype) for t, k in zip(ts, kinds)],
        scratch_shapes=[pltpu.SemaphoreType.DMA((m,)), pltpu.SemaphoreType.DMA((m,))],
    )(*ts)


def share_halves(rs, name):
    n = len(rs)

    def body(*refs):
        o_refs, (send_sems, recv_sems) = refs[n:2 * n], refs[2 * n:]
        x, y, c = lax.axis_index("x"), lax.axis_index("y"), lax.axis_index("c")
        swaps = [pltpu.make_async_remote_copy(
            src_ref=o_refs[k].at[c], dst_ref=o_refs[k].at[c], send_sem=send_sems.at[k], recv_sem=recv_sems.at[k],
            device_id=(x, y, 1 - c), device_id_type=_MESH) for k in range(n)]
        arrivals = [pltpu.make_async_remote_copy(
            src_ref=o_refs[k].at[c], dst_ref=o_refs[k].at[1 - c], send_sem=send_sems.at[k], recv_sem=recv_sems.at[k],
            device_id=(x, y, 1 - c), device_id_type=_MESH) for k in range(n)]
        for cp in swaps:
            cp.start()
        for cp in arrivals:
            cp.wait_recv()
        for cp in swaps:
            cp.wait_send()

    return pl.pallas_call(
        body, name=name, in_specs=[_ANY] * n, out_specs=[_ANY] * n,
        out_shape=[_sds(r.shape, r.dtype) for r in rs], input_output_aliases={k: k for k in range(n)},
        scratch_shapes=[pltpu.SemaphoreType.DMA((n,)), pltpu.SemaphoreType.DMA((n,))],
    )(*rs)


def _pack_rows(parts, mult, dtype):
    blocks = []
    for a in parts:
        flat = a.reshape(-1).astype(dtype)
        rows = -(-flat.shape[0] // LANE)
        blocks.append(jnp.pad(flat, (0, rows * LANE - flat.shape[0])).reshape(rows, LANE))
    buf = jnp.concatenate(blocks, axis=0)
    return jnp.pad(buf, ((0, -buf.shape[0] % mult), (0, 0)))


def _unpack_rows(buf, shapes):
    out, row = [], 0
    for s in shapes:
        n = 1
        for d in s:
            n *= d
        rows = -(-n // LANE)
        out.append(buf[row:row + rows].reshape(-1)[:n].reshape(s))
        row += rows
    return out


def _permuted_from_shards(shards):
    parts = []
    for lo, hi in GROUP_COLS:
        for q in range(N_CHIPS):
            a, b = max(lo, q * SHARD_COLS), min(hi, (q + 1) * SHARD_COLS)
            if a < b:
                parts.append(shards[q][..., a - q * SHARD_COLS:b - q * SHARD_COLS])
    pad = jnp.zeros(shards[0].shape[:-1] + (D_IN_PAD - D_IN,), shards[0].dtype)
    return jnp.concatenate(parts + [pad], axis=-1)


def _shards_from_groups(groups):
    in_order = sorted(range(len(GROUP_COLS)), key=lambda j: GROUP_COLS[j][0])
    shards = []
    for q in range(N_CHIPS):
        parts = []
        for j in in_order:
            lo, hi = GROUP_COLS[j]
            a, b = max(lo, q * SHARD_COLS), min(hi, (q + 1) * SHARD_COLS)
            if a < b:
                parts.append(groups[j][..., a - lo:b - lo])
        shards.append(jnp.concatenate(parts, axis=-1))
    return shards


def _expand_b(b):
    eye = jnp.eye(SSM_GROUPS, dtype=b.dtype)
    return jnp.einsum("gnc,gh->gchn", b, eye).reshape(D_SSM, N_STATE)


def _extract_b(e):
    return jnp.einsum("gcgn->gnc", e.reshape(SSM_GROUPS, SSM_GROUP, SSM_GROUPS, SSM_STATE))


def _expand_c(c):
    eye = jnp.eye(SSM_GROUPS, dtype=c.dtype)
    return jnp.einsum("gcn,gh->gnhc", c, eye).reshape(N_STATE, D_SSM)


def _extract_c(e):
    return jnp.einsum("gngc->gcn", e.reshape(SSM_GROUPS, SSM_STATE, SSM_GROUPS, SSM_GROUP))


def _lane_row(v):
    return jnp.pad(v, (0, LANE - v.shape[0])).reshape(1, LANE)


def _layer_params(w, l):
    return dict(
        norm_g=w["norm_g"][l][None], win=w["w_in_perm"][l], wout=w["w_out"][l].astype(_MXU),
        pg=w["ple_norm_g"][l][None], wgate=w["w_ple_gate"][l].astype(_MXU), wple=w["w_ple"][l].astype(_MXU),
        are=w["ssm_a_re"][l].reshape(1, N_STATE), aim=w["ssm_a_im"][l].reshape(1, N_STATE),
        ls=jnp.repeat(w["ssm_log_step"][l], SSM_STATE).reshape(1, N_STATE),
        bre=_expand_b(w["ssm_b_re"][l]), bim=_expand_b(w["ssm_b_im"][l]),
        cr=_expand_c(w["ssm_c_re"][l]), ci=_expand_c(w["ssm_c_im"][l]),
        dr=w["ssm_d"][l].reshape(1, D_SSM), wglu=w["ssm_w_glu"][l].astype(f32), bglu=w["ssm_b_glu"][l][None],
        convw=w["dn_conv_w"][l], alog=_lane_row(w["dn_a_log"][l]), dtb=_lane_row(w["dn_dt_bias"][l]),
        ng=w["dn_norm_g"][l][None],
        lng=w["sg_ln_g"][l][None], lnb=w["sg_ln_b"][l][None], sgw=w["sg_w"][l],
        bt=jnp.pad(w["sg_b"][l].T, ((0, 0), (0, LANE - SG_HEADS))),
    )


def _layer_fwd(x, p, lp, nb, tag):
    seq = x.shape[0] // nb
    h, zs, zq, zg, zsg, zab = in_fwd(x, lp["norm_g"], lp["win"], f"in_fwd{tag}")
    prep = s5_prep_fwd(lp["are"], lp["aim"], lp["ls"], lp["bre"], lp["bim"], f"s5_prep_fwd{tag}")
    s5p = tuple(prep) + (lp["cr"], lp["ci"], lp["dr"], lp["wglu"], lp["bglu"])
    ys, hs = s5_fwd(zs, s5p, nb, f"s5_fwd{tag}")
    qkv = dn_pre_fwd(zq, lp["convw"], seq, f"dn_pre_fwd{tag}")
    loc = dn_local_fwd(qkv, zab, lp["alog"], lp["dtb"], f"dn_local_fwd{tag}")
    yd, ss = dn_scan_fwd(loc, zg, lp["ng"], nb, f"dn_scan_fwd{tag}")
    yg = sg_fwd(zsg, lp["lng"], lp["lnb"], lp["sgw"], lp["bt"], f"sg_fwd{tag}")
    x2, x1, y, hn = post_fwd(x, ys, yd, yg, p, lp["wout"], lp["pg"], lp["wgate"], lp["wple"], f"post_fwd{tag}")
    saved = dict(x=x, h=h, zs=zs, zq=zq, zg=zg, zsg=zsg, zab=zab, s5p=s5p, hs=hs, qkv=qkv, loc=loc, ss=ss, x1=x1, y=y, hn=hn, p=p)
    return x2, saved


def _layer_bwd(dx2, sv, lp, nb, tag):
    seq = dx2.shape[0] // nb
    dx1, dgp, dpp, dys, dyd, dyg, dpg = post_bwd(dx2, sv["x1"], sv["hn"], sv["p"], lp["wout"], lp["pg"], lp["wgate"],
                                                 lp["wple"], f"post_bwd{tag}")
    g = {}
    g["w_out"] = wgrad(sv["y"], dx1, f"wgrad_out{tag}")
    g["w_ple_gate"] = wgrad(sv["hn"], dgp, f"wgrad_gate{tag}")
    g["w_ple"] = wgrad(sv["p"], dpp, f"wgrad_ple{tag}")
    g["ple_norm_g"] = dpg[0]
    dzsg, dlng, dlnb, dsgw, dbt = sg_bwd(sv["zsg"], lp["lng"], lp["lnb"], lp["sgw"], lp["bt"], dyg, f"sg_bwd{tag}")
    g["sg_ln_g"], g["sg_ln_b"], g["sg_w"], g["sg_b"] = dlng[0], dlnb[0], dsgw, dbt[:, :SG_HEADS].T
    dloc, dzg, dng = dn_scan_bwd(sv["loc"], sv["zg"], lp["ng"], sv["ss"], dyd, nb, f"dn_scan_bwd{tag}")
    dqkv, dzab, dalog, ddtb = dn_local_bwd(sv["qkv"], sv["zab"], lp["alog"], lp["dtb"], dloc, f"dn_local_bwd{tag}")
    dzq, dconv = dn_pre_bwd(sv["zq"], lp["convw"], dqkv, seq, f"dn_pre_bwd{tag}")
    g["dn_conv_w"], g["dn_a_log"], g["dn_dt_bias"], g["dn_norm_g"] = dconv, dalog[0, :DN_HEADS], ddtb[0, :DN_HEADS], dng[0]
    s5out = s5_bwd(sv["zs"], sv["s5p"], sv["hs"], dys, nb, f"s5_bwd{tag}")
    dzs, dprep, (dcr, dci, ddr, dwglu, dbglu) = s5out[0], s5out[1:1 + S5_PREPARED], s5out[1 + S5_PREPARED:]
    dare, daim, dls, dbre, dbim = s5_prep_bwd(lp["are"], lp["aim"], lp["ls"], lp["bre"], lp["bim"], dprep,
                                              f"s5_prep_bwd{tag}")
    g["ssm_a_re"] = dare.reshape(SSM_GROUPS, SSM_STATE)
    g["ssm_a_im"] = daim.reshape(SSM_GROUPS, SSM_STATE)
    g["ssm_log_step"] = dls.reshape(SSM_GROUPS, SSM_STATE).sum(axis=1)
    g["ssm_b_re"], g["ssm_b_im"] = _extract_b(dbre), _extract_b(dbim)
    g["ssm_c_re"], g["ssm_c_im"] = _extract_c(dcr), _extract_c(dci)
    g["ssm_d"] = ddr.reshape(SSM_GROUPS, SSM_GROUP)
    g["ssm_w_glu"], g["ssm_b_glu"] = dwglu, dbglu[0]
    dzs_all = (dzs, dzq, dzg, dzsg, dzab)
    dx, dng_in = in_bwd(sv["x"], lp["norm_g"], lp["win"], dzs_all, dx1, f"in_bwd{tag}")
    g["w_in_pieces"] = [wgrad(sv["h"], dz, f"wgrad_in{k}{tag}") for k, dz in enumerate(dzs_all)]
    g["norm_g"] = dng_in[0]
    return dx, g


def _local_step(x, p, target, w, nb):
    lps = [_layer_params(w, l) for l in range(DEPTH)]
    saved = []
    for l in range(DEPTH):
        x, sv = _layer_fwd(x, p[l], lps[l], nb, f"_l{l}")
        saved.append(sv)
    loss_blk, dx, dfg = loss_fwd_bwd(x, w["final_norm_g"][None], target, "loss")
    grads = [None] * DEPTH
    for l in reversed(range(DEPTH)):
        dx, grads[l] = _layer_bwd(dx, saved[l], lps[l], nb, f"_l{l}")
    out = {k: jnp.stack([grads[l][k] for l in range(DEPTH)]) for k in grads[0] if k != "w_in_pieces"}
    out["w_in_pieces"] = [grads[l]["w_in_pieces"] for l in range(DEPTH)]
    out["final_norm_g"] = dfg[0]
    return loss_blk[0, 0], dx, out


def kernel(x, p, norm_g, w_in, ssm_a_re, ssm_a_im, ssm_b_re, ssm_b_im, ssm_c_re, ssm_c_im, ssm_d, ssm_log_step, ssm_w_glu, ssm_b_glu, dn_conv_w, dn_a_log, dn_dt_bias, dn_norm_g, sg_ln_g, sg_ln_b, sg_w, sg_b, w_out, ple_norm_g, w_ple_gate, w_ple, final_norm_g, loss_target, m_norm_g, m_w_in, m_ssm_a_re, m_ssm_a_im, m_ssm_b_re, m_ssm_b_im, m_ssm_c_re, m_ssm_c_im, m_ssm_d, m_ssm_log_step, m_ssm_w_glu, m_ssm_b_glu, m_dn_conv_w, m_dn_a_log, m_dn_dt_bias, m_dn_norm_g, m_sg_ln_g, m_sg_ln_b, m_sg_w, m_sg_b, m_w_out, m_ple_norm_g, m_w_ple_gate, m_w_ple, m_final_norm_g, v_norm_g, v_w_in, v_ssm_a_re, v_ssm_a_im, v_ssm_b_re, v_ssm_b_im, v_ssm_c_re, v_ssm_c_im, v_ssm_d, v_ssm_log_step, v_ssm_w_glu, v_ssm_b_glu, v_dn_conv_w, v_dn_a_log, v_dn_dt_bias, v_dn_norm_g, v_sg_ln_g, v_sg_ln_b, v_sg_w, v_sg_b, v_w_out, v_ple_norm_g, v_w_ple_gate, v_w_ple, v_final_norm_g):
    args = locals()
    w = {n: args[n] for n in WEIGHTS}
    m = {n: args["m_" + n] for n in WEIGHTS}
    v = {n: args["v_" + n] for n in WEIGHTS}
    nb, seq = x.shape[0], x.shape[1]
    t = nb * seq

    full = _gather_full(w)
    loss_local, dx, grads = _local_step(x.reshape(t, D_MODEL), p.reshape(DEPTH, t, D_PLE),
                                        loss_target.reshape(t, D_MODEL), full, nb)
    outs = _reduce_and_update(grads, w, m, v)
    loss = lax.psum(loss_local, MESH_AXES)
    return (loss, dx.reshape(nb, seq, D_MODEL), *[outs[0][n] for n in WEIGHTS], *[outs[1][n] for n in WEIGHTS],
            *[outs[2][n] for n in WEIGHTS], *[outs[3][n] for n in WEIGHTS])


def _gather_full(w):
    sh_names = [n for n, _ in SHARDED]
    shards = [w[n] if n == "dn_conv_w" else w[n].astype(_COMM) for n in sh_names]
    gathered = gather_weights(shards, [k for _, k in SHARDED], "gather_weights")
    chip = 2 * lax.axis_index("x") + lax.axis_index("y")
    full = {n: w[n] for n in REPLICATED}
    for (n, kind), shard, got in zip(SHARDED, shards, gathered):
        if kind[0] == "slot":
            full[n] = lax.dynamic_update_index_in_dim(got, shard, chip, 0)
        else:
            full[n] = lax.dynamic_update_slice_in_dim(got, shard, chip * kind[2], axis=kind[1] + 1)
    slots = full.pop("w_in")
    full["w_in_perm"] = _permuted_from_shards([slots[q] for q in range(N_CHIPS)]).astype(_MXU)
    return full


def _reduce_and_update(grads, w, m, v):
    sh_names = [n for n, _ in SHARDED]
    sh_kinds = [k for _, k in SHARDED]

    def pack_small(d):
        buf = _pack_rows([d[n] for n in REPLICATED], 512, f32)
        return buf.reshape(2, buf.shape[0] // 2, LANE)

    grads["w_in"] = jnp.stack([jnp.stack(_shards_from_groups(pieces)) for pieces in grads["w_in_pieces"]])
    gs = [grads[n] if n == "dn_conv_w" else grads[n].astype(_COMM) for n in sh_names] + [pack_small(grads)]
    kinds = sh_kinds + [("all",)]
    core = lax.axis_index("c")
    chip = 2 * lax.axis_index("x") + lax.axis_index("y")
    got = exchange_halves(gs, "exchange_halves")
    sums = add_pairs([lax.dynamic_index_in_dim(g, core, 0, keepdims=False) for g in gs], got, "add_halves")
    parts = list(reduce_to_chips(sums, kinds, "reduce_to_chips"))
    for k, (kind, total) in enumerate(zip(kinds, sums)):
        if kind[0] == "slot":
            own = lax.dynamic_index_in_dim(total, chip, 0, keepdims=False)
        elif kind[0] == "win":
            own = lax.dynamic_slice_in_dim(total, chip * kind[2], kind[2], axis=kind[1])
        else:
            own = total
        parts[k] = lax.dynamic_update_index_in_dim(parts[k], own, chip, 0)
    states = [(w[n], m[n], v[n]) for n in sh_names] + [(pack_small(w), pack_small(m), pack_small(v))]
    half = core.astype(jnp.int32).reshape(1)
    results = []
    for n, part, (wn, mn, vn) in zip(sh_names + ["replicated"], parts, states):
        results += sum_adamw(half, part, wn, mn, vn, f"adamw_{n}")
    shared = share_halves(results, "share_halves")
    rep_shapes = [w[n].shape for n in REPLICATED]
    outs = []
    for j in range(4):
        d = {n: shared[4 * k + j] for k, n in enumerate(sh_names)}
        small = shared[4 * len(sh_names) + j]
        d.update(zip(REPLICATED, _unpack_rows(small.reshape(-1, LANE), rep_shapes)))
        outs.append(d)
    return outs
```

```python
import functools

import jax
import jax.numpy as jnp
from jax import lax
from jax.experimental import pallas as pl
from jax.experimental.pallas import tpu as pltpu

f32 = jnp.float32
bf16 = jnp.bfloat16

_MXU = bf16
_COMM = bf16
HIGH = lax.Precision.HIGH

D_MODEL = 1024
DEPTH = 2
D_PLE = 256
D_SSM = 256
D_DN = 512
D_SG = 256
SSM_GROUPS = 16
SSM_GROUP = 16
SSM_STATE = 64
N_STATE = SSM_GROUPS * SSM_STATE
DN_HEADS = 4
DN_HEAD_DIM = 128
DN_CONV = 4
DN_CHUNK = 64
SG_HEADS = 4
SG_HEAD_DIM = 64
SG_CHUNK = 128
S5_CHUNK = 256
S5_GROUP_ROWS = 8
EPS = 1e-6
D_IN = 3336
D_IN_PAD = 3456
LANE = 128

ADAM_LR = 0.001
ADAM_B1 = 0.9
ADAM_B2 = 0.999
ADAM_EPS = 1e-08
ADAM_WD = 0.01
ADAM_STEP = 10

N_CHIPS = 4
N_DEV = 8

Z_COLS = ((0, 512), (512, 2048), (2048, 2560), (2560, 3328), (3328, 3456))

GROUP_COLS = ((0, 512), (512, 2048), (2056, 2568), (2568, 3336), (2048, 2056))
SHARD_COLS = D_IN // 4

SHARDED = (("w_in", ("slot",)), ("ssm_w_glu", ("win", 0, 64)), ("dn_conv_w", ("win", 1, 384)),
           ("w_out", ("win", 0, 256)), ("w_ple_gate", ("win", 0, 256)), ("w_ple", ("win", 1, 256)))
REPLICATED = ("norm_g", "ssm_a_re", "ssm_a_im", "ssm_b_re", "ssm_b_im", "ssm_c_re", "ssm_c_im", "ssm_d",
              "ssm_log_step", "ssm_b_glu", "dn_a_log", "dn_dt_bias", "dn_norm_g", "sg_ln_g", "sg_ln_b", "sg_w",
              "sg_b", "ple_norm_g", "final_norm_g")
WEIGHTS = ("norm_g", "w_in", "ssm_a_re", "ssm_a_im", "ssm_b_re", "ssm_b_im", "ssm_c_re", "ssm_c_im", "ssm_d",
           "ssm_log_step", "ssm_w_glu", "ssm_b_glu", "dn_conv_w", "dn_a_log", "dn_dt_bias", "dn_norm_g", "sg_ln_g",
           "sg_ln_b", "sg_w", "sg_b", "w_out", "ple_norm_g", "w_ple_gate", "w_ple", "final_norm_g")

VMEM_BIG = 56 * 1024 * 1024


def _mm(a, b):
    return jnp.dot(a.astype(_MXU), b.astype(_MXU), preferred_element_type=f32)


def _mm_nt(a, b):
    return lax.dot_general(a.astype(_MXU), b.astype(_MXU), (((1,), (1,)), ((), ())), preferred_element_type=f32)


def _mm_tn(a, b):
    return lax.dot_general(a.astype(_MXU), b.astype(_MXU), (((0,), (0,)), ((), ())), preferred_element_type=f32)


@jax.custom_vjp
def bdot(a, b):
    return _mm(a, b)


def _bdot_fwd(a, b):
    return _mm(a, b), (a, b)


def _bdot_bwd(res, g):
    a, b = res
    return _mm_nt(g, b).astype(a.dtype), _mm_tn(a, g).astype(b.dtype)


bdot.defvjp(_bdot_fwd, _bdot_bwd)


@jax.custom_vjp
def bdot_nt(a, b):
    return _mm_nt(a, b)


def _bdot_nt_fwd(a, b):
    return _mm_nt(a, b), (a, b)


def _bdot_nt_bwd(res, g):
    a, b = res
    return _mm(g, b).astype(a.dtype), _mm_tn(g, a).astype(b.dtype)


bdot_nt.defvjp(_bdot_nt_fwd, _bdot_nt_bwd)


@jax.custom_vjp
def bdot_tn(a, b):
    return _mm_tn(a, b)


def _bdot_tn_fwd(a, b):
    return _mm_tn(a, b), (a, b)


def _bdot_tn_bwd(res, g):
    a, b = res
    return _mm_nt(b, g).astype(a.dtype), _mm(a, g).astype(b.dtype)


bdot_tn.defvjp(_bdot_tn_fwd, _bdot_tn_bwd)


def hdot(a, b):
    return jnp.dot(a, b, precision=HIGH, preferred_element_type=f32)


def _unit_lower_inverses(ms):
    n = ms[0].shape[0]
    eye = (lax.broadcasted_iota(jnp.int32, (n, n), 0) == lax.broadcasted_iota(jnp.int32, (n, n), 1)).astype(f32)
    pw = [-m for m in ms]
    inv = [eye + p for p in pw]
    for _ in range(n.bit_length() - 2):
        pw = [hdot(p, p) for p in pw]
        inv = [a + hdot(a, p) for a, p in zip(inv, pw)]
    return inv


@jax.custom_vjp
def solve_unit_lower(ms, rhs):
    return [hdot(a, r) for a, r in zip(_unit_lower_inverses(ms), rhs)]


def _solve_unit_lower_fwd(ms, rhs):
    inv = _unit_lower_inverses(ms)
    xs = [hdot(a, r) for a, r in zip(inv, rhs)]
    return xs, (inv, xs)


def _solve_unit_lower_bwd(res, gs):
    inv, xs = res
    d_rhs = [lax.dot_general(a, g, (((0,), (0,)), ((), ())), precision=HIGH, preferred_element_type=f32)
             for a, g in zip(inv, gs)]
    d_ms = [-lax.dot_general(d, x, (((1,), (1,)), ((), ())), precision=HIGH, preferred_element_type=f32)
            for d, x in zip(d_rhs, xs)]
    return d_ms, d_rhs


solve_unit_lower.defvjp(_solve_unit_lower_fwd, _solve_unit_lower_bwd)


@functools.partial(jax.custom_vjp, nondiff_argnums=(1,))
def roll_rows(x, k):
    return pltpu.roll(x, k, 0)


def _roll_rows_fwd(x, k):
    return pltpu.roll(x, k, 0), None


def _roll_rows_bwd(k, _, g):
    return (pltpu.roll(g, g.shape[0] - k, 0),)


roll_rows.defvjp(_roll_rows_fwd, _roll_rows_bwd)


def _row_ids(shape):
    return lax.broadcasted_iota(jnp.int32, shape, 0)


def _rms(x, g):
    return x * lax.rsqrt(jnp.mean(x * x, axis=-1, keepdims=True) + EPS) * g


def _layer_norm(x, g, b):
    mu = jnp.mean(x, axis=-1, keepdims=True)
    xc = x - mu
    return xc * lax.rsqrt(jnp.mean(xc * xc, axis=-1, keepdims=True) + EPS) * g + b


def _s5_prep(are, aim, ls, bre, bim):
    step = jnp.exp(ls)
    mag = jnp.exp(are * step)
    lr = mag * jnp.cos(aim * step)
    li = mag * jnp.sin(aim * step)
    den = are * are + aim * aim
    nr, ni = lr - 1.0, li
    fr = (nr * are + ni * aim) / den
    fi = (ni * are - nr * aim) / den
    bbr = fr * bre - fi * bim
    bbi = fr * bim + fi * bre
    pr = jnp.broadcast_to(lr, (S5_GROUP_ROWS, N_STATE))
    pi = jnp.broadcast_to(li, (S5_GROUP_ROWS, N_STATE))
    d = 1
    while d < S5_GROUP_ROWS:
        keep = _row_ids(pr.shape) >= d
        sr, si = roll_rows(pr, d), roll_rows(pi, d)
        pr, pi = jnp.where(keep, pr * sr - pi * si, pr), jnp.where(keep, pr * si + pi * sr, pi)
        d *= 2
    return pr, pi, bbr, bbi


def _s5_chunk(u, gate, hr, hi, pr, pi, bbr, bbi, cr, ci, dr, wglu, bglu):
    n, grp = u.shape[0], S5_GROUP_ROWS
    xr = bdot(u, bbr)
    xi = bdot(u, bbi)
    sub = _row_ids(xr.shape) % grp
    d = 1
    while d < grp:
        lr, li = pr[d - 1:d], pi[d - 1:d]
        sr = jnp.where(sub >= d, roll_rows(xr, d), 0.0)
        si = jnp.where(sub >= d, roll_rows(xi, d), 0.0)
        xr, xi = xr + lr * sr - li * si, xi + lr * si + li * sr
        d *= 2
    outs_r, outs_i = [], []
    for g in range(n // grp):
        gr, gi = xr[g * grp:(g + 1) * grp], xi[g * grp:(g + 1) * grp]
        gr, gi = gr + pr * hr - pi * hi, gi + pr * hi + pi * hr
        hr, hi = gr[grp - 1:grp], gi[grp - 1:grp]
        outs_r.append(gr)
        outs_i.append(gi)
    xr = jnp.concatenate(outs_r, axis=0)
    xi = jnp.concatenate(outs_i, axis=0)
    y = bdot(xr, cr) - bdot(xi, ci) + dr * u
    y = jax.nn.gelu(y)
    y = y * jax.nn.sigmoid(bdot(y, wglu) + bglu)
    return y * jax.nn.silu(gate), hr, hi


def _dn_pre(xc, xp, w0, w1, w2, w3, is_start, col):
    xp = jnp.where(is_start, 0.0, xp)
    rows = _row_ids(xc.shape)
    acc = w3 * xc
    for d, w in ((1, w2), (2, w1), (3, w0)):
        acc = acc + w * jnp.where(rows >= d, roll_rows(xc, d), roll_rows(xp, d))
    y = jax.nn.silu(acc)
    nrm = y * lax.rsqrt(jnp.sum(y * y, axis=-1, keepdims=True) + EPS)
    nrm = nrm * jnp.where(col < DN_HEADS, DN_HEAD_DIM ** -0.5, 1.0)
    return jnp.where(col < 2 * DN_HEADS, nrm, y)


def _dn_local(qs, ks, vs, abs_, alog, dtb):
    c = DN_CHUNK
    ri = lax.broadcasted_iota(jnp.int32, (c, c), 0)
    ci = lax.broadcasted_iota(jnp.int32, (c, c), 1)
    causal, strict = ri >= ci, ri > ci
    tril = causal.astype(f32)
    gcums = [hdot(tril, -jnp.exp(alog) * jax.nn.softplus(ab + dtb)) for ab in abs_]
    gcum_ts = [g.T for g in gcums]
    sigs = [jax.nn.sigmoid(ab) for ab in abs_]
    chains = [(j, h) for j in range(len(abs_)) for h in range(DN_HEADS)]
    gc = [gcums[j][:, h:h + 1] for j, h in chains]
    decay = [jnp.where(causal, jnp.exp(jnp.where(causal, gc[n] - gcum_ts[j][h:h + 1, :], 0.0)), 0.0)
             for n, (j, h) in enumerate(chains)]
    beta = [sigs[j][:, DN_HEADS + h:DN_HEADS + h + 1] for j, h in chains]
    kb = [ks[j][h] * beta[n] for n, (j, h) in enumerate(chains)]
    ms = [jnp.where(strict, bdot_nt(kb[n], ks[j][h]) * decay[n], 0.0) for n, (j, h) in enumerate(chains)]
    egc = [jnp.exp(g) for g in gc]
    rhs = [jnp.concatenate([vs[j][h] * beta[n], kb[n] * egc[n]], axis=1) for n, (j, h) in enumerate(chains)]
    sol = solve_unit_lower(ms, rhs)
    values = [s[:, :DN_HEAD_DIM] for s in sol]
    k_cds = [s[:, DN_HEAD_DIM:] for s in sol]
    attns = [bdot_nt(qs[j][h], ks[j][h]) * decay[n] for n, (j, h) in enumerate(chains)]
    q_decs = [qs[j][h] * egc[n] for n, (j, h) in enumerate(chains)]
    k_decs = [ks[j][h] * jnp.exp(gc[n][c - 1:c, :] - gc[n]) for n, (j, h) in enumerate(chains)]

    def nest(flat):
        return [flat[j * DN_HEADS:(j + 1) * DN_HEADS] for j in range(len(abs_))]

    return nest(values), nest(k_cds), nest(attns), nest(q_decs), nest(k_decs), [jnp.exp(g[c - 1:c, :]) for g in gcums]


def _dn_step(values, k_cds, attns, q_decs, k_decs, lasts, ggs, sts, ng):
    v_new = [v - bdot(kc, st) for v, kc, st in zip(values, k_cds, sts)]
    o = [bdot(qd, st) for qd, st in zip(q_decs, sts)]
    o = [a + bdot(at, vn) for a, at, vn in zip(o, attns, v_new)]
    new = [st * la + bdot_tn(kd, vn) for st, la, kd, vn in zip(sts, lasts, k_decs, v_new)]
    return [_rms(a, ng) * jax.nn.silu(g) for a, g in zip(o, ggs)], new


def _sg_chunk(u, v, gate, lng, lnb, ws, bt):
    n = SG_CHUNK
    ug = jax.nn.gelu(u)
    vn = _layer_norm(jax.nn.gelu(v), lng, lnb)
    causal = lax.broadcasted_iota(jnp.int32, (n, n), 0) >= lax.broadcasted_iota(jnp.int32, (n, n), 1)
    lane = lax.broadcasted_iota(jnp.int32, (n, D_SG), 1)
    s = jnp.zeros((n, D_SG), f32)
    for h in range(SG_HEADS):
        t = bdot(jnp.where(causal, ws[h], 0.0), vn) + bt[:, h:h + 1]
        s = s + jnp.where((lane >= h * SG_HEAD_DIM) & (lane < (h + 1) * SG_HEAD_DIM), t, 0.0)
    return ug * s * jax.nn.silu(gate)


def _cp(n_grid, vmem=None):
    return pltpu.CompilerParams(dimension_semantics=("arbitrary",) * n_grid, vmem_limit_bytes=vmem)


def _full(shape):
    nd = len(shape)
    return pl.BlockSpec(tuple(shape), lambda *_: (0,) * nd)


def _rows(tm, ncol):
    return pl.BlockSpec((tm, ncol), lambda i: (i, 0))


def _sds(shape, dtype=f32):
    return jax.ShapeDtypeStruct(tuple(shape), dtype)


def _acc(ref, val, first):
    @pl.when(first)
    def _():
        ref[...] = val

    @pl.when(jnp.logical_not(first))
    def _():
        ref[...] += val


def in_fwd(x, g, w, name):
    t, tm = x.shape[0], 256

    def body(x_ref, g_ref, w_ref, h_ref, *z_refs):
        h = _rms(x_ref[...], g_ref[...]).astype(_MXU)
        h_ref[...] = h
        for z_ref, (a, b) in zip(z_refs, Z_COLS):
            z_ref[...] = jnp.dot(h, w_ref[:, a:b], preferred_element_type=f32)

    widths = [b - a for a, b in Z_COLS]
    return pl.pallas_call(
        body, name=name, grid=(t // tm,),
        in_specs=[_rows(tm, D_MODEL), _full((1, D_MODEL)), _full((D_MODEL, D_IN_PAD))],
        out_specs=[_rows(tm, D_MODEL)] + [_rows(tm, n) for n in widths],
        out_shape=[_sds((t, D_MODEL), _MXU)] + [_sds((t, n)) for n in widths],
        compiler_params=_cp(1, VMEM_BIG),
    )(x, g, w)


def in_bwd(x, g, w, dzs, dres, name):
    t, tm = x.shape[0], 256
    widths = [b - a for a, b in Z_COLS]

    def body(x_ref, g_ref, w_ref, dres_ref, *rest):
        dz_refs, (dx_ref, dg_ref) = rest[:5], rest[5:]
        dh = jnp.zeros((tm, D_MODEL), f32)
        for dz_ref, (a, b) in zip(dz_refs, Z_COLS):
            dh = dh + _mm_nt(dz_ref[...], w_ref[:, a:b])
        _, vj = jax.vjp(_rms, x_ref[...], g_ref[...])
        dx, dg = vj(dh)
        dx_ref[...] = dres_ref[...] + dx
        _acc(dg_ref, dg, pl.program_id(0) == 0)

    return pl.pallas_call(
        body, name=name, grid=(t // tm,),
        in_specs=[_rows(tm, D_MODEL), _full((1, D_MODEL)), _full((D_MODEL, D_IN_PAD)), _rows(tm, D_MODEL)]
        + [_rows(tm, n) for n in widths],
        out_specs=[_rows(tm, D_MODEL), _full((1, D_MODEL))],
        out_shape=[_sds((t, D_MODEL)), _sds((1, D_MODEL))],
        compiler_params=_cp(1, VMEM_BIG),
    )(x, g, w, dres, *dzs)


def wgrad(a, g, name):
    t, k = a.shape
    n = g.shape[1]
    tm = min(t, 2048)
    tn = n if n <= 768 else (768 if n % 768 == 0 else 512)
    steps = t // tm

    def body(a_ref, g_ref, o_ref, acc):
        i = pl.program_id(1)
        _acc(acc, _mm_tn(a_ref[...], g_ref[...]), i == 0)

        @pl.when(i == steps - 1)
        def _():
            o_ref[...] = acc[...].astype(o_ref.dtype)

    return pl.pallas_call(
        body, name=name, grid=(n // tn, steps),
        in_specs=[pl.BlockSpec((tm, k), lambda j, i: (i, 0)), pl.BlockSpec((tm, tn), lambda j, i: (i, j))],
        out_specs=pl.BlockSpec((k, tn), lambda j, i: (0, j)),
        out_shape=_sds((k, n), _COMM),
        scratch_shapes=[pltpu.VMEM((k, tn), f32)],
        compiler_params=_cp(2, VMEM_BIG),
    )(a, g)


def post_fwd(x, ys, yd, yg, p, wout, pg, wgate, wple, name):
    t, tm = x.shape[0], 256

    def body(x_ref, ys_ref, yd_ref, yg_ref, p_ref, wout_ref, pg_ref, wgate_ref, wple_ref,
             x2_ref, x1_ref, y_ref, hn_ref):
        y = jnp.concatenate([ys_ref[...], yd_ref[...], yg_ref[...]], axis=1).astype(_MXU)
        y_ref[...] = y
        x1 = x_ref[...] + jnp.dot(y, wout_ref[...], preferred_element_type=f32)
        x1_ref[...] = x1
        hn = _rms(x1, pg_ref[...]).astype(_MXU)
        hn_ref[...] = hn
        gp = jnp.dot(hn, wgate_ref[...], preferred_element_type=f32)
        pp = _mm(p_ref[...], wple_ref[...])
        x2_ref[...] = x1 + jax.nn.sigmoid(gp) * pp

    return pl.pallas_call(
        body, name=name, grid=(t // tm,),
        in_specs=[_rows(tm, D_MODEL), _rows(tm, D_SSM), _rows(tm, D_DN), _rows(tm, D_SG), _rows(tm, D_PLE),
                  _full((D_MODEL, D_MODEL)), _full((1, D_MODEL)), _full((D_MODEL, D_MODEL)), _full((D_PLE, D_MODEL))],
        out_specs=[_rows(tm, D_MODEL)] * 4,
        out_shape=[_sds((t, D_MODEL)), _sds((t, D_MODEL)), _sds((t, D_MODEL), _MXU), _sds((t, D_MODEL), _MXU)],
        compiler_params=_cp(1, VMEM_BIG),
    )(x, ys, yd, yg, p, wout, pg, wgate, wple)


def post_bwd(dx2, x1, hn, p, wout, pg, wgate, wple, name):
    t, tm = dx2.shape[0], 256

    def body(dx2_ref, x1_ref, hn_ref, p_ref, wout_ref, pg_ref, wgate_ref, wple_ref,
             dx1_ref, dgp_ref, dpp_ref, dys_ref, dyd_ref, dyg_ref, dpg_ref):
        dx2 = dx2_ref[...]
        gp = jnp.dot(hn_ref[...], wgate_ref[...], preferred_element_type=f32)
        pp = _mm(p_ref[...], wple_ref[...])
        sg = jax.nn.sigmoid(gp)
        dpp_ref[...] = (dx2 * sg).astype(_MXU)
        dgp = (dx2 * pp * sg * (1.0 - sg)).astype(_MXU)
        dgp_ref[...] = dgp
        dhn = _mm_nt(dgp, wgate_ref[...])
        _, vj = jax.vjp(_rms, x1_ref[...], pg_ref[...])
        dx1n, dpg = vj(dhn)
        dx1 = dx2 + dx1n
        dx1_ref[...] = dx1
        dy = _mm_nt(dx1, wout_ref[...])
        dys_ref[...] = dy[:, :D_SSM]
        dyd_ref[...] = dy[:, D_SSM:D_SSM + D_DN]
        dyg_ref[...] = dy[:, D_SSM + D_DN:]
        _acc(dpg_ref, dpg, pl.program_id(0) == 0)

    return pl.pallas_call(
        body, name=name, grid=(t // tm,),
        in_specs=[_rows(tm, D_MODEL), _rows(tm, D_MODEL), _rows(tm, D_MODEL), _rows(tm, D_PLE),
                  _full((D_MODEL, D_MODEL)), _full((1, D_MODEL)), _full((D_MODEL, D_MODEL)), _full((D_PLE, D_MODEL))],
        out_specs=[_rows(tm, D_MODEL), _rows(tm, D_MODEL), _rows(tm, D_MODEL), _rows(tm, D_SSM), _rows(tm, D_DN),
                   _rows(tm, D_SG), _full((1, D_MODEL))],
        out_shape=[_sds((t, D_MODEL)), _sds((t, D_MODEL), _MXU), _sds((t, D_MODEL), _MXU), _sds((t, D_SSM)),
                   _sds((t, D_DN)), _sds((t, D_SG)), _sds((1, D_MODEL))],
        compiler_params=_cp(1, VMEM_BIG),
    )(dx2, x1, hn, p, wout, pg, wgate, wple)


def loss_fwd_bwd(x, fg, target, name):
    t, tm = x.shape[0], 512

    def body(x_ref, fg_ref, t_ref, loss_ref, dx_ref, dfg_ref):
        def f(xv, gv):
            err = _rms(xv, gv) - t_ref[...]
            return 0.5 * jnp.sum(jnp.mean(err * err, axis=-1))

        val, vj = jax.vjp(f, x_ref[...], fg_ref[...])
        dx, dfg = vj(jnp.ones((), f32))
        dx_ref[...] = dx
        first = pl.program_id(0) == 0
        _acc(dfg_ref, dfg, first)
        _acc(loss_ref, jnp.full((8, LANE), val, f32), first)

    return pl.pallas_call(
        body, name=name, grid=(t // tm,),
        in_specs=[_rows(tm, D_MODEL), _full((1, D_MODEL)), _rows(tm, D_MODEL)],
        out_specs=[_full((8, LANE)), _rows(tm, D_MODEL), _full((1, D_MODEL))],
        out_shape=[_sds((8, LANE)), _sds((t, D_MODEL)), _sds((1, D_MODEL))],
        compiler_params=_cp(1),
    )(x, fg, target)


S5_PREPARED = 4
_S5_PARAM_SHAPES = ((S5_GROUP_ROWS, N_STATE), (S5_GROUP_ROWS, N_STATE), (D_SSM, N_STATE), (D_SSM, N_STATE),
                    (N_STATE, D_SSM), (N_STATE, D_SSM), (1, D_SSM), (D_SSM, D_SSM), (1, D_SSM))

def s5_prep_fwd(are, aim, ls, bre, bim, name):
    def body(are_ref, aim_ref, ls_ref, bre_ref, bim_ref, *outs):
        vals = _s5_prep(are_ref[...], aim_ref[...], ls_ref[...], bre_ref[...], bim_ref[...])
        for o, v in zip(outs, vals):
            o[...] = v

    return pl.pallas_call(body, name=name, out_shape=[_sds(s) for s in _S5_PARAM_SHAPES[:S5_PREPARED]])(
        are, aim, ls, bre, bim)


def s5_prep_bwd(are, aim, ls, bre, bim, cts, name):
    def body(are_ref, aim_ref, ls_ref, bre_ref, bim_ref, *rest):
        ct_refs, outs = rest[:S5_PREPARED], rest[S5_PREPARED:]
        _, vj = jax.vjp(_s5_prep, are_ref[...], aim_ref[...], ls_ref[...], bre_ref[...], bim_ref[...])
        for o, v in zip(outs, vj(tuple(r[...] for r in ct_refs))):
            o[...] = v

    shapes = [(1, N_STATE)] * 3 + [(D_SSM, N_STATE)] * 2
    return pl.pallas_call(body, name=name, out_shape=[_sds(s) for s in shapes])(are, aim, ls, bre, bim, *cts)


def s5_fwd(z, params, nb, name):
    t = z.shape[0]
    nc = t // nb // S5_CHUNK
    npar = len(_S5_PARAM_SHAPES)

    def body(z_ref, *rest):
        p_refs, (y_ref, hs_ref, hr_s, hi_s) = rest[:npar], rest[npar:]

        @pl.when(pl.program_id(1) == 0)
        def _():
            hr_s[...] = jnp.zeros_like(hr_s)
            hi_s[...] = jnp.zeros_like(hi_s)

        hr, hi = hr_s[...], hi_s[...]
        hs_ref[0, :, :N_STATE] = hr
        hs_ref[0, :, N_STATE:] = hi
        y, nhr, nhi = _s5_chunk(z_ref[:, :D_SSM], z_ref[:, D_SSM:], hr, hi, *[r[...] for r in p_refs])
        y_ref[...] = y
        hr_s[...] = nhr
        hi_s[...] = nhi

    return pl.pallas_call(
        body, name=name, grid=(nb, nc),
        in_specs=[pl.BlockSpec((S5_CHUNK, 2 * D_SSM), lambda b, c: (b * nc + c, 0))]
        + [_full(s) for s in _S5_PARAM_SHAPES],
        out_specs=[pl.BlockSpec((S5_CHUNK, D_SSM), lambda b, c: (b * nc + c, 0)),
                   pl.BlockSpec((1, 1, 2 * N_STATE), lambda b, c: (b * nc + c, 0, 0))],
        out_shape=[_sds((t, D_SSM)), _sds((nb * nc, 1, 2 * N_STATE))],
        scratch_shapes=[pltpu.VMEM((1, N_STATE), f32), pltpu.VMEM((1, N_STATE), f32)],
        compiler_params=_cp(2, VMEM_BIG),
    )(z, *params)


def s5_bwd(z, params, hs, dy, nb, name):
    t = z.shape[0]
    nc = t // nb // S5_CHUNK
    npar = len(_S5_PARAM_SHAPES)

    def body(z_ref, hs_ref, dy_ref, *rest):
        p_refs, dz_ref, dp_refs, (dhr_s, dhi_s) = rest[:npar], rest[npar], rest[npar + 1:2 * npar + 1], rest[2 * npar + 1:]

        @pl.when(pl.program_id(1) == 0)
        def _():
            dhr_s[...] = jnp.zeros_like(dhr_s)
            dhi_s[...] = jnp.zeros_like(dhi_s)

        prim = (z_ref[:, :D_SSM], z_ref[:, D_SSM:], hs_ref[0, :, :N_STATE], hs_ref[0, :, N_STATE:]) + tuple(
            r[...] for r in p_refs)
        _, vj = jax.vjp(_s5_chunk, *prim)
        cts = vj((dy_ref[...], dhr_s[...], dhi_s[...]))
        dz_ref[:, :D_SSM] = cts[0]
        dz_ref[:, D_SSM:] = cts[1]
        dhr_s[...] = cts[2]
        dhi_s[...] = cts[3]
        first = (pl.program_id(0) == 0) & (pl.program_id(1) == 0)
        for r, v in zip(dp_refs, cts[4:]):
            _acc(r, v, first)

    rev = lambda b, c: (b * nc + nc - 1 - c, 0)
    return pl.pallas_call(
        body, name=name, grid=(nb, nc),
        in_specs=[pl.BlockSpec((S5_CHUNK, 2 * D_SSM), rev),
                  pl.BlockSpec((1, 1, 2 * N_STATE), lambda b, c: (b * nc + nc - 1 - c, 0, 0)),
                  pl.BlockSpec((S5_CHUNK, D_SSM), rev)] + [_full(s) for s in _S5_PARAM_SHAPES],
        out_specs=[pl.BlockSpec((S5_CHUNK, 2 * D_SSM), rev)] + [_full(s) for s in _S5_PARAM_SHAPES],
        out_shape=[_sds((t, 2 * D_SSM))] + [_sds(s) for s in _S5_PARAM_SHAPES],
        scratch_shapes=[pltpu.VMEM((1, N_STATE), f32), pltpu.VMEM((1, N_STATE), f32)],
        compiler_params=_cp(2, VMEM_BIG),
    )(z, hs, dy, *params)


DN_PRE_ROWS = 256
DN_COLS = 3 * D_DN // LANE


def dn_pre_fwd(zq, convw, seq, name):
    t, tb = zq.shape[0], DN_PRE_ROWS
    per_seq = seq // tb

    def body(xc_ref, xp_ref, w_ref, o_ref):
        is_start = pl.program_id(0) % per_seq == 0
        for j in range(DN_COLS):
            cols = slice(j * LANE, (j + 1) * LANE)
            o_ref[:, cols] = _dn_pre(xc_ref[:, cols], xp_ref[:, cols], w_ref[0:1, cols], w_ref[1:2, cols],
                                     w_ref[2:3, cols], w_ref[3:4, cols], is_start, j)

    return pl.pallas_call(
        body, name=name, grid=(t // tb,),
        in_specs=[_rows(tb, 3 * D_DN), pl.BlockSpec((tb, 3 * D_DN), lambda i: (jnp.maximum(i - 1, 0), 0)),
                  _full((DN_CONV, 3 * D_DN))],
        out_specs=_rows(tb, 3 * D_DN),
        out_shape=_sds((t, 3 * D_DN)),
        compiler_params=_cp(1, VMEM_BIG),
    )(zq, zq, convw)


def dn_pre_bwd(zq, convw, dqkv, seq, name):
    t, tb = zq.shape[0], DN_PRE_ROWS
    nrow = t // tb
    per_seq = seq // tb

    def body(xc_ref, xp_ref, w_ref, d_ref, dx_ref, dw_ref, carry):
        step = pl.program_id(0)
        i = nrow - 1 - step

        @pl.when(step == 0)
        def _():
            carry[...] = jnp.zeros_like(carry)

        for j in range(DN_COLS):
            cols = slice(j * LANE, (j + 1) * LANE)
            fn = functools.partial(_dn_pre, is_start=i % per_seq == 0, col=j)
            _, vj = jax.vjp(fn, xc_ref[:, cols], xp_ref[:, cols], w_ref[0:1, cols], w_ref[1:2, cols],
                            w_ref[2:3, cols], w_ref[3:4, cols])
            dxc, dxp, dw0, dw1, dw2, dw3 = vj(d_ref[:, cols])
            dx_ref[:, cols] = dxc + carry[:, cols]
            carry[:, cols] = dxp
            for k, dw in enumerate((dw0, dw1, dw2, dw3)):
                @pl.when(step == 0)
                def _():
                    dw_ref[k:k + 1, cols] = dw

                @pl.when(step != 0)
                def _():
                    dw_ref[k:k + 1, cols] += dw

    rev = lambda s: (nrow - 1 - s, 0)
    return pl.pallas_call(
        body, name=name, grid=(nrow,),
        in_specs=[pl.BlockSpec((tb, 3 * D_DN), rev),
                  pl.BlockSpec((tb, 3 * D_DN), lambda s: (jnp.maximum(nrow - 2 - s, 0), 0)),
                  _full((DN_CONV, 3 * D_DN)), pl.BlockSpec((tb, 3 * D_DN), rev)],
        out_specs=[pl.BlockSpec((tb, 3 * D_DN), rev), _full((DN_CONV, 3 * D_DN))],
        out_shape=[_sds((t, 3 * D_DN)), _sds((DN_CONV, 3 * D_DN))],
        scratch_shapes=[pltpu.VMEM((tb, 3 * D_DN), f32)],
        compiler_params=_cp(1, VMEM_BIG),
    )(zq, zq, convw, dqkv)


DN_LOCAL_CHUNKS = 2
DN_ATTN = DN_HEADS * DN_CHUNK


def _dn_heads(ref, rows, base=0):
    return [ref[rows, base + h * DN_HEAD_DIM:base + (h + 1) * DN_HEAD_DIM] for h in range(DN_HEADS)]


def dn_local_fwd(qkv, ab, alog, dtb, name):
    t = qkv.shape[0]
    c, n = DN_CHUNK, DN_LOCAL_CHUNKS

    def body(qkv_ref, ab_ref, alog_ref, dtb_ref, val_ref, kcd_ref, attn_ref, qd_ref, kd_ref, el_ref):
        rows = [pl.ds(j * c, c) for j in range(n)]
        vals, kcds, attns, qds, kds, els = _dn_local(
            [_dn_heads(qkv_ref, r) for r in rows], [_dn_heads(qkv_ref, r, D_DN) for r in rows],
            [_dn_heads(qkv_ref, r, 2 * D_DN) for r in rows], [ab_ref[r, :] for r in rows], alog_ref[...], dtb_ref[...])
        for j, r in enumerate(rows):
            for h in range(DN_HEADS):
                lo, hi = h * DN_HEAD_DIM, (h + 1) * DN_HEAD_DIM
                val_ref[r, lo:hi] = vals[j][h]
                kcd_ref[r, lo:hi] = kcds[j][h]
                qd_ref[r, lo:hi] = qds[j][h]
                kd_ref[r, lo:hi] = kds[j][h]
                attn_ref[r, h * c:(h + 1) * c] = attns[j][h]
            el_ref[j] = els[j]

    wide = _rows(n * c, D_DN)
    return pl.pallas_call(
        body, name=name, grid=(t // (n * c),),
        in_specs=[_rows(n * c, 3 * D_DN), _rows(n * c, LANE), _full((1, LANE)), _full((1, LANE))],
        out_specs=[wide, wide, _rows(n * c, DN_ATTN), wide, wide, pl.BlockSpec((n, 1, LANE), lambda i: (i, 0, 0))],
        out_shape=[_sds((t, D_DN)), _sds((t, D_DN)), _sds((t, DN_ATTN)), _sds((t, D_DN)), _sds((t, D_DN)),
                   _sds((t // c, 1, LANE))],
        compiler_params=_cp(1),
    )(qkv, ab, alog, dtb)


def dn_local_bwd(qkv, ab, alog, dtb, cts, name):
    t = qkv.shape[0]
    c, n = DN_CHUNK, DN_LOCAL_CHUNKS

    def body(qkv_ref, ab_ref, alog_ref, dtb_ref, dval_ref, dkcd_ref, dattn_ref, dqd_ref, dkd_ref, del_ref,
             dqkv_ref, dab_ref, dalog_ref, ddtb_ref):
        rows = [pl.ds(j * c, c) for j in range(n)]
        _, vj = jax.vjp(_dn_local, [_dn_heads(qkv_ref, r) for r in rows], [_dn_heads(qkv_ref, r, D_DN) for r in rows],
                        [_dn_heads(qkv_ref, r, 2 * D_DN) for r in rows], [ab_ref[r, :] for r in rows], alog_ref[...],
                        dtb_ref[...])
        dattn = [[dattn_ref[r, h * c:(h + 1) * c] for h in range(DN_HEADS)] for r in rows]
        dq, dk, dv, dab, dalog, ddtb = vj(([_dn_heads(dval_ref, r) for r in rows], [_dn_heads(dkcd_ref, r) for r in rows],
                                           dattn, [_dn_heads(dqd_ref, r) for r in rows],
                                           [_dn_heads(dkd_ref, r) for r in rows], [del_ref[j] for j in range(n)]))
        for j, r in enumerate(rows):
            for h in range(DN_HEADS):
                lo, hi = h * DN_HEAD_DIM, (h + 1) * DN_HEAD_DIM
                dqkv_ref[r, lo:hi] = dq[j][h]
                dqkv_ref[r, D_DN + lo:D_DN + hi] = dk[j][h]
                dqkv_ref[r, 2 * D_DN + lo:2 * D_DN + hi] = dv[j][h]
            dab_ref[r, :] = dab[j]
        first = pl.program_id(0) == 0
        _acc(dalog_ref, dalog, first)
        _acc(ddtb_ref, ddtb, first)

    wide = _rows(n * c, D_DN)
    return pl.pallas_call(
        body, name=name, grid=(t // (n * c),),
        in_specs=[_rows(n * c, 3 * D_DN), _rows(n * c, LANE), _full((1, LANE)), _full((1, LANE)),
                  wide, wide, _rows(n * c, DN_ATTN), wide, wide, pl.BlockSpec((n, 1, LANE), lambda i: (i, 0, 0))],
        out_specs=[_rows(n * c, 3 * D_DN), _rows(n * c, LANE), _full((1, LANE)), _full((1, LANE))],
        out_shape=[_sds((t, 3 * D_DN)), _sds((t, LANE)), _sds((1, LANE)), _sds((1, LANE))],
        compiler_params=_cp(1),
    )(qkv, ab, alog, dtb, *cts)


def _seq_view(a, nb):
    return a.reshape((nb, a.shape[0] // nb) + a.shape[1:])


def _dn_chains(nb):
    return [(b, h) for b in range(nb) for h in range(DN_HEADS)]


DN_SCAN_CHUNKS = 4


def _dn_step_operands(val_ref, kcd_ref, attn_ref, qd_ref, kd_ref, el_ref, gg_ref, nb, j):
    chains = _dn_chains(nb)
    c = DN_CHUNK
    rows = pl.ds(j * c, c)

    def wide(ref):
        return [ref[b, rows, h * DN_HEAD_DIM:(h + 1) * DN_HEAD_DIM] for b, h in chains]

    return (wide(val_ref), wide(kcd_ref), [attn_ref[b, rows, h * c:(h + 1) * c] for b, h in chains], wide(qd_ref),
            wide(kd_ref), [el_ref[b, j, :, h:h + 1] for b, h in chains], wide(gg_ref))


def dn_scan_fwd(loc, gg, ng, nb, name):
    val, kcd, attn, qd, kd, el = loc
    t = val.shape[0]
    c, n = DN_CHUNK, DN_SCAN_CHUNKS
    nc = t // nb // c
    ns = nb * DN_HEADS

    def body(val_ref, kcd_ref, attn_ref, qd_ref, kd_ref, el_ref, gg_ref, ng_ref, y_ref, ss_ref, st):
        @pl.when(pl.program_id(0) == 0)
        def _():
            st[...] = jnp.zeros_like(st)

        sts = [st[i] for i in range(ns)]
        for j in range(n):
            for i in range(ns):
                ss_ref[j, i] = sts[i]
            ys, sts = _dn_step(*_dn_step_operands(val_ref, kcd_ref, attn_ref, qd_ref, kd_ref, el_ref, gg_ref, nb, j),
                               sts, ng_ref[...])
            for i, (b, h) in enumerate(_dn_chains(nb)):
                y_ref[b, pl.ds(j * c, c), h * DN_HEAD_DIM:(h + 1) * DN_HEAD_DIM] = ys[i]
        for i in range(ns):
            st[i] = sts[i]

    def blk(w):
        return pl.BlockSpec((nb, n * c, w), lambda k: (0, k, 0))

    el_spec = pl.BlockSpec((nb, n, 1, LANE), lambda k: (0, k, 0, 0))
    y, ss = pl.pallas_call(
        body, name=name, grid=(nc // n,),
        in_specs=[blk(D_DN), blk(D_DN), blk(DN_ATTN), blk(D_DN), blk(D_DN), el_spec, blk(D_DN), _full((1, LANE))],
        out_specs=[blk(D_DN), pl.BlockSpec((n, ns, DN_HEAD_DIM, DN_HEAD_DIM), lambda k: (k, 0, 0, 0))],
        out_shape=[_sds((nb, t // nb, D_DN)), _sds((nc, ns, DN_HEAD_DIM, DN_HEAD_DIM))],
        scratch_shapes=[pltpu.VMEM((ns, DN_HEAD_DIM, DN_HEAD_DIM), f32)],
        compiler_params=_cp(1, VMEM_BIG),
    )(_seq_view(val, nb), _seq_view(kcd, nb), _seq_view(attn, nb), _seq_view(qd, nb), _seq_view(kd, nb),
      el.reshape(nb, nc, 1, LANE), _seq_view(gg, nb), ng)
    return y.reshape(t, D_DN), ss


def dn_scan_bwd(loc, gg, ng, ss, dy, nb, name):
    val, kcd, attn, qd, kd, el = loc
    t = val.shape[0]
    c, n = DN_CHUNK, DN_SCAN_CHUNKS
    nc = t // nb // c
    ns = nb * DN_HEADS
    steps = nc // n

    def body(val_ref, kcd_ref, attn_ref, qd_ref, kd_ref, el_ref, gg_ref, ng_ref, ss_ref, dy_ref,
             dval_ref, dkcd_ref, dattn_ref, dqd_ref, dkd_ref, del_ref, dgg_ref, dng_ref, dst):
        @pl.when(pl.program_id(0) == 0)
        def _():
            dst[...] = jnp.zeros_like(dst)

        lane = lax.broadcasted_iota(jnp.int32, (1, LANE), 1)
        chains = _dn_chains(nb)
        ds = [dst[i] for i in range(ns)]
        dng_tot = jnp.zeros((1, LANE), f32)
        for j in reversed(range(n)):
            rows = pl.ds(j * c, c)
            _, vj = jax.vjp(_dn_step,
                            *_dn_step_operands(val_ref, kcd_ref, attn_ref, qd_ref, kd_ref, el_ref, gg_ref, nb, j),
                            [ss_ref[j, i] for i in range(ns)], ng_ref[...])
            dys = [dy_ref[b, rows, h * DN_HEAD_DIM:(h + 1) * DN_HEAD_DIM] for b, h in chains]
            dval, dkcd, dattn, dqd, dkd, dlast, dgg, ds, dng = vj((dys, ds))
            dng_tot = dng_tot + dng
            del_rows = [jnp.zeros((1, LANE), f32) for _ in range(nb)]
            for i, (b, h) in enumerate(chains):
                cols = slice(h * DN_HEAD_DIM, (h + 1) * DN_HEAD_DIM)
                dval_ref[b, rows, cols] = dval[i]
                dkcd_ref[b, rows, cols] = dkcd[i]
                dattn_ref[b, rows, h * c:(h + 1) * c] = dattn[i]
                dqd_ref[b, rows, cols] = dqd[i]
                dkd_ref[b, rows, cols] = dkd[i]
                dgg_ref[b, rows, cols] = dgg[i]
                del_rows[b] = del_rows[b] + jnp.where(lane == h, dlast[i], 0.0)
            for b in range(nb):
                del_ref[b, j] = del_rows[b]
        for i in range(ns):
            dst[i] = ds[i]
        _acc(dng_ref, dng_tot, pl.program_id(0) == 0)

    def blk(w):
        return pl.BlockSpec((nb, n * c, w), lambda k: (0, steps - 1 - k, 0))

    el_spec = pl.BlockSpec((nb, n, 1, LANE), lambda k: (0, steps - 1 - k, 0, 0))
    outs = pl.pallas_call(
        body, name=name, grid=(steps,),
        in_specs=[blk(D_DN), blk(D_DN), blk(DN_ATTN), blk(D_DN), blk(D_DN), el_spec, blk(D_DN), _full((1, LANE)),
                  pl.BlockSpec((n, ns, DN_HEAD_DIM, DN_HEAD_DIM), lambda k: (steps - 1 - k, 0, 0, 0)), blk(D_DN)],
        out_specs=[blk(D_DN), blk(D_DN), blk(DN_ATTN), blk(D_DN), blk(D_DN), el_spec, blk(D_DN), _full((1, LANE))],
        out_shape=[_sds((nb, t // nb, D_DN)), _sds((nb, t // nb, D_DN)), _sds((nb, t // nb, DN_ATTN)),
                   _sds((nb, t // nb, D_DN)), _sds((nb, t // nb, D_DN)), _sds((nb, nc, 1, LANE)),
                   _sds((nb, t // nb, D_DN)), _sds((1, LANE))],
        scratch_shapes=[pltpu.VMEM((ns, DN_HEAD_DIM, DN_HEAD_DIM), f32)],
        compiler_params=_cp(1, VMEM_BIG),
    )(_seq_view(val, nb), _seq_view(kcd, nb), _seq_view(attn, nb), _seq_view(qd, nb), _seq_view(kd, nb),
      el.reshape(nb, nc, 1, LANE), _seq_view(gg, nb), ng, ss, _seq_view(dy, nb))
    dloc = [o.reshape((t,) + o.shape[2:]) for o in outs[:5]] + [outs[5].reshape(t // c, 1, LANE)]
    return dloc, outs[6].reshape(t, D_DN), outs[7]


SG_ROWS = 512


def sg_fwd(z, lng, lnb, w, bt, name):
    t = z.shape[0]

    def body(z_ref, lng_ref, lnb_ref, w_ref, bt_ref, y_ref):
        ws = [w_ref[h] for h in range(SG_HEADS)]
        for k in range(SG_ROWS // SG_CHUNK):
            r = pl.ds(k * SG_CHUNK, SG_CHUNK)
            y_ref[r, :] = _sg_chunk(z_ref[r, :D_SG], z_ref[r, D_SG:2 * D_SG], z_ref[r, 2 * D_SG:], lng_ref[...],
                                    lnb_ref[...], ws, bt_ref[...])

    return pl.pallas_call(
        body, name=name, grid=(t // SG_ROWS,),
        in_specs=[_rows(SG_ROWS, 3 * D_SG), _full((1, D_SG)), _full((1, D_SG)),
                  _full((SG_HEADS, SG_CHUNK, SG_CHUNK)), _full((SG_CHUNK, LANE))],
        out_specs=_rows(SG_ROWS, D_SG),
        out_shape=_sds((t, D_SG)),
        compiler_params=_cp(1),
    )(z, lng, lnb, w, bt)


def sg_bwd(z, lng, lnb, w, bt, dy, name):
    t = z.shape[0]

    def body(z_ref, lng_ref, lnb_ref, w_ref, bt_ref, dy_ref, dz_ref, dlng_ref, dlnb_ref, dw_ref, dbt_ref):
        ws = [w_ref[h] for h in range(SG_HEADS)]
        tot = None
        for k in range(SG_ROWS // SG_CHUNK):
            r = pl.ds(k * SG_CHUNK, SG_CHUNK)
            _, vj = jax.vjp(_sg_chunk, z_ref[r, :D_SG], z_ref[r, D_SG:2 * D_SG], z_ref[r, 2 * D_SG:], lng_ref[...],
                            lnb_ref[...], ws, bt_ref[...])
            du, dv, dgate, dlng, dlnb, dws, dbt = vj(dy_ref[r, :])
            dz_ref[r, :D_SG] = du
            dz_ref[r, D_SG:2 * D_SG] = dv
            dz_ref[r, 2 * D_SG:] = dgate
            part = [dlng, dlnb, dbt] + list(dws)
            tot = part if tot is None else [a + b for a, b in zip(tot, part)]
        first = pl.program_id(0) == 0
        _acc(dlng_ref, tot[0], first)
        _acc(dlnb_ref, tot[1], first)
        _acc(dbt_ref, tot[2], first)
        for h in range(SG_HEADS):
            @pl.when(first)
            def _():
                dw_ref[h] = tot[3 + h]

            @pl.when(jnp.logical_not(first))
            def _():
                dw_ref[h] += tot[3 + h]

    return pl.pallas_call(
        body, name=name, grid=(t // SG_ROWS,),
        in_specs=[_rows(SG_ROWS, 3 * D_SG), _full((1, D_SG)), _full((1, D_SG)),
                  _full((SG_HEADS, SG_CHUNK, SG_CHUNK)), _full((SG_CHUNK, LANE)), _rows(SG_ROWS, D_SG)],
        out_specs=[_rows(SG_ROWS, 3 * D_SG), _full((1, D_SG)), _full((1, D_SG)),
                   _full((SG_HEADS, SG_CHUNK, SG_CHUNK)), _full((SG_CHUNK, LANE))],
        out_shape=[_sds((t, 3 * D_SG)), _sds((1, D_SG)), _sds((1, D_SG)), _sds((SG_HEADS, SG_CHUNK, SG_CHUNK)),
                   _sds((SG_CHUNK, LANE))],
        compiler_params=_cp(1),
    )(z, lng, lnb, w, bt, dy)


def add_pairs(a_list, b_list, name):
    n = len(a_list)

    def body(*refs):
        for a_ref, b_ref, o_ref in zip(refs[:n], refs[n:2 * n], refs[2 * n:]):
            o_ref[...] = (a_ref[...].astype(f32) + b_ref[...].astype(f32)).astype(o_ref.dtype)

    return pl.pallas_call(
        body, name=name, out_shape=[_sds(a.shape, a.dtype) for a in a_list],
        compiler_params=pltpu.CompilerParams(vmem_limit_bytes=VMEM_BIG),
    )(*a_list, *b_list)


def sum_adamw(half, recv, w, m, v, name):
    _, r, c = w.shape
    tr = 256 if r % 256 == 0 else r

    def body(half_ref, recv_ref, w_ref, m_ref, v_ref, g_ref, d_ref, nm_ref, nv_ref):
        g = recv_ref[0].astype(f32)
        for k in range(1, N_CHIPS):
            g = g + recv_ref[k].astype(f32)
        wv = w_ref[...]
        nm = ADAM_B1 * m_ref[...] + (1.0 - ADAM_B1) * g
        nv = ADAM_B2 * v_ref[...] + (1.0 - ADAM_B2) * jnp.square(g)
        m_hat = nm / (1.0 - ADAM_B1 ** ADAM_STEP)
        v_hat = nv / (1.0 - ADAM_B2 ** ADAM_STEP)
        g_ref[...] = g
        d_ref[...] = -ADAM_LR * (m_hat / (jnp.sqrt(v_hat) + ADAM_EPS) + ADAM_WD * wv)
        nm_ref[...] = nm
        nv_ref[...] = nv

    own = pl.BlockSpec((None, tr, c), lambda i, h: (h[0], i, 0))
    return pl.pallas_call(
        body, name=name,
        grid_spec=pltpu.PrefetchScalarGridSpec(
            num_scalar_prefetch=1, grid=(r // tr,),
            in_specs=[pl.BlockSpec((N_CHIPS, tr, c), lambda i, h: (0, i, 0)), own, own, own], out_specs=[own] * 4),
        out_shape=[_sds((2, r, c))] * 4,
        compiler_params=_cp(1, VMEM_BIG),
    )(half, recv, w, m, v)


_ANY = pl.BlockSpec(memory_space=pl.ANY)
_MESH = pl.DeviceIdType.MESH


def _flip(v, bit):
    return 1 - v if bit else v


_CHIP_RELS = ((1, 0), (0, 1), (1, 1))


def _piece(ref, kind, q):
    if kind[0] == "slot":
        return ref.at[q]
    if kind[0] == "all":
        return ref
    _, axis, n = kind
    return ref.at[(slice(None),) * axis + (pl.ds(q * n, n),)]


def _piece_shape(shape, kind):
    if kind[0] == "slot":
        return tuple(shape[1:])
    if kind[0] == "all":
        return tuple(shape)
    _, axis, n = kind
    return tuple(shape[:axis]) + (n,) + tuple(shape[axis + 1:])


def gather_weights(shards, kinds, name):
    n = len(shards)

    def out_shape(s, kind):
        if kind[0] == "slot":
            return (N_CHIPS,) + tuple(s.shape)
        _, axis, w = kind
        return tuple(s.shape[:axis + 1]) + (N_CHIPS * w,) + tuple(s.shape[axis + 2:])

    def place(o_ref, kind, q, layer):
        if kind[0] == "slot":
            return o_ref.at[q, layer]
        return _piece(o_ref.at[layer], kind, q)

    def body(*refs):
        s_refs, o_refs = refs[:n], refs[n:2 * n]
        send_sems, recv_sems, fwd_send_sems, fwd_recv_sems = refs[2 * n:]
        x, y, c = lax.axis_index("x"), lax.axis_index("y"), lax.axis_index("c")
        mine = 2 * x + y
        sends, arrivals, forwards, fwd_arrivals = [], [], [], []
        for r, (fx, fy) in enumerate(_CHIP_RELS):
            px, py = _flip(x, fx), _flip(y, fy)
            peer = 2 * px + py
            for k in range(n):
                s = r * n + k
                sends.append(pltpu.make_async_remote_copy(
                    src_ref=s_refs[k].at[c], dst_ref=place(o_refs[k], kinds[k], mine, c), send_sem=send_sems.at[s],
                    recv_sem=recv_sems.at[s], device_id=(px, py, c), device_id_type=_MESH))
                arrivals.append(pltpu.make_async_remote_copy(
                    src_ref=s_refs[k].at[c], dst_ref=place(o_refs[k], kinds[k], peer, c), send_sem=send_sems.at[s],
                    recv_sem=recv_sems.at[s], device_id=(px, py, c), device_id_type=_MESH))
                block = place(o_refs[k], kinds[k], peer, c)
                forwards.append(pltpu.make_async_remote_copy(
                    src_ref=block, dst_ref=block, send_sem=fwd_send_sems.at[s], recv_sem=fwd_recv_sems.at[s],
                    device_id=(x, y, 1 - c), device_id_type=_MESH))
                other = place(o_refs[k], kinds[k], peer, 1 - c)
                fwd_arrivals.append(pltpu.make_async_remote_copy(
                    src_ref=other, dst_ref=other, send_sem=fwd_send_sems.at[s], recv_sem=fwd_recv_sems.at[s],
                    device_id=(x, y, 1 - c), device_id_type=_MESH))
        for cp in sends:
            cp.start()
        for arrived, fwd in zip(arrivals, forwards):
            arrived.wait_recv()
            fwd.start()
        for cp in fwd_arrivals:
            cp.wait_recv()
        for cp in sends + forwards:
            cp.wait_send()

    m = len(_CHIP_RELS) * n
    return pl.pallas_call(
        body, name=name, in_specs=[_ANY] * n, out_specs=[_ANY] * n,
        out_shape=[_sds(out_shape(s, k), s.dtype) for s, k in zip(shards, kinds)],
        scratch_shapes=[pltpu.SemaphoreType.DMA((m,))] * 4,
    )(*shards)


def exchange_halves(gs, name):
    n = len(gs)

    def body(*refs):
        g_refs, got_refs, (send_sems, recv_sems) = refs[:n], refs[n:2 * n], refs[2 * n:]
        x, y, c = lax.axis_index("x"), lax.axis_index("y"), lax.axis_index("c")
        swaps = [pltpu.make_async_remote_copy(
            src_ref=g_refs[k].at[1 - c], dst_ref=got_refs[k], send_sem=send_sems.at[k], recv_sem=recv_sems.at[k],
            device_id=(x, y, 1 - c), device_id_type=_MESH) for k in range(n)]
        for cp in swaps:
            cp.start()
        for cp in swaps:
            cp.wait()

    return pl.pallas_call(
        body, name=name, in_specs=[_ANY] * n, out_specs=[_ANY] * n,
        out_shape=[_sds(g.shape[1:], g.dtype) for g in gs],
        scratch_shapes=[pltpu.SemaphoreType.DMA((n,)), pltpu.SemaphoreType.DMA((n,))],
    )(*gs)


def reduce_to_chips(ts, kinds, name):
    n = len(ts)

    def body(*refs):
        t_refs, o_refs, (send_sems, recv_sems) = refs[:n], refs[n:2 * n], refs[2 * n:]
        x, y, c = lax.axis_index("x"), lax.axis_index("y"), lax.axis_index("c")
        mine = 2 * x + y
        sends, arrivals = [], []
        for r, (fx, fy) in enumerate(_CHIP_RELS):
            px, py = _flip(x, fx), _flip(y, fy)
            peer = 2 * px + py
            for k in range(n):
                s = r * n + k
                sends.append(pltpu.make_async_remote_copy(
                    src_ref=_piece(t_refs[k], kinds[k], peer), dst_ref=o_refs[k].at[mine], send_sem=send_sems.at[s],
                    recv_sem=recv_sems.at[s], device_id=(px, py, c), device_id_type=_MESH))
                arrivals.append(pltpu.make_async_remote_copy(
                    src_ref=_piece(t_refs[k], kinds[k], peer), dst_ref=o_refs[k].at[peer], send_sem=send_sems.at[s],
                    recv_sem=recv_sems.at[s], device_id=(px, py, c), device_id_type=_MESH))
        for cp in sends:
            cp.start()
        for cp in arrivals:
            cp.wait_recv()
        for cp in sends:
            cp.wait_send()

    m = len(_CHIP_RELS) * n
    return pl.pallas_call(
        body, name=name, in_specs=[_ANY] * n, out_specs=[_ANY] * n,
        out_shape=[_sds((N_CHIPS,) + _piece_shape(t.shape, k), t.dtype) for t, k in zip(ts, kinds)],
        scratch_shapes=[pltpu.SemaphoreType.DMA((m,)), pltpu.SemaphoreType.DMA((m,))],
    )(*ts)


def share_halves(rs, name):
    n = len(rs)

    def body(*refs):
        o_refs, (send_sems, recv_sems) = refs[n:2 * n], refs[2 * n:]
        x, y, c = lax.axis_index("x"), lax.axis_index("y"), lax.axis_index("c")
        swaps = [pltpu.make_async_remote_copy(
            src_ref=o_refs[k].at[c], dst_ref=o_refs[k].at[c], send_sem=send_sems.at[k], recv_sem=recv_sems.at[k],
            device_id=(x, y, 1 - c), device_id_type=_MESH) for k in range(n)]
        arrivals = [pltpu.make_async_remote_copy(
            src_ref=o_refs[k].at[c], dst_ref=o_refs[k].at[1 - c], send_sem=send_sems.at[k], recv_sem=recv_sems.at[k],
            device_id=(x, y, 1 - c), device_id_type=_MESH) for k in range(n)]
        for cp in swaps:
            cp.start()
        for cp in arrivals:
            cp.wait_recv()
        for cp in swaps:
            cp.wait_send()

    return pl.pallas_call(
        body, name=name, in_specs=[_ANY] * n, out_specs=[_ANY] * n,
        out_shape=[_sds(r.shape, r.dtype) for r in rs], input_output_aliases={k: k for k in range(n)},
        scratch_shapes=[pltpu.SemaphoreType.DMA((n,)), pltpu.SemaphoreType.DMA((n,))],
    )(*rs)


def _pack_rows(parts, mult, dtype):
    blocks = []
    for a in parts:
        flat = a.reshape(-1).astype(dtype)
        rows = -(-flat.shape[0] // LANE)
        blocks.append(jnp.pad(flat, (0, rows * LANE - flat.shape[0])).reshape(rows, LANE))
    buf = jnp.concatenate(blocks, axis=0)
    return jnp.pad(buf, ((0, -buf.shape[0] % mult), (0, 0)))


def _unpack_rows(buf, shapes):
    out, row = [], 0
    for s in shapes:
        n = 1
        for d in s:
            n *= d
        rows = -(-n // LANE)
        out.append(buf[row:row + rows].reshape(-1)[:n].reshape(s))
        row += rows
    return out


def _permuted_from_shards(shards):
    parts = []
    for lo, hi in GROUP_COLS:
        for q in range(N_CHIPS):
            a, b = max(lo, q * SHARD_COLS), min(hi, (q + 1) * SHARD_COLS)
            if a < b:
                parts.append(shards[q][..., a - q * SHARD_COLS:b - q * SHARD_COLS])
    pad = jnp.zeros(shards[0].shape[:-1] + (D_IN_PAD - D_IN,), shards[0].dtype)
    return jnp.concatenate(parts + [pad], axis=-1)


def _shards_from_groups(groups):
    in_order = sorted(range(len(GROUP_COLS)), key=lambda j: GROUP_COLS[j][0])
    shards = []
    for q in range(N_CHIPS):
        parts = []
        for j in in_order:
            lo, hi = GROUP_COLS[j]
            a, b = max(lo, q * SHARD_COLS), min(hi, (q + 1) * SHARD_COLS)
            if a < b:
                parts.append(groups[j][..., a - lo:b - lo])
        shards.append(jnp.concatenate(parts, axis=-1))
    return shards


def _expand_b(b):
    eye = jnp.eye(SSM_GROUPS, dtype=b.dtype)
    return jnp.einsum("gnc,gh->gchn", b, eye).reshape(D_SSM, N_STATE)


def _extract_b(e):
    return jnp.einsum("gcgn->gnc", e.reshape(SSM_GROUPS, SSM_GROUP, SSM_GROUPS, SSM_STATE))


def _expand_c(c):
    eye = jnp.eye(SSM_GROUPS, dtype=c.dtype)
    return jnp.einsum("gcn,gh->gnhc", c, eye).reshape(N_STATE, D_SSM)


def _extract_c(e):
    return jnp.einsum("gngc->gcn", e.reshape(SSM_GROUPS, SSM_STATE, SSM_GROUPS, SSM_GROUP))


def _lane_row(v):
    return jnp.pad(v, (0, LANE - v.shape[0])).reshape(1, LANE)


def _layer_params(w, l):
    return dict(
        norm_g=w["norm_g"][l][None], win=w["w_in_perm"][l], wout=w["w_out"][l].astype(_MXU),
        pg=w["ple_norm_g"][l][None], wgate=w["w_ple_gate"][l].astype(_MXU), wple=w["w_ple"][l].astype(_MXU),
        are=w["ssm_a_re"][l].reshape(1, N_STATE), aim=w["ssm_a_im"][l].reshape(1, N_STATE),
        ls=jnp.repeat(w["ssm_log_step"][l], SSM_STATE).reshape(1, N_STATE),
        bre=_expand_b(w["ssm_b_re"][l]), bim=_expand_b(w["ssm_b_im"][l]),
        cr=_expand_c(w["ssm_c_re"][l]), ci=_expand_c(w["ssm_c_im"][l]),
        dr=w["ssm_d"][l].reshape(1, D_SSM), wglu=w["ssm_w_glu"][l].astype(f32), bglu=w["ssm_b_glu"][l][None],
        convw=w["dn_conv_w"][l], alog=_lane_row(w["dn_a_log"][l]), dtb=_lane_row(w["dn_dt_bias"][l]),
        ng=w["dn_norm_g"][l][None],
        lng=w["sg_ln_g"][l][None], lnb=w["sg_ln_b"][l][None], sgw=w["sg_w"][l],
        bt=jnp.pad(w["sg_b"][l].T, ((0, 0), (0, LANE - SG_HEADS))),
    )


def _layer_fwd(x, p, lp, nb, tag):
    seq = x.shape[0] // nb
    h, zs, zq, zg, zsg, zab = in_fwd(x, lp["norm_g"], lp["win"], f"in_fwd{tag}")
    prep = s5_prep_fwd(lp["are"], lp["aim"], lp["ls"], lp["bre"], lp["bim"], f"s5_prep_fwd{tag}")
    s5p = tuple(prep) + (lp["cr"], lp["ci"], lp["dr"], lp["wglu"], lp["bglu"])
    ys, hs = s5_fwd(zs, s5p, nb, f"s5_fwd{tag}")
    qkv = dn_pre_fwd(zq, lp["convw"], seq, f"dn_pre_fwd{tag}")
    loc = dn_local_fwd(qkv, zab, lp["alog"], lp["dtb"], f"dn_local_fwd{tag}")
    yd, ss = dn_scan_fwd(loc, zg, lp["ng"], nb, f"dn_scan_fwd{tag}")
    yg = sg_fwd(zsg, lp["lng"], lp["lnb"], lp["sgw"], lp["bt"], f"sg_fwd{tag}")
    x2, x1, y, hn = post_fwd(x, ys, yd, yg, p, lp["wout"], lp["pg"], lp["wgate"], lp["wple"], f"post_fwd{tag}")
    saved = dict(x=x, h=h, zs=zs, zq=zq, zg=zg, zsg=zsg, zab=zab, s5p=s5p, hs=hs, qkv=qkv, loc=loc, ss=ss, x1=x1, y=y, hn=hn, p=p)
    return x2, saved


def _layer_bwd(dx2, sv, lp, nb, tag):
    seq = dx2.shape[0] // nb
    dx1, dgp, dpp, dys, dyd, dyg, dpg = post_bwd(dx2, sv["x1"], sv["hn"], sv["p"], lp["wout"], lp["pg"], lp["wgate"],
                                                 lp["wple"], f"post_bwd{tag}")
    g = {}
    g["w_out"] = wgrad(sv["y"], dx1, f"wgrad_out{tag}")
    g["w_ple_gate"] = wgrad(sv["hn"], dgp, f"wgrad_gate{tag}")
    g["w_ple"] = wgrad(sv["p"], dpp, f"wgrad_ple{tag}")
    g["ple_norm_g"] = dpg[0]
    dzsg, dlng, dlnb, dsgw, dbt = sg_bwd(sv["zsg"], lp["lng"], lp["lnb"], lp["sgw"], lp["bt"], dyg, f"sg_bwd{tag}")
    g["sg_ln_g"], g["sg_ln_b"], g["sg_w"], g["sg_b"] = dlng[0], dlnb[0], dsgw, dbt[:, :SG_HEADS].T
    dloc, dzg, dng = dn_scan_bwd(sv["loc"], sv["zg"], lp["ng"], sv["ss"], dyd, nb, f"dn_scan_bwd{tag}")
    dqkv, dzab, dalog, ddtb = dn_local_bwd(sv["qkv"], sv["zab"], lp["alog"], lp["dtb"], dloc, f"dn_local_bwd{tag}")
    dzq, dconv = dn_pre_bwd(sv["zq"], lp["convw"], dqkv, seq, f"dn_pre_bwd{tag}")
    g["dn_conv_w"], g["dn_a_log"], g["dn_dt_bias"], g["dn_norm_g"] = dconv, dalog[0, :DN_HEADS], ddtb[0, :DN_HEADS], dng[0]
    s5out = s5_bwd(sv["zs"], sv["s5p"], sv["hs"], dys, nb, f"s5_bwd{tag}")
    dzs, dprep, (dcr, dci, ddr, dwglu, dbglu) = s5out[0], s5out[1:1 + S5_PREPARED], s5out[1 + S5_PREPARED:]
    dare, daim, dls, dbre, dbim = s5_prep_bwd(lp["are"], lp["aim"], lp["ls"], lp["bre"], lp["bim"], dprep,
                                              f"s5_prep_bwd{tag}")
    g["ssm_a_re"] = dare.reshape(SSM_GROUPS, SSM_STATE)
    g["ssm_a_im"] = daim.reshape(SSM_GROUPS, SSM_STATE)
    g["ssm_log_step"] = dls.reshape(SSM_GROUPS, SSM_STATE).sum(axis=1)
    g["ssm_b_re"], g["ssm_b_im"] = _extract_b(dbre), _extract_b(dbim)
    g["ssm_c_re"], g["ssm_c_im"] = _extract_c(dcr), _extract_c(dci)
    g["ssm_d"] = ddr.reshape(SSM_GROUPS, SSM_GROUP)
    g["ssm_w_glu"], g["ssm_b_glu"] = dwglu, dbglu[0]
    dzs_all = (dzs, dzq, dzg, dzsg, dzab)
    dx, dng_in = in_bwd(sv["x"], lp["norm_g"], lp["win"], dzs_all, dx1, f"in_bwd{tag}")
    g["w_in_pieces"] = [wgrad(sv["h"], dz, f"wgrad_in{k}{tag}") for k, dz in enumerate(dzs_all)]
    g["norm_g"] = dng_in[0]
    return dx, g


def _local_step(x, p, target, w, nb):
    lps = [_layer_params(w, l) for l in range(DEPTH)]
    saved = []
    for l in range(DEPTH):
        x, sv = _layer_fwd(x, p[l], lps[l], nb, f"_l{l}")
        saved.append(sv)
    loss_blk, dx, dfg = loss_fwd_bwd(x, w["final_norm_g"][None], target, "loss")
    grads = [None] * DEPTH
    for l in reversed(range(DEPTH)):
        dx, grads[l] = _layer_bwd(dx, saved[l], lps[l], nb, f"_l{l}")
    out = {k: jnp.stack([grads[l][k] for l in range(DEPTH)]) for k in grads[0] if k != "w_in_pieces"}
    out["w_in_pieces"] = [grads[l]["w_in_pieces"] for l in range(DEPTH)]
    out["final_norm_g"] = dfg[0]
    return loss_blk[0, 0], dx, out


def kernel(x, p, norm_g, w_in, ssm_a_re, ssm_a_im, ssm_b_re, ssm_b_im, ssm_c_re, ssm_c_im, ssm_d, ssm_log_step, ssm_w_glu, ssm_b_glu, dn_conv_w, dn_a_log, dn_dt_bias, dn_norm_g, sg_ln_g, sg_ln_b, sg_w, sg_b, w_out, ple_norm_g, w_ple_gate, w_ple, final_norm_g, loss_target, m_norm_g, m_w_in, m_ssm_a_re, m_ssm_a_im, m_ssm_b_re, m_ssm_b_im, m_ssm_c_re, m_ssm_c_im, m_ssm_d, m_ssm_log_step, m_ssm_w_glu, m_ssm_b_glu, m_dn_conv_w, m_dn_a_log, m_dn_dt_bias, m_dn_norm_g, m_sg_ln_g, m_sg_ln_b, m_sg_w, m_sg_b, m_w_out, m_ple_norm_g, m_w_ple_gate, m_w_ple, m_final_norm_g, v_norm_g, v_w_in, v_ssm_a_re, v_ssm_a_im, v_ssm_b_re, v_ssm_b_im, v_ssm_c_re, v_ssm_c_im, v_ssm_d, v_ssm_log_step, v_ssm_w_glu, v_ssm_b_glu, v_dn_conv_w, v_dn_a_log, v_dn_dt_bias, v_dn_norm_g, v_sg_ln_g, v_sg_ln_b, v_sg_w, v_sg_b, v_w_out, v_ple_norm_g, v_w_ple_gate, v_w_ple, v_final_norm_g):
    args = locals()
    w = {n: args[n] for n in WEIGHTS}
    m = {n: args["m_" + n] for n in WEIGHTS}
    v = {n: args["v_" + n] for n in WEIGHTS}
    nb, seq = x.shape[0], x.shape[1]
    t = nb * seq

    full = _gather_full(w)
    loss_local, dx, grads = _local_step(x.reshape(t, D_MODEL), p.reshape(DEPTH, t, D_PLE),
                                        loss_target.reshape(t, D_MODEL), full, nb)
    outs, loss = _reduce_and_update(grads, w, m, v, loss_local)
    return (loss, dx.reshape(nb, seq, D_MODEL), *[outs[0][n] for n in WEIGHTS], *[outs[1][n] for n in WEIGHTS],
            *[outs[2][n] for n in WEIGHTS], *[outs[3][n] for n in WEIGHTS])


def _gather_full(w):
    sh_names = [n for n, _ in SHARDED]
    shards = [w[n] if n == "dn_conv_w" else w[n].astype(_COMM) for n in sh_names]
    gathered = gather_weights(shards, [k for _, k in SHARDED], "gather_weights")
    chip = 2 * lax.axis_index("x") + lax.axis_index("y")
    full = {n: w[n] for n in REPLICATED}
    for (n, kind), shard, got in zip(SHARDED, shards, gathered):
        if kind[0] == "slot":
            full[n] = lax.dynamic_update_index_in_dim(got, shard, chip, 0)
        else:
            full[n] = lax.dynamic_update_slice_in_dim(got, shard, chip * kind[2], axis=kind[1] + 1)
    slots = full.pop("w_in")
    full["w_in_perm"] = _permuted_from_shards([slots[q] for q in range(N_CHIPS)]).astype(_MXU)
    return full


def _reduce_and_update(grads, w, m, v, loss_local):
    sh_names = [n for n, _ in SHARDED]
    sh_kinds = [k for _, k in SHARDED]

    def pack_small(d, last):
        buf = _pack_rows([d[n] for n in REPLICATED] + [last], 512, f32)
        return buf.reshape(2, buf.shape[0] // 2, LANE)

    no_state = jnp.zeros((1,), f32)
    grads["w_in"] = jnp.stack([jnp.stack(_shards_from_groups(pieces)) for pieces in grads["w_in_pieces"]])
    gs = [grads[n] if n == "dn_conv_w" else grads[n].astype(_COMM) for n in sh_names]
    gs.append(pack_small(grads, loss_local.reshape(1)))
    kinds = sh_kinds + [("all",)]
    core = lax.axis_index("c")
    chip = 2 * lax.axis_index("x") + lax.axis_index("y")
    got = exchange_halves(gs, "exchange_halves")
    sums = add_pairs([lax.dynamic_index_in_dim(g, core, 0, keepdims=False) for g in gs], got, "add_halves")
    parts = list(reduce_to_chips(sums, kinds, "reduce_to_chips"))
    for k, (kind, total) in enumerate(zip(kinds, sums)):
        if kind[0] == "slot":
            own = lax.dynamic_index_in_dim(total, chip, 0, keepdims=False)
        elif kind[0] == "win":
            own = lax.dynamic_slice_in_dim(total, chip * kind[2], kind[2], axis=kind[1])
        else:
            own = total
        parts[k] = lax.dynamic_update_index_in_dim(parts[k], own, chip, 0)
    states = [(w[n], m[n], v[n]) for n in sh_names]
    states.append((pack_small(w, no_state), pack_small(m, no_state), pack_small(v, no_state)))
    half = core.astype(jnp.int32).reshape(1)
    results = []
    for n, part, (wn, mn, vn) in zip(sh_names + ["replicated"], parts, states):
        results += sum_adamw(half, part, wn, mn, vn, f"adamw_{n}")
    shared = share_halves(results, "share_halves")
    rep_shapes = [w[n].shape for n in REPLICATED] + [(1,)]
    outs = []
    for j in range(4):
        d = {n: shared[4 * k + j] for k, n in enumerate(sh_names)}
        small = shared[4 * len(sh_names) + j]
        d.update(zip(REPLICATED + ("loss",), _unpack_rows(small.reshape(-1, LANE), rep_shapes)))
        outs.append(d)
    return outs, outs[0]["loss"][0]
```

```python
import functools

import jax
import jax.numpy as jnp
from jax import lax
from jax.experimental import pallas as pl
from jax.experimental.pallas import tpu as pltpu

f32 = jnp.float32
bf16 = jnp.bfloat16

_MXU = bf16
_COMM = bf16
HIGH = lax.Precision.HIGH

D_MODEL = 1024
DEPTH = 2
D_PLE = 256
D_SSM = 256
D_DN = 512
D_SG = 256
SSM_GROUPS = 16
SSM_GROUP = 16
SSM_STATE = 64
N_STATE = SSM_GROUPS * SSM_STATE
DN_HEADS = 4
DN_HEAD_DIM = 128
DN_CONV = 4
DN_CHUNK = 64
SG_HEADS = 4
SG_HEAD_DIM = 64
SG_CHUNK = 128
S5_CHUNK = 256
S5_GROUP_ROWS = 8
EPS = 1e-6
D_IN = 3336
D_IN_PAD = 3456
LANE = 128

ADAM_LR = 0.001
ADAM_B1 = 0.9
ADAM_B2 = 0.999
ADAM_EPS = 1e-08
ADAM_WD = 0.01
ADAM_STEP = 10

N_CHIPS = 4
N_DEV = 8

Z_COLS = ((0, 512), (512, 2048), (2048, 2560), (2560, 3328), (3328, 3456))

GROUP_COLS = ((0, 512), (512, 2048), (2056, 2568), (2568, 3336), (2048, 2056))
SHARD_COLS = D_IN // 4

SHARDED = (("w_in", ("slot",)), ("ssm_w_glu", ("win", 0, 64)), ("dn_conv_w", ("win", 1, 384)),
           ("w_out", ("win", 0, 256)), ("w_ple_gate", ("win", 0, 256)), ("w_ple", ("win", 1, 256)))
REPLICATED = ("norm_g", "ssm_a_re", "ssm_a_im", "ssm_b_re", "ssm_b_im", "ssm_c_re", "ssm_c_im", "ssm_d",
              "ssm_log_step", "ssm_b_glu", "dn_a_log", "dn_dt_bias", "dn_norm_g", "sg_ln_g", "sg_ln_b", "sg_w",
              "sg_b", "ple_norm_g", "final_norm_g")
WEIGHTS = ("norm_g", "w_in", "ssm_a_re", "ssm_a_im", "ssm_b_re", "ssm_b_im", "ssm_c_re", "ssm_c_im", "ssm_d",
           "ssm_log_step", "ssm_w_glu", "ssm_b_glu", "dn_conv_w", "dn_a_log", "dn_dt_bias", "dn_norm_g", "sg_ln_g",
           "sg_ln_b", "sg_w", "sg_b", "w_out", "ple_norm_g", "w_ple_gate", "w_ple", "final_norm_g")

VMEM_BIG = 56 * 1024 * 1024


def _mm(a, b):
    return jnp.dot(a.astype(_MXU), b.astype(_MXU), preferred_element_type=f32)


def _mm_nt(a, b):
    return lax.dot_general(a.astype(_MXU), b.astype(_MXU), (((1,), (1,)), ((), ())), preferred_element_type=f32)


def _mm_tn(a, b):
    return lax.dot_general(a.astype(_MXU), b.astype(_MXU), (((0,), (0,)), ((), ())), preferred_element_type=f32)


@jax.custom_vjp
def bdot(a, b):
    return _mm(a, b)


def _bdot_fwd(a, b):
    return _mm(a, b), (a, b)


def _bdot_bwd(res, g):
    a, b = res
    return _mm_nt(g, b).astype(a.dtype), _mm_tn(a, g).astype(b.dtype)


bdot.defvjp(_bdot_fwd, _bdot_bwd)


@jax.custom_vjp
def bdot_nt(a, b):
    return _mm_nt(a, b)


def _bdot_nt_fwd(a, b):
    return _mm_nt(a, b), (a, b)


def _bdot_nt_bwd(res, g):
    a, b = res
    return _mm(g, b).astype(a.dtype), _mm_tn(g, a).astype(b.dtype)


bdot_nt.defvjp(_bdot_nt_fwd, _bdot_nt_bwd)


@jax.custom_vjp
def bdot_tn(a, b):
    return _mm_tn(a, b)


def _bdot_tn_fwd(a, b):
    return _mm_tn(a, b), (a, b)


def _bdot_tn_bwd(res, g):
    a, b = res
    return _mm_nt(b, g).astype(a.dtype), _mm(a, g).astype(b.dtype)


bdot_tn.defvjp(_bdot_tn_fwd, _bdot_tn_bwd)


def hdot(a, b):
    return jnp.dot(a, b, precision=HIGH, preferred_element_type=f32)


def _unit_lower_inverses(ms):
    n = ms[0].shape[0]
    eye = (lax.broadcasted_iota(jnp.int32, (n, n), 0) == lax.broadcasted_iota(jnp.int32, (n, n), 1)).astype(f32)
    pw = [-m for m in ms]
    inv = [eye + p for p in pw]
    for _ in range(n.bit_length() - 2):
        pw = [hdot(p, p) for p in pw]
        inv = [a + hdot(a, p) for a, p in zip(inv, pw)]
    return inv


@jax.custom_vjp
def solve_unit_lower(ms, rhs, inv):
    return [hdot(a, r) for a, r in zip(inv, rhs)]


def _solve_unit_lower_fwd(ms, rhs, inv):
    xs = [hdot(a, r) for a, r in zip(inv, rhs)]
    return xs, (inv, xs)


def _solve_unit_lower_bwd(res, gs):
    inv, xs = res
    d_rhs = [lax.dot_general(a, g, (((0,), (0,)), ((), ())), precision=HIGH, preferred_element_type=f32)
             for a, g in zip(inv, gs)]
    d_ms = [-lax.dot_general(d, x, (((1,), (1,)), ((), ())), precision=HIGH, preferred_element_type=f32)
            for d, x in zip(d_rhs, xs)]
    return d_ms, d_rhs, [jnp.zeros_like(a) for a in inv]


solve_unit_lower.defvjp(_solve_unit_lower_fwd, _solve_unit_lower_bwd)


@functools.partial(jax.custom_vjp, nondiff_argnums=(1,))
def roll_rows(x, k):
    return pltpu.roll(x, k, 0)


def _roll_rows_fwd(x, k):
    return pltpu.roll(x, k, 0), None


def _roll_rows_bwd(k, _, g):
    return (pltpu.roll(g, g.shape[0] - k, 0),)


roll_rows.defvjp(_roll_rows_fwd, _roll_rows_bwd)


def _row_ids(shape):
    return lax.broadcasted_iota(jnp.int32, shape, 0)


def _rms(x, g):
    return x * lax.rsqrt(jnp.mean(x * x, axis=-1, keepdims=True) + EPS) * g


def _layer_norm(x, g, b):
    mu = jnp.mean(x, axis=-1, keepdims=True)
    xc = x - mu
    return xc * lax.rsqrt(jnp.mean(xc * xc, axis=-1, keepdims=True) + EPS) * g + b


def _s5_prep(are, aim, ls, bre, bim):
    step = jnp.exp(ls)
    mag = jnp.exp(are * step)
    lr = mag * jnp.cos(aim * step)
    li = mag * jnp.sin(aim * step)
    den = are * are + aim * aim
    nr, ni = lr - 1.0, li
    fr = (nr * are + ni * aim) / den
    fi = (ni * are - nr * aim) / den
    bbr = fr * bre - fi * bim
    bbi = fr * bim + fi * bre
    pr = jnp.broadcast_to(lr, (S5_GROUP_ROWS, N_STATE))
    pi = jnp.broadcast_to(li, (S5_GROUP_ROWS, N_STATE))
    d = 1
    while d < S5_GROUP_ROWS:
        keep = _row_ids(pr.shape) >= d
        sr, si = roll_rows(pr, d), roll_rows(pi, d)
        pr, pi = jnp.where(keep, pr * sr - pi * si, pr), jnp.where(keep, pr * si + pi * sr, pi)
        d *= 2
    return pr, pi, bbr, bbi


def _s5_chunk(u, gate, hr, hi, pr, pi, bbr, bbi, cr, ci, dr, wglu, bglu):
    n, grp = u.shape[0], S5_GROUP_ROWS
    xr = bdot(u, bbr)
    xi = bdot(u, bbi)
    sub = _row_ids(xr.shape) % grp
    d = 1
    while d < grp:
        lr, li = pr[d - 1:d], pi[d - 1:d]
        sr = jnp.where(sub >= d, roll_rows(xr, d), 0.0)
        si = jnp.where(sub >= d, roll_rows(xi, d), 0.0)
        xr, xi = xr + lr * sr - li * si, xi + lr * si + li * sr
        d *= 2
    outs_r, outs_i = [], []
    for g in range(n // grp):
        gr, gi = xr[g * grp:(g + 1) * grp], xi[g * grp:(g + 1) * grp]
        gr, gi = gr + pr * hr - pi * hi, gi + pr * hi + pi * hr
        hr, hi = gr[grp - 1:grp], gi[grp - 1:grp]
        outs_r.append(gr)
        outs_i.append(gi)
    xr = jnp.concatenate(outs_r, axis=0)
    xi = jnp.concatenate(outs_i, axis=0)
    y = bdot(xr, cr) - bdot(xi, ci) + dr * u
    y = jax.nn.gelu(y)
    y = y * jax.nn.sigmoid(bdot(y, wglu) + bglu)
    return y * jax.nn.silu(gate), hr, hi


def _dn_pre(xc, xp, w0, w1, w2, w3, is_start, col):
    xp = jnp.where(is_start, 0.0, xp)
    rows = _row_ids(xc.shape)
    acc = w3 * xc
    for d, w in ((1, w2), (2, w1), (3, w0)):
        acc = acc + w * jnp.where(rows >= d, roll_rows(xc, d), roll_rows(xp, d))
    y = jax.nn.silu(acc)
    nrm = y * lax.rsqrt(jnp.sum(y * y, axis=-1, keepdims=True) + EPS)
    nrm = nrm * jnp.where(col < DN_HEADS, DN_HEAD_DIM ** -0.5, 1.0)
    return jnp.where(col < 2 * DN_HEADS, nrm, y)


def _dn_local(qs, ks, vs, abs_, alog, dtb, invs=None):
    c = DN_CHUNK
    ri = lax.broadcasted_iota(jnp.int32, (c, c), 0)
    ci = lax.broadcasted_iota(jnp.int32, (c, c), 1)
    causal, strict = ri >= ci, ri > ci
    tril = causal.astype(f32)
    gcums = [hdot(tril, -jnp.exp(alog) * jax.nn.softplus(ab + dtb)) for ab in abs_]
    gcum_ts = [g.T for g in gcums]
    sigs = [jax.nn.sigmoid(ab) for ab in abs_]
    chains = [(j, h) for j in range(len(abs_)) for h in range(DN_HEADS)]
    gc = [gcums[j][:, h:h + 1] for j, h in chains]
    decay = [jnp.where(causal, jnp.exp(jnp.where(causal, gc[n] - gcum_ts[j][h:h + 1, :], 0.0)), 0.0)
             for n, (j, h) in enumerate(chains)]
    beta = [sigs[j][:, DN_HEADS + h:DN_HEADS + h + 1] for j, h in chains]
    kb = [ks[j][h] * beta[n] for n, (j, h) in enumerate(chains)]
    ms = [jnp.where(strict, bdot_nt(kb[n], ks[j][h]) * decay[n], 0.0) for n, (j, h) in enumerate(chains)]
    egc = [jnp.exp(g) for g in gc]
    rhs = [jnp.concatenate([vs[j][h] * beta[n], kb[n] * egc[n]], axis=1) for n, (j, h) in enumerate(chains)]
    inv = _unit_lower_inverses(ms) if invs is None else [invs[j][h] for j, h in chains]
    sol = solve_unit_lower(ms, rhs, inv)
    values = [s[:, :DN_HEAD_DIM] for s in sol]
    k_cds = [s[:, DN_HEAD_DIM:] for s in sol]
    attns = [bdot_nt(qs[j][h], ks[j][h]) * decay[n] for n, (j, h) in enumerate(chains)]
    q_decs = [qs[j][h] * egc[n] for n, (j, h) in enumerate(chains)]
    k_decs = [ks[j][h] * jnp.exp(gc[n][c - 1:c, :] - gc[n]) for n, (j, h) in enumerate(chains)]

    def nest(flat):
        return [flat[j * DN_HEADS:(j + 1) * DN_HEADS] for j in range(len(abs_))]

    lasts = [jnp.exp(g[c - 1:c, :]) for g in gcums]
    return nest(values), nest(k_cds), nest(attns), nest(q_decs), nest(k_decs), lasts, nest(inv)


def _dn_step(values, k_cds, attns, q_decs, k_decs, lasts, ggs, sts, ng):
    v_new = [v - bdot(kc, st) for v, kc, st in zip(values, k_cds, sts)]
    o = [bdot(qd, st) for qd, st in zip(q_decs, sts)]
    o = [a + bdot(at, vn) for a, at, vn in zip(o, attns, v_new)]
    new = [st * la + bdot_tn(kd, vn) for st, la, kd, vn in zip(sts, lasts, k_decs, v_new)]
    return [_rms(a, ng) * jax.nn.silu(g) for a, g in zip(o, ggs)], new


def _sg_chunk(u, v, gate, lng, lnb, ws, bt):
    n = SG_CHUNK
    ug = jax.nn.gelu(u)
    vn = _layer_norm(jax.nn.gelu(v), lng, lnb)
    causal = lax.broadcasted_iota(jnp.int32, (n, n), 0) >= lax.broadcasted_iota(jnp.int32, (n, n), 1)
    lane = lax.broadcasted_iota(jnp.int32, (n, D_SG), 1)
    s = jnp.zeros((n, D_SG), f32)
    for h in range(SG_HEADS):
        t = bdot(jnp.where(causal, ws[h], 0.0), vn) + bt[:, h:h + 1]
        s = s + jnp.where((lane >= h * SG_HEAD_DIM) & (lane < (h + 1) * SG_HEAD_DIM), t, 0.0)
    return ug * s * jax.nn.silu(gate)


def _cp(n_grid, vmem=None):
    return pltpu.CompilerParams(dimension_semantics=("arbitrary",) * n_grid, vmem_limit_bytes=vmem)


def _full(shape):
    nd = len(shape)
    return pl.BlockSpec(tuple(shape), lambda *_: (0,) * nd)


def _rows(tm, ncol):
    return pl.BlockSpec((tm, ncol), lambda i: (i, 0))


def _sds(shape, dtype=f32):
    return jax.ShapeDtypeStruct(tuple(shape), dtype)


def _acc(ref, val, first):
    @pl.when(first)
    def _():
        ref[...] = val

    @pl.when(jnp.logical_not(first))
    def _():
        ref[...] += val


def in_fwd(x, g, w, name):
    t, tm = x.shape[0], 256

    def body(x_ref, g_ref, w_ref, h_ref, *z_refs):
        h = _rms(x_ref[...], g_ref[...]).astype(_MXU)
        h_ref[...] = h
        for z_ref, (a, b) in zip(z_refs, Z_COLS):
            z_ref[...] = jnp.dot(h, w_ref[:, a:b], preferred_element_type=f32)

    widths = [b - a for a, b in Z_COLS]
    return pl.pallas_call(
        body, name=name, grid=(t // tm,),
        in_specs=[_rows(tm, D_MODEL), _full((1, D_MODEL)), _full((D_MODEL, D_IN_PAD))],
        out_specs=[_rows(tm, D_MODEL)] + [_rows(tm, n) for n in widths],
        out_shape=[_sds((t, D_MODEL), _MXU)] + [_sds((t, n)) for n in widths],
        compiler_params=_cp(1, VMEM_BIG),
    )(x, g, w)


def in_bwd(x, g, w, dzs, dres, name):
    t, tm = x.shape[0], 256
    widths = [b - a for a, b in Z_COLS]

    def body(x_ref, g_ref, w_ref, dres_ref, *rest):
        dz_refs, (dx_ref, dg_ref) = rest[:5], rest[5:]
        dh = jnp.zeros((tm, D_MODEL), f32)
        for dz_ref, (a, b) in zip(dz_refs, Z_COLS):
            dh = dh + _mm_nt(dz_ref[...], w_ref[:, a:b])
        _, vj = jax.vjp(_rms, x_ref[...], g_ref[...])
        dx, dg = vj(dh)
        dx_ref[...] = dres_ref[...] + dx
        _acc(dg_ref, dg, pl.program_id(0) == 0)

    return pl.pallas_call(
        body, name=name, grid=(t // tm,),
        in_specs=[_rows(tm, D_MODEL), _full((1, D_MODEL)), _full((D_MODEL, D_IN_PAD)), _rows(tm, D_MODEL)]
        + [_rows(tm, n) for n in widths],
        out_specs=[_rows(tm, D_MODEL), _full((1, D_MODEL))],
        out_shape=[_sds((t, D_MODEL)), _sds((1, D_MODEL))],
        compiler_params=_cp(1, VMEM_BIG),
    )(x, g, w, dres, *dzs)


def wgrad(a, g, name):
    t, k = a.shape
    n = g.shape[1]
    tm = min(t, 2048)
    tn = n if n <= 768 else (768 if n % 768 == 0 else 512)
    steps = t // tm

    def body(a_ref, g_ref, o_ref, acc):
        i = pl.program_id(1)
        _acc(acc, _mm_tn(a_ref[...], g_ref[...]), i == 0)

        @pl.when(i == steps - 1)
        def _():
            o_ref[...] = acc[...].astype(o_ref.dtype)

    return pl.pallas_call(
        body, name=name, grid=(n // tn, steps),
        in_specs=[pl.BlockSpec((tm, k), lambda j, i: (i, 0)), pl.BlockSpec((tm, tn), lambda j, i: (i, j))],
        out_specs=pl.BlockSpec((k, tn), lambda j, i: (0, j)),
        out_shape=_sds((k, n), _COMM),
        scratch_shapes=[pltpu.VMEM((k, tn), f32)],
        compiler_params=_cp(2, VMEM_BIG),
    )(a, g)


def post_fwd(x, ys, yd, yg, p, wout, pg, wgate, wple, name):
    t, tm = x.shape[0], 256

    def body(x_ref, ys_ref, yd_ref, yg_ref, p_ref, wout_ref, pg_ref, wgate_ref, wple_ref,
             x2_ref, x1_ref, y_ref, hn_ref):
        y = jnp.concatenate([ys_ref[...], yd_ref[...], yg_ref[...]], axis=1).astype(_MXU)
        y_ref[...] = y
        x1 = x_ref[...] + jnp.dot(y, wout_ref[...], preferred_element_type=f32)
        x1_ref[...] = x1
        hn = _rms(x1, pg_ref[...]).astype(_MXU)
        hn_ref[...] = hn
        gp = jnp.dot(hn, wgate_ref[...], preferred_element_type=f32)
        pp = _mm(p_ref[...], wple_ref[...])
        x2_ref[...] = x1 + jax.nn.sigmoid(gp) * pp

    return pl.pallas_call(
        body, name=name, grid=(t // tm,),
        in_specs=[_rows(tm, D_MODEL), _rows(tm, D_SSM), _rows(tm, D_DN), _rows(tm, D_SG), _rows(tm, D_PLE),
                  _full((D_MODEL, D_MODEL)), _full((1, D_MODEL)), _full((D_MODEL, D_MODEL)), _full((D_PLE, D_MODEL))],
        out_specs=[_rows(tm, D_MODEL)] * 4,
        out_shape=[_sds((t, D_MODEL)), _sds((t, D_MODEL)), _sds((t, D_MODEL), _MXU), _sds((t, D_MODEL), _MXU)],
        compiler_params=_cp(1, VMEM_BIG),
    )(x, ys, yd, yg, p, wout, pg, wgate, wple)


def post_bwd(dx2, x1, hn, p, wout, pg, wgate, wple, name):
    t, tm = dx2.shape[0], 256

    def body(dx2_ref, x1_ref, hn_ref, p_ref, wout_ref, pg_ref, wgate_ref, wple_ref,
             dx1_ref, dgp_ref, dpp_ref, dys_ref, dyd_ref, dyg_ref, dpg_ref):
        dx2 = dx2_ref[...]
        gp = jnp.dot(hn_ref[...], wgate_ref[...], preferred_element_type=f32)
        pp = _mm(p_ref[...], wple_ref[...])
        sg = jax.nn.sigmoid(gp)
        dpp_ref[...] = (dx2 * sg).astype(_MXU)
        dgp = (dx2 * pp * sg * (1.0 - sg)).astype(_MXU)
        dgp_ref[...] = dgp
        dhn = _mm_nt(dgp, wgate_ref[...])
        _, vj = jax.vjp(_rms, x1_ref[...], pg_ref[...])
        dx1n, dpg = vj(dhn)
        dx1 = dx2 + dx1n
        dx1_ref[...] = dx1
        dy = _mm_nt(dx1, wout_ref[...])
        dys_ref[...] = dy[:, :D_SSM]
        dyd_ref[...] = dy[:, D_SSM:D_SSM + D_DN]
        dyg_ref[...] = dy[:, D_SSM + D_DN:]
        _acc(dpg_ref, dpg, pl.program_id(0) == 0)

    return pl.pallas_call(
        body, name=name, grid=(t // tm,),
        in_specs=[_rows(tm, D_MODEL), _rows(tm, D_MODEL), _rows(tm, D_MODEL), _rows(tm, D_PLE),
                  _full((D_MODEL, D_MODEL)), _full((1, D_MODEL)), _full((D_MODEL, D_MODEL)), _full((D_PLE, D_MODEL))],
        out_specs=[_rows(tm, D_MODEL), _rows(tm, D_MODEL), _rows(tm, D_MODEL), _rows(tm, D_SSM), _rows(tm, D_DN),
                   _rows(tm, D_SG), _full((1, D_MODEL))],
        out_shape=[_sds((t, D_MODEL)), _sds((t, D_MODEL), _MXU), _sds((t, D_MODEL), _MXU), _sds((t, D_SSM)),
                   _sds((t, D_DN)), _sds((t, D_SG)), _sds((1, D_MODEL))],
        compiler_params=_cp(1, VMEM_BIG),
    )(dx2, x1, hn, p, wout, pg, wgate, wple)


def loss_fwd_bwd(x, fg, target, name):
    t, tm = x.shape[0], 512

    def body(x_ref, fg_ref, t_ref, loss_ref, dx_ref, dfg_ref):
        def f(xv, gv):
            err = _rms(xv, gv) - t_ref[...]
            return 0.5 * jnp.sum(jnp.mean(err * err, axis=-1))

        val, vj = jax.vjp(f, x_ref[...], fg_ref[...])
        dx, dfg = vj(jnp.ones((), f32))
        dx_ref[...] = dx
        first = pl.program_id(0) == 0
        _acc(dfg_ref, dfg, first)
        _acc(loss_ref, jnp.full((8, LANE), val, f32), first)

    return pl.pallas_call(
        body, name=name, grid=(t // tm,),
        in_specs=[_rows(tm, D_MODEL), _full((1, D_MODEL)), _rows(tm, D_MODEL)],
        out_specs=[_full((8, LANE)), _rows(tm, D_MODEL), _full((1, D_MODEL))],
        out_shape=[_sds((8, LANE)), _sds((t, D_MODEL)), _sds((1, D_MODEL))],
        compiler_params=_cp(1),
    )(x, fg, target)


S5_PREPARED = 4
_S5_PARAM_SHAPES = ((S5_GROUP_ROWS, N_STATE), (S5_GROUP_ROWS, N_STATE), (D_SSM, N_STATE), (D_SSM, N_STATE),
                    (N_STATE, D_SSM), (N_STATE, D_SSM), (1, D_SSM), (D_SSM, D_SSM), (1, D_SSM))

def s5_prep_fwd(are, aim, ls, bre, bim, name):
    def body(are_ref, aim_ref, ls_ref, bre_ref, bim_ref, *outs):
        vals = _s5_prep(are_ref[...], aim_ref[...], ls_ref[...], bre_ref[...], bim_ref[...])
        for o, v in zip(outs, vals):
            o[...] = v

    return pl.pallas_call(body, name=name, out_shape=[_sds(s) for s in _S5_PARAM_SHAPES[:S5_PREPARED]])(
        are, aim, ls, bre, bim)


def s5_prep_bwd(are, aim, ls, bre, bim, cts, name):
    def body(are_ref, aim_ref, ls_ref, bre_ref, bim_ref, *rest):
        ct_refs, outs = rest[:S5_PREPARED], rest[S5_PREPARED:]
        _, vj = jax.vjp(_s5_prep, are_ref[...], aim_ref[...], ls_ref[...], bre_ref[...], bim_ref[...])
        for o, v in zip(outs, vj(tuple(r[...] for r in ct_refs))):
            o[...] = v

    shapes = [(1, N_STATE)] * 3 + [(D_SSM, N_STATE)] * 2
    return pl.pallas_call(body, name=name, out_shape=[_sds(s) for s in shapes])(are, aim, ls, bre, bim, *cts)


def s5_fwd(z, params, nb, name):
    t = z.shape[0]
    nc = t // nb // S5_CHUNK
    npar = len(_S5_PARAM_SHAPES)

    def body(z_ref, *rest):
        p_refs, (y_ref, hs_ref, hr_s, hi_s) = rest[:npar], rest[npar:]

        @pl.when(pl.program_id(1) == 0)
        def _():
            hr_s[...] = jnp.zeros_like(hr_s)
            hi_s[...] = jnp.zeros_like(hi_s)

        hr, hi = hr_s[...], hi_s[...]
        hs_ref[0, :, :N_STATE] = hr
        hs_ref[0, :, N_STATE:] = hi
        y, nhr, nhi = _s5_chunk(z_ref[:, :D_SSM], z_ref[:, D_SSM:], hr, hi, *[r[...] for r in p_refs])
        y_ref[...] = y
        hr_s[...] = nhr
        hi_s[...] = nhi

    return pl.pallas_call(
        body, name=name, grid=(nb, nc),
        in_specs=[pl.BlockSpec((S5_CHUNK, 2 * D_SSM), lambda b, c: (b * nc + c, 0))]
        + [_full(s) for s in _S5_PARAM_SHAPES],
        out_specs=[pl.BlockSpec((S5_CHUNK, D_SSM), lambda b, c: (b * nc + c, 0)),
                   pl.BlockSpec((1, 1, 2 * N_STATE), lambda b, c: (b * nc + c, 0, 0))],
        out_shape=[_sds((t, D_SSM)), _sds((nb * nc, 1, 2 * N_STATE))],
        scratch_shapes=[pltpu.VMEM((1, N_STATE), f32), pltpu.VMEM((1, N_STATE), f32)],
        compiler_params=_cp(2, VMEM_BIG),
    )(z, *params)


def s5_bwd(z, params, hs, dy, nb, name):
    t = z.shape[0]
    nc = t // nb // S5_CHUNK
    npar = len(_S5_PARAM_SHAPES)

    def body(z_ref, hs_ref, dy_ref, *rest):
        p_refs, dz_ref, dp_refs, (dhr_s, dhi_s) = rest[:npar], rest[npar], rest[npar + 1:2 * npar + 1], rest[2 * npar + 1:]

        @pl.when(pl.program_id(1) == 0)
        def _():
            dhr_s[...] = jnp.zeros_like(dhr_s)
            dhi_s[...] = jnp.zeros_like(dhi_s)

        prim = (z_ref[:, :D_SSM], z_ref[:, D_SSM:], hs_ref[0, :, :N_STATE], hs_ref[0, :, N_STATE:]) + tuple(
            r[...] for r in p_refs)
        _, vj = jax.vjp(_s5_chunk, *prim)
        cts = vj((dy_ref[...], dhr_s[...], dhi_s[...]))
        dz_ref[:, :D_SSM] = cts[0].astype(_MXU)
        dz_ref[:, D_SSM:] = cts[1].astype(_MXU)
        dhr_s[...] = cts[2]
        dhi_s[...] = cts[3]
        first = (pl.program_id(0) == 0) & (pl.program_id(1) == 0)
        for r, v in zip(dp_refs, cts[4:]):
            _acc(r, v, first)

    rev = lambda b, c: (b * nc + nc - 1 - c, 0)
    return pl.pallas_call(
        body, name=name, grid=(nb, nc),
        in_specs=[pl.BlockSpec((S5_CHUNK, 2 * D_SSM), rev),
                  pl.BlockSpec((1, 1, 2 * N_STATE), lambda b, c: (b * nc + nc - 1 - c, 0, 0)),
                  pl.BlockSpec((S5_CHUNK, D_SSM), rev)] + [_full(s) for s in _S5_PARAM_SHAPES],
        out_specs=[pl.BlockSpec((S5_CHUNK, 2 * D_SSM), rev)] + [_full(s) for s in _S5_PARAM_SHAPES],
        out_shape=[_sds((t, 2 * D_SSM), _MXU)] + [_sds(s) for s in _S5_PARAM_SHAPES],
        scratch_shapes=[pltpu.VMEM((1, N_STATE), f32), pltpu.VMEM((1, N_STATE), f32)],
        compiler_params=_cp(2, VMEM_BIG),
    )(z, hs, dy, *params)


DN_PRE_ROWS = 256
DN_COLS = 3 * D_DN // LANE


def dn_pre_fwd(zq, convw, seq, name):
    t, tb = zq.shape[0], DN_PRE_ROWS
    per_seq = seq // tb

    def body(xc_ref, xp_ref, w_ref, o_ref):
        is_start = pl.program_id(0) % per_seq == 0
        for j in range(DN_COLS):
            cols = slice(j * LANE, (j + 1) * LANE)
            o_ref[:, cols] = _dn_pre(xc_ref[:, cols], xp_ref[:, cols], w_ref[0:1, cols], w_ref[1:2, cols],
                                     w_ref[2:3, cols], w_ref[3:4, cols], is_start, j)

    return pl.pallas_call(
        body, name=name, grid=(t // tb,),
        in_specs=[_rows(tb, 3 * D_DN), pl.BlockSpec((tb, 3 * D_DN), lambda i: (jnp.maximum(i - 1, 0), 0)),
                  _full((DN_CONV, 3 * D_DN))],
        out_specs=_rows(tb, 3 * D_DN),
        out_shape=_sds((t, 3 * D_DN)),
        compiler_params=_cp(1, VMEM_BIG),
    )(zq, zq, convw)


def dn_pre_bwd(zq, convw, dqkv, seq, name):
    t, tb = zq.shape[0], DN_PRE_ROWS
    nrow = t // tb
    per_seq = seq // tb

    def body(xc_ref, xp_ref, w_ref, d_ref, dx_ref, dw_ref, carry):
        step = pl.program_id(0)
        i = nrow - 1 - step

        @pl.when(step == 0)
        def _():
            carry[...] = jnp.zeros_like(carry)

        for j in range(DN_COLS):
            cols = slice(j * LANE, (j + 1) * LANE)
            fn = functools.partial(_dn_pre, is_start=i % per_seq == 0, col=j)
            _, vj = jax.vjp(fn, xc_ref[:, cols], xp_ref[:, cols], w_ref[0:1, cols], w_ref[1:2, cols],
                            w_ref[2:3, cols], w_ref[3:4, cols])
            dxc, dxp, dw0, dw1, dw2, dw3 = vj(d_ref[:, cols])
            dx_ref[:, cols] = (dxc + carry[:, cols]).astype(_MXU)
            carry[:, cols] = dxp
            for k, dw in enumerate((dw0, dw1, dw2, dw3)):
                @pl.when(step == 0)
                def _():
                    dw_ref[k:k + 1, cols] = dw

                @pl.when(step != 0)
                def _():
                    dw_ref[k:k + 1, cols] += dw

    rev = lambda s: (nrow - 1 - s, 0)
    return pl.pallas_call(
        body, name=name, grid=(nrow,),
        in_specs=[pl.BlockSpec((tb, 3 * D_DN), rev),
                  pl.BlockSpec((tb, 3 * D_DN), lambda s: (jnp.maximum(nrow - 2 - s, 0), 0)),
                  _full((DN_CONV, 3 * D_DN)), pl.BlockSpec((tb, 3 * D_DN), rev)],
        out_specs=[pl.BlockSpec((tb, 3 * D_DN), rev), _full((DN_CONV, 3 * D_DN))],
        out_shape=[_sds((t, 3 * D_DN), _MXU), _sds((DN_CONV, 3 * D_DN))],
        scratch_shapes=[pltpu.VMEM((tb, 3 * D_DN), f32)],
        compiler_params=_cp(1, VMEM_BIG),
    )(zq, zq, convw, dqkv)


DN_LOCAL_CHUNKS = 2
DN_ATTN = DN_HEADS * DN_CHUNK


def _dn_heads(ref, rows, base=0):
    return [ref[rows, base + h * DN_HEAD_DIM:base + (h + 1) * DN_HEAD_DIM] for h in range(DN_HEADS)]


def dn_local_fwd(qkv, ab, alog, dtb, name):
    t = qkv.shape[0]
    c, n = DN_CHUNK, DN_LOCAL_CHUNKS

    def body(qkv_ref, ab_ref, alog_ref, dtb_ref, val_ref, kcd_ref, attn_ref, qd_ref, kd_ref, el_ref, inv_ref):
        rows = [pl.ds(j * c, c) for j in range(n)]
        vals, kcds, attns, qds, kds, els, invs = _dn_local(
            [_dn_heads(qkv_ref, r) for r in rows], [_dn_heads(qkv_ref, r, D_DN) for r in rows],
            [_dn_heads(qkv_ref, r, 2 * D_DN) for r in rows], [ab_ref[r, :] for r in rows], alog_ref[...], dtb_ref[...])
        for j, r in enumerate(rows):
            for h in range(DN_HEADS):
                lo, hi = h * DN_HEAD_DIM, (h + 1) * DN_HEAD_DIM
                val_ref[r, lo:hi] = vals[j][h]
                kcd_ref[r, lo:hi] = kcds[j][h].astype(_MXU)
                qd_ref[r, lo:hi] = qds[j][h].astype(_MXU)
                kd_ref[r, lo:hi] = kds[j][h].astype(_MXU)
                attn_ref[r, h * c:(h + 1) * c] = attns[j][h].astype(_MXU)
                inv_ref[r, h * c:(h + 1) * c] = invs[j][h]
            el_ref[j] = els[j]

    wide = _rows(n * c, D_DN)
    outs = pl.pallas_call(
        body, name=name, grid=(t // (n * c),),
        in_specs=[_rows(n * c, 3 * D_DN), _rows(n * c, LANE), _full((1, LANE)), _full((1, LANE))],
        out_specs=[wide, wide, _rows(n * c, DN_ATTN), wide, wide, pl.BlockSpec((n, 1, LANE), lambda i: (i, 0, 0)),
                   _rows(n * c, DN_ATTN)],
        out_shape=[_sds((t, D_DN)), _sds((t, D_DN), _MXU), _sds((t, DN_ATTN), _MXU), _sds((t, D_DN), _MXU),
                   _sds((t, D_DN), _MXU), _sds((t // c, 1, LANE)), _sds((t, DN_ATTN))],
        compiler_params=_cp(1),
    )(qkv, ab, alog, dtb)
    return outs[:6], outs[6]


def dn_local_bwd(qkv, ab, alog, dtb, inv, cts, name):
    t = qkv.shape[0]
    c, n = DN_CHUNK, DN_LOCAL_CHUNKS

    def body(qkv_ref, ab_ref, alog_ref, dtb_ref, inv_ref, dval_ref, dkcd_ref, dattn_ref, dqd_ref, dkd_ref, del_ref,
             dqkv_ref, dab_ref, dalog_ref, ddtb_ref):
        rows = [pl.ds(j * c, c) for j in range(n)]
        invs = [[inv_ref[r, h * c:(h + 1) * c] for h in range(DN_HEADS)] for r in rows]

        def local(qs, ks, vs, abs_, alog, dtb):
            return _dn_local(qs, ks, vs, abs_, alog, dtb, invs)[:6]

        _, vj = jax.vjp(local, [_dn_heads(qkv_ref, r) for r in rows], [_dn_heads(qkv_ref, r, D_DN) for r in rows],
                        [_dn_heads(qkv_ref, r, 2 * D_DN) for r in rows], [ab_ref[r, :] for r in rows], alog_ref[...],
                        dtb_ref[...])
        dattn = [[dattn_ref[r, h * c:(h + 1) * c] for h in range(DN_HEADS)] for r in rows]
        dq, dk, dv, dab, dalog, ddtb = vj(([_dn_heads(dval_ref, r) for r in rows], [_dn_heads(dkcd_ref, r) for r in rows],
                                           dattn, [_dn_heads(dqd_ref, r) for r in rows],
                                           [_dn_heads(dkd_ref, r) for r in rows], [del_ref[j] for j in range(n)]))
        for j, r in enumerate(rows):
            for h in range(DN_HEADS):
                lo, hi = h * DN_HEAD_DIM, (h + 1) * DN_HEAD_DIM
                dqkv_ref[r, lo:hi] = dq[j][h]
                dqkv_ref[r, D_DN + lo:D_DN + hi] = dk[j][h]
                dqkv_ref[r, 2 * D_DN + lo:2 * D_DN + hi] = dv[j][h]
            dab_ref[r, :] = dab[j].astype(_MXU)
        first = pl.program_id(0) == 0
        _acc(dalog_ref, dalog, first)
        _acc(ddtb_ref, ddtb, first)

    wide = _rows(n * c, D_DN)
    return pl.pallas_call(
        body, name=name, grid=(t // (n * c),),
        in_specs=[_rows(n * c, 3 * D_DN), _rows(n * c, LANE), _full((1, LANE)), _full((1, LANE)),
                  _rows(n * c, DN_ATTN), wide, wide, _rows(n * c, DN_ATTN), wide, wide,
                  pl.BlockSpec((n, 1, LANE), lambda i: (i, 0, 0))],
        out_specs=[_rows(n * c, 3 * D_DN), _rows(n * c, LANE), _full((1, LANE)), _full((1, LANE))],
        out_shape=[_sds((t, 3 * D_DN)), _sds((t, LANE), _MXU), _sds((1, LANE)), _sds((1, LANE))],
        compiler_params=_cp(1),
    )(qkv, ab, alog, dtb, inv, *cts)


def _seq_view(a, nb):
    return a.reshape((nb, a.shape[0] // nb) + a.shape[1:])


def _dn_chains(nb):
    return [(b, h) for b in range(nb) for h in range(DN_HEADS)]


DN_SCAN_CHUNKS = 4


def _dn_step_operands(val_ref, kcd_ref, attn_ref, qd_ref, kd_ref, el_ref, gg_ref, nb, j):
    chains = _dn_chains(nb)
    c = DN_CHUNK
    rows = pl.ds(j * c, c)

    def wide(ref):
        return [ref[b, rows, h * DN_HEAD_DIM:(h + 1) * DN_HEAD_DIM].astype(f32) for b, h in chains]

    attns = [attn_ref[b, rows, h * c:(h + 1) * c].astype(f32) for b, h in chains]
    return (wide(val_ref), wide(kcd_ref), attns, wide(qd_ref), wide(kd_ref),
            [el_ref[b, j, :, h:h + 1] for b, h in chains], wide(gg_ref))


def dn_scan_fwd(loc, gg, ng, nb, name):
    val, kcd, attn, qd, kd, el = loc
    t = val.shape[0]
    c, n = DN_CHUNK, DN_SCAN_CHUNKS
    nc = t // nb // c
    ns = nb * DN_HEADS

    def body(val_ref, kcd_ref, attn_ref, qd_ref, kd_ref, el_ref, gg_ref, ng_ref, y_ref, ss_ref, st):
        @pl.when(pl.program_id(0) == 0)
        def _():
            st[...] = jnp.zeros_like(st)

        sts = [st[i] for i in range(ns)]
        for j in range(n):
            for i in range(ns):
                ss_ref[j, i] = sts[i]
            ys, sts = _dn_step(*_dn_step_operands(val_ref, kcd_ref, attn_ref, qd_ref, kd_ref, el_ref, gg_ref, nb, j),
                               sts, ng_ref[...])
            for i, (b, h) in enumerate(_dn_chains(nb)):
                y_ref[b, pl.ds(j * c, c), h * DN_HEAD_DIM:(h + 1) * DN_HEAD_DIM] = ys[i]
        for i in range(ns):
            st[i] = sts[i]

    def blk(w):
        return pl.BlockSpec((nb, n * c, w), lambda k: (0, k, 0))

    el_spec = pl.BlockSpec((nb, n, 1, LANE), lambda k: (0, k, 0, 0))
    y, ss = pl.pallas_call(
        body, name=name, grid=(nc // n,),
        in_specs=[blk(D_DN), blk(D_DN), blk(DN_ATTN), blk(D_DN), blk(D_DN), el_spec, blk(D_DN), _full((1, LANE))],
        out_specs=[blk(D_DN), pl.BlockSpec((n, ns, DN_HEAD_DIM, DN_HEAD_DIM), lambda k: (k, 0, 0, 0))],
        out_shape=[_sds((nb, t // nb, D_DN)), _sds((nc, ns, DN_HEAD_DIM, DN_HEAD_DIM))],
        scratch_shapes=[pltpu.VMEM((ns, DN_HEAD_DIM, DN_HEAD_DIM), f32)],
        compiler_params=_cp(1, VMEM_BIG),
    )(_seq_view(val, nb), _seq_view(kcd, nb), _seq_view(attn, nb), _seq_view(qd, nb), _seq_view(kd, nb),
      el.reshape(nb, nc, 1, LANE), _seq_view(gg, nb), ng)
    return y.reshape(t, D_DN), ss


def dn_scan_bwd(loc, gg, ng, ss, dy, nb, name):
    val, kcd, attn, qd, kd, el = loc
    t = val.shape[0]
    c, n = DN_CHUNK, DN_SCAN_CHUNKS
    nc = t // nb // c
    ns = nb * DN_HEADS
    steps = nc // n

    def body(val_ref, kcd_ref, attn_ref, qd_ref, kd_ref, el_ref, gg_ref, ng_ref, ss_ref, dy_ref,
             dval_ref, dkcd_ref, dattn_ref, dqd_ref, dkd_ref, del_ref, dgg_ref, dng_ref, dst):
        @pl.when(pl.program_id(0) == 0)
        def _():
            dst[...] = jnp.zeros_like(dst)

        lane = lax.broadcasted_iota(jnp.int32, (1, LANE), 1)
        chains = _dn_chains(nb)
        ds = [dst[i] for i in range(ns)]
        dng_tot = jnp.zeros((1, LANE), f32)
        for j in reversed(range(n)):
            rows = pl.ds(j * c, c)
            _, vj = jax.vjp(_dn_step,
                            *_dn_step_operands(val_ref, kcd_ref, attn_ref, qd_ref, kd_ref, el_ref, gg_ref, nb, j),
                            [ss_ref[j, i] for i in range(ns)], ng_ref[...])
            dys = [dy_ref[b, rows, h * DN_HEAD_DIM:(h + 1) * DN_HEAD_DIM] for b, h in chains]
            dval, dkcd, dattn, dqd, dkd, dlast, dgg, ds, dng = vj((dys, ds))
            dng_tot = dng_tot + dng
            del_rows = [jnp.zeros((1, LANE), f32) for _ in range(nb)]
            for i, (b, h) in enumerate(chains):
                cols = slice(h * DN_HEAD_DIM, (h + 1) * DN_HEAD_DIM)
                dval_ref[b, rows, cols] = dval[i]
                dkcd_ref[b, rows, cols] = dkcd[i]
                dattn_ref[b, rows, h * c:(h + 1) * c] = dattn[i]
                dqd_ref[b, rows, cols] = dqd[i]
                dkd_ref[b, rows, cols] = dkd[i]
                dgg_ref[b, rows, cols] = dgg[i].astype(_MXU)
                del_rows[b] = del_rows[b] + jnp.where(lane == h, dlast[i], 0.0)
            for b in range(nb):
                del_ref[b, j] = del_rows[b]
        for i in range(ns):
            dst[i] = ds[i]
        _acc(dng_ref, dng_tot, pl.program_id(0) == 0)

    def blk(w):
        return pl.BlockSpec((nb, n * c, w), lambda k: (0, steps - 1 - k, 0))

    el_spec = pl.BlockSpec((nb, n, 1, LANE), lambda k: (0, steps - 1 - k, 0, 0))
    outs = pl.pallas_call(
        body, name=name, grid=(steps,),
        in_specs=[blk(D_DN), blk(D_DN), blk(DN_ATTN), blk(D_DN), blk(D_DN), el_spec, blk(D_DN), _full((1, LANE)),
                  pl.BlockSpec((n, ns, DN_HEAD_DIM, DN_HEAD_DIM), lambda k: (steps - 1 - k, 0, 0, 0)), blk(D_DN)],
        out_specs=[blk(D_DN), blk(D_DN), blk(DN_ATTN), blk(D_DN), blk(D_DN), el_spec, blk(D_DN), _full((1, LANE))],
        out_shape=[_sds((nb, t // nb, D_DN)), _sds((nb, t // nb, D_DN)), _sds((nb, t // nb, DN_ATTN)),
                   _sds((nb, t // nb, D_DN)), _sds((nb, t // nb, D_DN)), _sds((nb, nc, 1, LANE)),
                   _sds((nb, t // nb, D_DN), _MXU), _sds((1, LANE))],
        scratch_shapes=[pltpu.VMEM((ns, DN_HEAD_DIM, DN_HEAD_DIM), f32)],
        compiler_params=_cp(1, VMEM_BIG),
    )(_seq_view(val, nb), _seq_view(kcd, nb), _seq_view(attn, nb), _seq_view(qd, nb), _seq_view(kd, nb),
      el.reshape(nb, nc, 1, LANE), _seq_view(gg, nb), ng, ss, _seq_view(dy, nb))
    dloc = [o.reshape((t,) + o.shape[2:]) for o in outs[:5]] + [outs[5].reshape(t // c, 1, LANE)]
    return dloc, outs[6].reshape(t, D_DN), outs[7]


SG_ROWS = 512


def sg_fwd(z, lng, lnb, w, bt, name):
    t = z.shape[0]

    def body(z_ref, lng_ref, lnb_ref, w_ref, bt_ref, y_ref):
        ws = [w_ref[h] for h in range(SG_HEADS)]
        for k in range(SG_ROWS // SG_CHUNK):
            r = pl.ds(k * SG_CHUNK, SG_CHUNK)
            y_ref[r, :] = _sg_chunk(z_ref[r, :D_SG], z_ref[r, D_SG:2 * D_SG], z_ref[r, 2 * D_SG:], lng_ref[...],
                                    lnb_ref[...], ws, bt_ref[...])

    return pl.pallas_call(
        body, name=name, grid=(t // SG_ROWS,),
        in_specs=[_rows(SG_ROWS, 3 * D_SG), _full((1, D_SG)), _full((1, D_SG)),
                  _full((SG_HEADS, SG_CHUNK, SG_CHUNK)), _full((SG_CHUNK, LANE))],
        out_specs=_rows(SG_ROWS, D_SG),
        out_shape=_sds((t, D_SG)),
        compiler_params=_cp(1),
    )(z, lng, lnb, w, bt)


def sg_bwd(z, lng, lnb, w, bt, dy, name):
    t = z.shape[0]

    def body(z_ref, lng_ref, lnb_ref, w_ref, bt_ref, dy_ref, dz_ref, dlng_ref, dlnb_ref, dw_ref, dbt_ref):
        ws = [w_ref[h] for h in range(SG_HEADS)]
        tot = None
        for k in range(SG_ROWS // SG_CHUNK):
            r = pl.ds(k * SG_CHUNK, SG_CHUNK)
            _, vj = jax.vjp(_sg_chunk, z_ref[r, :D_SG], z_ref[r, D_SG:2 * D_SG], z_ref[r, 2 * D_SG:], lng_ref[...],
                            lnb_ref[...], ws, bt_ref[...])
            du, dv, dgate, dlng, dlnb, dws, dbt = vj(dy_ref[r, :])
            dz_ref[r, :D_SG] = du.astype(_MXU)
            dz_ref[r, D_SG:2 * D_SG] = dv.astype(_MXU)
            dz_ref[r, 2 * D_SG:] = dgate.astype(_MXU)
            part = [dlng, dlnb, dbt] + list(dws)
            tot = part if tot is None else [a + b for a, b in zip(tot, part)]
        first = pl.program_id(0) == 0
        _acc(dlng_ref, tot[0], first)
        _acc(dlnb_ref, tot[1], first)
        _acc(dbt_ref, tot[2], first)
        for h in range(SG_HEADS):
            @pl.when(first)
            def _():
                dw_ref[h] = tot[3 + h]

            @pl.when(jnp.logical_not(first))
            def _():
                dw_ref[h] += tot[3 + h]

    return pl.pallas_call(
        body, name=name, grid=(t // SG_ROWS,),
        in_specs=[_rows(SG_ROWS, 3 * D_SG), _full((1, D_SG)), _full((1, D_SG)),
                  _full((SG_HEADS, SG_CHUNK, SG_CHUNK)), _full((SG_CHUNK, LANE)), _rows(SG_ROWS, D_SG)],
        out_specs=[_rows(SG_ROWS, 3 * D_SG), _full((1, D_SG)), _full((1, D_SG)),
                   _full((SG_HEADS, SG_CHUNK, SG_CHUNK)), _full((SG_CHUNK, LANE))],
        out_shape=[_sds((t, 3 * D_SG), _MXU), _sds((1, D_SG)), _sds((1, D_SG)), _sds((SG_HEADS, SG_CHUNK, SG_CHUNK)),
                   _sds((SG_CHUNK, LANE))],
        compiler_params=_cp(1),
    )(z, lng, lnb, w, bt, dy)


def add_pairs(a_list, b_list, name):
    n = len(a_list)

    def body(*refs):
        for a_ref, b_ref, o_ref in zip(refs[:n], refs[n:2 * n], refs[2 * n:]):
            o_ref[...] = (a_ref[...].astype(f32) + b_ref[...].astype(f32)).astype(o_ref.dtype)

    return pl.pallas_call(
        body, name=name, out_shape=[_sds(a.shape, a.dtype) for a in a_list],
        compiler_params=pltpu.CompilerParams(vmem_limit_bytes=VMEM_BIG),
    )(*a_list, *b_list)


def sum_adamw(half, recv, w, m, v, name):
    _, r, c = w.shape
    tr = 256 if r % 256 == 0 else r

    def body(half_ref, recv_ref, w_ref, m_ref, v_ref, g_ref, d_ref, nm_ref, nv_ref):
        g = recv_ref[0].astype(f32)
        for k in range(1, N_CHIPS):
            g = g + recv_ref[k].astype(f32)
        wv = w_ref[...]
        nm = ADAM_B1 * m_ref[...] + (1.0 - ADAM_B1) * g
        nv = ADAM_B2 * v_ref[...] + (1.0 - ADAM_B2) * jnp.square(g)
        m_hat = nm / (1.0 - ADAM_B1 ** ADAM_STEP)
        v_hat = nv / (1.0 - ADAM_B2 ** ADAM_STEP)
        g_ref[...] = g
        d_ref[...] = -ADAM_LR * (m_hat / (jnp.sqrt(v_hat) + ADAM_EPS) + ADAM_WD * wv)
        nm_ref[...] = nm
        nv_ref[...] = nv

    own = pl.BlockSpec((None, tr, c), lambda i, h: (h[0], i, 0))
    return pl.pallas_call(
        body, name=name,
        grid_spec=pltpu.PrefetchScalarGridSpec(
            num_scalar_prefetch=1, grid=(r // tr,),
            in_specs=[pl.BlockSpec((N_CHIPS, tr, c), lambda i, h: (0, i, 0)), own, own, own], out_specs=[own] * 4),
        out_shape=[_sds((2, r, c))] * 4,
        compiler_params=_cp(1, VMEM_BIG),
    )(half, recv, w, m, v)


_ANY = pl.BlockSpec(memory_space=pl.ANY)
_MESH = pl.DeviceIdType.MESH


def _flip(v, bit):
    return 1 - v if bit else v


_CHIP_RELS = ((1, 0), (0, 1), (1, 1))


def _piece(ref, kind, q):
    if kind[0] == "slot":
        return ref.at[q]
    if kind[0] == "all":
        return ref
    _, axis, n = kind
    return ref.at[(slice(None),) * axis + (pl.ds(q * n, n),)]


def _piece_shape(shape, kind):
    if kind[0] == "slot":
        return tuple(shape[1:])
    if kind[0] == "all":
        return tuple(shape)
    _, axis, n = kind
    return tuple(shape[:axis]) + (n,) + tuple(shape[axis + 1:])


def gather_weights(shards, kinds, name):
    n = len(shards)

    def out_shape(s, kind):
        if kind[0] == "slot":
            return (N_CHIPS,) + tuple(s.shape)
        _, axis, w = kind
        return tuple(s.shape[:axis + 1]) + (N_CHIPS * w,) + tuple(s.shape[axis + 2:])

    def place(o_ref, kind, q, layer):
        if kind[0] == "slot":
            return o_ref.at[q, layer]
        return _piece(o_ref.at[layer], kind, q)

    def body(*refs):
        s_refs, o_refs = refs[:n], refs[n:2 * n]
        send_sems, recv_sems, fwd_send_sems, fwd_recv_sems = refs[2 * n:]
        x, y, c = lax.axis_index("x"), lax.axis_index("y"), lax.axis_index("c")
        mine = 2 * x + y
        sends, arrivals, forwards, fwd_arrivals = [], [], [], []
        for r, (fx, fy) in enumerate(_CHIP_RELS):
            px, py = _flip(x, fx), _flip(y, fy)
            peer = 2 * px + py
            for k in range(n):
                s = r * n + k
                sends.append(pltpu.make_async_remote_copy(
                    src_ref=s_refs[k].at[c], dst_ref=place(o_refs[k], kinds[k], mine, c), send_sem=send_sems.at[s],
                    recv_sem=recv_sems.at[s], device_id=(px, py, c), device_id_type=_MESH))
                arrivals.append(pltpu.make_async_remote_copy(
                    src_ref=s_refs[k].at[c], dst_ref=place(o_refs[k], kinds[k], peer, c), send_sem=send_sems.at[s],
                    recv_sem=recv_sems.at[s], device_id=(px, py, c), device_id_type=_MESH))
                block = place(o_refs[k], kinds[k], peer, c)
                forwards.append(pltpu.make_async_remote_copy(
                    src_ref=block, dst_ref=block, send_sem=fwd_send_sems.at[s], recv_sem=fwd_recv_sems.at[s],
                    device_id=(x, y, 1 - c), device_id_type=_MESH))
                other = place(o_refs[k], kinds[k], peer, 1 - c)
                fwd_arrivals.append(pltpu.make_async_remote_copy(
                    src_ref=other, dst_ref=other, send_sem=fwd_send_sems.at[s], recv_sem=fwd_recv_sems.at[s],
                    device_id=(x, y, 1 - c), device_id_type=_MESH))
        for cp in sends:
            cp.start()
        for arrived, fwd in zip(arrivals, forwards):
            arrived.wait_recv()
            fwd.start()
        for cp in fwd_arrivals:
            cp.wait_recv()
        for cp in sends + forwards:
            cp.wait_send()

    m = len(_CHIP_RELS) * n
    return pl.pallas_call(
        body, name=name, in_specs=[_ANY] * n, out_specs=[_ANY] * n,
        out_shape=[_sds(out_shape(s, k), s.dtype) for s, k in zip(shards, kinds)],
        scratch_shapes=[pltpu.SemaphoreType.DMA((m,))] * 4,
    )(*shards)


def exchange_halves(gs, name):
    n = len(gs)

    def body(*refs):
        g_refs, got_refs, (send_sems, recv_sems) = refs[:n], refs[n:2 * n], refs[2 * n:]
        x, y, c = lax.axis_index("x"), lax.axis_index("y"), lax.axis_index("c")
        swaps = [pltpu.make_async_remote_copy(
            src_ref=g_refs[k].at[1 - c], dst_ref=got_refs[k], send_sem=send_sems.at[k], recv_sem=recv_sems.at[k],
            device_id=(x, y, 1 - c), device_id_type=_MESH) for k in range(n)]
        for cp in swaps:
            cp.start()
        for cp in swaps:
            cp.wait()

    return pl.pallas_call(
        body, name=name, in_specs=[_ANY] * n, out_specs=[_ANY] * n,
        out_shape=[_sds(g.shape[1:], g.dtype) for g in gs],
        scratch_shapes=[pltpu.SemaphoreType.DMA((n,)), pltpu.SemaphoreType.DMA((n,))],
    )(*gs)


def reduce_to_chips(ts, kinds, name):
    n = len(ts)

    def body(*refs):
        t_refs, o_refs, (send_sems, recv_sems) = refs[:n], refs[n:2 * n], refs[2 * n:]
        x, y, c = lax.axis_index("x"), lax.axis_index("y"), lax.axis_index("c")
        mine = 2 * x + y
        sends, arrivals = [], []
        for r, (fx, fy) in enumerate(_CHIP_RELS):
            px, py = _flip(x, fx), _flip(y, fy)
            peer = 2 * px + py
            for k in range(n):
                s = r * n + k
                sends.append(pltpu.make_async_remote_copy(
                    src_ref=_piece(t_refs[k], kinds[k], peer), dst_ref=o_refs[k].at[mine], send_sem=send_sems.at[s],
                    recv_sem=recv_sems.at[s], device_id=(px, py, c), device_id_type=_MESH))
                arrivals.append(pltpu.make_async_remote_copy(
                    src_ref=_piece(t_refs[k], kinds[k], peer), dst_ref=o_refs[k].at[peer], send_sem=send_sems.at[s],
                    recv_sem=recv_sems.at[s], device_id=(px, py, c), device_id_type=_MESH))
        for cp in sends:
            cp.start()
        for cp in arrivals:
            cp.wait_recv()
        for cp in sends:
            cp.wait_send()

    m = len(_CHIP_RELS) * n
    return pl.pallas_call(
        body, name=name, in_specs=[_ANY] * n, out_specs=[_ANY] * n,
        out_shape=[_sds((N_CHIPS,) + _piece_shape(t.shape, k), t.dtype) for t, k in zip(ts, kinds)],
        scratch_shapes=[pltpu.SemaphoreType.DMA((m,)), pltpu.SemaphoreType.DMA((m,))],
    )(*ts)


def share_halves(rs, name):
    n = len(rs)

    def body(*refs):
        o_refs, (send_sems, recv_sems) = refs[n:2 * n], refs[2 * n:]
        x, y, c = lax.axis_index("x"), lax.axis_index("y"), lax.axis_index("c")
        swaps = [pltpu.make_async_remote_copy(
            src_ref=o_refs[k].at[c], dst_ref=o_refs[k].at[c], send_sem=send_sems.at[k], recv_sem=recv_sems.at[k],
            device_id=(x, y, 1 - c), device_id_type=_MESH) for k in range(n)]
        arrivals = [pltpu.make_async_remote_copy(
            src_ref=o_refs[k].at[c], dst_ref=o_refs[k].at[1 - c], send_sem=send_sems.at[k], recv_sem=recv_sems.at[k],
            device_id=(x, y, 1 - c), device_id_type=_MESH) for k in range(n)]
        for cp in swaps:
            cp.start()
        for cp in arrivals:
            cp.wait_recv()
        for cp in swaps:
            cp.wait_send()

    return pl.pallas_call(
        body, name=name, in_specs=[_ANY] * n, out_specs=[_ANY] * n,
        out_shape=[_sds(r.shape, r.dtype) for r in rs], input_output_aliases={k: k for k in range(n)},
        scratch_shapes=[pltpu.SemaphoreType.DMA((n,)), pltpu.SemaphoreType.DMA((n,))],
    )(*rs)


def _pack_rows(parts, mult, dtype):
    blocks = []
    for a in parts:
        flat = a.reshape(-1).astype(dtype)
        rows = -(-flat.shape[0] // LANE)
        blocks.append(jnp.pad(flat, (0, rows * LANE - flat.shape[0])).reshape(rows, LANE))
    buf = jnp.concatenate(blocks, axis=0)
    return jnp.pad(buf, ((0, -buf.shape[0] % mult), (0, 0)))


def _unpack_rows(buf, shapes):
    out, row = [], 0
    for s in shapes:
        n = 1
        for d in s:
            n *= d
        rows = -(-n // LANE)
        out.append(buf[row:row + rows].reshape(-1)[:n].reshape(s))
        row += rows
    return out


def _permuted_from_shards(shards):
    parts = []
    for lo, hi in GROUP_COLS:
        for q in range(N_CHIPS):
            a, b = max(lo, q * SHARD_COLS), min(hi, (q + 1) * SHARD_COLS)
            if a < b:
                parts.append(shards[q][..., a - q * SHARD_COLS:b - q * SHARD_COLS])
    pad = jnp.zeros(shards[0].shape[:-1] + (D_IN_PAD - D_IN,), shards[0].dtype)
    return jnp.concatenate(parts + [pad], axis=-1)


def _shards_from_groups(groups):
    in_order = sorted(range(len(GROUP_COLS)), key=lambda j: GROUP_COLS[j][0])
    shards = []
    for q in range(N_CHIPS):
        parts = []
        for j in in_order:
            lo, hi = GROUP_COLS[j]
            a, b = max(lo, q * SHARD_COLS), min(hi, (q + 1) * SHARD_COLS)
            if a < b:
                parts.append(groups[j][..., a - lo:b - lo])
        shards.append(jnp.concatenate(parts, axis=-1))
    return shards


def _expand_b(b):
    eye = jnp.eye(SSM_GROUPS, dtype=b.dtype)
    return jnp.einsum("gnc,gh->gchn", b, eye).reshape(D_SSM, N_STATE)


def _extract_b(e):
    return jnp.einsum("gcgn->gnc", e.reshape(SSM_GROUPS, SSM_GROUP, SSM_GROUPS, SSM_STATE))


def _expand_c(c):
    eye = jnp.eye(SSM_GROUPS, dtype=c.dtype)
    return jnp.einsum("gcn,gh->gnhc", c, eye).reshape(N_STATE, D_SSM)


def _extract_c(e):
    return jnp.einsum("gngc->gcn", e.reshape(SSM_GROUPS, SSM_STATE, SSM_GROUPS, SSM_GROUP))


def _lane_row(v):
    return jnp.pad(v, (0, LANE - v.shape[0])).reshape(1, LANE)


def _layer_params(w, l):
    return dict(
        norm_g=w["norm_g"][l][None], win=w["w_in_perm"][l], wout=w["w_out"][l].astype(_MXU),
        pg=w["ple_norm_g"][l][None], wgate=w["w_ple_gate"][l].astype(_MXU), wple=w["w_ple"][l].astype(_MXU),
        are=w["ssm_a_re"][l].reshape(1, N_STATE), aim=w["ssm_a_im"][l].reshape(1, N_STATE),
        ls=jnp.repeat(w["ssm_log_step"][l], SSM_STATE).reshape(1, N_STATE),
        bre=_expand_b(w["ssm_b_re"][l]), bim=_expand_b(w["ssm_b_im"][l]),
        cr=_expand_c(w["ssm_c_re"][l]), ci=_expand_c(w["ssm_c_im"][l]),
        dr=w["ssm_d"][l].reshape(1, D_SSM), wglu=w["ssm_w_glu"][l].astype(f32), bglu=w["ssm_b_glu"][l][None],
        convw=w["dn_conv_w"][l], alog=_lane_row(w["dn_a_log"][l]), dtb=_lane_row(w["dn_dt_bias"][l]),
        ng=w["dn_norm_g"][l][None],
        lng=w["sg_ln_g"][l][None], lnb=w["sg_ln_b"][l][None], sgw=w["sg_w"][l],
        bt=jnp.pad(w["sg_b"][l].T, ((0, 0), (0, LANE - SG_HEADS))),
    )


def _layer_fwd(x, p, lp, nb, tag):
    seq = x.shape[0] // nb
    h, zs, zq, zg, zsg, zab = in_fwd(x, lp["norm_g"], lp["win"], f"in_fwd{tag}")
    prep = s5_prep_fwd(lp["are"], lp["aim"], lp["ls"], lp["bre"], lp["bim"], f"s5_prep_fwd{tag}")
    s5p = tuple(prep) + (lp["cr"], lp["ci"], lp["dr"], lp["wglu"], lp["bglu"])
    ys, hs = s5_fwd(zs, s5p, nb, f"s5_fwd{tag}")
    qkv = dn_pre_fwd(zq, lp["convw"], seq, f"dn_pre_fwd{tag}")
    loc, inv = dn_local_fwd(qkv, zab, lp["alog"], lp["dtb"], f"dn_local_fwd{tag}")
    yd, ss = dn_scan_fwd(loc, zg, lp["ng"], nb, f"dn_scan_fwd{tag}")
    yg = sg_fwd(zsg, lp["lng"], lp["lnb"], lp["sgw"], lp["bt"], f"sg_fwd{tag}")
    x2, x1, y, hn = post_fwd(x, ys, yd, yg, p, lp["wout"], lp["pg"], lp["wgate"], lp["wple"], f"post_fwd{tag}")
    saved = dict(x=x, h=h, zs=zs, zq=zq, zg=zg, zsg=zsg, zab=zab, s5p=s5p, hs=hs, qkv=qkv, loc=loc, inv=inv, ss=ss, x1=x1, y=y, hn=hn, p=p)
    return x2, saved


def _layer_bwd(dx2, sv, lp, nb, tag):
    seq = dx2.shape[0] // nb
    dx1, dgp, dpp, dys, dyd, dyg, dpg = post_bwd(dx2, sv["x1"], sv["hn"], sv["p"], lp["wout"], lp["pg"], lp["wgate"],
                                                 lp["wple"], f"post_bwd{tag}")
    g = {}
    g["w_out"] = wgrad(sv["y"], dx1, f"wgrad_out{tag}")
    g["w_ple_gate"] = wgrad(sv["hn"], dgp, f"wgrad_gate{tag}")
    g["w_ple"] = wgrad(sv["p"], dpp, f"wgrad_ple{tag}")
    g["ple_norm_g"] = dpg[0]
    dzsg, dlng, dlnb, dsgw, dbt = sg_bwd(sv["zsg"], lp["lng"], lp["lnb"], lp["sgw"], lp["bt"], dyg, f"sg_bwd{tag}")
    g["sg_ln_g"], g["sg_ln_b"], g["sg_w"], g["sg_b"] = dlng[0], dlnb[0], dsgw, dbt[:, :SG_HEADS].T
    dloc, dzg, dng = dn_scan_bwd(sv["loc"], sv["zg"], lp["ng"], sv["ss"], dyd, nb, f"dn_scan_bwd{tag}")
    dqkv, dzab, dalog, ddtb = dn_local_bwd(sv["qkv"], sv["zab"], lp["alog"], lp["dtb"], sv["inv"], dloc,
                                           f"dn_local_bwd{tag}")
    dzq, dconv = dn_pre_bwd(sv["zq"], lp["convw"], dqkv, seq, f"dn_pre_bwd{tag}")
    g["dn_conv_w"], g["dn_a_log"], g["dn_dt_bias"], g["dn_norm_g"] = dconv, dalog[0, :DN_HEADS], ddtb[0, :DN_HEADS], dng[0]
    s5out = s5_bwd(sv["zs"], sv["s5p"], sv["hs"], dys, nb, f"s5_bwd{tag}")
    dzs, dprep, (dcr, dci, ddr, dwglu, dbglu) = s5out[0], s5out[1:1 + S5_PREPARED], s5out[1 + S5_PREPARED:]
    dare, daim, dls, dbre, dbim = s5_prep_bwd(lp["are"], lp["aim"], lp["ls"], lp["bre"], lp["bim"], dprep,
                                              f"s5_prep_bwd{tag}")
    g["ssm_a_re"] = dare.reshape(SSM_GROUPS, SSM_STATE)
    g["ssm_a_im"] = daim.reshape(SSM_GROUPS, SSM_STATE)
    g["ssm_log_step"] = dls.reshape(SSM_GROUPS, SSM_STATE).sum(axis=1)
    g["ssm_b_re"], g["ssm_b_im"] = _extract_b(dbre), _extract_b(dbim)
    g["ssm_c_re"], g["ssm_c_im"] = _extract_c(dcr), _extract_c(dci)
    g["ssm_d"] = ddr.reshape(SSM_GROUPS, SSM_GROUP)
    g["ssm_w_glu"], g["ssm_b_glu"] = dwglu, dbglu[0]
    dzs_all = (dzs, dzq, dzg, dzsg, dzab)
    dx, dng_in = in_bwd(sv["x"], lp["norm_g"], lp["win"], dzs_all, dx1, f"in_bwd{tag}")
    g["w_in_pieces"] = [wgrad(sv["h"], dz, f"wgrad_in{k}{tag}") for k, dz in enumerate(dzs_all)]
    g["norm_g"] = dng_in[0]
    return dx, g


def _local_step(x, p, target, w, nb):
    lps = [_layer_params(w, l) for l in range(DEPTH)]
    saved = []
    for l in range(DEPTH):
        x, sv = _layer_fwd(x, p[l], lps[l], nb, f"_l{l}")
        saved.append(sv)
    loss_blk, dx, dfg = loss_fwd_bwd(x, w["final_norm_g"][None], target, "loss")
    grads = [None] * DEPTH
    for l in reversed(range(DEPTH)):
        dx, grads[l] = _layer_bwd(dx, saved[l], lps[l], nb, f"_l{l}")
    out = {k: jnp.stack([grads[l][k] for l in range(DEPTH)]) for k in grads[0] if k != "w_in_pieces"}
    out["w_in_pieces"] = [grads[l]["w_in_pieces"] for l in range(DEPTH)]
    out["final_norm_g"] = dfg[0]
    return loss_blk[0, 0], dx, out


def kernel(x, p, norm_g, w_in, ssm_a_re, ssm_a_im, ssm_b_re, ssm_b_im, ssm_c_re, ssm_c_im, ssm_d, ssm_log_step, ssm_w_glu, ssm_b_glu, dn_conv_w, dn_a_log, dn_dt_bias, dn_norm_g, sg_ln_g, sg_ln_b, sg_w, sg_b, w_out, ple_norm_g, w_ple_gate, w_ple, final_norm_g, loss_target, m_norm_g, m_w_in, m_ssm_a_re, m_ssm_a_im, m_ssm_b_re, m_ssm_b_im, m_ssm_c_re, m_ssm_c_im, m_ssm_d, m_ssm_log_step, m_ssm_w_glu, m_ssm_b_glu, m_dn_conv_w, m_dn_a_log, m_dn_dt_bias, m_dn_norm_g, m_sg_ln_g, m_sg_ln_b, m_sg_w, m_sg_b, m_w_out, m_ple_norm_g, m_w_ple_gate, m_w_ple, m_final_norm_g, v_norm_g, v_w_in, v_ssm_a_re, v_ssm_a_im, v_ssm_b_re, v_ssm_b_im, v_ssm_c_re, v_ssm_c_im, v_ssm_d, v_ssm_log_step, v_ssm_w_glu, v_ssm_b_glu, v_dn_conv_w, v_dn_a_log, v_dn_dt_bias, v_dn_norm_g, v_sg_ln_g, v_sg_ln_b, v_sg_w, v_sg_b, v_w_out, v_ple_norm_g, v_w_ple_gate, v_w_ple, v_final_norm_g):
    args = locals()
    w = {n: args[n] for n in WEIGHTS}
    m = {n: args["m_" + n] for n in WEIGHTS}
    v = {n: args["v_" + n] for n in WEIGHTS}
    nb, seq = x.shape[0], x.shape[1]
    t = nb * seq

    full = _gather_full(w)
    loss_local, dx, grads = _local_step(x.reshape(t, D_MODEL), p.reshape(DEPTH, t, D_PLE),
                                        loss_target.reshape(t, D_MODEL), full, nb)
    outs, loss = _reduce_and_update(grads, w, m, v, loss_local)
    return (loss, dx.reshape(nb, seq, D_MODEL), *[outs[0][n] for n in WEIGHTS], *[outs[1][n] for n in WEIGHTS],
            *[outs[2][n] for n in WEIGHTS], *[outs[3][n] for n in WEIGHTS])


def _gather_full(w):
    sh_names = [n for n, _ in SHARDED]
    shards = [w[n] if n == "dn_conv_w" else w[n].astype(_COMM) for n in sh_names]
    gathered = gather_weights(shards, [k for _, k in SHARDED], "gather_weights")
    chip = 2 * lax.axis_index("x") + lax.axis_index("y")
    full = {n: w[n] for n in REPLICATED}
    for (n, kind), shard, got in zip(SHARDED, shards, gathered):
        if kind[0] == "slot":
            full[n] = lax.dynamic_update_index_in_dim(got, shard, chip, 0)
        else:
            full[n] = lax.dynamic_update_slice_in_dim(got, shard, chip * kind[2], axis=kind[1] + 1)
    slots = full.pop("w_in")
    full["w_in_perm"] = _permuted_from_shards([slots[q] for q in range(N_CHIPS)]).astype(_MXU)
    return full


def _reduce_and_update(grads, w, m, v, loss_local):
    sh_names = [n for n, _ in SHARDED]
    sh_kinds = [k for _, k in SHARDED]

    def pack_small(d, last):
        buf = _pack_rows([d[n] for n in REPLICATED] + [last], 512, f32)
        return buf.reshape(2, buf.shape[0] // 2, LANE)

    no_state = jnp.zeros((1,), f32)
    grads["w_in"] = jnp.stack([jnp.stack(_shards_from_groups(pieces)) for pieces in grads["w_in_pieces"]])
    gs = [grads[n] if n == "dn_conv_w" else grads[n].astype(_COMM) for n in sh_names]
    gs.append(pack_small(grads, loss_local.reshape(1)))
    kinds = sh_kinds + [("all",)]
    core = lax.axis_index("c")
    chip = 2 * lax.axis_index("x") + lax.axis_index("y")
    got = exchange_halves(gs, "exchange_halves")
    sums = add_pairs([lax.dynamic_index_in_dim(g, core, 0, keepdims=False) for g in gs], got, "add_halves")
    parts = list(reduce_to_chips(sums, kinds, "reduce_to_chips"))
    for k, (kind, total) in enumerate(zip(kinds, sums)):
        if kind[0] == "slot":
            own = lax.dynamic_index_in_dim(total, chip, 0, keepdims=False)
        elif kind[0] == "win":
            own = lax.dynamic_slice_in_dim(total, chip * kind[2], kind[2], axis=kind[1])
        else:
            own = total
        parts[k] = lax.dynamic_update_index_in_dim(parts[k], own, chip, 0)
    states = [(w[n], m[n], v[n]) for n in sh_names]
    states.append((pack_small(w, no_state), pack_small(m, no_state), pack_small(v, no_state)))
    half = core.astype(jnp.int32).reshape(1)
    results = []
    for n, part, (wn, mn, vn) in zip(sh_names + ["replicated"], parts, states):
        results += sum_adamw(half, part, wn, mn, vn, f"adamw_{n}")
    shared = share_halves(results, "share_halves")
    rep_shapes = [w[n].shape for n in REPLICATED] + [(1,)]
    outs = []
    for j in range(4):
        d = {n: shared[4 * k + j] for k, n in enumerate(sh_names)}
        small = shared[4 * len(sh_names) + j]
        d.update(zip(REPLICATED + ("loss",), _unpack_rows(small.reshape(-1, LANE), rep_shapes)))
        outs.append(d)
    return outs, outs[0]["loss"][0]
```

```python
import functools

import jax
import jax.numpy as jnp
from jax import lax
from jax.experimental import pallas as pl
from jax.experimental.pallas import tpu as pltpu

f32 = jnp.float32
bf16 = jnp.bfloat16

_MXU = bf16
_COMM = bf16
HIGH = lax.Precision.HIGH

D_MODEL = 1024
DEPTH = 2
D_PLE = 256
D_SSM = 256
D_DN = 512
D_SG = 256
SSM_GROUPS = 16
SSM_GROUP = 16
SSM_STATE = 64
N_STATE = SSM_GROUPS * SSM_STATE
DN_HEADS = 4
DN_HEAD_DIM = 128
DN_CONV = 4
DN_CHUNK = 64
SG_HEADS = 4
SG_HEAD_DIM = 64
SG_CHUNK = 128
S5_CHUNK = 256
S5_GROUP_ROWS = 8
EPS = 1e-6
D_IN = 3336
D_IN_PAD = 3456
LANE = 128

ADAM_LR = 0.001
ADAM_B1 = 0.9
ADAM_B2 = 0.999
ADAM_EPS = 1e-08
ADAM_WD = 0.01
ADAM_STEP = 10

N_CHIPS = 4
N_DEV = 8

Z_COLS = ((0, 512), (512, 2048), (2048, 2560), (2560, 3328), (3328, 3456))

GROUP_COLS = ((0, 512), (512, 2048), (2056, 2568), (2568, 3336), (2048, 2056))
SHARD_COLS = D_IN // 4

SHARDED = (("w_in", ("slot",)), ("ssm_w_glu", ("win", 0, 64)), ("dn_conv_w", ("win", 1, 384)),
           ("w_out", ("win", 0, 256)), ("w_ple_gate", ("win", 0, 256)), ("w_ple", ("win", 1, 256)))
REPLICATED = ("norm_g", "ssm_a_re", "ssm_a_im", "ssm_b_re", "ssm_b_im", "ssm_c_re", "ssm_c_im", "ssm_d",
              "ssm_log_step", "ssm_b_glu", "dn_a_log", "dn_dt_bias", "dn_norm_g", "sg_ln_g", "sg_ln_b", "sg_w",
              "sg_b", "ple_norm_g", "final_norm_g")
SMALL_OWNER = {n: int(n.startswith("ssm_")) for n in REPLICATED + ("loss",)}
WEIGHTS = ("norm_g", "w_in", "ssm_a_re", "ssm_a_im", "ssm_b_re", "ssm_b_im", "ssm_c_re", "ssm_c_im", "ssm_d",
           "ssm_log_step", "ssm_w_glu", "ssm_b_glu", "dn_conv_w", "dn_a_log", "dn_dt_bias", "dn_norm_g", "sg_ln_g",
           "sg_ln_b", "sg_w", "sg_b", "w_out", "ple_norm_g", "w_ple_gate", "w_ple", "final_norm_g")

VMEM_BIG = 56 * 1024 * 1024


def _mm(a, b):
    return jnp.dot(a.astype(_MXU), b.astype(_MXU), preferred_element_type=f32)


def _mm_nt(a, b):
    return lax.dot_general(a.astype(_MXU), b.astype(_MXU), (((1,), (1,)), ((), ())), preferred_element_type=f32)


def _mm_tn(a, b):
    return lax.dot_general(a.astype(_MXU), b.astype(_MXU), (((0,), (0,)), ((), ())), preferred_element_type=f32)


@jax.custom_vjp
def bdot(a, b):
    return _mm(a, b)


def _bdot_fwd(a, b):
    return _mm(a, b), (a, b)


def _bdot_bwd(res, g):
    a, b = res
    return _mm_nt(g, b).astype(a.dtype), _mm_tn(a, g).astype(b.dtype)


bdot.defvjp(_bdot_fwd, _bdot_bwd)


@jax.custom_vjp
def bdot_nt(a, b):
    return _mm_nt(a, b)


def _bdot_nt_fwd(a, b):
    return _mm_nt(a, b), (a, b)


def _bdot_nt_bwd(res, g):
    a, b = res
    return _mm(g, b).astype(a.dtype), _mm_tn(g, a).astype(b.dtype)


bdot_nt.defvjp(_bdot_nt_fwd, _bdot_nt_bwd)


@jax.custom_vjp
def bdot_tn(a, b):
    return _mm_tn(a, b)


def _bdot_tn_fwd(a, b):
    return _mm_tn(a, b), (a, b)


def _bdot_tn_bwd(res, g):
    a, b = res
    return _mm_nt(b, g).astype(a.dtype), _mm(a, g).astype(b.dtype)


bdot_tn.defvjp(_bdot_tn_fwd, _bdot_tn_bwd)


def hdot(a, b):
    return jnp.dot(a, b, precision=HIGH, preferred_element_type=f32)


def _unit_lower_inverses(ms):
    n = ms[0].shape[0]
    eye = (lax.broadcasted_iota(jnp.int32, (n, n), 0) == lax.broadcasted_iota(jnp.int32, (n, n), 1)).astype(f32)
    pw = [-m for m in ms]
    inv = [eye + p for p in pw]
    for _ in range(n.bit_length() - 2):
        pw = [hdot(p, p) for p in pw]
        inv = [a + hdot(a, p) for a, p in zip(inv, pw)]
    return inv


@jax.custom_vjp
def solve_unit_lower(ms, rhs, inv):
    return [hdot(a, r) for a, r in zip(inv, rhs)]


def _solve_unit_lower_fwd(ms, rhs, inv):
    xs = [hdot(a, r) for a, r in zip(inv, rhs)]
    return xs, (inv, xs)


def _solve_unit_lower_bwd(res, gs):
    inv, xs = res
    d_rhs = [lax.dot_general(a, g, (((0,), (0,)), ((), ())), precision=HIGH, preferred_element_type=f32)
             for a, g in zip(inv, gs)]
    d_ms = [-lax.dot_general(d, x, (((1,), (1,)), ((), ())), precision=HIGH, preferred_element_type=f32)
            for d, x in zip(d_rhs, xs)]
    return d_ms, d_rhs, [jnp.zeros_like(a) for a in inv]


solve_unit_lower.defvjp(_solve_unit_lower_fwd, _solve_unit_lower_bwd)


@functools.partial(jax.custom_vjp, nondiff_argnums=(1,))
def roll_rows(x, k):
    return pltpu.roll(x, k, 0)


def _roll_rows_fwd(x, k):
    return pltpu.roll(x, k, 0), None


def _roll_rows_bwd(k, _, g):
    return (pltpu.roll(g, g.shape[0] - k, 0),)


roll_rows.defvjp(_roll_rows_fwd, _roll_rows_bwd)


def _row_ids(shape):
    return lax.broadcasted_iota(jnp.int32, shape, 0)


def _rms(x, g):
    return x * lax.rsqrt(jnp.mean(x * x, axis=-1, keepdims=True) + EPS) * g


def _layer_norm(x, g, b):
    mu = jnp.mean(x, axis=-1, keepdims=True)
    xc = x - mu
    return xc * lax.rsqrt(jnp.mean(xc * xc, axis=-1, keepdims=True) + EPS) * g + b


def _s5_prep(are, aim, ls, bre, bim):
    step = jnp.exp(ls)
    mag = jnp.exp(are * step)
    lr = mag * jnp.cos(aim * step)
    li = mag * jnp.sin(aim * step)
    den = are * are + aim * aim
    nr, ni = lr - 1.0, li
    fr = (nr * are + ni * aim) / den
    fi = (ni * are - nr * aim) / den
    bbr = fr * bre - fi * bim
    bbi = fr * bim + fi * bre
    pr = jnp.broadcast_to(lr, (S5_GROUP_ROWS, N_STATE))
    pi = jnp.broadcast_to(li, (S5_GROUP_ROWS, N_STATE))
    d = 1
    while d < S5_GROUP_ROWS:
        keep = _row_ids(pr.shape) >= d
        sr, si = roll_rows(pr, d), roll_rows(pi, d)
        pr, pi = jnp.where(keep, pr * sr - pi * si, pr), jnp.where(keep, pr * si + pi * sr, pi)
        d *= 2
    return pr, pi, bbr, bbi


def _s5_chunk(u, gate, hr, hi, pr, pi, bbr, bbi, cr, ci, dr, wglu, bglu):
    n, grp = u.shape[0], S5_GROUP_ROWS
    xr = bdot(u, bbr)
    xi = bdot(u, bbi)
    sub = _row_ids(xr.shape) % grp
    d = 1
    while d < grp:
        lr, li = pr[d - 1:d], pi[d - 1:d]
        sr = jnp.where(sub >= d, roll_rows(xr, d), 0.0)
        si = jnp.where(sub >= d, roll_rows(xi, d), 0.0)
        xr, xi = xr + lr * sr - li * si, xi + lr * si + li * sr
        d *= 2
    outs_r, outs_i = [], []
    for g in range(n // grp):
        gr, gi = xr[g * grp:(g + 1) * grp], xi[g * grp:(g + 1) * grp]
        gr, gi = gr + pr * hr - pi * hi, gi + pr * hi + pi * hr
        hr, hi = gr[grp - 1:grp], gi[grp - 1:grp]
        outs_r.append(gr)
        outs_i.append(gi)
    xr = jnp.concatenate(outs_r, axis=0)
    xi = jnp.concatenate(outs_i, axis=0)
    y = bdot(xr, cr) - bdot(xi, ci) + dr * u
    y = jax.nn.gelu(y)
    y = y * jax.nn.sigmoid(bdot(y, wglu) + bglu)
    return y * jax.nn.silu(gate), hr, hi


def _dn_pre(xc, xp, w0, w1, w2, w3, is_start, col):
    xp = jnp.where(is_start, 0.0, xp)
    rows = _row_ids(xc.shape)
    acc = w3 * xc
    for d, w in ((1, w2), (2, w1), (3, w0)):
        acc = acc + w * jnp.where(rows >= d, roll_rows(xc, d), roll_rows(xp, d))
    y = jax.nn.silu(acc)
    nrm = y * lax.rsqrt(jnp.sum(y * y, axis=-1, keepdims=True) + EPS)
    nrm = nrm * jnp.where(col < DN_HEADS, DN_HEAD_DIM ** -0.5, 1.0)
    return jnp.where(col < 2 * DN_HEADS, nrm, y)


def _dn_local(qs, ks, vs, abs_, alog, dtb, invs=None):
    c = DN_CHUNK
    ri = lax.broadcasted_iota(jnp.int32, (c, c), 0)
    ci = lax.broadcasted_iota(jnp.int32, (c, c), 1)
    causal, strict = ri >= ci, ri > ci
    tril = causal.astype(f32)
    gcums = [hdot(tril, -jnp.exp(alog) * jax.nn.softplus(ab + dtb)) for ab in abs_]
    gcum_ts = [g.T for g in gcums]
    sigs = [jax.nn.sigmoid(ab) for ab in abs_]
    chains = [(j, h) for j in range(len(abs_)) for h in range(DN_HEADS)]
    gc = [gcums[j][:, h:h + 1] for j, h in chains]
    decay = [jnp.where(causal, jnp.exp(jnp.where(causal, gc[n] - gcum_ts[j][h:h + 1, :], 0.0)), 0.0)
             for n, (j, h) in enumerate(chains)]
    beta = [sigs[j][:, DN_HEADS + h:DN_HEADS + h + 1] for j, h in chains]
    kb = [ks[j][h] * beta[n] for n, (j, h) in enumerate(chains)]
    ms = [jnp.where(strict, bdot_nt(kb[n], ks[j][h]) * decay[n], 0.0) for n, (j, h) in enumerate(chains)]
    egc = [jnp.exp(g) for g in gc]
    rhs = [jnp.concatenate([vs[j][h] * beta[n], kb[n] * egc[n]], axis=1) for n, (j, h) in enumerate(chains)]
    inv = _unit_lower_inverses(ms) if invs is None else [invs[j][h] for j, h in chains]
    sol = solve_unit_lower(ms, rhs, inv)
    values = [s[:, :DN_HEAD_DIM] for s in sol]
    k_cds = [s[:, DN_HEAD_DIM:] for s in sol]
    attns = [bdot_nt(qs[j][h], ks[j][h]) * decay[n] for n, (j, h) in enumerate(chains)]
    q_decs = [qs[j][h] * egc[n] for n, (j, h) in enumerate(chains)]
    k_decs = [ks[j][h] * jnp.exp(gc[n][c - 1:c, :] - gc[n]) for n, (j, h) in enumerate(chains)]

    def nest(flat):
        return [flat[j * DN_HEADS:(j + 1) * DN_HEADS] for j in range(len(abs_))]

    lasts = [jnp.exp(g[c - 1:c, :]) for g in gcums]
    return nest(values), nest(k_cds), nest(attns), nest(q_decs), nest(k_decs), lasts, nest(inv)


def _dn_step(values, k_cds, attns, q_decs, k_decs, lasts, ggs, sts, ng):
    v_new = [v - bdot(kc, st) for v, kc, st in zip(values, k_cds, sts)]
    o = [bdot(qd, st) for qd, st in zip(q_decs, sts)]
    o = [a + bdot(at, vn) for a, at, vn in zip(o, attns, v_new)]
    new = [st * la + bdot_tn(kd, vn) for st, la, kd, vn in zip(sts, lasts, k_decs, v_new)]
    return [_rms(a, ng) * jax.nn.silu(g) for a, g in zip(o, ggs)], new


def _sg_chunk(u, v, gate, lng, lnb, ws, bt):
    n = SG_CHUNK
    ug = jax.nn.gelu(u)
    vn = _layer_norm(jax.nn.gelu(v), lng, lnb)
    causal = lax.broadcasted_iota(jnp.int32, (n, n), 0) >= lax.broadcasted_iota(jnp.int32, (n, n), 1)
    lane = lax.broadcasted_iota(jnp.int32, (n, D_SG), 1)
    s = jnp.zeros((n, D_SG), f32)
    for h in range(SG_HEADS):
        t = bdot(jnp.where(causal, ws[h], 0.0), vn) + bt[:, h:h + 1]
        s = s + jnp.where((lane >= h * SG_HEAD_DIM) & (lane < (h + 1) * SG_HEAD_DIM), t, 0.0)
    return ug * s * jax.nn.silu(gate)


def _cp(n_grid, vmem=None):
    return pltpu.CompilerParams(dimension_semantics=("arbitrary",) * n_grid, vmem_limit_bytes=vmem)


def _full(shape):
    nd = len(shape)
    return pl.BlockSpec(tuple(shape), lambda *_: (0,) * nd)


def _rows(tm, ncol):
    return pl.BlockSpec((tm, ncol), lambda i: (i, 0))


def _sds(shape, dtype=f32):
    return jax.ShapeDtypeStruct(tuple(shape), dtype)


def _acc(ref, val, first):
    @pl.when(first)
    def _():
        ref[...] = val

    @pl.when(jnp.logical_not(first))
    def _():
        ref[...] += val


def in_fwd(x, g, w, name):
    t, tm = x.shape[0], 256

    def body(x_ref, g_ref, w_ref, h_ref, *z_refs):
        h = _rms(x_ref[...], g_ref[...]).astype(_MXU)
        h_ref[...] = h
        for z_ref, (a, b) in zip(z_refs, Z_COLS):
            z_ref[...] = jnp.dot(h, w_ref[:, a:b], preferred_element_type=f32)

    widths = [b - a for a, b in Z_COLS]
    return pl.pallas_call(
        body, name=name, grid=(t // tm,),
        in_specs=[_rows(tm, D_MODEL), _full((1, D_MODEL)), _full((D_MODEL, D_IN_PAD))],
        out_specs=[_rows(tm, D_MODEL)] + [_rows(tm, n) for n in widths],
        out_shape=[_sds((t, D_MODEL), _MXU)] + [_sds((t, n)) for n in widths],
        compiler_params=_cp(1, VMEM_BIG),
    )(x, g, w)


def in_bwd(x, g, w, dzs, dres, name):
    t, tm = x.shape[0], 256
    widths = [b - a for a, b in Z_COLS]

    def body(x_ref, g_ref, w_ref, dres_ref, *rest):
        dz_refs, (dx_ref, dg_ref) = rest[:5], rest[5:]
        dh = jnp.zeros((tm, D_MODEL), f32)
        for dz_ref, (a, b) in zip(dz_refs, Z_COLS):
            dh = dh + _mm_nt(dz_ref[...], w_ref[:, a:b])
        _, vj = jax.vjp(_rms, x_ref[...], g_ref[...])
        dx, dg = vj(dh)
        dx_ref[...] = dres_ref[...] + dx
        _acc(dg_ref, dg, pl.program_id(0) == 0)

    return pl.pallas_call(
        body, name=name, grid=(t // tm,),
        in_specs=[_rows(tm, D_MODEL), _full((1, D_MODEL)), _full((D_MODEL, D_IN_PAD)), _rows(tm, D_MODEL)]
        + [_rows(tm, n) for n in widths],
        out_specs=[_rows(tm, D_MODEL), _full((1, D_MODEL))],
        out_shape=[_sds((t, D_MODEL)), _sds((1, D_MODEL))],
        compiler_params=_cp(1, VMEM_BIG),
    )(x, g, w, dres, *dzs)


def wgrad(a, g, name):
    t, k = a.shape
    n = g.shape[1]
    tm = min(t, 2048)
    tn = n if n <= 768 else (768 if n % 768 == 0 else 512)
    steps = t // tm

    def body(a_ref, g_ref, o_ref, acc):
        i = pl.program_id(1)
        _acc(acc, _mm_tn(a_ref[...], g_ref[...]), i == 0)

        @pl.when(i == steps - 1)
        def _():
            o_ref[...] = acc[...].astype(o_ref.dtype)

    return pl.pallas_call(
        body, name=name, grid=(n // tn, steps),
        in_specs=[pl.BlockSpec((tm, k), lambda j, i: (i, 0)), pl.BlockSpec((tm, tn), lambda j, i: (i, j))],
        out_specs=pl.BlockSpec((k, tn), lambda j, i: (0, j)),
        out_shape=_sds((k, n), _COMM),
        scratch_shapes=[pltpu.VMEM((k, tn), f32)],
        compiler_params=_cp(2, VMEM_BIG),
    )(a, g)


def post_fwd(x, ys, yd, yg, p, wout, pg, wgate, wple, name):
    t, tm = x.shape[0], 256

    def body(x_ref, ys_ref, yd_ref, yg_ref, p_ref, wout_ref, pg_ref, wgate_ref, wple_ref,
             x2_ref, x1_ref, y_ref, hn_ref):
        y = jnp.concatenate([ys_ref[...], yd_ref[...], yg_ref[...]], axis=1).astype(_MXU)
        y_ref[...] = y
        x1 = x_ref[...] + jnp.dot(y, wout_ref[...], preferred_element_type=f32)
        x1_ref[...] = x1
        hn = _rms(x1, pg_ref[...]).astype(_MXU)
        hn_ref[...] = hn
        gp = jnp.dot(hn, wgate_ref[...], preferred_element_type=f32)
        pp = _mm(p_ref[...], wple_ref[...])
        x2_ref[...] = x1 + jax.nn.sigmoid(gp) * pp

    return pl.pallas_call(
        body, name=name, grid=(t // tm,),
        in_specs=[_rows(tm, D_MODEL), _rows(tm, D_SSM), _rows(tm, D_DN), _rows(tm, D_SG), _rows(tm, D_PLE),
                  _full((D_MODEL, D_MODEL)), _full((1, D_MODEL)), _full((D_MODEL, D_MODEL)), _full((D_PLE, D_MODEL))],
        out_specs=[_rows(tm, D_MODEL)] * 4,
        out_shape=[_sds((t, D_MODEL)), _sds((t, D_MODEL)), _sds((t, D_MODEL), _MXU), _sds((t, D_MODEL), _MXU)],
        compiler_params=_cp(1, VMEM_BIG),
    )(x, ys, yd, yg, p, wout, pg, wgate, wple)


def post_bwd(dx2, x1, hn, p, wout, pg, wgate, wple, name):
    t, tm = dx2.shape[0], 256

    def body(dx2_ref, x1_ref, hn_ref, p_ref, wout_ref, pg_ref, wgate_ref, wple_ref,
             dx1_ref, dgp_ref, dpp_ref, dys_ref, dyd_ref, dyg_ref, dpg_ref):
        dx2 = dx2_ref[...]
        gp = jnp.dot(hn_ref[...], wgate_ref[...], preferred_element_type=f32)
        pp = _mm(p_ref[...], wple_ref[...])
        sg = jax.nn.sigmoid(gp)
        dpp_ref[...] = (dx2 * sg).astype(_MXU)
        dgp = (dx2 * pp * sg * (1.0 - sg)).astype(_MXU)
        dgp_ref[...] = dgp
        dhn = _mm_nt(dgp, wgate_ref[...])
        _, vj = jax.vjp(_rms, x1_ref[...], pg_ref[...])
        dx1n, dpg = vj(dhn)
        dx1 = dx2 + dx1n
        dx1_ref[...] = dx1
        dy = _mm_nt(dx1, wout_ref[...])
        dys_ref[...] = dy[:, :D_SSM]
        dyd_ref[...] = dy[:, D_SSM:D_SSM + D_DN]
        dyg_ref[...] = dy[:, D_SSM + D_DN:]
        _acc(dpg_ref, dpg, pl.program_id(0) == 0)

    return pl.pallas_call(
        body, name=name, grid=(t // tm,),
        in_specs=[_rows(tm, D_MODEL), _rows(tm, D_MODEL), _rows(tm, D_MODEL), _rows(tm, D_PLE),
                  _full((D_MODEL, D_MODEL)), _full((1, D_MODEL)), _full((D_MODEL, D_MODEL)), _full((D_PLE, D_MODEL))],
        out_specs=[_rows(tm, D_MODEL), _rows(tm, D_MODEL), _rows(tm, D_MODEL), _rows(tm, D_SSM), _rows(tm, D_DN),
                   _rows(tm, D_SG), _full((1, D_MODEL))],
        out_shape=[_sds((t, D_MODEL)), _sds((t, D_MODEL), _MXU), _sds((t, D_MODEL), _MXU), _sds((t, D_SSM)),
                   _sds((t, D_DN)), _sds((t, D_SG)), _sds((1, D_MODEL))],
        compiler_params=_cp(1, VMEM_BIG),
    )(dx2, x1, hn, p, wout, pg, wgate, wple)


def loss_fwd_bwd(x, fg, target, name):
    t, tm = x.shape[0], 512

    def body(x_ref, fg_ref, t_ref, loss_ref, dx_ref, dfg_ref):
        def f(xv, gv):
            err = _rms(xv, gv) - t_ref[...]
            return 0.5 * jnp.sum(jnp.mean(err * err, axis=-1))

        val, vj = jax.vjp(f, x_ref[...], fg_ref[...])
        dx, dfg = vj(jnp.ones((), f32))
        dx_ref[...] = dx
        first = pl.program_id(0) == 0
        _acc(dfg_ref, dfg, first)
        _acc(loss_ref, jnp.full((8, LANE), val, f32), first)

    return pl.pallas_call(
        body, name=name, grid=(t // tm,),
        in_specs=[_rows(tm, D_MODEL), _full((1, D_MODEL)), _rows(tm, D_MODEL)],
        out_specs=[_full((8, LANE)), _rows(tm, D_MODEL), _full((1, D_MODEL))],
        out_shape=[_sds((8, LANE)), _sds((t, D_MODEL)), _sds((1, D_MODEL))],
        compiler_params=_cp(1),
    )(x, fg, target)


S5_PREPARED = 4
_S5_PARAM_SHAPES = ((S5_GROUP_ROWS, N_STATE), (S5_GROUP_ROWS, N_STATE), (D_SSM, N_STATE), (D_SSM, N_STATE),
                    (N_STATE, D_SSM), (N_STATE, D_SSM), (1, D_SSM), (D_SSM, D_SSM), (1, D_SSM))

def s5_prep_fwd(are, aim, ls, bre, bim, name):
    def body(are_ref, aim_ref, ls_ref, bre_ref, bim_ref, *outs):
        vals = _s5_prep(are_ref[...], aim_ref[...], ls_ref[...], bre_ref[...], bim_ref[...])
        for o, v in zip(outs, vals):
            o[...] = v

    return pl.pallas_call(body, name=name, out_shape=[_sds(s) for s in _S5_PARAM_SHAPES[:S5_PREPARED]])(
        are, aim, ls, bre, bim)


def s5_prep_bwd(are, aim, ls, bre, bim, cts, name):
    def body(are_ref, aim_ref, ls_ref, bre_ref, bim_ref, *rest):
        ct_refs, outs = rest[:S5_PREPARED], rest[S5_PREPARED:]
        _, vj = jax.vjp(_s5_prep, are_ref[...], aim_ref[...], ls_ref[...], bre_ref[...], bim_ref[...])
        for o, v in zip(outs, vj(tuple(r[...] for r in ct_refs))):
            o[...] = v

    shapes = [(1, N_STATE)] * 3 + [(D_SSM, N_STATE)] * 2
    return pl.pallas_call(body, name=name, out_shape=[_sds(s) for s in shapes])(are, aim, ls, bre, bim, *cts)


def s5_fwd(z, params, nb, name):
    t = z.shape[0]
    nc = t // nb // S5_CHUNK
    npar = len(_S5_PARAM_SHAPES)

    def body(z_ref, *rest):
        p_refs, (y_ref, hs_ref, hr_s, hi_s) = rest[:npar], rest[npar:]

        @pl.when(pl.program_id(1) == 0)
        def _():
            hr_s[...] = jnp.zeros_like(hr_s)
            hi_s[...] = jnp.zeros_like(hi_s)

        hr, hi = hr_s[...], hi_s[...]
        hs_ref[0, :, :N_STATE] = hr
        hs_ref[0, :, N_STATE:] = hi
        y, nhr, nhi = _s5_chunk(z_ref[:, :D_SSM], z_ref[:, D_SSM:], hr, hi, *[r[...] for r in p_refs])
        y_ref[...] = y
        hr_s[...] = nhr
        hi_s[...] = nhi

    return pl.pallas_call(
        body, name=name, grid=(nb, nc),
        in_specs=[pl.BlockSpec((S5_CHUNK, 2 * D_SSM), lambda b, c: (b * nc + c, 0))]
        + [_full(s) for s in _S5_PARAM_SHAPES],
        out_specs=[pl.BlockSpec((S5_CHUNK, D_SSM), lambda b, c: (b * nc + c, 0)),
                   pl.BlockSpec((1, 1, 2 * N_STATE), lambda b, c: (b * nc + c, 0, 0))],
        out_shape=[_sds((t, D_SSM)), _sds((nb * nc, 1, 2 * N_STATE))],
        scratch_shapes=[pltpu.VMEM((1, N_STATE), f32), pltpu.VMEM((1, N_STATE), f32)],
        compiler_params=_cp(2, VMEM_BIG),
    )(z, *params)


def s5_bwd(z, params, hs, dy, nb, name):
    t = z.shape[0]
    nc = t // nb // S5_CHUNK
    npar = len(_S5_PARAM_SHAPES)

    def body(z_ref, hs_ref, dy_ref, *rest):
        p_refs, dz_ref, dp_refs, (dhr_s, dhi_s) = rest[:npar], rest[npar], rest[npar + 1:2 * npar + 1], rest[2 * npar + 1:]

        @pl.when(pl.program_id(1) == 0)
        def _():
            dhr_s[...] = jnp.zeros_like(dhr_s)
            dhi_s[...] = jnp.zeros_like(dhi_s)

        prim = (z_ref[:, :D_SSM], z_ref[:, D_SSM:], hs_ref[0, :, :N_STATE], hs_ref[0, :, N_STATE:]) + tuple(
            r[...] for r in p_refs)
        _, vj = jax.vjp(_s5_chunk, *prim)
        cts = vj((dy_ref[...], dhr_s[...], dhi_s[...]))
        dz_ref[:, :D_SSM] = cts[0].astype(_MXU)
        dz_ref[:, D_SSM:] = cts[1].astype(_MXU)
        dhr_s[...] = cts[2]
        dhi_s[...] = cts[3]
        first = (pl.program_id(0) == 0) & (pl.program_id(1) == 0)
        for r, v in zip(dp_refs, cts[4:]):
            _acc(r, v, first)

    rev = lambda b, c: (b * nc + nc - 1 - c, 0)
    return pl.pallas_call(
        body, name=name, grid=(nb, nc),
        in_specs=[pl.BlockSpec((S5_CHUNK, 2 * D_SSM), rev),
                  pl.BlockSpec((1, 1, 2 * N_STATE), lambda b, c: (b * nc + nc - 1 - c, 0, 0)),
                  pl.BlockSpec((S5_CHUNK, D_SSM), rev)] + [_full(s) for s in _S5_PARAM_SHAPES],
        out_specs=[pl.BlockSpec((S5_CHUNK, 2 * D_SSM), rev)] + [_full(s) for s in _S5_PARAM_SHAPES],
        out_shape=[_sds((t, 2 * D_SSM), _MXU)] + [_sds(s) for s in _S5_PARAM_SHAPES],
        scratch_shapes=[pltpu.VMEM((1, N_STATE), f32), pltpu.VMEM((1, N_STATE), f32)],
        compiler_params=_cp(2, VMEM_BIG),
    )(z, hs, dy, *params)


DN_PRE_ROWS = 256
DN_COLS = 3 * D_DN // LANE


def dn_pre_fwd(zq, convw, seq, name):
    t, tb = zq.shape[0], DN_PRE_ROWS
    per_seq = seq // tb

    def body(xc_ref, xp_ref, w_ref, o_ref):
        is_start = pl.program_id(0) % per_seq == 0
        for j in range(DN_COLS):
            cols = slice(j * LANE, (j + 1) * LANE)
            o_ref[:, cols] = _dn_pre(xc_ref[:, cols], xp_ref[:, cols], w_ref[0:1, cols], w_ref[1:2, cols],
                                     w_ref[2:3, cols], w_ref[3:4, cols], is_start, j)

    return pl.pallas_call(
        body, name=name, grid=(t // tb,),
        in_specs=[_rows(tb, 3 * D_DN), pl.BlockSpec((tb, 3 * D_DN), lambda i: (jnp.maximum(i - 1, 0), 0)),
                  _full((DN_CONV, 3 * D_DN))],
        out_specs=_rows(tb, 3 * D_DN),
        out_shape=_sds((t, 3 * D_DN)),
        compiler_params=_cp(1, VMEM_BIG),
    )(zq, zq, convw)


def dn_pre_bwd(zq, convw, dqkv, seq, name):
    t, tb = zq.shape[0], DN_PRE_ROWS
    nrow = t // tb
    per_seq = seq // tb

    def body(xc_ref, xp_ref, w_ref, d_ref, dx_ref, dw_ref, carry):
        step = pl.program_id(0)
        i = nrow - 1 - step

        @pl.when(step == 0)
        def _():
            carry[...] = jnp.zeros_like(carry)

        for j in range(DN_COLS):
            cols = slice(j * LANE, (j + 1) * LANE)
            fn = functools.partial(_dn_pre, is_start=i % per_seq == 0, col=j)
            _, vj = jax.vjp(fn, xc_ref[:, cols], xp_ref[:, cols], w_ref[0:1, cols], w_ref[1:2, cols],
                            w_ref[2:3, cols], w_ref[3:4, cols])
            dxc, dxp, dw0, dw1, dw2, dw3 = vj(d_ref[:, cols])
            dx_ref[:, cols] = (dxc + carry[:, cols]).astype(_MXU)
            carry[:, cols] = dxp
            for k, dw in enumerate((dw0, dw1, dw2, dw3)):
                @pl.when(step == 0)
                def _():
                    dw_ref[k:k + 1, cols] = dw

                @pl.when(step != 0)
                def _():
                    dw_ref[k:k + 1, cols] += dw

    rev = lambda s: (nrow - 1 - s, 0)
    return pl.pallas_call(
        body, name=name, grid=(nrow,),
        in_specs=[pl.BlockSpec((tb, 3 * D_DN), rev),
                  pl.BlockSpec((tb, 3 * D_DN), lambda s: (jnp.maximum(nrow - 2 - s, 0), 0)),
                  _full((DN_CONV, 3 * D_DN)), pl.BlockSpec((tb, 3 * D_DN), rev)],
        out_specs=[pl.BlockSpec((tb, 3 * D_DN), rev), _full((DN_CONV, 3 * D_DN))],
        out_shape=[_sds((t, 3 * D_DN), _MXU), _sds((DN_CONV, 3 * D_DN))],
        scratch_shapes=[pltpu.VMEM((tb, 3 * D_DN), f32)],
        compiler_params=_cp(1, VMEM_BIG),
    )(zq, zq, convw, dqkv)


DN_LOCAL_CHUNKS = 2
DN_ATTN = DN_HEADS * DN_CHUNK


def _dn_heads(ref, rows, base=0):
    return [ref[rows, base + h * DN_HEAD_DIM:base + (h + 1) * DN_HEAD_DIM] for h in range(DN_HEADS)]


def dn_local_fwd(qkv, ab, alog, dtb, name):
    t = qkv.shape[0]
    c, n = DN_CHUNK, DN_LOCAL_CHUNKS

    def body(qkv_ref, ab_ref, alog_ref, dtb_ref, val_ref, kcd_ref, attn_ref, qd_ref, kd_ref, el_ref, inv_ref):
        rows = [pl.ds(j * c, c) for j in range(n)]
        vals, kcds, attns, qds, kds, els, invs = _dn_local(
            [_dn_heads(qkv_ref, r) for r in rows], [_dn_heads(qkv_ref, r, D_DN) for r in rows],
            [_dn_heads(qkv_ref, r, 2 * D_DN) for r in rows], [ab_ref[r, :] for r in rows], alog_ref[...], dtb_ref[...])
        for j, r in enumerate(rows):
            for h in range(DN_HEADS):
                lo, hi = h * DN_HEAD_DIM, (h + 1) * DN_HEAD_DIM
                val_ref[r, lo:hi] = vals[j][h]
                kcd_ref[r, lo:hi] = kcds[j][h].astype(_MXU)
                qd_ref[r, lo:hi] = qds[j][h].astype(_MXU)
                kd_ref[r, lo:hi] = kds[j][h].astype(_MXU)
                attn_ref[r, h * c:(h + 1) * c] = attns[j][h].astype(_MXU)
                inv_ref[r, h * c:(h + 1) * c] = invs[j][h]
            el_ref[j] = els[j]

    wide = _rows(n * c, D_DN)
    outs = pl.pallas_call(
        body, name=name, grid=(t // (n * c),),
        in_specs=[_rows(n * c, 3 * D_DN), _rows(n * c, LANE), _full((1, LANE)), _full((1, LANE))],
        out_specs=[wide, wide, _rows(n * c, DN_ATTN), wide, wide, pl.BlockSpec((n, 1, LANE), lambda i: (i, 0, 0)),
                   _rows(n * c, DN_ATTN)],
        out_shape=[_sds((t, D_DN)), _sds((t, D_DN), _MXU), _sds((t, DN_ATTN), _MXU), _sds((t, D_DN), _MXU),
                   _sds((t, D_DN), _MXU), _sds((t // c, 1, LANE)), _sds((t, DN_ATTN))],
        compiler_params=_cp(1),
    )(qkv, ab, alog, dtb)
    return outs[:6], outs[6]


def dn_local_bwd(qkv, ab, alog, dtb, inv, cts, name):
    t = qkv.shape[0]
    c, n = DN_CHUNK, DN_LOCAL_CHUNKS

    def body(qkv_ref, ab_ref, alog_ref, dtb_ref, inv_ref, dval_ref, dkcd_ref, dattn_ref, dqd_ref, dkd_ref, del_ref,
             dqkv_ref, dab_ref, dalog_ref, ddtb_ref):
        rows = [pl.ds(j * c, c) for j in range(n)]
        invs = [[inv_ref[r, h * c:(h + 1) * c] for h in range(DN_HEADS)] for r in rows]

        def local(qs, ks, vs, abs_, alog, dtb):
            return _dn_local(qs, ks, vs, abs_, alog, dtb, invs)[:6]

        _, vj = jax.vjp(local, [_dn_heads(qkv_ref, r) for r in rows], [_dn_heads(qkv_ref, r, D_DN) for r in rows],
                        [_dn_heads(qkv_ref, r, 2 * D_DN) for r in rows], [ab_ref[r, :] for r in rows], alog_ref[...],
                        dtb_ref[...])
        dattn = [[dattn_ref[r, h * c:(h + 1) * c] for h in range(DN_HEADS)] for r in rows]
        dq, dk, dv, dab, dalog, ddtb = vj(([_dn_heads(dval_ref, r) for r in rows], [_dn_heads(dkcd_ref, r) for r in rows],
                                           dattn, [_dn_heads(dqd_ref, r) for r in rows],
                                           [_dn_heads(dkd_ref, r) for r in rows], [del_ref[j] for j in range(n)]))
        for j, r in enumerate(rows):
            for h in range(DN_HEADS):
                lo, hi = h * DN_HEAD_DIM, (h + 1) * DN_HEAD_DIM
                dqkv_ref[r, lo:hi] = dq[j][h]
                dqkv_ref[r, D_DN + lo:D_DN + hi] = dk[j][h]
                dqkv_ref[r, 2 * D_DN + lo:2 * D_DN + hi] = dv[j][h]
            dab_ref[r, :] = dab[j].astype(_MXU)
        first = pl.program_id(0) == 0
        _acc(dalog_ref, dalog, first)
        _acc(ddtb_ref, ddtb, first)

    wide = _rows(n * c, D_DN)
    return pl.pallas_call(
        body, name=name, grid=(t // (n * c),),
        in_specs=[_rows(n * c, 3 * D_DN), _rows(n * c, LANE), _full((1, LANE)), _full((1, LANE)),
                  _rows(n * c, DN_ATTN), wide, wide, _rows(n * c, DN_ATTN), wide, wide,
                  pl.BlockSpec((n, 1, LANE), lambda i: (i, 0, 0))],
        out_specs=[_rows(n * c, 3 * D_DN), _rows(n * c, LANE), _full((1, LANE)), _full((1, LANE))],
        out_shape=[_sds((t, 3 * D_DN)), _sds((t, LANE), _MXU), _sds((1, LANE)), _sds((1, LANE))],
        compiler_params=_cp(1),
    )(qkv, ab, alog, dtb, inv, *cts)


def _seq_view(a, nb):
    return a.reshape((nb, a.shape[0] // nb) + a.shape[1:])


def _dn_chains(nb):
    return [(b, h) for b in range(nb) for h in range(DN_HEADS)]


DN_SCAN_CHUNKS = 4


def _dn_step_operands(val_ref, kcd_ref, attn_ref, qd_ref, kd_ref, el_ref, gg_ref, nb, j):
    chains = _dn_chains(nb)
    c = DN_CHUNK
    rows = pl.ds(j * c, c)

    def wide(ref):
        return [ref[b, rows, h * DN_HEAD_DIM:(h + 1) * DN_HEAD_DIM].astype(f32) for b, h in chains]

    attns = [attn_ref[b, rows, h * c:(h + 1) * c].astype(f32) for b, h in chains]
    return (wide(val_ref), wide(kcd_ref), attns, wide(qd_ref), wide(kd_ref),
            [el_ref[b, j, :, h:h + 1] for b, h in chains], wide(gg_ref))


def dn_scan_fwd(loc, gg, ng, nb, name):
    val, kcd, attn, qd, kd, el = loc
    t = val.shape[0]
    c, n = DN_CHUNK, DN_SCAN_CHUNKS
    nc = t // nb // c
    ns = nb * DN_HEADS

    def body(val_ref, kcd_ref, attn_ref, qd_ref, kd_ref, el_ref, gg_ref, ng_ref, y_ref, ss_ref, st):
        @pl.when(pl.program_id(0) == 0)
        def _():
            st[...] = jnp.zeros_like(st)

        sts = [st[i] for i in range(ns)]
        for j in range(n):
            for i in range(ns):
                ss_ref[j, i] = sts[i]
            ys, sts = _dn_step(*_dn_step_operands(val_ref, kcd_ref, attn_ref, qd_ref, kd_ref, el_ref, gg_ref, nb, j),
                               sts, ng_ref[...])
            for i, (b, h) in enumerate(_dn_chains(nb)):
                y_ref[b, pl.ds(j * c, c), h * DN_HEAD_DIM:(h + 1) * DN_HEAD_DIM] = ys[i]
        for i in range(ns):
            st[i] = sts[i]

    def blk(w):
        return pl.BlockSpec((nb, n * c, w), lambda k: (0, k, 0))

    el_spec = pl.BlockSpec((nb, n, 1, LANE), lambda k: (0, k, 0, 0))
    y, ss = pl.pallas_call(
        body, name=name, grid=(nc // n,),
        in_specs=[blk(D_DN), blk(D_DN), blk(DN_ATTN), blk(D_DN), blk(D_DN), el_spec, blk(D_DN), _full((1, LANE))],
        out_specs=[blk(D_DN), pl.BlockSpec((n, ns, DN_HEAD_DIM, DN_HEAD_DIM), lambda k: (k, 0, 0, 0))],
        out_shape=[_sds((nb, t // nb, D_DN)), _sds((nc, ns, DN_HEAD_DIM, DN_HEAD_DIM))],
        scratch_shapes=[pltpu.VMEM((ns, DN_HEAD_DIM, DN_HEAD_DIM), f32)],
        compiler_params=_cp(1, VMEM_BIG),
    )(_seq_view(val, nb), _seq_view(kcd, nb), _seq_view(attn, nb), _seq_view(qd, nb), _seq_view(kd, nb),
      el.reshape(nb, nc, 1, LANE), _seq_view(gg, nb), ng)
    return y.reshape(t, D_DN), ss


def dn_scan_bwd(loc, gg, ng, ss, dy, nb, name):
    val, kcd, attn, qd, kd, el = loc
    t = val.shape[0]
    c, n = DN_CHUNK, DN_SCAN_CHUNKS
    nc = t // nb // c
    ns = nb * DN_HEADS
    steps = nc // n

    def body(val_ref, kcd_ref, attn_ref, qd_ref, kd_ref, el_ref, gg_ref, ng_ref, ss_ref, dy_ref,
             dval_ref, dkcd_ref, dattn_ref, dqd_ref, dkd_ref, del_ref, dgg_ref, dng_ref, dst):
        @pl.when(pl.program_id(0) == 0)
        def _():
            dst[...] = jnp.zeros_like(dst)

        lane = lax.broadcasted_iota(jnp.int32, (1, LANE), 1)
        chains = _dn_chains(nb)
        ds = [dst[i] for i in range(ns)]
        dng_tot = jnp.zeros((1, LANE), f32)
        for j in reversed(range(n)):
            rows = pl.ds(j * c, c)
            _, vj = jax.vjp(_dn_step,
                            *_dn_step_operands(val_ref, kcd_ref, attn_ref, qd_ref, kd_ref, el_ref, gg_ref, nb, j),
                            [ss_ref[j, i] for i in range(ns)], ng_ref[...])
            dys = [dy_ref[b, rows, h * DN_HEAD_DIM:(h + 1) * DN_HEAD_DIM] for b, h in chains]
            dval, dkcd, dattn, dqd, dkd, dlast, dgg, ds, dng = vj((dys, ds))
            dng_tot = dng_tot + dng
            del_rows = [jnp.zeros((1, LANE), f32) for _ in range(nb)]
            for i, (b, h) in enumerate(chains):
                cols = slice(h * DN_HEAD_DIM, (h + 1) * DN_HEAD_DIM)
                dval_ref[b, rows, cols] = dval[i]
                dkcd_ref[b, rows, cols] = dkcd[i]
                dattn_ref[b, rows, h * c:(h + 1) * c] = dattn[i]
                dqd_ref[b, rows, cols] = dqd[i]
                dkd_ref[b, rows, cols] = dkd[i]
                dgg_ref[b, rows, cols] = dgg[i].astype(_MXU)
                del_rows[b] = del_rows[b] + jnp.where(lane == h, dlast[i], 0.0)
            for b in range(nb):
                del_ref[b, j] = del_rows[b]
        for i in range(ns):
            dst[i] = ds[i]
        _acc(dng_ref, dng_tot, pl.program_id(0) == 0)

    def blk(w):
        return pl.BlockSpec((nb, n * c, w), lambda k: (0, steps - 1 - k, 0))

    el_spec = pl.BlockSpec((nb, n, 1, LANE), lambda k: (0, steps - 1 - k, 0, 0))
    outs = pl.pallas_call(
        body, name=name, grid=(steps,),
        in_specs=[blk(D_DN), blk(D_DN), blk(DN_ATTN), blk(D_DN), blk(D_DN), el_spec, blk(D_DN), _full((1, LANE)),
                  pl.BlockSpec((n, ns, DN_HEAD_DIM, DN_HEAD_DIM), lambda k: (steps - 1 - k, 0, 0, 0)), blk(D_DN)],
        out_specs=[blk(D_DN), blk(D_DN), blk(DN_ATTN), blk(D_DN), blk(D_DN), el_spec, blk(D_DN), _full((1, LANE))],
        out_shape=[_sds((nb, t // nb, D_DN)), _sds((nb, t // nb, D_DN)), _sds((nb, t // nb, DN_ATTN)),
                   _sds((nb, t // nb, D_DN)), _sds((nb, t // nb, D_DN)), _sds((nb, nc, 1, LANE)),
                   _sds((nb, t // nb, D_DN), _MXU), _sds((1, LANE))],
        scratch_shapes=[pltpu.VMEM((ns, DN_HEAD_DIM, DN_HEAD_DIM), f32)],
        compiler_params=_cp(1, VMEM_BIG),
    )(_seq_view(val, nb), _seq_view(kcd, nb), _seq_view(attn, nb), _seq_view(qd, nb), _seq_view(kd, nb),
      el.reshape(nb, nc, 1, LANE), _seq_view(gg, nb), ng, ss, _seq_view(dy, nb))
    dloc = [o.reshape((t,) + o.shape[2:]) for o in outs[:5]] + [outs[5].reshape(t // c, 1, LANE)]
    return dloc, outs[6].reshape(t, D_DN), outs[7]


SG_ROWS = 512


def sg_fwd(z, lng, lnb, w, bt, name):
    t = z.shape[0]

    def body(z_ref, lng_ref, lnb_ref, w_ref, bt_ref, y_ref):
        ws = [w_ref[h] for h in range(SG_HEADS)]
        for k in range(SG_ROWS // SG_CHUNK):
            r = pl.ds(k * SG_CHUNK, SG_CHUNK)
            y_ref[r, :] = _sg_chunk(z_ref[r, :D_SG], z_ref[r, D_SG:2 * D_SG], z_ref[r, 2 * D_SG:], lng_ref[...],
                                    lnb_ref[...], ws, bt_ref[...])

    return pl.pallas_call(
        body, name=name, grid=(t // SG_ROWS,),
        in_specs=[_rows(SG_ROWS, 3 * D_SG), _full((1, D_SG)), _full((1, D_SG)),
                  _full((SG_HEADS, SG_CHUNK, SG_CHUNK)), _full((SG_CHUNK, LANE))],
        out_specs=_rows(SG_ROWS, D_SG),
        out_shape=_sds((t, D_SG)),
        compiler_params=_cp(1),
    )(z, lng, lnb, w, bt)


def sg_bwd(z, lng, lnb, w, bt, dy, name):
    t = z.shape[0]

    def body(z_ref, lng_ref, lnb_ref, w_ref, bt_ref, dy_ref, dz_ref, dlng_ref, dlnb_ref, dw_ref, dbt_ref):
        ws = [w_ref[h] for h in range(SG_HEADS)]
        tot = None
        for k in range(SG_ROWS // SG_CHUNK):
            r = pl.ds(k * SG_CHUNK, SG_CHUNK)
            _, vj = jax.vjp(_sg_chunk, z_ref[r, :D_SG], z_ref[r, D_SG:2 * D_SG], z_ref[r, 2 * D_SG:], lng_ref[...],
                            lnb_ref[...], ws, bt_ref[...])
            du, dv, dgate, dlng, dlnb, dws, dbt = vj(dy_ref[r, :])
            dz_ref[r, :D_SG] = du.astype(_MXU)
            dz_ref[r, D_SG:2 * D_SG] = dv.astype(_MXU)
            dz_ref[r, 2 * D_SG:] = dgate.astype(_MXU)
            part = [dlng, dlnb, dbt] + list(dws)
            tot = part if tot is None else [a + b for a, b in zip(tot, part)]
        first = pl.program_id(0) == 0
        _acc(dlng_ref, tot[0], first)
        _acc(dlnb_ref, tot[1], first)
        _acc(dbt_ref, tot[2], first)
        for h in range(SG_HEADS):
            @pl.when(first)
            def _():
                dw_ref[h] = tot[3 + h]

            @pl.when(jnp.logical_not(first))
            def _():
                dw_ref[h] += tot[3 + h]

    return pl.pallas_call(
        body, name=name, grid=(t // SG_ROWS,),
        in_specs=[_rows(SG_ROWS, 3 * D_SG), _full((1, D_SG)), _full((1, D_SG)),
                  _full((SG_HEADS, SG_CHUNK, SG_CHUNK)), _full((SG_CHUNK, LANE)), _rows(SG_ROWS, D_SG)],
        out_specs=[_rows(SG_ROWS, 3 * D_SG), _full((1, D_SG)), _full((1, D_SG)),
                   _full((SG_HEADS, SG_CHUNK, SG_CHUNK)), _full((SG_CHUNK, LANE))],
        out_shape=[_sds((t, 3 * D_SG), _MXU), _sds((1, D_SG)), _sds((1, D_SG)), _sds((SG_HEADS, SG_CHUNK, SG_CHUNK)),
                   _sds((SG_CHUNK, LANE))],
        compiler_params=_cp(1),
    )(z, lng, lnb, w, bt, dy)


def add_pairs(a_list, b_list, name):
    n = len(a_list)

    def body(*refs):
        for a_ref, b_ref, o_ref in zip(refs[:n], refs[n:2 * n], refs[2 * n:]):
            o_ref[...] = (a_ref[...].astype(f32) + b_ref[...].astype(f32)).astype(o_ref.dtype)

    return pl.pallas_call(
        body, name=name, out_shape=[_sds(a.shape, a.dtype) for a in a_list],
        compiler_params=pltpu.CompilerParams(vmem_limit_bytes=VMEM_BIG),
    )(*a_list, *b_list)


def sum_adamw(half, recv, w, m, v, name):
    _, r, c = w.shape
    tr = 256 if r % 256 == 0 else r

    def body(half_ref, recv_ref, w_ref, m_ref, v_ref, g_ref, d_ref, nm_ref, nv_ref):
        g = recv_ref[0].astype(f32)
        for k in range(1, N_CHIPS):
            g = g + recv_ref[k].astype(f32)
        wv = w_ref[...]
        nm = ADAM_B1 * m_ref[...] + (1.0 - ADAM_B1) * g
        nv = ADAM_B2 * v_ref[...] + (1.0 - ADAM_B2) * jnp.square(g)
        m_hat = nm / (1.0 - ADAM_B1 ** ADAM_STEP)
        v_hat = nv / (1.0 - ADAM_B2 ** ADAM_STEP)
        g_ref[...] = g
        d_ref[...] = -ADAM_LR * (m_hat / (jnp.sqrt(v_hat) + ADAM_EPS) + ADAM_WD * wv)
        nm_ref[...] = nm
        nv_ref[...] = nv

    own = pl.BlockSpec((None, tr, c), lambda i, h: (h[0], i, 0))
    return pl.pallas_call(
        body, name=name,
        grid_spec=pltpu.PrefetchScalarGridSpec(
            num_scalar_prefetch=1, grid=(r // tr,),
            in_specs=[pl.BlockSpec((N_CHIPS, tr, c), lambda i, h: (0, i, 0)), own, own, own], out_specs=[own] * 4),
        out_shape=[_sds((2, r, c))] * 4,
        compiler_params=_cp(1, VMEM_BIG),
    )(half, recv, w, m, v)


def sum_adamw_small(chip, parts, sums, ws, ms, vs, name):
    n = len(ws)

    def body(chip_ref, *refs):
        ins, outs = refs[:5 * n], refs[5 * n:]
        for k in range(n):
            part, own, w_ref, m_ref, v_ref = ins[k], ins[n + k], ins[2 * n + k], ins[3 * n + k], ins[4 * n + k]
            g = jnp.where(chip_ref[0] == 0, own[...], part[0])
            for q in range(1, N_CHIPS):
                g = g + jnp.where(chip_ref[0] == q, own[...], part[q])
            nm = ADAM_B1 * m_ref[...] + (1.0 - ADAM_B1) * g
            nv = ADAM_B2 * v_ref[...] + (1.0 - ADAM_B2) * jnp.square(g)
            m_hat = nm / (1.0 - ADAM_B1 ** ADAM_STEP)
            v_hat = nv / (1.0 - ADAM_B2 ** ADAM_STEP)
            outs[k][...] = g
            outs[n + k][...] = -ADAM_LR * (m_hat / (jnp.sqrt(v_hat) + ADAM_EPS) + ADAM_WD * w_ref[...])
            outs[2 * n + k][...] = nm
            outs[3 * n + k][...] = nv

    vmem = pl.BlockSpec(memory_space=pltpu.VMEM)
    outs = pl.pallas_call(
        body, name=name, in_specs=[pl.BlockSpec(memory_space=pltpu.SMEM)] + [vmem] * (5 * n),
        out_specs=[vmem] * (4 * n), out_shape=[_sds(w.shape) for w in ws] * 4,
        compiler_params=pltpu.CompilerParams(vmem_limit_bytes=VMEM_BIG),
    )(chip, *parts, *sums, *ws, *ms, *vs)
    return [outs[j * n:(j + 1) * n] for j in range(4)]


_ANY = pl.BlockSpec(memory_space=pltpu.HBM)
_MESH = pl.DeviceIdType.MESH


def _flip(v, bit):
    return 1 - v if bit else v


_CHIP_RELS = ((1, 0), (0, 1), (1, 1))


def _piece(ref, kind, q):
    if kind[0] == "slot":
        return ref.at[q]
    if kind[0] == "all":
        return ref
    _, axis, n = kind
    return ref.at[(slice(None),) * axis + (pl.ds(q * n, n),)]


def _piece_shape(shape, kind):
    if kind[0] == "slot":
        return tuple(shape[1:])
    if kind[0] == "all":
        return tuple(shape)
    _, axis, n = kind
    return tuple(shape[:axis]) + (n,) + tuple(shape[axis + 1:])


def gather_weights(shards, kinds, name):
    n = len(shards)

    def out_shape(s, kind):
        if kind[0] == "slot":
            return (N_CHIPS,) + tuple(s.shape)
        _, axis, w = kind
        return tuple(s.shape[:axis + 1]) + (N_CHIPS * w,) + tuple(s.shape[axis + 2:])

    def place(o_ref, kind, q, layer):
        if kind[0] == "slot":
            return o_ref.at[q, layer]
        return _piece(o_ref.at[layer], kind, q)

    def body(*refs):
        s_refs, o_refs = refs[:n], refs[n:2 * n]
        send_sems, recv_sems, fwd_send_sems, fwd_recv_sems = refs[2 * n:]
        x, y, c = lax.axis_index("x"), lax.axis_index("y"), lax.axis_index("c")
        mine = 2 * x + y
        sends, arrivals, forwards, fwd_arrivals = [], [], [], []
        for r, (fx, fy) in enumerate(_CHIP_RELS):
            px, py = _flip(x, fx), _flip(y, fy)
            peer = 2 * px + py
            for k in range(n):
                s = r * n + k
                sends.append(pltpu.make_async_remote_copy(
                    src_ref=s_refs[k].at[c], dst_ref=place(o_refs[k], kinds[k], mine, c), send_sem=send_sems.at[s],
                    recv_sem=recv_sems.at[s], device_id=(px, py, c), device_id_type=_MESH))
                arrivals.append(pltpu.make_async_remote_copy(
                    src_ref=s_refs[k].at[c], dst_ref=place(o_refs[k], kinds[k], peer, c), send_sem=send_sems.at[s],
                    recv_sem=recv_sems.at[s], device_id=(px, py, c), device_id_type=_MESH))
                block = place(o_refs[k], kinds[k], peer, c)
                forwards.append(pltpu.make_async_remote_copy(
                    src_ref=block, dst_ref=block, send_sem=fwd_send_sems.at[s], recv_sem=fwd_recv_sems.at[s],
                    device_id=(x, y, 1 - c), device_id_type=_MESH))
                other = place(o_refs[k], kinds[k], peer, 1 - c)
                fwd_arrivals.append(pltpu.make_async_remote_copy(
                    src_ref=other, dst_ref=other, send_sem=fwd_send_sems.at[s], recv_sem=fwd_recv_sems.at[s],
                    device_id=(x, y, 1 - c), device_id_type=_MESH))
        for cp in sends:
            cp.start()
        for arrived, fwd in zip(arrivals, forwards):
            arrived.wait_recv()
            fwd.start()
        for cp in fwd_arrivals:
            cp.wait_recv()
        for cp in sends + forwards:
            cp.wait_send()

    m = len(_CHIP_RELS) * n
    return pl.pallas_call(
        body, name=name, in_specs=[_ANY] * n, out_specs=[_ANY] * n,
        out_shape=[_sds(out_shape(s, k), s.dtype) for s, k in zip(shards, kinds)],
        scratch_shapes=[pltpu.SemaphoreType.DMA((m,))] * 4,
    )(*shards)


def _owned_by(owners, side):
    return [k for k, o in enumerate(owners) if o == side]


def exchange_halves(gs, smalls, owners, name):
    n, ns = len(gs), len(smalls)

    def body(*refs):
        g_refs, s_refs = refs[:n], refs[n:n + ns]
        got_refs, sgot_refs = refs[n + ns:2 * n + ns], refs[2 * n + ns:2 * (n + ns)]
        send_sems, recv_sems = refs[2 * (n + ns):]
        x, y, c = lax.axis_index("x"), lax.axis_index("y"), lax.axis_index("c")
        sibling = (x, y, 1 - c)
        swaps = [pltpu.make_async_remote_copy(
            src_ref=g_refs[k].at[1 - c], dst_ref=got_refs[k], send_sem=send_sems.at[k], recv_sem=recv_sems.at[k],
            device_id=sibling, device_id_type=_MESH) for k in range(n)]
        gives = [pltpu.make_async_remote_copy(
            src_ref=s_refs[k], dst_ref=sgot_refs[k], send_sem=send_sems.at[n + k], recv_sem=recv_sems.at[n + k],
            device_id=sibling, device_id_type=_MESH) for k in range(ns)]
        for cp in swaps:
            cp.start()
        for side in (0, 1):
            @pl.when(c == 1 - side)
            def _():
                for k in _owned_by(owners, side):
                    gives[k].start()
        for cp in swaps:
            cp.wait()
        for side in (0, 1):
            @pl.when(c == 1 - side)
            def _():
                for k in _owned_by(owners, side):
                    gives[k].wait_send()

            @pl.when(c == side)
            def _():
                for k in _owned_by(owners, side):
                    gives[k].wait_recv()

    outs = pl.pallas_call(
        body, name=name, in_specs=[_ANY] * (n + ns), out_specs=[_ANY] * (n + ns),
        out_shape=[_sds(g.shape[1:], g.dtype) for g in gs] + [_sds(s.shape, s.dtype) for s in smalls],
        scratch_shapes=[pltpu.SemaphoreType.DMA((n + ns,)), pltpu.SemaphoreType.DMA((n + ns,))],
    )(*gs, *smalls)
    return outs[:n], outs[n:]


def reduce_to_chips(ts, kinds, smalls, owners, name):
    n, ns = len(ts), len(smalls)

    def body(*refs):
        t_refs, s_refs = refs[:n], refs[n:n + ns]
        o_refs, so_refs = refs[n + ns:2 * n + ns], refs[2 * n + ns:2 * (n + ns)]
        send_sems, recv_sems = refs[2 * (n + ns):]
        x, y, c = lax.axis_index("x"), lax.axis_index("y"), lax.axis_index("c")
        mine = 2 * x + y
        sends, arrivals, small_sends, small_arrivals = [], [], [], []
        for r, (fx, fy) in enumerate(_CHIP_RELS):
            px, py = _flip(x, fx), _flip(y, fy)
            peer = 2 * px + py
            for k in range(n + ns):
                s = r * (n + ns) + k
                if k < n:
                    src, dst = _piece(t_refs[k], kinds[k], peer), o_refs[k]
                else:
                    src, dst = s_refs[k - n], so_refs[k - n]
                go = pltpu.make_async_remote_copy(
                    src_ref=src, dst_ref=dst.at[mine], send_sem=send_sems.at[s], recv_sem=recv_sems.at[s],
                    device_id=(px, py, c), device_id_type=_MESH)
                come = pltpu.make_async_remote_copy(
                    src_ref=src, dst_ref=dst.at[peer], send_sem=send_sems.at[s], recv_sem=recv_sems.at[s],
                    device_id=(px, py, c), device_id_type=_MESH)
                (sends if k < n else small_sends).append(go)
                (arrivals if k < n else small_arrivals).append(come)

        def owned(copies, side):
            return [cp for j, cp in enumerate(copies) if owners[j % ns] == side]

        for cp in sends:
            cp.start()
        for side in (0, 1):
            @pl.when(c == side)
            def _():
                for cp in owned(small_sends, side):
                    cp.start()
        for cp in arrivals:
            cp.wait_recv()
        for cp in sends:
            cp.wait_send()
        for side in (0, 1):
            @pl.when(c == side)
            def _():
                for cp in owned(small_arrivals, side):
                    cp.wait_recv()
                for cp in owned(small_sends, side):
                    cp.wait_send()

    m = len(_CHIP_RELS) * (n + ns)
    outs = pl.pallas_call(
        body, name=name, in_specs=[_ANY] * (n + ns), out_specs=[_ANY] * (n + ns),
        out_shape=[_sds((N_CHIPS,) + _piece_shape(t.shape, k), t.dtype) for t, k in zip(ts, kinds)]
        + [_sds((N_CHIPS,) + s.shape, s.dtype) for s in smalls],
        scratch_shapes=[pltpu.SemaphoreType.DMA((m,)), pltpu.SemaphoreType.DMA((m,))],
    )(*ts, *smalls)
    return outs[:n], outs[n:]


def share_halves(rs, smalls, owners, name):
    n, ns = len(rs), len(smalls)

    def body(*refs):
        s_refs, o_refs, so_refs = refs[n:n + ns], refs[n + ns:2 * n + ns], refs[2 * n + ns:2 * (n + ns)]
        send_sems, recv_sems, local_sems = refs[2 * (n + ns):]
        x, y, c = lax.axis_index("x"), lax.axis_index("y"), lax.axis_index("c")
        sibling = (x, y, 1 - c)
        swaps = [pltpu.make_async_remote_copy(
            src_ref=o_refs[k].at[c], dst_ref=o_refs[k].at[c], send_sem=send_sems.at[k], recv_sem=recv_sems.at[k],
            device_id=sibling, device_id_type=_MESH) for k in range(n)]
        arrivals = [pltpu.make_async_remote_copy(
            src_ref=o_refs[k].at[c], dst_ref=o_refs[k].at[1 - c], send_sem=send_sems.at[k], recv_sem=recv_sems.at[k],
            device_id=sibling, device_id_type=_MESH) for k in range(n)]
        gives = [pltpu.make_async_remote_copy(
            src_ref=s_refs[k], dst_ref=so_refs[k], send_sem=send_sems.at[n + k], recv_sem=recv_sems.at[n + k],
            device_id=sibling, device_id_type=_MESH) for k in range(ns)]
        keeps = [pltpu.make_async_copy(s_refs[k], so_refs[k], local_sems.at[k]) for k in range(ns)]
        for cp in swaps:
            cp.start()
        for side in (0, 1):
            @pl.when(c == side)
            def _():
                for k in _owned_by(owners, side):
                    gives[k].start()
                    keeps[k].start()
        for cp in arrivals:
            cp.wait_recv()
        for cp in swaps:
            cp.wait_send()
        for side in (0, 1):
            @pl.when(c == side)
            def _():
                for k in _owned_by(owners, side):
                    gives[k].wait_send()
                    keeps[k].wait()

            @pl.when(c == 1 - side)
            def _():
                for k in _owned_by(owners, side):
                    gives[k].wait_recv()

    outs = pl.pallas_call(
        body, name=name, in_specs=[_ANY] * (n + ns), out_specs=[_ANY] * (n + ns),
        out_shape=[_sds(r.shape, r.dtype) for r in list(rs) + list(smalls)],
        input_output_aliases={k: k for k in range(n)},
        scratch_shapes=[pltpu.SemaphoreType.DMA((n + ns,)), pltpu.SemaphoreType.DMA((n + ns,)),
                        pltpu.SemaphoreType.DMA((ns,))],
    )(*rs, *smalls)
    return outs[:n], outs[n:]


def _small_view(a):
    if a.size < 8 * LANE:
        return jnp.pad(a.reshape(-1), (0, 8 * LANE - a.size)).reshape(8, LANE)
    if a.ndim == 1:
        return a.reshape(1, a.shape[0])
    if a.ndim == 4 and a.shape[-1] < LANE:
        return a.reshape(a.shape[0], a.shape[1], a.shape[2] * a.shape[3])
    return a


def _permuted_from_shards(shards):
    parts = []
    for lo, hi in GROUP_COLS:
        for q in range(N_CHIPS):
            a, b = max(lo, q * SHARD_COLS), min(hi, (q + 1) * SHARD_COLS)
            if a < b:
                parts.append(shards[q][..., a - q * SHARD_COLS:b - q * SHARD_COLS])
    pad = jnp.zeros(shards[0].shape[:-1] + (D_IN_PAD - D_IN,), shards[0].dtype)
    return jnp.concatenate(parts + [pad], axis=-1)


def _shards_from_groups(groups):
    in_order = sorted(range(len(GROUP_COLS)), key=lambda j: GROUP_COLS[j][0])
    shards = []
    for q in range(N_CHIPS):
        parts = []
        for j in in_order:
            lo, hi = GROUP_COLS[j]
            a, b = max(lo, q * SHARD_COLS), min(hi, (q + 1) * SHARD_COLS)
            if a < b:
                parts.append(groups[j][..., a - lo:b - lo])
        shards.append(jnp.concatenate(parts, axis=-1))
    return shards


def _expand_b(b):
    eye = jnp.eye(SSM_GROUPS, dtype=b.dtype)
    return jnp.einsum("gnc,gh->gchn", b, eye).reshape(D_SSM, N_STATE)


def _extract_b(e):
    return jnp.einsum("gcgn->gnc", e.reshape(SSM_GROUPS, SSM_GROUP, SSM_GROUPS, SSM_STATE))


def _expand_c(c):
    eye = jnp.eye(SSM_GROUPS, dtype=c.dtype)
    return jnp.einsum("gcn,gh->gnhc", c, eye).reshape(N_STATE, D_SSM)


def _extract_c(e):
    return jnp.einsum("gngc->gcn", e.reshape(SSM_GROUPS, SSM_STATE, SSM_GROUPS, SSM_GROUP))


def _lane_row(v):
    return jnp.pad(v, (0, LANE - v.shape[0])).reshape(1, LANE)


def _layer_params(w, l):
    return dict(
        norm_g=w["norm_g"][l][None], win=w["w_in_perm"][l], wout=w["w_out"][l].astype(_MXU),
        pg=w["ple_norm_g"][l][None], wgate=w["w_ple_gate"][l].astype(_MXU), wple=w["w_ple"][l].astype(_MXU),
        are=w["ssm_a_re"][l].reshape(1, N_STATE), aim=w["ssm_a_im"][l].reshape(1, N_STATE),
        ls=jnp.repeat(w["ssm_log_step"][l], SSM_STATE).reshape(1, N_STATE),
        bre=_expand_b(w["ssm_b_re"][l]), bim=_expand_b(w["ssm_b_im"][l]),
        cr=_expand_c(w["ssm_c_re"][l]), ci=_expand_c(w["ssm_c_im"][l]),
        dr=w["ssm_d"][l].reshape(1, D_SSM), wglu=w["ssm_w_glu"][l].astype(f32), bglu=w["ssm_b_glu"][l][None],
        convw=w["dn_conv_w"][l], alog=_lane_row(w["dn_a_log"][l]), dtb=_lane_row(w["dn_dt_bias"][l]),
        ng=w["dn_norm_g"][l][None],
        lng=w["sg_ln_g"][l][None], lnb=w["sg_ln_b"][l][None], sgw=w["sg_w"][l],
        bt=jnp.pad(w["sg_b"][l].T, ((0, 0), (0, LANE - SG_HEADS))),
    )


def _layer_fwd(x, p, lp, nb, tag):
    seq = x.shape[0] // nb
    h, zs, zq, zg, zsg, zab = in_fwd(x, lp["norm_g"], lp["win"], f"in_fwd{tag}")
    prep = s5_prep_fwd(lp["are"], lp["aim"], lp["ls"], lp["bre"], lp["bim"], f"s5_prep_fwd{tag}")
    s5p = tuple(prep) + (lp["cr"], lp["ci"], lp["dr"], lp["wglu"], lp["bglu"])
    ys, hs = s5_fwd(zs, s5p, nb, f"s5_fwd{tag}")
    qkv = dn_pre_fwd(zq, lp["convw"], seq, f"dn_pre_fwd{tag}")
    loc, inv = dn_local_fwd(qkv, zab, lp["alog"], lp["dtb"], f"dn_local_fwd{tag}")
    yd, ss = dn_scan_fwd(loc, zg, lp["ng"], nb, f"dn_scan_fwd{tag}")
    yg = sg_fwd(zsg, lp["lng"], lp["lnb"], lp["sgw"], lp["bt"], f"sg_fwd{tag}")
    x2, x1, y, hn = post_fwd(x, ys, yd, yg, p, lp["wout"], lp["pg"], lp["wgate"], lp["wple"], f"post_fwd{tag}")
    saved = dict(x=x, h=h, zs=zs, zq=zq, zg=zg, zsg=zsg, zab=zab, s5p=s5p, hs=hs, qkv=qkv, loc=loc, inv=inv, ss=ss, x1=x1, y=y, hn=hn, p=p)
    return x2, saved


def _layer_bwd(dx2, sv, lp, nb, tag):
    seq = dx2.shape[0] // nb
    dx1, dgp, dpp, dys, dyd, dyg, dpg = post_bwd(dx2, sv["x1"], sv["hn"], sv["p"], lp["wout"], lp["pg"], lp["wgate"],
                                                 lp["wple"], f"post_bwd{tag}")
    g = {}
    g["w_out"] = wgrad(sv["y"], dx1, f"wgrad_out{tag}")
    g["w_ple_gate"] = wgrad(sv["hn"], dgp, f"wgrad_gate{tag}")
    g["w_ple"] = wgrad(sv["p"], dpp, f"wgrad_ple{tag}")
    g["ple_norm_g"] = dpg[0]
    dzsg, dlng, dlnb, dsgw, dbt = sg_bwd(sv["zsg"], lp["lng"], lp["lnb"], lp["sgw"], lp["bt"], dyg, f"sg_bwd{tag}")
    g["sg_ln_g"], g["sg_ln_b"], g["sg_w"], g["sg_b"] = dlng[0], dlnb[0], dsgw, dbt[:, :SG_HEADS].T
    dloc, dzg, dng = dn_scan_bwd(sv["loc"], sv["zg"], lp["ng"], sv["ss"], dyd, nb, f"dn_scan_bwd{tag}")
    dqkv, dzab, dalog, ddtb = dn_local_bwd(sv["qkv"], sv["zab"], lp["alog"], lp["dtb"], sv["inv"], dloc,
                                           f"dn_local_bwd{tag}")
    dzq, dconv = dn_pre_bwd(sv["zq"], lp["convw"], dqkv, seq, f"dn_pre_bwd{tag}")
    g["dn_conv_w"], g["dn_a_log"], g["dn_dt_bias"], g["dn_norm_g"] = dconv, dalog[0, :DN_HEADS], ddtb[0, :DN_HEADS], dng[0]
    s5out = s5_bwd(sv["zs"], sv["s5p"], sv["hs"], dys, nb, f"s5_bwd{tag}")
    dzs, dprep, (dcr, dci, ddr, dwglu, dbglu) = s5out[0], s5out[1:1 + S5_PREPARED], s5out[1 + S5_PREPARED:]
    dare, daim, dls, dbre, dbim = s5_prep_bwd(lp["are"], lp["aim"], lp["ls"], lp["bre"], lp["bim"], dprep,
                                              f"s5_prep_bwd{tag}")
    g["ssm_a_re"] = dare.reshape(SSM_GROUPS, SSM_STATE)
    g["ssm_a_im"] = daim.reshape(SSM_GROUPS, SSM_STATE)
    g["ssm_log_step"] = dls.reshape(SSM_GROUPS, SSM_STATE).sum(axis=1)
    g["ssm_b_re"], g["ssm_b_im"] = _extract_b(dbre), _extract_b(dbim)
    g["ssm_c_re"], g["ssm_c_im"] = _extract_c(dcr), _extract_c(dci)
    g["ssm_d"] = ddr.reshape(SSM_GROUPS, SSM_GROUP)
    g["ssm_w_glu"], g["ssm_b_glu"] = dwglu, dbglu[0]
    dzs_all = (dzs, dzq, dzg, dzsg, dzab)
    dx, dng_in = in_bwd(sv["x"], lp["norm_g"], lp["win"], dzs_all, dx1, f"in_bwd{tag}")
    g["w_in_pieces"] = [wgrad(sv["h"], dz, f"wgrad_in{k}{tag}") for k, dz in enumerate(dzs_all)]
    g["norm_g"] = dng_in[0]
    return dx, g


def _local_step(x, p, target, w, nb):
    lps = [_layer_params(w, l) for l in range(DEPTH)]
    saved = []
    for l in range(DEPTH):
        x, sv = _layer_fwd(x, p[l], lps[l], nb, f"_l{l}")
        saved.append(sv)
    loss_blk, dx, dfg = loss_fwd_bwd(x, w["final_norm_g"][None], target, "loss")
    grads = [None] * DEPTH
    for l in reversed(range(DEPTH)):
        dx, grads[l] = _layer_bwd(dx, saved[l], lps[l], nb, f"_l{l}")
    out = {k: jnp.stack([grads[l][k] for l in range(DEPTH)]) for k in grads[0] if k != "w_in_pieces"}
    out["w_in_pieces"] = [grads[l]["w_in_pieces"] for l in range(DEPTH)]
    out["final_norm_g"] = dfg[0]
    return loss_blk[0, 0], dx, out


def kernel(x, p, norm_g, w_in, ssm_a_re, ssm_a_im, ssm_b_re, ssm_b_im, ssm_c_re, ssm_c_im, ssm_d, ssm_log_step, ssm_w_glu, ssm_b_glu, dn_conv_w, dn_a_log, dn_dt_bias, dn_norm_g, sg_ln_g, sg_ln_b, sg_w, sg_b, w_out, ple_norm_g, w_ple_gate, w_ple, final_norm_g, loss_target, m_norm_g, m_w_in, m_ssm_a_re, m_ssm_a_im, m_ssm_b_re, m_ssm_b_im, m_ssm_c_re, m_ssm_c_im, m_ssm_d, m_ssm_log_step, m_ssm_w_glu, m_ssm_b_glu, m_dn_conv_w, m_dn_a_log, m_dn_dt_bias, m_dn_norm_g, m_sg_ln_g, m_sg_ln_b, m_sg_w, m_sg_b, m_w_out, m_ple_norm_g, m_w_ple_gate, m_w_ple, m_final_norm_g, v_norm_g, v_w_in, v_ssm_a_re, v_ssm_a_im, v_ssm_b_re, v_ssm_b_im, v_ssm_c_re, v_ssm_c_im, v_ssm_d, v_ssm_log_step, v_ssm_w_glu, v_ssm_b_glu, v_dn_conv_w, v_dn_a_log, v_dn_dt_bias, v_dn_norm_g, v_sg_ln_g, v_sg_ln_b, v_sg_w, v_sg_b, v_w_out, v_ple_norm_g, v_w_ple_gate, v_w_ple, v_final_norm_g):
    args = locals()
    w = {n: args[n] for n in WEIGHTS}
    m = {n: args["m_" + n] for n in WEIGHTS}
    v = {n: args["v_" + n] for n in WEIGHTS}
    nb, seq = x.shape[0], x.shape[1]
    t = nb * seq

    full = _gather_full(w)
    loss_local, dx, grads = _local_step(x.reshape(t, D_MODEL), p.reshape(DEPTH, t, D_PLE),
                                        loss_target.reshape(t, D_MODEL), full, nb)
    outs, loss = _reduce_and_update(grads, w, m, v, loss_local)
    return (loss, dx.reshape(nb, seq, D_MODEL), *[outs[0][n] for n in WEIGHTS], *[outs[1][n] for n in WEIGHTS],
            *[outs[2][n] for n in WEIGHTS], *[outs[3][n] for n in WEIGHTS])


def _gather_full(w):
    sh_names = [n for n, _ in SHARDED]
    shards = [w[n] if n == "dn_conv_w" else w[n].astype(_COMM) for n in sh_names]
    gathered = gather_weights(shards, [k for _, k in SHARDED], "gather_weights")
    chip = 2 * lax.axis_index("x") + lax.axis_index("y")
    full = {n: w[n] for n in REPLICATED}
    for (n, kind), shard, got in zip(SHARDED, shards, gathered):
        if kind[0] == "slot":
            full[n] = lax.dynamic_update_index_in_dim(got, shard, chip, 0)
        else:
            full[n] = lax.dynamic_update_slice_in_dim(got, shard, chip * kind[2], axis=kind[1] + 1)
    slots = full.pop("w_in")
    full["w_in_perm"] = _permuted_from_shards([slots[q] for q in range(N_CHIPS)]).astype(_MXU)
    return full


def _reduce_and_update(grads, w, m, v, loss_local):
    sh_names = [n for n, _ in SHARDED]
    sh_kinds = [k for _, k in SHARDED]
    sm_names = list(REPLICATED) + ["loss"]
    owners = [SMALL_OWNER[n] for n in sm_names]

    def small_views(d, last):
        return [_small_view(d[n]) for n in REPLICATED] + [last]

    no_state = jnp.zeros((8, LANE), f32)
    grads["w_in"] = jnp.stack([jnp.stack(_shards_from_groups(pieces)) for pieces in grads["w_in_pieces"]])
    gs = [grads[n] if n == "dn_conv_w" else grads[n].astype(_COMM) for n in sh_names]
    sm = small_views(grads, _small_view(loss_local.reshape(1)))
    core = lax.axis_index("c")
    chip = 2 * lax.axis_index("x") + lax.axis_index("y")
    got, sm_got = exchange_halves(gs, sm, owners, "exchange_halves")
    sums = add_pairs([lax.dynamic_index_in_dim(g, core, 0, keepdims=False) for g in gs] + sm, list(got) + list(sm_got),
                     "add_halves")
    sums, sm_sums = sums[:len(gs)], sums[len(gs):]
    parts, sm_parts = reduce_to_chips(sums, sh_kinds, sm_sums, owners, "reduce_to_chips")
    parts = list(parts)
    for k, (kind, total) in enumerate(zip(sh_kinds, sums)):
        if kind[0] == "slot":
            own = lax.dynamic_index_in_dim(total, chip, 0, keepdims=False)
        else:
            own = lax.dynamic_slice_in_dim(total, chip * kind[2], kind[2], axis=kind[1])
        parts[k] = lax.dynamic_update_index_in_dim(parts[k], own, chip, 0)
    half = core.astype(jnp.int32).reshape(1)
    results = []
    for n, part in zip(sh_names, parts):
        results += sum_adamw(half, part, w[n], m[n], v[n], f"adamw_{n}")
    sm_results = sum_adamw_small(chip.astype(jnp.int32).reshape(1), sm_parts, sm_sums, small_views(w, no_state),
                                 small_views(m, no_state), small_views(v, no_state), "adamw_replicated")
    flat_small = [r for group in sm_results for r in group]
    shared, sm_shared = share_halves(results, flat_small, owners * 4, "share_halves")
    outs = []
    for j in range(4):
        d = {n: shared[4 * k + j] for k, n in enumerate(sh_names)}
        for k, n in enumerate(REPLICATED):
            d[n] = sm_shared[j * len(sm_names) + k].reshape(-1)[:w[n].size].reshape(w[n].shape)
        outs.append(d)
    return outs, sm_shared[len(sm_names) - 1][0, 0]
```

```python
import functools

import jax
import jax.numpy as jnp
from jax import lax
from jax.experimental import pallas as pl
from jax.experimental.pallas import tpu as pltpu

f32 = jnp.float32
bf16 = jnp.bfloat16

_MXU = bf16
_COMM = bf16
HIGH = lax.Precision.HIGH

D_MODEL = 1024
DEPTH = 2
D_PLE = 256
D_SSM = 256
D_DN = 512
D_SG = 256
SSM_GROUPS = 16
SSM_GROUP = 16
SSM_STATE = 64
N_STATE = SSM_GROUPS * SSM_STATE
DN_HEADS = 4
DN_HEAD_DIM = 128
DN_CONV = 4
DN_HALO = 16
DN_CHUNK = 64
SG_HEADS = 4
SG_HEAD_DIM = 64
SG_CHUNK = 128
S5_CHUNK = 256
S5_GROUP_ROWS = 8
EPS = 1e-6
D_IN = 3336
D_IN_PAD = 3456
LANE = 128

ADAM_LR = 0.001
ADAM_B1 = 0.9
ADAM_B2 = 0.999
ADAM_EPS = 1e-08
ADAM_WD = 0.01
ADAM_STEP = 10

N_CHIPS = 4
N_DEV = 8

Z_COLS = ((0, 512), (512, 2048), (2048, 2560), (2560, 3328), (3328, 3456))

GROUP_COLS = ((0, 512), (512, 2048), (2056, 2568), (2568, 3336), (2048, 2056))
SHARD_COLS = D_IN // 4

SHARDED = (("w_in", ("slot",)), ("ssm_w_glu", ("win", 0, 64)), ("dn_conv_w", ("win", 1, 384)),
           ("w_out", ("win", 0, 256)), ("w_ple_gate", ("win", 0, 256)), ("w_ple", ("win", 1, 256)))
REPLICATED = ("norm_g", "ssm_a_re", "ssm_a_im", "ssm_b_re", "ssm_b_im", "ssm_c_re", "ssm_c_im", "ssm_d",
              "ssm_log_step", "ssm_b_glu", "dn_a_log", "dn_dt_bias", "dn_norm_g", "sg_ln_g", "sg_ln_b", "sg_w",
              "sg_b", "ple_norm_g", "final_norm_g")
SMALL_OWNER = {n: int(n.startswith("ssm_")) for n in REPLICATED + ("loss",)}
WEIGHTS = ("norm_g", "w_in", "ssm_a_re", "ssm_a_im", "ssm_b_re", "ssm_b_im", "ssm_c_re", "ssm_c_im", "ssm_d",
           "ssm_log_step", "ssm_w_glu", "ssm_b_glu", "dn_conv_w", "dn_a_log", "dn_dt_bias", "dn_norm_g", "sg_ln_g",
           "sg_ln_b", "sg_w", "sg_b", "w_out", "ple_norm_g", "w_ple_gate", "w_ple", "final_norm_g")

VMEM_BIG = 56 * 1024 * 1024


def _mm(a, b):
    return jnp.dot(a.astype(_MXU), b.astype(_MXU), preferred_element_type=f32)


def _mm_nt(a, b):
    return lax.dot_general(a.astype(_MXU), b.astype(_MXU), (((1,), (1,)), ((), ())), preferred_element_type=f32)


def _mm_tn(a, b):
    return lax.dot_general(a.astype(_MXU), b.astype(_MXU), (((0,), (0,)), ((), ())), preferred_element_type=f32)


@jax.custom_vjp
def bdot(a, b):
    return _mm(a, b)


def _bdot_fwd(a, b):
    return _mm(a, b), (a, b)


def _bdot_bwd(res, g):
    a, b = res
    return _mm_nt(g, b).astype(a.dtype), _mm_tn(a, g).astype(b.dtype)


bdot.defvjp(_bdot_fwd, _bdot_bwd)


@jax.custom_vjp
def bdot_nt(a, b):
    return _mm_nt(a, b)


def _bdot_nt_fwd(a, b):
    return _mm_nt(a, b), (a, b)


def _bdot_nt_bwd(res, g):
    a, b = res
    return _mm(g, b).astype(a.dtype), _mm_tn(g, a).astype(b.dtype)


bdot_nt.defvjp(_bdot_nt_fwd, _bdot_nt_bwd)


@jax.custom_vjp
def bdot_tn(a, b):
    return _mm_tn(a, b)


def _bdot_tn_fwd(a, b):
    return _mm_tn(a, b), (a, b)


def _bdot_tn_bwd(res, g):
    a, b = res
    return _mm_nt(b, g).astype(a.dtype), _mm(a, g).astype(b.dtype)


bdot_tn.defvjp(_bdot_tn_fwd, _bdot_tn_bwd)


def hdot(a, b):
    return jnp.dot(a, b, precision=HIGH, preferred_element_type=f32)


def _unit_lower_inverses(ms):
    n = ms[0].shape[0]
    eye = (lax.broadcasted_iota(jnp.int32, (n, n), 0) == lax.broadcasted_iota(jnp.int32, (n, n), 1)).astype(f32)
    pw = [-m for m in ms]
    inv = [eye + p for p in pw]
    for _ in range(n.bit_length() - 2):
        pw = [hdot(p, p) for p in pw]
        inv = [a + hdot(a, p) for a, p in zip(inv, pw)]
    return inv


@jax.custom_vjp
def solve_unit_lower(ms, rhs, inv):
    return [hdot(a, r) for a, r in zip(inv, rhs)]


def _solve_unit_lower_fwd(ms, rhs, inv):
    xs = [hdot(a, r) for a, r in zip(inv, rhs)]
    return xs, (inv, xs)


def _solve_unit_lower_bwd(res, gs):
    inv, xs = res
    d_rhs = [lax.dot_general(a, g, (((0,), (0,)), ((), ())), precision=HIGH, preferred_element_type=f32)
             for a, g in zip(inv, gs)]
    d_ms = [-lax.dot_general(d, x, (((1,), (1,)), ((), ())), precision=HIGH, preferred_element_type=f32)
            for d, x in zip(d_rhs, xs)]
    return d_ms, d_rhs, [jnp.zeros_like(a) for a in inv]


solve_unit_lower.defvjp(_solve_unit_lower_fwd, _solve_unit_lower_bwd)


@functools.partial(jax.custom_vjp, nondiff_argnums=(1,))
def roll_rows(x, k):
    return pltpu.roll(x, k, 0)


def _roll_rows_fwd(x, k):
    return pltpu.roll(x, k, 0), None


def _roll_rows_bwd(k, _, g):
    return (pltpu.roll(g, g.shape[0] - k, 0),)


roll_rows.defvjp(_roll_rows_fwd, _roll_rows_bwd)


def _row_ids(shape):
    return lax.broadcasted_iota(jnp.int32, shape, 0)


def _rms(x, g):
    return x * lax.rsqrt(jnp.mean(x * x, axis=-1, keepdims=True) + EPS) * g


def _layer_norm(x, g, b):
    mu = jnp.mean(x, axis=-1, keepdims=True)
    xc = x - mu
    return xc * lax.rsqrt(jnp.mean(xc * xc, axis=-1, keepdims=True) + EPS) * g + b


def _s5_prep(are, aim, ls, bre, bim):
    step = jnp.exp(ls)
    mag = jnp.exp(are * step)
    lr = mag * jnp.cos(aim * step)
    li = mag * jnp.sin(aim * step)
    den = are * are + aim * aim
    nr, ni = lr - 1.0, li
    fr = (nr * are + ni * aim) / den
    fi = (ni * are - nr * aim) / den
    bbr = fr * bre - fi * bim
    bbi = fr * bim + fi * bre
    pr = jnp.broadcast_to(lr, (S5_GROUP_ROWS, N_STATE))
    pi = jnp.broadcast_to(li, (S5_GROUP_ROWS, N_STATE))
    d = 1
    while d < S5_GROUP_ROWS:
        keep = _row_ids(pr.shape) >= d
        sr, si = roll_rows(pr, d), roll_rows(pi, d)
        pr, pi = jnp.where(keep, pr * sr - pi * si, pr), jnp.where(keep, pr * si + pi * sr, pi)
        d *= 2
    return pr, pi, bbr, bbi


def _s5_chunk(u, gate, hr, hi, pr, pi, bbr, bbi, cr, ci, dr, wglu, bglu):
    n, grp = u.shape[0], S5_GROUP_ROWS
    xr = bdot(u, bbr)
    xi = bdot(u, bbi)
    sub = _row_ids(xr.shape) % grp
    d = 1
    while d < grp:
        lr, li = pr[d - 1:d], pi[d - 1:d]
        sr = jnp.where(sub >= d, roll_rows(xr, d), 0.0)
        si = jnp.where(sub >= d, roll_rows(xi, d), 0.0)
        xr, xi = xr + lr * sr - li * si, xi + lr * si + li * sr
        d *= 2
    outs_r, outs_i = [], []
    for g in range(n // grp):
        gr, gi = xr[g * grp:(g + 1) * grp], xi[g * grp:(g + 1) * grp]
        gr, gi = gr + pr * hr - pi * hi, gi + pr * hi + pi * hr
        hr, hi = gr[grp - 1:grp], gi[grp - 1:grp]
        outs_r.append(gr)
        outs_i.append(gi)
    xr = jnp.concatenate(outs_r, axis=0)
    xi = jnp.concatenate(outs_i, axis=0)
    y = bdot(xr, cr) - bdot(xi, ci) + dr * u
    y = jax.nn.gelu(y)
    y = y * jax.nn.sigmoid(bdot(y, wglu) + bglu)
    return y * jax.nn.silu(gate), hr, hi


def _dn_pre(xc, xp, w0, w1, w2, w3, is_start, col):
    xp = jnp.where(is_start, 0.0, xp)
    halo_rows = _row_ids(xp.shape)
    acc = w3 * xc
    for d, w in ((1, w2), (2, w1), (3, w0)):
        r = roll_rows(xc, d)
        head = jnp.where(halo_rows >= d, r[:DN_HALO], roll_rows(xp, d))
        acc = acc + w * jnp.concatenate([head, r[DN_HALO:]], axis=0)
    y = jax.nn.silu(acc)
    nrm = y * lax.rsqrt(jnp.sum(y * y, axis=-1, keepdims=True) + EPS)
    nrm = nrm * jnp.where(col < DN_HEADS, DN_HEAD_DIM ** -0.5, 1.0)
    return jnp.where(col < 2 * DN_HEADS, nrm, y)


def _dn_local(qs, ks, vs, abs_, alog, dtb, invs=None):
    c = DN_CHUNK
    ri = lax.broadcasted_iota(jnp.int32, (c, c), 0)
    ci = lax.broadcasted_iota(jnp.int32, (c, c), 1)
    causal, strict = ri >= ci, ri > ci
    tril = causal.astype(f32)
    gcums = [hdot(tril, -jnp.exp(alog) * jax.nn.softplus(ab + dtb)) for ab in abs_]
    gcum_ts = [g.T for g in gcums]
    sigs = [jax.nn.sigmoid(ab) for ab in abs_]
    chains = [(j, h) for j in range(len(abs_)) for h in range(DN_HEADS)]
    gc = [gcums[j][:, h:h + 1] for j, h in chains]
    decay = [jnp.where(causal, jnp.exp(jnp.where(causal, gc[n] - gcum_ts[j][h:h + 1, :], 0.0)), 0.0)
             for n, (j, h) in enumerate(chains)]
    beta = [sigs[j][:, DN_HEADS + h:DN_HEADS + h + 1] for j, h in chains]
    kb = [ks[j][h] * beta[n] for n, (j, h) in enumerate(chains)]
    ms = [jnp.where(strict, bdot_nt(kb[n], ks[j][h]) * decay[n], 0.0) for n, (j, h) in enumerate(chains)]
    egc = [jnp.exp(g) for g in gc]
    rhs = [jnp.concatenate([vs[j][h] * beta[n], kb[n] * egc[n]], axis=1) for n, (j, h) in enumerate(chains)]
    inv = _unit_lower_inverses(ms) if invs is None else [invs[j][h] for j, h in chains]
    sol = solve_unit_lower(ms, rhs, inv)
    values = [s[:, :DN_HEAD_DIM] for s in sol]
    k_cds = [s[:, DN_HEAD_DIM:] for s in sol]
    attns = [bdot_nt(qs[j][h], ks[j][h]) * decay[n] for n, (j, h) in enumerate(chains)]
    q_decs = [qs[j][h] * egc[n] for n, (j, h) in enumerate(chains)]
    k_decs = [ks[j][h] * jnp.exp(gc[n][c - 1:c, :] - gc[n]) for n, (j, h) in enumerate(chains)]

    def nest(flat):
        return [flat[j * DN_HEADS:(j + 1) * DN_HEADS] for j in range(len(abs_))]

    lasts = [jnp.exp(g[c - 1:c, :]) for g in gcums]
    return nest(values), nest(k_cds), nest(attns), nest(q_decs), nest(k_decs), lasts, nest(inv)


def _dn_step(values, k_cds, attns, q_decs, k_decs, lasts, ggs, sts, ng):
    v_new = [v - bdot(kc, st) for v, kc, st in zip(values, k_cds, sts)]
    o = [bdot(qd, st) for qd, st in zip(q_decs, sts)]
    o = [a + bdot(at, vn) for a, at, vn in zip(o, attns, v_new)]
    new = [st * la + bdot_tn(kd, vn) for st, la, kd, vn in zip(sts, lasts, k_decs, v_new)]
    return [_rms(a, ng) * jax.nn.silu(g) for a, g in zip(o, ggs)], new


def _sg_chunk(u, v, gate, lng, lnb, ws, bt):
    n = SG_CHUNK
    ug = jax.nn.gelu(u)
    vn = _layer_norm(jax.nn.gelu(v), lng, lnb)
    causal = lax.broadcasted_iota(jnp.int32, (n, n), 0) >= lax.broadcasted_iota(jnp.int32, (n, n), 1)
    lane = lax.broadcasted_iota(jnp.int32, (n, D_SG), 1)
    s = jnp.zeros((n, D_SG), f32)
    for h in range(SG_HEADS):
        t = bdot(jnp.where(causal, ws[h], 0.0), vn) + bt[:, h:h + 1]
        s = s + jnp.where((lane >= h * SG_HEAD_DIM) & (lane < (h + 1) * SG_HEAD_DIM), t, 0.0)
    return ug * s * jax.nn.silu(gate)


def _cp(n_grid, vmem=None):
    return pltpu.CompilerParams(dimension_semantics=("arbitrary",) * n_grid, vmem_limit_bytes=vmem)


def _full(shape):
    nd = len(shape)
    return pl.BlockSpec(tuple(shape), lambda *_: (0,) * nd)


def _rows(tm, ncol):
    return pl.BlockSpec((tm, ncol), lambda i: (i, 0))


def _sds(shape, dtype=f32):
    return jax.ShapeDtypeStruct(tuple(shape), dtype)


def _acc(ref, val, first):
    @pl.when(first)
    def _():
        ref[...] = val

    @pl.when(jnp.logical_not(first))
    def _():
        ref[...] += val


def in_fwd(x, g, w, name):
    t, tm = x.shape[0], 256

    def body(x_ref, g_ref, w_ref, h_ref, *z_refs):
        h = _rms(x_ref[...], g_ref[...]).astype(_MXU)
        h_ref[...] = h
        for z_ref, (a, b) in zip(z_refs, Z_COLS):
            z_ref[...] = jnp.dot(h, w_ref[:, a:b], preferred_element_type=f32)

    widths = [b - a for a, b in Z_COLS]
    return pl.pallas_call(
        body, name=name, grid=(t // tm,),
        in_specs=[_rows(tm, D_MODEL), _full((1, D_MODEL)), _full((D_MODEL, D_IN_PAD))],
        out_specs=[_rows(tm, D_MODEL)] + [_rows(tm, n) for n in widths],
        out_shape=[_sds((t, D_MODEL), _MXU)] + [_sds((t, n)) for n in widths],
        compiler_params=_cp(1, VMEM_BIG),
    )(x, g, w)


def in_bwd(x, g, w, dzs, dres, name):
    t, tm = x.shape[0], 256
    widths = [b - a for a, b in Z_COLS]

    def body(x_ref, g_ref, w_ref, dres_ref, *rest):
        dz_refs, (dx_ref, dg_ref) = rest[:5], rest[5:]
        dh = jnp.zeros((tm, D_MODEL), f32)
        for dz_ref, (a, b) in zip(dz_refs, Z_COLS):
            dh = dh + _mm_nt(dz_ref[...], w_ref[:, a:b])
        _, vj = jax.vjp(_rms, x_ref[...], g_ref[...])
        dx, dg = vj(dh)
        dx_ref[...] = dres_ref[...] + dx
        _acc(dg_ref, dg, pl.program_id(0) == 0)

    return pl.pallas_call(
        body, name=name, grid=(t // tm,),
        in_specs=[_rows(tm, D_MODEL), _full((1, D_MODEL)), _full((D_MODEL, D_IN_PAD)), _rows(tm, D_MODEL)]
        + [_rows(tm, n) for n in widths],
        out_specs=[_rows(tm, D_MODEL), _full((1, D_MODEL))],
        out_shape=[_sds((t, D_MODEL)), _sds((1, D_MODEL))],
        compiler_params=_cp(1, VMEM_BIG),
    )(x, g, w, dres, *dzs)


def wgrad(a, g, name):
    t, k = a.shape
    n = g.shape[1]
    tm = min(t, 2048)
    tn = n if n <= 768 else (768 if n % 768 == 0 else 512)
    steps = t // tm

    def body(a_ref, g_ref, o_ref, acc):
        i = pl.program_id(1)
        _acc(acc, _mm_tn(a_ref[...], g_ref[...]), i == 0)

        @pl.when(i == steps - 1)
        def _():
            o_ref[...] = acc[...].astype(o_ref.dtype)

    return pl.pallas_call(
        body, name=name, grid=(n // tn, steps),
        in_specs=[pl.BlockSpec((tm, k), lambda j, i: (i, 0)), pl.BlockSpec((tm, tn), lambda j, i: (i, j))],
        out_specs=pl.BlockSpec((k, tn), lambda j, i: (0, j)),
        out_shape=_sds((k, n), _COMM),
        scratch_shapes=[pltpu.VMEM((k, tn), f32)],
        compiler_params=_cp(2, VMEM_BIG),
    )(a, g)


def post_fwd(x, ys, yd, yg, p, wout, pg, wgate, wple, name):
    t, tm = x.shape[0], 256

    def body(x_ref, ys_ref, yd_ref, yg_ref, p_ref, wout_ref, pg_ref, wgate_ref, wple_ref,
             x2_ref, x1_ref, y_ref, hn_ref):
        y = jnp.concatenate([ys_ref[...], yd_ref[...], yg_ref[...]], axis=1).astype(_MXU)
        y_ref[...] = y
        x1 = x_ref[...] + jnp.dot(y, wout_ref[...], preferred_element_type=f32)
        x1_ref[...] = x1
        hn = _rms(x1, pg_ref[...]).astype(_MXU)
        hn_ref[...] = hn
        gp = jnp.dot(hn, wgate_ref[...], preferred_element_type=f32)
        pp = _mm(p_ref[...], wple_ref[...])
        x2_ref[...] = x1 + jax.nn.sigmoid(gp) * pp

    return pl.pallas_call(
        body, name=name, grid=(t // tm,),
        in_specs=[_rows(tm, D_MODEL), _rows(tm, D_SSM), _rows(tm, D_DN), _rows(tm, D_SG), _rows(tm, D_PLE),
                  _full((D_MODEL, D_MODEL)), _full((1, D_MODEL)), _full((D_MODEL, D_MODEL)), _full((D_PLE, D_MODEL))],
        out_specs=[_rows(tm, D_MODEL)] * 4,
        out_shape=[_sds((t, D_MODEL)), _sds((t, D_MODEL)), _sds((t, D_MODEL), _MXU), _sds((t, D_MODEL), _MXU)],
        compiler_params=_cp(1, VMEM_BIG),
    )(x, ys, yd, yg, p, wout, pg, wgate, wple)


def post_bwd(dx2, x1, hn, p, wout, pg, wgate, wple, name):
    t, tm = dx2.shape[0], 256

    def body(dx2_ref, x1_ref, hn_ref, p_ref, wout_ref, pg_ref, wgate_ref, wple_ref,
             dx1_ref, dgp_ref, dpp_ref, dys_ref, dyd_ref, dyg_ref, dpg_ref):
        dx2 = dx2_ref[...]
        gp = jnp.dot(hn_ref[...], wgate_ref[...], preferred_element_type=f32)
        pp = _mm(p_ref[...], wple_ref[...])
        sg = jax.nn.sigmoid(gp)
        dpp_ref[...] = (dx2 * sg).astype(_MXU)
        dgp = (dx2 * pp * sg * (1.0 - sg)).astype(_MXU)
        dgp_ref[...] = dgp
        dhn = _mm_nt(dgp, wgate_ref[...])
        _, vj = jax.vjp(_rms, x1_ref[...], pg_ref[...])
        dx1n, dpg = vj(dhn)
        dx1 = dx2 + dx1n
        dx1_ref[...] = dx1
        dy = _mm_nt(dx1, wout_ref[...])
        dys_ref[...] = dy[:, :D_SSM]
        dyd_ref[...] = dy[:, D_SSM:D_SSM + D_DN]
        dyg_ref[...] = dy[:, D_SSM + D_DN:]
        _acc(dpg_ref, dpg, pl.program_id(0) == 0)

    return pl.pallas_call(
        body, name=name, grid=(t // tm,),
        in_specs=[_rows(tm, D_MODEL), _rows(tm, D_MODEL), _rows(tm, D_MODEL), _rows(tm, D_PLE),
                  _full((D_MODEL, D_MODEL)), _full((1, D_MODEL)), _full((D_MODEL, D_MODEL)), _full((D_PLE, D_MODEL))],
        out_specs=[_rows(tm, D_MODEL), _rows(tm, D_MODEL), _rows(tm, D_MODEL), _rows(tm, D_SSM), _rows(tm, D_DN),
                   _rows(tm, D_SG), _full((1, D_MODEL))],
        out_shape=[_sds((t, D_MODEL)), _sds((t, D_MODEL), _MXU), _sds((t, D_MODEL), _MXU), _sds((t, D_SSM)),
                   _sds((t, D_DN)), _sds((t, D_SG)), _sds((1, D_MODEL))],
        compiler_params=_cp(1, VMEM_BIG),
    )(dx2, x1, hn, p, wout, pg, wgate, wple)


def loss_fwd_bwd(x, fg, target, name):
    t, tm = x.shape[0], 512

    def body(x_ref, fg_ref, t_ref, loss_ref, dx_ref, dfg_ref):
        def f(xv, gv):
            err = _rms(xv, gv) - t_ref[...]
            return 0.5 * jnp.sum(jnp.mean(err * err, axis=-1))

        val, vj = jax.vjp(f, x_ref[...], fg_ref[...])
        dx, dfg = vj(jnp.ones((), f32))
        dx_ref[...] = dx
        first = pl.program_id(0) == 0
        _acc(dfg_ref, dfg, first)
        _acc(loss_ref, jnp.full((8, LANE), val, f32), first)

    return pl.pallas_call(
        body, name=name, grid=(t // tm,),
        in_specs=[_rows(tm, D_MODEL), _full((1, D_MODEL)), _rows(tm, D_MODEL)],
        out_specs=[_full((8, LANE)), _rows(tm, D_MODEL), _full((1, D_MODEL))],
        out_shape=[_sds((8, LANE)), _sds((t, D_MODEL)), _sds((1, D_MODEL))],
        compiler_params=_cp(1),
    )(x, fg, target)


S5_PREPARED = 4
_S5_PARAM_SHAPES = ((S5_GROUP_ROWS, N_STATE), (S5_GROUP_ROWS, N_STATE), (D_SSM, N_STATE), (D_SSM, N_STATE),
                    (N_STATE, D_SSM), (N_STATE, D_SSM), (1, D_SSM), (D_SSM, D_SSM), (1, D_SSM))

def s5_prep_fwd(are, aim, ls, bre, bim, name):
    def body(are_ref, aim_ref, ls_ref, bre_ref, bim_ref, *outs):
        vals = _s5_prep(are_ref[...], aim_ref[...], ls_ref[...], bre_ref[...], bim_ref[...])
        for o, v in zip(outs, vals):
            o[...] = v

    return pl.pallas_call(body, name=name, out_shape=[_sds(s) for s in _S5_PARAM_SHAPES[:S5_PREPARED]])(
        are, aim, ls, bre, bim)


def s5_prep_bwd(are, aim, ls, bre, bim, cts, name):
    def body(are_ref, aim_ref, ls_ref, bre_ref, bim_ref, *rest):
        ct_refs, outs = rest[:S5_PREPARED], rest[S5_PREPARED:]
        _, vj = jax.vjp(_s5_prep, are_ref[...], aim_ref[...], ls_ref[...], bre_ref[...], bim_ref[...])
        for o, v in zip(outs, vj(tuple(r[...] for r in ct_refs))):
            o[...] = v

    shapes = [(1, N_STATE)] * 3 + [(D_SSM, N_STATE)] * 2
    return pl.pallas_call(body, name=name, out_shape=[_sds(s) for s in shapes])(are, aim, ls, bre, bim, *cts)


def s5_fwd(z, params, nb, name):
    t = z.shape[0]
    nc = t // nb // S5_CHUNK
    npar = len(_S5_PARAM_SHAPES)

    def body(z_ref, *rest):
        p_refs, (y_ref, hs_ref, hr_s, hi_s) = rest[:npar], rest[npar:]

        @pl.when(pl.program_id(1) == 0)
        def _():
            hr_s[...] = jnp.zeros_like(hr_s)
            hi_s[...] = jnp.zeros_like(hi_s)

        hr, hi = hr_s[...], hi_s[...]
        hs_ref[0, :, :N_STATE] = hr
        hs_ref[0, :, N_STATE:] = hi
        y, nhr, nhi = _s5_chunk(z_ref[:, :D_SSM], z_ref[:, D_SSM:], hr, hi, *[r[...] for r in p_refs])
        y_ref[...] = y
        hr_s[...] = nhr
        hi_s[...] = nhi

    return pl.pallas_call(
        body, name=name, grid=(nb, nc),
        in_specs=[pl.BlockSpec((S5_CHUNK, 2 * D_SSM), lambda b, c: (b * nc + c, 0))]
        + [_full(s) for s in _S5_PARAM_SHAPES],
        out_specs=[pl.BlockSpec((S5_CHUNK, D_SSM), lambda b, c: (b * nc + c, 0)),
                   pl.BlockSpec((1, 1, 2 * N_STATE), lambda b, c: (b * nc + c, 0, 0))],
        out_shape=[_sds((t, D_SSM)), _sds((nb * nc, 1, 2 * N_STATE))],
        scratch_shapes=[pltpu.VMEM((1, N_STATE), f32), pltpu.VMEM((1, N_STATE), f32)],
        compiler_params=_cp(2, VMEM_BIG),
    )(z, *params)


def s5_bwd(z, params, hs, dy, nb, name):
    t = z.shape[0]
    nc = t // nb // S5_CHUNK
    npar = len(_S5_PARAM_SHAPES)

    def body(z_ref, hs_ref, dy_ref, *rest):
        p_refs, dz_ref, dp_refs, (dhr_s, dhi_s) = rest[:npar], rest[npar], rest[npar + 1:2 * npar + 1], rest[2 * npar + 1:]

        @pl.when(pl.program_id(1) == 0)
        def _():
            dhr_s[...] = jnp.zeros_like(dhr_s)
            dhi_s[...] = jnp.zeros_like(dhi_s)

        prim = (z_ref[:, :D_SSM], z_ref[:, D_SSM:], hs_ref[0, :, :N_STATE], hs_ref[0, :, N_STATE:]) + tuple(
            r[...] for r in p_refs)
        _, vj = jax.vjp(_s5_chunk, *prim)
        cts = vj((dy_ref[...], dhr_s[...], dhi_s[...]))
        dz_ref[:, :D_SSM] = cts[0].astype(_MXU)
        dz_ref[:, D_SSM:] = cts[1].astype(_MXU)
        dhr_s[...] = cts[2]
        dhi_s[...] = cts[3]
        first = (pl.program_id(0) == 0) & (pl.program_id(1) == 0)
        for r, v in zip(dp_refs, cts[4:]):
            _acc(r, v, first)

    rev = lambda b, c: (b * nc + nc - 1 - c, 0)
    return pl.pallas_call(
        body, name=name, grid=(nb, nc),
        in_specs=[pl.BlockSpec((S5_CHUNK, 2 * D_SSM), rev),
                  pl.BlockSpec((1, 1, 2 * N_STATE), lambda b, c: (b * nc + nc - 1 - c, 0, 0)),
                  pl.BlockSpec((S5_CHUNK, D_SSM), rev)] + [_full(s) for s in _S5_PARAM_SHAPES],
        out_specs=[pl.BlockSpec((S5_CHUNK, 2 * D_SSM), rev)] + [_full(s) for s in _S5_PARAM_SHAPES],
        out_shape=[_sds((t, 2 * D_SSM), _MXU)] + [_sds(s) for s in _S5_PARAM_SHAPES],
        scratch_shapes=[pltpu.VMEM((1, N_STATE), f32), pltpu.VMEM((1, N_STATE), f32)],
        compiler_params=_cp(2, VMEM_BIG),
    )(z, hs, dy, *params)


DN_PRE_ROWS = 256
DN_COLS = 3 * D_DN // LANE


def dn_pre_fwd(zq, convw, seq, name):
    t, tb = zq.shape[0], DN_PRE_ROWS
    per_seq = seq // tb

    def body(xc_ref, xp_ref, w_ref, o_ref):
        is_start = pl.program_id(0) % per_seq == 0
        for j in range(DN_COLS):
            cols = slice(j * LANE, (j + 1) * LANE)
            o_ref[:, cols] = _dn_pre(xc_ref[:, cols], xp_ref[:, cols], w_ref[0:1, cols], w_ref[1:2, cols],
                                     w_ref[2:3, cols], w_ref[3:4, cols], is_start, j)

    return pl.pallas_call(
        body, name=name, grid=(t // tb,),
        in_specs=[_rows(tb, 3 * D_DN),
                  pl.BlockSpec((DN_HALO, 3 * D_DN), lambda i: (jnp.maximum(i * (tb // DN_HALO) - 1, 0), 0)),
                  _full((DN_CONV, 3 * D_DN))],
        out_specs=_rows(tb, 3 * D_DN),
        out_shape=_sds((t, 3 * D_DN)),
        compiler_params=_cp(1, VMEM_BIG),
    )(zq, zq, convw)


def dn_pre_bwd(zq, convw, dqkv, seq, name):
    t, tb = zq.shape[0], DN_PRE_ROWS
    nrow = t // tb
    per_seq = seq // tb

    def body(xc_ref, xp_ref, w_ref, d_ref, dx_ref, dw_ref, carry):
        step = pl.program_id(0)
        i = nrow - 1 - step

        @pl.when(step == 0)
        def _():
            carry[...] = jnp.zeros_like(carry)

        for j in range(DN_COLS):
            cols = slice(j * LANE, (j + 1) * LANE)
            fn = functools.partial(_dn_pre, is_start=i % per_seq == 0, col=j)
            _, vj = jax.vjp(fn, xc_ref[:, cols], xp_ref[:, cols], w_ref[0:1, cols], w_ref[1:2, cols],
                            w_ref[2:3, cols], w_ref[3:4, cols])
            dxc, dxp, dw0, dw1, dw2, dw3 = vj(d_ref[:, cols])
            dx_ref[:tb - DN_HALO, cols] = dxc[:tb - DN_HALO].astype(_MXU)
            dx_ref[tb - DN_HALO:, cols] = (dxc[tb - DN_HALO:] + carry[:, cols]).astype(_MXU)
            carry[:, cols] = dxp
            for k, dw in enumerate((dw0, dw1, dw2, dw3)):
                @pl.when(step == 0)
                def _():
                    dw_ref[k:k + 1, cols] = dw

                @pl.when(step != 0)
                def _():
                    dw_ref[k:k + 1, cols] += dw

    rev = lambda s: (nrow - 1 - s, 0)
    return pl.pallas_call(
        body, name=name, grid=(nrow,),
        in_specs=[pl.BlockSpec((tb, 3 * D_DN), rev),
                  pl.BlockSpec((DN_HALO, 3 * D_DN),
                               lambda s: (jnp.maximum((nrow - 1 - s) * (tb // DN_HALO) - 1, 0), 0)),
                  _full((DN_CONV, 3 * D_DN)), pl.BlockSpec((tb, 3 * D_DN), rev)],
        out_specs=[pl.BlockSpec((tb, 3 * D_DN), rev), _full((DN_CONV, 3 * D_DN))],
        out_shape=[_sds((t, 3 * D_DN), _MXU), _sds((DN_CONV, 3 * D_DN))],
        scratch_shapes=[pltpu.VMEM((DN_HALO, 3 * D_DN), f32)],
        compiler_params=_cp(1, VMEM_BIG),
    )(zq, zq, convw, dqkv)


DN_LOCAL_CHUNKS = 2
DN_ATTN = DN_HEADS * DN_CHUNK


def _dn_heads(ref, rows, base=0):
    return [ref[rows, base + h * DN_HEAD_DIM:base + (h + 1) * DN_HEAD_DIM] for h in range(DN_HEADS)]


def dn_local_fwd(qkv, ab, alog, dtb, name):
    t = qkv.shape[0]
    c, n = DN_CHUNK, DN_LOCAL_CHUNKS

    def body(qkv_ref, ab_ref, alog_ref, dtb_ref, val_ref, kcd_ref, attn_ref, qd_ref, kd_ref, el_ref, inv_ref):
        rows = [pl.ds(j * c, c) for j in range(n)]
        vals, kcds, attns, qds, kds, els, invs = _dn_local(
            [_dn_heads(qkv_ref, r) for r in rows], [_dn_heads(qkv_ref, r, D_DN) for r in rows],
            [_dn_heads(qkv_ref, r, 2 * D_DN) for r in rows], [ab_ref[r, :] for r in rows], alog_ref[...], dtb_ref[...])
        for j, r in enumerate(rows):
            for h in range(DN_HEADS):
                lo, hi = h * DN_HEAD_DIM, (h + 1) * DN_HEAD_DIM
                val_ref[r, lo:hi] = vals[j][h]
                kcd_ref[r, lo:hi] = kcds[j][h].astype(_MXU)
                qd_ref[r, lo:hi] = qds[j][h].astype(_MXU)
                kd_ref[r, lo:hi] = kds[j][h].astype(_MXU)
                attn_ref[r, h * c:(h + 1) * c] = attns[j][h].astype(_MXU)
                inv_ref[r, h * c:(h + 1) * c] = invs[j][h]
            el_ref[j] = els[j]

    wide = _rows(n * c, D_DN)
    outs = pl.pallas_call(
        body, name=name, grid=(t // (n * c),),
        in_specs=[_rows(n * c, 3 * D_DN), _rows(n * c, LANE), _full((1, LANE)), _full((1, LANE))],
        out_specs=[wide, wide, _rows(n * c, DN_ATTN), wide, wide, pl.BlockSpec((n, 1, LANE), lambda i: (i, 0, 0)),
                   _rows(n * c, DN_ATTN)],
        out_shape=[_sds((t, D_DN)), _sds((t, D_DN), _MXU), _sds((t, DN_ATTN), _MXU), _sds((t, D_DN), _MXU),
                   _sds((t, D_DN), _MXU), _sds((t // c, 1, LANE)), _sds((t, DN_ATTN))],
        compiler_params=_cp(1),
    )(qkv, ab, alog, dtb)
    return outs[:6], outs[6]


def dn_local_bwd(qkv, ab, alog, dtb, inv, cts, name):
    t = qkv.shape[0]
    c, n = DN_CHUNK, DN_LOCAL_CHUNKS

    def body(qkv_ref, ab_ref, alog_ref, dtb_ref, inv_ref, dval_ref, dkcd_ref, dattn_ref, dqd_ref, dkd_ref, del_ref,
             dqkv_ref, dab_ref, dalog_ref, ddtb_ref):
        rows = [pl.ds(j * c, c) for j in range(n)]
        invs = [[inv_ref[r, h * c:(h + 1) * c] for h in range(DN_HEADS)] for r in rows]

        def local(qs, ks, vs, abs_, alog, dtb):
            return _dn_local(qs, ks, vs, abs_, alog, dtb, invs)[:6]

        _, vj = jax.vjp(local, [_dn_heads(qkv_ref, r) for r in rows], [_dn_heads(qkv_ref, r, D_DN) for r in rows],
                        [_dn_heads(qkv_ref, r, 2 * D_DN) for r in rows], [ab_ref[r, :] for r in rows], alog_ref[...],
                        dtb_ref[...])
        dattn = [[dattn_ref[r, h * c:(h + 1) * c] for h in range(DN_HEADS)] for r in rows]
        dq, dk, dv, dab, dalog, ddtb = vj(([_dn_heads(dval_ref, r) for r in rows], [_dn_heads(dkcd_ref, r) for r in rows],
                                           dattn, [_dn_heads(dqd_ref, r) for r in rows],
                                           [_dn_heads(dkd_ref, r) for r in rows], [del_ref[j] for j in range(n)]))
        for j, r in enumerate(rows):
            for h in range(DN_HEADS):
                lo, hi = h * DN_HEAD_DIM, (h + 1) * DN_HEAD_DIM
                dqkv_ref[r, lo:hi] = dq[j][h]
                dqkv_ref[r, D_DN + lo:D_DN + hi] = dk[j][h]
                dqkv_ref[r, 2 * D_DN + lo:2 * D_DN + hi] = dv[j][h]
            dab_ref[r, :] = dab[j].astype(_MXU)
        first = pl.program_id(0) == 0
        _acc(dalog_ref, dalog, first)
        _acc(ddtb_ref, ddtb, first)

    wide = _rows(n * c, D_DN)
    return pl.pallas_call(
        body, name=name, grid=(t // (n * c),),
        in_specs=[_rows(n * c, 3 * D_DN), _rows(n * c, LANE), _full((1, LANE)), _full((1, LANE)),
                  _rows(n * c, DN_ATTN), wide, wide, _rows(n * c, DN_ATTN), wide, wide,
                  pl.BlockSpec((n, 1, LANE), lambda i: (i, 0, 0))],
        out_specs=[_rows(n * c, 3 * D_DN), _rows(n * c, LANE), _full((1, LANE)), _full((1, LANE))],
        out_shape=[_sds((t, 3 * D_DN)), _sds((t, LANE), _MXU), _sds((1, LANE)), _sds((1, LANE))],
        compiler_params=_cp(1),
    )(qkv, ab, alog, dtb, inv, *cts)


def _seq_view(a, nb):
    return a.reshape((nb, a.shape[0] // nb) + a.shape[1:])


def _dn_chains(nb):
    return [(b, h) for b in range(nb) for h in range(DN_HEADS)]


DN_SCAN_CHUNKS = 4


def _dn_step_operands(val_ref, kcd_ref, attn_ref, qd_ref, kd_ref, el_ref, gg_ref, nb, j):
    chains = _dn_chains(nb)
    c = DN_CHUNK
    rows = pl.ds(j * c, c)

    def wide(ref):
        return [ref[b, rows, h * DN_HEAD_DIM:(h + 1) * DN_HEAD_DIM].astype(f32) for b, h in chains]

    attns = [attn_ref[b, rows, h * c:(h + 1) * c].astype(f32) for b, h in chains]
    return (wide(val_ref), wide(kcd_ref), attns, wide(qd_ref), wide(kd_ref),
            [el_ref[b, j, :, h:h + 1] for b, h in chains], wide(gg_ref))


def dn_scan_fwd(loc, gg, ng, nb, name):
    val, kcd, attn, qd, kd, el = loc
    t = val.shape[0]
    c, n = DN_CHUNK, DN_SCAN_CHUNKS
    nc = t // nb // c
    ns = nb * DN_HEADS

    def body(val_ref, kcd_ref, attn_ref, qd_ref, kd_ref, el_ref, gg_ref, ng_ref, y_ref, ss_ref, st):
        @pl.when(pl.program_id(0) == 0)
        def _():
            st[...] = jnp.zeros_like(st)

        sts = [st[i] for i in range(ns)]
        for j in range(n):
            for i in range(ns):
                ss_ref[j, i] = sts[i]
            ys, sts = _dn_step(*_dn_step_operands(val_ref, kcd_ref, attn_ref, qd_ref, kd_ref, el_ref, gg_ref, nb, j),
                               sts, ng_ref[...])
            for i, (b, h) in enumerate(_dn_chains(nb)):
                y_ref[b, pl.ds(j * c, c), h * DN_HEAD_DIM:(h + 1) * DN_HEAD_DIM] = ys[i]
        for i in range(ns):
            st[i] = sts[i]

    def blk(w):
        return pl.BlockSpec((nb, n * c, w), lambda k: (0, k, 0))

    el_spec = pl.BlockSpec((nb, n, 1, LANE), lambda k: (0, k, 0, 0))
    y, ss = pl.pallas_call(
        body, name=name, grid=(nc // n,),
        in_specs=[blk(D_DN), blk(D_DN), blk(DN_ATTN), blk(D_DN), blk(D_DN), el_spec, blk(D_DN), _full((1, LANE))],
        out_specs=[blk(D_DN), pl.BlockSpec((n, ns, DN_HEAD_DIM, DN_HEAD_DIM), lambda k: (k, 0, 0, 0))],
        out_shape=[_sds((nb, t // nb, D_DN)), _sds((nc, ns, DN_HEAD_DIM, DN_HEAD_DIM))],
        scratch_shapes=[pltpu.VMEM((ns, DN_HEAD_DIM, DN_HEAD_DIM), f32)],
        compiler_params=_cp(1, VMEM_BIG),
    )(_seq_view(val, nb), _seq_view(kcd, nb), _seq_view(attn, nb), _seq_view(qd, nb), _seq_view(kd, nb),
      el.reshape(nb, nc, 1, LANE), _seq_view(gg, nb), ng)
    return y.reshape(t, D_DN), ss


def dn_scan_bwd(loc, gg, ng, ss, dy, nb, name):
    val, kcd, attn, qd, kd, el = loc
    t = val.shape[0]
    c, n = DN_CHUNK, DN_SCAN_CHUNKS
    nc = t // nb // c
    ns = nb * DN_HEADS
    steps = nc // n

    def body(val_ref, kcd_ref, attn_ref, qd_ref, kd_ref, el_ref, gg_ref, ng_ref, ss_ref, dy_ref,
             dval_ref, dkcd_ref, dattn_ref, dqd_ref, dkd_ref, del_ref, dgg_ref, dng_ref, dst):
        @pl.when(pl.program_id(0) == 0)
        def _():
            dst[...] = jnp.zeros_like(dst)

        lane = lax.broadcasted_iota(jnp.int32, (1, LANE), 1)
        chains = _dn_chains(nb)
        ds = [dst[i] for i in range(ns)]
        dng_tot = jnp.zeros((1, LANE), f32)
        for j in reversed(range(n)):
            rows = pl.ds(j * c, c)
            _, vj = jax.vjp(_dn_step,
                            *_dn_step_operands(val_ref, kcd_ref, attn_ref, qd_ref, kd_ref, el_ref, gg_ref, nb, j),
                            [ss_ref[j, i] for i in range(ns)], ng_ref[...])
            dys = [dy_ref[b, rows, h * DN_HEAD_DIM:(h + 1) * DN_HEAD_DIM] for b, h in chains]
            dval, dkcd, dattn, dqd, dkd, dlast, dgg, ds, dng = vj((dys, ds))
            dng_tot = dng_tot + dng
            del_rows = [jnp.zeros((1, LANE), f32) for _ in range(nb)]
            for i, (b, h) in enumerate(chains):
                cols = slice(h * DN_HEAD_DIM, (h + 1) * DN_HEAD_DIM)
                dval_ref[b, rows, cols] = dval[i]
                dkcd_ref[b, rows, cols] = dkcd[i]
                dattn_ref[b, rows, h * c:(h + 1) * c] = dattn[i]
                dqd_ref[b, rows, cols] = dqd[i]
                dkd_ref[b, rows, cols] = dkd[i]
                dgg_ref[b, rows, cols] = dgg[i].astype(_MXU)
                del_rows[b] = del_rows[b] + jnp.where(lane == h, dlast[i], 0.0)
            for b in range(nb):
                del_ref[b, j] = del_rows[b]
        for i in range(ns):
            dst[i] = ds[i]
        _acc(dng_ref, dng_tot, pl.program_id(0) == 0)

    def blk(w):
        return pl.BlockSpec((nb, n * c, w), lambda k: (0, steps - 1 - k, 0))

    el_spec = pl.BlockSpec((nb, n, 1, LANE), lambda k: (0, steps - 1 - k, 0, 0))
    outs = pl.pallas_call(
        body, name=name, grid=(steps,),
        in_specs=[blk(D_DN), blk(D_DN), blk(DN_ATTN), blk(D_DN), blk(D_DN), el_spec, blk(D_DN), _full((1, LANE)),
                  pl.BlockSpec((n, ns, DN_HEAD_DIM, DN_HEAD_DIM), lambda k: (steps - 1 - k, 0, 0, 0)), blk(D_DN)],
        out_specs=[blk(D_DN), blk(D_DN), blk(DN_ATTN), blk(D_DN), blk(D_DN), el_spec, blk(D_DN), _full((1, LANE))],
        out_shape=[_sds((nb, t // nb, D_DN)), _sds((nb, t // nb, D_DN)), _sds((nb, t // nb, DN_ATTN)),
                   _sds((nb, t // nb, D_DN)), _sds((nb, t // nb, D_DN)), _sds((nb, nc, 1, LANE)),
                   _sds((nb, t // nb, D_DN), _MXU), _sds((1, LANE))],
        scratch_shapes=[pltpu.VMEM((ns, DN_HEAD_DIM, DN_HEAD_DIM), f32)],
        compiler_params=_cp(1, VMEM_BIG),
    )(_seq_view(val, nb), _seq_view(kcd, nb), _seq_view(attn, nb), _seq_view(qd, nb), _seq_view(kd, nb),
      el.reshape(nb, nc, 1, LANE), _seq_view(gg, nb), ng, ss, _seq_view(dy, nb))
    dloc = [o.reshape((t,) + o.shape[2:]) for o in outs[:5]] + [outs[5].reshape(t // c, 1, LANE)]
    return dloc, outs[6].reshape(t, D_DN), outs[7]


SG_ROWS = 512


def sg_fwd(z, lng, lnb, w, bt, name):
    t = z.shape[0]

    def body(z_ref, lng_ref, lnb_ref, w_ref, bt_ref, y_ref):
        ws = [w_ref[h] for h in range(SG_HEADS)]
        for k in range(SG_ROWS // SG_CHUNK):
            r = pl.ds(k * SG_CHUNK, SG_CHUNK)
            y_ref[r, :] = _sg_chunk(z_ref[r, :D_SG], z_ref[r, D_SG:2 * D_SG], z_ref[r, 2 * D_SG:], lng_ref[...],
                                    lnb_ref[...], ws, bt_ref[...])

    return pl.pallas_call(
        body, name=name, grid=(t // SG_ROWS,),
        in_specs=[_rows(SG_ROWS, 3 * D_SG), _full((1, D_SG)), _full((1, D_SG)),
                  _full((SG_HEADS, SG_CHUNK, SG_CHUNK)), _full((SG_CHUNK, LANE))],
        out_specs=_rows(SG_ROWS, D_SG),
        out_shape=_sds((t, D_SG)),
        compiler_params=_cp(1),
    )(z, lng, lnb, w, bt)


def sg_bwd(z, lng, lnb, w, bt, dy, name):
    t = z.shape[0]

    def body(z_ref, lng_ref, lnb_ref, w_ref, bt_ref, dy_ref, dz_ref, dlng_ref, dlnb_ref, dw_ref, dbt_ref):
        ws = [w_ref[h] for h in range(SG_HEADS)]
        tot = None
        for k in range(SG_ROWS // SG_CHUNK):
            r = pl.ds(k * SG_CHUNK, SG_CHUNK)
            _, vj = jax.vjp(_sg_chunk, z_ref[r, :D_SG], z_ref[r, D_SG:2 * D_SG], z_ref[r, 2 * D_SG:], lng_ref[...],
                            lnb_ref[...], ws, bt_ref[...])
            du, dv, dgate, dlng, dlnb, dws, dbt = vj(dy_ref[r, :])
            dz_ref[r, :D_SG] = du.astype(_MXU)
            dz_ref[r, D_SG:2 * D_SG] = dv.astype(_MXU)
            dz_ref[r, 2 * D_SG:] = dgate.astype(_MXU)
            part = [dlng, dlnb, dbt] + list(dws)
            tot = part if tot is None else [a + b for a, b in zip(tot, part)]
        first = pl.program_id(0) == 0
        _acc(dlng_ref, tot[0], first)
        _acc(dlnb_ref, tot[1], first)
        _acc(dbt_ref, tot[2], first)
        for h in range(SG_HEADS):
            @pl.when(first)
            def _():
                dw_ref[h] = tot[3 + h]

            @pl.when(jnp.logical_not(first))
            def _():
                dw_ref[h] += tot[3 + h]

    return pl.pallas_call(
        body, name=name, grid=(t // SG_ROWS,),
        in_specs=[_rows(SG_ROWS, 3 * D_SG), _full((1, D_SG)), _full((1, D_SG)),
                  _full((SG_HEADS, SG_CHUNK, SG_CHUNK)), _full((SG_CHUNK, LANE)), _rows(SG_ROWS, D_SG)],
        out_specs=[_rows(SG_ROWS, 3 * D_SG), _full((1, D_SG)), _full((1, D_SG)),
                   _full((SG_HEADS, SG_CHUNK, SG_CHUNK)), _full((SG_CHUNK, LANE))],
        out_shape=[_sds((t, 3 * D_SG), _MXU), _sds((1, D_SG)), _sds((1, D_SG)), _sds((SG_HEADS, SG_CHUNK, SG_CHUNK)),
                   _sds((SG_CHUNK, LANE))],
        compiler_params=_cp(1),
    )(z, lng, lnb, w, bt, dy)


def add_pairs(a_list, b_list, name):
    n = len(a_list)

    def body(*refs):
        for a_ref, b_ref, o_ref in zip(refs[:n], refs[n:2 * n], refs[2 * n:]):
            o_ref[...] = (a_ref[...].astype(f32) + b_ref[...].astype(f32)).astype(o_ref.dtype)

    return pl.pallas_call(
        body, name=name, out_shape=[_sds(a.shape, a.dtype) for a in a_list],
        compiler_params=pltpu.CompilerParams(vmem_limit_bytes=VMEM_BIG),
    )(*a_list, *b_list)


def _adamw(g, w, m, v):
    nm = ADAM_B1 * m + (1.0 - ADAM_B1) * g
    nv = ADAM_B2 * v + (1.0 - ADAM_B2) * jnp.square(g)
    m_hat = nm / (1.0 - ADAM_B1 ** ADAM_STEP)
    v_hat = nv / (1.0 - ADAM_B2 ** ADAM_STEP)
    return -ADAM_LR * (m_hat / (jnp.sqrt(v_hat) + ADAM_EPS) + ADAM_WD * w), nm, nv


def sum_parts(half, recv, name):
    _, r, c = recv.shape
    tr = 256 if r % 256 == 0 else r

    def body(half_ref, recv_ref, g_ref):
        g = recv_ref[0].astype(f32)
        for k in range(1, N_CHIPS):
            g = g + recv_ref[k].astype(f32)
        g_ref[...] = g

    return pl.pallas_call(
        body, name=name,
        grid_spec=pltpu.PrefetchScalarGridSpec(
            num_scalar_prefetch=1, grid=(r // tr,),
            in_specs=[pl.BlockSpec((N_CHIPS, tr, c), lambda i, h: (0, i, 0))],
            out_specs=pl.BlockSpec((None, tr, c), lambda i, h: (h[0], i, 0))),
        out_shape=_sds((2, r, c)),
        compiler_params=_cp(1, VMEM_BIG),
    )(half, recv)


def adamw(g, w, m, v, name):
    _, r, c = w.shape
    tr = 256 if r % 256 == 0 else r

    def body(g_ref, w_ref, m_ref, v_ref, d_ref, nm_ref, nv_ref):
        d_ref[...], nm_ref[...], nv_ref[...] = _adamw(g_ref[...], w_ref[...], m_ref[...], v_ref[...])

    blk = pl.BlockSpec((None, tr, c), lambda l, i: (l, i, 0))
    return pl.pallas_call(
        body, name=name, grid=(2, r // tr), in_specs=[blk] * 4, out_specs=[blk] * 3, out_shape=[_sds((2, r, c))] * 3,
        compiler_params=_cp(2, VMEM_BIG),
    )(g, w, m, v)


def sum_parts_small(chip, parts, sums, name):
    n = len(sums)

    def body(chip_ref, *refs):
        for part, own, out in zip(refs[:n], refs[n:2 * n], refs[2 * n:]):
            g = jnp.where(chip_ref[0] == 0, own[...], part[0])
            for q in range(1, N_CHIPS):
                g = g + jnp.where(chip_ref[0] == q, own[...], part[q])
            out[...] = g

    vmem = pl.BlockSpec(memory_space=pltpu.VMEM)
    return pl.pallas_call(
        body, name=name, in_specs=[pl.BlockSpec(memory_space=pltpu.SMEM)] + [vmem] * (2 * n), out_specs=[vmem] * n,
        out_shape=[_sds(s.shape) for s in sums], compiler_params=pltpu.CompilerParams(vmem_limit_bytes=VMEM_BIG),
    )(chip, *parts, *sums)


def adamw_small(gs, ws, ms, vs, name):
    n = len(ws)

    def body(*refs):
        ins, outs = refs[:4 * n], refs[4 * n:]
        for k in range(n):
            outs[k][...], outs[n + k][...], outs[2 * n + k][...] = _adamw(
                ins[k][...], ins[n + k][...], ins[2 * n + k][...], ins[3 * n + k][...])

    outs = pl.pallas_call(
        body, name=name, out_shape=[_sds(w.shape) for w in ws] * 3,
        compiler_params=pltpu.CompilerParams(vmem_limit_bytes=VMEM_BIG),
    )(*gs, *ws, *ms, *vs)
    return [outs[j * n:(j + 1) * n] for j in range(3)]


_ANY = pl.BlockSpec(memory_space=pltpu.HBM)
_MESH = pl.DeviceIdType.MESH


def _flip(v, bit):
    return 1 - v if bit else v


_CHIP_RELS = ((1, 0), (0, 1), (1, 1))


def _piece(ref, kind, q):
    if kind[0] == "slot":
        return ref.at[q]
    if kind[0] == "all":
        return ref
    _, axis, n = kind
    return ref.at[(slice(None),) * axis + (pl.ds(q * n, n),)]


def _piece_shape(shape, kind):
    if kind[0] == "slot":
        return tuple(shape[1:])
    if kind[0] == "all":
        return tuple(shape)
    _, axis, n = kind
    return tuple(shape[:axis]) + (n,) + tuple(shape[axis + 1:])


def gather_weights(shards, kinds, name):
    n = len(shards)

    def out_shape(s, kind):
        if kind[0] == "slot":
            return (N_CHIPS,) + tuple(s.shape)
        _, axis, w = kind
        return tuple(s.shape[:axis + 1]) + (N_CHIPS * w,) + tuple(s.shape[axis + 2:])

    def place(o_ref, kind, q, layer):
        if kind[0] == "slot":
            return o_ref.at[q, layer]
        return _piece(o_ref.at[layer], kind, q)

    def body(*refs):
        s_refs, o_refs = refs[:n], refs[n:2 * n]
        send_sems, recv_sems, fwd_send_sems, fwd_recv_sems = refs[2 * n:]
        x, y, c = lax.axis_index("x"), lax.axis_index("y"), lax.axis_index("c")
        mine = 2 * x + y
        sends, arrivals, forwards, fwd_arrivals = [], [], [], []
        for r, (fx, fy) in enumerate(_CHIP_RELS):
            px, py = _flip(x, fx), _flip(y, fy)
            peer = 2 * px + py
            for k in range(n):
                s = r * n + k
                sends.append(pltpu.make_async_remote_copy(
                    src_ref=s_refs[k].at[c], dst_ref=place(o_refs[k], kinds[k], mine, c), send_sem=send_sems.at[s],
                    recv_sem=recv_sems.at[s], device_id=(px, py, c), device_id_type=_MESH))
                arrivals.append(pltpu.make_async_remote_copy(
                    src_ref=s_refs[k].at[c], dst_ref=place(o_refs[k], kinds[k], peer, c), send_sem=send_sems.at[s],
                    recv_sem=recv_sems.at[s], device_id=(px, py, c), device_id_type=_MESH))
                block = place(o_refs[k], kinds[k], peer, c)
                forwards.append(pltpu.make_async_remote_copy(
                    src_ref=block, dst_ref=block, send_sem=fwd_send_sems.at[s], recv_sem=fwd_recv_sems.at[s],
                    device_id=(x, y, 1 - c), device_id_type=_MESH))
                other = place(o_refs[k], kinds[k], peer, 1 - c)
                fwd_arrivals.append(pltpu.make_async_remote_copy(
                    src_ref=other, dst_ref=other, send_sem=fwd_send_sems.at[s], recv_sem=fwd_recv_sems.at[s],
                    device_id=(x, y, 1 - c), device_id_type=_MESH))
        for cp in sends:
            cp.start()
        for arrived, fwd in zip(arrivals, forwards):
            arrived.wait_recv()
            fwd.start()
        for cp in fwd_arrivals:
            cp.wait_recv()
        for cp in sends + forwards:
            cp.wait_send()

    m = len(_CHIP_RELS) * n
    return pl.pallas_call(
        body, name=name, in_specs=[_ANY] * n, out_specs=[_ANY] * n,
        out_shape=[_sds(out_shape(s, k), s.dtype) for s, k in zip(shards, kinds)],
        scratch_shapes=[pltpu.SemaphoreType.DMA((m,))] * 4,
    )(*shards)


def _owned_by(owners, side):
    return [k for k, o in enumerate(owners) if o == side]


def exchange_halves(gs, smalls, owners, name):
    n, ns = len(gs), len(smalls)

    def body(*refs):
        g_refs, s_refs = refs[:n], refs[n:n + ns]
        got_refs, sgot_refs = refs[n + ns:2 * n + ns], refs[2 * n + ns:2 * (n + ns)]
        send_sems, recv_sems = refs[2 * (n + ns):]
        x, y, c = lax.axis_index("x"), lax.axis_index("y"), lax.axis_index("c")
        sibling = (x, y, 1 - c)
        swaps = [pltpu.make_async_remote_copy(
            src_ref=g_refs[k].at[1 - c], dst_ref=got_refs[k], send_sem=send_sems.at[k], recv_sem=recv_sems.at[k],
            device_id=sibling, device_id_type=_MESH) for k in range(n)]
        gives = [pltpu.make_async_remote_copy(
            src_ref=s_refs[k], dst_ref=sgot_refs[k], send_sem=send_sems.at[n + k], recv_sem=recv_sems.at[n + k],
            device_id=sibling, device_id_type=_MESH) for k in range(ns)]
        for cp in swaps:
            cp.start()
        for side in (0, 1):
            @pl.when(c == 1 - side)
            def _():
                for k in _owned_by(owners, side):
                    gives[k].start()
        for cp in swaps:
            cp.wait()
        for side in (0, 1):
            @pl.when(c == 1 - side)
            def _():
                for k in _owned_by(owners, side):
                    gives[k].wait_send()

            @pl.when(c == side)
            def _():
                for k in _owned_by(owners, side):
                    gives[k].wait_recv()

    outs = pl.pallas_call(
        body, name=name, in_specs=[_ANY] * (n + ns), out_specs=[_ANY] * (n + ns),
        out_shape=[_sds(g.shape[1:], g.dtype) for g in gs] + [_sds(s.shape, s.dtype) for s in smalls],
        scratch_shapes=[pltpu.SemaphoreType.DMA((n + ns,)), pltpu.SemaphoreType.DMA((n + ns,))],
    )(*gs, *smalls)
    return outs[:n], outs[n:]


def reduce_to_chips(ts, kinds, smalls, owners, name):
    n, ns = len(ts), len(smalls)

    def body(*refs):
        t_refs, s_refs = refs[:n], refs[n:n + ns]
        o_refs, so_refs = refs[n + ns:2 * n + ns], refs[2 * n + ns:2 * (n + ns)]
        send_sems, recv_sems = refs[2 * (n + ns):]
        x, y, c = lax.axis_index("x"), lax.axis_index("y"), lax.axis_index("c")
        mine = 2 * x + y
        sends, arrivals, small_sends, small_arrivals = [], [], [], []
        for r, (fx, fy) in enumerate(_CHIP_RELS):
            px, py = _flip(x, fx), _flip(y, fy)
            peer = 2 * px + py
            for k in range(n + ns):
                s = r * (n + ns) + k
                if k < n:
                    src, dst = _piece(t_refs[k], kinds[k], peer), o_refs[k]
                else:
                    src, dst = s_refs[k - n], so_refs[k - n]
                go = pltpu.make_async_remote_copy(
                    src_ref=src, dst_ref=dst.at[mine], send_sem=send_sems.at[s], recv_sem=recv_sems.at[s],
                    device_id=(px, py, c), device_id_type=_MESH)
                come = pltpu.make_async_remote_copy(
                    src_ref=src, dst_ref=dst.at[peer], send_sem=send_sems.at[s], recv_sem=recv_sems.at[s],
                    device_id=(px, py, c), device_id_type=_MESH)
                (sends if k < n else small_sends).append(go)
                (arrivals if k < n else small_arrivals).append(come)

        def owned(copies, side):
            return [cp for j, cp in enumerate(copies) if owners[j % ns] == side]

        for cp in sends:
            cp.start()
        for side in (0, 1):
            @pl.when(c == side)
            def _():
                for cp in owned(small_sends, side):
                    cp.start()
        for cp in arrivals:
            cp.wait_recv()
        for cp in sends:
            cp.wait_send()
        for side in (0, 1):
            @pl.when(c == side)
            def _():
                for cp in owned(small_arrivals, side):
                    cp.wait_recv()
                for cp in owned(small_sends, side):
                    cp.wait_send()

    m = len(_CHIP_RELS) * (n + ns)
    outs = pl.pallas_call(
        body, name=name, in_specs=[_ANY] * (n + ns), out_specs=[_ANY] * (n + ns),
        out_shape=[_sds((N_CHIPS,) + _piece_shape(t.shape, k), t.dtype) for t, k in zip(ts, kinds)]
        + [_sds((N_CHIPS,) + s.shape, s.dtype) for s in smalls],
        scratch_shapes=[pltpu.SemaphoreType.DMA((m,)), pltpu.SemaphoreType.DMA((m,))],
    )(*ts, *smalls)
    return outs[:n], outs[n:]


def share_halves(rs, smalls, owners, name):
    n, ns = len(rs), len(smalls)

    def body(*refs):
        o_refs, so_refs = refs[n + ns:2 * n + ns], refs[2 * n + ns:2 * (n + ns)]
        send_sems, recv_sems = refs[2 * (n + ns):]
        x, y, c = lax.axis_index("x"), lax.axis_index("y"), lax.axis_index("c")
        sibling = (x, y, 1 - c)
        swaps = [pltpu.make_async_remote_copy(
            src_ref=o_refs[k].at[c], dst_ref=o_refs[k].at[c], send_sem=send_sems.at[k], recv_sem=recv_sems.at[k],
            device_id=sibling, device_id_type=_MESH) for k in range(n)]
        arrivals = [pltpu.make_async_remote_copy(
            src_ref=o_refs[k].at[c], dst_ref=o_refs[k].at[1 - c], send_sem=send_sems.at[k], recv_sem=recv_sems.at[k],
            device_id=sibling, device_id_type=_MESH) for k in range(n)]
        gives = [pltpu.make_async_remote_copy(
            src_ref=so_refs[k], dst_ref=so_refs[k], send_sem=send_sems.at[n + k], recv_sem=recv_sems.at[n + k],
            device_id=sibling, device_id_type=_MESH) for k in range(ns)]
        for cp in swaps:
            cp.start()
        for side in (0, 1):
            @pl.when(c == side)
            def _():
                for k in _owned_by(owners, side):
                    gives[k].start()
        for cp in arrivals:
            cp.wait_recv()
        for cp in swaps:
            cp.wait_send()
        for side in (0, 1):
            @pl.when(c == side)
            def _():
                for k in _owned_by(owners, side):
                    gives[k].wait_send()

            @pl.when(c == 1 - side)
            def _():
                for k in _owned_by(owners, side):
                    gives[k].wait_recv()

    outs = pl.pallas_call(
        body, name=name, in_specs=[_ANY] * (n + ns), out_specs=[_ANY] * (n + ns),
        out_shape=[_sds(r.shape, r.dtype) for r in list(rs) + list(smalls)],
        input_output_aliases={k: k for k in range(n + ns)},
        scratch_shapes=[pltpu.SemaphoreType.DMA((n + ns,)), pltpu.SemaphoreType.DMA((n + ns,))],
    )(*rs, *smalls)
    return outs[:n], outs[n:]


def _small_view(a):
    if a.size < 8 * LANE:
        return jnp.pad(a.reshape(-1), (0, 8 * LANE - a.size)).reshape(8, LANE)
    if a.ndim == 1:
        return a.reshape(1, a.shape[0])
    if a.ndim == 4 and a.shape[-1] < LANE:
        return a.reshape(a.shape[0], a.shape[1], a.shape[2] * a.shape[3])
    return a


def _permuted_from_shards(shards):
    parts = []
    for lo, hi in GROUP_COLS:
        for q in range(N_CHIPS):
            a, b = max(lo, q * SHARD_COLS), min(hi, (q + 1) * SHARD_COLS)
            if a < b:
                parts.append(shards[q][..., a - q * SHARD_COLS:b - q * SHARD_COLS])
    pad = jnp.zeros(shards[0].shape[:-1] + (D_IN_PAD - D_IN,), shards[0].dtype)
    return jnp.concatenate(parts + [pad], axis=-1)


def _shards_from_groups(groups):
    in_order = sorted(range(len(GROUP_COLS)), key=lambda j: GROUP_COLS[j][0])
    shards = []
    for q in range(N_CHIPS):
        parts = []
        for j in in_order:
            lo, hi = GROUP_COLS[j]
            a, b = max(lo, q * SHARD_COLS), min(hi, (q + 1) * SHARD_COLS)
            if a < b:
                parts.append(groups[j][..., a - lo:b - lo])
        shards.append(jnp.concatenate(parts, axis=-1))
    return shards


def _expand_b(b):
    eye = jnp.eye(SSM_GROUPS, dtype=b.dtype)
    return jnp.einsum("gnc,gh->gchn", b, eye).reshape(D_SSM, N_STATE)


def _extract_b(e):
    return jnp.einsum("gcgn->gnc", e.reshape(SSM_GROUPS, SSM_GROUP, SSM_GROUPS, SSM_STATE))


def _expand_c(c):
    eye = jnp.eye(SSM_GROUPS, dtype=c.dtype)
    return jnp.einsum("gcn,gh->gnhc", c, eye).reshape(N_STATE, D_SSM)


def _extract_c(e):
    return jnp.einsum("gngc->gcn", e.reshape(SSM_GROUPS, SSM_STATE, SSM_GROUPS, SSM_GROUP))


def _lane_row(v):
    return jnp.pad(v, (0, LANE - v.shape[0])).reshape(1, LANE)


def _layer_params(w, l):
    return dict(
        norm_g=w["norm_g"][l][None], win=w["w_in_perm"][l], wout=w["w_out"][l].astype(_MXU),
        pg=w["ple_norm_g"][l][None], wgate=w["w_ple_gate"][l].astype(_MXU), wple=w["w_ple"][l].astype(_MXU),
        are=w["ssm_a_re"][l].reshape(1, N_STATE), aim=w["ssm_a_im"][l].reshape(1, N_STATE),
        ls=jnp.repeat(w["ssm_log_step"][l], SSM_STATE).reshape(1, N_STATE),
        bre=_expand_b(w["ssm_b_re"][l]), bim=_expand_b(w["ssm_b_im"][l]),
        cr=_expand_c(w["ssm_c_re"][l]), ci=_expand_c(w["ssm_c_im"][l]),
        dr=w["ssm_d"][l].reshape(1, D_SSM), wglu=w["ssm_w_glu"][l].astype(f32), bglu=w["ssm_b_glu"][l][None],
        convw=w["dn_conv_w"][l], alog=_lane_row(w["dn_a_log"][l]), dtb=_lane_row(w["dn_dt_bias"][l]),
        ng=w["dn_norm_g"][l][None],
        lng=w["sg_ln_g"][l][None], lnb=w["sg_ln_b"][l][None], sgw=w["sg_w"][l],
        bt=jnp.pad(w["sg_b"][l].T, ((0, 0), (0, LANE - SG_HEADS))),
    )


def _layer_fwd(x, p, lp, nb, tag):
    seq = x.shape[0] // nb
    h, zs, zq, zg, zsg, zab = in_fwd(x, lp["norm_g"], lp["win"], f"in_fwd{tag}")
    prep = s5_prep_fwd(lp["are"], lp["aim"], lp["ls"], lp["bre"], lp["bim"], f"s5_prep_fwd{tag}")
    s5p = tuple(prep) + (lp["cr"], lp["ci"], lp["dr"], lp["wglu"], lp["bglu"])
    ys, hs = s5_fwd(zs, s5p, nb, f"s5_fwd{tag}")
    qkv = dn_pre_fwd(zq, lp["convw"], seq, f"dn_pre_fwd{tag}")
    loc, inv = dn_local_fwd(qkv, zab, lp["alog"], lp["dtb"], f"dn_local_fwd{tag}")
    yd, ss = dn_scan_fwd(loc, zg, lp["ng"], nb, f"dn_scan_fwd{tag}")
    yg = sg_fwd(zsg, lp["lng"], lp["lnb"], lp["sgw"], lp["bt"], f"sg_fwd{tag}")
    x2, x1, y, hn = post_fwd(x, ys, yd, yg, p, lp["wout"], lp["pg"], lp["wgate"], lp["wple"], f"post_fwd{tag}")
    saved = dict(x=x, h=h, zs=zs, zq=zq, zg=zg, zsg=zsg, zab=zab, s5p=s5p, hs=hs, qkv=qkv, loc=loc, inv=inv, ss=ss, x1=x1, y=y, hn=hn, p=p)
    return x2, saved


def _layer_bwd(dx2, sv, lp, nb, tag):
    seq = dx2.shape[0] // nb
    dx1, dgp, dpp, dys, dyd, dyg, dpg = post_bwd(dx2, sv["x1"], sv["hn"], sv["p"], lp["wout"], lp["pg"], lp["wgate"],
                                                 lp["wple"], f"post_bwd{tag}")
    g = {}
    g["w_out"] = wgrad(sv["y"], dx1, f"wgrad_out{tag}")
    g["w_ple_gate"] = wgrad(sv["hn"], dgp, f"wgrad_gate{tag}")
    g["w_ple"] = wgrad(sv["p"], dpp, f"wgrad_ple{tag}")
    g["ple_norm_g"] = dpg[0]
    dzsg, dlng, dlnb, dsgw, dbt = sg_bwd(sv["zsg"], lp["lng"], lp["lnb"], lp["sgw"], lp["bt"], dyg, f"sg_bwd{tag}")
    g["sg_ln_g"], g["sg_ln_b"], g["sg_w"], g["sg_b"] = dlng[0], dlnb[0], dsgw, dbt[:, :SG_HEADS].T
    dloc, dzg, dng = dn_scan_bwd(sv["loc"], sv["zg"], lp["ng"], sv["ss"], dyd, nb, f"dn_scan_bwd{tag}")
    dqkv, dzab, dalog, ddtb = dn_local_bwd(sv["qkv"], sv["zab"], lp["alog"], lp["dtb"], sv["inv"], dloc,
                                           f"dn_local_bwd{tag}")
    dzq, dconv = dn_pre_bwd(sv["zq"], lp["convw"], dqkv, seq, f"dn_pre_bwd{tag}")
    g["dn_conv_w"], g["dn_a_log"], g["dn_dt_bias"], g["dn_norm_g"] = dconv, dalog[0, :DN_HEADS], ddtb[0, :DN_HEADS], dng[0]
    s5out = s5_bwd(sv["zs"], sv["s5p"], sv["hs"], dys, nb, f"s5_bwd{tag}")
    dzs, dprep, (dcr, dci, ddr, dwglu, dbglu) = s5out[0], s5out[1:1 + S5_PREPARED], s5out[1 + S5_PREPARED:]
    dare, daim, dls, dbre, dbim = s5_prep_bwd(lp["are"], lp["aim"], lp["ls"], lp["bre"], lp["bim"], dprep,
                                              f"s5_prep_bwd{tag}")
    g["ssm_a_re"] = dare.reshape(SSM_GROUPS, SSM_STATE)
    g["ssm_a_im"] = daim.reshape(SSM_GROUPS, SSM_STATE)
    g["ssm_log_step"] = dls.reshape(SSM_GROUPS, SSM_STATE).sum(axis=1)
    g["ssm_b_re"], g["ssm_b_im"] = _extract_b(dbre), _extract_b(dbim)
    g["ssm_c_re"], g["ssm_c_im"] = _extract_c(dcr), _extract_c(dci)
    g["ssm_d"] = ddr.reshape(SSM_GROUPS, SSM_GROUP)
    g["ssm_w_glu"], g["ssm_b_glu"] = dwglu, dbglu[0]
    dzs_all = (dzs, dzq, dzg, dzsg, dzab)
    dx, dng_in = in_bwd(sv["x"], lp["norm_g"], lp["win"], dzs_all, dx1, f"in_bwd{tag}")
    g["w_in_pieces"] = [wgrad(sv["h"], dz, f"wgrad_in{k}{tag}") for k, dz in enumerate(dzs_all)]
    g["norm_g"] = dng_in[0]
    return dx, g


def _local_step(x, p, target, w, nb):
    lps = [_layer_params(w, l) for l in range(DEPTH)]
    saved = []
    for l in range(DEPTH):
        x, sv = _layer_fwd(x, p[l], lps[l], nb, f"_l{l}")
        saved.append(sv)
    loss_blk, dx, dfg = loss_fwd_bwd(x, w["final_norm_g"][None], target, "loss")
    grads = [None] * DEPTH
    for l in reversed(range(DEPTH)):
        dx, grads[l] = _layer_bwd(dx, saved[l], lps[l], nb, f"_l{l}")
    out = {k: jnp.stack([grads[l][k] for l in range(DEPTH)]) for k in grads[0] if k != "w_in_pieces"}
    out["w_in_pieces"] = [grads[l]["w_in_pieces"] for l in range(DEPTH)]
    out["final_norm_g"] = dfg[0]
    return loss_blk[0, 0], dx, out


def kernel(x, p, norm_g, w_in, ssm_a_re, ssm_a_im, ssm_b_re, ssm_b_im, ssm_c_re, ssm_c_im, ssm_d, ssm_log_step, ssm_w_glu, ssm_b_glu, dn_conv_w, dn_a_log, dn_dt_bias, dn_norm_g, sg_ln_g, sg_ln_b, sg_w, sg_b, w_out, ple_norm_g, w_ple_gate, w_ple, final_norm_g, loss_target, m_norm_g, m_w_in, m_ssm_a_re, m_ssm_a_im, m_ssm_b_re, m_ssm_b_im, m_ssm_c_re, m_ssm_c_im, m_ssm_d, m_ssm_log_step, m_ssm_w_glu, m_ssm_b_glu, m_dn_conv_w, m_dn_a_log, m_dn_dt_bias, m_dn_norm_g, m_sg_ln_g, m_sg_ln_b, m_sg_w, m_sg_b, m_w_out, m_ple_norm_g, m_w_ple_gate, m_w_ple, m_final_norm_g, v_norm_g, v_w_in, v_ssm_a_re, v_ssm_a_im, v_ssm_b_re, v_ssm_b_im, v_ssm_c_re, v_ssm_c_im, v_ssm_d, v_ssm_log_step, v_ssm_w_glu, v_ssm_b_glu, v_dn_conv_w, v_dn_a_log, v_dn_dt_bias, v_dn_norm_g, v_sg_ln_g, v_sg_ln_b, v_sg_w, v_sg_b, v_w_out, v_ple_norm_g, v_w_ple_gate, v_w_ple, v_final_norm_g):
    args = locals()
    w = {n: args[n] for n in WEIGHTS}
    m = {n: args["m_" + n] for n in WEIGHTS}
    v = {n: args["v_" + n] for n in WEIGHTS}
    nb, seq = x.shape[0], x.shape[1]
    t = nb * seq

    full = _gather_full(w)
    loss_local, dx, grads = _local_step(x.reshape(t, D_MODEL), p.reshape(DEPTH, t, D_PLE),
                                        loss_target.reshape(t, D_MODEL), full, nb)
    outs, loss = _reduce_and_update(grads, w, m, v, loss_local)
    return (loss, dx.reshape(nb, seq, D_MODEL), *[outs[0][n] for n in WEIGHTS], *[outs[1][n] for n in WEIGHTS],
            *[outs[2][n] for n in WEIGHTS], *[outs[3][n] for n in WEIGHTS])


def _gather_full(w):
    sh_names = [n for n, _ in SHARDED]
    shards = [w[n] if n == "dn_conv_w" else w[n].astype(_COMM) for n in sh_names]
    gathered = gather_weights(shards, [k for _, k in SHARDED], "gather_weights")
    chip = 2 * lax.axis_index("x") + lax.axis_index("y")
    full = {n: w[n] for n in REPLICATED}
    for (n, kind), shard, got in zip(SHARDED, shards, gathered):
        if kind[0] == "slot":
            full[n] = lax.dynamic_update_index_in_dim(got, shard, chip, 0)
        else:
            full[n] = lax.dynamic_update_slice_in_dim(got, shard, chip * kind[2], axis=kind[1] + 1)
    slots = full.pop("w_in")
    full["w_in_perm"] = _permuted_from_shards([slots[q] for q in range(N_CHIPS)]).astype(_MXU)
    return full


def _reduce_and_update(grads, w, m, v, loss_local):
    sh_names = [n for n, _ in SHARDED]
    sh_kinds = [k for _, k in SHARDED]
    owners = [SMALL_OWNER[n] for n in REPLICATED + ("loss",)]

    def small_views(d):
        return [_small_view(d[n]) for n in REPLICATED]

    grads["w_in"] = jnp.stack([jnp.stack(_shards_from_groups(pieces)) for pieces in grads["w_in_pieces"]])
    gs = [grads[n] if n == "dn_conv_w" else grads[n].astype(_COMM) for n in sh_names]
    sm = small_views(grads) + [_small_view(loss_local.reshape(1))]
    core = lax.axis_index("c")
    chip = 2 * lax.axis_index("x") + lax.axis_index("y")
    got, sm_got = exchange_halves(gs, sm, owners, "exchange_halves")
    sums = add_pairs([lax.dynamic_index_in_dim(g, core, 0, keepdims=False) for g in gs] + sm, list(got) + list(sm_got),
                     "add_halves")
    sums, sm_sums = sums[:len(gs)], sums[len(gs):]
    parts, sm_parts = reduce_to_chips(sums, sh_kinds, sm_sums, owners, "reduce_to_chips")
    parts = list(parts)
    for k, (kind, total) in enumerate(zip(sh_kinds, sums)):
        if kind[0] == "slot":
            own = lax.dynamic_index_in_dim(total, chip, 0, keepdims=False)
        else:
            own = lax.dynamic_slice_in_dim(total, chip * kind[2], kind[2], axis=kind[1])
        parts[k] = lax.dynamic_update_index_in_dim(parts[k], own, chip, 0)
    half = core.astype(jnp.int32).reshape(1)
    totals = [sum_parts(half, part, f"sum_{n}") for n, part in zip(sh_names, parts)]
    sm_totals = sum_parts_small(chip.astype(jnp.int32).reshape(1), sm_parts, sm_sums, "sum_replicated")
    g_big, g_small = share_halves(totals, sm_totals, owners, "share_halves")
    outs = [dict(zip(sh_names, g_big)), {}, {}, {}]
    for n, g in zip(sh_names, g_big):
        outs[1][n], outs[2][n], outs[3][n] = adamw(g, w[n], m[n], v[n], f"adamw_{n}")
    small_results = [g_small[:-1]] + adamw_small(g_small[:-1], small_views(w), small_views(m), small_views(v),
                                                 "adamw_replicated")
    for j in range(4):
        for n, r in zip(REPLICATED, small_results[j]):
            outs[j][n] = r.reshape(-1)[:w[n].size].reshape(w[n].shape)
    return outs, g_small[-1][0, 0]
```

```python
import functools

import jax
import jax.numpy as jnp
from jax import lax
from jax.experimental import pallas as pl
from jax.experimental.pallas import tpu as pltpu

f32 = jnp.float32
bf16 = jnp.bfloat16

_MXU = bf16
_COMM = bf16
HIGH = lax.Precision.HIGH

D_MODEL = 1024
DEPTH = 2
D_PLE = 256
D_SSM = 256
D_DN = 512
D_SG = 256
SSM_GROUPS = 16
SSM_GROUP = 16
SSM_STATE = 64
N_STATE = SSM_GROUPS * SSM_STATE
DN_HEADS = 4
DN_HEAD_DIM = 128
DN_CONV = 4
DN_HALO = 16
DN_CHUNK = 64
SG_HEADS = 4
SG_HEAD_DIM = 64
SG_CHUNK = 128
S5_CHUNK = 512
S5_GROUP_ROWS = 8
EPS = 1e-6
D_IN = 3336
D_IN_PAD = 3456
LANE = 128

ADAM_LR = 0.001
ADAM_B1 = 0.9
ADAM_B2 = 0.999
ADAM_EPS = 1e-08
ADAM_WD = 0.01
ADAM_STEP = 10

N_CHIPS = 4
N_DEV = 8

Z_COLS = ((0, 512), (512, 2048), (2048, 2560), (2560, 3328), (3328, 3456))

GROUP_COLS = ((0, 512), (512, 2048), (2056, 2568), (2568, 3336), (2048, 2056))
SHARD_COLS = D_IN // 4

SHARDED = (("w_in", ("slot",)), ("ssm_w_glu", ("win", 0, 64)), ("dn_conv_w", ("win", 1, 384)),
           ("w_out", ("win", 0, 256)), ("w_ple_gate", ("win", 0, 256)), ("w_ple", ("win", 1, 256)))
REPLICATED = ("norm_g", "ssm_a_re", "ssm_a_im", "ssm_b_re", "ssm_b_im", "ssm_c_re", "ssm_c_im", "ssm_d",
              "ssm_log_step", "ssm_b_glu", "dn_a_log", "dn_dt_bias", "dn_norm_g", "sg_ln_g", "sg_ln_b", "sg_w",
              "sg_b", "ple_norm_g", "final_norm_g")
SMALL_OWNER = {n: int(n.startswith("ssm_")) for n in REPLICATED + ("loss",)}
WEIGHTS = ("norm_g", "w_in", "ssm_a_re", "ssm_a_im", "ssm_b_re", "ssm_b_im", "ssm_c_re", "ssm_c_im", "ssm_d",
           "ssm_log_step", "ssm_w_glu", "ssm_b_glu", "dn_conv_w", "dn_a_log", "dn_dt_bias", "dn_norm_g", "sg_ln_g",
           "sg_ln_b", "sg_w", "sg_b", "w_out", "ple_norm_g", "w_ple_gate", "w_ple", "final_norm_g")

VMEM_BIG = 56 * 1024 * 1024


def _mm(a, b):
    return jnp.dot(a.astype(_MXU), b.astype(_MXU), preferred_element_type=f32)


def _mm_nt(a, b):
    return lax.dot_general(a.astype(_MXU), b.astype(_MXU), (((1,), (1,)), ((), ())), preferred_element_type=f32)


def _mm_tn(a, b):
    return lax.dot_general(a.astype(_MXU), b.astype(_MXU), (((0,), (0,)), ((), ())), preferred_element_type=f32)


@jax.custom_vjp
def bdot(a, b):
    return _mm(a, b)


def _bdot_fwd(a, b):
    return _mm(a, b), (a, b)


def _bdot_bwd(res, g):
    a, b = res
    return _mm_nt(g, b).astype(a.dtype), _mm_tn(a, g).astype(b.dtype)


bdot.defvjp(_bdot_fwd, _bdot_bwd)


@jax.custom_vjp
def bdot_nt(a, b):
    return _mm_nt(a, b)


def _bdot_nt_fwd(a, b):
    return _mm_nt(a, b), (a, b)


def _bdot_nt_bwd(res, g):
    a, b = res
    return _mm(g, b).astype(a.dtype), _mm_tn(g, a).astype(b.dtype)


bdot_nt.defvjp(_bdot_nt_fwd, _bdot_nt_bwd)


@jax.custom_vjp
def bdot_tn(a, b):
    return _mm_tn(a, b)


def _bdot_tn_fwd(a, b):
    return _mm_tn(a, b), (a, b)


def _bdot_tn_bwd(res, g):
    a, b = res
    return _mm_nt(b, g).astype(a.dtype), _mm(a, g).astype(b.dtype)


bdot_tn.defvjp(_bdot_tn_fwd, _bdot_tn_bwd)


def hdot(a, b):
    return jnp.dot(a, b, precision=HIGH, preferred_element_type=f32)


def _unit_lower_inverses(ms):
    n = ms[0].shape[0]
    eye = (lax.broadcasted_iota(jnp.int32, (n, n), 0) == lax.broadcasted_iota(jnp.int32, (n, n), 1)).astype(f32)
    pw = [-m for m in ms]
    inv = [eye + p for p in pw]
    for _ in range(n.bit_length() - 2):
        pw = [hdot(p, p) for p in pw]
        inv = [a + hdot(a, p) for a, p in zip(inv, pw)]
    return inv


@jax.custom_vjp
def solve_unit_lower(ms, rhs, inv):
    return [hdot(a, r) for a, r in zip(inv, rhs)]


def _solve_unit_lower_fwd(ms, rhs, inv):
    xs = [hdot(a, r) for a, r in zip(inv, rhs)]
    return xs, (inv, xs)


def _solve_unit_lower_bwd(res, gs):
    inv, xs = res
    d_rhs = [lax.dot_general(a, g, (((0,), (0,)), ((), ())), precision=HIGH, preferred_element_type=f32)
             for a, g in zip(inv, gs)]
    d_ms = [-lax.dot_general(d, x, (((1,), (1,)), ((), ())), precision=HIGH, preferred_element_type=f32)
            for d, x in zip(d_rhs, xs)]
    return d_ms, d_rhs, [jnp.zeros_like(a) for a in inv]


solve_unit_lower.defvjp(_solve_unit_lower_fwd, _solve_unit_lower_bwd)


@functools.partial(jax.custom_vjp, nondiff_argnums=(1,))
def roll_rows(x, k):
    return pltpu.roll(x, k, 0)


def _roll_rows_fwd(x, k):
    return pltpu.roll(x, k, 0), None


def _roll_rows_bwd(k, _, g):
    return (pltpu.roll(g, g.shape[0] - k, 0),)


roll_rows.defvjp(_roll_rows_fwd, _roll_rows_bwd)


def _row_ids(shape):
    return lax.broadcasted_iota(jnp.int32, shape, 0)


def _rms(x, g):
    return x * lax.rsqrt(jnp.mean(x * x, axis=-1, keepdims=True) + EPS) * g


def _layer_norm(x, g, b):
    mu = jnp.mean(x, axis=-1, keepdims=True)
    xc = x - mu
    return xc * lax.rsqrt(jnp.mean(xc * xc, axis=-1, keepdims=True) + EPS) * g + b


def _s5_prep(are, aim, ls, bre, bim):
    step = jnp.exp(ls)
    mag = jnp.exp(are * step)
    lr = mag * jnp.cos(aim * step)
    li = mag * jnp.sin(aim * step)
    den = are * are + aim * aim
    nr, ni = lr - 1.0, li
    fr = (nr * are + ni * aim) / den
    fi = (ni * are - nr * aim) / den
    bbr = fr * bre - fi * bim
    bbi = fr * bim + fi * bre
    pr = jnp.broadcast_to(lr, (S5_GROUP_ROWS, N_STATE))
    pi = jnp.broadcast_to(li, (S5_GROUP_ROWS, N_STATE))
    d = 1
    while d < S5_GROUP_ROWS:
        keep = _row_ids(pr.shape) >= d
        sr, si = roll_rows(pr, d), roll_rows(pi, d)
        pr, pi = jnp.where(keep, pr * sr - pi * si, pr), jnp.where(keep, pr * si + pi * sr, pi)
        d *= 2
    return pr, pi, bbr, bbi


def _s5_chunk(u, gate, hr, hi, pr, pi, bbr, bbi, cr, ci, dr, wglu, bglu):
    n, steps = u.shape[0], S5_GROUP_ROWS
    groups = n // steps
    xr = bdot(u, bbr)
    xi = bdot(u, bbi)
    lr, li = pr[0:1], pi[0:1]
    rs, ims = [xr[:groups]], [xi[:groups]]
    for t in range(1, steps):
        a, b = rs[-1], ims[-1]
        rs.append(xr[t * groups:(t + 1) * groups] + lr * a - li * b)
        ims.append(xi[t * groups:(t + 1) * groups] + lr * b + li * a)
    er, ei = rs[-1], ims[-1]
    mr, mi = pr[steps - 1:steps], pi[steps - 1:steps]
    gid = _row_ids(er.shape)
    er, ei = (er + jnp.where(gid == 0, mr * hr - mi * hi, 0.0), ei + jnp.where(gid == 0, mr * hi + mi * hr, 0.0))
    d = 1
    while d < groups:
        sr = jnp.where(gid >= d, roll_rows(er, d), 0.0)
        si = jnp.where(gid >= d, roll_rows(ei, d), 0.0)
        er, ei = er + mr * sr - mi * si, ei + mr * si + mi * sr
        mr, mi = mr * mr - mi * mi, 2.0 * mr * mi
        d *= 2
    before_r = jnp.where(gid == 0, hr, roll_rows(er, 1))
    before_i = jnp.where(gid == 0, hi, roll_rows(ei, 1))
    xr = jnp.concatenate([rs[t] + pr[t:t + 1] * before_r - pi[t:t + 1] * before_i for t in range(steps)], axis=0)
    xi = jnp.concatenate([ims[t] + pr[t:t + 1] * before_i + pi[t:t + 1] * before_r for t in range(steps)], axis=0)
    y = bdot(xr, cr) - bdot(xi, ci) + dr * u
    y = jax.nn.gelu(y)
    y = y * jax.nn.sigmoid(bdot(y, wglu) + bglu)
    return y * jax.nn.silu(gate), er[groups - 1:groups], ei[groups - 1:groups]


def _dn_pre(xc, xp, w0, w1, w2, w3, is_start, col):
    xp = jnp.where(is_start, 0.0, xp)
    halo_rows = _row_ids(xp.shape)
    acc = w3 * xc
    for d, w in ((1, w2), (2, w1), (3, w0)):
        r = roll_rows(xc, d)
        head = jnp.where(halo_rows >= d, r[:DN_HALO], roll_rows(xp, d))
        acc = acc + w * jnp.concatenate([head, r[DN_HALO:]], axis=0)
    y = jax.nn.silu(acc)
    nrm = y * lax.rsqrt(jnp.sum(y * y, axis=-1, keepdims=True) + EPS)
    nrm = nrm * jnp.where(col < DN_HEADS, DN_HEAD_DIM ** -0.5, 1.0)
    return jnp.where(col < 2 * DN_HEADS, nrm, y)


def _dn_local(qs, ks, vs, abs_, alog, dtb, invs=None):
    c = DN_CHUNK
    ri = lax.broadcasted_iota(jnp.int32, (c, c), 0)
    ci = lax.broadcasted_iota(jnp.int32, (c, c), 1)
    causal, strict = ri >= ci, ri > ci
    tril = causal.astype(f32)
    gcums = [hdot(tril, -jnp.exp(alog) * jax.nn.softplus(ab + dtb)) for ab in abs_]
    gcum_ts = [g.T for g in gcums]
    sigs = [jax.nn.sigmoid(ab) for ab in abs_]
    chains = [(j, h) for j in range(len(abs_)) for h in range(DN_HEADS)]
    gc = [gcums[j][:, h:h + 1] for j, h in chains]
    decay = [jnp.where(causal, jnp.exp(jnp.where(causal, gc[n] - gcum_ts[j][h:h + 1, :], 0.0)), 0.0)
             for n, (j, h) in enumerate(chains)]
    beta = [sigs[j][:, DN_HEADS + h:DN_HEADS + h + 1] for j, h in chains]
    kb = [ks[j][h] * beta[n] for n, (j, h) in enumerate(chains)]
    ms = [jnp.where(strict, bdot_nt(kb[n], ks[j][h]) * decay[n], 0.0) for n, (j, h) in enumerate(chains)]
    egc = [jnp.exp(g) for g in gc]
    rhs = [jnp.concatenate([vs[j][h] * beta[n], kb[n] * egc[n]], axis=1) for n, (j, h) in enumerate(chains)]
    inv = _unit_lower_inverses(ms) if invs is None else [invs[j][h] for j, h in chains]
    sol = solve_unit_lower(ms, rhs, inv)
    values = [s[:, :DN_HEAD_DIM] for s in sol]
    k_cds = [s[:, DN_HEAD_DIM:] for s in sol]
    attns = [bdot_nt(qs[j][h], ks[j][h]) * decay[n] for n, (j, h) in enumerate(chains)]
    q_decs = [qs[j][h] * egc[n] for n, (j, h) in enumerate(chains)]
    k_decs = [ks[j][h] * jnp.exp(gc[n][c - 1:c, :] - gc[n]) for n, (j, h) in enumerate(chains)]

    def nest(flat):
        return [flat[j * DN_HEADS:(j + 1) * DN_HEADS] for j in range(len(abs_))]

    lasts = [jnp.exp(g[c - 1:c, :]) for g in gcums]
    return nest(values), nest(k_cds), nest(attns), nest(q_decs), nest(k_decs), lasts, nest(inv)


def _dn_step(values, k_cds, attns, q_decs, k_decs, lasts, ggs, sts, ng):
    v_new = [v - bdot(kc, st) for v, kc, st in zip(values, k_cds, sts)]
    o = [bdot(qd, st) for qd, st in zip(q_decs, sts)]
    o = [a + bdot(at, vn) for a, at, vn in zip(o, attns, v_new)]
    new = [st * la + bdot_tn(kd, vn) for st, la, kd, vn in zip(sts, lasts, k_decs, v_new)]
    return [_rms(a, ng) * jax.nn.silu(g) for a, g in zip(o, ggs)], new


def _sg_chunk(u, v, gate, lng, lnb, ws, bt):
    n = SG_CHUNK
    ug = jax.nn.gelu(u)
    vn = _layer_norm(jax.nn.gelu(v), lng, lnb)
    causal = lax.broadcasted_iota(jnp.int32, (n, n), 0) >= lax.broadcasted_iota(jnp.int32, (n, n), 1)
    lane = lax.broadcasted_iota(jnp.int32, (n, D_SG), 1)
    s = jnp.zeros((n, D_SG), f32)
    for h in range(SG_HEADS):
        t = bdot(jnp.where(causal, ws[h], 0.0), vn) + bt[:, h:h + 1]
        s = s + jnp.where((lane >= h * SG_HEAD_DIM) & (lane < (h + 1) * SG_HEAD_DIM), t, 0.0)
    return ug * s * jax.nn.silu(gate)


def _cp(n_grid, vmem=None):
    return pltpu.CompilerParams(dimension_semantics=("arbitrary",) * n_grid, vmem_limit_bytes=vmem)


def _full(shape):
    nd = len(shape)
    return pl.BlockSpec(tuple(shape), lambda *_: (0,) * nd)


def _rows(tm, ncol):
    return pl.BlockSpec((tm, ncol), lambda i: (i, 0))


def _sds(shape, dtype=f32):
    return jax.ShapeDtypeStruct(tuple(shape), dtype)


def _acc(ref, val, first):
    @pl.when(first)
    def _():
        ref[...] = val

    @pl.when(jnp.logical_not(first))
    def _():
        ref[...] += val


def in_fwd(x, g, w, name):
    t, tm = x.shape[0], 256

    def body(x_ref, g_ref, w_ref, h_ref, *z_refs):
        h = _rms(x_ref[...], g_ref[...]).astype(_MXU)
        h_ref[...] = h
        for z_ref, (a, b) in zip(z_refs, Z_COLS):
            z_ref[...] = jnp.dot(h, w_ref[:, a:b], preferred_element_type=f32)

    widths = [b - a for a, b in Z_COLS]
    return pl.pallas_call(
        body, name=name, grid=(t // tm,),
        in_specs=[_rows(tm, D_MODEL), _full((1, D_MODEL)), _full((D_MODEL, D_IN_PAD))],
        out_specs=[_rows(tm, D_MODEL)] + [_rows(tm, n) for n in widths],
        out_shape=[_sds((t, D_MODEL), _MXU)] + [_sds((t, n)) for n in widths],
        compiler_params=_cp(1, VMEM_BIG),
    )(x, g, w)


def in_bwd(x, g, w, dzs, dres, name):
    t, tm = x.shape[0], 256
    widths = [b - a for a, b in Z_COLS]

    def body(x_ref, g_ref, w_ref, dres_ref, *rest):
        dz_refs, (dx_ref, dg_ref) = rest[:5], rest[5:]
        dh = jnp.zeros((tm, D_MODEL), f32)
        for dz_ref, (a, b) in zip(dz_refs, Z_COLS):
            dh = dh + _mm_nt(dz_ref[...], w_ref[:, a:b])
        _, vj = jax.vjp(_rms, x_ref[...], g_ref[...])
        dx, dg = vj(dh)
        dx_ref[...] = dres_ref[...] + dx
        _acc(dg_ref, dg, pl.program_id(0) == 0)

    return pl.pallas_call(
        body, name=name, grid=(t // tm,),
        in_specs=[_rows(tm, D_MODEL), _full((1, D_MODEL)), _full((D_MODEL, D_IN_PAD)), _rows(tm, D_MODEL)]
        + [_rows(tm, n) for n in widths],
        out_specs=[_rows(tm, D_MODEL), _full((1, D_MODEL))],
        out_shape=[_sds((t, D_MODEL)), _sds((1, D_MODEL))],
        compiler_params=_cp(1, VMEM_BIG),
    )(x, g, w, dres, *dzs)


def wgrad(a, g, name):
    t, k = a.shape
    n = g.shape[1]
    tm = min(t, 2048)
    tn = n if n <= 768 else (768 if n % 768 == 0 else 512)
    steps = t // tm

    def body(a_ref, g_ref, o_ref, acc):
        i = pl.program_id(1)
        _acc(acc, _mm_tn(a_ref[...], g_ref[...]), i == 0)

        @pl.when(i == steps - 1)
        def _():
            o_ref[...] = acc[...].astype(o_ref.dtype)

    return pl.pallas_call(
        body, name=name, grid=(n // tn, steps),
        in_specs=[pl.BlockSpec((tm, k), lambda j, i: (i, 0)), pl.BlockSpec((tm, tn), lambda j, i: (i, j))],
        out_specs=pl.BlockSpec((k, tn), lambda j, i: (0, j)),
        out_shape=_sds((k, n), _COMM),
        scratch_shapes=[pltpu.VMEM((k, tn), f32)],
        compiler_params=_cp(2, VMEM_BIG),
    )(a, g)


def post_fwd(x, ys, yd, yg, p, wout, pg, wgate, wple, name):
    t, tm = x.shape[0], 256

    def body(x_ref, ys_ref, yd_ref, yg_ref, p_ref, wout_ref, pg_ref, wgate_ref, wple_ref,
             x2_ref, x1_ref, y_ref, hn_ref):
        y = jnp.concatenate([ys_ref[...], yd_ref[...], yg_ref[...]], axis=1).astype(_MXU)
        y_ref[...] = y
        x1 = x_ref[...] + jnp.dot(y, wout_ref[...], preferred_element_type=f32)
        x1_ref[...] = x1
        hn = _rms(x1, pg_ref[...]).astype(_MXU)
        hn_ref[...] = hn
        gp = jnp.dot(hn, wgate_ref[...], preferred_element_type=f32)
        pp = _mm(p_ref[...], wple_ref[...])
        x2_ref[...] = x1 + jax.nn.sigmoid(gp) * pp

    return pl.pallas_call(
        body, name=name, grid=(t // tm,),
        in_specs=[_rows(tm, D_MODEL), _rows(tm, D_SSM), _rows(tm, D_DN), _rows(tm, D_SG), _rows(tm, D_PLE),
                  _full((D_MODEL, D_MODEL)), _full((1, D_MODEL)), _full((D_MODEL, D_MODEL)), _full((D_PLE, D_MODEL))],
        out_specs=[_rows(tm, D_MODEL)] * 4,
        out_shape=[_sds((t, D_MODEL)), _sds((t, D_MODEL)), _sds((t, D_MODEL), _MXU), _sds((t, D_MODEL), _MXU)],
        compiler_params=_cp(1, VMEM_BIG),
    )(x, ys, yd, yg, p, wout, pg, wgate, wple)


def post_bwd(dx2, x1, hn, p, wout, pg, wgate, wple, name):
    t, tm = dx2.shape[0], 256

    def body(dx2_ref, x1_ref, hn_ref, p_ref, wout_ref, pg_ref, wgate_ref, wple_ref,
             dx1_ref, dgp_ref, dpp_ref, dys_ref, dyd_ref, dyg_ref, dpg_ref):
        dx2 = dx2_ref[...]
        gp = jnp.dot(hn_ref[...], wgate_ref[...], preferred_element_type=f32)
        pp = _mm(p_ref[...], wple_ref[...])
        sg = jax.nn.sigmoid(gp)
        dpp_ref[...] = (dx2 * sg).astype(_MXU)
        dgp = (dx2 * pp * sg * (1.0 - sg)).astype(_MXU)
        dgp_ref[...] = dgp
        dhn = _mm_nt(dgp, wgate_ref[...])
        _, vj = jax.vjp(_rms, x1_ref[...], pg_ref[...])
        dx1n, dpg = vj(dhn)
        dx1 = dx2 + dx1n
        dx1_ref[...] = dx1
        dy = _mm_nt(dx1, wout_ref[...])
        dys_ref[...] = dy[:, :D_SSM]
        dyd_ref[...] = dy[:, D_SSM:D_SSM + D_DN]
        dyg_ref[...] = dy[:, D_SSM + D_DN:]
        _acc(dpg_ref, dpg, pl.program_id(0) == 0)

    return pl.pallas_call(
        body, name=name, grid=(t // tm,),
        in_specs=[_rows(tm, D_MODEL), _rows(tm, D_MODEL), _rows(tm, D_MODEL), _rows(tm, D_PLE),
                  _full((D_MODEL, D_MODEL)), _full((1, D_MODEL)), _full((D_MODEL, D_MODEL)), _full((D_PLE, D_MODEL))],
        out_specs=[_rows(tm, D_MODEL), _rows(tm, D_MODEL), _rows(tm, D_MODEL), _rows(tm, D_SSM), _rows(tm, D_DN),
                   _rows(tm, D_SG), _full((1, D_MODEL))],
        out_shape=[_sds((t, D_MODEL)), _sds((t, D_MODEL), _MXU), _sds((t, D_MODEL), _MXU), _sds((t, D_SSM)),
                   _sds((t, D_DN)), _sds((t, D_SG)), _sds((1, D_MODEL))],
        compiler_params=_cp(1, VMEM_BIG),
    )(dx2, x1, hn, p, wout, pg, wgate, wple)


def loss_fwd_bwd(x, fg, target, name):
    t, tm = x.shape[0], 512

    def body(x_ref, fg_ref, t_ref, loss_ref, dx_ref, dfg_ref):
        def f(xv, gv):
            err = _rms(xv, gv) - t_ref[...]
            return 0.5 * jnp.sum(jnp.mean(err * err, axis=-1))

        val, vj = jax.vjp(f, x_ref[...], fg_ref[...])
        dx, dfg = vj(jnp.ones((), f32))
        dx_ref[...] = dx
        first = pl.program_id(0) == 0
        _acc(dfg_ref, dfg, first)
        _acc(loss_ref, jnp.full((8, LANE), val, f32), first)

    return pl.pallas_call(
        body, name=name, grid=(t // tm,),
        in_specs=[_rows(tm, D_MODEL), _full((1, D_MODEL)), _rows(tm, D_MODEL)],
        out_specs=[_full((8, LANE)), _rows(tm, D_MODEL), _full((1, D_MODEL))],
        out_shape=[_sds((8, LANE)), _sds((t, D_MODEL)), _sds((1, D_MODEL))],
        compiler_params=_cp(1),
    )(x, fg, target)


S5_PREPARED = 4
_S5_PARAM_SHAPES = ((S5_GROUP_ROWS, N_STATE), (S5_GROUP_ROWS, N_STATE), (D_SSM, N_STATE), (D_SSM, N_STATE),
                    (N_STATE, D_SSM), (N_STATE, D_SSM), (1, D_SSM), (D_SSM, D_SSM), (1, D_SSM))

def s5_prep_fwd(are, aim, ls, bre, bim, name):
    def body(are_ref, aim_ref, ls_ref, bre_ref, bim_ref, *outs):
        vals = _s5_prep(are_ref[...], aim_ref[...], ls_ref[...], bre_ref[...], bim_ref[...])
        for o, v in zip(outs, vals):
            o[...] = v

    return pl.pallas_call(body, name=name, out_shape=[_sds(s) for s in _S5_PARAM_SHAPES[:S5_PREPARED]])(
        are, aim, ls, bre, bim)


def s5_prep_bwd(are, aim, ls, bre, bim, cts, name):
    def body(are_ref, aim_ref, ls_ref, bre_ref, bim_ref, *rest):
        ct_refs, outs = rest[:S5_PREPARED], rest[S5_PREPARED:]
        _, vj = jax.vjp(_s5_prep, are_ref[...], aim_ref[...], ls_ref[...], bre_ref[...], bim_ref[...])
        for o, v in zip(outs, vj(tuple(r[...] for r in ct_refs))):
            o[...] = v

    shapes = [(1, N_STATE)] * 3 + [(D_SSM, N_STATE)] * 2
    return pl.pallas_call(body, name=name, out_shape=[_sds(s) for s in shapes])(are, aim, ls, bre, bim, *cts)


def _step_major(ref, cols):
    x = ref[:, cols]
    n, w = x.shape
    return jnp.swapaxes(x.reshape(n // S5_GROUP_ROWS, S5_GROUP_ROWS, w), 0, 1).reshape(n, w)


def _store_step_major(ref, cols, val):
    n, w = val.shape
    ref[:, cols] = jnp.swapaxes(val.reshape(S5_GROUP_ROWS, n // S5_GROUP_ROWS, w), 0, 1).reshape(n, w).astype(ref.dtype)


def s5_fwd(z, params, nb, name):
    t = z.shape[0]
    nc = t // nb // S5_CHUNK
    npar = len(_S5_PARAM_SHAPES)

    def body(z_ref, *rest):
        p_refs, (y_ref, hs_ref, hr_s, hi_s) = rest[:npar], rest[npar:]

        @pl.when(pl.program_id(1) == 0)
        def _():
            hr_s[...] = jnp.zeros_like(hr_s)
            hi_s[...] = jnp.zeros_like(hi_s)

        hr, hi = hr_s[...], hi_s[...]
        hs_ref[0, :, :N_STATE] = hr
        hs_ref[0, :, N_STATE:] = hi
        y, nhr, nhi = _s5_chunk(_step_major(z_ref, slice(0, D_SSM)), _step_major(z_ref, slice(D_SSM, 2 * D_SSM)),
                                hr, hi, *[r[...] for r in p_refs])
        _store_step_major(y_ref, slice(0, D_SSM), y)
        hr_s[...] = nhr
        hi_s[...] = nhi

    return pl.pallas_call(
        body, name=name, grid=(nb, nc),
        in_specs=[pl.BlockSpec((S5_CHUNK, 2 * D_SSM), lambda b, c: (b * nc + c, 0))]
        + [_full(s) for s in _S5_PARAM_SHAPES],
        out_specs=[pl.BlockSpec((S5_CHUNK, D_SSM), lambda b, c: (b * nc + c, 0)),
                   pl.BlockSpec((1, 1, 2 * N_STATE), lambda b, c: (b * nc + c, 0, 0))],
        out_shape=[_sds((t, D_SSM)), _sds((nb * nc, 1, 2 * N_STATE))],
        scratch_shapes=[pltpu.VMEM((1, N_STATE), f32), pltpu.VMEM((1, N_STATE), f32)],
        compiler_params=_cp(2, VMEM_BIG),
    )(z, *params)


def s5_bwd(z, params, hs, dy, nb, name):
    t = z.shape[0]
    nc = t // nb // S5_CHUNK
    npar = len(_S5_PARAM_SHAPES)

    def body(z_ref, hs_ref, dy_ref, *rest):
        p_refs, dz_ref, dp_refs, (dhr_s, dhi_s) = rest[:npar], rest[npar], rest[npar + 1:2 * npar + 1], rest[2 * npar + 1:]

        @pl.when(pl.program_id(1) == 0)
        def _():
            dhr_s[...] = jnp.zeros_like(dhr_s)
            dhi_s[...] = jnp.zeros_like(dhi_s)

        prim = (_step_major(z_ref, slice(0, D_SSM)), _step_major(z_ref, slice(D_SSM, 2 * D_SSM)),
                hs_ref[0, :, :N_STATE], hs_ref[0, :, N_STATE:]) + tuple(r[...] for r in p_refs)
        _, vj = jax.vjp(_s5_chunk, *prim)
        cts = vj((_step_major(dy_ref, slice(0, D_SSM)), dhr_s[...], dhi_s[...]))
        _store_step_major(dz_ref, slice(0, D_SSM), cts[0])
        _store_step_major(dz_ref, slice(D_SSM, 2 * D_SSM), cts[1])
        dhr_s[...] = cts[2]
        dhi_s[...] = cts[3]
        first = (pl.program_id(0) == 0) & (pl.program_id(1) == 0)
        for r, v in zip(dp_refs, cts[4:]):
            _acc(r, v, first)

    rev = lambda b, c: (b * nc + nc - 1 - c, 0)
    return pl.pallas_call(
        body, name=name, grid=(nb, nc),
        in_specs=[pl.BlockSpec((S5_CHUNK, 2 * D_SSM), rev),
                  pl.BlockSpec((1, 1, 2 * N_STATE), lambda b, c: (b * nc + nc - 1 - c, 0, 0)),
                  pl.BlockSpec((S5_CHUNK, D_SSM), rev)] + [_full(s) for s in _S5_PARAM_SHAPES],
        out_specs=[pl.BlockSpec((S5_CHUNK, 2 * D_SSM), rev)] + [_full(s) for s in _S5_PARAM_SHAPES],
        out_shape=[_sds((t, 2 * D_SSM), _MXU)] + [_sds(s) for s in _S5_PARAM_SHAPES],
        scratch_shapes=[pltpu.VMEM((1, N_STATE), f32), pltpu.VMEM((1, N_STATE), f32)],
        compiler_params=_cp(2, VMEM_BIG),
    )(z, hs, dy, *params)


DN_PRE_ROWS = 256
DN_COLS = 3 * D_DN // LANE


def dn_pre_fwd(zq, convw, seq, name):
    t, tb = zq.shape[0], DN_PRE_ROWS
    per_seq = seq // tb

    def body(xc_ref, xp_ref, w_ref, o_ref):
        is_start = pl.program_id(0) % per_seq == 0
        for j in range(DN_COLS):
            cols = slice(j * LANE, (j + 1) * LANE)
            o_ref[:, cols] = _dn_pre(xc_ref[:, cols], xp_ref[:, cols], w_ref[0:1, cols], w_ref[1:2, cols],
                                     w_ref[2:3, cols], w_ref[3:4, cols], is_start, j)

    return pl.pallas_call(
        body, name=name, grid=(t // tb,),
        in_specs=[_rows(tb, 3 * D_DN),
                  pl.BlockSpec((DN_HALO, 3 * D_DN), lambda i: (jnp.maximum(i * (tb // DN_HALO) - 1, 0), 0)),
                  _full((DN_CONV, 3 * D_DN))],
        out_specs=_rows(tb, 3 * D_DN),
        out_shape=_sds((t, 3 * D_DN)),
        compiler_params=_cp(1, VMEM_BIG),
    )(zq, zq, convw)


def dn_pre_bwd(zq, convw, dqkv, seq, name):
    t, tb = zq.shape[0], DN_PRE_ROWS
    nrow = t // tb
    per_seq = seq // tb

    def body(xc_ref, xp_ref, w_ref, d_ref, dx_ref, dw_ref, carry):
        step = pl.program_id(0)
        i = nrow - 1 - step

        @pl.when(step == 0)
        def _():
            carry[...] = jnp.zeros_like(carry)

        for j in range(DN_COLS):
            cols = slice(j * LANE, (j + 1) * LANE)
            fn = functools.partial(_dn_pre, is_start=i % per_seq == 0, col=j)
            _, vj = jax.vjp(fn, xc_ref[:, cols], xp_ref[:, cols], w_ref[0:1, cols], w_ref[1:2, cols],
                            w_ref[2:3, cols], w_ref[3:4, cols])
            dxc, dxp, dw0, dw1, dw2, dw3 = vj(d_ref[:, cols])
            dx_ref[:tb - DN_HALO, cols] = dxc[:tb - DN_HALO].astype(_MXU)
            dx_ref[tb - DN_HALO:, cols] = (dxc[tb - DN_HALO:] + carry[:, cols]).astype(_MXU)
            carry[:, cols] = dxp
            for k, dw in enumerate((dw0, dw1, dw2, dw3)):
                @pl.when(step == 0)
                def _():
                    dw_ref[k:k + 1, cols] = dw

                @pl.when(step != 0)
                def _():
                    dw_ref[k:k + 1, cols] += dw

    rev = lambda s: (nrow - 1 - s, 0)
    return pl.pallas_call(
        body, name=name, grid=(nrow,),
        in_specs=[pl.BlockSpec((tb, 3 * D_DN), rev),
                  pl.BlockSpec((DN_HALO, 3 * D_DN),
                               lambda s: (jnp.maximum((nrow - 1 - s) * (tb // DN_HALO) - 1, 0), 0)),
                  _full((DN_CONV, 3 * D_DN)), pl.BlockSpec((tb, 3 * D_DN), rev)],
        out_specs=[pl.BlockSpec((tb, 3 * D_DN), rev), _full((DN_CONV, 3 * D_DN))],
        out_shape=[_sds((t, 3 * D_DN), _MXU), _sds((DN_CONV, 3 * D_DN))],
        scratch_shapes=[pltpu.VMEM((DN_HALO, 3 * D_DN), f32)],
        compiler_params=_cp(1, VMEM_BIG),
    )(zq, zq, convw, dqkv)


DN_LOCAL_CHUNKS = 2
DN_ATTN = DN_HEADS * DN_CHUNK


def _dn_heads(ref, rows, base=0):
    return [ref[rows, base + h * DN_HEAD_DIM:base + (h + 1) * DN_HEAD_DIM] for h in range(DN_HEADS)]


def dn_local_fwd(qkv, ab, alog, dtb, name):
    t = qkv.shape[0]
    c, n = DN_CHUNK, DN_LOCAL_CHUNKS

    def body(qkv_ref, ab_ref, alog_ref, dtb_ref, val_ref, kcd_ref, attn_ref, qd_ref, kd_ref, el_ref, inv_ref):
        rows = [pl.ds(j * c, c) for j in range(n)]
        vals, kcds, attns, qds, kds, els, invs = _dn_local(
            [_dn_heads(qkv_ref, r) for r in rows], [_dn_heads(qkv_ref, r, D_DN) for r in rows],
            [_dn_heads(qkv_ref, r, 2 * D_DN) for r in rows], [ab_ref[r, :] for r in rows], alog_ref[...], dtb_ref[...])
        for j, r in enumerate(rows):
            for h in range(DN_HEADS):
                lo, hi = h * DN_HEAD_DIM, (h + 1) * DN_HEAD_DIM
                val_ref[r, lo:hi] = vals[j][h]
                kcd_ref[r, lo:hi] = kcds[j][h].astype(_MXU)
                qd_ref[r, lo:hi] = qds[j][h].astype(_MXU)
                kd_ref[r, lo:hi] = kds[j][h].astype(_MXU)
                attn_ref[r, h * c:(h + 1) * c] = attns[j][h].astype(_MXU)
                inv_ref[r, h * c:(h + 1) * c] = invs[j][h]
            el_ref[j] = els[j]

    wide = _rows(n * c, D_DN)
    outs = pl.pallas_call(
        body, name=name, grid=(t // (n * c),),
        in_specs=[_rows(n * c, 3 * D_DN), _rows(n * c, LANE), _full((1, LANE)), _full((1, LANE))],
        out_specs=[wide, wide, _rows(n * c, DN_ATTN), wide, wide, pl.BlockSpec((n, 1, LANE), lambda i: (i, 0, 0)),
                   _rows(n * c, DN_ATTN)],
        out_shape=[_sds((t, D_DN)), _sds((t, D_DN), _MXU), _sds((t, DN_ATTN), _MXU), _sds((t, D_DN), _MXU),
                   _sds((t, D_DN), _MXU), _sds((t // c, 1, LANE)), _sds((t, DN_ATTN))],
        compiler_params=_cp(1),
    )(qkv, ab, alog, dtb)
    return outs[:6], outs[6]


def dn_local_bwd(qkv, ab, alog, dtb, inv, cts, name):
    t = qkv.shape[0]
    c, n = DN_CHUNK, DN_LOCAL_CHUNKS

    def body(qkv_ref, ab_ref, alog_ref, dtb_ref, inv_ref, dval_ref, dkcd_ref, dattn_ref, dqd_ref, dkd_ref, del_ref,
             dqkv_ref, dab_ref, dalog_ref, ddtb_ref):
        rows = [pl.ds(j * c, c) for j in range(n)]
        invs = [[inv_ref[r, h * c:(h + 1) * c] for h in range(DN_HEADS)] for r in rows]

        def local(qs, ks, vs, abs_, alog, dtb):
            return _dn_local(qs, ks, vs, abs_, alog, dtb, invs)[:6]

        _, vj = jax.vjp(local, [_dn_heads(qkv_ref, r) for r in rows], [_dn_heads(qkv_ref, r, D_DN) for r in rows],
                        [_dn_heads(qkv_ref, r, 2 * D_DN) for r in rows], [ab_ref[r, :] for r in rows], alog_ref[...],
                        dtb_ref[...])
        dattn = [[dattn_ref[r, h * c:(h + 1) * c] for h in range(DN_HEADS)] for r in rows]
        dq, dk, dv, dab, dalog, ddtb = vj(([_dn_heads(dval_ref, r) for r in rows], [_dn_heads(dkcd_ref, r) for r in rows],
                                           dattn, [_dn_heads(dqd_ref, r) for r in rows],
                                           [_dn_heads(dkd_ref, r) for r in rows], [del_ref[j] for j in range(n)]))
        for j, r in enumerate(rows):
            for h in range(DN_HEADS):
                lo, hi = h * DN_HEAD_DIM, (h + 1) * DN_HEAD_DIM
                dqkv_ref[r, lo:hi] = dq[j][h]
                dqkv_ref[r, D_DN + lo:D_DN + hi] = dk[j][h]
                dqkv_ref[r, 2 * D_DN + lo:2 * D_DN + hi] = dv[j][h]
            dab_ref[r, :] = dab[j].astype(_MXU)
        first = pl.program_id(0) == 0
        _acc(dalog_ref, dalog, first)
        _acc(ddtb_ref, ddtb, first)

    wide = _rows(n * c, D_DN)
    return pl.pallas_call(
        body, name=name, grid=(t // (n * c),),
        in_specs=[_rows(n * c, 3 * D_DN), _rows(n * c, LANE), _full((1, LANE)), _full((1, LANE)),
                  _rows(n * c, DN_ATTN), wide, wide, _rows(n * c, DN_ATTN), wide, wide,
                  pl.BlockSpec((n, 1, LANE), lambda i: (i, 0, 0))],
        out_specs=[_rows(n * c, 3 * D_DN), _rows(n * c, LANE), _full((1, LANE)), _full((1, LANE))],
        out_shape=[_sds((t, 3 * D_DN)), _sds((t, LANE), _MXU), _sds((1, LANE)), _sds((1, LANE))],
        compiler_params=_cp(1),
    )(qkv, ab, alog, dtb, inv, *cts)


def _seq_view(a, nb):
    return a.reshape((nb, a.shape[0] // nb) + a.shape[1:])


def _dn_chains(nb):
    return [(b, h) for b in range(nb) for h in range(DN_HEADS)]


DN_SCAN_CHUNKS = 4


def _dn_step_operands(val_ref, kcd_ref, attn_ref, qd_ref, kd_ref, el_ref, gg_ref, nb, j):
    chains = _dn_chains(nb)
    c = DN_CHUNK
    rows = pl.ds(j * c, c)

    def wide(ref):
        return [ref[b, rows, h * DN_HEAD_DIM:(h + 1) * DN_HEAD_DIM].astype(f32) for b, h in chains]

    attns = [attn_ref[b, rows, h * c:(h + 1) * c].astype(f32) for b, h in chains]
    return (wide(val_ref), wide(kcd_ref), attns, wide(qd_ref), wide(kd_ref),
            [el_ref[b, j, :, h:h + 1] for b, h in chains], wide(gg_ref))


def dn_scan_fwd(loc, gg, ng, nb, name):
    val, kcd, attn, qd, kd, el = loc
    t = val.shape[0]
    c, n = DN_CHUNK, DN_SCAN_CHUNKS
    nc = t // nb // c
    ns = nb * DN_HEADS

    def body(val_ref, kcd_ref, attn_ref, qd_ref, kd_ref, el_ref, gg_ref, ng_ref, y_ref, ss_ref, st):
        @pl.when(pl.program_id(0) == 0)
        def _():
            st[...] = jnp.zeros_like(st)

        sts = [st[i] for i in range(ns)]
        for j in range(n):
            for i in range(ns):
                ss_ref[j, i] = sts[i]
            ys, sts = _dn_step(*_dn_step_operands(val_ref, kcd_ref, attn_ref, qd_ref, kd_ref, el_ref, gg_ref, nb, j),
                               sts, ng_ref[...])
            for i, (b, h) in enumerate(_dn_chains(nb)):
                y_ref[b, pl.ds(j * c, c), h * DN_HEAD_DIM:(h + 1) * DN_HEAD_DIM] = ys[i]
        for i in range(ns):
            st[i] = sts[i]

    def blk(w):
        return pl.BlockSpec((nb, n * c, w), lambda k: (0, k, 0))

    el_spec = pl.BlockSpec((nb, n, 1, LANE), lambda k: (0, k, 0, 0))
    y, ss = pl.pallas_call(
        body, name=name, grid=(nc // n,),
        in_specs=[blk(D_DN), blk(D_DN), blk(DN_ATTN), blk(D_DN), blk(D_DN), el_spec, blk(D_DN), _full((1, LANE))],
        out_specs=[blk(D_DN), pl.BlockSpec((n, ns, DN_HEAD_DIM, DN_HEAD_DIM), lambda k: (k, 0, 0, 0))],
        out_shape=[_sds((nb, t // nb, D_DN)), _sds((nc, ns, DN_HEAD_DIM, DN_HEAD_DIM))],
        scratch_shapes=[pltpu.VMEM((ns, DN_HEAD_DIM, DN_HEAD_DIM), f32)],
        compiler_params=_cp(1, VMEM_BIG),
    )(_seq_view(val, nb), _seq_view(kcd, nb), _seq_view(attn, nb), _seq_view(qd, nb), _seq_view(kd, nb),
      el.reshape(nb, nc, 1, LANE), _seq_view(gg, nb), ng)
    return y.reshape(t, D_DN), ss


def dn_scan_bwd(loc, gg, ng, ss, dy, nb, name):
    val, kcd, attn, qd, kd, el = loc
    t = val.shape[0]
    c, n = DN_CHUNK, DN_SCAN_CHUNKS
    nc = t // nb // c
    ns = nb * DN_HEADS
    steps = nc // n

    def body(val_ref, kcd_ref, attn_ref, qd_ref, kd_ref, el_ref, gg_ref, ng_ref, ss_ref, dy_ref,
             dval_ref, dkcd_ref, dattn_ref, dqd_ref, dkd_ref, del_ref, dgg_ref, dng_ref, dst):
        @pl.when(pl.program_id(0) == 0)
        def _():
            dst[...] = jnp.zeros_like(dst)

        lane = lax.broadcasted_iota(jnp.int32, (1, LANE), 1)
        chains = _dn_chains(nb)
        ds = [dst[i] for i in range(ns)]
        dng_tot = jnp.zeros((1, LANE), f32)
        for j in reversed(range(n)):
            rows = pl.ds(j * c, c)
            _, vj = jax.vjp(_dn_step,
                            *_dn_step_operands(val_ref, kcd_ref, attn_ref, qd_ref, kd_ref, el_ref, gg_ref, nb, j),
                            [ss_ref[j, i] for i in range(ns)], ng_ref[...])
            dys = [dy_ref[b, rows, h * DN_HEAD_DIM:(h + 1) * DN_HEAD_DIM] for b, h in chains]
            dval, dkcd, dattn, dqd, dkd, dlast, dgg, ds, dng = vj((dys, ds))
            dng_tot = dng_tot + dng
            del_rows = [jnp.zeros((1, LANE), f32) for _ in range(nb)]
            for i, (b, h) in enumerate(chains):
                cols = slice(h * DN_HEAD_DIM, (h + 1) * DN_HEAD_DIM)
                dval_ref[b, rows, cols] = dval[i]
                dkcd_ref[b, rows, cols] = dkcd[i]
                dattn_ref[b, rows, h * c:(h + 1) * c] = dattn[i]
                dqd_ref[b, rows, cols] = dqd[i]
                dkd_ref[b, rows, cols] = dkd[i]
                dgg_ref[b, rows, cols] = dgg[i].astype(_MXU)
                del_rows[b] = del_rows[b] + jnp.where(lane == h, dlast[i], 0.0)
            for b in range(nb):
                del_ref[b, j] = del_rows[b]
        for i in range(ns):
            dst[i] = ds[i]
        _acc(dng_ref, dng_tot, pl.program_id(0) == 0)

    def blk(w):
        return pl.BlockSpec((nb, n * c, w), lambda k: (0, steps - 1 - k, 0))

    el_spec = pl.BlockSpec((nb, n, 1, LANE), lambda k: (0, steps - 1 - k, 0, 0))
    outs = pl.pallas_call(
        body, name=name, grid=(steps,),
        in_specs=[blk(D_DN), blk(D_DN), blk(DN_ATTN), blk(D_DN), blk(D_DN), el_spec, blk(D_DN), _full((1, LANE)),
                  pl.BlockSpec((n, ns, DN_HEAD_DIM, DN_HEAD_DIM), lambda k: (steps - 1 - k, 0, 0, 0)), blk(D_DN)],
        out_specs=[blk(D_DN), blk(D_DN), blk(DN_ATTN), blk(D_DN), blk(D_DN), el_spec, blk(D_DN), _full((1, LANE))],
        out_shape=[_sds((nb, t // nb, D_DN)), _sds((nb, t // nb, D_DN)), _sds((nb, t // nb, DN_ATTN)),
                   _sds((nb, t // nb, D_DN)), _sds((nb, t // nb, D_DN)), _sds((nb, nc, 1, LANE)),
                   _sds((nb, t // nb, D_DN), _MXU), _sds((1, LANE))],
        scratch_shapes=[pltpu.VMEM((ns, DN_HEAD_DIM, DN_HEAD_DIM), f32)],
        compiler_params=_cp(1, VMEM_BIG),
    )(_seq_view(val, nb), _seq_view(kcd, nb), _seq_view(attn, nb), _seq_view(qd, nb), _seq_view(kd, nb),
      el.reshape(nb, nc, 1, LANE), _seq_view(gg, nb), ng, ss, _seq_view(dy, nb))
    dloc = [o.reshape((t,) + o.shape[2:]) for o in outs[:5]] + [outs[5].reshape(t // c, 1, LANE)]
    return dloc, outs[6].reshape(t, D_DN), outs[7]


SG_ROWS = 512


def sg_fwd(z, lng, lnb, w, bt, name):
    t = z.shape[0]

    def body(z_ref, lng_ref, lnb_ref, w_ref, bt_ref, y_ref):
        ws = [w_ref[h] for h in range(SG_HEADS)]
        for k in range(SG_ROWS // SG_CHUNK):
            r = pl.ds(k * SG_CHUNK, SG_CHUNK)
            y_ref[r, :] = _sg_chunk(z_ref[r, :D_SG], z_ref[r, D_SG:2 * D_SG], z_ref[r, 2 * D_SG:], lng_ref[...],
                                    lnb_ref[...], ws, bt_ref[...])

    return pl.pallas_call(
        body, name=name, grid=(t // SG_ROWS,),
        in_specs=[_rows(SG_ROWS, 3 * D_SG), _full((1, D_SG)), _full((1, D_SG)),
                  _full((SG_HEADS, SG_CHUNK, SG_CHUNK)), _full((SG_CHUNK, LANE))],
        out_specs=_rows(SG_ROWS, D_SG),
        out_shape=_sds((t, D_SG)),
        compiler_params=_cp(1),
    )(z, lng, lnb, w, bt)


def sg_bwd(z, lng, lnb, w, bt, dy, name):
    t = z.shape[0]

    def body(z_ref, lng_ref, lnb_ref, w_ref, bt_ref, dy_ref, dz_ref, dlng_ref, dlnb_ref, dw_ref, dbt_ref):
        ws = [w_ref[h] for h in range(SG_HEADS)]
        tot = None
        for k in range(SG_ROWS // SG_CHUNK):
            r = pl.ds(k * SG_CHUNK, SG_CHUNK)
            _, vj = jax.vjp(_sg_chunk, z_ref[r, :D_SG], z_ref[r, D_SG:2 * D_SG], z_ref[r, 2 * D_SG:], lng_ref[...],
                            lnb_ref[...], ws, bt_ref[...])
            du, dv, dgate, dlng, dlnb, dws, dbt = vj(dy_ref[r, :])
            dz_ref[r, :D_SG] = du.astype(_MXU)
            dz_ref[r, D_SG:2 * D_SG] = dv.astype(_MXU)
            dz_ref[r, 2 * D_SG:] = dgate.astype(_MXU)
            part = [dlng, dlnb, dbt] + list(dws)
            tot = part if tot is None else [a + b for a, b in zip(tot, part)]
        first = pl.program_id(0) == 0
        _acc(dlng_ref, tot[0], first)
        _acc(dlnb_ref, tot[1], first)
        _acc(dbt_ref, tot[2], first)
        for h in range(SG_HEADS):
            @pl.when(first)
            def _():
                dw_ref[h] = tot[3 + h]

            @pl.when(jnp.logical_not(first))
            def _():
                dw_ref[h] += tot[3 + h]

    return pl.pallas_call(
        body, name=name, grid=(t // SG_ROWS,),
        in_specs=[_rows(SG_ROWS, 3 * D_SG), _full((1, D_SG)), _full((1, D_SG)),
                  _full((SG_HEADS, SG_CHUNK, SG_CHUNK)), _full((SG_CHUNK, LANE)), _rows(SG_ROWS, D_SG)],
        out_specs=[_rows(SG_ROWS, 3 * D_SG), _full((1, D_SG)), _full((1, D_SG)),
                   _full((SG_HEADS, SG_CHUNK, SG_CHUNK)), _full((SG_CHUNK, LANE))],
        out_shape=[_sds((t, 3 * D_SG), _MXU), _sds((1, D_SG)), _sds((1, D_SG)), _sds((SG_HEADS, SG_CHUNK, SG_CHUNK)),
                   _sds((SG_CHUNK, LANE))],
        compiler_params=_cp(1),
    )(z, lng, lnb, w, bt, dy)


def add_pairs(a_list, b_list, name):
    n = len(a_list)

    def body(*refs):
        for a_ref, b_ref, o_ref in zip(refs[:n], refs[n:2 * n], refs[2 * n:]):
            o_ref[...] = (a_ref[...].astype(f32) + b_ref[...].astype(f32)).astype(o_ref.dtype)

    return pl.pallas_call(
        body, name=name, out_shape=[_sds(a.shape, a.dtype) for a in a_list],
        compiler_params=pltpu.CompilerParams(vmem_limit_bytes=VMEM_BIG),
    )(*a_list, *b_list)


def _adamw(g, w, m, v):
    nm = ADAM_B1 * m + (1.0 - ADAM_B1) * g
    nv = ADAM_B2 * v + (1.0 - ADAM_B2) * jnp.square(g)
    m_hat = nm / (1.0 - ADAM_B1 ** ADAM_STEP)
    v_hat = nv / (1.0 - ADAM_B2 ** ADAM_STEP)
    return -ADAM_LR * (m_hat / (jnp.sqrt(v_hat) + ADAM_EPS) + ADAM_WD * w), nm, nv


def sum_parts(half, recv, name):
    _, r, c = recv.shape
    tr = 256 if r % 256 == 0 else r

    def body(half_ref, recv_ref, g_ref):
        g = recv_ref[0].astype(f32)
        for k in range(1, N_CHIPS):
            g = g + recv_ref[k].astype(f32)
        g_ref[...] = g

    return pl.pallas_call(
        body, name=name,
        grid_spec=pltpu.PrefetchScalarGridSpec(
            num_scalar_prefetch=1, grid=(r // tr,),
            in_specs=[pl.BlockSpec((N_CHIPS, tr, c), lambda i, h: (0, i, 0))],
            out_specs=pl.BlockSpec((None, tr, c), lambda i, h: (h[0], i, 0))),
        out_shape=_sds((2, r, c)),
        compiler_params=_cp(1, VMEM_BIG),
    )(half, recv)


def adamw(g, w, m, v, name):
    _, r, c = w.shape
    tr = 256 if r % 256 == 0 else r

    def body(g_ref, w_ref, m_ref, v_ref, d_ref, nm_ref, nv_ref):
        d_ref[...], nm_ref[...], nv_ref[...] = _adamw(g_ref[...], w_ref[...], m_ref[...], v_ref[...])

    blk = pl.BlockSpec((None, tr, c), lambda l, i: (l, i, 0))
    return pl.pallas_call(
        body, name=name, grid=(2, r // tr), in_specs=[blk] * 4, out_specs=[blk] * 3, out_shape=[_sds((2, r, c))] * 3,
        compiler_params=_cp(2, VMEM_BIG),
    )(g, w, m, v)


def sum_parts_small(chip, parts, sums, name):
    n = len(sums)

    def body(chip_ref, *refs):
        for part, own, out in zip(refs[:n], refs[n:2 * n], refs[2 * n:]):
            g = jnp.where(chip_ref[0] == 0, own[...], part[0])
            for q in range(1, N_CHIPS):
                g = g + jnp.where(chip_ref[0] == q, own[...], part[q])
            out[...] = g

    vmem = pl.BlockSpec(memory_space=pltpu.VMEM)
    return pl.pallas_call(
        body, name=name, in_specs=[pl.BlockSpec(memory_space=pltpu.SMEM)] + [vmem] * (2 * n), out_specs=[vmem] * n,
        out_shape=[_sds(s.shape) for s in sums], compiler_params=pltpu.CompilerParams(vmem_limit_bytes=VMEM_BIG),
    )(chip, *parts, *sums)


def adamw_small(gs, ws, ms, vs, name):
    n = len(ws)

    def body(*refs):
        ins, outs = refs[:4 * n], refs[4 * n:]
        for k in range(n):
            outs[k][...], outs[n + k][...], outs[2 * n + k][...] = _adamw(
                ins[k][...], ins[n + k][...], ins[2 * n + k][...], ins[3 * n + k][...])

    outs = pl.pallas_call(
        body, name=name, out_shape=[_sds(w.shape) for w in ws] * 3,
        compiler_params=pltpu.CompilerParams(vmem_limit_bytes=VMEM_BIG),
    )(*gs, *ws, *ms, *vs)
    return [outs[j * n:(j + 1) * n] for j in range(3)]


_ANY = pl.BlockSpec(memory_space=pltpu.HBM)
_MESH = pl.DeviceIdType.MESH


def _flip(v, bit):
    return 1 - v if bit else v


_CHIP_RELS = ((1, 0), (0, 1), (1, 1))


def _piece(ref, kind, q):
    if kind[0] == "slot":
        return ref.at[q]
    if kind[0] == "all":
        return ref
    _, axis, n = kind
    return ref.at[(slice(None),) * axis + (pl.ds(q * n, n),)]


def _piece_shape(shape, kind):
    if kind[0] == "slot":
        return tuple(shape[1:])
    if kind[0] == "all":
        return tuple(shape)
    _, axis, n = kind
    return tuple(shape[:axis]) + (n,) + tuple(shape[axis + 1:])


def gather_weights(shards, kinds, name):
    n = len(shards)

    def out_shape(s, kind):
        if kind[0] == "slot":
            return (N_CHIPS,) + tuple(s.shape)
        _, axis, w = kind
        return tuple(s.shape[:axis + 1]) + (N_CHIPS * w,) + tuple(s.shape[axis + 2:])

    def place(o_ref, kind, q, layer):
        if kind[0] == "slot":
            return o_ref.at[q, layer]
        return _piece(o_ref.at[layer], kind, q)

    def body(*refs):
        s_refs, o_refs = refs[:n], refs[n:2 * n]
        send_sems, recv_sems, fwd_send_sems, fwd_recv_sems = refs[2 * n:]
        x, y, c = lax.axis_index("x"), lax.axis_index("y"), lax.axis_index("c")
        mine = 2 * x + y
        sends, arrivals, forwards, fwd_arrivals = [], [], [], []
        for r, (fx, fy) in enumerate(_CHIP_RELS):
            px, py = _flip(x, fx), _flip(y, fy)
            peer = 2 * px + py
            for k in range(n):
                s = r * n + k
                sends.append(pltpu.make_async_remote_copy(
                    src_ref=s_refs[k].at[c], dst_ref=place(o_refs[k], kinds[k], mine, c), send_sem=send_sems.at[s],
                    recv_sem=recv_sems.at[s], device_id=(px, py, c), device_id_type=_MESH))
                arrivals.append(pltpu.make_async_remote_copy(
                    src_ref=s_refs[k].at[c], dst_ref=place(o_refs[k], kinds[k], peer, c), send_sem=send_sems.at[s],
                    recv_sem=recv_sems.at[s], device_id=(px, py, c), device_id_type=_MESH))
                block = place(o_refs[k], kinds[k], peer, c)
                forwards.append(pltpu.make_async_remote_copy(
                    src_ref=block, dst_ref=block, send_sem=fwd_send_sems.at[s], recv_sem=fwd_recv_sems.at[s],
                    device_id=(x, y, 1 - c), device_id_type=_MESH))
                other = place(o_refs[k], kinds[k], peer, 1 - c)
                fwd_arrivals.append(pltpu.make_async_remote_copy(
                    src_ref=other, dst_ref=other, send_sem=fwd_send_sems.at[s], recv_sem=fwd_recv_sems.at[s],
                    device_id=(x, y, 1 - c), device_id_type=_MESH))
        for cp in sends:
            cp.start()
        for arrived, fwd in zip(arrivals, forwards):
            arrived.wait_recv()
            fwd.start()
        for cp in fwd_arrivals:
            cp.wait_recv()
        for cp in sends + forwards:
            cp.wait_send()

    m = len(_CHIP_RELS) * n
    return pl.pallas_call(
        body, name=name, in_specs=[_ANY] * n, out_specs=[_ANY] * n,
        out_shape=[_sds(out_shape(s, k), s.dtype) for s, k in zip(shards, kinds)],
        scratch_shapes=[pltpu.SemaphoreType.DMA((m,))] * 4,
    )(*shards)


def _owned_by(owners, side):
    return [k for k, o in enumerate(owners) if o == side]


def exchange_halves(gs, smalls, owners, name):
    n, ns = len(gs), len(smalls)

    def body(*refs):
        g_refs, s_refs = refs[:n], refs[n:n + ns]
        got_refs, sgot_refs = refs[n + ns:2 * n + ns], refs[2 * n + ns:2 * (n + ns)]
        send_sems, recv_sems = refs[2 * (n + ns):]
        x, y, c = lax.axis_index("x"), lax.axis_index("y"), lax.axis_index("c")
        sibling = (x, y, 1 - c)
        swaps = [pltpu.make_async_remote_copy(
            src_ref=g_refs[k].at[1 - c], dst_ref=got_refs[k], send_sem=send_sems.at[k], recv_sem=recv_sems.at[k],
            device_id=sibling, device_id_type=_MESH) for k in range(n)]
        gives = [pltpu.make_async_remote_copy(
            src_ref=s_refs[k], dst_ref=sgot_refs[k], send_sem=send_sems.at[n + k], recv_sem=recv_sems.at[n + k],
            device_id=sibling, device_id_type=_MESH) for k in range(ns)]
        for cp in swaps:
            cp.start()
        for side in (0, 1):
            @pl.when(c == 1 - side)
            def _():
                for k in _owned_by(owners, side):
                    gives[k].start()
        for cp in swaps:
            cp.wait()
        for side in (0, 1):
            @pl.when(c == 1 - side)
            def _():
                for k in _owned_by(owners, side):
                    gives[k].wait_send()

            @pl.when(c == side)
            def _():
                for k in _owned_by(owners, side):
                    gives[k].wait_recv()

    outs = pl.pallas_call(
        body, name=name, in_specs=[_ANY] * (n + ns), out_specs=[_ANY] * (n + ns),
        out_shape=[_sds(g.shape[1:], g.dtype) for g in gs] + [_sds(s.shape, s.dtype) for s in smalls],
        scratch_shapes=[pltpu.SemaphoreType.DMA((n + ns,)), pltpu.SemaphoreType.DMA((n + ns,))],
    )(*gs, *smalls)
    return outs[:n], outs[n:]


def reduce_to_chips(ts, kinds, smalls, owners, name):
    n, ns = len(ts), len(smalls)

    def body(*refs):
        t_refs, s_refs = refs[:n], refs[n:n + ns]
        o_refs, so_refs = refs[n + ns:2 * n + ns], refs[2 * n + ns:2 * (n + ns)]
        send_sems, recv_sems = refs[2 * (n + ns):]
        x, y, c = lax.axis_index("x"), lax.axis_index("y"), lax.axis_index("c")
        mine = 2 * x + y
        sends, arrivals, small_sends, small_arrivals = [], [], [], []
        for r, (fx, fy) in enumerate(_CHIP_RELS):
            px, py = _flip(x, fx), _flip(y, fy)
            peer = 2 * px + py
            for k in range(n + ns):
                s = r * (n + ns) + k
                if k < n:
                    src, dst = _piece(t_refs[k], kinds[k], peer), o_refs[k]
                else:
                    src, dst = s_refs[k - n], so_refs[k - n]
                go = pltpu.make_async_remote_copy(
                    src_ref=src, dst_ref=dst.at[mine], send_sem=send_sems.at[s], recv_sem=recv_sems.at[s],
                    device_id=(px, py, c), device_id_type=_MESH)
                come = pltpu.make_async_remote_copy(
                    src_ref=src, dst_ref=dst.at[peer], send_sem=send_sems.at[s], recv_sem=recv_sems.at[s],
                    device_id=(px, py, c), device_id_type=_MESH)
                (sends if k < n else small_sends).append(go)
                (arrivals if k < n else small_arrivals).append(come)

        def owned(copies, side):
            return [cp for j, cp in enumerate(copies) if owners[j % ns] == side]

        for cp in sends:
            cp.start()
        for side in (0, 1):
            @pl.when(c == side)
            def _():
                for cp in owned(small_sends, side):
                    cp.start()
        for cp in arrivals:
            cp.wait_recv()
        for cp in sends:
            cp.wait_send()
        for side in (0, 1):
            @pl.when(c == side)
            def _():
                for cp in owned(small_arrivals, side):
                    cp.wait_recv()
                for cp in owned(small_sends, side):
                    cp.wait_send()

    m = len(_CHIP_RELS) * (n + ns)
    outs = pl.pallas_call(
        body, name=name, in_specs=[_ANY] * (n + ns), out_specs=[_ANY] * (n + ns),
        out_shape=[_sds((N_CHIPS,) + _piece_shape(t.shape, k), t.dtype) for t, k in zip(ts, kinds)]
        + [_sds((N_CHIPS,) + s.shape, s.dtype) for s in smalls],
        scratch_shapes=[pltpu.SemaphoreType.DMA((m,)), pltpu.SemaphoreType.DMA((m,))],
    )(*ts, *smalls)
    return outs[:n], outs[n:]


def share_halves(rs, smalls, owners, name):
    n, ns = len(rs), len(smalls)

    def body(*refs):
        o_refs, so_refs = refs[n + ns:2 * n + ns], refs[2 * n + ns:2 * (n + ns)]
        send_sems, recv_sems = refs[2 * (n + ns):]
        x, y, c = lax.axis_index("x"), lax.axis_index("y"), lax.axis_index("c")
        sibling = (x, y, 1 - c)
        swaps = [pltpu.make_async_remote_copy(
            src_ref=o_refs[k].at[c], dst_ref=o_refs[k].at[c], send_sem=send_sems.at[k], recv_sem=recv_sems.at[k],
            device_id=sibling, device_id_type=_MESH) for k in range(n)]
        arrivals = [pltpu.make_async_remote_copy(
            src_ref=o_refs[k].at[c], dst_ref=o_refs[k].at[1 - c], send_sem=send_sems.at[k], recv_sem=recv_sems.at[k],
            device_id=sibling, device_id_type=_MESH) for k in range(n)]
        gives = [pltpu.make_async_remote_copy(
            src_ref=so_refs[k], dst_ref=so_refs[k], send_sem=send_sems.at[n + k], recv_sem=recv_sems.at[n + k],
            device_id=sibling, device_id_type=_MESH) for k in range(ns)]
        for cp in swaps:
            cp.start()
        for side in (0, 1):
            @pl.when(c == side)
            def _():
                for k in _owned_by(owners, side):
                    gives[k].start()
        for cp in arrivals:
            cp.wait_recv()
        for cp in swaps:
            cp.wait_send()
        for side in (0, 1):
            @pl.when(c == side)
            def _():
                for k in _owned_by(owners, side):
                    gives[k].wait_send()

            @pl.when(c == 1 - side)
            def _():
                for k in _owned_by(owners, side):
                    gives[k].wait_recv()

    outs = pl.pallas_call(
        body, name=name, in_specs=[_ANY] * (n + ns), out_specs=[_ANY] * (n + ns),
        out_shape=[_sds(r.shape, r.dtype) for r in list(rs) + list(smalls)],
        input_output_aliases={k: k for k in range(n + ns)},
        scratch_shapes=[pltpu.SemaphoreType.DMA((n + ns,)), pltpu.SemaphoreType.DMA((n + ns,))],
    )(*rs, *smalls)
    return outs[:n], outs[n:]


def _small_view(a):
    if a.size < 8 * LANE:
        return jnp.pad(a.reshape(-1), (0, 8 * LANE - a.size)).reshape(8, LANE)
    if a.ndim == 1:
        return a.reshape(1, a.shape[0])
    if a.ndim == 4 and a.shape[-1] < LANE:
        return a.reshape(a.shape[0], a.shape[1], a.shape[2] * a.shape[3])
    return a


def _permuted_from_shards(shards):
    parts = []
    for lo, hi in GROUP_COLS:
        for q in range(N_CHIPS):
            a, b = max(lo, q * SHARD_COLS), min(hi, (q + 1) * SHARD_COLS)
            if a < b:
                parts.append(shards[q][..., a - q * SHARD_COLS:b - q * SHARD_COLS])
    pad = jnp.zeros(shards[0].shape[:-1] + (D_IN_PAD - D_IN,), shards[0].dtype)
    return jnp.concatenate(parts + [pad], axis=-1)


def _shards_from_groups(groups):
    in_order = sorted(range(len(GROUP_COLS)), key=lambda j: GROUP_COLS[j][0])
    shards = []
    for q in range(N_CHIPS):
        parts = []
        for j in in_order:
            lo, hi = GROUP_COLS[j]
            a, b = max(lo, q * SHARD_COLS), min(hi, (q + 1) * SHARD_COLS)
            if a < b:
                parts.append(groups[j][..., a - lo:b - lo])
        shards.append(jnp.concatenate(parts, axis=-1))
    return shards


def _expand_b(b):
    eye = jnp.eye(SSM_GROUPS, dtype=b.dtype)
    return jnp.einsum("gnc,gh->gchn", b, eye).reshape(D_SSM, N_STATE)


def _extract_b(e):
    return jnp.einsum("gcgn->gnc", e.reshape(SSM_GROUPS, SSM_GROUP, SSM_GROUPS, SSM_STATE))


def _expand_c(c):
    eye = jnp.eye(SSM_GROUPS, dtype=c.dtype)
    return jnp.einsum("gcn,gh->gnhc", c, eye).reshape(N_STATE, D_SSM)


def _extract_c(e):
    return jnp.einsum("gngc->gcn", e.reshape(SSM_GROUPS, SSM_STATE, SSM_GROUPS, SSM_GROUP))


def _lane_row(v):
    return jnp.pad(v, (0, LANE - v.shape[0])).reshape(1, LANE)


def _layer_params(w, l):
    return dict(
        norm_g=w["norm_g"][l][None], win=w["w_in_perm"][l], wout=w["w_out"][l].astype(_MXU),
        pg=w["ple_norm_g"][l][None], wgate=w["w_ple_gate"][l].astype(_MXU), wple=w["w_ple"][l].astype(_MXU),
        are=w["ssm_a_re"][l].reshape(1, N_STATE), aim=w["ssm_a_im"][l].reshape(1, N_STATE),
        ls=jnp.repeat(w["ssm_log_step"][l], SSM_STATE).reshape(1, N_STATE),
        bre=_expand_b(w["ssm_b_re"][l]), bim=_expand_b(w["ssm_b_im"][l]),
        cr=_expand_c(w["ssm_c_re"][l]), ci=_expand_c(w["ssm_c_im"][l]),
        dr=w["ssm_d"][l].reshape(1, D_SSM), wglu=w["ssm_w_glu"][l].astype(f32), bglu=w["ssm_b_glu"][l][None],
        convw=w["dn_conv_w"][l], alog=_lane_row(w["dn_a_log"][l]), dtb=_lane_row(w["dn_dt_bias"][l]),
        ng=w["dn_norm_g"][l][None],
        lng=w["sg_ln_g"][l][None], lnb=w["sg_ln_b"][l][None], sgw=w["sg_w"][l],
        bt=jnp.pad(w["sg_b"][l].T, ((0, 0), (0, LANE - SG_HEADS))),
    )


def _layer_fwd(x, p, lp, nb, tag):
    seq = x.shape[0] // nb
    h, zs, zq, zg, zsg, zab = in_fwd(x, lp["norm_g"], lp["win"], f"in_fwd{tag}")
    prep = s5_prep_fwd(lp["are"], lp["aim"], lp["ls"], lp["bre"], lp["bim"], f"s5_prep_fwd{tag}")
    s5p = tuple(prep) + (lp["cr"], lp["ci"], lp["dr"], lp["wglu"], lp["bglu"])
    ys, hs = s5_fwd(zs, s5p, nb, f"s5_fwd{tag}")
    qkv = dn_pre_fwd(zq, lp["convw"], seq, f"dn_pre_fwd{tag}")
    loc, inv = dn_local_fwd(qkv, zab, lp["alog"], lp["dtb"], f"dn_local_fwd{tag}")
    yd, ss = dn_scan_fwd(loc, zg, lp["ng"], nb, f"dn_scan_fwd{tag}")
    yg = sg_fwd(zsg, lp["lng"], lp["lnb"], lp["sgw"], lp["bt"], f"sg_fwd{tag}")
    x2, x1, y, hn = post_fwd(x, ys, yd, yg, p, lp["wout"], lp["pg"], lp["wgate"], lp["wple"], f"post_fwd{tag}")
    saved = dict(x=x, h=h, zs=zs, zq=zq, zg=zg, zsg=zsg, zab=zab, s5p=s5p, hs=hs, qkv=qkv, loc=loc, inv=inv, ss=ss, x1=x1, y=y, hn=hn, p=p)
    return x2, saved


def _layer_bwd(dx2, sv, lp, nb, tag):
    seq = dx2.shape[0] // nb
    dx1, dgp, dpp, dys, dyd, dyg, dpg = post_bwd(dx2, sv["x1"], sv["hn"], sv["p"], lp["wout"], lp["pg"], lp["wgate"],
                                                 lp["wple"], f"post_bwd{tag}")
    g = {}
    g["w_out"] = wgrad(sv["y"], dx1, f"wgrad_out{tag}")
    g["w_ple_gate"] = wgrad(sv["hn"], dgp, f"wgrad_gate{tag}")
    g["w_ple"] = wgrad(sv["p"], dpp, f"wgrad_ple{tag}")
    g["ple_norm_g"] = dpg[0]
    dzsg, dlng, dlnb, dsgw, dbt = sg_bwd(sv["zsg"], lp["lng"], lp["lnb"], lp["sgw"], lp["bt"], dyg, f"sg_bwd{tag}")
    g["sg_ln_g"], g["sg_ln_b"], g["sg_w"], g["sg_b"] = dlng[0], dlnb[0], dsgw, dbt[:, :SG_HEADS].T
    dloc, dzg, dng = dn_scan_bwd(sv["loc"], sv["zg"], lp["ng"], sv["ss"], dyd, nb, f"dn_scan_bwd{tag}")
    dqkv, dzab, dalog, ddtb = dn_local_bwd(sv["qkv"], sv["zab"], lp["alog"], lp["dtb"], sv["inv"], dloc,
                                           f"dn_local_bwd{tag}")
    dzq, dconv = dn_pre_bwd(sv["zq"], lp["convw"], dqkv, seq, f"dn_pre_bwd{tag}")
    g["dn_conv_w"], g["dn_a_log"], g["dn_dt_bias"], g["dn_norm_g"] = dconv, dalog[0, :DN_HEADS], ddtb[0, :DN_HEADS], dng[0]
    s5out = s5_bwd(sv["zs"], sv["s5p"], sv["hs"], dys, nb, f"s5_bwd{tag}")
    dzs, dprep, (dcr, dci, ddr, dwglu, dbglu) = s5out[0], s5out[1:1 + S5_PREPARED], s5out[1 + S5_PREPARED:]
    dare, daim, dls, dbre, dbim = s5_prep_bwd(lp["are"], lp["aim"], lp["ls"], lp["bre"], lp["bim"], dprep,
                                              f"s5_prep_bwd{tag}")
    g["ssm_a_re"] = dare.reshape(SSM_GROUPS, SSM_STATE)
    g["ssm_a_im"] = daim.reshape(SSM_GROUPS, SSM_STATE)
    g["ssm_log_step"] = dls.reshape(SSM_GROUPS, SSM_STATE).sum(axis=1)
    g["ssm_b_re"], g["ssm_b_im"] = _extract_b(dbre), _extract_b(dbim)
    g["ssm_c_re"], g["ssm_c_im"] = _extract_c(dcr), _extract_c(dci)
    g["ssm_d"] = ddr.reshape(SSM_GROUPS, SSM_GROUP)
    g["ssm_w_glu"], g["ssm_b_glu"] = dwglu, dbglu[0]
    dzs_all = (dzs, dzq, dzg, dzsg, dzab)
    dx, dng_in = in_bwd(sv["x"], lp["norm_g"], lp["win"], dzs_all, dx1, f"in_bwd{tag}")
    g["w_in_pieces"] = [wgrad(sv["h"], dz, f"wgrad_in{k}{tag}") for k, dz in enumerate(dzs_all)]
    g["norm_g"] = dng_in[0]
    return dx, g


def _local_step(x, p, target, w, nb):
    lps = [_layer_params(w, l) for l in range(DEPTH)]
    saved = []
    for l in range(DEPTH):
        x, sv = _layer_fwd(x, p[l], lps[l], nb, f"_l{l}")
        saved.append(sv)
    loss_blk, dx, dfg = loss_fwd_bwd(x, w["final_norm_g"][None], target, "loss")
    grads = [None] * DEPTH
    for l in reversed(range(DEPTH)):
        dx, grads[l] = _layer_bwd(dx, saved[l], lps[l], nb, f"_l{l}")
    out = {k: jnp.stack([grads[l][k] for l in range(DEPTH)]) for k in grads[0] if k != "w_in_pieces"}
    out["w_in_pieces"] = [grads[l]["w_in_pieces"] for l in range(DEPTH)]
    out["final_norm_g"] = dfg[0]
    return loss_blk[0, 0], dx, out


def kernel(x, p, norm_g, w_in, ssm_a_re, ssm_a_im, ssm_b_re, ssm_b_im, ssm_c_re, ssm_c_im, ssm_d, ssm_log_step, ssm_w_glu, ssm_b_glu, dn_conv_w, dn_a_log, dn_dt_bias, dn_norm_g, sg_ln_g, sg_ln_b, sg_w, sg_b, w_out, ple_norm_g, w_ple_gate, w_ple, final_norm_g, loss_target, m_norm_g, m_w_in, m_ssm_a_re, m_ssm_a_im, m_ssm_b_re, m_ssm_b_im, m_ssm_c_re, m_ssm_c_im, m_ssm_d, m_ssm_log_step, m_ssm_w_glu, m_ssm_b_glu, m_dn_conv_w, m_dn_a_log, m_dn_dt_bias, m_dn_norm_g, m_sg_ln_g, m_sg_ln_b, m_sg_w, m_sg_b, m_w_out, m_ple_norm_g, m_w_ple_gate, m_w_ple, m_final_norm_g, v_norm_g, v_w_in, v_ssm_a_re, v_ssm_a_im, v_ssm_b_re, v_ssm_b_im, v_ssm_c_re, v_ssm_c_im, v_ssm_d, v_ssm_log_step, v_ssm_w_glu, v_ssm_b_glu, v_dn_conv_w, v_dn_a_log, v_dn_dt_bias, v_dn_norm_g, v_sg_ln_g, v_sg_ln_b, v_sg_w, v_sg_b, v_w_out, v_ple_norm_g, v_w_ple_gate, v_w_ple, v_final_norm_g):
    args = locals()
    w = {n: args[n] for n in WEIGHTS}
    m = {n: args["m_" + n] for n in WEIGHTS}
    v = {n: args["v_" + n] for n in WEIGHTS}
    nb, seq = x.shape[0], x.shape[1]
    t = nb * seq

    full = _gather_full(w)
    loss_local, dx, grads = _local_step(x.reshape(t, D_MODEL), p.reshape(DEPTH, t, D_PLE),
                                        loss_target.reshape(t, D_MODEL), full, nb)
    outs, loss = _reduce_and_update(grads, w, m, v, loss_local)
    return (loss, dx.reshape(nb, seq, D_MODEL), *[outs[0][n] for n in WEIGHTS], *[outs[1][n] for n in WEIGHTS],
            *[outs[2][n] for n in WEIGHTS], *[outs[3][n] for n in WEIGHTS])


def _gather_full(w):
    sh_names = [n for n, _ in SHARDED]
    shards = [w[n] if n == "dn_conv_w" else w[n].astype(_COMM) for n in sh_names]
    gathered = gather_weights(shards, [k for _, k in SHARDED], "gather_weights")
    chip = 2 * lax.axis_index("x") + lax.axis_index("y")
    full = {n: w[n] for n in REPLICATED}
    for (n, kind), shard, got in zip(SHARDED, shards, gathered):
        if kind[0] == "slot":
            full[n] = lax.dynamic_update_index_in_dim(got, shard, chip, 0)
        else:
            full[n] = lax.dynamic_update_slice_in_dim(got, shard, chip * kind[2], axis=kind[1] + 1)
    slots = full.pop("w_in")
    full["w_in_perm"] = _permuted_from_shards([slots[q] for q in range(N_CHIPS)]).astype(_MXU)
    return full


def _reduce_and_update(grads, w, m, v, loss_local):
    sh_names = [n for n, _ in SHARDED]
    sh_kinds = [k for _, k in SHARDED]
    owners = [SMALL_OWNER[n] for n in REPLICATED + ("loss",)]

    def small_views(d):
        return [_small_view(d[n]) for n in REPLICATED]

    grads["w_in"] = jnp.stack([jnp.stack(_shards_from_groups(pieces)) for pieces in grads["w_in_pieces"]])
    gs = [grads[n] if n == "dn_conv_w" else grads[n].astype(_COMM) for n in sh_names]
    sm = small_views(grads) + [_small_view(loss_local.reshape(1))]
    core = lax.axis_index("c")
    chip = 2 * lax.axis_index("x") + lax.axis_index("y")
    got, sm_got = exchange_halves(gs, sm, owners, "exchange_halves")
    sums = add_pairs([lax.dynamic_index_in_dim(g, core, 0, keepdims=False) for g in gs] + sm, list(got) + list(sm_got),
                     "add_halves")
    sums, sm_sums = sums[:len(gs)], sums[len(gs):]
    parts, sm_parts = reduce_to_chips(sums, sh_kinds, sm_sums, owners, "reduce_to_chips")
    parts = list(parts)
    for k, (kind, total) in enumerate(zip(sh_kinds, sums)):
        if kind[0] == "slot":
            own = lax.dynamic_index_in_dim(total, chip, 0, keepdims=False)
        else:
            own = lax.dynamic_slice_in_dim(total, chip * kind[2], kind[2], axis=kind[1])
        parts[k] = lax.dynamic_update_index_in_dim(parts[k], own, chip, 0)
    half = core.astype(jnp.int32).reshape(1)
    totals = [sum_parts(half, part, f"sum_{n}") for n, part in zip(sh_names, parts)]
    sm_totals = sum_parts_small(chip.astype(jnp.int32).reshape(1), sm_parts, sm_sums, "sum_replicated")
    g_big, g_small = share_halves(totals, sm_totals, owners, "share_halves")
    outs = [dict(zip(sh_names, g_big)), {}, {}, {}]
    for n, g in zip(sh_names, g_big):
        outs[1][n], outs[2][n], outs[3][n] = adamw(g, w[n], m[n], v[n], f"adamw_{n}")
    small_results = [g_small[:-1]] + adamw_small(g_small[:-1], small_views(w), small_views(m), small_views(v),
                                                 "adamw_replicated")
    for j in range(4):
        for n, r in zip(REPLICATED, small_results[j]):
            outs[j][n] = r.reshape(-1)[:w[n].size].reshape(w[n].shape)
    return outs, g_small[-1][0, 0]
```

```python
import functools

import jax
import jax.numpy as jnp
from jax import lax
from jax.experimental import pallas as pl
from jax.experimental.pallas import tpu as pltpu

f32 = jnp.float32
bf16 = jnp.bfloat16

_MXU = bf16
_COMM = bf16
HIGH = lax.Precision.HIGH

D_MODEL = 1024
DEPTH = 2
D_PLE = 256
D_SSM = 256
D_DN = 512
D_SG = 256
SSM_GROUPS = 16
SSM_GROUP = 16
SSM_STATE = 64
N_STATE = SSM_GROUPS * SSM_STATE
DN_HEADS = 4
DN_HEAD_DIM = 128
DN_CONV = 4
DN_HALO = 16
DN_CHUNK = 64
SG_HEADS = 4
SG_HEAD_DIM = 64
SG_CHUNK = 128
S5_CHUNK = 512
S5_GROUP_ROWS = 8
EPS = 1e-6
D_IN = 3336
D_IN_PAD = 3456
LANE = 128

ADAM_LR = 0.001
ADAM_B1 = 0.9
ADAM_B2 = 0.999
ADAM_EPS = 1e-08
ADAM_WD = 0.01
ADAM_STEP = 10

N_CHIPS = 4
N_DEV = 8

Z_COLS = ((0, 512), (512, 2048), (2048, 2560), (2560, 3328), (3328, 3456))

GROUP_COLS = ((0, 512), (512, 2048), (2056, 2568), (2568, 3336), (2048, 2056))
SHARD_COLS = D_IN // 4

SHARDED = (("w_in", ("slot",)), ("ssm_w_glu", ("win", 0, 64)), ("dn_conv_w", ("win", 1, 384)),
           ("w_out", ("win", 0, 256)), ("w_ple_gate", ("win", 0, 256)), ("w_ple", ("win", 1, 256)))
REPLICATED = ("norm_g", "ssm_a_re", "ssm_a_im", "ssm_b_re", "ssm_b_im", "ssm_c_re", "ssm_c_im", "ssm_d",
              "ssm_log_step", "ssm_b_glu", "dn_a_log", "dn_dt_bias", "dn_norm_g", "sg_ln_g", "sg_ln_b", "sg_w",
              "sg_b", "ple_norm_g", "final_norm_g")
SMALL_OWNER = {n: int(n.startswith("ssm_")) for n in REPLICATED + ("loss",)}
WEIGHTS = ("norm_g", "w_in", "ssm_a_re", "ssm_a_im", "ssm_b_re", "ssm_b_im", "ssm_c_re", "ssm_c_im", "ssm_d",
           "ssm_log_step", "ssm_w_glu", "ssm_b_glu", "dn_conv_w", "dn_a_log", "dn_dt_bias", "dn_norm_g", "sg_ln_g",
           "sg_ln_b", "sg_w", "sg_b", "w_out", "ple_norm_g", "w_ple_gate", "w_ple", "final_norm_g")

VMEM_BIG = 56 * 1024 * 1024


def _mm(a, b):
    return jnp.dot(a.astype(_MXU), b.astype(_MXU), preferred_element_type=f32)


def _mm_nt(a, b):
    return lax.dot_general(a.astype(_MXU), b.astype(_MXU), (((1,), (1,)), ((), ())), preferred_element_type=f32)


def _mm_tn(a, b):
    return lax.dot_general(a.astype(_MXU), b.astype(_MXU), (((0,), (0,)), ((), ())), preferred_element_type=f32)


@jax.custom_vjp
def bdot(a, b):
    return _mm(a, b)


def _bdot_fwd(a, b):
    return _mm(a, b), (a, b)


def _bdot_bwd(res, g):
    a, b = res
    return _mm_nt(g, b).astype(a.dtype), _mm_tn(a, g).astype(b.dtype)


bdot.defvjp(_bdot_fwd, _bdot_bwd)


@jax.custom_vjp
def bdot_nt(a, b):
    return _mm_nt(a, b)


def _bdot_nt_fwd(a, b):
    return _mm_nt(a, b), (a, b)


def _bdot_nt_bwd(res, g):
    a, b = res
    return _mm(g, b).astype(a.dtype), _mm_tn(g, a).astype(b.dtype)


bdot_nt.defvjp(_bdot_nt_fwd, _bdot_nt_bwd)


@jax.custom_vjp
def bdot_tn(a, b):
    return _mm_tn(a, b)


def _bdot_tn_fwd(a, b):
    return _mm_tn(a, b), (a, b)


def _bdot_tn_bwd(res, g):
    a, b = res
    return _mm_nt(b, g).astype(a.dtype), _mm(a, g).astype(b.dtype)


bdot_tn.defvjp(_bdot_tn_fwd, _bdot_tn_bwd)


def hdot(a, b):
    return jnp.dot(a, b, precision=HIGH, preferred_element_type=f32)


def _unit_lower_inverses(ms):
    n = ms[0].shape[0]
    eye = (lax.broadcasted_iota(jnp.int32, (n, n), 0) == lax.broadcasted_iota(jnp.int32, (n, n), 1)).astype(f32)
    pw = [-m for m in ms]
    inv = [eye + p for p in pw]
    for _ in range(n.bit_length() - 2):
        pw = [hdot(p, p) for p in pw]
        inv = [a + hdot(a, p) for a, p in zip(inv, pw)]
    return inv


@jax.custom_vjp
def solve_unit_lower(ms, rhs, inv):
    return [hdot(a, r) for a, r in zip(inv, rhs)]


def _solve_unit_lower_fwd(ms, rhs, inv):
    xs = [hdot(a, r) for a, r in zip(inv, rhs)]
    return xs, (inv, xs)


def _solve_unit_lower_bwd(res, gs):
    inv, xs = res
    d_rhs = [lax.dot_general(a, g, (((0,), (0,)), ((), ())), precision=HIGH, preferred_element_type=f32)
             for a, g in zip(inv, gs)]
    d_ms = [-lax.dot_general(d, x, (((1,), (1,)), ((), ())), precision=HIGH, preferred_element_type=f32)
            for d, x in zip(d_rhs, xs)]
    return d_ms, d_rhs, [jnp.zeros_like(a) for a in inv]


solve_unit_lower.defvjp(_solve_unit_lower_fwd, _solve_unit_lower_bwd)


@functools.partial(jax.custom_vjp, nondiff_argnums=(1,))
def roll_rows(x, k):
    return pltpu.roll(x, k, 0)


def _roll_rows_fwd(x, k):
    return pltpu.roll(x, k, 0), None


def _roll_rows_bwd(k, _, g):
    return (pltpu.roll(g, g.shape[0] - k, 0),)


roll_rows.defvjp(_roll_rows_fwd, _roll_rows_bwd)


def _row_ids(shape):
    return lax.broadcasted_iota(jnp.int32, shape, 0)


def _rms(x, g):
    return x * lax.rsqrt(jnp.mean(x * x, axis=-1, keepdims=True) + EPS) * g


def _layer_norm(x, g, b):
    mu = jnp.mean(x, axis=-1, keepdims=True)
    xc = x - mu
    return xc * lax.rsqrt(jnp.mean(xc * xc, axis=-1, keepdims=True) + EPS) * g + b


def _s5_prep(are, aim, ls, bre, bim):
    step = jnp.exp(ls)
    mag = jnp.exp(are * step)
    lr = mag * jnp.cos(aim * step)
    li = mag * jnp.sin(aim * step)
    den = are * are + aim * aim
    nr, ni = lr - 1.0, li
    fr = (nr * are + ni * aim) / den
    fi = (ni * are - nr * aim) / den
    bbr = fr * bre - fi * bim
    bbi = fr * bim + fi * bre
    pr = jnp.broadcast_to(lr, (S5_GROUP_ROWS, N_STATE))
    pi = jnp.broadcast_to(li, (S5_GROUP_ROWS, N_STATE))
    d = 1
    while d < S5_GROUP_ROWS:
        keep = _row_ids(pr.shape) >= d
        sr, si = roll_rows(pr, d), roll_rows(pi, d)
        pr, pi = jnp.where(keep, pr * sr - pi * si, pr), jnp.where(keep, pr * si + pi * sr, pi)
        d *= 2
    return pr, pi, bbr, bbi


def _s5_chunk(u, gate, hr, hi, pr, pi, bbr, bbi, cr, ci, dr, wglu, bglu):
    n, steps = u.shape[0], S5_GROUP_ROWS
    groups = n // steps
    xr = bdot(u, bbr)
    xi = bdot(u, bbi)
    lr, li = pr[0:1], pi[0:1]
    rs, ims = [xr[:groups]], [xi[:groups]]
    for t in range(1, steps):
        a, b = rs[-1], ims[-1]
        rs.append(xr[t * groups:(t + 1) * groups] + lr * a - li * b)
        ims.append(xi[t * groups:(t + 1) * groups] + lr * b + li * a)
    er, ei = rs[-1], ims[-1]
    mr, mi = pr[steps - 1:steps], pi[steps - 1:steps]
    gid = _row_ids(er.shape)
    er, ei = (er + jnp.where(gid == 0, mr * hr - mi * hi, 0.0), ei + jnp.where(gid == 0, mr * hi + mi * hr, 0.0))
    d = 1
    while d < groups:
        sr = jnp.where(gid >= d, roll_rows(er, d), 0.0)
        si = jnp.where(gid >= d, roll_rows(ei, d), 0.0)
        er, ei = er + mr * sr - mi * si, ei + mr * si + mi * sr
        mr, mi = mr * mr - mi * mi, 2.0 * mr * mi
        d *= 2
    before_r = jnp.where(gid == 0, hr, roll_rows(er, 1))
    before_i = jnp.where(gid == 0, hi, roll_rows(ei, 1))
    xr = jnp.concatenate([rs[t] + pr[t:t + 1] * before_r - pi[t:t + 1] * before_i for t in range(steps)], axis=0)
    xi = jnp.concatenate([ims[t] + pr[t:t + 1] * before_i + pi[t:t + 1] * before_r for t in range(steps)], axis=0)
    y = bdot(xr, cr) - bdot(xi, ci) + dr * u
    y = jax.nn.gelu(y)
    y = y * jax.nn.sigmoid(bdot(y, wglu) + bglu)
    return y * jax.nn.silu(gate), er[groups - 1:groups], ei[groups - 1:groups]


def _dn_pre(xc, xp, w0, w1, w2, w3, is_start, col):
    xp = jnp.where(is_start, 0.0, xp)
    halo_rows = _row_ids(xp.shape)
    acc = w3 * xc
    for d, w in ((1, w2), (2, w1), (3, w0)):
        r = roll_rows(xc, d)
        head = jnp.where(halo_rows >= d, r[:DN_HALO], roll_rows(xp, d))
        acc = acc + w * jnp.concatenate([head, r[DN_HALO:]], axis=0)
    y = jax.nn.silu(acc)
    nrm = y * lax.rsqrt(jnp.sum(y * y, axis=-1, keepdims=True) + EPS)
    nrm = nrm * jnp.where(col < DN_HEADS, DN_HEAD_DIM ** -0.5, 1.0)
    return jnp.where(col < 2 * DN_HEADS, nrm, y)


def _dn_local(qs, ks, vs, abs_, alog, dtb, invs=None):
    c = DN_CHUNK
    ri = lax.broadcasted_iota(jnp.int32, (c, c), 0)
    ci = lax.broadcasted_iota(jnp.int32, (c, c), 1)
    causal, strict = ri >= ci, ri > ci
    tril = causal.astype(f32)
    gcums = [hdot(tril, -jnp.exp(alog) * jax.nn.softplus(ab + dtb)) for ab in abs_]
    gcum_ts = [g.T for g in gcums]
    sigs = [jax.nn.sigmoid(ab) for ab in abs_]
    chains = [(j, h) for j in range(len(abs_)) for h in range(DN_HEADS)]
    gc = [gcums[j][:, h:h + 1] for j, h in chains]
    decay = [jnp.where(causal, jnp.exp(jnp.where(causal, gc[n] - gcum_ts[j][h:h + 1, :], 0.0)), 0.0)
             for n, (j, h) in enumerate(chains)]
    beta = [sigs[j][:, DN_HEADS + h:DN_HEADS + h + 1] for j, h in chains]
    kb = [ks[j][h] * beta[n] for n, (j, h) in enumerate(chains)]
    ms = [jnp.where(strict, bdot_nt(kb[n], ks[j][h]) * decay[n], 0.0) for n, (j, h) in enumerate(chains)]
    egc = [jnp.exp(g) for g in gc]
    rhs = [jnp.concatenate([vs[j][h] * beta[n], kb[n] * egc[n]], axis=1) for n, (j, h) in enumerate(chains)]
    inv = _unit_lower_inverses(ms) if invs is None else [invs[j][h] for j, h in chains]
    sol = solve_unit_lower(ms, rhs, inv)
    values = [s[:, :DN_HEAD_DIM] for s in sol]
    k_cds = [s[:, DN_HEAD_DIM:] for s in sol]
    attns = [bdot_nt(qs[j][h], ks[j][h]) * decay[n] for n, (j, h) in enumerate(chains)]
    q_decs = [qs[j][h] * egc[n] for n, (j, h) in enumerate(chains)]
    k_decs = [ks[j][h] * jnp.exp(gc[n][c - 1:c, :] - gc[n]) for n, (j, h) in enumerate(chains)]

    def nest(flat):
        return [flat[j * DN_HEADS:(j + 1) * DN_HEADS] for j in range(len(abs_))]

    lasts = [jnp.exp(g[c - 1:c, :]) for g in gcums]
    return nest(values), nest(k_cds), nest(attns), nest(q_decs), nest(k_decs), lasts, nest(inv)


def _dn_step(values, k_cds, attns, q_decs, k_decs, lasts, ggs, sts, ng):
    v_new = [v - bdot(kc, st) for v, kc, st in zip(values, k_cds, sts)]
    o = [bdot(qd, st) for qd, st in zip(q_decs, sts)]
    o = [a + bdot(at, vn) for a, at, vn in zip(o, attns, v_new)]
    new = [st * la + bdot_tn(kd, vn) for st, la, kd, vn in zip(sts, lasts, k_decs, v_new)]
    return [_rms(a, ng) * jax.nn.silu(g) for a, g in zip(o, ggs)], new


def _sg_chunk(u, v, gate, lng, lnb, ws, bt):
    n = SG_CHUNK
    ug = jax.nn.gelu(u)
    vn = _layer_norm(jax.nn.gelu(v), lng, lnb)
    causal = lax.broadcasted_iota(jnp.int32, (n, n), 0) >= lax.broadcasted_iota(jnp.int32, (n, n), 1)
    lane = lax.broadcasted_iota(jnp.int32, (n, D_SG), 1)
    s = jnp.zeros((n, D_SG), f32)
    for h in range(SG_HEADS):
        t = bdot(jnp.where(causal, ws[h], 0.0), vn) + bt[:, h:h + 1]
        s = s + jnp.where((lane >= h * SG_HEAD_DIM) & (lane < (h + 1) * SG_HEAD_DIM), t, 0.0)
    return ug * s * jax.nn.silu(gate)


def _cp(n_grid, vmem=None):
    return pltpu.CompilerParams(dimension_semantics=("arbitrary",) * n_grid, vmem_limit_bytes=vmem)


def _full(shape):
    nd = len(shape)
    return pl.BlockSpec(tuple(shape), lambda *_: (0,) * nd)


def _rows(tm, ncol):
    return pl.BlockSpec((tm, ncol), lambda i: (i, 0))


def _sds(shape, dtype=f32):
    return jax.ShapeDtypeStruct(tuple(shape), dtype)


def _acc(ref, val, first):
    @pl.when(first)
    def _():
        ref[...] = val

    @pl.when(jnp.logical_not(first))
    def _():
        ref[...] += val


def in_fwd(x, g, w, name):
    t, tm = x.shape[0], 512

    def body(x_ref, g_ref, w_ref, h_ref, *z_refs):
        h = _rms(x_ref[...], g_ref[...]).astype(_MXU)
        h_ref[...] = h
        for z_ref, (a, b) in zip(z_refs, Z_COLS):
            z_ref[...] = jnp.dot(h, w_ref[:, a:b], preferred_element_type=f32)

    widths = [b - a for a, b in Z_COLS]
    return pl.pallas_call(
        body, name=name, grid=(t // tm,),
        in_specs=[_rows(tm, D_MODEL), _full((1, D_MODEL)), _full((D_MODEL, D_IN_PAD))],
        out_specs=[_rows(tm, D_MODEL)] + [_rows(tm, n) for n in widths],
        out_shape=[_sds((t, D_MODEL), _MXU)] + [_sds((t, n)) for n in widths],
        compiler_params=_cp(1, VMEM_BIG),
    )(x, g, w)


def in_bwd(x, g, w, dzs, dres, name):
    t, tm = x.shape[0], 512
    widths = [b - a for a, b in Z_COLS]

    def body(x_ref, g_ref, w_ref, dres_ref, *rest):
        dz_refs, (dx_ref, dg_ref) = rest[:5], rest[5:]
        dh = jnp.zeros((tm, D_MODEL), f32)
        for dz_ref, (a, b) in zip(dz_refs, Z_COLS):
            dh = dh + _mm_nt(dz_ref[...], w_ref[:, a:b])
        _, vj = jax.vjp(_rms, x_ref[...], g_ref[...])
        dx, dg = vj(dh)
        dx_ref[...] = dres_ref[...] + dx
        _acc(dg_ref, dg, pl.program_id(0) == 0)

    return pl.pallas_call(
        body, name=name, grid=(t // tm,),
        in_specs=[_rows(tm, D_MODEL), _full((1, D_MODEL)), _full((D_MODEL, D_IN_PAD)), _rows(tm, D_MODEL)]
        + [_rows(tm, n) for n in widths],
        out_specs=[_rows(tm, D_MODEL), _full((1, D_MODEL))],
        out_shape=[_sds((t, D_MODEL)), _sds((1, D_MODEL))],
        compiler_params=_cp(1, VMEM_BIG),
    )(x, g, w, dres, *dzs)


def wgrad(a, g, name):
    t, k = a.shape
    n = g.shape[1]
    tm = min(t, 2048)
    tn = n if n <= 768 else (768 if n % 768 == 0 else 512)
    steps = t // tm

    def body(a_ref, g_ref, o_ref, acc):
        i = pl.program_id(1)
        _acc(acc, _mm_tn(a_ref[...], g_ref[...]), i == 0)

        @pl.when(i == steps - 1)
        def _():
            o_ref[...] = acc[...].astype(o_ref.dtype)

    return pl.pallas_call(
        body, name=name, grid=(n // tn, steps),
        in_specs=[pl.BlockSpec((tm, k), lambda j, i: (i, 0)), pl.BlockSpec((tm, tn), lambda j, i: (i, j))],
        out_specs=pl.BlockSpec((k, tn), lambda j, i: (0, j)),
        out_shape=_sds((k, n), _COMM),
        scratch_shapes=[pltpu.VMEM((k, tn), f32)],
        compiler_params=_cp(2, VMEM_BIG),
    )(a, g)


def post_fwd(x, ys, yd, yg, p, wout, pg, wgate, wple, name):
    t, tm = x.shape[0], 512

    def body(x_ref, ys_ref, yd_ref, yg_ref, p_ref, wout_ref, pg_ref, wgate_ref, wple_ref,
             x2_ref, x1_ref, y_ref, hn_ref):
        y = jnp.concatenate([ys_ref[...], yd_ref[...], yg_ref[...]], axis=1).astype(_MXU)
        y_ref[...] = y
        x1 = x_ref[...] + jnp.dot(y, wout_ref[...], preferred_element_type=f32)
        x1_ref[...] = x1
        hn = _rms(x1, pg_ref[...]).astype(_MXU)
        hn_ref[...] = hn
        gp = jnp.dot(hn, wgate_ref[...], preferred_element_type=f32)
        pp = _mm(p_ref[...], wple_ref[...])
        x2_ref[...] = x1 + jax.nn.sigmoid(gp) * pp

    return pl.pallas_call(
        body, name=name, grid=(t // tm,),
        in_specs=[_rows(tm, D_MODEL), _rows(tm, D_SSM), _rows(tm, D_DN), _rows(tm, D_SG), _rows(tm, D_PLE),
                  _full((D_MODEL, D_MODEL)), _full((1, D_MODEL)), _full((D_MODEL, D_MODEL)), _full((D_PLE, D_MODEL))],
        out_specs=[_rows(tm, D_MODEL)] * 4,
        out_shape=[_sds((t, D_MODEL)), _sds((t, D_MODEL)), _sds((t, D_MODEL), _MXU), _sds((t, D_MODEL), _MXU)],
        compiler_params=_cp(1, VMEM_BIG),
    )(x, ys, yd, yg, p, wout, pg, wgate, wple)


def post_bwd(dx2, x1, hn, p, wout, pg, wgate, wple, name):
    t, tm = dx2.shape[0], 512

    def body(dx2_ref, x1_ref, hn_ref, p_ref, wout_ref, pg_ref, wgate_ref, wple_ref,
             dx1_ref, dgp_ref, dpp_ref, dys_ref, dyd_ref, dyg_ref, dpg_ref):
        dx2 = dx2_ref[...]
        gp = jnp.dot(hn_ref[...], wgate_ref[...], preferred_element_type=f32)
        pp = _mm(p_ref[...], wple_ref[...])
        sg = jax.nn.sigmoid(gp)
        dpp_ref[...] = (dx2 * sg).astype(_MXU)
        dgp = (dx2 * pp * sg * (1.0 - sg)).astype(_MXU)
        dgp_ref[...] = dgp
        dhn = _mm_nt(dgp, wgate_ref[...])
        _, vj = jax.vjp(_rms, x1_ref[...], pg_ref[...])
        dx1n, dpg = vj(dhn)
        dx1 = dx2 + dx1n
        dx1_ref[...] = dx1
        dy = _mm_nt(dx1, wout_ref[...])
        dys_ref[...] = dy[:, :D_SSM]
        dyd_ref[...] = dy[:, D_SSM:D_SSM + D_DN]
        dyg_ref[...] = dy[:, D_SSM + D_DN:]
        _acc(dpg_ref, dpg, pl.program_id(0) == 0)

    return pl.pallas_call(
        body, name=name, grid=(t // tm,),
        in_specs=[_rows(tm, D_MODEL), _rows(tm, D_MODEL), _rows(tm, D_MODEL), _rows(tm, D_PLE),
                  _full((D_MODEL, D_MODEL)), _full((1, D_MODEL)), _full((D_MODEL, D_MODEL)), _full((D_PLE, D_MODEL))],
        out_specs=[_rows(tm, D_MODEL), _rows(tm, D_MODEL), _rows(tm, D_MODEL), _rows(tm, D_SSM), _rows(tm, D_DN),
                   _rows(tm, D_SG), _full((1, D_MODEL))],
        out_shape=[_sds((t, D_MODEL)), _sds((t, D_MODEL), _MXU), _sds((t, D_MODEL), _MXU), _sds((t, D_SSM)),
                   _sds((t, D_DN)), _sds((t, D_SG)), _sds((1, D_MODEL))],
        compiler_params=_cp(1, VMEM_BIG),
    )(dx2, x1, hn, p, wout, pg, wgate, wple)


def loss_fwd_bwd(x, fg, target, name):
    t, tm = x.shape[0], 512

    def body(x_ref, fg_ref, t_ref, loss_ref, dx_ref, dfg_ref):
        def f(xv, gv):
            err = _rms(xv, gv) - t_ref[...]
            return 0.5 * jnp.sum(jnp.mean(err * err, axis=-1))

        val, vj = jax.vjp(f, x_ref[...], fg_ref[...])
        dx, dfg = vj(jnp.ones((), f32))
        dx_ref[...] = dx
        first = pl.program_id(0) == 0
        _acc(dfg_ref, dfg, first)
        _acc(loss_ref, jnp.full((8, LANE), val, f32), first)

    return pl.pallas_call(
        body, name=name, grid=(t // tm,),
        in_specs=[_rows(tm, D_MODEL), _full((1, D_MODEL)), _rows(tm, D_MODEL)],
        out_specs=[_full((8, LANE)), _rows(tm, D_MODEL), _full((1, D_MODEL))],
        out_shape=[_sds((8, LANE)), _sds((t, D_MODEL)), _sds((1, D_MODEL))],
        compiler_params=_cp(1),
    )(x, fg, target)


S5_PREPARED = 4
_S5_PARAM_SHAPES = ((S5_GROUP_ROWS, N_STATE), (S5_GROUP_ROWS, N_STATE), (D_SSM, N_STATE), (D_SSM, N_STATE),
                    (N_STATE, D_SSM), (N_STATE, D_SSM), (1, D_SSM), (D_SSM, D_SSM), (1, D_SSM))

def s5_prep_fwd(are, aim, ls, bre, bim, name):
    def body(are_ref, aim_ref, ls_ref, bre_ref, bim_ref, *outs):
        vals = _s5_prep(are_ref[...], aim_ref[...], ls_ref[...], bre_ref[...], bim_ref[...])
        for o, v in zip(outs, vals):
            o[...] = v

    return pl.pallas_call(body, name=name, out_shape=[_sds(s) for s in _S5_PARAM_SHAPES[:S5_PREPARED]])(
        are, aim, ls, bre, bim)


def s5_prep_bwd(are, aim, ls, bre, bim, cts, name):
    def body(are_ref, aim_ref, ls_ref, bre_ref, bim_ref, *rest):
        ct_refs, outs = rest[:S5_PREPARED], rest[S5_PREPARED:]
        _, vj = jax.vjp(_s5_prep, are_ref[...], aim_ref[...], ls_ref[...], bre_ref[...], bim_ref[...])
        for o, v in zip(outs, vj(tuple(r[...] for r in ct_refs))):
            o[...] = v

    shapes = [(1, N_STATE)] * 3 + [(D_SSM, N_STATE)] * 2
    return pl.pallas_call(body, name=name, out_shape=[_sds(s) for s in shapes])(are, aim, ls, bre, bim, *cts)


def _step_major(ref, cols):
    x = ref[:, cols]
    n, w = x.shape
    return jnp.swapaxes(x.reshape(n // S5_GROUP_ROWS, S5_GROUP_ROWS, w), 0, 1).reshape(n, w)


def _store_step_major(ref, cols, val):
    n, w = val.shape
    ref[:, cols] = jnp.swapaxes(val.reshape(S5_GROUP_ROWS, n // S5_GROUP_ROWS, w), 0, 1).reshape(n, w).astype(ref.dtype)


def s5_fwd(z, params, nb, name):
    t = z.shape[0]
    nc = t // nb // S5_CHUNK
    npar = len(_S5_PARAM_SHAPES)

    def body(z_ref, *rest):
        p_refs, (y_ref, hs_ref, hr_s, hi_s) = rest[:npar], rest[npar:]

        @pl.when(pl.program_id(1) == 0)
        def _():
            hr_s[...] = jnp.zeros_like(hr_s)
            hi_s[...] = jnp.zeros_like(hi_s)

        hr, hi = hr_s[...], hi_s[...]
        hs_ref[0, :, :N_STATE] = hr
        hs_ref[0, :, N_STATE:] = hi
        y, nhr, nhi = _s5_chunk(_step_major(z_ref, slice(0, D_SSM)), _step_major(z_ref, slice(D_SSM, 2 * D_SSM)),
                                hr, hi, *[r[...] for r in p_refs])
        _store_step_major(y_ref, slice(0, D_SSM), y)
        hr_s[...] = nhr
        hi_s[...] = nhi

    return pl.pallas_call(
        body, name=name, grid=(nb, nc),
        in_specs=[pl.BlockSpec((S5_CHUNK, 2 * D_SSM), lambda b, c: (b * nc + c, 0))]
        + [_full(s) for s in _S5_PARAM_SHAPES],
        out_specs=[pl.BlockSpec((S5_CHUNK, D_SSM), lambda b, c: (b * nc + c, 0)),
                   pl.BlockSpec((1, 1, 2 * N_STATE), lambda b, c: (b * nc + c, 0, 0))],
        out_shape=[_sds((t, D_SSM)), _sds((nb * nc, 1, 2 * N_STATE))],
        scratch_shapes=[pltpu.VMEM((1, N_STATE), f32), pltpu.VMEM((1, N_STATE), f32)],
        compiler_params=_cp(2, VMEM_BIG),
    )(z, *params)


def s5_bwd(z, params, hs, dy, nb, name):
    t = z.shape[0]
    nc = t // nb // S5_CHUNK
    npar = len(_S5_PARAM_SHAPES)

    def body(z_ref, hs_ref, dy_ref, *rest):
        p_refs, dz_ref, dp_refs, (dhr_s, dhi_s) = rest[:npar], rest[npar], rest[npar + 1:2 * npar + 1], rest[2 * npar + 1:]

        @pl.when(pl.program_id(1) == 0)
        def _():
            dhr_s[...] = jnp.zeros_like(dhr_s)
            dhi_s[...] = jnp.zeros_like(dhi_s)

        prim = (_step_major(z_ref, slice(0, D_SSM)), _step_major(z_ref, slice(D_SSM, 2 * D_SSM)),
                hs_ref[0, :, :N_STATE], hs_ref[0, :, N_STATE:]) + tuple(r[...] for r in p_refs)
        _, vj = jax.vjp(_s5_chunk, *prim)
        cts = vj((_step_major(dy_ref, slice(0, D_SSM)), dhr_s[...], dhi_s[...]))
        _store_step_major(dz_ref, slice(0, D_SSM), cts[0])
        _store_step_major(dz_ref, slice(D_SSM, 2 * D_SSM), cts[1])
        dhr_s[...] = cts[2]
        dhi_s[...] = cts[3]
        first = (pl.program_id(0) == 0) & (pl.program_id(1) == 0)
        for r, v in zip(dp_refs, cts[4:]):
            _acc(r, v, first)

    rev = lambda b, c: (b * nc + nc - 1 - c, 0)
    return pl.pallas_call(
        body, name=name, grid=(nb, nc),
        in_specs=[pl.BlockSpec((S5_CHUNK, 2 * D_SSM), rev),
                  pl.BlockSpec((1, 1, 2 * N_STATE), lambda b, c: (b * nc + nc - 1 - c, 0, 0)),
                  pl.BlockSpec((S5_CHUNK, D_SSM), rev)] + [_full(s) for s in _S5_PARAM_SHAPES],
        out_specs=[pl.BlockSpec((S5_CHUNK, 2 * D_SSM), rev)] + [_full(s) for s in _S5_PARAM_SHAPES],
        out_shape=[_sds((t, 2 * D_SSM), _MXU)] + [_sds(s) for s in _S5_PARAM_SHAPES],
        scratch_shapes=[pltpu.VMEM((1, N_STATE), f32), pltpu.VMEM((1, N_STATE), f32)],
        compiler_params=_cp(2, VMEM_BIG),
    )(z, hs, dy, *params)


DN_PRE_ROWS = 256
DN_COLS = 3 * D_DN // LANE


def dn_pre_fwd(zq, convw, seq, name):
    t, tb = zq.shape[0], DN_PRE_ROWS
    per_seq = seq // tb

    def body(xc_ref, xp_ref, w_ref, o_ref):
        is_start = pl.program_id(0) % per_seq == 0
        for j in range(DN_COLS):
            cols = slice(j * LANE, (j + 1) * LANE)
            o_ref[:, cols] = _dn_pre(xc_ref[:, cols], xp_ref[:, cols], w_ref[0:1, cols], w_ref[1:2, cols],
                                     w_ref[2:3, cols], w_ref[3:4, cols], is_start, j)

    return pl.pallas_call(
        body, name=name, grid=(t // tb,),
        in_specs=[_rows(tb, 3 * D_DN),
                  pl.BlockSpec((DN_HALO, 3 * D_DN), lambda i: (jnp.maximum(i * (tb // DN_HALO) - 1, 0), 0)),
                  _full((DN_CONV, 3 * D_DN))],
        out_specs=_rows(tb, 3 * D_DN),
        out_shape=_sds((t, 3 * D_DN)),
        compiler_params=_cp(1, VMEM_BIG),
    )(zq, zq, convw)


def dn_pre_bwd(zq, convw, dqkv, seq, name):
    t, tb = zq.shape[0], DN_PRE_ROWS
    nrow = t // tb
    per_seq = seq // tb

    def body(xc_ref, xp_ref, w_ref, d_ref, dx_ref, dw_ref, carry):
        step = pl.program_id(0)
        i = nrow - 1 - step

        @pl.when(step == 0)
        def _():
            carry[...] = jnp.zeros_like(carry)

        for j in range(DN_COLS):
            cols = slice(j * LANE, (j + 1) * LANE)
            fn = functools.partial(_dn_pre, is_start=i % per_seq == 0, col=j)
            _, vj = jax.vjp(fn, xc_ref[:, cols], xp_ref[:, cols], w_ref[0:1, cols], w_ref[1:2, cols],
                            w_ref[2:3, cols], w_ref[3:4, cols])
            dxc, dxp, dw0, dw1, dw2, dw3 = vj(d_ref[:, cols])
            dx_ref[:tb - DN_HALO, cols] = dxc[:tb - DN_HALO].astype(_MXU)
            dx_ref[tb - DN_HALO:, cols] = (dxc[tb - DN_HALO:] + carry[:, cols]).astype(_MXU)
            carry[:, cols] = dxp
            for k, dw in enumerate((dw0, dw1, dw2, dw3)):
                @pl.when(step == 0)
                def _():
                    dw_ref[k:k + 1, cols] = dw

                @pl.when(step != 0)
                def _():
                    dw_ref[k:k + 1, cols] += dw

    rev = lambda s: (nrow - 1 - s, 0)
    return pl.pallas_call(
        body, name=name, grid=(nrow,),
        in_specs=[pl.BlockSpec((tb, 3 * D_DN), rev),
                  pl.BlockSpec((DN_HALO, 3 * D_DN),
                               lambda s: (jnp.maximum((nrow - 1 - s) * (tb // DN_HALO) - 1, 0), 0)),
                  _full((DN_CONV, 3 * D_DN)), pl.BlockSpec((tb, 3 * D_DN), rev)],
        out_specs=[pl.BlockSpec((tb, 3 * D_DN), rev), _full((DN_CONV, 3 * D_DN))],
        out_shape=[_sds((t, 3 * D_DN), _MXU), _sds((DN_CONV, 3 * D_DN))],
        scratch_shapes=[pltpu.VMEM((DN_HALO, 3 * D_DN), f32)],
        compiler_params=_cp(1, VMEM_BIG),
    )(zq, zq, convw, dqkv)


DN_LOCAL_CHUNKS = 4
DN_ATTN = DN_HEADS * DN_CHUNK


def _dn_heads(ref, rows, base=0):
    return [ref[rows, base + h * DN_HEAD_DIM:base + (h + 1) * DN_HEAD_DIM] for h in range(DN_HEADS)]


def dn_local_fwd(qkv, ab, alog, dtb, name):
    t = qkv.shape[0]
    c, n = DN_CHUNK, DN_LOCAL_CHUNKS

    def body(qkv_ref, ab_ref, alog_ref, dtb_ref, val_ref, kcd_ref, attn_ref, qd_ref, kd_ref, el_ref, inv_ref):
        rows = [pl.ds(j * c, c) for j in range(n)]
        vals, kcds, attns, qds, kds, els, invs = _dn_local(
            [_dn_heads(qkv_ref, r) for r in rows], [_dn_heads(qkv_ref, r, D_DN) for r in rows],
            [_dn_heads(qkv_ref, r, 2 * D_DN) for r in rows], [ab_ref[r, :] for r in rows], alog_ref[...], dtb_ref[...])
        for j, r in enumerate(rows):
            for h in range(DN_HEADS):
                lo, hi = h * DN_HEAD_DIM, (h + 1) * DN_HEAD_DIM
                val_ref[r, lo:hi] = vals[j][h]
                kcd_ref[r, lo:hi] = kcds[j][h].astype(_MXU)
                qd_ref[r, lo:hi] = qds[j][h].astype(_MXU)
                kd_ref[r, lo:hi] = kds[j][h].astype(_MXU)
                attn_ref[r, h * c:(h + 1) * c] = attns[j][h].astype(_MXU)
                inv_ref[r, h * c:(h + 1) * c] = invs[j][h]
            el_ref[j] = els[j]

    wide = _rows(n * c, D_DN)
    outs = pl.pallas_call(
        body, name=name, grid=(t // (n * c),),
        in_specs=[_rows(n * c, 3 * D_DN), _rows(n * c, LANE), _full((1, LANE)), _full((1, LANE))],
        out_specs=[wide, wide, _rows(n * c, DN_ATTN), wide, wide, pl.BlockSpec((n, 1, LANE), lambda i: (i, 0, 0)),
                   _rows(n * c, DN_ATTN)],
        out_shape=[_sds((t, D_DN)), _sds((t, D_DN), _MXU), _sds((t, DN_ATTN), _MXU), _sds((t, D_DN), _MXU),
                   _sds((t, D_DN), _MXU), _sds((t // c, 1, LANE)), _sds((t, DN_ATTN))],
        compiler_params=_cp(1),
    )(qkv, ab, alog, dtb)
    return outs[:6], outs[6]


def dn_local_bwd(qkv, ab, alog, dtb, inv, cts, name):
    t = qkv.shape[0]
    c, n = DN_CHUNK, DN_LOCAL_CHUNKS

    def body(qkv_ref, ab_ref, alog_ref, dtb_ref, inv_ref, dval_ref, dkcd_ref, dattn_ref, dqd_ref, dkd_ref, del_ref,
             dqkv_ref, dab_ref, dalog_ref, ddtb_ref):
        rows = [pl.ds(j * c, c) for j in range(n)]
        invs = [[inv_ref[r, h * c:(h + 1) * c] for h in range(DN_HEADS)] for r in rows]

        def local(qs, ks, vs, abs_, alog, dtb):
            return _dn_local(qs, ks, vs, abs_, alog, dtb, invs)[:6]

        _, vj = jax.vjp(local, [_dn_heads(qkv_ref, r) for r in rows], [_dn_heads(qkv_ref, r, D_DN) for r in rows],
                        [_dn_heads(qkv_ref, r, 2 * D_DN) for r in rows], [ab_ref[r, :] for r in rows], alog_ref[...],
                        dtb_ref[...])
        dattn = [[dattn_ref[r, h * c:(h + 1) * c] for h in range(DN_HEADS)] for r in rows]
        dq, dk, dv, dab, dalog, ddtb = vj(([_dn_heads(dval_ref, r) for r in rows], [_dn_heads(dkcd_ref, r) for r in rows],
                                           dattn, [_dn_heads(dqd_ref, r) for r in rows],
                                           [_dn_heads(dkd_ref, r) for r in rows], [del_ref[j] for j in range(n)]))
        for j, r in enumerate(rows):
            for h in range(DN_HEADS):
                lo, hi = h * DN_HEAD_DIM, (h + 1) * DN_HEAD_DIM
                dqkv_ref[r, lo:hi] = dq[j][h]
                dqkv_ref[r, D_DN + lo:D_DN + hi] = dk[j][h]
                dqkv_ref[r, 2 * D_DN + lo:2 * D_DN + hi] = dv[j][h]
            dab_ref[r, :] = dab[j].astype(_MXU)
        first = pl.program_id(0) == 0
        _acc(dalog_ref, dalog, first)
        _acc(ddtb_ref, ddtb, first)

    wide = _rows(n * c, D_DN)
    return pl.pallas_call(
        body, name=name, grid=(t // (n * c),),
        in_specs=[_rows(n * c, 3 * D_DN), _rows(n * c, LANE), _full((1, LANE)), _full((1, LANE)),
                  _rows(n * c, DN_ATTN), wide, wide, _rows(n * c, DN_ATTN), wide, wide,
                  pl.BlockSpec((n, 1, LANE), lambda i: (i, 0, 0))],
        out_specs=[_rows(n * c, 3 * D_DN), _rows(n * c, LANE), _full((1, LANE)), _full((1, LANE))],
        out_shape=[_sds((t, 3 * D_DN)), _sds((t, LANE), _MXU), _sds((1, LANE)), _sds((1, LANE))],
        compiler_params=_cp(1),
    )(qkv, ab, alog, dtb, inv, *cts)


def _seq_view(a, nb):
    return a.reshape((nb, a.shape[0] // nb) + a.shape[1:])


def _dn_chains(nb):
    return [(b, h) for b in range(nb) for h in range(DN_HEADS)]


DN_SCAN_CHUNKS = 4


def _dn_step_operands(val_ref, kcd_ref, attn_ref, qd_ref, kd_ref, el_ref, gg_ref, nb, j):
    chains = _dn_chains(nb)
    c = DN_CHUNK
    rows = pl.ds(j * c, c)

    def wide(ref):
        return [ref[b, rows, h * DN_HEAD_DIM:(h + 1) * DN_HEAD_DIM].astype(f32) for b, h in chains]

    attns = [attn_ref[b, rows, h * c:(h + 1) * c].astype(f32) for b, h in chains]
    return (wide(val_ref), wide(kcd_ref), attns, wide(qd_ref), wide(kd_ref),
            [el_ref[b, j, :, h:h + 1] for b, h in chains], wide(gg_ref))


def dn_scan_fwd(loc, gg, ng, nb, name):
    val, kcd, attn, qd, kd, el = loc
    t = val.shape[0]
    c, n = DN_CHUNK, DN_SCAN_CHUNKS
    nc = t // nb // c
    ns = nb * DN_HEADS

    def body(val_ref, kcd_ref, attn_ref, qd_ref, kd_ref, el_ref, gg_ref, ng_ref, y_ref, ss_ref, st):
        @pl.when(pl.program_id(0) == 0)
        def _():
            st[...] = jnp.zeros_like(st)

        sts = [st[i] for i in range(ns)]
        for j in range(n):
            for i in range(ns):
                ss_ref[j, i] = sts[i]
            ys, sts = _dn_step(*_dn_step_operands(val_ref, kcd_ref, attn_ref, qd_ref, kd_ref, el_ref, gg_ref, nb, j),
                               sts, ng_ref[...])
            for i, (b, h) in enumerate(_dn_chains(nb)):
                y_ref[b, pl.ds(j * c, c), h * DN_HEAD_DIM:(h + 1) * DN_HEAD_DIM] = ys[i]
        for i in range(ns):
            st[i] = sts[i]

    def blk(w):
        return pl.BlockSpec((nb, n * c, w), lambda k: (0, k, 0))

    el_spec = pl.BlockSpec((nb, n, 1, LANE), lambda k: (0, k, 0, 0))
    y, ss = pl.pallas_call(
        body, name=name, grid=(nc // n,),
        in_specs=[blk(D_DN), blk(D_DN), blk(DN_ATTN), blk(D_DN), blk(D_DN), el_spec, blk(D_DN), _full((1, LANE))],
        out_specs=[blk(D_DN), pl.BlockSpec((n, ns, DN_HEAD_DIM, DN_HEAD_DIM), lambda k: (k, 0, 0, 0))],
        out_shape=[_sds((nb, t // nb, D_DN)), _sds((nc, ns, DN_HEAD_DIM, DN_HEAD_DIM))],
        scratch_shapes=[pltpu.VMEM((ns, DN_HEAD_DIM, DN_HEAD_DIM), f32)],
        compiler_params=_cp(1, VMEM_BIG),
    )(_seq_view(val, nb), _seq_view(kcd, nb), _seq_view(attn, nb), _seq_view(qd, nb), _seq_view(kd, nb),
      el.reshape(nb, nc, 1, LANE), _seq_view(gg, nb), ng)
    return y.reshape(t, D_DN), ss


def dn_scan_bwd(loc, gg, ng, ss, dy, nb, name):
    val, kcd, attn, qd, kd, el = loc
    t = val.shape[0]
    c, n = DN_CHUNK, DN_SCAN_CHUNKS
    nc = t // nb // c
    ns = nb * DN_HEADS
    steps = nc // n

    def body(val_ref, kcd_ref, attn_ref, qd_ref, kd_ref, el_ref, gg_ref, ng_ref, ss_ref, dy_ref,
             dval_ref, dkcd_ref, dattn_ref, dqd_ref, dkd_ref, del_ref, dgg_ref, dng_ref, dst):
        @pl.when(pl.program_id(0) == 0)
        def _():
            dst[...] = jnp.zeros_like(dst)

        lane = lax.broadcasted_iota(jnp.int32, (1, LANE), 1)
        chains = _dn_chains(nb)
        ds = [dst[i] for i in range(ns)]
        dng_tot = jnp.zeros((1, LANE), f32)
        for j in reversed(range(n)):
            rows = pl.ds(j * c, c)
            _, vj = jax.vjp(_dn_step,
                            *_dn_step_operands(val_ref, kcd_ref, attn_ref, qd_ref, kd_ref, el_ref, gg_ref, nb, j),
                            [ss_ref[j, i] for i in range(ns)], ng_ref[...])
            dys = [dy_ref[b, rows, h * DN_HEAD_DIM:(h + 1) * DN_HEAD_DIM] for b, h in chains]
            dval, dkcd, dattn, dqd, dkd, dlast, dgg, ds, dng = vj((dys, ds))
            dng_tot = dng_tot + dng
            del_rows = [jnp.zeros((1, LANE), f32) for _ in range(nb)]
            for i, (b, h) in enumerate(chains):
                cols = slice(h * DN_HEAD_DIM, (h + 1) * DN_HEAD_DIM)
                dval_ref[b, rows, cols] = dval[i]
                dkcd_ref[b, rows, cols] = dkcd[i]
                dattn_ref[b, rows, h * c:(h + 1) * c] = dattn[i]
                dqd_ref[b, rows, cols] = dqd[i]
                dkd_ref[b, rows, cols] = dkd[i]
                dgg_ref[b, rows, cols] = dgg[i].astype(_MXU)
                del_rows[b] = del_rows[b] + jnp.where(lane == h, dlast[i], 0.0)
            for b in range(nb):
                del_ref[b, j] = del_rows[b]
        for i in range(ns):
            dst[i] = ds[i]
        _acc(dng_ref, dng_tot, pl.program_id(0) == 0)

    def blk(w):
        return pl.BlockSpec((nb, n * c, w), lambda k: (0, steps - 1 - k, 0))

    el_spec = pl.BlockSpec((nb, n, 1, LANE), lambda k: (0, steps - 1 - k, 0, 0))
    outs = pl.pallas_call(
        body, name=name, grid=(steps,),
        in_specs=[blk(D_DN), blk(D_DN), blk(DN_ATTN), blk(D_DN), blk(D_DN), el_spec, blk(D_DN), _full((1, LANE)),
                  pl.BlockSpec((n, ns, DN_HEAD_DIM, DN_HEAD_DIM), lambda k: (steps - 1 - k, 0, 0, 0)), blk(D_DN)],
        out_specs=[blk(D_DN), blk(D_DN), blk(DN_ATTN), blk(D_DN), blk(D_DN), el_spec, blk(D_DN), _full((1, LANE))],
        out_shape=[_sds((nb, t // nb, D_DN)), _sds((nb, t // nb, D_DN)), _sds((nb, t // nb, DN_ATTN)),
                   _sds((nb, t // nb, D_DN)), _sds((nb, t // nb, D_DN)), _sds((nb, nc, 1, LANE)),
                   _sds((nb, t // nb, D_DN), _MXU), _sds((1, LANE))],
        scratch_shapes=[pltpu.VMEM((ns, DN_HEAD_DIM, DN_HEAD_DIM), f32)],
        compiler_params=_cp(1, VMEM_BIG),
    )(_seq_view(val, nb), _seq_view(kcd, nb), _seq_view(attn, nb), _seq_view(qd, nb), _seq_view(kd, nb),
      el.reshape(nb, nc, 1, LANE), _seq_view(gg, nb), ng, ss, _seq_view(dy, nb))
    dloc = [o.reshape((t,) + o.shape[2:]) for o in outs[:5]] + [outs[5].reshape(t // c, 1, LANE)]
    return dloc, outs[6].reshape(t, D_DN), outs[7]


SG_ROWS = 512


def sg_fwd(z, lng, lnb, w, bt, name):
    t = z.shape[0]

    def body(z_ref, lng_ref, lnb_ref, w_ref, bt_ref, y_ref):
        ws = [w_ref[h] for h in range(SG_HEADS)]
        for k in range(SG_ROWS // SG_CHUNK):
            r = pl.ds(k * SG_CHUNK, SG_CHUNK)
            y_ref[r, :] = _sg_chunk(z_ref[r, :D_SG], z_ref[r, D_SG:2 * D_SG], z_ref[r, 2 * D_SG:], lng_ref[...],
                                    lnb_ref[...], ws, bt_ref[...])

    return pl.pallas_call(
        body, name=name, grid=(t // SG_ROWS,),
        in_specs=[_rows(SG_ROWS, 3 * D_SG), _full((1, D_SG)), _full((1, D_SG)),
                  _full((SG_HEADS, SG_CHUNK, SG_CHUNK)), _full((SG_CHUNK, LANE))],
        out_specs=_rows(SG_ROWS, D_SG),
        out_shape=_sds((t, D_SG)),
        compiler_params=_cp(1),
    )(z, lng, lnb, w, bt)


def sg_bwd(z, lng, lnb, w, bt, dy, name):
    t = z.shape[0]

    def body(z_ref, lng_ref, lnb_ref, w_ref, bt_ref, dy_ref, dz_ref, dlng_ref, dlnb_ref, dw_ref, dbt_ref):
        ws = [w_ref[h] for h in range(SG_HEADS)]
        tot = None
        for k in range(SG_ROWS // SG_CHUNK):
            r = pl.ds(k * SG_CHUNK, SG_CHUNK)
            _, vj = jax.vjp(_sg_chunk, z_ref[r, :D_SG], z_ref[r, D_SG:2 * D_SG], z_ref[r, 2 * D_SG:], lng_ref[...],
                            lnb_ref[...], ws, bt_ref[...])
            du, dv, dgate, dlng, dlnb, dws, dbt = vj(dy_ref[r, :])
            dz_ref[r, :D_SG] = du.astype(_MXU)
            dz_ref[r, D_SG:2 * D_SG] = dv.astype(_MXU)
            dz_ref[r, 2 * D_SG:] = dgate.astype(_MXU)
            part = [dlng, dlnb, dbt] + list(dws)
            tot = part if tot is None else [a + b for a, b in zip(tot, part)]
        first = pl.program_id(0) == 0
        _acc(dlng_ref, tot[0], first)
        _acc(dlnb_ref, tot[1], first)
        _acc(dbt_ref, tot[2], first)
        for h in range(SG_HEADS):
            @pl.when(first)
            def _():
                dw_ref[h] = tot[3 + h]

            @pl.when(jnp.logical_not(first))
            def _():
                dw_ref[h] += tot[3 + h]

    return pl.pallas_call(
        body, name=name, grid=(t // SG_ROWS,),
        in_specs=[_rows(SG_ROWS, 3 * D_SG), _full((1, D_SG)), _full((1, D_SG)),
                  _full((SG_HEADS, SG_CHUNK, SG_CHUNK)), _full((SG_CHUNK, LANE)), _rows(SG_ROWS, D_SG)],
        out_specs=[_rows(SG_ROWS, 3 * D_SG), _full((1, D_SG)), _full((1, D_SG)),
                   _full((SG_HEADS, SG_CHUNK, SG_CHUNK)), _full((SG_CHUNK, LANE))],
        out_shape=[_sds((t, 3 * D_SG), _MXU), _sds((1, D_SG)), _sds((1, D_SG)), _sds((SG_HEADS, SG_CHUNK, SG_CHUNK)),
                   _sds((SG_CHUNK, LANE))],
        compiler_params=_cp(1),
    )(z, lng, lnb, w, bt, dy)


def add_pairs(a_list, b_list, name):
    n = len(a_list)

    def body(*refs):
        for a_ref, b_ref, o_ref in zip(refs[:n], refs[n:2 * n], refs[2 * n:]):
            o_ref[...] = (a_ref[...].astype(f32) + b_ref[...].astype(f32)).astype(o_ref.dtype)

    return pl.pallas_call(
        body, name=name, out_shape=[_sds(a.shape, a.dtype) for a in a_list],
        compiler_params=pltpu.CompilerParams(vmem_limit_bytes=VMEM_BIG),
    )(*a_list, *b_list)


def _adamw(g, w, m, v):
    nm = ADAM_B1 * m + (1.0 - ADAM_B1) * g
    nv = ADAM_B2 * v + (1.0 - ADAM_B2) * jnp.square(g)
    m_hat = nm / (1.0 - ADAM_B1 ** ADAM_STEP)
    v_hat = nv / (1.0 - ADAM_B2 ** ADAM_STEP)
    return -ADAM_LR * (m_hat / (jnp.sqrt(v_hat) + ADAM_EPS) + ADAM_WD * w), nm, nv


def sum_parts(half, recv, name):
    _, r, c = recv.shape
    tr = 256 if r % 256 == 0 else r

    def body(half_ref, recv_ref, g_ref):
        g = recv_ref[0].astype(f32)
        for k in range(1, N_CHIPS):
            g = g + recv_ref[k].astype(f32)
        g_ref[...] = g

    return pl.pallas_call(
        body, name=name,
        grid_spec=pltpu.PrefetchScalarGridSpec(
            num_scalar_prefetch=1, grid=(r // tr,),
            in_specs=[pl.BlockSpec((N_CHIPS, tr, c), lambda i, h: (0, i, 0))],
            out_specs=pl.BlockSpec((None, tr, c), lambda i, h: (h[0], i, 0))),
        out_shape=_sds((2, r, c)),
        compiler_params=_cp(1, VMEM_BIG),
    )(half, recv)


def adamw(g, w, m, v, name):
    _, r, c = w.shape
    tr = 256 if r % 256 == 0 else r

    def body(g_ref, w_ref, m_ref, v_ref, d_ref, nm_ref, nv_ref):
        d_ref[...], nm_ref[...], nv_ref[...] = _adamw(g_ref[...], w_ref[...], m_ref[...], v_ref[...])

    blk = pl.BlockSpec((None, tr, c), lambda l, i: (l, i, 0))
    return pl.pallas_call(
        body, name=name, grid=(2, r // tr), in_specs=[blk] * 4, out_specs=[blk] * 3, out_shape=[_sds((2, r, c))] * 3,
        compiler_params=_cp(2, VMEM_BIG),
    )(g, w, m, v)


def sum_parts_small(chip, parts, sums, name):
    n = len(sums)

    def body(chip_ref, *refs):
        for part, own, out in zip(refs[:n], refs[n:2 * n], refs[2 * n:]):
            g = jnp.where(chip_ref[0] == 0, own[...], part[0])
            for q in range(1, N_CHIPS):
                g = g + jnp.where(chip_ref[0] == q, own[...], part[q])
            out[...] = g

    vmem = pl.BlockSpec(memory_space=pltpu.VMEM)
    return pl.pallas_call(
        body, name=name, in_specs=[pl.BlockSpec(memory_space=pltpu.SMEM)] + [vmem] * (2 * n), out_specs=[vmem] * n,
        out_shape=[_sds(s.shape) for s in sums], compiler_params=pltpu.CompilerParams(vmem_limit_bytes=VMEM_BIG),
    )(chip, *parts, *sums)


def adamw_small(gs, ws, ms, vs, name):
    n = len(ws)

    def body(*refs):
        ins, outs = refs[:4 * n], refs[4 * n:]
        for k in range(n):
            outs[k][...], outs[n + k][...], outs[2 * n + k][...] = _adamw(
                ins[k][...], ins[n + k][...], ins[2 * n + k][...], ins[3 * n + k][...])

    outs = pl.pallas_call(
        body, name=name, out_shape=[_sds(w.shape) for w in ws] * 3,
        compiler_params=pltpu.CompilerParams(vmem_limit_bytes=VMEM_BIG),
    )(*gs, *ws, *ms, *vs)
    return [outs[j * n:(j + 1) * n] for j in range(3)]


_ANY = pl.BlockSpec(memory_space=pltpu.HBM)
_MESH = pl.DeviceIdType.MESH


def _flip(v, bit):
    return 1 - v if bit else v


_CHIP_RELS = ((1, 0), (0, 1), (1, 1))


def _piece(ref, kind, q):
    if kind[0] == "slot":
        return ref.at[q]
    if kind[0] == "all":
        return ref
    _, axis, n = kind
    return ref.at[(slice(None),) * axis + (pl.ds(q * n, n),)]


def _piece_shape(shape, kind):
    if kind[0] == "slot":
        return tuple(shape[1:])
    if kind[0] == "all":
        return tuple(shape)
    _, axis, n = kind
    return tuple(shape[:axis]) + (n,) + tuple(shape[axis + 1:])


def gather_weights(shards, kinds, name):
    n = len(shards)

    def out_shape(s, kind):
        if kind[0] == "slot":
            return (N_CHIPS,) + tuple(s.shape)
        _, axis, w = kind
        return tuple(s.shape[:axis + 1]) + (N_CHIPS * w,) + tuple(s.shape[axis + 2:])

    def place(o_ref, kind, q, layer):
        if kind[0] == "slot":
            return o_ref.at[q, layer]
        return _piece(o_ref.at[layer], kind, q)

    def body(*refs):
        s_refs, o_refs = refs[:n], refs[n:2 * n]
        send_sems, recv_sems, fwd_send_sems, fwd_recv_sems = refs[2 * n:]
        x, y, c = lax.axis_index("x"), lax.axis_index("y"), lax.axis_index("c")
        mine = 2 * x + y
        sends, arrivals, forwards, fwd_arrivals = [], [], [], []
        for r, (fx, fy) in enumerate(_CHIP_RELS):
            px, py = _flip(x, fx), _flip(y, fy)
            peer = 2 * px + py
            for k in range(n):
                s = r * n + k
                sends.append(pltpu.make_async_remote_copy(
                    src_ref=s_refs[k].at[c], dst_ref=place(o_refs[k], kinds[k], mine, c), send_sem=send_sems.at[s],
                    recv_sem=recv_sems.at[s], device_id=(px, py, c), device_id_type=_MESH))
                arrivals.append(pltpu.make_async_remote_copy(
                    src_ref=s_refs[k].at[c], dst_ref=place(o_refs[k], kinds[k], peer, c), send_sem=send_sems.at[s],
                    recv_sem=recv_sems.at[s], device_id=(px, py, c), device_id_type=_MESH))
                block = place(o_refs[k], kinds[k], peer, c)
                forwards.append(pltpu.make_async_remote_copy(
                    src_ref=block, dst_ref=block, send_sem=fwd_send_sems.at[s], recv_sem=fwd_recv_sems.at[s],
                    device_id=(x, y, 1 - c), device_id_type=_MESH))
                other = place(o_refs[k], kinds[k], peer, 1 - c)
                fwd_arrivals.append(pltpu.make_async_remote_copy(
                    src_ref=other, dst_ref=other, send_sem=fwd_send_sems.at[s], recv_sem=fwd_recv_sems.at[s],
                    device_id=(x, y, 1 - c), device_id_type=_MESH))
        for cp in sends:
            cp.start()
        for arrived, fwd in zip(arrivals, forwards):
            arrived.wait_recv()
            fwd.start()
        for cp in fwd_arrivals:
            cp.wait_recv()
        for cp in sends + forwards:
            cp.wait_send()

    m = len(_CHIP_RELS) * n
    return pl.pallas_call(
        body, name=name, in_specs=[_ANY] * n, out_specs=[_ANY] * n,
        out_shape=[_sds(out_shape(s, k), s.dtype) for s, k in zip(shards, kinds)],
        scratch_shapes=[pltpu.SemaphoreType.DMA((m,))] * 4,
    )(*shards)


def _owned_by(owners, side):
    return [k for k, o in enumerate(owners) if o == side]


def exchange_halves(gs, smalls, owners, name):
    n, ns = len(gs), len(smalls)

    def body(*refs):
        g_refs, s_refs = refs[:n], refs[n:n + ns]
        got_refs, sgot_refs = refs[n + ns:2 * n + ns], refs[2 * n + ns:2 * (n + ns)]
        send_sems, recv_sems = refs[2 * (n + ns):]
        x, y, c = lax.axis_index("x"), lax.axis_index("y"), lax.axis_index("c")
        sibling = (x, y, 1 - c)
        swaps = [pltpu.make_async_remote_copy(
            src_ref=g_refs[k].at[1 - c], dst_ref=got_refs[k], send_sem=send_sems.at[k], recv_sem=recv_sems.at[k],
            device_id=sibling, device_id_type=_MESH) for k in range(n)]
        gives = [pltpu.make_async_remote_copy(
            src_ref=s_refs[k], dst_ref=sgot_refs[k], send_sem=send_sems.at[n + k], recv_sem=recv_sems.at[n + k],
            device_id=sibling, device_id_type=_MESH) for k in range(ns)]
        for cp in swaps:
            cp.start()
        for side in (0, 1):
            @pl.when(c == 1 - side)
            def _():
                for k in _owned_by(owners, side):
                    gives[k].start()
        for cp in swaps:
            cp.wait()
        for side in (0, 1):
            @pl.when(c == 1 - side)
            def _():
                for k in _owned_by(owners, side):
                    gives[k].wait_send()

            @pl.when(c == side)
            def _():
                for k in _owned_by(owners, side):
                    gives[k].wait_recv()

    outs = pl.pallas_call(
        body, name=name, in_specs=[_ANY] * (n + ns), out_specs=[_ANY] * (n + ns),
        out_shape=[_sds(g.shape[1:], g.dtype) for g in gs] + [_sds(s.shape, s.dtype) for s in smalls],
        scratch_shapes=[pltpu.SemaphoreType.DMA((n + ns,)), pltpu.SemaphoreType.DMA((n + ns,))],
    )(*gs, *smalls)
    return outs[:n], outs[n:]


def reduce_to_chips(ts, kinds, smalls, owners, name):
    n, ns = len(ts), len(smalls)

    def body(*refs):
        t_refs, s_refs = refs[:n], refs[n:n + ns]
        o_refs, so_refs = refs[n + ns:2 * n + ns], refs[2 * n + ns:2 * (n + ns)]
        send_sems, recv_sems = refs[2 * (n + ns):]
        x, y, c = lax.axis_index("x"), lax.axis_index("y"), lax.axis_index("c")
        mine = 2 * x + y
        sends, arrivals, small_sends, small_arrivals = [], [], [], []
        for r, (fx, fy) in enumerate(_CHIP_RELS):
            px, py = _flip(x, fx), _flip(y, fy)
            peer = 2 * px + py
            for k in range(n + ns):
                s = r * (n + ns) + k
                if k < n:
                    src, dst = _piece(t_refs[k], kinds[k], peer), o_refs[k]
                else:
                    src, dst = s_refs[k - n], so_refs[k - n]
                go = pltpu.make_async_remote_copy(
                    src_ref=src, dst_ref=dst.at[mine], send_sem=send_sems.at[s], recv_sem=recv_sems.at[s],
                    device_id=(px, py, c), device_id_type=_MESH)
                come = pltpu.make_async_remote_copy(
                    src_ref=src, dst_ref=dst.at[peer], send_sem=send_sems.at[s], recv_sem=recv_sems.at[s],
                    device_id=(px, py, c), device_id_type=_MESH)
                (sends if k < n else small_sends).append(go)
                (arrivals if k < n else small_arrivals).append(come)

        def owned(copies, side):
            return [cp for j, cp in enumerate(copies) if owners[j % ns] == side]

        for cp in sends:
            cp.start()
        for side in (0, 1):
            @pl.when(c == side)
            def _():
                for cp in owned(small_sends, side):
                    cp.start()
        for cp in arrivals:
            cp.wait_recv()
        for cp in sends:
            cp.wait_send()
        for side in (0, 1):
            @pl.when(c == side)
            def _():
                for cp in owned(small_arrivals, side):
                    cp.wait_recv()
                for cp in owned(small_sends, side):
                    cp.wait_send()

    m = len(_CHIP_RELS) * (n + ns)
    outs = pl.pallas_call(
        body, name=name, in_specs=[_ANY] * (n + ns), out_specs=[_ANY] * (n + ns),
        out_shape=[_sds((N_CHIPS,) + _piece_shape(t.shape, k), t.dtype) for t, k in zip(ts, kinds)]
        + [_sds((N_CHIPS,) + s.shape, s.dtype) for s in smalls],
        scratch_shapes=[pltpu.SemaphoreType.DMA((m,)), pltpu.SemaphoreType.DMA((m,))],
    )(*ts, *smalls)
    return outs[:n], outs[n:]


def share_halves(rs, smalls, owners, name):
    n, ns = len(rs), len(smalls)

    def body(*refs):
        o_refs, so_refs = refs[n + ns:2 * n + ns], refs[2 * n + ns:2 * (n + ns)]
        send_sems, recv_sems = refs[2 * (n + ns):]
        x, y, c = lax.axis_index("x"), lax.axis_index("y"), lax.axis_index("c")
        sibling = (x, y, 1 - c)
        swaps = [pltpu.make_async_remote_copy(
            src_ref=o_refs[k].at[c], dst_ref=o_refs[k].at[c], send_sem=send_sems.at[k], recv_sem=recv_sems.at[k],
            device_id=sibling, device_id_type=_MESH) for k in range(n)]
        arrivals = [pltpu.make_async_remote_copy(
            src_ref=o_refs[k].at[c], dst_ref=o_refs[k].at[1 - c], send_sem=send_sems.at[k], recv_sem=recv_sems.at[k],
            device_id=sibling, device_id_type=_MESH) for k in range(n)]
        gives = [pltpu.make_async_remote_copy(
            src_ref=so_refs[k], dst_ref=so_refs[k], send_sem=send_sems.at[n + k], recv_sem=recv_sems.at[n + k],
            device_id=sibling, device_id_type=_MESH) for k in range(ns)]
        for cp in swaps:
            cp.start()
        for side in (0, 1):
            @pl.when(c == side)
            def _():
                for k in _owned_by(owners, side):
                    gives[k].start()
        for cp in arrivals:
            cp.wait_recv()
        for cp in swaps:
            cp.wait_send()
        for side in (0, 1):
            @pl.when(c == side)
            def _():
                for k in _owned_by(owners, side):
                    gives[k].wait_send()

            @pl.when(c == 1 - side)
            def _():
                for k in _owned_by(owners, side):
                    gives[k].wait_recv()

    outs = pl.pallas_call(
        body, name=name, in_specs=[_ANY] * (n + ns), out_specs=[_ANY] * (n + ns),
        out_shape=[_sds(r.shape, r.dtype) for r in list(rs) + list(smalls)],
        input_output_aliases={k: k for k in range(n + ns)},
        scratch_shapes=[pltpu.SemaphoreType.DMA((n + ns,)), pltpu.SemaphoreType.DMA((n + ns,))],
    )(*rs, *smalls)
    return outs[:n], outs[n:]


def _small_view(a):
    if a.size < 8 * LANE:
        return jnp.pad(a.reshape(-1), (0, 8 * LANE - a.size)).reshape(8, LANE)
    if a.ndim == 1:
        return a.reshape(1, a.shape[0])
    if a.ndim == 4 and a.shape[-1] < LANE:
        return a.reshape(a.shape[0], a.shape[1], a.shape[2] * a.shape[3])
    return a


def _permuted_from_shards(shards):
    parts = []
    for lo, hi in GROUP_COLS:
        for q in range(N_CHIPS):
            a, b = max(lo, q * SHARD_COLS), min(hi, (q + 1) * SHARD_COLS)
            if a < b:
                parts.append(shards[q][..., a - q * SHARD_COLS:b - q * SHARD_COLS])
    pad = jnp.zeros(shards[0].shape[:-1] + (D_IN_PAD - D_IN,), shards[0].dtype)
    return jnp.concatenate(parts + [pad], axis=-1)


def _shards_from_groups(groups):
    in_order = sorted(range(len(GROUP_COLS)), key=lambda j: GROUP_COLS[j][0])
    shards = []
    for q in range(N_CHIPS):
        parts = []
        for j in in_order:
            lo, hi = GROUP_COLS[j]
            a, b = max(lo, q * SHARD_COLS), min(hi, (q + 1) * SHARD_COLS)
            if a < b:
                parts.append(groups[j][..., a - lo:b - lo])
        shards.append(jnp.concatenate(parts, axis=-1))
    return shards


def _expand_b(b):
    eye = jnp.eye(SSM_GROUPS, dtype=b.dtype)
    return jnp.einsum("gnc,gh->gchn", b, eye).reshape(D_SSM, N_STATE)


def _extract_b(e):
    return jnp.einsum("gcgn->gnc", e.reshape(SSM_GROUPS, SSM_GROUP, SSM_GROUPS, SSM_STATE))


def _expand_c(c):
    eye = jnp.eye(SSM_GROUPS, dtype=c.dtype)
    return jnp.einsum("gcn,gh->gnhc", c, eye).reshape(N_STATE, D_SSM)


def _extract_c(e):
    return jnp.einsum("gngc->gcn", e.reshape(SSM_GROUPS, SSM_STATE, SSM_GROUPS, SSM_GROUP))


def _lane_row(v):
    return jnp.pad(v, (0, LANE - v.shape[0])).reshape(1, LANE)


def _layer_params(w, l):
    return dict(
        norm_g=w["norm_g"][l][None], win=w["w_in_perm"][l], wout=w["w_out"][l].astype(_MXU),
        pg=w["ple_norm_g"][l][None], wgate=w["w_ple_gate"][l].astype(_MXU), wple=w["w_ple"][l].astype(_MXU),
        are=w["ssm_a_re"][l].reshape(1, N_STATE), aim=w["ssm_a_im"][l].reshape(1, N_STATE),
        ls=jnp.repeat(w["ssm_log_step"][l], SSM_STATE).reshape(1, N_STATE),
        bre=_expand_b(w["ssm_b_re"][l]), bim=_expand_b(w["ssm_b_im"][l]),
        cr=_expand_c(w["ssm_c_re"][l]), ci=_expand_c(w["ssm_c_im"][l]),
        dr=w["ssm_d"][l].reshape(1, D_SSM), wglu=w["ssm_w_glu"][l].astype(f32), bglu=w["ssm_b_glu"][l][None],
        convw=w["dn_conv_w"][l], alog=_lane_row(w["dn_a_log"][l]), dtb=_lane_row(w["dn_dt_bias"][l]),
        ng=w["dn_norm_g"][l][None],
        lng=w["sg_ln_g"][l][None], lnb=w["sg_ln_b"][l][None], sgw=w["sg_w"][l],
        bt=jnp.pad(w["sg_b"][l].T, ((0, 0), (0, LANE - SG_HEADS))),
    )


def _layer_fwd(x, p, lp, nb, tag):
    seq = x.shape[0] // nb
    h, zs, zq, zg, zsg, zab = in_fwd(x, lp["norm_g"], lp["win"], f"in_fwd{tag}")
    prep = s5_prep_fwd(lp["are"], lp["aim"], lp["ls"], lp["bre"], lp["bim"], f"s5_prep_fwd{tag}")
    s5p = tuple(prep) + (lp["cr"], lp["ci"], lp["dr"], lp["wglu"], lp["bglu"])
    ys, hs = s5_fwd(zs, s5p, nb, f"s5_fwd{tag}")
    qkv = dn_pre_fwd(zq, lp["convw"], seq, f"dn_pre_fwd{tag}")
    loc, inv = dn_local_fwd(qkv, zab, lp["alog"], lp["dtb"], f"dn_local_fwd{tag}")
    yd, ss = dn_scan_fwd(loc, zg, lp["ng"], nb, f"dn_scan_fwd{tag}")
    yg = sg_fwd(zsg, lp["lng"], lp["lnb"], lp["sgw"], lp["bt"], f"sg_fwd{tag}")
    x2, x1, y, hn = post_fwd(x, ys, yd, yg, p, lp["wout"], lp["pg"], lp["wgate"], lp["wple"], f"post_fwd{tag}")
    saved = dict(x=x, h=h, zs=zs, zq=zq, zg=zg, zsg=zsg, zab=zab, s5p=s5p, hs=hs, qkv=qkv, loc=loc, inv=inv, ss=ss, x1=x1, y=y, hn=hn, p=p)
    return x2, saved


def _layer_bwd(dx2, sv, lp, nb, tag):
    seq = dx2.shape[0] // nb
    dx1, dgp, dpp, dys, dyd, dyg, dpg = post_bwd(dx2, sv["x1"], sv["hn"], sv["p"], lp["wout"], lp["pg"], lp["wgate"],
                                                 lp["wple"], f"post_bwd{tag}")
    g = {}
    g["w_out"] = wgrad(sv["y"], dx1, f"wgrad_out{tag}")
    g["w_ple_gate"] = wgrad(sv["hn"], dgp, f"wgrad_gate{tag}")
    g["w_ple"] = wgrad(sv["p"], dpp, f"wgrad_ple{tag}")
    g["ple_norm_g"] = dpg[0]
    dzsg, dlng, dlnb, dsgw, dbt = sg_bwd(sv["zsg"], lp["lng"], lp["lnb"], lp["sgw"], lp["bt"], dyg, f"sg_bwd{tag}")
    g["sg_ln_g"], g["sg_ln_b"], g["sg_w"], g["sg_b"] = dlng[0], dlnb[0], dsgw, dbt[:, :SG_HEADS].T
    dloc, dzg, dng = dn_scan_bwd(sv["loc"], sv["zg"], lp["ng"], sv["ss"], dyd, nb, f"dn_scan_bwd{tag}")
    dqkv, dzab, dalog, ddtb = dn_local_bwd(sv["qkv"], sv["zab"], lp["alog"], lp["dtb"], sv["inv"], dloc,
                                           f"dn_local_bwd{tag}")
    dzq, dconv = dn_pre_bwd(sv["zq"], lp["convw"], dqkv, seq, f"dn_pre_bwd{tag}")
    g["dn_conv_w"], g["dn_a_log"], g["dn_dt_bias"], g["dn_norm_g"] = dconv, dalog[0, :DN_HEADS], ddtb[0, :DN_HEADS], dng[0]
    s5out = s5_bwd(sv["zs"], sv["s5p"], sv["hs"], dys, nb, f"s5_bwd{tag}")
    dzs, dprep, (dcr, dci, ddr, dwglu, dbglu) = s5out[0], s5out[1:1 + S5_PREPARED], s5out[1 + S5_PREPARED:]
    dare, daim, dls, dbre, dbim = s5_prep_bwd(lp["are"], lp["aim"], lp["ls"], lp["bre"], lp["bim"], dprep,
                                              f"s5_prep_bwd{tag}")
    g["ssm_a_re"] = dare.reshape(SSM_GROUPS, SSM_STATE)
    g["ssm_a_im"] = daim.reshape(SSM_GROUPS, SSM_STATE)
    g["ssm_log_step"] = dls.reshape(SSM_GROUPS, SSM_STATE).sum(axis=1)
    g["ssm_b_re"], g["ssm_b_im"] = _extract_b(dbre), _extract_b(dbim)
    g["ssm_c_re"], g["ssm_c_im"] = _extract_c(dcr), _extract_c(dci)
    g["ssm_d"] = ddr.reshape(SSM_GROUPS, SSM_GROUP)
    g["ssm_w_glu"], g["ssm_b_glu"] = dwglu, dbglu[0]
    dzs_all = (dzs, dzq, dzg, dzsg, dzab)
    dx, dng_in = in_bwd(sv["x"], lp["norm_g"], lp["win"], dzs_all, dx1, f"in_bwd{tag}")
    g["w_in_pieces"] = [wgrad(sv["h"], dz, f"wgrad_in{k}{tag}") for k, dz in enumerate(dzs_all)]
    g["norm_g"] = dng_in[0]
    return dx, g


def _local_step(x, p, target, w, nb):
    lps = [_layer_params(w, l) for l in range(DEPTH)]
    saved = []
    for l in range(DEPTH):
        x, sv = _layer_fwd(x, p[l], lps[l], nb, f"_l{l}")
        saved.append(sv)
    loss_blk, dx, dfg = loss_fwd_bwd(x, w["final_norm_g"][None], target, "loss")
    grads = [None] * DEPTH
    for l in reversed(range(DEPTH)):
        dx, grads[l] = _layer_bwd(dx, saved[l], lps[l], nb, f"_l{l}")
    out = {k: jnp.stack([grads[l][k] for l in range(DEPTH)]) for k in grads[0] if k != "w_in_pieces"}
    out["w_in_pieces"] = [grads[l]["w_in_pieces"] for l in range(DEPTH)]
    out["final_norm_g"] = dfg[0]
    return loss_blk[0, 0], dx, out


def kernel(x, p, norm_g, w_in, ssm_a_re, ssm_a_im, ssm_b_re, ssm_b_im, ssm_c_re, ssm_c_im, ssm_d, ssm_log_step, ssm_w_glu, ssm_b_glu, dn_conv_w, dn_a_log, dn_dt_bias, dn_norm_g, sg_ln_g, sg_ln_b, sg_w, sg_b, w_out, ple_norm_g, w_ple_gate, w_ple, final_norm_g, loss_target, m_norm_g, m_w_in, m_ssm_a_re, m_ssm_a_im, m_ssm_b_re, m_ssm_b_im, m_ssm_c_re, m_ssm_c_im, m_ssm_d, m_ssm_log_step, m_ssm_w_glu, m_ssm_b_glu, m_dn_conv_w, m_dn_a_log, m_dn_dt_bias, m_dn_norm_g, m_sg_ln_g, m_sg_ln_b, m_sg_w, m_sg_b, m_w_out, m_ple_norm_g, m_w_ple_gate, m_w_ple, m_final_norm_g, v_norm_g, v_w_in, v_ssm_a_re, v_ssm_a_im, v_ssm_b_re, v_ssm_b_im, v_ssm_c_re, v_ssm_c_im, v_ssm_d, v_ssm_log_step, v_ssm_w_glu, v_ssm_b_glu, v_dn_conv_w, v_dn_a_log, v_dn_dt_bias, v_dn_norm_g, v_sg_ln_g, v_sg_ln_b, v_sg_w, v_sg_b, v_w_out, v_ple_norm_g, v_w_ple_gate, v_w_ple, v_final_norm_g):
    args = locals()
    w = {n: args[n] for n in WEIGHTS}
    m = {n: args["m_" + n] for n in WEIGHTS}
    v = {n: args["v_" + n] for n in WEIGHTS}
    nb, seq = x.shape[0], x.shape[1]
    t = nb * seq

    full = _gather_full(w)
    loss_local, dx, grads = _local_step(x.reshape(t, D_MODEL), p.reshape(DEPTH, t, D_PLE),
                                        loss_target.reshape(t, D_MODEL), full, nb)
    outs, loss = _reduce_and_update(grads, w, m, v, loss_local)
    return (loss, dx.reshape(nb, seq, D_MODEL), *[outs[0][n] for n in WEIGHTS], *[outs[1][n] for n in WEIGHTS],
            *[outs[2][n] for n in WEIGHTS], *[outs[3][n] for n in WEIGHTS])


def _gather_full(w):
    sh_names = [n for n, _ in SHARDED]
    shards = [w[n] if n == "dn_conv_w" else w[n].astype(_COMM) for n in sh_names]
    gathered = gather_weights(shards, [k for _, k in SHARDED], "gather_weights")
    chip = 2 * lax.axis_index("x") + lax.axis_index("y")
    full = {n: w[n] for n in REPLICATED}
    for (n, kind), shard, got in zip(SHARDED, shards, gathered):
        if kind[0] == "slot":
            full[n] = lax.dynamic_update_index_in_dim(got, shard, chip, 0)
        else:
            full[n] = lax.dynamic_update_slice_in_dim(got, shard, chip * kind[2], axis=kind[1] + 1)
    slots = full.pop("w_in")
    full["w_in_perm"] = _permuted_from_shards([slots[q] for q in range(N_CHIPS)]).astype(_MXU)
    return full


def _reduce_and_update(grads, w, m, v, loss_local):
    sh_names = [n for n, _ in SHARDED]
    sh_kinds = [k for _, k in SHARDED]
    owners = [SMALL_OWNER[n] for n in REPLICATED + ("loss",)]

    def small_views(d):
        return [_small_view(d[n]) for n in REPLICATED]

    grads["w_in"] = jnp.stack([jnp.stack(_shards_from_groups(pieces)) for pieces in grads["w_in_pieces"]])
    gs = [grads[n] if n == "dn_conv_w" else grads[n].astype(_COMM) for n in sh_names]
    sm = small_views(grads) + [_small_view(loss_local.reshape(1))]
    core = lax.axis_index("c")
    chip = 2 * lax.axis_index("x") + lax.axis_index("y")
    got, sm_got = exchange_halves(gs, sm, owners, "exchange_halves")
    sums = add_pairs([lax.dynamic_index_in_dim(g, core, 0, keepdims=False) for g in gs] + sm, list(got) + list(sm_got),
                     "add_halves")
    sums, sm_sums = sums[:len(gs)], sums[len(gs):]
    parts, sm_parts = reduce_to_chips(sums, sh_kinds, sm_sums, owners, "reduce_to_chips")
    parts = list(parts)
    for k, (kind, total) in enumerate(zip(sh_kinds, sums)):
        if kind[0] == "slot":
            own = lax.dynamic_index_in_dim(total, chip, 0, keepdims=False)
        else:
            own = lax.dynamic_slice_in_dim(total, chip * kind[2], kind[2], axis=kind[1])
        parts[k] = lax.dynamic_update_index_in_dim(parts[k], own, chip, 0)
    half = core.astype(jnp.int32).reshape(1)
    totals = [sum_parts(half, part, f"sum_{n}") for n, part in zip(sh_names, parts)]
    sm_totals = sum_parts_small(chip.astype(jnp.int32).reshape(1), sm_parts, sm_sums, "sum_replicated")
    g_big, g_small = share_halves(totals, sm_totals, owners, "share_halves")
    outs = [dict(zip(sh_names, g_big)), {}, {}, {}]
    for n, g in zip(sh_names, g_big):
        outs[1][n], outs[2][n], outs[3][n] = adamw(g, w[n], m[n], v[n], f"adamw_{n}")
    small_results = [g_small[:-1]] + adamw_small(g_small[:-1], small_views(w), small_views(m), small_views(v),
                                                 "adamw_replicated")
    for j in range(4):
        for n, r in zip(REPLICATED, small_results[j]):
            outs[j][n] = r.reshape(-1)[:w[n].size].reshape(w[n].shape)
    return outs, g_small[-1][0, 0]
```

```python
import functools

import jax
import jax.numpy as jnp
from jax import lax
from jax.experimental import pallas as pl
from jax.experimental.pallas import tpu as pltpu

f32 = jnp.float32
bf16 = jnp.bfloat16

_MXU = bf16
_COMM = bf16
HIGH = lax.Precision.HIGH

D_MODEL = 1024
DEPTH = 2
D_PLE = 256
D_SSM = 256
D_DN = 512
D_SG = 256
SSM_GROUPS = 16
SSM_GROUP = 16
SSM_STATE = 64
N_STATE = SSM_GROUPS * SSM_STATE
DN_HEADS = 4
DN_HEAD_DIM = 128
DN_CONV = 4
DN_HALO = 16
DN_CHUNK = 64
SG_HEADS = 4
SG_HEAD_DIM = 64
SG_CHUNK = 128
S5_CHUNK = 512
S5_GROUP_ROWS = 8
EPS = 1e-6
D_IN = 3336
D_IN_PAD = 3456
LANE = 128

ADAM_LR = 0.001
ADAM_B1 = 0.9
ADAM_B2 = 0.999
ADAM_EPS = 1e-08
ADAM_WD = 0.01
ADAM_STEP = 10

N_CHIPS = 4
N_DEV = 8

Z_COLS = ((0, 512), (512, 2048), (2048, 2560), (2560, 3328), (3328, 3456))

GROUP_COLS = ((0, 512), (512, 2048), (2056, 2568), (2568, 3336), (2048, 2056))
SHARD_COLS = D_IN // 4

SHARDED = (("w_in", ("slot",)), ("ssm_w_glu", ("win", 0, 64)), ("dn_conv_w", ("win", 1, 384)),
           ("w_out", ("win", 0, 256)), ("w_ple_gate", ("win", 0, 256)), ("w_ple", ("win", 1, 256)))
REPLICATED = ("norm_g", "ssm_a_re", "ssm_a_im", "ssm_b_re", "ssm_b_im", "ssm_c_re", "ssm_c_im", "ssm_d",
              "ssm_log_step", "ssm_b_glu", "dn_a_log", "dn_dt_bias", "dn_norm_g", "sg_ln_g", "sg_ln_b", "sg_w",
              "sg_b", "ple_norm_g", "final_norm_g")
SMALL_OWNER = {n: int(n.startswith("ssm_")) for n in REPLICATED + ("loss",)}
WEIGHTS = ("norm_g", "w_in", "ssm_a_re", "ssm_a_im", "ssm_b_re", "ssm_b_im", "ssm_c_re", "ssm_c_im", "ssm_d",
           "ssm_log_step", "ssm_w_glu", "ssm_b_glu", "dn_conv_w", "dn_a_log", "dn_dt_bias", "dn_norm_g", "sg_ln_g",
           "sg_ln_b", "sg_w", "sg_b", "w_out", "ple_norm_g", "w_ple_gate", "w_ple", "final_norm_g")

VMEM_BIG = 56 * 1024 * 1024


def _mm(a, b):
    return jnp.dot(a.astype(_MXU), b.astype(_MXU), preferred_element_type=f32)


def _mm_nt(a, b):
    return lax.dot_general(a.astype(_MXU), b.astype(_MXU), (((1,), (1,)), ((), ())), preferred_element_type=f32)


def _mm_tn(a, b):
    return lax.dot_general(a.astype(_MXU), b.astype(_MXU), (((0,), (0,)), ((), ())), preferred_element_type=f32)


@jax.custom_vjp
def bdot(a, b):
    return _mm(a, b)


def _bdot_fwd(a, b):
    return _mm(a, b), (a, b)


def _bdot_bwd(res, g):
    a, b = res
    return _mm_nt(g, b).astype(a.dtype), _mm_tn(a, g).astype(b.dtype)


bdot.defvjp(_bdot_fwd, _bdot_bwd)


@jax.custom_vjp
def bdot_nt(a, b):
    return _mm_nt(a, b)


def _bdot_nt_fwd(a, b):
    return _mm_nt(a, b), (a, b)


def _bdot_nt_bwd(res, g):
    a, b = res
    return _mm(g, b).astype(a.dtype), _mm_tn(g, a).astype(b.dtype)


bdot_nt.defvjp(_bdot_nt_fwd, _bdot_nt_bwd)


@jax.custom_vjp
def bdot_tn(a, b):
    return _mm_tn(a, b)


def _bdot_tn_fwd(a, b):
    return _mm_tn(a, b), (a, b)


def _bdot_tn_bwd(res, g):
    a, b = res
    return _mm_nt(b, g).astype(a.dtype), _mm(a, g).astype(b.dtype)


bdot_tn.defvjp(_bdot_tn_fwd, _bdot_tn_bwd)


def hdot(a, b):
    return jnp.dot(a, b, precision=HIGH, preferred_element_type=f32)


def _unit_lower_inverses(ms):
    n = ms[0].shape[0]
    eye = (lax.broadcasted_iota(jnp.int32, (n, n), 0) == lax.broadcasted_iota(jnp.int32, (n, n), 1)).astype(f32)
    pw = [-m for m in ms]
    inv = [eye + p for p in pw]
    for _ in range(n.bit_length() - 2):
        pw = [hdot(p, p) for p in pw]
        inv = [a + hdot(a, p) for a, p in zip(inv, pw)]
    return inv


@jax.custom_vjp
def solve_unit_lower(ms, rhs, inv):
    return [hdot(a, r) for a, r in zip(inv, rhs)]


def _solve_unit_lower_fwd(ms, rhs, inv):
    xs = [hdot(a, r) for a, r in zip(inv, rhs)]
    return xs, (inv, xs)


def _solve_unit_lower_bwd(res, gs):
    inv, xs = res
    d_rhs = [lax.dot_general(a, g, (((0,), (0,)), ((), ())), precision=HIGH, preferred_element_type=f32)
             for a, g in zip(inv, gs)]
    d_ms = [-lax.dot_general(d, x, (((1,), (1,)), ((), ())), precision=HIGH, preferred_element_type=f32)
            for d, x in zip(d_rhs, xs)]
    return d_ms, d_rhs, [jnp.zeros_like(a) for a in inv]


solve_unit_lower.defvjp(_solve_unit_lower_fwd, _solve_unit_lower_bwd)


@functools.partial(jax.custom_vjp, nondiff_argnums=(1,))
def roll_rows(x, k):
    return pltpu.roll(x, k, 0)


def _roll_rows_fwd(x, k):
    return pltpu.roll(x, k, 0), None


def _roll_rows_bwd(k, _, g):
    return (pltpu.roll(g, g.shape[0] - k, 0),)


roll_rows.defvjp(_roll_rows_fwd, _roll_rows_bwd)


def _row_ids(shape):
    return lax.broadcasted_iota(jnp.int32, shape, 0)


def _rms(x, g):
    return x * lax.rsqrt(jnp.mean(x * x, axis=-1, keepdims=True) + EPS) * g


def _layer_norm(x, g, b):
    mu = jnp.mean(x, axis=-1, keepdims=True)
    xc = x - mu
    return xc * lax.rsqrt(jnp.mean(xc * xc, axis=-1, keepdims=True) + EPS) * g + b


def _s5_prep(are, aim, ls, bre, bim):
    step = jnp.exp(ls)
    mag = jnp.exp(are * step)
    lr = mag * jnp.cos(aim * step)
    li = mag * jnp.sin(aim * step)
    den = are * are + aim * aim
    nr, ni = lr - 1.0, li
    fr = (nr * are + ni * aim) / den
    fi = (ni * are - nr * aim) / den
    bbr = fr * bre - fi * bim
    bbi = fr * bim + fi * bre
    pr = jnp.broadcast_to(lr, (S5_GROUP_ROWS, N_STATE))
    pi = jnp.broadcast_to(li, (S5_GROUP_ROWS, N_STATE))
    d = 1
    while d < S5_GROUP_ROWS:
        keep = _row_ids(pr.shape) >= d
        sr, si = roll_rows(pr, d), roll_rows(pi, d)
        pr, pi = jnp.where(keep, pr * sr - pi * si, pr), jnp.where(keep, pr * si + pi * sr, pi)
        d *= 2
    return pr, pi, bbr, bbi


def _s5_chunk(u, gate, hr, hi, pr, pi, bbr, bbi, cr, ci, dr, wglu, bglu):
    n, steps = u.shape[0], S5_GROUP_ROWS
    groups = n // steps
    xr = bdot(u, bbr)
    xi = bdot(u, bbi)
    lr, li = pr[0:1], pi[0:1]
    rs, ims = [xr[:groups]], [xi[:groups]]
    for t in range(1, steps):
        a, b = rs[-1], ims[-1]
        rs.append(xr[t * groups:(t + 1) * groups] + lr * a - li * b)
        ims.append(xi[t * groups:(t + 1) * groups] + lr * b + li * a)
    er, ei = rs[-1], ims[-1]
    mr, mi = pr[steps - 1:steps], pi[steps - 1:steps]
    gid = _row_ids(er.shape)
    er, ei = (er + jnp.where(gid == 0, mr * hr - mi * hi, 0.0), ei + jnp.where(gid == 0, mr * hi + mi * hr, 0.0))
    d = 1
    while d < groups:
        sr = jnp.where(gid >= d, roll_rows(er, d), 0.0)
        si = jnp.where(gid >= d, roll_rows(ei, d), 0.0)
        er, ei = er + mr * sr - mi * si, ei + mr * si + mi * sr
        mr, mi = mr * mr - mi * mi, 2.0 * mr * mi
        d *= 2
    before_r = jnp.where(gid == 0, hr, roll_rows(er, 1))
    before_i = jnp.where(gid == 0, hi, roll_rows(ei, 1))
    xr = jnp.concatenate([rs[t] + pr[t:t + 1] * before_r - pi[t:t + 1] * before_i for t in range(steps)], axis=0)
    xi = jnp.concatenate([ims[t] + pr[t:t + 1] * before_i + pi[t:t + 1] * before_r for t in range(steps)], axis=0)
    y = bdot(xr, cr) - bdot(xi, ci) + dr * u
    y = jax.nn.gelu(y)
    y = y * jax.nn.sigmoid(bdot(y, wglu) + bglu)
    return y * jax.nn.silu(gate), er[groups - 1:groups], ei[groups - 1:groups]


def _dn_pre(xc, xp, w0, w1, w2, w3, is_start, col):
    xp = jnp.where(is_start, 0.0, xp)
    halo_rows = _row_ids(xp.shape)
    acc = w3 * xc
    for d, w in ((1, w2), (2, w1), (3, w0)):
        r = roll_rows(xc, d)
        head = jnp.where(halo_rows >= d, r[:DN_HALO], roll_rows(xp, d))
        acc = acc + w * jnp.concatenate([head, r[DN_HALO:]], axis=0)
    y = jax.nn.silu(acc)
    if col >= 2 * DN_HEADS:
        return y
    nrm = y * lax.rsqrt(jnp.sum(y * y, axis=-1, keepdims=True) + EPS)
    return nrm * DN_HEAD_DIM ** -0.5 if col < DN_HEADS else nrm


def _dn_local(qs, ks, vs, abs_, alog, dtb, invs=None):
    c = DN_CHUNK
    ri = lax.broadcasted_iota(jnp.int32, (c, c), 0)
    ci = lax.broadcasted_iota(jnp.int32, (c, c), 1)
    causal, strict = ri >= ci, ri > ci
    tril = causal.astype(f32)
    gcums = [hdot(tril, -jnp.exp(alog) * jax.nn.softplus(ab + dtb)) for ab in abs_]
    gcum_ts = [g.T for g in gcums]
    sigs = [jax.nn.sigmoid(ab) for ab in abs_]
    chains = [(j, h) for j in range(len(abs_)) for h in range(DN_HEADS)]
    gc = [gcums[j][:, h:h + 1] for j, h in chains]
    decay = [jnp.where(causal, jnp.exp(jnp.where(causal, gc[n] - gcum_ts[j][h:h + 1, :], 0.0)), 0.0)
             for n, (j, h) in enumerate(chains)]
    beta = [sigs[j][:, DN_HEADS + h:DN_HEADS + h + 1] for j, h in chains]
    kb = [ks[j][h] * beta[n] for n, (j, h) in enumerate(chains)]
    ms = [jnp.where(strict, bdot_nt(kb[n], ks[j][h]) * decay[n], 0.0) for n, (j, h) in enumerate(chains)]
    egc = [jnp.exp(g) for g in gc]
    rhs = [jnp.concatenate([vs[j][h] * beta[n], kb[n] * egc[n]], axis=1) for n, (j, h) in enumerate(chains)]
    inv = _unit_lower_inverses(ms) if invs is None else [invs[j][h] for j, h in chains]
    sol = solve_unit_lower(ms, rhs, inv)
    values = [s[:, :DN_HEAD_DIM] for s in sol]
    k_cds = [s[:, DN_HEAD_DIM:] for s in sol]
    attns = [bdot_nt(qs[j][h], ks[j][h]) * decay[n] for n, (j, h) in enumerate(chains)]
    q_decs = [qs[j][h] * egc[n] for n, (j, h) in enumerate(chains)]
    k_decs = [ks[j][h] * jnp.exp(gc[n][c - 1:c, :] - gc[n]) for n, (j, h) in enumerate(chains)]

    def nest(flat):
        return [flat[j * DN_HEADS:(j + 1) * DN_HEADS] for j in range(len(abs_))]

    lasts = [jnp.exp(g[c - 1:c, :]) for g in gcums]
    return nest(values), nest(k_cds), nest(attns), nest(q_decs), nest(k_decs), lasts, nest(inv)


def _dn_step(values, k_cds, attns, q_decs, k_decs, lasts, ggs, sts, ng):
    v_new = [v - bdot(kc, st) for v, kc, st in zip(values, k_cds, sts)]
    o = [bdot(qd, st) for qd, st in zip(q_decs, sts)]
    o = [a + bdot(at, vn) for a, at, vn in zip(o, attns, v_new)]
    new = [st * la + bdot_tn(kd, vn) for st, la, kd, vn in zip(sts, lasts, k_decs, v_new)]
    return [_rms(a, ng) * jax.nn.silu(g) for a, g in zip(o, ggs)], new


def _sg_chunk(u, v, gate, lng, lnb, ws, bt):
    n = SG_CHUNK
    ug = jax.nn.gelu(u)
    vn = _layer_norm(jax.nn.gelu(v), lng, lnb)
    causal = lax.broadcasted_iota(jnp.int32, (n, n), 0) >= lax.broadcasted_iota(jnp.int32, (n, n), 1)
    lane = lax.broadcasted_iota(jnp.int32, (n, D_SG), 1)
    s = jnp.zeros((n, D_SG), f32)
    for h in range(SG_HEADS):
        t = bdot(jnp.where(causal, ws[h], 0.0), vn) + bt[:, h:h + 1]
        s = s + jnp.where((lane >= h * SG_HEAD_DIM) & (lane < (h + 1) * SG_HEAD_DIM), t, 0.0)
    return ug * s * jax.nn.silu(gate)


def _cp(n_grid, vmem=None):
    return pltpu.CompilerParams(dimension_semantics=("arbitrary",) * n_grid, vmem_limit_bytes=vmem)


def _full(shape):
    nd = len(shape)
    return pl.BlockSpec(tuple(shape), lambda *_: (0,) * nd)


def _rows(tm, ncol):
    return pl.BlockSpec((tm, ncol), lambda i: (i, 0))


def _sds(shape, dtype=f32):
    return jax.ShapeDtypeStruct(tuple(shape), dtype)


def _acc(ref, val, first):
    @pl.when(first)
    def _():
        ref[...] = val

    @pl.when(jnp.logical_not(first))
    def _():
        ref[...] += val


def in_fwd(x, g, w, name):
    t, tm = x.shape[0], 512

    def body(x_ref, g_ref, w_ref, h_ref, *z_refs):
        h = _rms(x_ref[...], g_ref[...]).astype(_MXU)
        h_ref[...] = h
        for z_ref, (a, b) in zip(z_refs, Z_COLS):
            z_ref[...] = jnp.dot(h, w_ref[:, a:b], preferred_element_type=f32)

    widths = [b - a for a, b in Z_COLS]
    return pl.pallas_call(
        body, name=name, grid=(t // tm,),
        in_specs=[_rows(tm, D_MODEL), _full((1, D_MODEL)), _full((D_MODEL, D_IN_PAD))],
        out_specs=[_rows(tm, D_MODEL)] + [_rows(tm, n) for n in widths],
        out_shape=[_sds((t, D_MODEL), _MXU)] + [_sds((t, n)) for n in widths],
        compiler_params=_cp(1, VMEM_BIG),
    )(x, g, w)


def in_bwd(x, g, w, dzs, dres, name):
    t, tm = x.shape[0], 512
    widths = [b - a for a, b in Z_COLS]

    def body(x_ref, g_ref, w_ref, dres_ref, *rest):
        dz_refs, (dx_ref, dg_ref) = rest[:5], rest[5:]
        dh = jnp.zeros((tm, D_MODEL), f32)
        for dz_ref, (a, b) in zip(dz_refs, Z_COLS):
            dh = dh + _mm_nt(dz_ref[...], w_ref[:, a:b])
        _, vj = jax.vjp(_rms, x_ref[...], g_ref[...])
        dx, dg = vj(dh)
        dx_ref[...] = dres_ref[...] + dx
        _acc(dg_ref, dg, pl.program_id(0) == 0)

    return pl.pallas_call(
        body, name=name, grid=(t // tm,),
        in_specs=[_rows(tm, D_MODEL), _full((1, D_MODEL)), _full((D_MODEL, D_IN_PAD)), _rows(tm, D_MODEL)]
        + [_rows(tm, n) for n in widths],
        out_specs=[_rows(tm, D_MODEL), _full((1, D_MODEL))],
        out_shape=[_sds((t, D_MODEL)), _sds((1, D_MODEL))],
        compiler_params=_cp(1, VMEM_BIG),
    )(x, g, w, dres, *dzs)


def wgrad(a, g, name):
    t, k = a.shape
    n = g.shape[1]
    tm = min(t, 2048)
    tn = n if n <= 768 else (768 if n % 768 == 0 else 512)
    steps = t // tm

    def body(a_ref, g_ref, o_ref, acc):
        i = pl.program_id(1)
        _acc(acc, _mm_tn(a_ref[...], g_ref[...]), i == 0)

        @pl.when(i == steps - 1)
        def _():
            o_ref[...] = acc[...].astype(o_ref.dtype)

    return pl.pallas_call(
        body, name=name, grid=(n // tn, steps),
        in_specs=[pl.BlockSpec((tm, k), lambda j, i: (i, 0)), pl.BlockSpec((tm, tn), lambda j, i: (i, j))],
        out_specs=pl.BlockSpec((k, tn), lambda j, i: (0, j)),
        out_shape=_sds((k, n), _COMM),
        scratch_shapes=[pltpu.VMEM((k, tn), f32)],
        compiler_params=_cp(2, VMEM_BIG),
    )(a, g)


def post_fwd(x, ys, yd, yg, p, wout, pg, wgate, wple, name):
    t, tm = x.shape[0], 512

    def body(x_ref, ys_ref, yd_ref, yg_ref, p_ref, wout_ref, pg_ref, wgate_ref, wple_ref,
             x2_ref, x1_ref, y_ref, hn_ref):
        y = jnp.concatenate([ys_ref[...], yd_ref[...], yg_ref[...]], axis=1).astype(_MXU)
        y_ref[...] = y
        x1 = x_ref[...] + jnp.dot(y, wout_ref[...], preferred_element_type=f32)
        x1_ref[...] = x1
        hn = _rms(x1, pg_ref[...]).astype(_MXU)
        hn_ref[...] = hn
        gp = jnp.dot(hn, wgate_ref[...], preferred_element_type=f32)
        pp = _mm(p_ref[...], wple_ref[...])
        x2_ref[...] = x1 + jax.nn.sigmoid(gp) * pp

    return pl.pallas_call(
        body, name=name, grid=(t // tm,),
        in_specs=[_rows(tm, D_MODEL), _rows(tm, D_SSM), _rows(tm, D_DN), _rows(tm, D_SG), _rows(tm, D_PLE),
                  _full((D_MODEL, D_MODEL)), _full((1, D_MODEL)), _full((D_MODEL, D_MODEL)), _full((D_PLE, D_MODEL))],
        out_specs=[_rows(tm, D_MODEL)] * 4,
        out_shape=[_sds((t, D_MODEL)), _sds((t, D_MODEL)), _sds((t, D_MODEL), _MXU), _sds((t, D_MODEL), _MXU)],
        compiler_params=_cp(1, VMEM_BIG),
    )(x, ys, yd, yg, p, wout, pg, wgate, wple)


def post_bwd(dx2, x1, hn, p, wout, pg, wgate, wple, name):
    t, tm = dx2.shape[0], 512

    def body(dx2_ref, x1_ref, hn_ref, p_ref, wout_ref, pg_ref, wgate_ref, wple_ref,
             dx1_ref, dgp_ref, dpp_ref, dys_ref, dyd_ref, dyg_ref, dpg_ref):
        dx2 = dx2_ref[...]
        gp = jnp.dot(hn_ref[...], wgate_ref[...], preferred_element_type=f32)
        pp = _mm(p_ref[...], wple_ref[...])
        sg = jax.nn.sigmoid(gp)
        dpp_ref[...] = (dx2 * sg).astype(_MXU)
        dgp = (dx2 * pp * sg * (1.0 - sg)).astype(_MXU)
        dgp_ref[...] = dgp
        dhn = _mm_nt(dgp, wgate_ref[...])
        _, vj = jax.vjp(_rms, x1_ref[...], pg_ref[...])
        dx1n, dpg = vj(dhn)
        dx1 = dx2 + dx1n
        dx1_ref[...] = dx1
        dy = _mm_nt(dx1, wout_ref[...])
        dys_ref[...] = dy[:, :D_SSM]
        dyd_ref[...] = dy[:, D_SSM:D_SSM + D_DN]
        dyg_ref[...] = dy[:, D_SSM + D_DN:]
        _acc(dpg_ref, dpg, pl.program_id(0) == 0)

    return pl.pallas_call(
        body, name=name, grid=(t // tm,),
        in_specs=[_rows(tm, D_MODEL), _rows(tm, D_MODEL), _rows(tm, D_MODEL), _rows(tm, D_PLE),
                  _full((D_MODEL, D_MODEL)), _full((1, D_MODEL)), _full((D_MODEL, D_MODEL)), _full((D_PLE, D_MODEL))],
        out_specs=[_rows(tm, D_MODEL), _rows(tm, D_MODEL), _rows(tm, D_MODEL), _rows(tm, D_SSM), _rows(tm, D_DN),
                   _rows(tm, D_SG), _full((1, D_MODEL))],
        out_shape=[_sds((t, D_MODEL)), _sds((t, D_MODEL), _MXU), _sds((t, D_MODEL), _MXU), _sds((t, D_SSM)),
                   _sds((t, D_DN)), _sds((t, D_SG)), _sds((1, D_MODEL))],
        compiler_params=_cp(1, VMEM_BIG),
    )(dx2, x1, hn, p, wout, pg, wgate, wple)


def loss_fwd_bwd(x, fg, target, name):
    t, tm = x.shape[0], 512

    def body(x_ref, fg_ref, t_ref, loss_ref, dx_ref, dfg_ref):
        def f(xv, gv):
            err = _rms(xv, gv) - t_ref[...]
            return 0.5 * jnp.sum(jnp.mean(err * err, axis=-1))

        val, vj = jax.vjp(f, x_ref[...], fg_ref[...])
        dx, dfg = vj(jnp.ones((), f32))
        dx_ref[...] = dx
        first = pl.program_id(0) == 0
        _acc(dfg_ref, dfg, first)
        _acc(loss_ref, jnp.full((8, LANE), val, f32), first)

    return pl.pallas_call(
        body, name=name, grid=(t // tm,),
        in_specs=[_rows(tm, D_MODEL), _full((1, D_MODEL)), _rows(tm, D_MODEL)],
        out_specs=[_full((8, LANE)), _rows(tm, D_MODEL), _full((1, D_MODEL))],
        out_shape=[_sds((8, LANE)), _sds((t, D_MODEL)), _sds((1, D_MODEL))],
        compiler_params=_cp(1),
    )(x, fg, target)


S5_PREPARED = 4
_S5_PARAM_SHAPES = ((S5_GROUP_ROWS, N_STATE), (S5_GROUP_ROWS, N_STATE), (D_SSM, N_STATE), (D_SSM, N_STATE),
                    (N_STATE, D_SSM), (N_STATE, D_SSM), (1, D_SSM), (D_SSM, D_SSM), (1, D_SSM))

def s5_prep_fwd(are, aim, ls, bre, bim, name):
    def body(are_ref, aim_ref, ls_ref, bre_ref, bim_ref, *outs):
        vals = _s5_prep(are_ref[...], aim_ref[...], ls_ref[...], bre_ref[...], bim_ref[...])
        for o, v in zip(outs, vals):
            o[...] = v

    return pl.pallas_call(body, name=name, out_shape=[_sds(s) for s in _S5_PARAM_SHAPES[:S5_PREPARED]])(
        are, aim, ls, bre, bim)


def s5_prep_bwd(are, aim, ls, bre, bim, cts, name):
    def body(are_ref, aim_ref, ls_ref, bre_ref, bim_ref, *rest):
        ct_refs, outs = rest[:S5_PREPARED], rest[S5_PREPARED:]
        _, vj = jax.vjp(_s5_prep, are_ref[...], aim_ref[...], ls_ref[...], bre_ref[...], bim_ref[...])
        for o, v in zip(outs, vj(tuple(r[...] for r in ct_refs))):
            o[...] = v

    shapes = [(1, N_STATE)] * 3 + [(D_SSM, N_STATE)] * 2
    return pl.pallas_call(body, name=name, out_shape=[_sds(s) for s in shapes])(are, aim, ls, bre, bim, *cts)


def _step_major(ref, cols):
    x = ref[:, cols]
    n, w = x.shape
    return jnp.swapaxes(x.reshape(n // S5_GROUP_ROWS, S5_GROUP_ROWS, w), 0, 1).reshape(n, w)


def _store_step_major(ref, cols, val):
    n, w = val.shape
    ref[:, cols] = jnp.swapaxes(val.reshape(S5_GROUP_ROWS, n // S5_GROUP_ROWS, w), 0, 1).reshape(n, w).astype(ref.dtype)


def s5_fwd(z, params, nb, name):
    t = z.shape[0]
    nc = t // nb // S5_CHUNK
    npar = len(_S5_PARAM_SHAPES)

    def body(z_ref, *rest):
        p_refs, (y_ref, hs_ref, hr_s, hi_s) = rest[:npar], rest[npar:]

        @pl.when(pl.program_id(1) == 0)
        def _():
            hr_s[...] = jnp.zeros_like(hr_s)
            hi_s[...] = jnp.zeros_like(hi_s)

        hr, hi = hr_s[...], hi_s[...]
        hs_ref[0, :, :N_STATE] = hr
        hs_ref[0, :, N_STATE:] = hi
        y, nhr, nhi = _s5_chunk(_step_major(z_ref, slice(0, D_SSM)), _step_major(z_ref, slice(D_SSM, 2 * D_SSM)),
                                hr, hi, *[r[...] for r in p_refs])
        _store_step_major(y_ref, slice(0, D_SSM), y)
        hr_s[...] = nhr
        hi_s[...] = nhi

    return pl.pallas_call(
        body, name=name, grid=(nb, nc),
        in_specs=[pl.BlockSpec((S5_CHUNK, 2 * D_SSM), lambda b, c: (b * nc + c, 0))]
        + [_full(s) for s in _S5_PARAM_SHAPES],
        out_specs=[pl.BlockSpec((S5_CHUNK, D_SSM), lambda b, c: (b * nc + c, 0)),
                   pl.BlockSpec((1, 1, 2 * N_STATE), lambda b, c: (b * nc + c, 0, 0))],
        out_shape=[_sds((t, D_SSM)), _sds((nb * nc, 1, 2 * N_STATE))],
        scratch_shapes=[pltpu.VMEM((1, N_STATE), f32), pltpu.VMEM((1, N_STATE), f32)],
        compiler_params=_cp(2, VMEM_BIG),
    )(z, *params)


def s5_bwd(z, params, hs, dy, nb, name):
    t = z.shape[0]
    nc = t // nb // S5_CHUNK
    npar = len(_S5_PARAM_SHAPES)

    def body(z_ref, hs_ref, dy_ref, *rest):
        p_refs, dz_ref, dp_refs, (dhr_s, dhi_s) = rest[:npar], rest[npar], rest[npar + 1:2 * npar + 1], rest[2 * npar + 1:]

        @pl.when(pl.program_id(1) == 0)
        def _():
            dhr_s[...] = jnp.zeros_like(dhr_s)
            dhi_s[...] = jnp.zeros_like(dhi_s)

        prim = (_step_major(z_ref, slice(0, D_SSM)), _step_major(z_ref, slice(D_SSM, 2 * D_SSM)),
                hs_ref[0, :, :N_STATE], hs_ref[0, :, N_STATE:]) + tuple(r[...] for r in p_refs)
        _, vj = jax.vjp(_s5_chunk, *prim)
        cts = vj((_step_major(dy_ref, slice(0, D_SSM)), dhr_s[...], dhi_s[...]))
        _store_step_major(dz_ref, slice(0, D_SSM), cts[0])
        _store_step_major(dz_ref, slice(D_SSM, 2 * D_SSM), cts[1])
        dhr_s[...] = cts[2]
        dhi_s[...] = cts[3]
        first = (pl.program_id(0) == 0) & (pl.program_id(1) == 0)
        for r, v in zip(dp_refs, cts[4:]):
            _acc(r, v, first)

    rev = lambda b, c: (b * nc + nc - 1 - c, 0)
    return pl.pallas_call(
        body, name=name, grid=(nb, nc),
        in_specs=[pl.BlockSpec((S5_CHUNK, 2 * D_SSM), rev),
                  pl.BlockSpec((1, 1, 2 * N_STATE), lambda b, c: (b * nc + nc - 1 - c, 0, 0)),
                  pl.BlockSpec((S5_CHUNK, D_SSM), rev)] + [_full(s) for s in _S5_PARAM_SHAPES],
        out_specs=[pl.BlockSpec((S5_CHUNK, 2 * D_SSM), rev)] + [_full(s) for s in _S5_PARAM_SHAPES],
        out_shape=[_sds((t, 2 * D_SSM), _MXU)] + [_sds(s) for s in _S5_PARAM_SHAPES],
        scratch_shapes=[pltpu.VMEM((1, N_STATE), f32), pltpu.VMEM((1, N_STATE), f32)],
        compiler_params=_cp(2, VMEM_BIG),
    )(z, hs, dy, *params)


DN_PRE_ROWS = 256
DN_COLS = 3 * D_DN // LANE


def dn_pre_bwd(zq, convw, dqkv, seq, name):
    t, tb = zq.shape[0], DN_PRE_ROWS
    nrow = t // tb
    per_seq = seq // tb

    def body(xc_ref, xp_ref, w_ref, d_ref, dx_ref, dw_ref, carry):
        step = pl.program_id(0)
        i = nrow - 1 - step

        @pl.when(step == 0)
        def _():
            carry[...] = jnp.zeros_like(carry)

        for j in range(DN_COLS):
            cols = slice(j * LANE, (j + 1) * LANE)
            fn = functools.partial(_dn_pre, is_start=i % per_seq == 0, col=j)
            _, vj = jax.vjp(fn, xc_ref[:, cols], xp_ref[:, cols], w_ref[0:1, cols], w_ref[1:2, cols],
                            w_ref[2:3, cols], w_ref[3:4, cols])
            dxc, dxp, dw0, dw1, dw2, dw3 = vj(d_ref[:, cols])
            dx_ref[:tb - DN_HALO, cols] = dxc[:tb - DN_HALO].astype(_MXU)
            dx_ref[tb - DN_HALO:, cols] = (dxc[tb - DN_HALO:] + carry[:, cols]).astype(_MXU)
            carry[:, cols] = dxp
            for k, dw in enumerate((dw0, dw1, dw2, dw3)):
                @pl.when(step == 0)
                def _():
                    dw_ref[k:k + 1, cols] = dw

                @pl.when(step != 0)
                def _():
                    dw_ref[k:k + 1, cols] += dw

    rev = lambda s: (nrow - 1 - s, 0)
    return pl.pallas_call(
        body, name=name, grid=(nrow,),
        in_specs=[pl.BlockSpec((tb, 3 * D_DN), rev),
                  pl.BlockSpec((DN_HALO, 3 * D_DN),
                               lambda s: (jnp.maximum((nrow - 1 - s) * (tb // DN_HALO) - 1, 0), 0)),
                  _full((DN_CONV, 3 * D_DN)), pl.BlockSpec((tb, 3 * D_DN), rev)],
        out_specs=[pl.BlockSpec((tb, 3 * D_DN), rev), _full((DN_CONV, 3 * D_DN))],
        out_shape=[_sds((t, 3 * D_DN), _MXU), _sds((DN_CONV, 3 * D_DN))],
        scratch_shapes=[pltpu.VMEM((DN_HALO, 3 * D_DN), f32)],
        compiler_params=_cp(1, VMEM_BIG),
    )(zq, zq, convw, dqkv)


DN_LOCAL_CHUNKS = 4
DN_ATTN = DN_HEADS * DN_CHUNK


def _dn_heads(ref, rows, base=0):
    return [ref[rows, base + h * DN_HEAD_DIM:base + (h + 1) * DN_HEAD_DIM] for h in range(DN_HEADS)]


def dn_front_fwd(zq, convw, ab, alog, dtb, seq, name):
    t = zq.shape[0]
    c, n = DN_CHUNK, DN_LOCAL_CHUNKS
    per_seq = seq // (n * c)

    def body(xc_ref, xp_ref, w_ref, ab_ref, alog_ref, dtb_ref,
             qkv_ref, val_ref, kcd_ref, attn_ref, qd_ref, kd_ref, el_ref, inv_ref):
        is_start = pl.program_id(0) % per_seq == 0
        blocks = []
        for j in range(DN_COLS):
            cols = slice(j * LANE, (j + 1) * LANE)
            blocks.append(_dn_pre(xc_ref[:, cols], xp_ref[:, cols], w_ref[0:1, cols], w_ref[1:2, cols],
                                  w_ref[2:3, cols], w_ref[3:4, cols], is_start, j))
            qkv_ref[:, cols] = blocks[-1]
        rows = [pl.ds(j * c, c) for j in range(n)]

        def heads(base, j):
            return [blocks[base + h][j * c:(j + 1) * c] for h in range(DN_HEADS)]

        vals, kcds, attns, qds, kds, els, invs = _dn_local(
            [heads(0, j) for j in range(n)], [heads(DN_HEADS, j) for j in range(n)],
            [heads(2 * DN_HEADS, j) for j in range(n)], [ab_ref[r, :] for r in rows], alog_ref[...], dtb_ref[...])
        for j, r in enumerate(rows):
            for h in range(DN_HEADS):
                lo, hi = h * DN_HEAD_DIM, (h + 1) * DN_HEAD_DIM
                val_ref[r, lo:hi] = vals[j][h]
                kcd_ref[r, lo:hi] = kcds[j][h].astype(_MXU)
                qd_ref[r, lo:hi] = qds[j][h].astype(_MXU)
                kd_ref[r, lo:hi] = kds[j][h].astype(_MXU)
                attn_ref[r, h * c:(h + 1) * c] = attns[j][h].astype(_MXU)
                inv_ref[r, h * c:(h + 1) * c] = invs[j][h]
            el_ref[j] = els[j]

    wide = _rows(n * c, D_DN)
    outs = pl.pallas_call(
        body, name=name, grid=(t // (n * c),),
        in_specs=[_rows(n * c, 3 * D_DN),
                  pl.BlockSpec((DN_HALO, 3 * D_DN), lambda i: (jnp.maximum(i * (n * c // DN_HALO) - 1, 0), 0)),
                  _full((DN_CONV, 3 * D_DN)), _rows(n * c, LANE), _full((1, LANE)), _full((1, LANE))],
        out_specs=[_rows(n * c, 3 * D_DN), wide, wide, _rows(n * c, DN_ATTN), wide, wide,
                   pl.BlockSpec((n, 1, LANE), lambda i: (i, 0, 0)), _rows(n * c, DN_ATTN)],
        out_shape=[_sds((t, 3 * D_DN)), _sds((t, D_DN)), _sds((t, D_DN), _MXU), _sds((t, DN_ATTN), _MXU),
                   _sds((t, D_DN), _MXU), _sds((t, D_DN), _MXU), _sds((t // c, 1, LANE)), _sds((t, DN_ATTN))],
        compiler_params=_cp(1, VMEM_BIG),
    )(zq, zq, convw, ab, alog, dtb)
    return outs[0], outs[1:7], outs[7]


def dn_local_bwd(qkv, ab, alog, dtb, inv, cts, name):
    t = qkv.shape[0]
    c, n = DN_CHUNK, DN_LOCAL_CHUNKS

    def body(qkv_ref, ab_ref, alog_ref, dtb_ref, inv_ref, dval_ref, dkcd_ref, dattn_ref, dqd_ref, dkd_ref, del_ref,
             dqkv_ref, dab_ref, dalog_ref, ddtb_ref):
        rows = [pl.ds(j * c, c) for j in range(n)]
        invs = [[inv_ref[r, h * c:(h + 1) * c] for h in range(DN_HEADS)] for r in rows]

        def local(qs, ks, vs, abs_, alog, dtb):
            return _dn_local(qs, ks, vs, abs_, alog, dtb, invs)[:6]

        _, vj = jax.vjp(local, [_dn_heads(qkv_ref, r) for r in rows], [_dn_heads(qkv_ref, r, D_DN) for r in rows],
                        [_dn_heads(qkv_ref, r, 2 * D_DN) for r in rows], [ab_ref[r, :] for r in rows], alog_ref[...],
                        dtb_ref[...])
        dattn = [[dattn_ref[r, h * c:(h + 1) * c] for h in range(DN_HEADS)] for r in rows]
        dq, dk, dv, dab, dalog, ddtb = vj(([_dn_heads(dval_ref, r) for r in rows], [_dn_heads(dkcd_ref, r) for r in rows],
                                           dattn, [_dn_heads(dqd_ref, r) for r in rows],
                                           [_dn_heads(dkd_ref, r) for r in rows], [del_ref[j] for j in range(n)]))
        for j, r in enumerate(rows):
            for h in range(DN_HEADS):
                lo, hi = h * DN_HEAD_DIM, (h + 1) * DN_HEAD_DIM
                dqkv_ref[r, lo:hi] = dq[j][h]
                dqkv_ref[r, D_DN + lo:D_DN + hi] = dk[j][h]
                dqkv_ref[r, 2 * D_DN + lo:2 * D_DN + hi] = dv[j][h]
            dab_ref[r, :] = dab[j].astype(_MXU)
        first = pl.program_id(0) == 0
        _acc(dalog_ref, dalog, first)
        _acc(ddtb_ref, ddtb, first)

    wide = _rows(n * c, D_DN)
    return pl.pallas_call(
        body, name=name, grid=(t // (n * c),),
        in_specs=[_rows(n * c, 3 * D_DN), _rows(n * c, LANE), _full((1, LANE)), _full((1, LANE)),
                  _rows(n * c, DN_ATTN), wide, wide, _rows(n * c, DN_ATTN), wide, wide,
                  pl.BlockSpec((n, 1, LANE), lambda i: (i, 0, 0))],
        out_specs=[_rows(n * c, 3 * D_DN), _rows(n * c, LANE), _full((1, LANE)), _full((1, LANE))],
        out_shape=[_sds((t, 3 * D_DN)), _sds((t, LANE), _MXU), _sds((1, LANE)), _sds((1, LANE))],
        compiler_params=_cp(1),
    )(qkv, ab, alog, dtb, inv, *cts)


def _seq_view(a, nb):
    return a.reshape((nb, a.shape[0] // nb) + a.shape[1:])


def _dn_chains(nb):
    return [(b, h) for b in range(nb) for h in range(DN_HEADS)]


DN_SCAN_CHUNKS = 4


def _dn_step_operands(val_ref, kcd_ref, attn_ref, qd_ref, kd_ref, el_ref, gg_ref, nb, j):
    chains = _dn_chains(nb)
    c = DN_CHUNK
    rows = pl.ds(j * c, c)

    def wide(ref):
        return [ref[b, rows, h * DN_HEAD_DIM:(h + 1) * DN_HEAD_DIM].astype(f32) for b, h in chains]

    attns = [attn_ref[b, rows, h * c:(h + 1) * c].astype(f32) for b, h in chains]
    return (wide(val_ref), wide(kcd_ref), attns, wide(qd_ref), wide(kd_ref),
            [el_ref[b, j, :, h:h + 1] for b, h in chains], wide(gg_ref))


def dn_scan_fwd(loc, gg, ng, nb, name):
    val, kcd, attn, qd, kd, el = loc
    t = val.shape[0]
    c, n = DN_CHUNK, DN_SCAN_CHUNKS
    nc = t // nb // c
    ns = nb * DN_HEADS

    def body(val_ref, kcd_ref, attn_ref, qd_ref, kd_ref, el_ref, gg_ref, ng_ref, y_ref, ss_ref, st):
        @pl.when(pl.program_id(0) == 0)
        def _():
            st[...] = jnp.zeros_like(st)

        sts = [st[i] for i in range(ns)]
        for j in range(n):
            for i in range(ns):
                ss_ref[j, i] = sts[i]
            ys, sts = _dn_step(*_dn_step_operands(val_ref, kcd_ref, attn_ref, qd_ref, kd_ref, el_ref, gg_ref, nb, j),
                               sts, ng_ref[...])
            for i, (b, h) in enumerate(_dn_chains(nb)):
                y_ref[b, pl.ds(j * c, c), h * DN_HEAD_DIM:(h + 1) * DN_HEAD_DIM] = ys[i]
        for i in range(ns):
            st[i] = sts[i]

    def blk(w):
        return pl.BlockSpec((nb, n * c, w), lambda k: (0, k, 0))

    el_spec = pl.BlockSpec((nb, n, 1, LANE), lambda k: (0, k, 0, 0))
    y, ss = pl.pallas_call(
        body, name=name, grid=(nc // n,),
        in_specs=[blk(D_DN), blk(D_DN), blk(DN_ATTN), blk(D_DN), blk(D_DN), el_spec, blk(D_DN), _full((1, LANE))],
        out_specs=[blk(D_DN), pl.BlockSpec((n, ns, DN_HEAD_DIM, DN_HEAD_DIM), lambda k: (k, 0, 0, 0))],
        out_shape=[_sds((nb, t // nb, D_DN)), _sds((nc, ns, DN_HEAD_DIM, DN_HEAD_DIM))],
        scratch_shapes=[pltpu.VMEM((ns, DN_HEAD_DIM, DN_HEAD_DIM), f32)],
        compiler_params=_cp(1, VMEM_BIG),
    )(_seq_view(val, nb), _seq_view(kcd, nb), _seq_view(attn, nb), _seq_view(qd, nb), _seq_view(kd, nb),
      el.reshape(nb, nc, 1, LANE), _seq_view(gg, nb), ng)
    return y.reshape(t, D_DN), ss


def dn_scan_bwd(loc, gg, ng, ss, dy, nb, name):
    val, kcd, attn, qd, kd, el = loc
    t = val.shape[0]
    c, n = DN_CHUNK, DN_SCAN_CHUNKS
    nc = t // nb // c
    ns = nb * DN_HEADS
    steps = nc // n

    def body(val_ref, kcd_ref, attn_ref, qd_ref, kd_ref, el_ref, gg_ref, ng_ref, ss_ref, dy_ref,
             dval_ref, dkcd_ref, dattn_ref, dqd_ref, dkd_ref, del_ref, dgg_ref, dng_ref, dst):
        @pl.when(pl.program_id(0) == 0)
        def _():
            dst[...] = jnp.zeros_like(dst)

        lane = lax.broadcasted_iota(jnp.int32, (1, LANE), 1)
        chains = _dn_chains(nb)
        ds = [dst[i] for i in range(ns)]
        dng_tot = jnp.zeros((1, LANE), f32)
        for j in reversed(range(n)):
            rows = pl.ds(j * c, c)
            _, vj = jax.vjp(_dn_step,
                            *_dn_step_operands(val_ref, kcd_ref, attn_ref, qd_ref, kd_ref, el_ref, gg_ref, nb, j),
                            [ss_ref[j, i] for i in range(ns)], ng_ref[...])
            dys = [dy_ref[b, rows, h * DN_HEAD_DIM:(h + 1) * DN_HEAD_DIM] for b, h in chains]
            dval, dkcd, dattn, dqd, dkd, dlast, dgg, ds, dng = vj((dys, ds))
            dng_tot = dng_tot + dng
            del_rows = [jnp.zeros((1, LANE), f32) for _ in range(nb)]
            for i, (b, h) in enumerate(chains):
                cols = slice(h * DN_HEAD_DIM, (h + 1) * DN_HEAD_DIM)
                dval_ref[b, rows, cols] = dval[i]
                dkcd_ref[b, rows, cols] = dkcd[i]
                dattn_ref[b, rows, h * c:(h + 1) * c] = dattn[i]
                dqd_ref[b, rows, cols] = dqd[i]
                dkd_ref[b, rows, cols] = dkd[i]
                dgg_ref[b, rows, cols] = dgg[i].astype(_MXU)
                del_rows[b] = del_rows[b] + jnp.where(lane == h, dlast[i], 0.0)
            for b in range(nb):
                del_ref[b, j] = del_rows[b]
        for i in range(ns):
            dst[i] = ds[i]
        _acc(dng_ref, dng_tot, pl.program_id(0) == 0)

    def blk(w):
        return pl.BlockSpec((nb, n * c, w), lambda k: (0, steps - 1 - k, 0))

    el_spec = pl.BlockSpec((nb, n, 1, LANE), lambda k: (0, steps - 1 - k, 0, 0))
    outs = pl.pallas_call(
        body, name=name, grid=(steps,),
        in_specs=[blk(D_DN), blk(D_DN), blk(DN_ATTN), blk(D_DN), blk(D_DN), el_spec, blk(D_DN), _full((1, LANE)),
                  pl.BlockSpec((n, ns, DN_HEAD_DIM, DN_HEAD_DIM), lambda k: (steps - 1 - k, 0, 0, 0)), blk(D_DN)],
        out_specs=[blk(D_DN), blk(D_DN), blk(DN_ATTN), blk(D_DN), blk(D_DN), el_spec, blk(D_DN), _full((1, LANE))],
        out_shape=[_sds((nb, t // nb, D_DN)), _sds((nb, t // nb, D_DN)), _sds((nb, t // nb, DN_ATTN)),
                   _sds((nb, t // nb, D_DN)), _sds((nb, t // nb, D_DN)), _sds((nb, nc, 1, LANE)),
                   _sds((nb, t // nb, D_DN), _MXU), _sds((1, LANE))],
        scratch_shapes=[pltpu.VMEM((ns, DN_HEAD_DIM, DN_HEAD_DIM), f32)],
        compiler_params=_cp(1, VMEM_BIG),
    )(_seq_view(val, nb), _seq_view(kcd, nb), _seq_view(attn, nb), _seq_view(qd, nb), _seq_view(kd, nb),
      el.reshape(nb, nc, 1, LANE), _seq_view(gg, nb), ng, ss, _seq_view(dy, nb))
    dloc = [o.reshape((t,) + o.shape[2:]) for o in outs[:5]] + [outs[5].reshape(t // c, 1, LANE)]
    return dloc, outs[6].reshape(t, D_DN), outs[7]


SG_ROWS = 512


def sg_fwd(z, lng, lnb, w, bt, name):
    t = z.shape[0]

    def body(z_ref, lng_ref, lnb_ref, w_ref, bt_ref, y_ref):
        ws = [w_ref[h] for h in range(SG_HEADS)]
        for k in range(SG_ROWS // SG_CHUNK):
            r = pl.ds(k * SG_CHUNK, SG_CHUNK)
            y_ref[r, :] = _sg_chunk(z_ref[r, :D_SG], z_ref[r, D_SG:2 * D_SG], z_ref[r, 2 * D_SG:], lng_ref[...],
                                    lnb_ref[...], ws, bt_ref[...])

    return pl.pallas_call(
        body, name=name, grid=(t // SG_ROWS,),
        in_specs=[_rows(SG_ROWS, 3 * D_SG), _full((1, D_SG)), _full((1, D_SG)),
                  _full((SG_HEADS, SG_CHUNK, SG_CHUNK)), _full((SG_CHUNK, LANE))],
        out_specs=_rows(SG_ROWS, D_SG),
        out_shape=_sds((t, D_SG)),
        compiler_params=_cp(1),
    )(z, lng, lnb, w, bt)


def sg_bwd(z, lng, lnb, w, bt, dy, name):
    t = z.shape[0]

    def body(z_ref, lng_ref, lnb_ref, w_ref, bt_ref, dy_ref, dz_ref, dlng_ref, dlnb_ref, dw_ref, dbt_ref):
        ws = [w_ref[h] for h in range(SG_HEADS)]
        tot = None
        for k in range(SG_ROWS // SG_CHUNK):
            r = pl.ds(k * SG_CHUNK, SG_CHUNK)
            _, vj = jax.vjp(_sg_chunk, z_ref[r, :D_SG], z_ref[r, D_SG:2 * D_SG], z_ref[r, 2 * D_SG:], lng_ref[...],
                            lnb_ref[...], ws, bt_ref[...])
            du, dv, dgate, dlng, dlnb, dws, dbt = vj(dy_ref[r, :])
            dz_ref[r, :D_SG] = du.astype(_MXU)
            dz_ref[r, D_SG:2 * D_SG] = dv.astype(_MXU)
            dz_ref[r, 2 * D_SG:] = dgate.astype(_MXU)
            part = [dlng, dlnb, dbt] + list(dws)
            tot = part if tot is None else [a + b for a, b in zip(tot, part)]
        first = pl.program_id(0) == 0
        _acc(dlng_ref, tot[0], first)
        _acc(dlnb_ref, tot[1], first)
        _acc(dbt_ref, tot[2], first)
        for h in range(SG_HEADS):
            @pl.when(first)
            def _():
                dw_ref[h] = tot[3 + h]

            @pl.when(jnp.logical_not(first))
            def _():
                dw_ref[h] += tot[3 + h]

    return pl.pallas_call(
        body, name=name, grid=(t // SG_ROWS,),
        in_specs=[_rows(SG_ROWS, 3 * D_SG), _full((1, D_SG)), _full((1, D_SG)),
                  _full((SG_HEADS, SG_CHUNK, SG_CHUNK)), _full((SG_CHUNK, LANE)), _rows(SG_ROWS, D_SG)],
        out_specs=[_rows(SG_ROWS, 3 * D_SG), _full((1, D_SG)), _full((1, D_SG)),
                   _full((SG_HEADS, SG_CHUNK, SG_CHUNK)), _full((SG_CHUNK, LANE))],
        out_shape=[_sds((t, 3 * D_SG), _MXU), _sds((1, D_SG)), _sds((1, D_SG)), _sds((SG_HEADS, SG_CHUNK, SG_CHUNK)),
                   _sds((SG_CHUNK, LANE))],
        compiler_params=_cp(1),
    )(z, lng, lnb, w, bt, dy)


def add_pairs(a_list, b_list, name):
    n = len(a_list)

    def body(*refs):
        for a_ref, b_ref, o_ref in zip(refs[:n], refs[n:2 * n], refs[2 * n:]):
            o_ref[...] = (a_ref[...].astype(f32) + b_ref[...].astype(f32)).astype(o_ref.dtype)

    return pl.pallas_call(
        body, name=name, out_shape=[_sds(a.shape, a.dtype) for a in a_list],
        compiler_params=pltpu.CompilerParams(vmem_limit_bytes=VMEM_BIG),
    )(*a_list, *b_list)


def _adamw(g, w, m, v):
    nm = ADAM_B1 * m + (1.0 - ADAM_B1) * g
    nv = ADAM_B2 * v + (1.0 - ADAM_B2) * jnp.square(g)
    m_hat = nm / (1.0 - ADAM_B1 ** ADAM_STEP)
    v_hat = nv / (1.0 - ADAM_B2 ** ADAM_STEP)
    return -ADAM_LR * (m_hat / (jnp.sqrt(v_hat) + ADAM_EPS) + ADAM_WD * w), nm, nv


def sum_parts(half, recv, name):
    _, r, c = recv.shape
    tr = 256 if r % 256 == 0 else r

    def body(half_ref, recv_ref, g_ref):
        g = recv_ref[0].astype(f32)
        for k in range(1, N_CHIPS):
            g = g + recv_ref[k].astype(f32)
        g_ref[...] = g

    return pl.pallas_call(
        body, name=name,
        grid_spec=pltpu.PrefetchScalarGridSpec(
            num_scalar_prefetch=1, grid=(r // tr,),
            in_specs=[pl.BlockSpec((N_CHIPS, tr, c), lambda i, h: (0, i, 0))],
            out_specs=pl.BlockSpec((None, tr, c), lambda i, h: (h[0], i, 0))),
        out_shape=_sds((2, r, c)),
        compiler_params=_cp(1, VMEM_BIG),
    )(half, recv)


def adamw(g, w, m, v, name):
    _, r, c = w.shape
    tr = 256 if r % 256 == 0 else r

    def body(g_ref, w_ref, m_ref, v_ref, d_ref, nm_ref, nv_ref):
        d_ref[...], nm_ref[...], nv_ref[...] = _adamw(g_ref[...], w_ref[...], m_ref[...], v_ref[...])

    blk = pl.BlockSpec((None, tr, c), lambda l, i: (l, i, 0))
    return pl.pallas_call(
        body, name=name, grid=(2, r // tr), in_specs=[blk] * 4, out_specs=[blk] * 3, out_shape=[_sds((2, r, c))] * 3,
        compiler_params=_cp(2, VMEM_BIG),
    )(g, w, m, v)


def sum_parts_small(chip, parts, sums, name):
    n = len(sums)

    def body(chip_ref, *refs):
        for part, own, out in zip(refs[:n], refs[n:2 * n], refs[2 * n:]):
            g = jnp.where(chip_ref[0] == 0, own[...], part[0])
            for q in range(1, N_CHIPS):
                g = g + jnp.where(chip_ref[0] == q, own[...], part[q])
            out[...] = g

    vmem = pl.BlockSpec(memory_space=pltpu.VMEM)
    return pl.pallas_call(
        body, name=name, in_specs=[pl.BlockSpec(memory_space=pltpu.SMEM)] + [vmem] * (2 * n), out_specs=[vmem] * n,
        out_shape=[_sds(s.shape) for s in sums], compiler_params=pltpu.CompilerParams(vmem_limit_bytes=VMEM_BIG),
    )(chip, *parts, *sums)


def adamw_small(gs, ws, ms, vs, name):
    n = len(ws)

    def body(*refs):
        ins, outs = refs[:4 * n], refs[4 * n:]
        for k in range(n):
            outs[k][...], outs[n + k][...], outs[2 * n + k][...] = _adamw(
                ins[k][...], ins[n + k][...], ins[2 * n + k][...], ins[3 * n + k][...])

    outs = pl.pallas_call(
        body, name=name, out_shape=[_sds(w.shape) for w in ws] * 3,
        compiler_params=pltpu.CompilerParams(vmem_limit_bytes=VMEM_BIG),
    )(*gs, *ws, *ms, *vs)
    return [outs[j * n:(j + 1) * n] for j in range(3)]


_ANY = pl.BlockSpec(memory_space=pltpu.HBM)
_MESH = pl.DeviceIdType.MESH


def _flip(v, bit):
    return 1 - v if bit else v


_CHIP_RELS = ((1, 0), (0, 1), (1, 1))


def _piece(ref, kind, q):
    if kind[0] == "slot":
        return ref.at[q]
    if kind[0] == "all":
        return ref
    _, axis, n = kind
    return ref.at[(slice(None),) * axis + (pl.ds(q * n, n),)]


def _piece_shape(shape, kind):
    if kind[0] == "slot":
        return tuple(shape[1:])
    if kind[0] == "all":
        return tuple(shape)
    _, axis, n = kind
    return tuple(shape[:axis]) + (n,) + tuple(shape[axis + 1:])


def gather_weights(shards, kinds, name):
    n = len(shards)

    def out_shape(s, kind):
        if kind[0] == "slot":
            return (N_CHIPS,) + tuple(s.shape)
        _, axis, w = kind
        return tuple(s.shape[:axis + 1]) + (N_CHIPS * w,) + tuple(s.shape[axis + 2:])

    def place(o_ref, kind, q, layer):
        if kind[0] == "slot":
            return o_ref.at[q, layer]
        return _piece(o_ref.at[layer], kind, q)

    def body(*refs):
        s_refs, o_refs = refs[:n], refs[n:2 * n]
        send_sems, recv_sems, fwd_send_sems, fwd_recv_sems = refs[2 * n:]
        x, y, c = lax.axis_index("x"), lax.axis_index("y"), lax.axis_index("c")
        mine = 2 * x + y
        sends, arrivals, forwards, fwd_arrivals = [], [], [], []
        for r, (fx, fy) in enumerate(_CHIP_RELS):
            px, py = _flip(x, fx), _flip(y, fy)
            peer = 2 * px + py
            for k in range(n):
                s = r * n + k
                sends.append(pltpu.make_async_remote_copy(
                    src_ref=s_refs[k].at[c], dst_ref=place(o_refs[k], kinds[k], mine, c), send_sem=send_sems.at[s],
                    recv_sem=recv_sems.at[s], device_id=(px, py, c), device_id_type=_MESH))
                arrivals.append(pltpu.make_async_remote_copy(
                    src_ref=s_refs[k].at[c], dst_ref=place(o_refs[k], kinds[k], peer, c), send_sem=send_sems.at[s],
                    recv_sem=recv_sems.at[s], device_id=(px, py, c), device_id_type=_MESH))
                block = place(o_refs[k], kinds[k], peer, c)
                forwards.append(pltpu.make_async_remote_copy(
                    src_ref=block, dst_ref=block, send_sem=fwd_send_sems.at[s], recv_sem=fwd_recv_sems.at[s],
                    device_id=(x, y, 1 - c), device_id_type=_MESH))
                other = place(o_refs[k], kinds[k], peer, 1 - c)
                fwd_arrivals.append(pltpu.make_async_remote_copy(
                    src_ref=other, dst_ref=other, send_sem=fwd_send_sems.at[s], recv_sem=fwd_recv_sems.at[s],
                    device_id=(x, y, 1 - c), device_id_type=_MESH))
        for cp in sends:
            cp.start()
        for arrived, fwd in zip(arrivals, forwards):
            arrived.wait_recv()
            fwd.start()
        for cp in fwd_arrivals:
            cp.wait_recv()
        for cp in sends + forwards:
            cp.wait_send()

    m = len(_CHIP_RELS) * n
    return pl.pallas_call(
        body, name=name, in_specs=[_ANY] * n, out_specs=[_ANY] * n,
        out_shape=[_sds(out_shape(s, k), s.dtype) for s, k in zip(shards, kinds)],
        scratch_shapes=[pltpu.SemaphoreType.DMA((m,))] * 4,
    )(*shards)


def _owned_by(owners, side):
    return [k for k, o in enumerate(owners) if o == side]


def exchange_halves(gs, smalls, owners, name):
    n, ns = len(gs), len(smalls)

    def body(*refs):
        g_refs, s_refs = refs[:n], refs[n:n + ns]
        got_refs, sgot_refs = refs[n + ns:2 * n + ns], refs[2 * n + ns:2 * (n + ns)]
        send_sems, recv_sems = refs[2 * (n + ns):]
        x, y, c = lax.axis_index("x"), lax.axis_index("y"), lax.axis_index("c")
        sibling = (x, y, 1 - c)
        swaps = [pltpu.make_async_remote_copy(
            src_ref=g_refs[k].at[1 - c], dst_ref=got_refs[k], send_sem=send_sems.at[k], recv_sem=recv_sems.at[k],
            device_id=sibling, device_id_type=_MESH) for k in range(n)]
        gives = [pltpu.make_async_remote_copy(
            src_ref=s_refs[k], dst_ref=sgot_refs[k], send_sem=send_sems.at[n + k], recv_sem=recv_sems.at[n + k],
            device_id=sibling, device_id_type=_MESH) for k in range(ns)]
        for cp in swaps:
            cp.start()
        for side in (0, 1):
            @pl.when(c == 1 - side)
            def _():
                for k in _owned_by(owners, side):
                    gives[k].start()
        for cp in swaps:
            cp.wait()
        for side in (0, 1):
            @pl.when(c == 1 - side)
            def _():
                for k in _owned_by(owners, side):
                    gives[k].wait_send()

            @pl.when(c == side)
            def _():
                for k in _owned_by(owners, side):
                    gives[k].wait_recv()

    outs = pl.pallas_call(
        body, name=name, in_specs=[_ANY] * (n + ns), out_specs=[_ANY] * (n + ns),
        out_shape=[_sds(g.shape[1:], g.dtype) for g in gs] + [_sds(s.shape, s.dtype) for s in smalls],
        scratch_shapes=[pltpu.SemaphoreType.DMA((n + ns,)), pltpu.SemaphoreType.DMA((n + ns,))],
    )(*gs, *smalls)
    return outs[:n], outs[n:]


def reduce_to_chips(ts, kinds, smalls, owners, name):
    n, ns = len(ts), len(smalls)

    def body(*refs):
        t_refs, s_refs = refs[:n], refs[n:n + ns]
        o_refs, so_refs = refs[n + ns:2 * n + ns], refs[2 * n + ns:2 * (n + ns)]
        send_sems, recv_sems = refs[2 * (n + ns):]
        x, y, c = lax.axis_index("x"), lax.axis_index("y"), lax.axis_index("c")
        mine = 2 * x + y
        sends, arrivals, small_sends, small_arrivals = [], [], [], []
        for r, (fx, fy) in enumerate(_CHIP_RELS):
            px, py = _flip(x, fx), _flip(y, fy)
            peer = 2 * px + py
            for k in range(n + ns):
                s = r * (n + ns) + k
                if k < n:
                    src, dst = _piece(t_refs[k], kinds[k], peer), o_refs[k]
                else:
                    src, dst = s_refs[k - n], so_refs[k - n]
                go = pltpu.make_async_remote_copy(
                    src_ref=src, dst_ref=dst.at[mine], send_sem=send_sems.at[s], recv_sem=recv_sems.at[s],
                    device_id=(px, py, c), device_id_type=_MESH)
                come = pltpu.make_async_remote_copy(
                    src_ref=src, dst_ref=dst.at[peer], send_sem=send_sems.at[s], recv_sem=recv_sems.at[s],
                    device_id=(px, py, c), device_id_type=_MESH)
                (sends if k < n else small_sends).append(go)
                (arrivals if k < n else small_arrivals).append(come)

        def owned(copies, side):
            return [cp for j, cp in enumerate(copies) if owners[j % ns] == side]

        for cp in sends:
            cp.start()
        for side in (0, 1):
            @pl.when(c == side)
            def _():
                for cp in owned(small_sends, side):
                    cp.start()
        for cp in arrivals:
            cp.wait_recv()
        for cp in sends:
            cp.wait_send()
        for side in (0, 1):
            @pl.when(c == side)
            def _():
                for cp in owned(small_arrivals, side):
                    cp.wait_recv()
                for cp in owned(small_sends, side):
                    cp.wait_send()

    m = len(_CHIP_RELS) * (n + ns)
    outs = pl.pallas_call(
        body, name=name, in_specs=[_ANY] * (n + ns), out_specs=[_ANY] * (n + ns),
        out_shape=[_sds((N_CHIPS,) + _piece_shape(t.shape, k), t.dtype) for t, k in zip(ts, kinds)]
        + [_sds((N_CHIPS,) + s.shape, s.dtype) for s in smalls],
        scratch_shapes=[pltpu.SemaphoreType.DMA((m,)), pltpu.SemaphoreType.DMA((m,))],
    )(*ts, *smalls)
    return outs[:n], outs[n:]


def share_halves(rs, smalls, owners, name):
    n, ns = len(rs), len(smalls)

    def body(*refs):
        o_refs, so_refs = refs[n + ns:2 * n + ns], refs[2 * n + ns:2 * (n + ns)]
        send_sems, recv_sems = refs[2 * (n + ns):]
        x, y, c = lax.axis_index("x"), lax.axis_index("y"), lax.axis_index("c")
        sibling = (x, y, 1 - c)
        swaps = [pltpu.make_async_remote_copy(
            src_ref=o_refs[k].at[c], dst_ref=o_refs[k].at[c], send_sem=send_sems.at[k], recv_sem=recv_sems.at[k],
            device_id=sibling, device_id_type=_MESH) for k in range(n)]
        arrivals = [pltpu.make_async_remote_copy(
            src_ref=o_refs[k].at[c], dst_ref=o_refs[k].at[1 - c], send_sem=send_sems.at[k], recv_sem=recv_sems.at[k],
            device_id=sibling, device_id_type=_MESH) for k in range(n)]
        gives = [pltpu.make_async_remote_copy(
            src_ref=so_refs[k], dst_ref=so_refs[k], send_sem=send_sems.at[n + k], recv_sem=recv_sems.at[n + k],
            device_id=sibling, device_id_type=_MESH) for k in range(ns)]
        for cp in swaps:
            cp.start()
        for side in (0, 1):
            @pl.when(c == side)
            def _():
                for k in _owned_by(owners, side):
                    gives[k].start()
        for cp in arrivals:
            cp.wait_recv()
        for cp in swaps:
            cp.wait_send()
        for side in (0, 1):
            @pl.when(c == side)
            def _():
                for k in _owned_by(owners, side):
                    gives[k].wait_send()

            @pl.when(c == 1 - side)
            def _():
                for k in _owned_by(owners, side):
                    gives[k].wait_recv()

    outs = pl.pallas_call(
        body, name=name, in_specs=[_ANY] * (n + ns), out_specs=[_ANY] * (n + ns),
        out_shape=[_sds(r.shape, r.dtype) for r in list(rs) + list(smalls)],
        input_output_aliases={k: k for k in range(n + ns)},
        scratch_shapes=[pltpu.SemaphoreType.DMA((n + ns,)), pltpu.SemaphoreType.DMA((n + ns,))],
    )(*rs, *smalls)
    return outs[:n], outs[n:]


def _small_view(a):
    if a.size < 8 * LANE:
        return jnp.pad(a.reshape(-1), (0, 8 * LANE - a.size)).reshape(8, LANE)
    if a.ndim == 1:
        return a.reshape(1, a.shape[0])
    if a.ndim == 4 and a.shape[-1] < LANE:
        return a.reshape(a.shape[0], a.shape[1], a.shape[2] * a.shape[3])
    return a


def _permuted_from_shards(shards):
    parts = []
    for lo, hi in GROUP_COLS:
        for q in range(N_CHIPS):
            a, b = max(lo, q * SHARD_COLS), min(hi, (q + 1) * SHARD_COLS)
            if a < b:
                parts.append(shards[q][..., a - q * SHARD_COLS:b - q * SHARD_COLS])
    pad = jnp.zeros(shards[0].shape[:-1] + (D_IN_PAD - D_IN,), shards[0].dtype)
    return jnp.concatenate(parts + [pad], axis=-1)


def _shards_from_groups(groups):
    in_order = sorted(range(len(GROUP_COLS)), key=lambda j: GROUP_COLS[j][0])
    shards = []
    for q in range(N_CHIPS):
        parts = []
        for j in in_order:
            lo, hi = GROUP_COLS[j]
            a, b = max(lo, q * SHARD_COLS), min(hi, (q + 1) * SHARD_COLS)
            if a < b:
                parts.append(groups[j][..., a - lo:b - lo])
        shards.append(jnp.concatenate(parts, axis=-1))
    return shards


def _expand_b(b):
    eye = jnp.eye(SSM_GROUPS, dtype=b.dtype)
    return jnp.einsum("gnc,gh->gchn", b, eye).reshape(D_SSM, N_STATE)


def _extract_b(e):
    return jnp.einsum("gcgn->gnc", e.reshape(SSM_GROUPS, SSM_GROUP, SSM_GROUPS, SSM_STATE))


def _expand_c(c):
    eye = jnp.eye(SSM_GROUPS, dtype=c.dtype)
    return jnp.einsum("gcn,gh->gnhc", c, eye).reshape(N_STATE, D_SSM)


def _extract_c(e):
    return jnp.einsum("gngc->gcn", e.reshape(SSM_GROUPS, SSM_STATE, SSM_GROUPS, SSM_GROUP))


def _lane_row(v):
    return jnp.pad(v, (0, LANE - v.shape[0])).reshape(1, LANE)


def _layer_params(w, l):
    return dict(
        norm_g=w["norm_g"][l][None], win=w["w_in_perm"][l], wout=w["w_out"][l].astype(_MXU),
        pg=w["ple_norm_g"][l][None], wgate=w["w_ple_gate"][l].astype(_MXU), wple=w["w_ple"][l].astype(_MXU),
        are=w["ssm_a_re"][l].reshape(1, N_STATE), aim=w["ssm_a_im"][l].reshape(1, N_STATE),
        ls=jnp.repeat(w["ssm_log_step"][l], SSM_STATE).reshape(1, N_STATE),
        bre=_expand_b(w["ssm_b_re"][l]), bim=_expand_b(w["ssm_b_im"][l]),
        cr=_expand_c(w["ssm_c_re"][l]), ci=_expand_c(w["ssm_c_im"][l]),
        dr=w["ssm_d"][l].reshape(1, D_SSM), wglu=w["ssm_w_glu"][l].astype(f32), bglu=w["ssm_b_glu"][l][None],
        convw=w["dn_conv_w"][l], alog=_lane_row(w["dn_a_log"][l]), dtb=_lane_row(w["dn_dt_bias"][l]),
        ng=w["dn_norm_g"][l][None],
        lng=w["sg_ln_g"][l][None], lnb=w["sg_ln_b"][l][None], sgw=w["sg_w"][l],
        bt=jnp.pad(w["sg_b"][l].T, ((0, 0), (0, LANE - SG_HEADS))),
    )


def _layer_fwd(x, p, lp, nb, tag):
    seq = x.shape[0] // nb
    h, zs, zq, zg, zsg, zab = in_fwd(x, lp["norm_g"], lp["win"], f"in_fwd{tag}")
    prep = s5_prep_fwd(lp["are"], lp["aim"], lp["ls"], lp["bre"], lp["bim"], f"s5_prep_fwd{tag}")
    s5p = tuple(prep) + (lp["cr"], lp["ci"], lp["dr"], lp["wglu"], lp["bglu"])
    ys, hs = s5_fwd(zs, s5p, nb, f"s5_fwd{tag}")
    qkv, loc, inv = dn_front_fwd(zq, lp["convw"], zab, lp["alog"], lp["dtb"], seq, f"dn_front_fwd{tag}")
    yd, ss = dn_scan_fwd(loc, zg, lp["ng"], nb, f"dn_scan_fwd{tag}")
    yg = sg_fwd(zsg, lp["lng"], lp["lnb"], lp["sgw"], lp["bt"], f"sg_fwd{tag}")
    x2, x1, y, hn = post_fwd(x, ys, yd, yg, p, lp["wout"], lp["pg"], lp["wgate"], lp["wple"], f"post_fwd{tag}")
    saved = dict(x=x, h=h, zs=zs, zq=zq, zg=zg, zsg=zsg, zab=zab, s5p=s5p, hs=hs, qkv=qkv, loc=loc, inv=inv, ss=ss, x1=x1, y=y, hn=hn, p=p)
    return x2, saved


def _layer_bwd(dx2, sv, lp, nb, tag):
    seq = dx2.shape[0] // nb
    dx1, dgp, dpp, dys, dyd, dyg, dpg = post_bwd(dx2, sv["x1"], sv["hn"], sv["p"], lp["wout"], lp["pg"], lp["wgate"],
                                                 lp["wple"], f"post_bwd{tag}")
    g = {}
    g["w_out"] = wgrad(sv["y"], dx1, f"wgrad_out{tag}")
    g["w_ple_gate"] = wgrad(sv["hn"], dgp, f"wgrad_gate{tag}")
    g["w_ple"] = wgrad(sv["p"], dpp, f"wgrad_ple{tag}")
    g["ple_norm_g"] = dpg[0]
    dzsg, dlng, dlnb, dsgw, dbt = sg_bwd(sv["zsg"], lp["lng"], lp["lnb"], lp["sgw"], lp["bt"], dyg, f"sg_bwd{tag}")
    g["sg_ln_g"], g["sg_ln_b"], g["sg_w"], g["sg_b"] = dlng[0], dlnb[0], dsgw, dbt[:, :SG_HEADS].T
    dloc, dzg, dng = dn_scan_bwd(sv["loc"], sv["zg"], lp["ng"], sv["ss"], dyd, nb, f"dn_scan_bwd{tag}")
    dqkv, dzab, dalog, ddtb = dn_local_bwd(sv["qkv"], sv["zab"], lp["alog"], lp["dtb"], sv["inv"], dloc,
                                           f"dn_local_bwd{tag}")
    dzq, dconv = dn_pre_bwd(sv["zq"], lp["convw"], dqkv, seq, f"dn_pre_bwd{tag}")
    g["dn_conv_w"], g["dn_a_log"], g["dn_dt_bias"], g["dn_norm_g"] = dconv, dalog[0, :DN_HEADS], ddtb[0, :DN_HEADS], dng[0]
    s5out = s5_bwd(sv["zs"], sv["s5p"], sv["hs"], dys, nb, f"s5_bwd{tag}")
    dzs, dprep, (dcr, dci, ddr, dwglu, dbglu) = s5out[0], s5out[1:1 + S5_PREPARED], s5out[1 + S5_PREPARED:]
    dare, daim, dls, dbre, dbim = s5_prep_bwd(lp["are"], lp["aim"], lp["ls"], lp["bre"], lp["bim"], dprep,
                                              f"s5_prep_bwd{tag}")
    g["ssm_a_re"] = dare.reshape(SSM_GROUPS, SSM_STATE)
    g["ssm_a_im"] = daim.reshape(SSM_GROUPS, SSM_STATE)
    g["ssm_log_step"] = dls.reshape(SSM_GROUPS, SSM_STATE).sum(axis=1)
    g["ssm_b_re"], g["ssm_b_im"] = _extract_b(dbre), _extract_b(dbim)
    g["ssm_c_re"], g["ssm_c_im"] = _extract_c(dcr), _extract_c(dci)
    g["ssm_d"] = ddr.reshape(SSM_GROUPS, SSM_GROUP)
    g["ssm_w_glu"], g["ssm_b_glu"] = dwglu, dbglu[0]
    dzs_all = (dzs, dzq, dzg, dzsg, dzab)
    dx, dng_in = in_bwd(sv["x"], lp["norm_g"], lp["win"], dzs_all, dx1, f"in_bwd{tag}")
    g["w_in_pieces"] = [wgrad(sv["h"], dz, f"wgrad_in{k}{tag}") for k, dz in enumerate(dzs_all)]
    g["norm_g"] = dng_in[0]
    return dx, g


def _local_step(x, p, target, w, nb):
    lps = [_layer_params(w, l) for l in range(DEPTH)]
    saved = []
    for l in range(DEPTH):
        x, sv = _layer_fwd(x, p[l], lps[l], nb, f"_l{l}")
        saved.append(sv)
    loss_blk, dx, dfg = loss_fwd_bwd(x, w["final_norm_g"][None], target, "loss")
    grads = [None] * DEPTH
    for l in reversed(range(DEPTH)):
        dx, grads[l] = _layer_bwd(dx, saved[l], lps[l], nb, f"_l{l}")
    out = {k: jnp.stack([grads[l][k] for l in range(DEPTH)]) for k in grads[0] if k != "w_in_pieces"}
    out["w_in_pieces"] = [grads[l]["w_in_pieces"] for l in range(DEPTH)]
    out["final_norm_g"] = dfg[0]
    return loss_blk[0, 0], dx, out


def kernel(x, p, norm_g, w_in, ssm_a_re, ssm_a_im, ssm_b_re, ssm_b_im, ssm_c_re, ssm_c_im, ssm_d, ssm_log_step, ssm_w_glu, ssm_b_glu, dn_conv_w, dn_a_log, dn_dt_bias, dn_norm_g, sg_ln_g, sg_ln_b, sg_w, sg_b, w_out, ple_norm_g, w_ple_gate, w_ple, final_norm_g, loss_target, m_norm_g, m_w_in, m_ssm_a_re, m_ssm_a_im, m_ssm_b_re, m_ssm_b_im, m_ssm_c_re, m_ssm_c_im, m_ssm_d, m_ssm_log_step, m_ssm_w_glu, m_ssm_b_glu, m_dn_conv_w, m_dn_a_log, m_dn_dt_bias, m_dn_norm_g, m_sg_ln_g, m_sg_ln_b, m_sg_w, m_sg_b, m_w_out, m_ple_norm_g, m_w_ple_gate, m_w_ple, m_final_norm_g, v_norm_g, v_w_in, v_ssm_a_re, v_ssm_a_im, v_ssm_b_re, v_ssm_b_im, v_ssm_c_re, v_ssm_c_im, v_ssm_d, v_ssm_log_step, v_ssm_w_glu, v_ssm_b_glu, v_dn_conv_w, v_dn_a_log, v_dn_dt_bias, v_dn_norm_g, v_sg_ln_g, v_sg_ln_b, v_sg_w, v_sg_b, v_w_out, v_ple_norm_g, v_w_ple_gate, v_w_ple, v_final_norm_g):
    args = locals()
    w = {n: args[n] for n in WEIGHTS}
    m = {n: args["m_" + n] for n in WEIGHTS}
    v = {n: args["v_" + n] for n in WEIGHTS}
    nb, seq = x.shape[0], x.shape[1]
    t = nb * seq

    full = _gather_full(w)
    loss_local, dx, grads = _local_step(x.reshape(t, D_MODEL), p.reshape(DEPTH, t, D_PLE),
                                        loss_target.reshape(t, D_MODEL), full, nb)
    outs, loss = _reduce_and_update(grads, w, m, v, loss_local)
    return (loss, dx.reshape(nb, seq, D_MODEL), *[outs[0][n] for n in WEIGHTS], *[outs[1][n] for n in WEIGHTS],
            *[outs[2][n] for n in WEIGHTS], *[outs[3][n] for n in WEIGHTS])


def _gather_full(w):
    sh_names = [n for n, _ in SHARDED]
    shards = [w[n] if n == "dn_conv_w" else w[n].astype(_COMM) for n in sh_names]
    gathered = gather_weights(shards, [k for _, k in SHARDED], "gather_weights")
    chip = 2 * lax.axis_index("x") + lax.axis_index("y")
    full = {n: w[n] for n in REPLICATED}
    for (n, kind), shard, got in zip(SHARDED, shards, gathered):
        if kind[0] == "slot":
            full[n] = lax.dynamic_update_index_in_dim(got, shard, chip, 0)
        else:
            full[n] = lax.dynamic_update_slice_in_dim(got, shard, chip * kind[2], axis=kind[1] + 1)
    slots = full.pop("w_in")
    full["w_in_perm"] = _permuted_from_shards([slots[q] for q in range(N_CHIPS)]).astype(_MXU)
    return full


def _reduce_and_update(grads, w, m, v, loss_local):
    sh_names = [n for n, _ in SHARDED]
    sh_kinds = [k for _, k in SHARDED]
    owners = [SMALL_OWNER[n] for n in REPLICATED + ("loss",)]

    def small_views(d):
        return [_small_view(d[n]) for n in REPLICATED]

    grads["w_in"] = jnp.stack([jnp.stack(_shards_from_groups(pieces)) for pieces in grads["w_in_pieces"]])
    gs = [grads[n] if n == "dn_conv_w" else grads[n].astype(_COMM) for n in sh_names]
    sm = small_views(grads) + [_small_view(loss_local.reshape(1))]
    core = lax.axis_index("c")
    chip = 2 * lax.axis_index("x") + lax.axis_index("y")
    got, sm_got = exchange_halves(gs, sm, owners, "exchange_halves")
    sums = add_pairs([lax.dynamic_index_in_dim(g, core, 0, keepdims=False) for g in gs] + sm, list(got) + list(sm_got),
                     "add_halves")
    sums, sm_sums = sums[:len(gs)], sums[len(gs):]
    parts, sm_parts = reduce_to_chips(sums, sh_kinds, sm_sums, owners, "reduce_to_chips")
    parts = list(parts)
    for k, (kind, total) in enumerate(zip(sh_kinds, sums)):
        if kind[0] == "slot":
            own = lax.dynamic_index_in_dim(total, chip, 0, keepdims=False)
        else:
            own = lax.dynamic_slice_in_dim(total, chip * kind[2], kind[2], axis=kind[1])
        parts[k] = lax.dynamic_update_index_in_dim(parts[k], own, chip, 0)
    half = core.astype(jnp.int32).reshape(1)
    totals = [sum_parts(half, part, f"sum_{n}") for n, part in zip(sh_names, parts)]
    sm_totals = sum_parts_small(chip.astype(jnp.int32).reshape(1), sm_parts, sm_sums, "sum_replicated")
    g_big, g_small = share_halves(totals, sm_totals, owners, "share_halves")
    outs = [dict(zip(sh_names, g_big)), {}, {}, {}]
    for n, g in zip(sh_names, g_big):
        outs[1][n], outs[2][n], outs[3][n] = adamw(g, w[n], m[n], v[n], f"adamw_{n}")
    small_results = [g_small[:-1]] + adamw_small(g_small[:-1], small_views(w), small_views(m), small_views(v),
                                                 "adamw_replicated")
    for j in range(4):
        for n, r in zip(REPLICATED, small_results[j]):
            outs[j][n] = r.reshape(-1)[:w[n].size].reshape(w[n].shape)
    return outs, g_small[-1][0, 0]
```

```python
import functools

import jax
import jax.numpy as jnp
from jax import lax
from jax.experimental import pallas as pl
from jax.experimental.pallas import tpu as pltpu

f32 = jnp.float32
bf16 = jnp.bfloat16

_MXU = bf16
_COMM = bf16
HIGH = lax.Precision.HIGH

D_MODEL = 1024
DEPTH = 2
D_PLE = 256
D_SSM = 256
D_DN = 512
D_SG = 256
SSM_GROUPS = 16
SSM_GROUP = 16
SSM_STATE = 64
N_STATE = SSM_GROUPS * SSM_STATE
DN_HEADS = 4
DN_HEAD_DIM = 128
DN_CONV = 4
DN_HALO = 16
DN_CHUNK = 64
SG_HEADS = 4
SG_HEAD_DIM = 64
SG_CHUNK = 128
S5_CHUNK = 1024
S5_GROUP_ROWS = 8
EPS = 1e-6
D_IN = 3336
D_IN_PAD = 3456
LANE = 128

ADAM_LR = 0.001
ADAM_B1 = 0.9
ADAM_B2 = 0.999
ADAM_EPS = 1e-08
ADAM_WD = 0.01
ADAM_STEP = 10

N_CHIPS = 4
N_DEV = 8

Z_COLS = ((0, 512), (512, 2048), (2048, 2560), (2560, 3328), (3328, 3456))

GROUP_COLS = ((0, 512), (512, 2048), (2056, 2568), (2568, 3336), (2048, 2056))
SHARD_COLS = D_IN // 4

SHARDED = (("w_in", ("slot",)), ("ssm_w_glu", ("win", 0, 64)), ("dn_conv_w", ("win", 1, 384)),
           ("w_out", ("win", 0, 256)), ("w_ple_gate", ("win", 0, 256)), ("w_ple", ("win", 1, 256)))
REPLICATED = ("norm_g", "ssm_a_re", "ssm_a_im", "ssm_b_re", "ssm_b_im", "ssm_c_re", "ssm_c_im", "ssm_d",
              "ssm_log_step", "ssm_b_glu", "dn_a_log", "dn_dt_bias", "dn_norm_g", "sg_ln_g", "sg_ln_b", "sg_w",
              "sg_b", "ple_norm_g", "final_norm_g")
SMALL_OWNER = {n: int(n.startswith("ssm_")) for n in REPLICATED + ("loss",)}
WEIGHTS = ("norm_g", "w_in", "ssm_a_re", "ssm_a_im", "ssm_b_re", "ssm_b_im", "ssm_c_re", "ssm_c_im", "ssm_d",
           "ssm_log_step", "ssm_w_glu", "ssm_b_glu", "dn_conv_w", "dn_a_log", "dn_dt_bias", "dn_norm_g", "sg_ln_g",
           "sg_ln_b", "sg_w", "sg_b", "w_out", "ple_norm_g", "w_ple_gate", "w_ple", "final_norm_g")

VMEM_BIG = 56 * 1024 * 1024


def _mm(a, b):
    return jnp.dot(a.astype(_MXU), b.astype(_MXU), preferred_element_type=f32)


def _mm_nt(a, b):
    return lax.dot_general(a.astype(_MXU), b.astype(_MXU), (((1,), (1,)), ((), ())), preferred_element_type=f32)


def _mm_tn(a, b):
    return lax.dot_general(a.astype(_MXU), b.astype(_MXU), (((0,), (0,)), ((), ())), preferred_element_type=f32)


@jax.custom_vjp
def bdot(a, b):
    return _mm(a, b)


def _bdot_fwd(a, b):
    return _mm(a, b), (a, b)


def _bdot_bwd(res, g):
    a, b = res
    return _mm_nt(g, b).astype(a.dtype), _mm_tn(a, g).astype(b.dtype)


bdot.defvjp(_bdot_fwd, _bdot_bwd)


@jax.custom_vjp
def bdot_nt(a, b):
    return _mm_nt(a, b)


def _bdot_nt_fwd(a, b):
    return _mm_nt(a, b), (a, b)


def _bdot_nt_bwd(res, g):
    a, b = res
    return _mm(g, b).astype(a.dtype), _mm_tn(g, a).astype(b.dtype)


bdot_nt.defvjp(_bdot_nt_fwd, _bdot_nt_bwd)


@jax.custom_vjp
def bdot_tn(a, b):
    return _mm_tn(a, b)


def _bdot_tn_fwd(a, b):
    return _mm_tn(a, b), (a, b)


def _bdot_tn_bwd(res, g):
    a, b = res
    return _mm_nt(b, g).astype(a.dtype), _mm(a, g).astype(b.dtype)


bdot_tn.defvjp(_bdot_tn_fwd, _bdot_tn_bwd)


def hdot(a, b):
    return jnp.dot(a, b, precision=HIGH, preferred_element_type=f32)


def _unit_lower_inverses(ms):
    n = ms[0].shape[0]
    eye = (lax.broadcasted_iota(jnp.int32, (n, n), 0) == lax.broadcasted_iota(jnp.int32, (n, n), 1)).astype(f32)
    pw = [-m for m in ms]
    inv = [eye + p for p in pw]
    for _ in range(n.bit_length() - 2):
        pw = [hdot(p, p) for p in pw]
        inv = [a + hdot(a, p) for a, p in zip(inv, pw)]
    return inv


@jax.custom_vjp
def solve_unit_lower(ms, rhs, inv):
    return [hdot(a, r) for a, r in zip(inv, rhs)]


def _solve_unit_lower_fwd(ms, rhs, inv):
    xs = [hdot(a, r) for a, r in zip(inv, rhs)]
    return xs, (inv, xs)


def _solve_unit_lower_bwd(res, gs):
    inv, xs = res
    d_rhs = [lax.dot_general(a, g, (((0,), (0,)), ((), ())), precision=HIGH, preferred_element_type=f32)
             for a, g in zip(inv, gs)]
    d_ms = [-lax.dot_general(d, x, (((1,), (1,)), ((), ())), precision=HIGH, preferred_element_type=f32)
            for d, x in zip(d_rhs, xs)]
    return d_ms, d_rhs, [jnp.zeros_like(a) for a in inv]


solve_unit_lower.defvjp(_solve_unit_lower_fwd, _solve_unit_lower_bwd)


@functools.partial(jax.custom_vjp, nondiff_argnums=(1,))
def roll_rows(x, k):
    return pltpu.roll(x, k, 0)


def _roll_rows_fwd(x, k):
    return pltpu.roll(x, k, 0), None


def _roll_rows_bwd(k, _, g):
    return (pltpu.roll(g, g.shape[0] - k, 0),)


roll_rows.defvjp(_roll_rows_fwd, _roll_rows_bwd)


def _row_ids(shape):
    return lax.broadcasted_iota(jnp.int32, shape, 0)


def _rms(x, g):
    return x * lax.rsqrt(jnp.mean(x * x, axis=-1, keepdims=True) + EPS) * g


def _layer_norm(x, g, b):
    mu = jnp.mean(x, axis=-1, keepdims=True)
    xc = x - mu
    return xc * lax.rsqrt(jnp.mean(xc * xc, axis=-1, keepdims=True) + EPS) * g + b


def _s5_prep(are, aim, ls, bre, bim):
    step = jnp.exp(ls)
    mag = jnp.exp(are * step)
    lr = mag * jnp.cos(aim * step)
    li = mag * jnp.sin(aim * step)
    den = are * are + aim * aim
    nr, ni = lr - 1.0, li
    fr = (nr * are + ni * aim) / den
    fi = (ni * are - nr * aim) / den
    bbr = fr * bre - fi * bim
    bbi = fr * bim + fi * bre
    pr = jnp.broadcast_to(lr, (S5_GROUP_ROWS, N_STATE))
    pi = jnp.broadcast_to(li, (S5_GROUP_ROWS, N_STATE))
    d = 1
    while d < S5_GROUP_ROWS:
        keep = _row_ids(pr.shape) >= d
        sr, si = roll_rows(pr, d), roll_rows(pi, d)
        pr, pi = jnp.where(keep, pr * sr - pi * si, pr), jnp.where(keep, pr * si + pi * sr, pi)
        d *= 2
    return pr, pi, bbr, bbi


def _s5_chunk(u, gate, hr, hi, pr, pi, bbr, bbi, cr, ci, dr, wglu, bglu):
    n, steps = u.shape[0], S5_GROUP_ROWS
    groups = n // steps
    xr = bdot(u, bbr)
    xi = bdot(u, bbi)
    lr, li = pr[0:1], pi[0:1]
    rs, ims = [xr[:groups]], [xi[:groups]]
    for t in range(1, steps):
        a, b = rs[-1], ims[-1]
        rs.append(xr[t * groups:(t + 1) * groups] + lr * a - li * b)
        ims.append(xi[t * groups:(t + 1) * groups] + lr * b + li * a)
    er, ei = rs[-1], ims[-1]
    mr, mi = pr[steps - 1:steps], pi[steps - 1:steps]
    gid = _row_ids(er.shape)
    er, ei = (er + jnp.where(gid == 0, mr * hr - mi * hi, 0.0), ei + jnp.where(gid == 0, mr * hi + mi * hr, 0.0))
    d = 1
    while d < groups:
        sr = jnp.where(gid >= d, roll_rows(er, d), 0.0)
        si = jnp.where(gid >= d, roll_rows(ei, d), 0.0)
        er, ei = er + mr * sr - mi * si, ei + mr * si + mi * sr
        mr, mi = mr * mr - mi * mi, 2.0 * mr * mi
        d *= 2
    before_r = jnp.where(gid == 0, hr, roll_rows(er, 1))
    before_i = jnp.where(gid == 0, hi, roll_rows(ei, 1))
    xr = jnp.concatenate([rs[t] + pr[t:t + 1] * before_r - pi[t:t + 1] * before_i for t in range(steps)], axis=0)
    xi = jnp.concatenate([ims[t] + pr[t:t + 1] * before_i + pi[t:t + 1] * before_r for t in range(steps)], axis=0)
    y = bdot(xr, cr) - bdot(xi, ci) + dr * u
    y = jax.nn.gelu(y)
    y = y * jax.nn.sigmoid(bdot(y, wglu) + bglu)
    return y * jax.nn.silu(gate), er[groups - 1:groups], ei[groups - 1:groups]


def _dn_pre(xc, xp, w0, w1, w2, w3, is_start, col):
    xp = jnp.where(is_start, 0.0, xp)
    halo_rows = _row_ids(xp.shape)
    acc = w3 * xc
    for d, w in ((1, w2), (2, w1), (3, w0)):
        r = roll_rows(xc, d)
        head = jnp.where(halo_rows >= d, r[:DN_HALO], roll_rows(xp, d))
        acc = acc + w * jnp.concatenate([head, r[DN_HALO:]], axis=0)
    y = jax.nn.silu(acc)
    if col >= 2 * DN_HEADS:
        return y
    nrm = y * lax.rsqrt(jnp.sum(y * y, axis=-1, keepdims=True) + EPS)
    return nrm * DN_HEAD_DIM ** -0.5 if col < DN_HEADS else nrm


def _dn_local(qs, ks, vs, abs_, alog, dtb, invs=None):
    c = DN_CHUNK
    ri = lax.broadcasted_iota(jnp.int32, (c, c), 0)
    ci = lax.broadcasted_iota(jnp.int32, (c, c), 1)
    causal, strict = ri >= ci, ri > ci
    tril = causal.astype(f32)
    gcums = [hdot(tril, -jnp.exp(alog) * jax.nn.softplus(ab + dtb)) for ab in abs_]
    gcum_ts = [g.T for g in gcums]
    sigs = [jax.nn.sigmoid(ab) for ab in abs_]
    chains = [(j, h) for j in range(len(abs_)) for h in range(DN_HEADS)]
    gc = [gcums[j][:, h:h + 1] for j, h in chains]
    decay = [jnp.where(causal, jnp.exp(jnp.where(causal, gc[n] - gcum_ts[j][h:h + 1, :], 0.0)), 0.0)
             for n, (j, h) in enumerate(chains)]
    beta = [sigs[j][:, DN_HEADS + h:DN_HEADS + h + 1] for j, h in chains]
    kb = [ks[j][h] * beta[n] for n, (j, h) in enumerate(chains)]
    ms = [jnp.where(strict, bdot_nt(kb[n], ks[j][h]) * decay[n], 0.0) for n, (j, h) in enumerate(chains)]
    egc = [jnp.exp(g) for g in gc]
    rhs = [jnp.concatenate([vs[j][h] * beta[n], kb[n] * egc[n]], axis=1) for n, (j, h) in enumerate(chains)]
    inv = _unit_lower_inverses(ms) if invs is None else [invs[j][h] for j, h in chains]
    sol = solve_unit_lower(ms, rhs, inv)
    values = [s[:, :DN_HEAD_DIM] for s in sol]
    k_cds = [s[:, DN_HEAD_DIM:] for s in sol]
    attns = [bdot_nt(qs[j][h], ks[j][h]) * decay[n] for n, (j, h) in enumerate(chains)]
    q_decs = [qs[j][h] * egc[n] for n, (j, h) in enumerate(chains)]
    k_decs = [ks[j][h] * jnp.exp(gc[n][c - 1:c, :] - gc[n]) for n, (j, h) in enumerate(chains)]

    def nest(flat):
        return [flat[j * DN_HEADS:(j + 1) * DN_HEADS] for j in range(len(abs_))]

    lasts = [jnp.exp(g[c - 1:c, :]) for g in gcums]
    return nest(values), nest(k_cds), nest(attns), nest(q_decs), nest(k_decs), lasts, nest(inv)


def _dn_step(values, k_cds, attns, q_decs, k_decs, lasts, ggs, sts, ng):
    v_new = [v - bdot(kc, st) for v, kc, st in zip(values, k_cds, sts)]
    o = [bdot(qd, st) for qd, st in zip(q_decs, sts)]
    o = [a + bdot(at, vn) for a, at, vn in zip(o, attns, v_new)]
    new = [st * la + bdot_tn(kd, vn) for st, la, kd, vn in zip(sts, lasts, k_decs, v_new)]
    return [_rms(a, ng) * jax.nn.silu(g) for a, g in zip(o, ggs)], new


def _sg_chunk(u, v, gate, lng, lnb, ws, bt):
    n = SG_CHUNK
    ug = jax.nn.gelu(u)
    vn = _layer_norm(jax.nn.gelu(v), lng, lnb)
    causal = lax.broadcasted_iota(jnp.int32, (n, n), 0) >= lax.broadcasted_iota(jnp.int32, (n, n), 1)
    lane = lax.broadcasted_iota(jnp.int32, (n, D_SG), 1)
    s = jnp.zeros((n, D_SG), f32)
    for h in range(SG_HEADS):
        t = bdot(jnp.where(causal, ws[h], 0.0), vn) + bt[:, h:h + 1]
        s = s + jnp.where((lane >= h * SG_HEAD_DIM) & (lane < (h + 1) * SG_HEAD_DIM), t, 0.0)
    return ug * s * jax.nn.silu(gate)


def _cp(n_grid, vmem=None):
    return pltpu.CompilerParams(dimension_semantics=("arbitrary",) * n_grid, vmem_limit_bytes=vmem)


def _full(shape):
    nd = len(shape)
    return pl.BlockSpec(tuple(shape), lambda *_: (0,) * nd)


def _rows(tm, ncol):
    return pl.BlockSpec((tm, ncol), lambda i: (i, 0))


def _sds(shape, dtype=f32):
    return jax.ShapeDtypeStruct(tuple(shape), dtype)


def _acc(ref, val, first):
    @pl.when(first)
    def _():
        ref[...] = val

    @pl.when(jnp.logical_not(first))
    def _():
        ref[...] += val


def in_fwd(x, g, w, name):
    t, tm = x.shape[0], 512

    def body(x_ref, g_ref, w_ref, h_ref, *z_refs):
        h = _rms(x_ref[...], g_ref[...]).astype(_MXU)
        h_ref[...] = h
        for z_ref, (a, b) in zip(z_refs, Z_COLS):
            z_ref[...] = jnp.dot(h, w_ref[:, a:b], preferred_element_type=f32)

    widths = [b - a for a, b in Z_COLS]
    return pl.pallas_call(
        body, name=name, grid=(t // tm,),
        in_specs=[_rows(tm, D_MODEL), _full((1, D_MODEL)), _full((D_MODEL, D_IN_PAD))],
        out_specs=[_rows(tm, D_MODEL)] + [_rows(tm, n) for n in widths],
        out_shape=[_sds((t, D_MODEL), _MXU)] + [_sds((t, n)) for n in widths],
        compiler_params=_cp(1, VMEM_BIG),
    )(x, g, w)


def in_bwd(x, g, w, dzs, dres, name):
    t, tm = x.shape[0], 512
    widths = [b - a for a, b in Z_COLS]

    def body(x_ref, g_ref, w_ref, dres_ref, *rest):
        dz_refs, (dx_ref, dg_ref) = rest[:5], rest[5:]
        dh = jnp.zeros((tm, D_MODEL), f32)
        for dz_ref, (a, b) in zip(dz_refs, Z_COLS):
            dh = dh + _mm_nt(dz_ref[...], w_ref[:, a:b])
        _, vj = jax.vjp(_rms, x_ref[...], g_ref[...])
        dx, dg = vj(dh)
        dx_ref[...] = dres_ref[...] + dx
        _acc(dg_ref, dg, pl.program_id(0) == 0)

    return pl.pallas_call(
        body, name=name, grid=(t // tm,),
        in_specs=[_rows(tm, D_MODEL), _full((1, D_MODEL)), _full((D_MODEL, D_IN_PAD)), _rows(tm, D_MODEL)]
        + [_rows(tm, n) for n in widths],
        out_specs=[_rows(tm, D_MODEL), _full((1, D_MODEL))],
        out_shape=[_sds((t, D_MODEL)), _sds((1, D_MODEL))],
        compiler_params=_cp(1, VMEM_BIG),
    )(x, g, w, dres, *dzs)


def wgrad(a, g, name):
    t, k = a.shape
    n = g.shape[1]
    tm = min(t, 2048)
    tn = n if n <= 768 else (768 if n % 768 == 0 else 512)
    steps = t // tm

    def body(a_ref, g_ref, o_ref, acc):
        i = pl.program_id(1)
        _acc(acc, _mm_tn(a_ref[...], g_ref[...]), i == 0)

        @pl.when(i == steps - 1)
        def _():
            o_ref[...] = acc[...].astype(o_ref.dtype)

    return pl.pallas_call(
        body, name=name, grid=(n // tn, steps),
        in_specs=[pl.BlockSpec((tm, k), lambda j, i: (i, 0)), pl.BlockSpec((tm, tn), lambda j, i: (i, j))],
        out_specs=pl.BlockSpec((k, tn), lambda j, i: (0, j)),
        out_shape=_sds((k, n), _COMM),
        scratch_shapes=[pltpu.VMEM((k, tn), f32)],
        compiler_params=_cp(2, VMEM_BIG),
    )(a, g)


def post_fwd(x, ys, yd, yg, p, wout, pg, wgate, wple, name):
    t, tm = x.shape[0], 512

    def body(x_ref, ys_ref, yd_ref, yg_ref, p_ref, wout_ref, pg_ref, wgate_ref, wple_ref,
             x2_ref, x1_ref, y_ref, hn_ref):
        y = jnp.concatenate([ys_ref[...], yd_ref[...], yg_ref[...]], axis=1).astype(_MXU)
        y_ref[...] = y
        x1 = x_ref[...] + jnp.dot(y, wout_ref[...], preferred_element_type=f32)
        x1_ref[...] = x1
        hn = _rms(x1, pg_ref[...]).astype(_MXU)
        hn_ref[...] = hn
        gp = jnp.dot(hn, wgate_ref[...], preferred_element_type=f32)
        pp = _mm(p_ref[...], wple_ref[...])
        x2_ref[...] = x1 + jax.nn.sigmoid(gp) * pp

    return pl.pallas_call(
        body, name=name, grid=(t // tm,),
        in_specs=[_rows(tm, D_MODEL), _rows(tm, D_SSM), _rows(tm, D_DN), _rows(tm, D_SG), _rows(tm, D_PLE),
                  _full((D_MODEL, D_MODEL)), _full((1, D_MODEL)), _full((D_MODEL, D_MODEL)), _full((D_PLE, D_MODEL))],
        out_specs=[_rows(tm, D_MODEL)] * 4,
        out_shape=[_sds((t, D_MODEL)), _sds((t, D_MODEL)), _sds((t, D_MODEL), _MXU), _sds((t, D_MODEL), _MXU)],
        compiler_params=_cp(1, VMEM_BIG),
    )(x, ys, yd, yg, p, wout, pg, wgate, wple)


def post_bwd(dx2, x1, hn, p, wout, pg, wgate, wple, name):
    t, tm = dx2.shape[0], 512

    def body(dx2_ref, x1_ref, hn_ref, p_ref, wout_ref, pg_ref, wgate_ref, wple_ref,
             dx1_ref, dgp_ref, dpp_ref, dys_ref, dyd_ref, dyg_ref, dpg_ref):
        dx2 = dx2_ref[...]
        gp = jnp.dot(hn_ref[...], wgate_ref[...], preferred_element_type=f32)
        pp = _mm(p_ref[...], wple_ref[...])
        sg = jax.nn.sigmoid(gp)
        dpp_ref[...] = (dx2 * sg).astype(_MXU)
        dgp = (dx2 * pp * sg * (1.0 - sg)).astype(_MXU)
        dgp_ref[...] = dgp
        dhn = _mm_nt(dgp, wgate_ref[...])
        _, vj = jax.vjp(_rms, x1_ref[...], pg_ref[...])
        dx1n, dpg = vj(dhn)
        dx1 = dx2 + dx1n
        dx1_ref[...] = dx1
        dy = _mm_nt(dx1, wout_ref[...])
        dys_ref[...] = dy[:, :D_SSM]
        dyd_ref[...] = dy[:, D_SSM:D_SSM + D_DN]
        dyg_ref[...] = dy[:, D_SSM + D_DN:]
        _acc(dpg_ref, dpg, pl.program_id(0) == 0)

    return pl.pallas_call(
        body, name=name, grid=(t // tm,),
        in_specs=[_rows(tm, D_MODEL), _rows(tm, D_MODEL), _rows(tm, D_MODEL), _rows(tm, D_PLE),
                  _full((D_MODEL, D_MODEL)), _full((1, D_MODEL)), _full((D_MODEL, D_MODEL)), _full((D_PLE, D_MODEL))],
        out_specs=[_rows(tm, D_MODEL), _rows(tm, D_MODEL), _rows(tm, D_MODEL), _rows(tm, D_SSM), _rows(tm, D_DN),
                   _rows(tm, D_SG), _full((1, D_MODEL))],
        out_shape=[_sds((t, D_MODEL)), _sds((t, D_MODEL), _MXU), _sds((t, D_MODEL), _MXU), _sds((t, D_SSM)),
                   _sds((t, D_DN)), _sds((t, D_SG)), _sds((1, D_MODEL))],
        compiler_params=_cp(1, VMEM_BIG),
    )(dx2, x1, hn, p, wout, pg, wgate, wple)


def loss_fwd_bwd(x, fg, target, name):
    t, tm = x.shape[0], 512

    def body(x_ref, fg_ref, t_ref, loss_ref, dx_ref, dfg_ref):
        def f(xv, gv):
            err = _rms(xv, gv) - t_ref[...]
            return 0.5 * jnp.sum(jnp.mean(err * err, axis=-1))

        val, vj = jax.vjp(f, x_ref[...], fg_ref[...])
        dx, dfg = vj(jnp.ones((), f32))
        dx_ref[...] = dx
        first = pl.program_id(0) == 0
        _acc(dfg_ref, dfg, first)
        _acc(loss_ref, jnp.full((8, LANE), val, f32), first)

    return pl.pallas_call(
        body, name=name, grid=(t // tm,),
        in_specs=[_rows(tm, D_MODEL), _full((1, D_MODEL)), _rows(tm, D_MODEL)],
        out_specs=[_full((8, LANE)), _rows(tm, D_MODEL), _full((1, D_MODEL))],
        out_shape=[_sds((8, LANE)), _sds((t, D_MODEL)), _sds((1, D_MODEL))],
        compiler_params=_cp(1),
    )(x, fg, target)


S5_PREPARED = 4
_S5_PARAM_SHAPES = ((S5_GROUP_ROWS, N_STATE), (S5_GROUP_ROWS, N_STATE), (D_SSM, N_STATE), (D_SSM, N_STATE),
                    (N_STATE, D_SSM), (N_STATE, D_SSM), (1, D_SSM), (D_SSM, D_SSM), (1, D_SSM))

def s5_prep_fwd(are, aim, ls, bre, bim, name):
    def body(are_ref, aim_ref, ls_ref, bre_ref, bim_ref, *outs):
        vals = _s5_prep(are_ref[...], aim_ref[...], ls_ref[...], bre_ref[...], bim_ref[...])
        for o, v in zip(outs, vals):
            o[...] = v

    return pl.pallas_call(body, name=name, out_shape=[_sds(s) for s in _S5_PARAM_SHAPES[:S5_PREPARED]])(
        are, aim, ls, bre, bim)


def s5_prep_bwd(are, aim, ls, bre, bim, cts, name):
    def body(are_ref, aim_ref, ls_ref, bre_ref, bim_ref, *rest):
        ct_refs, outs = rest[:S5_PREPARED], rest[S5_PREPARED:]
        _, vj = jax.vjp(_s5_prep, are_ref[...], aim_ref[...], ls_ref[...], bre_ref[...], bim_ref[...])
        for o, v in zip(outs, vj(tuple(r[...] for r in ct_refs))):
            o[...] = v

    shapes = [(1, N_STATE)] * 3 + [(D_SSM, N_STATE)] * 2
    return pl.pallas_call(body, name=name, out_shape=[_sds(s) for s in shapes])(are, aim, ls, bre, bim, *cts)


def _step_major(ref, cols):
    x = ref[:, cols]
    n, w = x.shape
    return jnp.swapaxes(x.reshape(n // S5_GROUP_ROWS, S5_GROUP_ROWS, w), 0, 1).reshape(n, w)


def _store_step_major(ref, cols, val):
    n, w = val.shape
    ref[:, cols] = jnp.swapaxes(val.reshape(S5_GROUP_ROWS, n // S5_GROUP_ROWS, w), 0, 1).reshape(n, w).astype(ref.dtype)


def s5_fwd(z, params, nb, name):
    t = z.shape[0]
    nc = t // nb // S5_CHUNK
    npar = len(_S5_PARAM_SHAPES)

    def body(z_ref, *rest):
        p_refs, (y_ref, hs_ref, hr_s, hi_s) = rest[:npar], rest[npar:]

        @pl.when(pl.program_id(1) == 0)
        def _():
            hr_s[...] = jnp.zeros_like(hr_s)
            hi_s[...] = jnp.zeros_like(hi_s)

        hr, hi = hr_s[...], hi_s[...]
        hs_ref[0, :, :N_STATE] = hr
        hs_ref[0, :, N_STATE:] = hi
        y, nhr, nhi = _s5_chunk(_step_major(z_ref, slice(0, D_SSM)), _step_major(z_ref, slice(D_SSM, 2 * D_SSM)),
                                hr, hi, *[r[...] for r in p_refs])
        _store_step_major(y_ref, slice(0, D_SSM), y)
        hr_s[...] = nhr
        hi_s[...] = nhi

    return pl.pallas_call(
        body, name=name, grid=(nb, nc),
        in_specs=[pl.BlockSpec((S5_CHUNK, 2 * D_SSM), lambda b, c: (b * nc + c, 0))]
        + [_full(s) for s in _S5_PARAM_SHAPES],
        out_specs=[pl.BlockSpec((S5_CHUNK, D_SSM), lambda b, c: (b * nc + c, 0)),
                   pl.BlockSpec((1, 1, 2 * N_STATE), lambda b, c: (b * nc + c, 0, 0))],
        out_shape=[_sds((t, D_SSM)), _sds((nb * nc, 1, 2 * N_STATE))],
        scratch_shapes=[pltpu.VMEM((1, N_STATE), f32), pltpu.VMEM((1, N_STATE), f32)],
        compiler_params=_cp(2, VMEM_BIG),
    )(z, *params)


def s5_bwd(z, params, hs, dy, nb, name):
    t = z.shape[0]
    nc = t // nb // S5_CHUNK
    npar = len(_S5_PARAM_SHAPES)

    def body(z_ref, hs_ref, dy_ref, *rest):
        p_refs, dz_ref, dp_refs, (dhr_s, dhi_s) = rest[:npar], rest[npar], rest[npar + 1:2 * npar + 1], rest[2 * npar + 1:]

        @pl.when(pl.program_id(1) == 0)
        def _():
            dhr_s[...] = jnp.zeros_like(dhr_s)
            dhi_s[...] = jnp.zeros_like(dhi_s)

        prim = (_step_major(z_ref, slice(0, D_SSM)), _step_major(z_ref, slice(D_SSM, 2 * D_SSM)),
                hs_ref[0, :, :N_STATE], hs_ref[0, :, N_STATE:]) + tuple(r[...] for r in p_refs)
        _, vj = jax.vjp(_s5_chunk, *prim)
        cts = vj((_step_major(dy_ref, slice(0, D_SSM)), dhr_s[...], dhi_s[...]))
        _store_step_major(dz_ref, slice(0, D_SSM), cts[0])
        _store_step_major(dz_ref, slice(D_SSM, 2 * D_SSM), cts[1])
        dhr_s[...] = cts[2]
        dhi_s[...] = cts[3]
        first = (pl.program_id(0) == 0) & (pl.program_id(1) == 0)
        for r, v in zip(dp_refs, cts[4:]):
            _acc(r, v, first)

    rev = lambda b, c: (b * nc + nc - 1 - c, 0)
    return pl.pallas_call(
        body, name=name, grid=(nb, nc),
        in_specs=[pl.BlockSpec((S5_CHUNK, 2 * D_SSM), rev),
                  pl.BlockSpec((1, 1, 2 * N_STATE), lambda b, c: (b * nc + nc - 1 - c, 0, 0)),
                  pl.BlockSpec((S5_CHUNK, D_SSM), rev)] + [_full(s) for s in _S5_PARAM_SHAPES],
        out_specs=[pl.BlockSpec((S5_CHUNK, 2 * D_SSM), rev)] + [_full(s) for s in _S5_PARAM_SHAPES],
        out_shape=[_sds((t, 2 * D_SSM), _MXU)] + [_sds(s) for s in _S5_PARAM_SHAPES],
        scratch_shapes=[pltpu.VMEM((1, N_STATE), f32), pltpu.VMEM((1, N_STATE), f32)],
        compiler_params=_cp(2, VMEM_BIG),
    )(z, hs, dy, *params)


DN_PRE_ROWS = 256
DN_COLS = 3 * D_DN // LANE


def dn_pre_bwd(zq, convw, dqkv, seq, name):
    t, tb = zq.shape[0], DN_PRE_ROWS
    nrow = t // tb
    per_seq = seq // tb

    def body(xc_ref, xp_ref, w_ref, d_ref, dx_ref, dw_ref, carry):
        step = pl.program_id(0)
        i = nrow - 1 - step

        @pl.when(step == 0)
        def _():
            carry[...] = jnp.zeros_like(carry)

        for j in range(DN_COLS):
            cols = slice(j * LANE, (j + 1) * LANE)
            fn = functools.partial(_dn_pre, is_start=i % per_seq == 0, col=j)
            _, vj = jax.vjp(fn, xc_ref[:, cols], xp_ref[:, cols], w_ref[0:1, cols], w_ref[1:2, cols],
                            w_ref[2:3, cols], w_ref[3:4, cols])
            dxc, dxp, dw0, dw1, dw2, dw3 = vj(d_ref[:, cols])
            dx_ref[:tb - DN_HALO, cols] = dxc[:tb - DN_HALO].astype(_MXU)
            dx_ref[tb - DN_HALO:, cols] = (dxc[tb - DN_HALO:] + carry[:, cols]).astype(_MXU)
            carry[:, cols] = dxp
            for k, dw in enumerate((dw0, dw1, dw2, dw3)):
                @pl.when(step == 0)
                def _():
                    dw_ref[k:k + 1, cols] = dw

                @pl.when(step != 0)
                def _():
                    dw_ref[k:k + 1, cols] += dw

    rev = lambda s: (nrow - 1 - s, 0)
    return pl.pallas_call(
        body, name=name, grid=(nrow,),
        in_specs=[pl.BlockSpec((tb, 3 * D_DN), rev),
                  pl.BlockSpec((DN_HALO, 3 * D_DN),
                               lambda s: (jnp.maximum((nrow - 1 - s) * (tb // DN_HALO) - 1, 0), 0)),
                  _full((DN_CONV, 3 * D_DN)), pl.BlockSpec((tb, 3 * D_DN), rev)],
        out_specs=[pl.BlockSpec((tb, 3 * D_DN), rev), _full((DN_CONV, 3 * D_DN))],
        out_shape=[_sds((t, 3 * D_DN), _MXU), _sds((DN_CONV, 3 * D_DN))],
        scratch_shapes=[pltpu.VMEM((DN_HALO, 3 * D_DN), f32)],
        compiler_params=_cp(1, VMEM_BIG),
    )(zq, zq, convw, dqkv)


DN_LOCAL_CHUNKS = 4
DN_ATTN = DN_HEADS * DN_CHUNK


def _dn_heads(ref, rows, base=0):
    return [ref[rows, base + h * DN_HEAD_DIM:base + (h + 1) * DN_HEAD_DIM] for h in range(DN_HEADS)]


def dn_front_fwd(zq, convw, ab, alog, dtb, seq, name):
    t = zq.shape[0]
    c, n = DN_CHUNK, DN_LOCAL_CHUNKS
    per_seq = seq // (n * c)

    def body(xc_ref, xp_ref, w_ref, ab_ref, alog_ref, dtb_ref,
             qkv_ref, val_ref, kcd_ref, attn_ref, qd_ref, kd_ref, el_ref, inv_ref):
        is_start = pl.program_id(0) % per_seq == 0
        blocks = []
        for j in range(DN_COLS):
            cols = slice(j * LANE, (j + 1) * LANE)
            blocks.append(_dn_pre(xc_ref[:, cols], xp_ref[:, cols], w_ref[0:1, cols], w_ref[1:2, cols],
                                  w_ref[2:3, cols], w_ref[3:4, cols], is_start, j))
            qkv_ref[:, cols] = blocks[-1]
        rows = [pl.ds(j * c, c) for j in range(n)]

        def heads(base, j):
            return [blocks[base + h][j * c:(j + 1) * c] for h in range(DN_HEADS)]

        vals, kcds, attns, qds, kds, els, invs = _dn_local(
            [heads(0, j) for j in range(n)], [heads(DN_HEADS, j) for j in range(n)],
            [heads(2 * DN_HEADS, j) for j in range(n)], [ab_ref[r, :] for r in rows], alog_ref[...], dtb_ref[...])
        for j, r in enumerate(rows):
            for h in range(DN_HEADS):
                lo, hi = h * DN_HEAD_DIM, (h + 1) * DN_HEAD_DIM
                val_ref[r, lo:hi] = vals[j][h]
                kcd_ref[r, lo:hi] = kcds[j][h].astype(_MXU)
                qd_ref[r, lo:hi] = qds[j][h].astype(_MXU)
                kd_ref[r, lo:hi] = kds[j][h].astype(_MXU)
                attn_ref[r, h * c:(h + 1) * c] = attns[j][h].astype(_MXU)
                inv_ref[r, h * c:(h + 1) * c] = invs[j][h]
            el_ref[j] = els[j]

    wide = _rows(n * c, D_DN)
    outs = pl.pallas_call(
        body, name=name, grid=(t // (n * c),),
        in_specs=[_rows(n * c, 3 * D_DN),
                  pl.BlockSpec((DN_HALO, 3 * D_DN), lambda i: (jnp.maximum(i * (n * c // DN_HALO) - 1, 0), 0)),
                  _full((DN_CONV, 3 * D_DN)), _rows(n * c, LANE), _full((1, LANE)), _full((1, LANE))],
        out_specs=[_rows(n * c, 3 * D_DN), wide, wide, _rows(n * c, DN_ATTN), wide, wide,
                   pl.BlockSpec((n, 1, LANE), lambda i: (i, 0, 0)), _rows(n * c, DN_ATTN)],
        out_shape=[_sds((t, 3 * D_DN)), _sds((t, D_DN)), _sds((t, D_DN), _MXU), _sds((t, DN_ATTN), _MXU),
                   _sds((t, D_DN), _MXU), _sds((t, D_DN), _MXU), _sds((t // c, 1, LANE)), _sds((t, DN_ATTN))],
        compiler_params=_cp(1, VMEM_BIG),
    )(zq, zq, convw, ab, alog, dtb)
    return outs[0], outs[1:7], outs[7]


def dn_local_bwd(qkv, ab, alog, dtb, inv, cts, name):
    t = qkv.shape[0]
    c, n = DN_CHUNK, DN_LOCAL_CHUNKS

    def body(qkv_ref, ab_ref, alog_ref, dtb_ref, inv_ref, dval_ref, dkcd_ref, dattn_ref, dqd_ref, dkd_ref, del_ref,
             dqkv_ref, dab_ref, dalog_ref, ddtb_ref):
        rows = [pl.ds(j * c, c) for j in range(n)]
        invs = [[inv_ref[r, h * c:(h + 1) * c] for h in range(DN_HEADS)] for r in rows]

        def local(qs, ks, vs, abs_, alog, dtb):
            return _dn_local(qs, ks, vs, abs_, alog, dtb, invs)[:6]

        _, vj = jax.vjp(local, [_dn_heads(qkv_ref, r) for r in rows], [_dn_heads(qkv_ref, r, D_DN) for r in rows],
                        [_dn_heads(qkv_ref, r, 2 * D_DN) for r in rows], [ab_ref[r, :] for r in rows], alog_ref[...],
                        dtb_ref[...])
        dattn = [[dattn_ref[r, h * c:(h + 1) * c] for h in range(DN_HEADS)] for r in rows]
        dq, dk, dv, dab, dalog, ddtb = vj(([_dn_heads(dval_ref, r) for r in rows], [_dn_heads(dkcd_ref, r) for r in rows],
                                           dattn, [_dn_heads(dqd_ref, r) for r in rows],
                                           [_dn_heads(dkd_ref, r) for r in rows], [del_ref[j] for j in range(n)]))
        for j, r in enumerate(rows):
            for h in range(DN_HEADS):
                lo, hi = h * DN_HEAD_DIM, (h + 1) * DN_HEAD_DIM
                dqkv_ref[r, lo:hi] = dq[j][h]
                dqkv_ref[r, D_DN + lo:D_DN + hi] = dk[j][h]
                dqkv_ref[r, 2 * D_DN + lo:2 * D_DN + hi] = dv[j][h]
            dab_ref[r, :] = dab[j].astype(_MXU)
        first = pl.program_id(0) == 0
        _acc(dalog_ref, dalog, first)
        _acc(ddtb_ref, ddtb, first)

    wide = _rows(n * c, D_DN)
    return pl.pallas_call(
        body, name=name, grid=(t // (n * c),),
        in_specs=[_rows(n * c, 3 * D_DN), _rows(n * c, LANE), _full((1, LANE)), _full((1, LANE)),
                  _rows(n * c, DN_ATTN), wide, wide, _rows(n * c, DN_ATTN), wide, wide,
                  pl.BlockSpec((n, 1, LANE), lambda i: (i, 0, 0))],
        out_specs=[_rows(n * c, 3 * D_DN), _rows(n * c, LANE), _full((1, LANE)), _full((1, LANE))],
        out_shape=[_sds((t, 3 * D_DN)), _sds((t, LANE), _MXU), _sds((1, LANE)), _sds((1, LANE))],
        compiler_params=_cp(1),
    )(qkv, ab, alog, dtb, inv, *cts)


def _seq_view(a, nb):
    return a.reshape((nb, a.shape[0] // nb) + a.shape[1:])


def _dn_chains(nb):
    return [(b, h) for b in range(nb) for h in range(DN_HEADS)]


DN_SCAN_CHUNKS = 4


def _dn_step_operands(val_ref, kcd_ref, attn_ref, qd_ref, kd_ref, el_ref, gg_ref, nb, j):
    chains = _dn_chains(nb)
    c = DN_CHUNK
    rows = pl.ds(j * c, c)

    def wide(ref):
        return [ref[b, rows, h * DN_HEAD_DIM:(h + 1) * DN_HEAD_DIM].astype(f32) for b, h in chains]

    attns = [attn_ref[b, rows, h * c:(h + 1) * c].astype(f32) for b, h in chains]
    return (wide(val_ref), wide(kcd_ref), attns, wide(qd_ref), wide(kd_ref),
            [el_ref[b, j, :, h:h + 1] for b, h in chains], wide(gg_ref))


def dn_scan_fwd(loc, gg, ng, nb, name):
    val, kcd, attn, qd, kd, el = loc
    t = val.shape[0]
    c, n = DN_CHUNK, DN_SCAN_CHUNKS
    nc = t // nb // c
    ns = nb * DN_HEADS

    def body(val_ref, kcd_ref, attn_ref, qd_ref, kd_ref, el_ref, gg_ref, ng_ref, y_ref, ss_ref, st):
        @pl.when(pl.program_id(0) == 0)
        def _():
            st[...] = jnp.zeros_like(st)

        sts = [st[i] for i in range(ns)]
        for j in range(n):
            for i in range(ns):
                ss_ref[j, i] = sts[i]
            ys, sts = _dn_step(*_dn_step_operands(val_ref, kcd_ref, attn_ref, qd_ref, kd_ref, el_ref, gg_ref, nb, j),
                               sts, ng_ref[...])
            for i, (b, h) in enumerate(_dn_chains(nb)):
                y_ref[b, pl.ds(j * c, c), h * DN_HEAD_DIM:(h + 1) * DN_HEAD_DIM] = ys[i]
        for i in range(ns):
            st[i] = sts[i]

    def blk(w):
        return pl.BlockSpec((nb, n * c, w), lambda k: (0, k, 0))

    el_spec = pl.BlockSpec((nb, n, 1, LANE), lambda k: (0, k, 0, 0))
    y, ss = pl.pallas_call(
        body, name=name, grid=(nc // n,),
        in_specs=[blk(D_DN), blk(D_DN), blk(DN_ATTN), blk(D_DN), blk(D_DN), el_spec, blk(D_DN), _full((1, LANE))],
        out_specs=[blk(D_DN), pl.BlockSpec((n, ns, DN_HEAD_DIM, DN_HEAD_DIM), lambda k: (k, 0, 0, 0))],
        out_shape=[_sds((nb, t // nb, D_DN)), _sds((nc, ns, DN_HEAD_DIM, DN_HEAD_DIM))],
        scratch_shapes=[pltpu.VMEM((ns, DN_HEAD_DIM, DN_HEAD_DIM), f32)],
        compiler_params=_cp(1, VMEM_BIG),
    )(_seq_view(val, nb), _seq_view(kcd, nb), _seq_view(attn, nb), _seq_view(qd, nb), _seq_view(kd, nb),
      el.reshape(nb, nc, 1, LANE), _seq_view(gg, nb), ng)
    return y.reshape(t, D_DN), ss


def dn_scan_bwd(loc, gg, ng, ss, dy, nb, name):
    val, kcd, attn, qd, kd, el = loc
    t = val.shape[0]
    c, n = DN_CHUNK, DN_SCAN_CHUNKS
    nc = t // nb // c
    ns = nb * DN_HEADS
    steps = nc // n

    def body(val_ref, kcd_ref, attn_ref, qd_ref, kd_ref, el_ref, gg_ref, ng_ref, ss_ref, dy_ref,
             dval_ref, dkcd_ref, dattn_ref, dqd_ref, dkd_ref, del_ref, dgg_ref, dng_ref, dst):
        @pl.when(pl.program_id(0) == 0)
        def _():
            dst[...] = jnp.zeros_like(dst)

        lane = lax.broadcasted_iota(jnp.int32, (1, LANE), 1)
        chains = _dn_chains(nb)
        ds = [dst[i] for i in range(ns)]
        dng_tot = jnp.zeros((1, LANE), f32)
        for j in reversed(range(n)):
            rows = pl.ds(j * c, c)
            _, vj = jax.vjp(_dn_step,
                            *_dn_step_operands(val_ref, kcd_ref, attn_ref, qd_ref, kd_ref, el_ref, gg_ref, nb, j),
                            [ss_ref[j, i] for i in range(ns)], ng_ref[...])
            dys = [dy_ref[b, rows, h * DN_HEAD_DIM:(h + 1) * DN_HEAD_DIM] for b, h in chains]
            dval, dkcd, dattn, dqd, dkd, dlast, dgg, ds, dng = vj((dys, ds))
            dng_tot = dng_tot + dng
            del_rows = [jnp.zeros((1, LANE), f32) for _ in range(nb)]
            for i, (b, h) in enumerate(chains):
                cols = slice(h * DN_HEAD_DIM, (h + 1) * DN_HEAD_DIM)
                dval_ref[b, rows, cols] = dval[i]
                dkcd_ref[b, rows, cols] = dkcd[i]
                dattn_ref[b, rows, h * c:(h + 1) * c] = dattn[i]
                dqd_ref[b, rows, cols] = dqd[i]
                dkd_ref[b, rows, cols] = dkd[i]
                dgg_ref[b, rows, cols] = dgg[i].astype(_MXU)
                del_rows[b] = del_rows[b] + jnp.where(lane == h, dlast[i], 0.0)
            for b in range(nb):
                del_ref[b, j] = del_rows[b]
        for i in range(ns):
            dst[i] = ds[i]
        _acc(dng_ref, dng_tot, pl.program_id(0) == 0)

    def blk(w):
        return pl.BlockSpec((nb, n * c, w), lambda k: (0, steps - 1 - k, 0))

    el_spec = pl.BlockSpec((nb, n, 1, LANE), lambda k: (0, steps - 1 - k, 0, 0))
    outs = pl.pallas_call(
        body, name=name, grid=(steps,),
        in_specs=[blk(D_DN), blk(D_DN), blk(DN_ATTN), blk(D_DN), blk(D_DN), el_spec, blk(D_DN), _full((1, LANE)),
                  pl.BlockSpec((n, ns, DN_HEAD_DIM, DN_HEAD_DIM), lambda k: (steps - 1 - k, 0, 0, 0)), blk(D_DN)],
        out_specs=[blk(D_DN), blk(D_DN), blk(DN_ATTN), blk(D_DN), blk(D_DN), el_spec, blk(D_DN), _full((1, LANE))],
        out_shape=[_sds((nb, t // nb, D_DN)), _sds((nb, t // nb, D_DN)), _sds((nb, t // nb, DN_ATTN)),
                   _sds((nb, t // nb, D_DN)), _sds((nb, t // nb, D_DN)), _sds((nb, nc, 1, LANE)),
                   _sds((nb, t // nb, D_DN), _MXU), _sds((1, LANE))],
        scratch_shapes=[pltpu.VMEM((ns, DN_HEAD_DIM, DN_HEAD_DIM), f32)],
        compiler_params=_cp(1, VMEM_BIG),
    )(_seq_view(val, nb), _seq_view(kcd, nb), _seq_view(attn, nb), _seq_view(qd, nb), _seq_view(kd, nb),
      el.reshape(nb, nc, 1, LANE), _seq_view(gg, nb), ng, ss, _seq_view(dy, nb))
    dloc = [o.reshape((t,) + o.shape[2:]) for o in outs[:5]] + [outs[5].reshape(t // c, 1, LANE)]
    return dloc, outs[6].reshape(t, D_DN), outs[7]


SG_ROWS = 512


def sg_fwd(z, lng, lnb, w, bt, name):
    t = z.shape[0]

    def body(z_ref, lng_ref, lnb_ref, w_ref, bt_ref, y_ref):
        ws = [w_ref[h] for h in range(SG_HEADS)]
        for k in range(SG_ROWS // SG_CHUNK):
            r = pl.ds(k * SG_CHUNK, SG_CHUNK)
            y_ref[r, :] = _sg_chunk(z_ref[r, :D_SG], z_ref[r, D_SG:2 * D_SG], z_ref[r, 2 * D_SG:], lng_ref[...],
                                    lnb_ref[...], ws, bt_ref[...])

    return pl.pallas_call(
        body, name=name, grid=(t // SG_ROWS,),
        in_specs=[_rows(SG_ROWS, 3 * D_SG), _full((1, D_SG)), _full((1, D_SG)),
                  _full((SG_HEADS, SG_CHUNK, SG_CHUNK)), _full((SG_CHUNK, LANE))],
        out_specs=_rows(SG_ROWS, D_SG),
        out_shape=_sds((t, D_SG)),
        compiler_params=_cp(1),
    )(z, lng, lnb, w, bt)


def sg_bwd(z, lng, lnb, w, bt, dy, name):
    t = z.shape[0]

    def body(z_ref, lng_ref, lnb_ref, w_ref, bt_ref, dy_ref, dz_ref, dlng_ref, dlnb_ref, dw_ref, dbt_ref):
        ws = [w_ref[h] for h in range(SG_HEADS)]
        tot = None
        for k in range(SG_ROWS // SG_CHUNK):
            r = pl.ds(k * SG_CHUNK, SG_CHUNK)
            _, vj = jax.vjp(_sg_chunk, z_ref[r, :D_SG], z_ref[r, D_SG:2 * D_SG], z_ref[r, 2 * D_SG:], lng_ref[...],
                            lnb_ref[...], ws, bt_ref[...])
            du, dv, dgate, dlng, dlnb, dws, dbt = vj(dy_ref[r, :])
            dz_ref[r, :D_SG] = du.astype(_MXU)
            dz_ref[r, D_SG:2 * D_SG] = dv.astype(_MXU)
            dz_ref[r, 2 * D_SG:] = dgate.astype(_MXU)
            part = [dlng, dlnb, dbt] + list(dws)
            tot = part if tot is None else [a + b for a, b in zip(tot, part)]
        first = pl.program_id(0) == 0
        _acc(dlng_ref, tot[0], first)
        _acc(dlnb_ref, tot[1], first)
        _acc(dbt_ref, tot[2], first)
        for h in range(SG_HEADS):
            @pl.when(first)
            def _():
                dw_ref[h] = tot[3 + h]

            @pl.when(jnp.logical_not(first))
            def _():
                dw_ref[h] += tot[3 + h]

    return pl.pallas_call(
        body, name=name, grid=(t // SG_ROWS,),
        in_specs=[_rows(SG_ROWS, 3 * D_SG), _full((1, D_SG)), _full((1, D_SG)),
                  _full((SG_HEADS, SG_CHUNK, SG_CHUNK)), _full((SG_CHUNK, LANE)), _rows(SG_ROWS, D_SG)],
        out_specs=[_rows(SG_ROWS, 3 * D_SG), _full((1, D_SG)), _full((1, D_SG)),
                   _full((SG_HEADS, SG_CHUNK, SG_CHUNK)), _full((SG_CHUNK, LANE))],
        out_shape=[_sds((t, 3 * D_SG), _MXU), _sds((1, D_SG)), _sds((1, D_SG)), _sds((SG_HEADS, SG_CHUNK, SG_CHUNK)),
                   _sds((SG_CHUNK, LANE))],
        compiler_params=_cp(1),
    )(z, lng, lnb, w, bt, dy)


def add_pairs(a_list, b_list, name):
    n = len(a_list)

    def body(*refs):
        for a_ref, b_ref, o_ref in zip(refs[:n], refs[n:2 * n], refs[2 * n:]):
            o_ref[...] = (a_ref[...].astype(f32) + b_ref[...].astype(f32)).astype(o_ref.dtype)

    return pl.pallas_call(
        body, name=name, out_shape=[_sds(a.shape, a.dtype) for a in a_list],
        compiler_params=pltpu.CompilerParams(vmem_limit_bytes=VMEM_BIG),
    )(*a_list, *b_list)


def _adamw(g, w, m, v):
    nm = ADAM_B1 * m + (1.0 - ADAM_B1) * g
    nv = ADAM_B2 * v + (1.0 - ADAM_B2) * jnp.square(g)
    m_hat = nm / (1.0 - ADAM_B1 ** ADAM_STEP)
    v_hat = nv / (1.0 - ADAM_B2 ** ADAM_STEP)
    return -ADAM_LR * (m_hat / (jnp.sqrt(v_hat) + ADAM_EPS) + ADAM_WD * w), nm, nv


def sum_parts(half, recv, name):
    _, r, c = recv.shape
    tr = 256 if r % 256 == 0 else r

    def body(half_ref, recv_ref, g_ref):
        g = recv_ref[0].astype(f32)
        for k in range(1, N_CHIPS):
            g = g + recv_ref[k].astype(f32)
        g_ref[...] = g

    return pl.pallas_call(
        body, name=name,
        grid_spec=pltpu.PrefetchScalarGridSpec(
            num_scalar_prefetch=1, grid=(r // tr,),
            in_specs=[pl.BlockSpec((N_CHIPS, tr, c), lambda i, h: (0, i, 0))],
            out_specs=pl.BlockSpec((None, tr, c), lambda i, h: (h[0], i, 0))),
        out_shape=_sds((2, r, c)),
        compiler_params=_cp(1, VMEM_BIG),
    )(half, recv)


def adamw(g, w, m, v, name):
    _, r, c = w.shape
    tr = 256 if r % 256 == 0 else r

    def body(g_ref, w_ref, m_ref, v_ref, d_ref, nm_ref, nv_ref):
        d_ref[...], nm_ref[...], nv_ref[...] = _adamw(g_ref[...], w_ref[...], m_ref[...], v_ref[...])

    blk = pl.BlockSpec((None, tr, c), lambda l, i: (l, i, 0))
    return pl.pallas_call(
        body, name=name, grid=(2, r // tr), in_specs=[blk] * 4, out_specs=[blk] * 3, out_shape=[_sds((2, r, c))] * 3,
        compiler_params=_cp(2, VMEM_BIG),
    )(g, w, m, v)


def sum_parts_small(chip, parts, sums, name):
    n = len(sums)

    def body(chip_ref, *refs):
        for part, own, out in zip(refs[:n], refs[n:2 * n], refs[2 * n:]):
            g = jnp.where(chip_ref[0] == 0, own[...], part[0])
            for q in range(1, N_CHIPS):
                g = g + jnp.where(chip_ref[0] == q, own[...], part[q])
            out[...] = g

    vmem = pl.BlockSpec(memory_space=pltpu.VMEM)
    return pl.pallas_call(
        body, name=name, in_specs=[pl.BlockSpec(memory_space=pltpu.SMEM)] + [vmem] * (2 * n), out_specs=[vmem] * n,
        out_shape=[_sds(s.shape) for s in sums], compiler_params=pltpu.CompilerParams(vmem_limit_bytes=VMEM_BIG),
    )(chip, *parts, *sums)


def adamw_small(gs, ws, ms, vs, name):
    n = len(ws)

    def body(*refs):
        ins, outs = refs[:4 * n], refs[4 * n:]
        for k in range(n):
            outs[k][...], outs[n + k][...], outs[2 * n + k][...] = _adamw(
                ins[k][...], ins[n + k][...], ins[2 * n + k][...], ins[3 * n + k][...])

    outs = pl.pallas_call(
        body, name=name, out_shape=[_sds(w.shape) for w in ws] * 3,
        compiler_params=pltpu.CompilerParams(vmem_limit_bytes=VMEM_BIG),
    )(*gs, *ws, *ms, *vs)
    return [outs[j * n:(j + 1) * n] for j in range(3)]


_ANY = pl.BlockSpec(memory_space=pltpu.HBM)
_MESH = pl.DeviceIdType.MESH


def _flip(v, bit):
    return 1 - v if bit else v


_CHIP_RELS = ((1, 0), (0, 1), (1, 1))


def _piece(ref, kind, q):
    if kind[0] == "slot":
        return ref.at[q]
    if kind[0] == "all":
        return ref
    _, axis, n = kind
    return ref.at[(slice(None),) * axis + (pl.ds(q * n, n),)]


def _piece_shape(shape, kind):
    if kind[0] == "slot":
        return tuple(shape[1:])
    if kind[0] == "all":
        return tuple(shape)
    _, axis, n = kind
    return tuple(shape[:axis]) + (n,) + tuple(shape[axis + 1:])


def gather_weights(shards, kinds, name):
    n = len(shards)

    def out_shape(s, kind):
        if kind[0] == "slot":
            return (N_CHIPS,) + tuple(s.shape)
        _, axis, w = kind
        return tuple(s.shape[:axis + 1]) + (N_CHIPS * w,) + tuple(s.shape[axis + 2:])

    def place(o_ref, kind, q, layer):
        if kind[0] == "slot":
            return o_ref.at[q, layer]
        return _piece(o_ref.at[layer], kind, q)

    def body(*refs):
        s_refs, o_refs = refs[:n], refs[n:2 * n]
        send_sems, recv_sems, fwd_send_sems, fwd_recv_sems = refs[2 * n:]
        x, y, c = lax.axis_index("x"), lax.axis_index("y"), lax.axis_index("c")
        mine = 2 * x + y
        sends, arrivals, forwards, fwd_arrivals = [], [], [], []
        for r, (fx, fy) in enumerate(_CHIP_RELS):
            px, py = _flip(x, fx), _flip(y, fy)
            peer = 2 * px + py
            for k in range(n):
                s = r * n + k
                sends.append(pltpu.make_async_remote_copy(
                    src_ref=s_refs[k].at[c], dst_ref=place(o_refs[k], kinds[k], mine, c), send_sem=send_sems.at[s],
                    recv_sem=recv_sems.at[s], device_id=(px, py, c), device_id_type=_MESH))
                arrivals.append(pltpu.make_async_remote_copy(
                    src_ref=s_refs[k].at[c], dst_ref=place(o_refs[k], kinds[k], peer, c), send_sem=send_sems.at[s],
                    recv_sem=recv_sems.at[s], device_id=(px, py, c), device_id_type=_MESH))
                block = place(o_refs[k], kinds[k], peer, c)
                forwards.append(pltpu.make_async_remote_copy(
                    src_ref=block, dst_ref=block, send_sem=fwd_send_sems.at[s], recv_sem=fwd_recv_sems.at[s],
                    device_id=(x, y, 1 - c), device_id_type=_MESH))
                other = place(o_refs[k], kinds[k], peer, 1 - c)
                fwd_arrivals.append(pltpu.make_async_remote_copy(
                    src_ref=other, dst_ref=other, send_sem=fwd_send_sems.at[s], recv_sem=fwd_recv_sems.at[s],
                    device_id=(x, y, 1 - c), device_id_type=_MESH))
        for cp in sends:
            cp.start()
        for arrived, fwd in zip(arrivals, forwards):
            arrived.wait_recv()
            fwd.start()
        for cp in fwd_arrivals:
            cp.wait_recv()
        for cp in sends + forwards:
            cp.wait_send()

    m = len(_CHIP_RELS) * n
    return pl.pallas_call(
        body, name=name, in_specs=[_ANY] * n, out_specs=[_ANY] * n,
        out_shape=[_sds(out_shape(s, k), s.dtype) for s, k in zip(shards, kinds)],
        scratch_shapes=[pltpu.SemaphoreType.DMA((m,))] * 4,
    )(*shards)


def _owned_by(owners, side):
    return [k for k, o in enumerate(owners) if o == side]


def exchange_halves(gs, smalls, owners, name):
    n, ns = len(gs), len(smalls)

    def body(*refs):
        g_refs, s_refs = refs[:n], refs[n:n + ns]
        got_refs, sgot_refs = refs[n + ns:2 * n + ns], refs[2 * n + ns:2 * (n + ns)]
        send_sems, recv_sems = refs[2 * (n + ns):]
        x, y, c = lax.axis_index("x"), lax.axis_index("y"), lax.axis_index("c")
        sibling = (x, y, 1 - c)
        swaps = [pltpu.make_async_remote_copy(
            src_ref=g_refs[k].at[1 - c], dst_ref=got_refs[k], send_sem=send_sems.at[k], recv_sem=recv_sems.at[k],
            device_id=sibling, device_id_type=_MESH) for k in range(n)]
        gives = [pltpu.make_async_remote_copy(
            src_ref=s_refs[k], dst_ref=sgot_refs[k], send_sem=send_sems.at[n + k], recv_sem=recv_sems.at[n + k],
            device_id=sibling, device_id_type=_MESH) for k in range(ns)]
        for cp in swaps:
            cp.start()
        for side in (0, 1):
            @pl.when(c == 1 - side)
            def _():
                for k in _owned_by(owners, side):
                    gives[k].start()
        for cp in swaps:
            cp.wait()
        for side in (0, 1):
            @pl.when(c == 1 - side)
            def _():
                for k in _owned_by(owners, side):
                    gives[k].wait_send()

            @pl.when(c == side)
            def _():
                for k in _owned_by(owners, side):
                    gives[k].wait_recv()

    outs = pl.pallas_call(
        body, name=name, in_specs=[_ANY] * (n + ns), out_specs=[_ANY] * (n + ns),
        out_shape=[_sds(g.shape[1:], g.dtype) for g in gs] + [_sds(s.shape, s.dtype) for s in smalls],
        scratch_shapes=[pltpu.SemaphoreType.DMA((n + ns,)), pltpu.SemaphoreType.DMA((n + ns,))],
    )(*gs, *smalls)
    return outs[:n], outs[n:]


def reduce_to_chips(ts, kinds, smalls, owners, name):
    n, ns = len(ts), len(smalls)

    def body(*refs):
        t_refs, s_refs = refs[:n], refs[n:n + ns]
        o_refs, so_refs = refs[n + ns:2 * n + ns], refs[2 * n + ns:2 * (n + ns)]
        send_sems, recv_sems = refs[2 * (n + ns):]
        x, y, c = lax.axis_index("x"), lax.axis_index("y"), lax.axis_index("c")
        mine = 2 * x + y
        sends, arrivals, small_sends, small_arrivals = [], [], [], []
        for r, (fx, fy) in enumerate(_CHIP_RELS):
            px, py = _flip(x, fx), _flip(y, fy)
            peer = 2 * px + py
            for k in range(n + ns):
                s = r * (n + ns) + k
                if k < n:
                    src, dst = _piece(t_refs[k], kinds[k], peer), o_refs[k]
                else:
                    src, dst = s_refs[k - n], so_refs[k - n]
                go = pltpu.make_async_remote_copy(
                    src_ref=src, dst_ref=dst.at[mine], send_sem=send_sems.at[s], recv_sem=recv_sems.at[s],
                    device_id=(px, py, c), device_id_type=_MESH)
                come = pltpu.make_async_remote_copy(
                    src_ref=src, dst_ref=dst.at[peer], send_sem=send_sems.at[s], recv_sem=recv_sems.at[s],
                    device_id=(px, py, c), device_id_type=_MESH)
                (sends if k < n else small_sends).append(go)
                (arrivals if k < n else small_arrivals).append(come)

        def owned(copies, side):
            return [cp for j, cp in enumerate(copies) if owners[j % ns] == side]

        for cp in sends:
            cp.start()
        for side in (0, 1):
            @pl.when(c == side)
            def _():
                for cp in owned(small_sends, side):
                    cp.start()
        for cp in arrivals:
            cp.wait_recv()
        for cp in sends:
            cp.wait_send()
        for side in (0, 1):
            @pl.when(c == side)
            def _():
                for cp in owned(small_arrivals, side):
                    cp.wait_recv()
                for cp in owned(small_sends, side):
                    cp.wait_send()

    m = len(_CHIP_RELS) * (n + ns)
    outs = pl.pallas_call(
        body, name=name, in_specs=[_ANY] * (n + ns), out_specs=[_ANY] * (n + ns),
        out_shape=[_sds((N_CHIPS,) + _piece_shape(t.shape, k), t.dtype) for t, k in zip(ts, kinds)]
        + [_sds((N_CHIPS,) + s.shape, s.dtype) for s in smalls],
        scratch_shapes=[pltpu.SemaphoreType.DMA((m,)), pltpu.SemaphoreType.DMA((m,))],
    )(*ts, *smalls)
    return outs[:n], outs[n:]


def share_halves(rs, smalls, owners, name):
    n, ns = len(rs), len(smalls)

    def body(*refs):
        o_refs, so_refs = refs[n + ns:2 * n + ns], refs[2 * n + ns:2 * (n + ns)]
        send_sems, recv_sems = refs[2 * (n + ns):]
        x, y, c = lax.axis_index("x"), lax.axis_index("y"), lax.axis_index("c")
        sibling = (x, y, 1 - c)
        swaps = [pltpu.make_async_remote_copy(
            src_ref=o_refs[k].at[c], dst_ref=o_refs[k].at[c], send_sem=send_sems.at[k], recv_sem=recv_sems.at[k],
            device_id=sibling, device_id_type=_MESH) for k in range(n)]
        arrivals = [pltpu.make_async_remote_copy(
            src_ref=o_refs[k].at[c], dst_ref=o_refs[k].at[1 - c], send_sem=send_sems.at[k], recv_sem=recv_sems.at[k],
            device_id=sibling, device_id_type=_MESH) for k in range(n)]
        gives = [pltpu.make_async_remote_copy(
            src_ref=so_refs[k], dst_ref=so_refs[k], send_sem=send_sems.at[n + k], recv_sem=recv_sems.at[n + k],
            device_id=sibling, device_id_type=_MESH) for k in range(ns)]
        for cp in swaps:
            cp.start()
        for side in (0, 1):
            @pl.when(c == side)
            def _():
                for k in _owned_by(owners, side):
                    gives[k].start()
        for cp in arrivals:
            cp.wait_recv()
        for cp in swaps:
            cp.wait_send()
        for side in (0, 1):
            @pl.when(c == side)
            def _():
                for k in _owned_by(owners, side):
                    gives[k].wait_send()

            @pl.when(c == 1 - side)
            def _():
                for k in _owned_by(owners, side):
                    gives[k].wait_recv()

    outs = pl.pallas_call(
        body, name=name, in_specs=[_ANY] * (n + ns), out_specs=[_ANY] * (n + ns),
        out_shape=[_sds(r.shape, r.dtype) for r in list(rs) + list(smalls)],
        input_output_aliases={k: k for k in range(n + ns)},
        scratch_shapes=[pltpu.SemaphoreType.DMA((n + ns,)), pltpu.SemaphoreType.DMA((n + ns,))],
    )(*rs, *smalls)
    return outs[:n], outs[n:]


def _small_view(a):
    if a.size < 8 * LANE:
        return jnp.pad(a.reshape(-1), (0, 8 * LANE - a.size)).reshape(8, LANE)
    if a.ndim == 1:
        return a.reshape(1, a.shape[0])
    if a.ndim == 4 and a.shape[-1] < LANE:
        return a.reshape(a.shape[0], a.shape[1], a.shape[2] * a.shape[3])
    return a


def _permuted_from_shards(shards):
    parts = []
    for lo, hi in GROUP_COLS:
        for q in range(N_CHIPS):
            a, b = max(lo, q * SHARD_COLS), min(hi, (q + 1) * SHARD_COLS)
            if a < b:
                parts.append(shards[q][..., a - q * SHARD_COLS:b - q * SHARD_COLS])
    pad = jnp.zeros(shards[0].shape[:-1] + (D_IN_PAD - D_IN,), shards[0].dtype)
    return jnp.concatenate(parts + [pad], axis=-1)


def _shards_from_groups(groups):
    in_order = sorted(range(len(GROUP_COLS)), key=lambda j: GROUP_COLS[j][0])
    shards = []
    for q in range(N_CHIPS):
        parts = []
        for j in in_order:
            lo, hi = GROUP_COLS[j]
            a, b = max(lo, q * SHARD_COLS), min(hi, (q + 1) * SHARD_COLS)
            if a < b:
                parts.append(groups[j][..., a - lo:b - lo])
        shards.append(jnp.concatenate(parts, axis=-1))
    return shards


def _expand_b(b):
    eye = jnp.eye(SSM_GROUPS, dtype=b.dtype)
    return jnp.einsum("gnc,gh->gchn", b, eye).reshape(D_SSM, N_STATE)


def _extract_b(e):
    return jnp.einsum("gcgn->gnc", e.reshape(SSM_GROUPS, SSM_GROUP, SSM_GROUPS, SSM_STATE))


def _expand_c(c):
    eye = jnp.eye(SSM_GROUPS, dtype=c.dtype)
    return jnp.einsum("gcn,gh->gnhc", c, eye).reshape(N_STATE, D_SSM)


def _extract_c(e):
    return jnp.einsum("gngc->gcn", e.reshape(SSM_GROUPS, SSM_STATE, SSM_GROUPS, SSM_GROUP))


def _lane_row(v):
    return jnp.pad(v, (0, LANE - v.shape[0])).reshape(1, LANE)


def _layer_params(w, l):
    return dict(
        norm_g=w["norm_g"][l][None], win=w["w_in_perm"][l], wout=w["w_out"][l].astype(_MXU),
        pg=w["ple_norm_g"][l][None], wgate=w["w_ple_gate"][l].astype(_MXU), wple=w["w_ple"][l].astype(_MXU),
        are=w["ssm_a_re"][l].reshape(1, N_STATE), aim=w["ssm_a_im"][l].reshape(1, N_STATE),
        ls=jnp.repeat(w["ssm_log_step"][l], SSM_STATE).reshape(1, N_STATE),
        bre=_expand_b(w["ssm_b_re"][l]), bim=_expand_b(w["ssm_b_im"][l]),
        cr=_expand_c(w["ssm_c_re"][l]), ci=_expand_c(w["ssm_c_im"][l]),
        dr=w["ssm_d"][l].reshape(1, D_SSM), wglu=w["ssm_w_glu"][l].astype(f32), bglu=w["ssm_b_glu"][l][None],
        convw=w["dn_conv_w"][l], alog=_lane_row(w["dn_a_log"][l]), dtb=_lane_row(w["dn_dt_bias"][l]),
        ng=w["dn_norm_g"][l][None],
        lng=w["sg_ln_g"][l][None], lnb=w["sg_ln_b"][l][None], sgw=w["sg_w"][l],
        bt=jnp.pad(w["sg_b"][l].T, ((0, 0), (0, LANE - SG_HEADS))),
    )


def _layer_fwd(x, p, lp, nb, tag):
    seq = x.shape[0] // nb
    h, zs, zq, zg, zsg, zab = in_fwd(x, lp["norm_g"], lp["win"], f"in_fwd{tag}")
    prep = s5_prep_fwd(lp["are"], lp["aim"], lp["ls"], lp["bre"], lp["bim"], f"s5_prep_fwd{tag}")
    s5p = tuple(prep) + (lp["cr"], lp["ci"], lp["dr"], lp["wglu"], lp["bglu"])
    ys, hs = s5_fwd(zs, s5p, nb, f"s5_fwd{tag}")
    qkv, loc, inv = dn_front_fwd(zq, lp["convw"], zab, lp["alog"], lp["dtb"], seq, f"dn_front_fwd{tag}")
    yd, ss = dn_scan_fwd(loc, zg, lp["ng"], nb, f"dn_scan_fwd{tag}")
    yg = sg_fwd(zsg, lp["lng"], lp["lnb"], lp["sgw"], lp["bt"], f"sg_fwd{tag}")
    x2, x1, y, hn = post_fwd(x, ys, yd, yg, p, lp["wout"], lp["pg"], lp["wgate"], lp["wple"], f"post_fwd{tag}")
    saved = dict(x=x, h=h, zs=zs, zq=zq, zg=zg, zsg=zsg, zab=zab, s5p=s5p, hs=hs, qkv=qkv, loc=loc, inv=inv, ss=ss, x1=x1, y=y, hn=hn, p=p)
    return x2, saved


def _layer_bwd(dx2, sv, lp, nb, tag):
    seq = dx2.shape[0] // nb
    dx1, dgp, dpp, dys, dyd, dyg, dpg = post_bwd(dx2, sv["x1"], sv["hn"], sv["p"], lp["wout"], lp["pg"], lp["wgate"],
                                                 lp["wple"], f"post_bwd{tag}")
    g = {}
    g["w_out"] = wgrad(sv["y"], dx1, f"wgrad_out{tag}")
    g["w_ple_gate"] = wgrad(sv["hn"], dgp, f"wgrad_gate{tag}")
    g["w_ple"] = wgrad(sv["p"], dpp, f"wgrad_ple{tag}")
    g["ple_norm_g"] = dpg[0]
    dzsg, dlng, dlnb, dsgw, dbt = sg_bwd(sv["zsg"], lp["lng"], lp["lnb"], lp["sgw"], lp["bt"], dyg, f"sg_bwd{tag}")
    g["sg_ln_g"], g["sg_ln_b"], g["sg_w"], g["sg_b"] = dlng[0], dlnb[0], dsgw, dbt[:, :SG_HEADS].T
    dloc, dzg, dng = dn_scan_bwd(sv["loc"], sv["zg"], lp["ng"], sv["ss"], dyd, nb, f"dn_scan_bwd{tag}")
    dqkv, dzab, dalog, ddtb = dn_local_bwd(sv["qkv"], sv["zab"], lp["alog"], lp["dtb"], sv["inv"], dloc,
                                           f"dn_local_bwd{tag}")
    dzq, dconv = dn_pre_bwd(sv["zq"], lp["convw"], dqkv, seq, f"dn_pre_bwd{tag}")
    g["dn_conv_w"], g["dn_a_log"], g["dn_dt_bias"], g["dn_norm_g"] = dconv, dalog[0, :DN_HEADS], ddtb[0, :DN_HEADS], dng[0]
    s5out = s5_bwd(sv["zs"], sv["s5p"], sv["hs"], dys, nb, f"s5_bwd{tag}")
    dzs, dprep, (dcr, dci, ddr, dwglu, dbglu) = s5out[0], s5out[1:1 + S5_PREPARED], s5out[1 + S5_PREPARED:]
    dare, daim, dls, dbre, dbim = s5_prep_bwd(lp["are"], lp["aim"], lp["ls"], lp["bre"], lp["bim"], dprep,
                                              f"s5_prep_bwd{tag}")
    g["ssm_a_re"] = dare.reshape(SSM_GROUPS, SSM_STATE)
    g["ssm_a_im"] = daim.reshape(SSM_GROUPS, SSM_STATE)
    g["ssm_log_step"] = dls.reshape(SSM_GROUPS, SSM_STATE).sum(axis=1)
    g["ssm_b_re"], g["ssm_b_im"] = _extract_b(dbre), _extract_b(dbim)
    g["ssm_c_re"], g["ssm_c_im"] = _extract_c(dcr), _extract_c(dci)
    g["ssm_d"] = ddr.reshape(SSM_GROUPS, SSM_GROUP)
    g["ssm_w_glu"], g["ssm_b_glu"] = dwglu, dbglu[0]
    dzs_all = (dzs, dzq, dzg, dzsg, dzab)
    dx, dng_in = in_bwd(sv["x"], lp["norm_g"], lp["win"], dzs_all, dx1, f"in_bwd{tag}")
    g["w_in_pieces"] = [wgrad(sv["h"], dz, f"wgrad_in{k}{tag}") for k, dz in enumerate(dzs_all)]
    g["norm_g"] = dng_in[0]
    return dx, g


def _local_step(x, p, target, w, nb):
    lps = [_layer_params(w, l) for l in range(DEPTH)]
    saved = []
    for l in range(DEPTH):
        x, sv = _layer_fwd(x, p[l], lps[l], nb, f"_l{l}")
        saved.append(sv)
    loss_blk, dx, dfg = loss_fwd_bwd(x, w["final_norm_g"][None], target, "loss")
    grads = [None] * DEPTH
    for l in reversed(range(DEPTH)):
        dx, grads[l] = _layer_bwd(dx, saved[l], lps[l], nb, f"_l{l}")
    out = {k: jnp.stack([grads[l][k] for l in range(DEPTH)]) for k in grads[0] if k != "w_in_pieces"}
    out["w_in_pieces"] = [grads[l]["w_in_pieces"] for l in range(DEPTH)]
    out["final_norm_g"] = dfg[0]
    return loss_blk[0, 0], dx, out


def kernel(x, p, norm_g, w_in, ssm_a_re, ssm_a_im, ssm_b_re, ssm_b_im, ssm_c_re, ssm_c_im, ssm_d, ssm_log_step, ssm_w_glu, ssm_b_glu, dn_conv_w, dn_a_log, dn_dt_bias, dn_norm_g, sg_ln_g, sg_ln_b, sg_w, sg_b, w_out, ple_norm_g, w_ple_gate, w_ple, final_norm_g, loss_target, m_norm_g, m_w_in, m_ssm_a_re, m_ssm_a_im, m_ssm_b_re, m_ssm_b_im, m_ssm_c_re, m_ssm_c_im, m_ssm_d, m_ssm_log_step, m_ssm_w_glu, m_ssm_b_glu, m_dn_conv_w, m_dn_a_log, m_dn_dt_bias, m_dn_norm_g, m_sg_ln_g, m_sg_ln_b, m_sg_w, m_sg_b, m_w_out, m_ple_norm_g, m_w_ple_gate, m_w_ple, m_final_norm_g, v_norm_g, v_w_in, v_ssm_a_re, v_ssm_a_im, v_ssm_b_re, v_ssm_b_im, v_ssm_c_re, v_ssm_c_im, v_ssm_d, v_ssm_log_step, v_ssm_w_glu, v_ssm_b_glu, v_dn_conv_w, v_dn_a_log, v_dn_dt_bias, v_dn_norm_g, v_sg_ln_g, v_sg_ln_b, v_sg_w, v_sg_b, v_w_out, v_ple_norm_g, v_w_ple_gate, v_w_ple, v_final_norm_g):
    args = locals()
    w = {n: args[n] for n in WEIGHTS}
    m = {n: args["m_" + n] for n in WEIGHTS}
    v = {n: args["v_" + n] for n in WEIGHTS}
    nb, seq = x.shape[0], x.shape[1]
    t = nb * seq

    full = _gather_full(w)
    loss_local, dx, grads = _local_step(x.reshape(t, D_MODEL), p.reshape(DEPTH, t, D_PLE),
                                        loss_target.reshape(t, D_MODEL), full, nb)
    outs, loss = _reduce_and_update(grads, w, m, v, loss_local)
    return (loss, dx.reshape(nb, seq, D_MODEL), *[outs[0][n] for n in WEIGHTS], *[outs[1][n] for n in WEIGHTS],
            *[outs[2][n] for n in WEIGHTS], *[outs[3][n] for n in WEIGHTS])


def _gather_full(w):
    sh_names = [n for n, _ in SHARDED]
    shards = [w[n] if n == "dn_conv_w" else w[n].astype(_COMM) for n in sh_names]
    gathered = gather_weights(shards, [k for _, k in SHARDED], "gather_weights")
    chip = 2 * lax.axis_index("x") + lax.axis_index("y")
    full = {n: w[n] for n in REPLICATED}
    for (n, kind), shard, got in zip(SHARDED, shards, gathered):
        if kind[0] == "slot":
            full[n] = lax.dynamic_update_index_in_dim(got, shard, chip, 0)
        else:
            full[n] = lax.dynamic_update_slice_in_dim(got, shard, chip * kind[2], axis=kind[1] + 1)
    slots = full.pop("w_in")
    full["w_in_perm"] = _permuted_from_shards([slots[q] for q in range(N_CHIPS)]).astype(_MXU)
    return full


def _reduce_and_update(grads, w, m, v, loss_local):
    sh_names = [n for n, _ in SHARDED]
    sh_kinds = [k for _, k in SHARDED]
    owners = [SMALL_OWNER[n] for n in REPLICATED + ("loss",)]

    def small_views(d):
        return [_small_view(d[n]) for n in REPLICATED]

    grads["w_in"] = jnp.stack([jnp.stack(_shards_from_groups(pieces)) for pieces in grads["w_in_pieces"]])
    gs = [grads[n] if n == "dn_conv_w" else grads[n].astype(_COMM) for n in sh_names]
    sm = small_views(grads) + [_small_view(loss_local.reshape(1))]
    core = lax.axis_index("c")
    chip = 2 * lax.axis_index("x") + lax.axis_index("y")
    got, sm_got = exchange_halves(gs, sm, owners, "exchange_halves")
    sums = add_pairs([lax.dynamic_index_in_dim(g, core, 0, keepdims=False) for g in gs] + sm, list(got) + list(sm_got),
                     "add_halves")
    sums, sm_sums = sums[:len(gs)], sums[len(gs):]
    parts, sm_parts = reduce_to_chips(sums, sh_kinds, sm_sums, owners, "reduce_to_chips")
    parts = list(parts)
    for k, (kind, total) in enumerate(zip(sh_kinds, sums)):
        if kind[0] == "slot":
            own = lax.dynamic_index_in_dim(total, chip, 0, keepdims=False)
        else:
            own = lax.dynamic_slice_in_dim(total, chip * kind[2], kind[2], axis=kind[1])
        parts[k] = lax.dynamic_update_index_in_dim(parts[k], own, chip, 0)
    half = core.astype(jnp.int32).reshape(1)
    totals = [sum_parts(half, part, f"sum_{n}") for n, part in zip(sh_names, parts)]
    sm_totals = sum_parts_small(chip.astype(jnp.int32).reshape(1), sm_parts, sm_sums, "sum_replicated")
    g_big, g_small = share_halves(totals, sm_totals, owners, "share_halves")
    outs = [dict(zip(sh_names, g_big)), {}, {}, {}]
    for n, g in zip(sh_names, g_big):
        outs[1][n], outs[2][n], outs[3][n] = adamw(g, w[n], m[n], v[n], f"adamw_{n}")
    small_results = [g_small[:-1]] + adamw_small(g_small[:-1], small_views(w), small_views(m), small_views(v),
                                                 "adamw_replicated")
    for j in range(4):
        for n, r in zip(REPLICATED, small_results[j]):
            outs[j][n] = r.reshape(-1)[:w[n].size].reshape(w[n].shape)
    return outs, g_small[-1][0, 0]
```

```python
import functools

import jax
import jax.numpy as jnp
from jax import lax
from jax.experimental import pallas as pl
from jax.experimental.pallas import tpu as pltpu

f32 = jnp.float32
bf16 = jnp.bfloat16

_MXU = bf16
_COMM = bf16
HIGH = lax.Precision.HIGH

D_MODEL = 1024
DEPTH = 2
D_PLE = 256
D_SSM = 256
D_DN = 512
D_SG = 256
SSM_GROUPS = 16
SSM_GROUP = 16
SSM_STATE = 64
N_STATE = SSM_GROUPS * SSM_STATE
DN_HEADS = 4
DN_HEAD_DIM = 128
DN_CONV = 4
DN_HALO = 16
DN_CHUNK = 64
SG_HEADS = 4
SG_HEAD_DIM = 64
SG_CHUNK = 128
S5_CHUNK = 1024
S5_GROUP_ROWS = 8
EPS = 1e-6
D_IN = 3336
D_IN_PAD = 3456
LANE = 128

ADAM_LR = 0.001
ADAM_B1 = 0.9
ADAM_B2 = 0.999
ADAM_EPS = 1e-08
ADAM_WD = 0.01
ADAM_STEP = 10

N_CHIPS = 4
N_DEV = 8

Z_COLS = ((0, 512), (512, 2048), (2048, 2560), (2560, 3328), (3328, 3456))

GROUP_COLS = ((0, 512), (512, 2048), (2056, 2568), (2568, 3336), (2048, 2056))
SHARD_COLS = D_IN // 4

SHARDED = (("w_in", ("slot",)), ("ssm_w_glu", ("win", 0, 64)), ("dn_conv_w", ("win", 1, 384)),
           ("w_out", ("win", 0, 256)), ("w_ple_gate", ("win", 0, 256)), ("w_ple", ("win", 1, 256)))
REPLICATED = ("norm_g", "ssm_a_re", "ssm_a_im", "ssm_b_re", "ssm_b_im", "ssm_c_re", "ssm_c_im", "ssm_d",
              "ssm_log_step", "ssm_b_glu", "dn_a_log", "dn_dt_bias", "dn_norm_g", "sg_ln_g", "sg_ln_b", "sg_w",
              "sg_b", "ple_norm_g", "final_norm_g")
SMALL_OWNER = {n: int(n.startswith("ssm_")) for n in REPLICATED + ("loss",)}
WEIGHTS = ("norm_g", "w_in", "ssm_a_re", "ssm_a_im", "ssm_b_re", "ssm_b_im", "ssm_c_re", "ssm_c_im", "ssm_d",
           "ssm_log_step", "ssm_w_glu", "ssm_b_glu", "dn_conv_w", "dn_a_log", "dn_dt_bias", "dn_norm_g", "sg_ln_g",
           "sg_ln_b", "sg_w", "sg_b", "w_out", "ple_norm_g", "w_ple_gate", "w_ple", "final_norm_g")

VMEM_BIG = 56 * 1024 * 1024


def _mm(a, b):
    return jnp.dot(a.astype(_MXU), b.astype(_MXU), preferred_element_type=f32)


def _mm_nt(a, b):
    return lax.dot_general(a.astype(_MXU), b.astype(_MXU), (((1,), (1,)), ((), ())), preferred_element_type=f32)


def _mm_tn(a, b):
    return lax.dot_general(a.astype(_MXU), b.astype(_MXU), (((0,), (0,)), ((), ())), preferred_element_type=f32)


@jax.custom_vjp
def bdot(a, b):
    return _mm(a, b)


def _bdot_fwd(a, b):
    return _mm(a, b), (a, b)


def _bdot_bwd(res, g):
    a, b = res
    return _mm_nt(g, b).astype(a.dtype), _mm_tn(a, g).astype(b.dtype)


bdot.defvjp(_bdot_fwd, _bdot_bwd)


@jax.custom_vjp
def bdot_nt(a, b):
    return _mm_nt(a, b)


def _bdot_nt_fwd(a, b):
    return _mm_nt(a, b), (a, b)


def _bdot_nt_bwd(res, g):
    a, b = res
    return _mm(g, b).astype(a.dtype), _mm_tn(g, a).astype(b.dtype)


bdot_nt.defvjp(_bdot_nt_fwd, _bdot_nt_bwd)


@jax.custom_vjp
def bdot_tn(a, b):
    return _mm_tn(a, b)


def _bdot_tn_fwd(a, b):
    return _mm_tn(a, b), (a, b)


def _bdot_tn_bwd(res, g):
    a, b = res
    return _mm_nt(b, g).astype(a.dtype), _mm(a, g).astype(b.dtype)


bdot_tn.defvjp(_bdot_tn_fwd, _bdot_tn_bwd)


def hdot(a, b):
    return jnp.dot(a, b, precision=HIGH, preferred_element_type=f32)


def _unit_lower_inverses(ms):
    n = ms[0].shape[0]
    eye = (lax.broadcasted_iota(jnp.int32, (n, n), 0) == lax.broadcasted_iota(jnp.int32, (n, n), 1)).astype(f32)
    pw = [-m for m in ms]
    inv = [eye + p for p in pw]
    for _ in range(n.bit_length() - 2):
        pw = [hdot(p, p) for p in pw]
        inv = [a + hdot(a, p) for a, p in zip(inv, pw)]
    return inv


@jax.custom_vjp
def solve_unit_lower(ms, rhs, inv):
    return [hdot(a, r) for a, r in zip(inv, rhs)]


def _solve_unit_lower_fwd(ms, rhs, inv):
    xs = [hdot(a, r) for a, r in zip(inv, rhs)]
    return xs, (inv, xs)


def _solve_unit_lower_bwd(res, gs):
    inv, xs = res
    d_rhs = [lax.dot_general(a, g, (((0,), (0,)), ((), ())), precision=HIGH, preferred_element_type=f32)
             for a, g in zip(inv, gs)]
    d_ms = [-lax.dot_general(d, x, (((1,), (1,)), ((), ())), precision=HIGH, preferred_element_type=f32)
            for d, x in zip(d_rhs, xs)]
    return d_ms, d_rhs, [jnp.zeros_like(a) for a in inv]


solve_unit_lower.defvjp(_solve_unit_lower_fwd, _solve_unit_lower_bwd)


@functools.partial(jax.custom_vjp, nondiff_argnums=(1,))
def roll_rows(x, k):
    return pltpu.roll(x, k, 0)


def _roll_rows_fwd(x, k):
    return pltpu.roll(x, k, 0), None


def _roll_rows_bwd(k, _, g):
    return (pltpu.roll(g, g.shape[0] - k, 0),)


roll_rows.defvjp(_roll_rows_fwd, _roll_rows_bwd)


def _row_ids(shape):
    return lax.broadcasted_iota(jnp.int32, shape, 0)


def _rms(x, g):
    return x * lax.rsqrt(jnp.mean(x * x, axis=-1, keepdims=True) + EPS) * g


def _layer_norm(x, g, b):
    mu = jnp.mean(x, axis=-1, keepdims=True)
    xc = x - mu
    return xc * lax.rsqrt(jnp.mean(xc * xc, axis=-1, keepdims=True) + EPS) * g + b


def _s5_prep(are, aim, ls, bre, bim):
    step = jnp.exp(ls)
    mag = jnp.exp(are * step)
    lr = mag * jnp.cos(aim * step)
    li = mag * jnp.sin(aim * step)
    den = are * are + aim * aim
    nr, ni = lr - 1.0, li
    fr = (nr * are + ni * aim) / den
    fi = (ni * are - nr * aim) / den
    bbr = fr * bre - fi * bim
    bbi = fr * bim + fi * bre
    pr = jnp.broadcast_to(lr, (S5_GROUP_ROWS, N_STATE))
    pi = jnp.broadcast_to(li, (S5_GROUP_ROWS, N_STATE))
    d = 1
    while d < S5_GROUP_ROWS:
        keep = _row_ids(pr.shape) >= d
        sr, si = roll_rows(pr, d), roll_rows(pi, d)
        pr, pi = jnp.where(keep, pr * sr - pi * si, pr), jnp.where(keep, pr * si + pi * sr, pi)
        d *= 2
    return pr, pi, bbr, bbi


def _s5_chunk(u, gate, hr, hi, pr, pi, bbr, bbi, cr, ci, dr, wglu, bglu):
    n, steps = u.shape[0], S5_GROUP_ROWS
    groups = n // steps
    xr = bdot(u, bbr)
    xi = bdot(u, bbi)
    lr, li = pr[0:1], pi[0:1]
    rs, ims = [xr[:groups]], [xi[:groups]]
    for t in range(1, steps):
        a, b = rs[-1], ims[-1]
        rs.append(xr[t * groups:(t + 1) * groups] + lr * a - li * b)
        ims.append(xi[t * groups:(t + 1) * groups] + lr * b + li * a)
    er, ei = rs[-1], ims[-1]
    mr, mi = pr[steps - 1:steps], pi[steps - 1:steps]
    gid = _row_ids(er.shape)
    er, ei = (er + jnp.where(gid == 0, mr * hr - mi * hi, 0.0), ei + jnp.where(gid == 0, mr * hi + mi * hr, 0.0))
    d = 1
    while d < groups:
        sr = jnp.where(gid >= d, roll_rows(er, d), 0.0)
        si = jnp.where(gid >= d, roll_rows(ei, d), 0.0)
        er, ei = er + mr * sr - mi * si, ei + mr * si + mi * sr
        mr, mi = mr * mr - mi * mi, 2.0 * mr * mi
        d *= 2
    before_r = jnp.where(gid == 0, hr, roll_rows(er, 1))
    before_i = jnp.where(gid == 0, hi, roll_rows(ei, 1))
    xr = jnp.concatenate([rs[t] + pr[t:t + 1] * before_r - pi[t:t + 1] * before_i for t in range(steps)], axis=0)
    xi = jnp.concatenate([ims[t] + pr[t:t + 1] * before_i + pi[t:t + 1] * before_r for t in range(steps)], axis=0)
    y = bdot(xr, cr) - bdot(xi, ci) + dr * u
    y = jax.nn.gelu(y)
    y = y * jax.nn.sigmoid(bdot(y, wglu) + bglu)
    return y * jax.nn.silu(gate), er[groups - 1:groups], ei[groups - 1:groups]


def _dn_pre(xc, xp, w0, w1, w2, w3, is_start, col):
    xp = jnp.where(is_start, 0.0, xp)
    halo_rows = _row_ids(xp.shape)
    acc = w3 * xc
    for d, w in ((1, w2), (2, w1), (3, w0)):
        r = roll_rows(xc, d)
        head = jnp.where(halo_rows >= d, r[:DN_HALO], roll_rows(xp, d))
        acc = acc + w * jnp.concatenate([head, r[DN_HALO:]], axis=0)
    y = jax.nn.silu(acc)
    if col >= 2 * DN_HEADS:
        return y
    nrm = y * lax.rsqrt(jnp.sum(y * y, axis=-1, keepdims=True) + EPS)
    return nrm * DN_HEAD_DIM ** -0.5 if col < DN_HEADS else nrm


def _dn_local(qs, ks, vs, abs_, alog, dtb, invs=None):
    c = DN_CHUNK
    ri = lax.broadcasted_iota(jnp.int32, (c, c), 0)
    ci = lax.broadcasted_iota(jnp.int32, (c, c), 1)
    causal, strict = ri >= ci, ri > ci
    tril = causal.astype(f32)
    gcums = [hdot(tril, -jnp.exp(alog) * jax.nn.softplus(ab + dtb)) for ab in abs_]
    gcum_ts = [g.T for g in gcums]
    sigs = [jax.nn.sigmoid(ab) for ab in abs_]
    chains = [(j, h) for j in range(len(abs_)) for h in range(DN_HEADS)]
    gc = [gcums[j][:, h:h + 1] for j, h in chains]
    decay = [jnp.where(causal, jnp.exp(jnp.where(causal, gc[n] - gcum_ts[j][h:h + 1, :], 0.0)), 0.0)
             for n, (j, h) in enumerate(chains)]
    beta = [sigs[j][:, DN_HEADS + h:DN_HEADS + h + 1] for j, h in chains]
    kb = [ks[j][h] * beta[n] for n, (j, h) in enumerate(chains)]
    ms = [jnp.where(strict, bdot_nt(kb[n], ks[j][h]) * decay[n], 0.0) for n, (j, h) in enumerate(chains)]
    egc = [jnp.exp(g) for g in gc]
    rhs = [jnp.concatenate([vs[j][h] * beta[n], kb[n] * egc[n]], axis=1) for n, (j, h) in enumerate(chains)]
    inv = _unit_lower_inverses(ms) if invs is None else [invs[j][h] for j, h in chains]
    sol = solve_unit_lower(ms, rhs, inv)
    values = [s[:, :DN_HEAD_DIM] for s in sol]
    k_cds = [s[:, DN_HEAD_DIM:] for s in sol]
    attns = [bdot_nt(qs[j][h], ks[j][h]) * decay[n] for n, (j, h) in enumerate(chains)]
    q_decs = [qs[j][h] * egc[n] for n, (j, h) in enumerate(chains)]
    k_decs = [ks[j][h] * jnp.exp(gc[n][c - 1:c, :] - gc[n]) for n, (j, h) in enumerate(chains)]

    def nest(flat):
        return [flat[j * DN_HEADS:(j + 1) * DN_HEADS] for j in range(len(abs_))]

    lasts = [jnp.exp(g[c - 1:c, :]) for g in gcums]
    return nest(values), nest(k_cds), nest(attns), nest(q_decs), nest(k_decs), lasts, nest(inv)


def _dn_step(values, k_cds, attns, q_decs, k_decs, lasts, ggs, sts, ng):
    v_new = [v - bdot(kc, st) for v, kc, st in zip(values, k_cds, sts)]
    o = [bdot(qd, st) for qd, st in zip(q_decs, sts)]
    o = [a + bdot(at, vn) for a, at, vn in zip(o, attns, v_new)]
    new = [st * la + bdot_tn(kd, vn) for st, la, kd, vn in zip(sts, lasts, k_decs, v_new)]
    return [_rms(a, ng) * jax.nn.silu(g) for a, g in zip(o, ggs)], new


def _sg_chunk(u, v, gate, lng, lnb, ws, bt):
    n = SG_CHUNK
    ug = jax.nn.gelu(u)
    vn = _layer_norm(jax.nn.gelu(v), lng, lnb)
    causal = lax.broadcasted_iota(jnp.int32, (n, n), 0) >= lax.broadcasted_iota(jnp.int32, (n, n), 1)
    lane = lax.broadcasted_iota(jnp.int32, (n, D_SG), 1)
    s = jnp.zeros((n, D_SG), f32)
    for h in range(SG_HEADS):
        t = bdot(jnp.where(causal, ws[h], 0.0), vn) + bt[:, h:h + 1]
        s = s + jnp.where((lane >= h * SG_HEAD_DIM) & (lane < (h + 1) * SG_HEAD_DIM), t, 0.0)
    return ug * s * jax.nn.silu(gate)


def _cp(n_grid, vmem=None):
    return pltpu.CompilerParams(dimension_semantics=("arbitrary",) * n_grid, vmem_limit_bytes=vmem)


def _full(shape):
    nd = len(shape)
    return pl.BlockSpec(tuple(shape), lambda *_: (0,) * nd)


def _rows(tm, ncol):
    return pl.BlockSpec((tm, ncol), lambda i: (i, 0))


def _sds(shape, dtype=f32):
    return jax.ShapeDtypeStruct(tuple(shape), dtype)


def _acc(ref, val, first):
    @pl.when(first)
    def _():
        ref[...] = val

    @pl.when(jnp.logical_not(first))
    def _():
        ref[...] += val


def in_fwd(x, g, w, name):
    t, tm = x.shape[0], 512

    def body(x_ref, g_ref, w_ref, h_ref, *z_refs):
        h = _rms(x_ref[...], g_ref[...]).astype(_MXU)
        h_ref[...] = h
        for z_ref, (a, b) in zip(z_refs, Z_COLS):
            z_ref[...] = jnp.dot(h, w_ref[:, a:b], preferred_element_type=f32)

    widths = [b - a for a, b in Z_COLS]
    return pl.pallas_call(
        body, name=name, grid=(t // tm,),
        in_specs=[_rows(tm, D_MODEL), _full((1, D_MODEL)), _full((D_MODEL, D_IN_PAD))],
        out_specs=[_rows(tm, D_MODEL)] + [_rows(tm, n) for n in widths],
        out_shape=[_sds((t, D_MODEL), _MXU)] + [_sds((t, n)) for n in widths],
        compiler_params=_cp(1, VMEM_BIG),
    )(x, g, w)


def in_bwd(x, g, w, dzs, dres, name):
    t, tm = x.shape[0], 512
    widths = [b - a for a, b in Z_COLS]

    def body(x_ref, g_ref, w_ref, dres_ref, *rest):
        dz_refs, (dx_ref, dg_ref) = rest[:5], rest[5:]
        dh = jnp.zeros((tm, D_MODEL), f32)
        for dz_ref, (a, b) in zip(dz_refs, Z_COLS):
            dh = dh + _mm_nt(dz_ref[...], w_ref[:, a:b])
        _, vj = jax.vjp(_rms, x_ref[...], g_ref[...])
        dx, dg = vj(dh)
        dx_ref[...] = dres_ref[...] + dx
        _acc(dg_ref, dg, pl.program_id(0) == 0)

    return pl.pallas_call(
        body, name=name, grid=(t // tm,),
        in_specs=[_rows(tm, D_MODEL), _full((1, D_MODEL)), _full((D_MODEL, D_IN_PAD)), _rows(tm, D_MODEL)]
        + [_rows(tm, n) for n in widths],
        out_specs=[_rows(tm, D_MODEL), _full((1, D_MODEL))],
        out_shape=[_sds((t, D_MODEL)), _sds((1, D_MODEL))],
        compiler_params=_cp(1, VMEM_BIG),
    )(x, g, w, dres, *dzs)


def wgrad(a, g, name):
    t, k = a.shape
    n = g.shape[1]
    tm = min(t, 2048)
    tn = n if n <= 1536 else n // 2
    steps = t // tm

    def body(a_ref, g_ref, o_ref, acc):
        i = pl.program_id(1)
        _acc(acc, _mm_tn(a_ref[...], g_ref[...]), i == 0)

        @pl.when(i == steps - 1)
        def _():
            o_ref[...] = acc[...].astype(o_ref.dtype)

    return pl.pallas_call(
        body, name=name, grid=(n // tn, steps),
        in_specs=[pl.BlockSpec((tm, k), lambda j, i: (i, 0)), pl.BlockSpec((tm, tn), lambda j, i: (i, j))],
        out_specs=pl.BlockSpec((k, tn), lambda j, i: (0, j)),
        out_shape=_sds((k, n), _COMM),
        scratch_shapes=[pltpu.VMEM((k, tn), f32)],
        compiler_params=_cp(2, VMEM_BIG),
    )(a, g)


def post_fwd(x, ys, yd, yg, p, wout, pg, wgate, wple, name):
    t, tm = x.shape[0], 512

    def body(x_ref, ys_ref, yd_ref, yg_ref, p_ref, wout_ref, pg_ref, wgate_ref, wple_ref,
             x2_ref, x1_ref, y_ref, hn_ref):
        y = jnp.concatenate([ys_ref[...], yd_ref[...], yg_ref[...]], axis=1).astype(_MXU)
        y_ref[...] = y
        x1 = x_ref[...] + jnp.dot(y, wout_ref[...], preferred_element_type=f32)
        x1_ref[...] = x1
        hn = _rms(x1, pg_ref[...]).astype(_MXU)
        hn_ref[...] = hn
        gp = jnp.dot(hn, wgate_ref[...], preferred_element_type=f32)
        pp = _mm(p_ref[...], wple_ref[...])
        x2_ref[...] = x1 + jax.nn.sigmoid(gp) * pp

    return pl.pallas_call(
        body, name=name, grid=(t // tm,),
        in_specs=[_rows(tm, D_MODEL), _rows(tm, D_SSM), _rows(tm, D_DN), _rows(tm, D_SG), _rows(tm, D_PLE),
                  _full((D_MODEL, D_MODEL)), _full((1, D_MODEL)), _full((D_MODEL, D_MODEL)), _full((D_PLE, D_MODEL))],
        out_specs=[_rows(tm, D_MODEL)] * 4,
        out_shape=[_sds((t, D_MODEL)), _sds((t, D_MODEL)), _sds((t, D_MODEL), _MXU), _sds((t, D_MODEL), _MXU)],
        compiler_params=_cp(1, VMEM_BIG),
    )(x, ys, yd, yg, p, wout, pg, wgate, wple)


def post_bwd(dx2, x1, hn, p, wout, pg, wgate, wple, name):
    t, tm = dx2.shape[0], 512

    def body(dx2_ref, x1_ref, hn_ref, p_ref, wout_ref, pg_ref, wgate_ref, wple_ref,
             dx1_ref, dgp_ref, dpp_ref, dys_ref, dyd_ref, dyg_ref, dpg_ref):
        dx2 = dx2_ref[...]
        gp = jnp.dot(hn_ref[...], wgate_ref[...], preferred_element_type=f32)
        pp = _mm(p_ref[...], wple_ref[...])
        sg = jax.nn.sigmoid(gp)
        dpp_ref[...] = (dx2 * sg).astype(_MXU)
        dgp = (dx2 * pp * sg * (1.0 - sg)).astype(_MXU)
        dgp_ref[...] = dgp
        dhn = _mm_nt(dgp, wgate_ref[...])
        _, vj = jax.vjp(_rms, x1_ref[...], pg_ref[...])
        dx1n, dpg = vj(dhn)
        dx1 = dx2 + dx1n
        dx1_ref[...] = dx1
        dy = _mm_nt(dx1, wout_ref[...])
        dys_ref[...] = dy[:, :D_SSM]
        dyd_ref[...] = dy[:, D_SSM:D_SSM + D_DN]
        dyg_ref[...] = dy[:, D_SSM + D_DN:]
        _acc(dpg_ref, dpg, pl.program_id(0) == 0)

    return pl.pallas_call(
        body, name=name, grid=(t // tm,),
        in_specs=[_rows(tm, D_MODEL), _rows(tm, D_MODEL), _rows(tm, D_MODEL), _rows(tm, D_PLE),
                  _full((D_MODEL, D_MODEL)), _full((1, D_MODEL)), _full((D_MODEL, D_MODEL)), _full((D_PLE, D_MODEL))],
        out_specs=[_rows(tm, D_MODEL), _rows(tm, D_MODEL), _rows(tm, D_MODEL), _rows(tm, D_SSM), _rows(tm, D_DN),
                   _rows(tm, D_SG), _full((1, D_MODEL))],
        out_shape=[_sds((t, D_MODEL)), _sds((t, D_MODEL), _MXU), _sds((t, D_MODEL), _MXU), _sds((t, D_SSM)),
                   _sds((t, D_DN)), _sds((t, D_SG)), _sds((1, D_MODEL))],
        compiler_params=_cp(1, VMEM_BIG),
    )(dx2, x1, hn, p, wout, pg, wgate, wple)


def loss_fwd_bwd(x, fg, target, name):
    t, tm = x.shape[0], 512

    def body(x_ref, fg_ref, t_ref, loss_ref, dx_ref, dfg_ref):
        def f(xv, gv):
            err = _rms(xv, gv) - t_ref[...]
            return 0.5 * jnp.sum(jnp.mean(err * err, axis=-1))

        val, vj = jax.vjp(f, x_ref[...], fg_ref[...])
        dx, dfg = vj(jnp.ones((), f32))
        dx_ref[...] = dx
        first = pl.program_id(0) == 0
        _acc(dfg_ref, dfg, first)
        _acc(loss_ref, jnp.full((8, LANE), val, f32), first)

    return pl.pallas_call(
        body, name=name, grid=(t // tm,),
        in_specs=[_rows(tm, D_MODEL), _full((1, D_MODEL)), _rows(tm, D_MODEL)],
        out_specs=[_full((8, LANE)), _rows(tm, D_MODEL), _full((1, D_MODEL))],
        out_shape=[_sds((8, LANE)), _sds((t, D_MODEL)), _sds((1, D_MODEL))],
        compiler_params=_cp(1),
    )(x, fg, target)


S5_PREPARED = 4
_S5_PARAM_SHAPES = ((S5_GROUP_ROWS, N_STATE), (S5_GROUP_ROWS, N_STATE), (D_SSM, N_STATE), (D_SSM, N_STATE),
                    (N_STATE, D_SSM), (N_STATE, D_SSM), (1, D_SSM), (D_SSM, D_SSM), (1, D_SSM))

def s5_prep_fwd(are, aim, ls, bre, bim, name):
    def body(are_ref, aim_ref, ls_ref, bre_ref, bim_ref, *outs):
        vals = _s5_prep(are_ref[...], aim_ref[...], ls_ref[...], bre_ref[...], bim_ref[...])
        for o, v in zip(outs, vals):
            o[...] = v

    return pl.pallas_call(body, name=name, out_shape=[_sds(s) for s in _S5_PARAM_SHAPES[:S5_PREPARED]])(
        are, aim, ls, bre, bim)


def s5_prep_bwd(are, aim, ls, bre, bim, cts, name):
    def body(are_ref, aim_ref, ls_ref, bre_ref, bim_ref, *rest):
        ct_refs, outs = rest[:S5_PREPARED], rest[S5_PREPARED:]
        _, vj = jax.vjp(_s5_prep, are_ref[...], aim_ref[...], ls_ref[...], bre_ref[...], bim_ref[...])
        for o, v in zip(outs, vj(tuple(r[...] for r in ct_refs))):
            o[...] = v

    shapes = [(1, N_STATE)] * 3 + [(D_SSM, N_STATE)] * 2
    return pl.pallas_call(body, name=name, out_shape=[_sds(s) for s in shapes])(are, aim, ls, bre, bim, *cts)


def _step_major(ref, cols):
    x = ref[:, cols]
    n, w = x.shape
    return jnp.swapaxes(x.reshape(n // S5_GROUP_ROWS, S5_GROUP_ROWS, w), 0, 1).reshape(n, w)


def _store_step_major(ref, cols, val):
    n, w = val.shape
    ref[:, cols] = jnp.swapaxes(val.reshape(S5_GROUP_ROWS, n // S5_GROUP_ROWS, w), 0, 1).reshape(n, w).astype(ref.dtype)


def s5_fwd(z, params, nb, name):
    t = z.shape[0]
    nc = t // nb // S5_CHUNK
    npar = len(_S5_PARAM_SHAPES)

    def body(z_ref, *rest):
        p_refs, (y_ref, hs_ref, hr_s, hi_s) = rest[:npar], rest[npar:]

        @pl.when(pl.program_id(1) == 0)
        def _():
            hr_s[...] = jnp.zeros_like(hr_s)
            hi_s[...] = jnp.zeros_like(hi_s)

        hr, hi = hr_s[...], hi_s[...]
        hs_ref[0, :, :N_STATE] = hr
        hs_ref[0, :, N_STATE:] = hi
        y, nhr, nhi = _s5_chunk(_step_major(z_ref, slice(0, D_SSM)), _step_major(z_ref, slice(D_SSM, 2 * D_SSM)),
                                hr, hi, *[r[...] for r in p_refs])
        _store_step_major(y_ref, slice(0, D_SSM), y)
        hr_s[...] = nhr
        hi_s[...] = nhi

    return pl.pallas_call(
        body, name=name, grid=(nb, nc),
        in_specs=[pl.BlockSpec((S5_CHUNK, 2 * D_SSM), lambda b, c: (b * nc + c, 0))]
        + [_full(s) for s in _S5_PARAM_SHAPES],
        out_specs=[pl.BlockSpec((S5_CHUNK, D_SSM), lambda b, c: (b * nc + c, 0)),
                   pl.BlockSpec((1, 1, 2 * N_STATE), lambda b, c: (b * nc + c, 0, 0))],
        out_shape=[_sds((t, D_SSM)), _sds((nb * nc, 1, 2 * N_STATE))],
        scratch_shapes=[pltpu.VMEM((1, N_STATE), f32), pltpu.VMEM((1, N_STATE), f32)],
        compiler_params=_cp(2, VMEM_BIG),
    )(z, *params)


def s5_bwd(z, params, hs, dy, nb, name):
    t = z.shape[0]
    nc = t // nb // S5_CHUNK
    npar = len(_S5_PARAM_SHAPES)

    def body(z_ref, hs_ref, dy_ref, *rest):
        p_refs, dz_ref, dp_refs, (dhr_s, dhi_s) = rest[:npar], rest[npar], rest[npar + 1:2 * npar + 1], rest[2 * npar + 1:]

        @pl.when(pl.program_id(1) == 0)
        def _():
            dhr_s[...] = jnp.zeros_like(dhr_s)
            dhi_s[...] = jnp.zeros_like(dhi_s)

        prim = (_step_major(z_ref, slice(0, D_SSM)), _step_major(z_ref, slice(D_SSM, 2 * D_SSM)),
                hs_ref[0, :, :N_STATE], hs_ref[0, :, N_STATE:]) + tuple(r[...] for r in p_refs)
        _, vj = jax.vjp(_s5_chunk, *prim)
        cts = vj((_step_major(dy_ref, slice(0, D_SSM)), dhr_s[...], dhi_s[...]))
        _store_step_major(dz_ref, slice(0, D_SSM), cts[0])
        _store_step_major(dz_ref, slice(D_SSM, 2 * D_SSM), cts[1])
        dhr_s[...] = cts[2]
        dhi_s[...] = cts[3]
        first = (pl.program_id(0) == 0) & (pl.program_id(1) == 0)
        for r, v in zip(dp_refs, cts[4:]):
            _acc(r, v, first)

    rev = lambda b, c: (b * nc + nc - 1 - c, 0)
    return pl.pallas_call(
        body, name=name, grid=(nb, nc),
        in_specs=[pl.BlockSpec((S5_CHUNK, 2 * D_SSM), rev),
                  pl.BlockSpec((1, 1, 2 * N_STATE), lambda b, c: (b * nc + nc - 1 - c, 0, 0)),
                  pl.BlockSpec((S5_CHUNK, D_SSM), rev)] + [_full(s) for s in _S5_PARAM_SHAPES],
        out_specs=[pl.BlockSpec((S5_CHUNK, 2 * D_SSM), rev)] + [_full(s) for s in _S5_PARAM_SHAPES],
        out_shape=[_sds((t, 2 * D_SSM), _MXU)] + [_sds(s) for s in _S5_PARAM_SHAPES],
        scratch_shapes=[pltpu.VMEM((1, N_STATE), f32), pltpu.VMEM((1, N_STATE), f32)],
        compiler_params=_cp(2, VMEM_BIG),
    )(z, hs, dy, *params)


DN_PRE_ROWS = 256
DN_COLS = 3 * D_DN // LANE


def dn_pre_bwd(zq, convw, dqkv, seq, name):
    t, tb = zq.shape[0], DN_PRE_ROWS
    nrow = t // tb
    per_seq = seq // tb

    def body(xc_ref, xp_ref, w_ref, d_ref, dx_ref, dw_ref, carry):
        step = pl.program_id(0)
        i = nrow - 1 - step

        @pl.when(step == 0)
        def _():
            carry[...] = jnp.zeros_like(carry)

        for j in range(DN_COLS):
            cols = slice(j * LANE, (j + 1) * LANE)
            fn = functools.partial(_dn_pre, is_start=i % per_seq == 0, col=j)
            _, vj = jax.vjp(fn, xc_ref[:, cols], xp_ref[:, cols], w_ref[0:1, cols], w_ref[1:2, cols],
                            w_ref[2:3, cols], w_ref[3:4, cols])
            dxc, dxp, dw0, dw1, dw2, dw3 = vj(d_ref[:, cols])
            dx_ref[:tb - DN_HALO, cols] = dxc[:tb - DN_HALO].astype(_MXU)
            dx_ref[tb - DN_HALO:, cols] = (dxc[tb - DN_HALO:] + carry[:, cols]).astype(_MXU)
            carry[:, cols] = dxp
            for k, dw in enumerate((dw0, dw1, dw2, dw3)):
                @pl.when(step == 0)
                def _():
                    dw_ref[k:k + 1, cols] = dw

                @pl.when(step != 0)
                def _():
                    dw_ref[k:k + 1, cols] += dw

    rev = lambda s: (nrow - 1 - s, 0)
    return pl.pallas_call(
        body, name=name, grid=(nrow,),
        in_specs=[pl.BlockSpec((tb, 3 * D_DN), rev),
                  pl.BlockSpec((DN_HALO, 3 * D_DN),
                               lambda s: (jnp.maximum((nrow - 1 - s) * (tb // DN_HALO) - 1, 0), 0)),
                  _full((DN_CONV, 3 * D_DN)), pl.BlockSpec((tb, 3 * D_DN), rev)],
        out_specs=[pl.BlockSpec((tb, 3 * D_DN), rev), _full((DN_CONV, 3 * D_DN))],
        out_shape=[_sds((t, 3 * D_DN), _MXU), _sds((DN_CONV, 3 * D_DN))],
        scratch_shapes=[pltpu.VMEM((DN_HALO, 3 * D_DN), f32)],
        compiler_params=_cp(1, VMEM_BIG),
    )(zq, zq, convw, dqkv)


DN_LOCAL_CHUNKS = 4
DN_ATTN = DN_HEADS * DN_CHUNK


def _dn_heads(ref, rows, base=0):
    return [ref[rows, base + h * DN_HEAD_DIM:base + (h + 1) * DN_HEAD_DIM] for h in range(DN_HEADS)]


def dn_front_fwd(zq, convw, ab, alog, dtb, seq, name):
    t = zq.shape[0]
    c, n = DN_CHUNK, DN_LOCAL_CHUNKS
    per_seq = seq // (n * c)

    def body(xc_ref, xp_ref, w_ref, ab_ref, alog_ref, dtb_ref,
             qkv_ref, val_ref, kcd_ref, attn_ref, qd_ref, kd_ref, el_ref, inv_ref):
        is_start = pl.program_id(0) % per_seq == 0
        blocks = []
        for j in range(DN_COLS):
            cols = slice(j * LANE, (j + 1) * LANE)
            blocks.append(_dn_pre(xc_ref[:, cols], xp_ref[:, cols], w_ref[0:1, cols], w_ref[1:2, cols],
                                  w_ref[2:3, cols], w_ref[3:4, cols], is_start, j))
            qkv_ref[:, cols] = blocks[-1]
        rows = [pl.ds(j * c, c) for j in range(n)]

        def heads(base, j):
            return [blocks[base + h][j * c:(j + 1) * c] for h in range(DN_HEADS)]

        vals, kcds, attns, qds, kds, els, invs = _dn_local(
            [heads(0, j) for j in range(n)], [heads(DN_HEADS, j) for j in range(n)],
            [heads(2 * DN_HEADS, j) for j in range(n)], [ab_ref[r, :] for r in rows], alog_ref[...], dtb_ref[...])
        for j, r in enumerate(rows):
            for h in range(DN_HEADS):
                lo, hi = h * DN_HEAD_DIM, (h + 1) * DN_HEAD_DIM
                val_ref[r, lo:hi] = vals[j][h]
                kcd_ref[r, lo:hi] = kcds[j][h].astype(_MXU)
                qd_ref[r, lo:hi] = qds[j][h].astype(_MXU)
                kd_ref[r, lo:hi] = kds[j][h].astype(_MXU)
                attn_ref[r, h * c:(h + 1) * c] = attns[j][h].astype(_MXU)
                inv_ref[r, h * c:(h + 1) * c] = invs[j][h]
            el_ref[j] = els[j]

    wide = _rows(n * c, D_DN)
    outs = pl.pallas_call(
        body, name=name, grid=(t // (n * c),),
        in_specs=[_rows(n * c, 3 * D_DN),
                  pl.BlockSpec((DN_HALO, 3 * D_DN), lambda i: (jnp.maximum(i * (n * c // DN_HALO) - 1, 0), 0)),
                  _full((DN_CONV, 3 * D_DN)), _rows(n * c, LANE), _full((1, LANE)), _full((1, LANE))],
        out_specs=[_rows(n * c, 3 * D_DN), wide, wide, _rows(n * c, DN_ATTN), wide, wide,
                   pl.BlockSpec((n, 1, LANE), lambda i: (i, 0, 0)), _rows(n * c, DN_ATTN)],
        out_shape=[_sds((t, 3 * D_DN)), _sds((t, D_DN)), _sds((t, D_DN), _MXU), _sds((t, DN_ATTN), _MXU),
                   _sds((t, D_DN), _MXU), _sds((t, D_DN), _MXU), _sds((t // c, 1, LANE)), _sds((t, DN_ATTN))],
        compiler_params=_cp(1, VMEM_BIG),
    )(zq, zq, convw, ab, alog, dtb)
    return outs[0], outs[1:7], outs[7]


def dn_local_bwd(qkv, ab, alog, dtb, inv, cts, name):
    t = qkv.shape[0]
    c, n = DN_CHUNK, DN_LOCAL_CHUNKS

    def body(qkv_ref, ab_ref, alog_ref, dtb_ref, inv_ref, dval_ref, dkcd_ref, dattn_ref, dqd_ref, dkd_ref, del_ref,
             dqkv_ref, dab_ref, dalog_ref, ddtb_ref):
        rows = [pl.ds(j * c, c) for j in range(n)]
        invs = [[inv_ref[r, h * c:(h + 1) * c] for h in range(DN_HEADS)] for r in rows]

        def local(qs, ks, vs, abs_, alog, dtb):
            return _dn_local(qs, ks, vs, abs_, alog, dtb, invs)[:6]

        _, vj = jax.vjp(local, [_dn_heads(qkv_ref, r) for r in rows], [_dn_heads(qkv_ref, r, D_DN) for r in rows],
                        [_dn_heads(qkv_ref, r, 2 * D_DN) for r in rows], [ab_ref[r, :] for r in rows], alog_ref[...],
                        dtb_ref[...])
        dattn = [[dattn_ref[r, h * c:(h + 1) * c] for h in range(DN_HEADS)] for r in rows]
        dq, dk, dv, dab, dalog, ddtb = vj(([_dn_heads(dval_ref, r) for r in rows], [_dn_heads(dkcd_ref, r) for r in rows],
                                           dattn, [_dn_heads(dqd_ref, r) for r in rows],
                                           [_dn_heads(dkd_ref, r) for r in rows], [del_ref[j] for j in range(n)]))
        for j, r in enumerate(rows):
            for h in range(DN_HEADS):
                lo, hi = h * DN_HEAD_DIM, (h + 1) * DN_HEAD_DIM
                dqkv_ref[r, lo:hi] = dq[j][h]
                dqkv_ref[r, D_DN + lo:D_DN + hi] = dk[j][h]
                dqkv_ref[r, 2 * D_DN + lo:2 * D_DN + hi] = dv[j][h]
            dab_ref[r, :] = dab[j].astype(_MXU)
        first = pl.program_id(0) == 0
        _acc(dalog_ref, dalog, first)
        _acc(ddtb_ref, ddtb, first)

    wide = _rows(n * c, D_DN)
    return pl.pallas_call(
        body, name=name, grid=(t // (n * c),),
        in_specs=[_rows(n * c, 3 * D_DN), _rows(n * c, LANE), _full((1, LANE)), _full((1, LANE)),
                  _rows(n * c, DN_ATTN), wide, wide, _rows(n * c, DN_ATTN), wide, wide,
                  pl.BlockSpec((n, 1, LANE), lambda i: (i, 0, 0))],
        out_specs=[_rows(n * c, 3 * D_DN), _rows(n * c, LANE), _full((1, LANE)), _full((1, LANE))],
        out_shape=[_sds((t, 3 * D_DN)), _sds((t, LANE), _MXU), _sds((1, LANE)), _sds((1, LANE))],
        compiler_params=_cp(1),
    )(qkv, ab, alog, dtb, inv, *cts)


def _seq_view(a, nb):
    return a.reshape((nb, a.shape[0] // nb) + a.shape[1:])


def _dn_chains(nb):
    return [(b, h) for b in range(nb) for h in range(DN_HEADS)]


DN_SCAN_CHUNKS = 4
DN_SCAN_FWD_CHUNKS = 8


def _dn_step_operands(val_ref, kcd_ref, attn_ref, qd_ref, kd_ref, el_ref, gg_ref, nb, j):
    chains = _dn_chains(nb)
    c = DN_CHUNK
    rows = pl.ds(j * c, c)

    def wide(ref):
        return [ref[b, rows, h * DN_HEAD_DIM:(h + 1) * DN_HEAD_DIM].astype(f32) for b, h in chains]

    attns = [attn_ref[b, rows, h * c:(h + 1) * c].astype(f32) for b, h in chains]
    return (wide(val_ref), wide(kcd_ref), attns, wide(qd_ref), wide(kd_ref),
            [el_ref[b, j, :, h:h + 1] for b, h in chains], wide(gg_ref))


def dn_scan_fwd(loc, gg, ng, nb, name):
    val, kcd, attn, qd, kd, el = loc
    t = val.shape[0]
    c, n = DN_CHUNK, DN_SCAN_FWD_CHUNKS
    nc = t // nb // c
    ns = nb * DN_HEADS

    def body(val_ref, kcd_ref, attn_ref, qd_ref, kd_ref, el_ref, gg_ref, ng_ref, y_ref, ss_ref, st):
        @pl.when(pl.program_id(0) == 0)
        def _():
            st[...] = jnp.zeros_like(st)

        sts = [st[i] for i in range(ns)]
        for j in range(n):
            for i in range(ns):
                ss_ref[j, i] = sts[i]
            ys, sts = _dn_step(*_dn_step_operands(val_ref, kcd_ref, attn_ref, qd_ref, kd_ref, el_ref, gg_ref, nb, j),
                               sts, ng_ref[...])
            for i, (b, h) in enumerate(_dn_chains(nb)):
                y_ref[b, pl.ds(j * c, c), h * DN_HEAD_DIM:(h + 1) * DN_HEAD_DIM] = ys[i]
        for i in range(ns):
            st[i] = sts[i]

    def blk(w):
        return pl.BlockSpec((nb, n * c, w), lambda k: (0, k, 0))

    el_spec = pl.BlockSpec((nb, n, 1, LANE), lambda k: (0, k, 0, 0))
    y, ss = pl.pallas_call(
        body, name=name, grid=(nc // n,),
        in_specs=[blk(D_DN), blk(D_DN), blk(DN_ATTN), blk(D_DN), blk(D_DN), el_spec, blk(D_DN), _full((1, LANE))],
        out_specs=[blk(D_DN), pl.BlockSpec((n, ns, DN_HEAD_DIM, DN_HEAD_DIM), lambda k: (k, 0, 0, 0))],
        out_shape=[_sds((nb, t // nb, D_DN)), _sds((nc, ns, DN_HEAD_DIM, DN_HEAD_DIM))],
        scratch_shapes=[pltpu.VMEM((ns, DN_HEAD_DIM, DN_HEAD_DIM), f32)],
        compiler_params=_cp(1, VMEM_BIG),
    )(_seq_view(val, nb), _seq_view(kcd, nb), _seq_view(attn, nb), _seq_view(qd, nb), _seq_view(kd, nb),
      el.reshape(nb, nc, 1, LANE), _seq_view(gg, nb), ng)
    return y.reshape(t, D_DN), ss


def dn_scan_bwd(loc, gg, ng, ss, dy, nb, name):
    val, kcd, attn, qd, kd, el = loc
    t = val.shape[0]
    c, n = DN_CHUNK, DN_SCAN_CHUNKS
    nc = t // nb // c
    ns = nb * DN_HEADS
    steps = nc // n

    def body(val_ref, kcd_ref, attn_ref, qd_ref, kd_ref, el_ref, gg_ref, ng_ref, ss_ref, dy_ref,
             dval_ref, dkcd_ref, dattn_ref, dqd_ref, dkd_ref, del_ref, dgg_ref, dng_ref, dst):
        @pl.when(pl.program_id(0) == 0)
        def _():
            dst[...] = jnp.zeros_like(dst)

        lane = lax.broadcasted_iota(jnp.int32, (1, LANE), 1)
        chains = _dn_chains(nb)
        ds = [dst[i] for i in range(ns)]
        dng_tot = jnp.zeros((1, LANE), f32)
        for j in reversed(range(n)):
            rows = pl.ds(j * c, c)
            _, vj = jax.vjp(_dn_step,
                            *_dn_step_operands(val_ref, kcd_ref, attn_ref, qd_ref, kd_ref, el_ref, gg_ref, nb, j),
                            [ss_ref[j, i] for i in range(ns)], ng_ref[...])
            dys = [dy_ref[b, rows, h * DN_HEAD_DIM:(h + 1) * DN_HEAD_DIM] for b, h in chains]
            dval, dkcd, dattn, dqd, dkd, dlast, dgg, ds, dng = vj((dys, ds))
            dng_tot = dng_tot + dng
            del_rows = [jnp.zeros((1, LANE), f32) for _ in range(nb)]
            for i, (b, h) in enumerate(chains):
                cols = slice(h * DN_HEAD_DIM, (h + 1) * DN_HEAD_DIM)
                dval_ref[b, rows, cols] = dval[i]
                dkcd_ref[b, rows, cols] = dkcd[i]
                dattn_ref[b, rows, h * c:(h + 1) * c] = dattn[i]
                dqd_ref[b, rows, cols] = dqd[i]
                dkd_ref[b, rows, cols] = dkd[i]
                dgg_ref[b, rows, cols] = dgg[i].astype(_MXU)
                del_rows[b] = del_rows[b] + jnp.where(lane == h, dlast[i], 0.0)
            for b in range(nb):
                del_ref[b, j] = del_rows[b]
        for i in range(ns):
            dst[i] = ds[i]
        _acc(dng_ref, dng_tot, pl.program_id(0) == 0)

    def blk(w):
        return pl.BlockSpec((nb, n * c, w), lambda k: (0, steps - 1 - k, 0))

    el_spec = pl.BlockSpec((nb, n, 1, LANE), lambda k: (0, steps - 1 - k, 0, 0))
    outs = pl.pallas_call(
        body, name=name, grid=(steps,),
        in_specs=[blk(D_DN), blk(D_DN), blk(DN_ATTN), blk(D_DN), blk(D_DN), el_spec, blk(D_DN), _full((1, LANE)),
                  pl.BlockSpec((n, ns, DN_HEAD_DIM, DN_HEAD_DIM), lambda k: (steps - 1 - k, 0, 0, 0)), blk(D_DN)],
        out_specs=[blk(D_DN), blk(D_DN), blk(DN_ATTN), blk(D_DN), blk(D_DN), el_spec, blk(D_DN), _full((1, LANE))],
        out_shape=[_sds((nb, t // nb, D_DN)), _sds((nb, t // nb, D_DN)), _sds((nb, t // nb, DN_ATTN)),
                   _sds((nb, t // nb, D_DN)), _sds((nb, t // nb, D_DN)), _sds((nb, nc, 1, LANE)),
                   _sds((nb, t // nb, D_DN), _MXU), _sds((1, LANE))],
        scratch_shapes=[pltpu.VMEM((ns, DN_HEAD_DIM, DN_HEAD_DIM), f32)],
        compiler_params=_cp(1, VMEM_BIG),
    )(_seq_view(val, nb), _seq_view(kcd, nb), _seq_view(attn, nb), _seq_view(qd, nb), _seq_view(kd, nb),
      el.reshape(nb, nc, 1, LANE), _seq_view(gg, nb), ng, ss, _seq_view(dy, nb))
    dloc = [o.reshape((t,) + o.shape[2:]) for o in outs[:5]] + [outs[5].reshape(t // c, 1, LANE)]
    return dloc, outs[6].reshape(t, D_DN), outs[7]


SG_ROWS = 512


def sg_fwd(z, lng, lnb, w, bt, name):
    t = z.shape[0]

    def body(z_ref, lng_ref, lnb_ref, w_ref, bt_ref, y_ref):
        ws = [w_ref[h] for h in range(SG_HEADS)]
        for k in range(SG_ROWS // SG_CHUNK):
            r = pl.ds(k * SG_CHUNK, SG_CHUNK)
            y_ref[r, :] = _sg_chunk(z_ref[r, :D_SG], z_ref[r, D_SG:2 * D_SG], z_ref[r, 2 * D_SG:], lng_ref[...],
                                    lnb_ref[...], ws, bt_ref[...])

    return pl.pallas_call(
        body, name=name, grid=(t // SG_ROWS,),
        in_specs=[_rows(SG_ROWS, 3 * D_SG), _full((1, D_SG)), _full((1, D_SG)),
                  _full((SG_HEADS, SG_CHUNK, SG_CHUNK)), _full((SG_CHUNK, LANE))],
        out_specs=_rows(SG_ROWS, D_SG),
        out_shape=_sds((t, D_SG)),
        compiler_params=_cp(1),
    )(z, lng, lnb, w, bt)


def sg_bwd(z, lng, lnb, w, bt, dy, name):
    t = z.shape[0]

    def body(z_ref, lng_ref, lnb_ref, w_ref, bt_ref, dy_ref, dz_ref, dlng_ref, dlnb_ref, dw_ref, dbt_ref):
        ws = [w_ref[h] for h in range(SG_HEADS)]
        tot = None
        for k in range(SG_ROWS // SG_CHUNK):
            r = pl.ds(k * SG_CHUNK, SG_CHUNK)
            _, vj = jax.vjp(_sg_chunk, z_ref[r, :D_SG], z_ref[r, D_SG:2 * D_SG], z_ref[r, 2 * D_SG:], lng_ref[...],
                            lnb_ref[...], ws, bt_ref[...])
            du, dv, dgate, dlng, dlnb, dws, dbt = vj(dy_ref[r, :])
            dz_ref[r, :D_SG] = du.astype(_MXU)
            dz_ref[r, D_SG:2 * D_SG] = dv.astype(_MXU)
            dz_ref[r, 2 * D_SG:] = dgate.astype(_MXU)
            part = [dlng, dlnb, dbt] + list(dws)
            tot = part if tot is None else [a + b for a, b in zip(tot, part)]
        first = pl.program_id(0) == 0
        _acc(dlng_ref, tot[0], first)
        _acc(dlnb_ref, tot[1], first)
        _acc(dbt_ref, tot[2], first)
        for h in range(SG_HEADS):
            @pl.when(first)
            def _():
                dw_ref[h] = tot[3 + h]

            @pl.when(jnp.logical_not(first))
            def _():
                dw_ref[h] += tot[3 + h]

    return pl.pallas_call(
        body, name=name, grid=(t // SG_ROWS,),
        in_specs=[_rows(SG_ROWS, 3 * D_SG), _full((1, D_SG)), _full((1, D_SG)),
                  _full((SG_HEADS, SG_CHUNK, SG_CHUNK)), _full((SG_CHUNK, LANE)), _rows(SG_ROWS, D_SG)],
        out_specs=[_rows(SG_ROWS, 3 * D_SG), _full((1, D_SG)), _full((1, D_SG)),
                   _full((SG_HEADS, SG_CHUNK, SG_CHUNK)), _full((SG_CHUNK, LANE))],
        out_shape=[_sds((t, 3 * D_SG), _MXU), _sds((1, D_SG)), _sds((1, D_SG)), _sds((SG_HEADS, SG_CHUNK, SG_CHUNK)),
                   _sds((SG_CHUNK, LANE))],
        compiler_params=_cp(1),
    )(z, lng, lnb, w, bt, dy)


def add_pairs(a_list, b_list, name):
    n = len(a_list)

    def body(*refs):
        for a_ref, b_ref, o_ref in zip(refs[:n], refs[n:2 * n], refs[2 * n:]):
            o_ref[...] = (a_ref[...].astype(f32) + b_ref[...].astype(f32)).astype(o_ref.dtype)

    return pl.pallas_call(
        body, name=name, out_shape=[_sds(a.shape, a.dtype) for a in a_list],
        compiler_params=pltpu.CompilerParams(vmem_limit_bytes=VMEM_BIG),
    )(*a_list, *b_list)


def _adamw(g, w, m, v):
    nm = ADAM_B1 * m + (1.0 - ADAM_B1) * g
    nv = ADAM_B2 * v + (1.0 - ADAM_B2) * jnp.square(g)
    m_hat = nm / (1.0 - ADAM_B1 ** ADAM_STEP)
    v_hat = nv / (1.0 - ADAM_B2 ** ADAM_STEP)
    return -ADAM_LR * (m_hat / (jnp.sqrt(v_hat) + ADAM_EPS) + ADAM_WD * w), nm, nv


def sum_parts(half, recv, name):
    _, r, c = recv.shape
    tr = 256 if r % 256 == 0 else r

    def body(half_ref, recv_ref, g_ref):
        g = recv_ref[0].astype(f32)
        for k in range(1, N_CHIPS):
            g = g + recv_ref[k].astype(f32)
        g_ref[...] = g

    return pl.pallas_call(
        body, name=name,
        grid_spec=pltpu.PrefetchScalarGridSpec(
            num_scalar_prefetch=1, grid=(r // tr,),
            in_specs=[pl.BlockSpec((N_CHIPS, tr, c), lambda i, h: (0, i, 0))],
            out_specs=pl.BlockSpec((None, tr, c), lambda i, h: (h[0], i, 0))),
        out_shape=_sds((2, r, c)),
        compiler_params=_cp(1, VMEM_BIG),
    )(half, recv)


def adamw(g, w, m, v, name):
    _, r, c = w.shape
    tr = 256 if r % 256 == 0 else r

    def body(g_ref, w_ref, m_ref, v_ref, d_ref, nm_ref, nv_ref):
        d_ref[...], nm_ref[...], nv_ref[...] = _adamw(g_ref[...], w_ref[...], m_ref[...], v_ref[...])

    blk = pl.BlockSpec((None, tr, c), lambda l, i: (l, i, 0))
    return pl.pallas_call(
        body, name=name, grid=(2, r // tr), in_specs=[blk] * 4, out_specs=[blk] * 3, out_shape=[_sds((2, r, c))] * 3,
        compiler_params=_cp(2, VMEM_BIG),
    )(g, w, m, v)


def sum_parts_small(chip, parts, sums, name):
    n = len(sums)

    def body(chip_ref, *refs):
        for part, own, out in zip(refs[:n], refs[n:2 * n], refs[2 * n:]):
            g = jnp.where(chip_ref[0] == 0, own[...], part[0])
            for q in range(1, N_CHIPS):
                g = g + jnp.where(chip_ref[0] == q, own[...], part[q])
            out[...] = g

    vmem = pl.BlockSpec(memory_space=pltpu.VMEM)
    return pl.pallas_call(
        body, name=name, in_specs=[pl.BlockSpec(memory_space=pltpu.SMEM)] + [vmem] * (2 * n), out_specs=[vmem] * n,
        out_shape=[_sds(s.shape) for s in sums], compiler_params=pltpu.CompilerParams(vmem_limit_bytes=VMEM_BIG),
    )(chip, *parts, *sums)


def adamw_small(gs, ws, ms, vs, name):
    n = len(ws)

    def body(*refs):
        ins, outs = refs[:4 * n], refs[4 * n:]
        for k in range(n):
            outs[k][...], outs[n + k][...], outs[2 * n + k][...] = _adamw(
                ins[k][...], ins[n + k][...], ins[2 * n + k][...], ins[3 * n + k][...])

    outs = pl.pallas_call(
        body, name=name, out_shape=[_sds(w.shape) for w in ws] * 3,
        compiler_params=pltpu.CompilerParams(vmem_limit_bytes=VMEM_BIG),
    )(*gs, *ws, *ms, *vs)
    return [outs[j * n:(j + 1) * n] for j in range(3)]


_ANY = pl.BlockSpec(memory_space=pltpu.HBM)
_MESH = pl.DeviceIdType.MESH


def _flip(v, bit):
    return 1 - v if bit else v


_CHIP_RELS = ((1, 0), (0, 1), (1, 1))


def _piece(ref, kind, q):
    if kind[0] == "slot":
        return ref.at[q]
    if kind[0] == "all":
        return ref
    _, axis, n = kind
    return ref.at[(slice(None),) * axis + (pl.ds(q * n, n),)]


def _piece_shape(shape, kind):
    if kind[0] == "slot":
        return tuple(shape[1:])
    if kind[0] == "all":
        return tuple(shape)
    _, axis, n = kind
    return tuple(shape[:axis]) + (n,) + tuple(shape[axis + 1:])


def gather_weights(shards, kinds, name):
    n = len(shards)

    def out_shape(s, kind):
        if kind[0] == "slot":
            return (N_CHIPS,) + tuple(s.shape)
        _, axis, w = kind
        return tuple(s.shape[:axis + 1]) + (N_CHIPS * w,) + tuple(s.shape[axis + 2:])

    def place(o_ref, kind, q, layer):
        if kind[0] == "slot":
            return o_ref.at[q, layer]
        return _piece(o_ref.at[layer], kind, q)

    def body(*refs):
        s_refs, o_refs = refs[:n], refs[n:2 * n]
        send_sems, recv_sems, fwd_send_sems, fwd_recv_sems = refs[2 * n:]
        x, y, c = lax.axis_index("x"), lax.axis_index("y"), lax.axis_index("c")
        mine = 2 * x + y
        sends, arrivals, forwards, fwd_arrivals = [], [], [], []
        for r, (fx, fy) in enumerate(_CHIP_RELS):
            px, py = _flip(x, fx), _flip(y, fy)
            peer = 2 * px + py
            for k in range(n):
                s = r * n + k
                sends.append(pltpu.make_async_remote_copy(
                    src_ref=s_refs[k].at[c], dst_ref=place(o_refs[k], kinds[k], mine, c), send_sem=send_sems.at[s],
                    recv_sem=recv_sems.at[s], device_id=(px, py, c), device_id_type=_MESH))
                arrivals.append(pltpu.make_async_remote_copy(
                    src_ref=s_refs[k].at[c], dst_ref=place(o_refs[k], kinds[k], peer, c), send_sem=send_sems.at[s],
                    recv_sem=recv_sems.at[s], device_id=(px, py, c), device_id_type=_MESH))
                block = place(o_refs[k], kinds[k], peer, c)
                forwards.append(pltpu.make_async_remote_copy(
                    src_ref=block, dst_ref=block, send_sem=fwd_send_sems.at[s], recv_sem=fwd_recv_sems.at[s],
                    device_id=(x, y, 1 - c), device_id_type=_MESH))
                other = place(o_refs[k], kinds[k], peer, 1 - c)
                fwd_arrivals.append(pltpu.make_async_remote_copy(
                    src_ref=other, dst_ref=other, send_sem=fwd_send_sems.at[s], recv_sem=fwd_recv_sems.at[s],
                    device_id=(x, y, 1 - c), device_id_type=_MESH))
        for cp in sends:
            cp.start()
        for arrived, fwd in zip(arrivals, forwards):
            arrived.wait_recv()
            fwd.start()
        for cp in fwd_arrivals:
            cp.wait_recv()
        for cp in sends + forwards:
            cp.wait_send()

    m = len(_CHIP_RELS) * n
    return pl.pallas_call(
        body, name=name, in_specs=[_ANY] * n, out_specs=[_ANY] * n,
        out_shape=[_sds(out_shape(s, k), s.dtype) for s, k in zip(shards, kinds)],
        scratch_shapes=[pltpu.SemaphoreType.DMA((m,))] * 4,
    )(*shards)


def _owned_by(owners, side):
    return [k for k, o in enumerate(owners) if o == side]


def exchange_halves(gs, smalls, owners, name):
    n, ns = len(gs), len(smalls)

    def body(*refs):
        g_refs, s_refs = refs[:n], refs[n:n + ns]
        got_refs, sgot_refs = refs[n + ns:2 * n + ns], refs[2 * n + ns:2 * (n + ns)]
        send_sems, recv_sems = refs[2 * (n + ns):]
        x, y, c = lax.axis_index("x"), lax.axis_index("y"), lax.axis_index("c")
        sibling = (x, y, 1 - c)
        swaps = [pltpu.make_async_remote_copy(
            src_ref=g_refs[k].at[1 - c], dst_ref=got_refs[k], send_sem=send_sems.at[k], recv_sem=recv_sems.at[k],
            device_id=sibling, device_id_type=_MESH) for k in range(n)]
        gives = [pltpu.make_async_remote_copy(
            src_ref=s_refs[k], dst_ref=sgot_refs[k], send_sem=send_sems.at[n + k], recv_sem=recv_sems.at[n + k],
            device_id=sibling, device_id_type=_MESH) for k in range(ns)]
        for cp in swaps:
            cp.start()
        for side in (0, 1):
            @pl.when(c == 1 - side)
            def _():
                for k in _owned_by(owners, side):
                    gives[k].start()
        for cp in swaps:
            cp.wait()
        for side in (0, 1):
            @pl.when(c == 1 - side)
            def _():
                for k in _owned_by(owners, side):
                    gives[k].wait_send()

            @pl.when(c == side)
            def _():
                for k in _owned_by(owners, side):
                    gives[k].wait_recv()

    outs = pl.pallas_call(
        body, name=name, in_specs=[_ANY] * (n + ns), out_specs=[_ANY] * (n + ns),
        out_shape=[_sds(g.shape[1:], g.dtype) for g in gs] + [_sds(s.shape, s.dtype) for s in smalls],
        scratch_shapes=[pltpu.SemaphoreType.DMA((n + ns,)), pltpu.SemaphoreType.DMA((n + ns,))],
    )(*gs, *smalls)
    return outs[:n], outs[n:]


def reduce_to_chips(ts, kinds, smalls, owners, name):
    n, ns = len(ts), len(smalls)

    def body(*refs):
        t_refs, s_refs = refs[:n], refs[n:n + ns]
        o_refs, so_refs = refs[n + ns:2 * n + ns], refs[2 * n + ns:2 * (n + ns)]
        send_sems, recv_sems = refs[2 * (n + ns):]
        x, y, c = lax.axis_index("x"), lax.axis_index("y"), lax.axis_index("c")
        mine = 2 * x + y
        sends, arrivals, small_sends, small_arrivals = [], [], [], []
        for r, (fx, fy) in enumerate(_CHIP_RELS):
            px, py = _flip(x, fx), _flip(y, fy)
            peer = 2 * px + py
            for k in range(n + ns):
                s = r * (n + ns) + k
                if k < n:
                    src, dst = _piece(t_refs[k], kinds[k], peer), o_refs[k]
                else:
                    src, dst = s_refs[k - n], so_refs[k - n]
                go = pltpu.make_async_remote_copy(
                    src_ref=src, dst_ref=dst.at[mine], send_sem=send_sems.at[s], recv_sem=recv_sems.at[s],
                    device_id=(px, py, c), device_id_type=_MESH)
                come = pltpu.make_async_remote_copy(
                    src_ref=src, dst_ref=dst.at[peer], send_sem=send_sems.at[s], recv_sem=recv_sems.at[s],
                    device_id=(px, py, c), device_id_type=_MESH)
                (sends if k < n else small_sends).append(go)
                (arrivals if k < n else small_arrivals).append(come)

        def owned(copies, side):
            return [cp for j, cp in enumerate(copies) if owners[j % ns] == side]

        for cp in sends:
            cp.start()
        for side in (0, 1):
            @pl.when(c == side)
            def _():
                for cp in owned(small_sends, side):
                    cp.start()
        for cp in arrivals:
            cp.wait_recv()
        for cp in sends:
            cp.wait_send()
        for side in (0, 1):
            @pl.when(c == side)
            def _():
                for cp in owned(small_arrivals, side):
                    cp.wait_recv()
                for cp in owned(small_sends, side):
                    cp.wait_send()

    m = len(_CHIP_RELS) * (n + ns)
    outs = pl.pallas_call(
        body, name=name, in_specs=[_ANY] * (n + ns), out_specs=[_ANY] * (n + ns),
        out_shape=[_sds((N_CHIPS,) + _piece_shape(t.shape, k), t.dtype) for t, k in zip(ts, kinds)]
        + [_sds((N_CHIPS,) + s.shape, s.dtype) for s in smalls],
        scratch_shapes=[pltpu.SemaphoreType.DMA((m,)), pltpu.SemaphoreType.DMA((m,))],
    )(*ts, *smalls)
    return outs[:n], outs[n:]


def share_halves(rs, smalls, owners, name):
    n, ns = len(rs), len(smalls)

    def body(*refs):
        o_refs, so_refs = refs[n + ns:2 * n + ns], refs[2 * n + ns:2 * (n + ns)]
        send_sems, recv_sems = refs[2 * (n + ns):]
        x, y, c = lax.axis_index("x"), lax.axis_index("y"), lax.axis_index("c")
        sibling = (x, y, 1 - c)
        swaps = [pltpu.make_async_remote_copy(
            src_ref=o_refs[k].at[c], dst_ref=o_refs[k].at[c], send_sem=send_sems.at[k], recv_sem=recv_sems.at[k],
            device_id=sibling, device_id_type=_MESH) for k in range(n)]
        arrivals = [pltpu.make_async_remote_copy(
            src_ref=o_refs[k].at[c], dst_ref=o_refs[k].at[1 - c], send_sem=send_sems.at[k], recv_sem=recv_sems.at[k],
            device_id=sibling, device_id_type=_MESH) for k in range(n)]
        gives = [pltpu.make_async_remote_copy(
            src_ref=so_refs[k], dst_ref=so_refs[k], send_sem=send_sems.at[n + k], recv_sem=recv_sems.at[n + k],
            device_id=sibling, device_id_type=_MESH) for k in range(ns)]
        for cp in swaps:
            cp.start()
        for side in (0, 1):
            @pl.when(c == side)
            def _():
                for k in _owned_by(owners, side):
                    gives[k].start()
        for cp in arrivals:
            cp.wait_recv()
        for cp in swaps:
            cp.wait_send()
        for side in (0, 1):
            @pl.when(c == side)
            def _():
                for k in _owned_by(owners, side):
                    gives[k].wait_send()

            @pl.when(c == 1 - side)
            def _():
                for k in _owned_by(owners, side):
                    gives[k].wait_recv()

    outs = pl.pallas_call(
        body, name=name, in_specs=[_ANY] * (n + ns), out_specs=[_ANY] * (n + ns),
        out_shape=[_sds(r.shape, r.dtype) for r in list(rs) + list(smalls)],
        input_output_aliases={k: k for k in range(n + ns)},
        scratch_shapes=[pltpu.SemaphoreType.DMA((n + ns,)), pltpu.SemaphoreType.DMA((n + ns,))],
    )(*rs, *smalls)
    return outs[:n], outs[n:]


def _small_view(a):
    if a.size < 8 * LANE:
        return jnp.pad(a.reshape(-1), (0, 8 * LANE - a.size)).reshape(8, LANE)
    if a.ndim == 1:
        return a.reshape(1, a.shape[0])
    if a.ndim == 4 and a.shape[-1] < LANE:
        return a.reshape(a.shape[0], a.shape[1], a.shape[2] * a.shape[3])
    return a


def _permuted_from_shards(shards):
    parts = []
    for lo, hi in GROUP_COLS:
        for q in range(N_CHIPS):
            a, b = max(lo, q * SHARD_COLS), min(hi, (q + 1) * SHARD_COLS)
            if a < b:
                parts.append(shards[q][..., a - q * SHARD_COLS:b - q * SHARD_COLS])
    pad = jnp.zeros(shards[0].shape[:-1] + (D_IN_PAD - D_IN,), shards[0].dtype)
    return jnp.concatenate(parts + [pad], axis=-1)


def _shards_from_groups(groups):
    in_order = sorted(range(len(GROUP_COLS)), key=lambda j: GROUP_COLS[j][0])
    shards = []
    for q in range(N_CHIPS):
        parts = []
        for j in in_order:
            lo, hi = GROUP_COLS[j]
            a, b = max(lo, q * SHARD_COLS), min(hi, (q + 1) * SHARD_COLS)
            if a < b:
                parts.append(groups[j][..., a - lo:b - lo])
        shards.append(jnp.concatenate(parts, axis=-1))
    return shards


def _expand_b(b):
    eye = jnp.eye(SSM_GROUPS, dtype=b.dtype)
    return jnp.einsum("gnc,gh->gchn", b, eye).reshape(D_SSM, N_STATE)


def _extract_b(e):
    return jnp.einsum("gcgn->gnc", e.reshape(SSM_GROUPS, SSM_GROUP, SSM_GROUPS, SSM_STATE))


def _expand_c(c):
    eye = jnp.eye(SSM_GROUPS, dtype=c.dtype)
    return jnp.einsum("gcn,gh->gnhc", c, eye).reshape(N_STATE, D_SSM)


def _extract_c(e):
    return jnp.einsum("gngc->gcn", e.reshape(SSM_GROUPS, SSM_STATE, SSM_GROUPS, SSM_GROUP))


def _lane_row(v):
    return jnp.pad(v, (0, LANE - v.shape[0])).reshape(1, LANE)


def _layer_params(w, l):
    return dict(
        norm_g=w["norm_g"][l][None], win=w["w_in_perm"][l], wout=w["w_out"][l].astype(_MXU),
        pg=w["ple_norm_g"][l][None], wgate=w["w_ple_gate"][l].astype(_MXU), wple=w["w_ple"][l].astype(_MXU),
        are=w["ssm_a_re"][l].reshape(1, N_STATE), aim=w["ssm_a_im"][l].reshape(1, N_STATE),
        ls=jnp.repeat(w["ssm_log_step"][l], SSM_STATE).reshape(1, N_STATE),
        bre=_expand_b(w["ssm_b_re"][l]), bim=_expand_b(w["ssm_b_im"][l]),
        cr=_expand_c(w["ssm_c_re"][l]), ci=_expand_c(w["ssm_c_im"][l]),
        dr=w["ssm_d"][l].reshape(1, D_SSM), wglu=w["ssm_w_glu"][l].astype(f32), bglu=w["ssm_b_glu"][l][None],
        convw=w["dn_conv_w"][l], alog=_lane_row(w["dn_a_log"][l]), dtb=_lane_row(w["dn_dt_bias"][l]),
        ng=w["dn_norm_g"][l][None],
        lng=w["sg_ln_g"][l][None], lnb=w["sg_ln_b"][l][None], sgw=w["sg_w"][l],
        bt=jnp.pad(w["sg_b"][l].T, ((0, 0), (0, LANE - SG_HEADS))),
    )


def _layer_fwd(x, p, lp, nb, tag):
    seq = x.shape[0] // nb
    h, zs, zq, zg, zsg, zab = in_fwd(x, lp["norm_g"], lp["win"], f"in_fwd{tag}")
    prep = s5_prep_fwd(lp["are"], lp["aim"], lp["ls"], lp["bre"], lp["bim"], f"s5_prep_fwd{tag}")
    s5p = tuple(prep) + (lp["cr"], lp["ci"], lp["dr"], lp["wglu"], lp["bglu"])
    ys, hs = s5_fwd(zs, s5p, nb, f"s5_fwd{tag}")
    qkv, loc, inv = dn_front_fwd(zq, lp["convw"], zab, lp["alog"], lp["dtb"], seq, f"dn_front_fwd{tag}")
    yd, ss = dn_scan_fwd(loc, zg, lp["ng"], nb, f"dn_scan_fwd{tag}")
    yg = sg_fwd(zsg, lp["lng"], lp["lnb"], lp["sgw"], lp["bt"], f"sg_fwd{tag}")
    x2, x1, y, hn = post_fwd(x, ys, yd, yg, p, lp["wout"], lp["pg"], lp["wgate"], lp["wple"], f"post_fwd{tag}")
    saved = dict(x=x, h=h, zs=zs, zq=zq, zg=zg, zsg=zsg, zab=zab, s5p=s5p, hs=hs, qkv=qkv, loc=loc, inv=inv, ss=ss, x1=x1, y=y, hn=hn, p=p)
    return x2, saved


def _layer_bwd(dx2, sv, lp, nb, tag):
    seq = dx2.shape[0] // nb
    dx1, dgp, dpp, dys, dyd, dyg, dpg = post_bwd(dx2, sv["x1"], sv["hn"], sv["p"], lp["wout"], lp["pg"], lp["wgate"],
                                                 lp["wple"], f"post_bwd{tag}")
    g = {}
    g["w_out"] = wgrad(sv["y"], dx1, f"wgrad_out{tag}")
    g["w_ple_gate"] = wgrad(sv["hn"], dgp, f"wgrad_gate{tag}")
    g["w_ple"] = wgrad(sv["p"], dpp, f"wgrad_ple{tag}")
    g["ple_norm_g"] = dpg[0]
    dzsg, dlng, dlnb, dsgw, dbt = sg_bwd(sv["zsg"], lp["lng"], lp["lnb"], lp["sgw"], lp["bt"], dyg, f"sg_bwd{tag}")
    g["sg_ln_g"], g["sg_ln_b"], g["sg_w"], g["sg_b"] = dlng[0], dlnb[0], dsgw, dbt[:, :SG_HEADS].T
    dloc, dzg, dng = dn_scan_bwd(sv["loc"], sv["zg"], lp["ng"], sv["ss"], dyd, nb, f"dn_scan_bwd{tag}")
    dqkv, dzab, dalog, ddtb = dn_local_bwd(sv["qkv"], sv["zab"], lp["alog"], lp["dtb"], sv["inv"], dloc,
                                           f"dn_local_bwd{tag}")
    dzq, dconv = dn_pre_bwd(sv["zq"], lp["convw"], dqkv, seq, f"dn_pre_bwd{tag}")
    g["dn_conv_w"], g["dn_a_log"], g["dn_dt_bias"], g["dn_norm_g"] = dconv, dalog[0, :DN_HEADS], ddtb[0, :DN_HEADS], dng[0]
    s5out = s5_bwd(sv["zs"], sv["s5p"], sv["hs"], dys, nb, f"s5_bwd{tag}")
    dzs, dprep, (dcr, dci, ddr, dwglu, dbglu) = s5out[0], s5out[1:1 + S5_PREPARED], s5out[1 + S5_PREPARED:]
    dare, daim, dls, dbre, dbim = s5_prep_bwd(lp["are"], lp["aim"], lp["ls"], lp["bre"], lp["bim"], dprep,
                                              f"s5_prep_bwd{tag}")
    g["ssm_a_re"] = dare.reshape(SSM_GROUPS, SSM_STATE)
    g["ssm_a_im"] = daim.reshape(SSM_GROUPS, SSM_STATE)
    g["ssm_log_step"] = dls.reshape(SSM_GROUPS, SSM_STATE).sum(axis=1)
    g["ssm_b_re"], g["ssm_b_im"] = _extract_b(dbre), _extract_b(dbim)
    g["ssm_c_re"], g["ssm_c_im"] = _extract_c(dcr), _extract_c(dci)
    g["ssm_d"] = ddr.reshape(SSM_GROUPS, SSM_GROUP)
    g["ssm_w_glu"], g["ssm_b_glu"] = dwglu, dbglu[0]
    dzs_all = (dzs, dzq, dzg, dzsg, dzab)
    dx, dng_in = in_bwd(sv["x"], lp["norm_g"], lp["win"], dzs_all, dx1, f"in_bwd{tag}")
    g["w_in_pieces"] = [wgrad(sv["h"], dz, f"wgrad_in{k}{tag}") for k, dz in enumerate(dzs_all)]
    g["norm_g"] = dng_in[0]
    return dx, g


def _local_step(x, p, target, w, nb):
    lps = [_layer_params(w, l) for l in range(DEPTH)]
    saved = []
    for l in range(DEPTH):
        x, sv = _layer_fwd(x, p[l], lps[l], nb, f"_l{l}")
        saved.append(sv)
    loss_blk, dx, dfg = loss_fwd_bwd(x, w["final_norm_g"][None], target, "loss")
    grads = [None] * DEPTH
    for l in reversed(range(DEPTH)):
        dx, grads[l] = _layer_bwd(dx, saved[l], lps[l], nb, f"_l{l}")
    out = {k: jnp.stack([grads[l][k] for l in range(DEPTH)]) for k in grads[0] if k != "w_in_pieces"}
    out["w_in_pieces"] = [grads[l]["w_in_pieces"] for l in range(DEPTH)]
    out["final_norm_g"] = dfg[0]
    return loss_blk[0, 0], dx, out


def kernel(x, p, norm_g, w_in, ssm_a_re, ssm_a_im, ssm_b_re, ssm_b_im, ssm_c_re, ssm_c_im, ssm_d, ssm_log_step, ssm_w_glu, ssm_b_glu, dn_conv_w, dn_a_log, dn_dt_bias, dn_norm_g, sg_ln_g, sg_ln_b, sg_w, sg_b, w_out, ple_norm_g, w_ple_gate, w_ple, final_norm_g, loss_target, m_norm_g, m_w_in, m_ssm_a_re, m_ssm_a_im, m_ssm_b_re, m_ssm_b_im, m_ssm_c_re, m_ssm_c_im, m_ssm_d, m_ssm_log_step, m_ssm_w_glu, m_ssm_b_glu, m_dn_conv_w, m_dn_a_log, m_dn_dt_bias, m_dn_norm_g, m_sg_ln_g, m_sg_ln_b, m_sg_w, m_sg_b, m_w_out, m_ple_norm_g, m_w_ple_gate, m_w_ple, m_final_norm_g, v_norm_g, v_w_in, v_ssm_a_re, v_ssm_a_im, v_ssm_b_re, v_ssm_b_im, v_ssm_c_re, v_ssm_c_im, v_ssm_d, v_ssm_log_step, v_ssm_w_glu, v_ssm_b_glu, v_dn_conv_w, v_dn_a_log, v_dn_dt_bias, v_dn_norm_g, v_sg_ln_g, v_sg_ln_b, v_sg_w, v_sg_b, v_w_out, v_ple_norm_g, v_w_ple_gate, v_w_ple, v_final_norm_g):
    args = locals()
    w = {n: args[n] for n in WEIGHTS}
    m = {n: args["m_" + n] for n in WEIGHTS}
    v = {n: args["v_" + n] for n in WEIGHTS}
    nb, seq = x.shape[0], x.shape[1]
    t = nb * seq

    full = _gather_full(w)
    loss_local, dx, grads = _local_step(x.reshape(t, D_MODEL), p.reshape(DEPTH, t, D_PLE),
                                        loss_target.reshape(t, D_MODEL), full, nb)
    outs, loss = _reduce_and_update(grads, w, m, v, loss_local)
    return (loss, dx.reshape(nb, seq, D_MODEL), *[outs[0][n] for n in WEIGHTS], *[outs[1][n] for n in WEIGHTS],
            *[outs[2][n] for n in WEIGHTS], *[outs[3][n] for n in WEIGHTS])


def _gather_full(w):
    sh_names = [n for n, _ in SHARDED]
    shards = [w[n] if n == "dn_conv_w" else w[n].astype(_COMM) for n in sh_names]
    gathered = gather_weights(shards, [k for _, k in SHARDED], "gather_weights")
    chip = 2 * lax.axis_index("x") + lax.axis_index("y")
    full = {n: w[n] for n in REPLICATED}
    for (n, kind), shard, got in zip(SHARDED, shards, gathered):
        if kind[0] == "slot":
            full[n] = lax.dynamic_update_index_in_dim(got, shard, chip, 0)
        else:
            full[n] = lax.dynamic_update_slice_in_dim(got, shard, chip * kind[2], axis=kind[1] + 1)
    slots = full.pop("w_in")
    full["w_in_perm"] = _permuted_from_shards([slots[q] for q in range(N_CHIPS)]).astype(_MXU)
    return full


def _reduce_and_update(grads, w, m, v, loss_local):
    sh_names = [n for n, _ in SHARDED]
    sh_kinds = [k for _, k in SHARDED]
    owners = [SMALL_OWNER[n] for n in REPLICATED + ("loss",)]

    def small_views(d):
        return [_small_view(d[n]) for n in REPLICATED]

    grads["w_in"] = jnp.stack([jnp.stack(_shards_from_groups(pieces)) for pieces in grads["w_in_pieces"]])
    gs = [grads[n] if n == "dn_conv_w" else grads[n].astype(_COMM) for n in sh_names]
    sm = small_views(grads) + [_small_view(loss_local.reshape(1))]
    core = lax.axis_index("c")
    chip = 2 * lax.axis_index("x") + lax.axis_index("y")
    got, sm_got = exchange_halves(gs, sm, owners, "exchange_halves")
    sums = add_pairs([lax.dynamic_index_in_dim(g, core, 0, keepdims=False) for g in gs] + sm, list(got) + list(sm_got),
                     "add_halves")
    sums, sm_sums = sums[:len(gs)], sums[len(gs):]
    parts, sm_parts = reduce_to_chips(sums, sh_kinds, sm_sums, owners, "reduce_to_chips")
    parts = list(parts)
    for k, (kind, total) in enumerate(zip(sh_kinds, sums)):
        if kind[0] == "slot":
            own = lax.dynamic_index_in_dim(total, chip, 0, keepdims=False)
        else:
            own = lax.dynamic_slice_in_dim(total, chip * kind[2], kind[2], axis=kind[1])
        parts[k] = lax.dynamic_update_index_in_dim(parts[k], own, chip, 0)
    half = core.astype(jnp.int32).reshape(1)
    totals = [sum_parts(half, part, f"sum_{n}") for n, part in zip(sh_names, parts)]
    sm_totals = sum_parts_small(chip.astype(jnp.int32).reshape(1), sm_parts, sm_sums, "sum_replicated")
    g_big, g_small = share_halves(totals, sm_totals, owners, "share_halves")
    outs = [dict(zip(sh_names, g_big)), {}, {}, {}]
    for n, g in zip(sh_names, g_big):
        outs[1][n], outs[2][n], outs[3][n] = adamw(g, w[n], m[n], v[n], f"adamw_{n}")
    small_results = [g_small[:-1]] + adamw_small(g_small[:-1], small_views(w), small_views(m), small_views(v),
                                                 "adamw_replicated")
    for j in range(4):
        for n, r in zip(REPLICATED, small_results[j]):
            outs[j][n] = r.reshape(-1)[:w[n].size].reshape(w[n].shape)
    return outs, g_small[-1][0, 0]
```

```python
import functools

import jax
import jax.numpy as jnp
from jax import lax
from jax.experimental import pallas as pl
from jax.experimental.pallas import tpu as pltpu

f32 = jnp.float32
bf16 = jnp.bfloat16

_MXU = bf16
_COMM = bf16
HIGH = lax.Precision.HIGH

D_MODEL = 1024
DEPTH = 2
D_PLE = 256
D_SSM = 256
D_DN = 512
D_SG = 256
SSM_GROUPS = 16
SSM_GROUP = 16
SSM_STATE = 64
N_STATE = SSM_GROUPS * SSM_STATE
DN_HEADS = 4
DN_HEAD_DIM = 128
DN_CONV = 4
DN_HALO = 16
DN_CHUNK = 64
SG_HEADS = 4
SG_HEAD_DIM = 64
SG_CHUNK = 128
S5_CHUNK = 1024
S5_GROUP_ROWS = 8
EPS = 1e-6
D_IN = 3336
D_IN_PAD = 3456
LANE = 128

ADAM_LR = 0.001
ADAM_B1 = 0.9
ADAM_B2 = 0.999
ADAM_EPS = 1e-08
ADAM_WD = 0.01
ADAM_STEP = 10

N_CHIPS = 4

Z_COLS = ((0, 512), (512, 2048), (2048, 2560), (2560, 3328), (3328, 3456))

GROUP_COLS = ((0, 512), (512, 2048), (2056, 2568), (2568, 3336), (2048, 2056))
SHARD_COLS = D_IN // 4

SHARDED = (("w_in", ("slot",)), ("ssm_w_glu", ("win", 0, 64)), ("dn_conv_w", ("win", 1, 384)),
           ("w_out", ("win", 0, 256)), ("w_ple_gate", ("win", 0, 256)), ("w_ple", ("win", 1, 256)))
REPLICATED = ("norm_g", "ssm_a_re", "ssm_a_im", "ssm_b_re", "ssm_b_im", "ssm_c_re", "ssm_c_im", "ssm_d",
              "ssm_log_step", "ssm_b_glu", "dn_a_log", "dn_dt_bias", "dn_norm_g", "sg_ln_g", "sg_ln_b", "sg_w",
              "sg_b", "ple_norm_g", "final_norm_g")
SMALL_OWNER = {n: int(n.startswith("ssm_")) for n in REPLICATED + ("loss",)}
WEIGHTS = ("norm_g", "w_in", "ssm_a_re", "ssm_a_im", "ssm_b_re", "ssm_b_im", "ssm_c_re", "ssm_c_im", "ssm_d",
           "ssm_log_step", "ssm_w_glu", "ssm_b_glu", "dn_conv_w", "dn_a_log", "dn_dt_bias", "dn_norm_g", "sg_ln_g",
           "sg_ln_b", "sg_w", "sg_b", "w_out", "ple_norm_g", "w_ple_gate", "w_ple", "final_norm_g")

VMEM_BIG = 56 * 1024 * 1024


def _mm(a, b):
    return jnp.dot(a.astype(_MXU), b.astype(_MXU), preferred_element_type=f32)


def _mm_nt(a, b):
    return lax.dot_general(a.astype(_MXU), b.astype(_MXU), (((1,), (1,)), ((), ())), preferred_element_type=f32)


def _mm_tn(a, b):
    return lax.dot_general(a.astype(_MXU), b.astype(_MXU), (((0,), (0,)), ((), ())), preferred_element_type=f32)


@jax.custom_vjp
def bdot(a, b):
    return _mm(a, b)


def _bdot_fwd(a, b):
    return _mm(a, b), (a, b)


def _bdot_bwd(res, g):
    a, b = res
    return _mm_nt(g, b).astype(a.dtype), _mm_tn(a, g).astype(b.dtype)


bdot.defvjp(_bdot_fwd, _bdot_bwd)


@jax.custom_vjp
def bdot_nt(a, b):
    return _mm_nt(a, b)


def _bdot_nt_fwd(a, b):
    return _mm_nt(a, b), (a, b)


def _bdot_nt_bwd(res, g):
    a, b = res
    return _mm(g, b).astype(a.dtype), _mm_tn(g, a).astype(b.dtype)


bdot_nt.defvjp(_bdot_nt_fwd, _bdot_nt_bwd)


@jax.custom_vjp
def bdot_tn(a, b):
    return _mm_tn(a, b)


def _bdot_tn_fwd(a, b):
    return _mm_tn(a, b), (a, b)


def _bdot_tn_bwd(res, g):
    a, b = res
    return _mm_nt(b, g).astype(a.dtype), _mm(a, g).astype(b.dtype)


bdot_tn.defvjp(_bdot_tn_fwd, _bdot_tn_bwd)


def hdot(a, b):
    return jnp.dot(a, b, precision=HIGH, preferred_element_type=f32)


def _unit_lower_inverses(ms):
    n = ms[0].shape[0]
    eye = (lax.broadcasted_iota(jnp.int32, (n, n), 0) == lax.broadcasted_iota(jnp.int32, (n, n), 1)).astype(f32)
    pw = [-m for m in ms]
    inv = [eye + p for p in pw]
    for _ in range(n.bit_length() - 2):
        pw = [hdot(p, p) for p in pw]
        inv = [a + hdot(a, p) for a, p in zip(inv, pw)]
    return inv


@jax.custom_vjp
def solve_unit_lower(ms, rhs, inv):
    return [hdot(a, r) for a, r in zip(inv, rhs)]


def _solve_unit_lower_fwd(ms, rhs, inv):
    xs = [hdot(a, r) for a, r in zip(inv, rhs)]
    return xs, (inv, xs)


def _solve_unit_lower_bwd(res, gs):
    inv, xs = res
    d_rhs = [lax.dot_general(a, g, (((0,), (0,)), ((), ())), precision=HIGH, preferred_element_type=f32)
             for a, g in zip(inv, gs)]
    d_ms = [-lax.dot_general(d, x, (((1,), (1,)), ((), ())), precision=HIGH, preferred_element_type=f32)
            for d, x in zip(d_rhs, xs)]
    return d_ms, d_rhs, [jnp.zeros_like(a) for a in inv]


solve_unit_lower.defvjp(_solve_unit_lower_fwd, _solve_unit_lower_bwd)


@functools.partial(jax.custom_vjp, nondiff_argnums=(1,))
def roll_rows(x, k):
    return pltpu.roll(x, k, 0)


def _roll_rows_fwd(x, k):
    return pltpu.roll(x, k, 0), None


def _roll_rows_bwd(k, _, g):
    return (pltpu.roll(g, g.shape[0] - k, 0),)


roll_rows.defvjp(_roll_rows_fwd, _roll_rows_bwd)


def _row_ids(shape):
    return lax.broadcasted_iota(jnp.int32, shape, 0)


def _rms(x, g):
    return x * lax.rsqrt(jnp.mean(x * x, axis=-1, keepdims=True) + EPS) * g


def _layer_norm(x, g, b):
    mu = jnp.mean(x, axis=-1, keepdims=True)
    xc = x - mu
    return xc * lax.rsqrt(jnp.mean(xc * xc, axis=-1, keepdims=True) + EPS) * g + b


def _s5_prep(are, aim, ls, bre, bim):
    step = jnp.exp(ls)
    mag = jnp.exp(are * step)
    lr = mag * jnp.cos(aim * step)
    li = mag * jnp.sin(aim * step)
    den = are * are + aim * aim
    nr, ni = lr - 1.0, li
    fr = (nr * are + ni * aim) / den
    fi = (ni * are - nr * aim) / den
    bbr = fr * bre - fi * bim
    bbi = fr * bim + fi * bre
    pr = jnp.broadcast_to(lr, (S5_GROUP_ROWS, N_STATE))
    pi = jnp.broadcast_to(li, (S5_GROUP_ROWS, N_STATE))
    d = 1
    while d < S5_GROUP_ROWS:
        keep = _row_ids(pr.shape) >= d
        sr, si = roll_rows(pr, d), roll_rows(pi, d)
        pr, pi = jnp.where(keep, pr * sr - pi * si, pr), jnp.where(keep, pr * si + pi * sr, pi)
        d *= 2
    return pr, pi, bbr, bbi


def _s5_chunk(u, gate, hr, hi, pr, pi, bbr, bbi, cr, ci, dr, wglu, bglu):
    n, steps = u.shape[0], S5_GROUP_ROWS
    groups = n // steps
    xr = bdot(u, bbr)
    xi = bdot(u, bbi)
    lr, li = pr[0:1], pi[0:1]
    rs, ims = [xr[:groups]], [xi[:groups]]
    for t in range(1, steps):
        a, b = rs[-1], ims[-1]
        rs.append(xr[t * groups:(t + 1) * groups] + lr * a - li * b)
        ims.append(xi[t * groups:(t + 1) * groups] + lr * b + li * a)
    er, ei = rs[-1], ims[-1]
    mr, mi = pr[steps - 1:steps], pi[steps - 1:steps]
    gid = _row_ids(er.shape)
    er, ei = (er + jnp.where(gid == 0, mr * hr - mi * hi, 0.0), ei + jnp.where(gid == 0, mr * hi + mi * hr, 0.0))
    d = 1
    while d < groups:
        sr = jnp.where(gid >= d, roll_rows(er, d), 0.0)
        si = jnp.where(gid >= d, roll_rows(ei, d), 0.0)
        er, ei = er + mr * sr - mi * si, ei + mr * si + mi * sr
        mr, mi = mr * mr - mi * mi, 2.0 * mr * mi
        d *= 2
    before_r = jnp.where(gid == 0, hr, roll_rows(er, 1))
    before_i = jnp.where(gid == 0, hi, roll_rows(ei, 1))
    xr = jnp.concatenate([rs[t] + pr[t:t + 1] * before_r - pi[t:t + 1] * before_i for t in range(steps)], axis=0)
    xi = jnp.concatenate([ims[t] + pr[t:t + 1] * before_i + pi[t:t + 1] * before_r for t in range(steps)], axis=0)
    y = bdot(xr, cr) - bdot(xi, ci) + dr * u
    y = jax.nn.gelu(y)
    y = y * jax.nn.sigmoid(bdot(y, wglu) + bglu)
    return y * jax.nn.silu(gate), er[groups - 1:groups], ei[groups - 1:groups]


def _dn_pre(xc, xp, w0, w1, w2, w3, is_start, col):
    xp = jnp.where(is_start, 0.0, xp)
    halo_rows = _row_ids(xp.shape)
    acc = w3 * xc
    for d, w in ((1, w2), (2, w1), (3, w0)):
        r = roll_rows(xc, d)
        head = jnp.where(halo_rows >= d, r[:DN_HALO], roll_rows(xp, d))
        acc = acc + w * jnp.concatenate([head, r[DN_HALO:]], axis=0)
    y = jax.nn.silu(acc)
    if col >= 2 * DN_HEADS:
        return y
    nrm = y * lax.rsqrt(jnp.sum(y * y, axis=-1, keepdims=True) + EPS)
    return nrm * DN_HEAD_DIM ** -0.5 if col < DN_HEADS else nrm


def _dn_local(qs, ks, vs, abs_, alog, dtb, invs=None):
    c = DN_CHUNK
    ri = lax.broadcasted_iota(jnp.int32, (c, c), 0)
    ci = lax.broadcasted_iota(jnp.int32, (c, c), 1)
    causal, strict = ri >= ci, ri > ci
    tril = causal.astype(f32)
    gcums = [hdot(tril, -jnp.exp(alog) * jax.nn.softplus(ab + dtb)) for ab in abs_]
    gcum_ts = [g.T for g in gcums]
    sigs = [jax.nn.sigmoid(ab) for ab in abs_]
    chains = [(j, h) for j in range(len(abs_)) for h in range(DN_HEADS)]
    gc = [gcums[j][:, h:h + 1] for j, h in chains]
    decay = [jnp.where(causal, jnp.exp(jnp.where(causal, gc[n] - gcum_ts[j][h:h + 1, :], 0.0)), 0.0)
             for n, (j, h) in enumerate(chains)]
    beta = [sigs[j][:, DN_HEADS + h:DN_HEADS + h + 1] for j, h in chains]
    kb = [ks[j][h] * beta[n] for n, (j, h) in enumerate(chains)]
    ms = [jnp.where(strict, bdot_nt(kb[n], ks[j][h]) * decay[n], 0.0) for n, (j, h) in enumerate(chains)]
    egc = [jnp.exp(g) for g in gc]
    rhs = [jnp.concatenate([vs[j][h] * beta[n], kb[n] * egc[n]], axis=1) for n, (j, h) in enumerate(chains)]
    inv = _unit_lower_inverses(ms) if invs is None else [invs[j][h] for j, h in chains]
    sol = solve_unit_lower(ms, rhs, inv)
    values = [s[:, :DN_HEAD_DIM] for s in sol]
    k_cds = [s[:, DN_HEAD_DIM:] for s in sol]
    attns = [bdot_nt(qs[j][h], ks[j][h]) * decay[n] for n, (j, h) in enumerate(chains)]
    q_decs = [qs[j][h] * egc[n] for n, (j, h) in enumerate(chains)]
    k_decs = [ks[j][h] * jnp.exp(gc[n][c - 1:c, :] - gc[n]) for n, (j, h) in enumerate(chains)]

    def nest(flat):
        return [flat[j * DN_HEADS:(j + 1) * DN_HEADS] for j in range(len(abs_))]

    lasts = [jnp.exp(g[c - 1:c, :]) for g in gcums]
    return nest(values), nest(k_cds), nest(attns), nest(q_decs), nest(k_decs), lasts, nest(inv)


def _dn_step(values, k_cds, attns, q_decs, k_decs, lasts, ggs, sts, ng):
    v_new = [v - bdot(kc, st) for v, kc, st in zip(values, k_cds, sts)]
    o = [bdot(qd, st) for qd, st in zip(q_decs, sts)]
    o = [a + bdot(at, vn) for a, at, vn in zip(o, attns, v_new)]
    new = [st * la + bdot_tn(kd, vn) for st, la, kd, vn in zip(sts, lasts, k_decs, v_new)]
    return [_rms(a, ng) * jax.nn.silu(g) for a, g in zip(o, ggs)], new


def _sg_chunk(u, v, gate, lng, lnb, ws, bt):
    n = SG_CHUNK
    ug = jax.nn.gelu(u)
    vn = _layer_norm(jax.nn.gelu(v), lng, lnb)
    causal = lax.broadcasted_iota(jnp.int32, (n, n), 0) >= lax.broadcasted_iota(jnp.int32, (n, n), 1)
    lane = lax.broadcasted_iota(jnp.int32, (n, D_SG), 1)
    s = jnp.zeros((n, D_SG), f32)
    for h in range(SG_HEADS):
        t = bdot(jnp.where(causal, ws[h], 0.0), vn) + bt[:, h:h + 1]
        s = s + jnp.where((lane >= h * SG_HEAD_DIM) & (lane < (h + 1) * SG_HEAD_DIM), t, 0.0)
    return ug * s * jax.nn.silu(gate)


def _cp(n_grid, vmem=None):
    return pltpu.CompilerParams(dimension_semantics=("arbitrary",) * n_grid, vmem_limit_bytes=vmem)


def _full(shape):
    nd = len(shape)
    return pl.BlockSpec(tuple(shape), lambda *_: (0,) * nd)


def _rows(tm, ncol):
    return pl.BlockSpec((tm, ncol), lambda i: (i, 0))


def _sds(shape, dtype=f32):
    return jax.ShapeDtypeStruct(tuple(shape), dtype)


def _acc(ref, val, first):
    @pl.when(first)
    def _():
        ref[...] = val

    @pl.when(jnp.logical_not(first))
    def _():
        ref[...] += val


def in_fwd(x, g, w, name):
    t, tm = x.shape[0], 512

    def body(x_ref, g_ref, w_ref, h_ref, *z_refs):
        h = _rms(x_ref[...], g_ref[...]).astype(_MXU)
        h_ref[...] = h
        for z_ref, (a, b) in zip(z_refs, Z_COLS):
            z_ref[...] = jnp.dot(h, w_ref[:, a:b], preferred_element_type=f32)

    widths = [b - a for a, b in Z_COLS]
    return pl.pallas_call(
        body, name=name, grid=(t // tm,),
        in_specs=[_rows(tm, D_MODEL), _full((1, D_MODEL)), _full((D_MODEL, D_IN_PAD))],
        out_specs=[_rows(tm, D_MODEL)] + [_rows(tm, n) for n in widths],
        out_shape=[_sds((t, D_MODEL), _MXU)] + [_sds((t, n)) for n in widths],
        compiler_params=_cp(1, VMEM_BIG),
    )(x, g, w)


def in_bwd(x, g, w, dzs, dres, name):
    t, tm = x.shape[0], 512
    widths = [b - a for a, b in Z_COLS]

    def body(x_ref, g_ref, w_ref, dres_ref, *rest):
        dz_refs, (dx_ref, dg_ref) = rest[:5], rest[5:]
        dh = jnp.zeros((tm, D_MODEL), f32)
        for dz_ref, (a, b) in zip(dz_refs, Z_COLS):
            dh = dh + _mm_nt(dz_ref[...], w_ref[:, a:b])
        _, vj = jax.vjp(_rms, x_ref[...], g_ref[...])
        dx, dg = vj(dh)
        dx_ref[...] = dres_ref[...] + dx
        _acc(dg_ref, dg, pl.program_id(0) == 0)

    return pl.pallas_call(
        body, name=name, grid=(t // tm,),
        in_specs=[_rows(tm, D_MODEL), _full((1, D_MODEL)), _full((D_MODEL, D_IN_PAD)), _rows(tm, D_MODEL)]
        + [_rows(tm, n) for n in widths],
        out_specs=[_rows(tm, D_MODEL), _full((1, D_MODEL))],
        out_shape=[_sds((t, D_MODEL)), _sds((1, D_MODEL))],
        compiler_params=_cp(1, VMEM_BIG),
    )(x, g, w, dres, *dzs)


def wgrad(a, g, name):
    t, k = a.shape
    n = g.shape[1]
    tm = min(t, 2048)
    tn = n if n <= 1536 else n // 2
    steps = t // tm

    def body(a_ref, g_ref, o_ref, acc):
        i = pl.program_id(1)
        _acc(acc, _mm_tn(a_ref[...], g_ref[...]), i == 0)

        @pl.when(i == steps - 1)
        def _():
            o_ref[...] = acc[...].astype(o_ref.dtype)

    return pl.pallas_call(
        body, name=name, grid=(n // tn, steps),
        in_specs=[pl.BlockSpec((tm, k), lambda j, i: (i, 0)), pl.BlockSpec((tm, tn), lambda j, i: (i, j))],
        out_specs=pl.BlockSpec((k, tn), lambda j, i: (0, j)),
        out_shape=_sds((k, n), _COMM),
        scratch_shapes=[pltpu.VMEM((k, tn), f32)],
        compiler_params=_cp(2, VMEM_BIG),
    )(a, g)


def post_fwd(x, ys, yd, yg, p, wout, pg, wgate, wple, name):
    t, tm = x.shape[0], 512

    def body(x_ref, ys_ref, yd_ref, yg_ref, p_ref, wout_ref, pg_ref, wgate_ref, wple_ref,
             x2_ref, x1_ref, y_ref, hn_ref):
        y = jnp.concatenate([ys_ref[...], yd_ref[...], yg_ref[...]], axis=1).astype(_MXU)
        y_ref[...] = y
        x1 = x_ref[...] + jnp.dot(y, wout_ref[...], preferred_element_type=f32)
        x1_ref[...] = x1
        hn = _rms(x1, pg_ref[...]).astype(_MXU)
        hn_ref[...] = hn
        gp = jnp.dot(hn, wgate_ref[...], preferred_element_type=f32)
        pp = _mm(p_ref[...], wple_ref[...])
        x2_ref[...] = x1 + jax.nn.sigmoid(gp) * pp

    return pl.pallas_call(
        body, name=name, grid=(t // tm,),
        in_specs=[_rows(tm, D_MODEL), _rows(tm, D_SSM), _rows(tm, D_DN), _rows(tm, D_SG), _rows(tm, D_PLE),
                  _full((D_MODEL, D_MODEL)), _full((1, D_MODEL)), _full((D_MODEL, D_MODEL)), _full((D_PLE, D_MODEL))],
        out_specs=[_rows(tm, D_MODEL)] * 4,
        out_shape=[_sds((t, D_MODEL)), _sds((t, D_MODEL)), _sds((t, D_MODEL), _MXU), _sds((t, D_MODEL), _MXU)],
        compiler_params=_cp(1, VMEM_BIG),
    )(x, ys, yd, yg, p, wout, pg, wgate, wple)


def post_bwd(dx2, x1, hn, p, wout, pg, wgate, wple, name):
    t, tm = dx2.shape[0], 512

    def body(dx2_ref, x1_ref, hn_ref, p_ref, wout_ref, pg_ref, wgate_ref, wple_ref,
             dx1_ref, dgp_ref, dpp_ref, dys_ref, dyd_ref, dyg_ref, dpg_ref):
        dx2 = dx2_ref[...]
        gp = jnp.dot(hn_ref[...], wgate_ref[...], preferred_element_type=f32)
        pp = _mm(p_ref[...], wple_ref[...])
        sg = jax.nn.sigmoid(gp)
        dpp_ref[...] = (dx2 * sg).astype(_MXU)
        dgp = (dx2 * pp * sg * (1.0 - sg)).astype(_MXU)
        dgp_ref[...] = dgp
        dhn = _mm_nt(dgp, wgate_ref[...])
        _, vj = jax.vjp(_rms, x1_ref[...], pg_ref[...])
        dx1n, dpg = vj(dhn)
        dx1 = dx2 + dx1n
        dx1_ref[...] = dx1
        dy = _mm_nt(dx1, wout_ref[...])
        dys_ref[...] = dy[:, :D_SSM]
        dyd_ref[...] = dy[:, D_SSM:D_SSM + D_DN]
        dyg_ref[...] = dy[:, D_SSM + D_DN:]
        _acc(dpg_ref, dpg, pl.program_id(0) == 0)

    return pl.pallas_call(
        body, name=name, grid=(t // tm,),
        in_specs=[_rows(tm, D_MODEL), _rows(tm, D_MODEL), _rows(tm, D_MODEL), _rows(tm, D_PLE),
                  _full((D_MODEL, D_MODEL)), _full((1, D_MODEL)), _full((D_MODEL, D_MODEL)), _full((D_PLE, D_MODEL))],
        out_specs=[_rows(tm, D_MODEL), _rows(tm, D_MODEL), _rows(tm, D_MODEL), _rows(tm, D_SSM), _rows(tm, D_DN),
                   _rows(tm, D_SG), _full((1, D_MODEL))],
        out_shape=[_sds((t, D_MODEL)), _sds((t, D_MODEL), _MXU), _sds((t, D_MODEL), _MXU), _sds((t, D_SSM)),
                   _sds((t, D_DN)), _sds((t, D_SG)), _sds((1, D_MODEL))],
        compiler_params=_cp(1, VMEM_BIG),
    )(dx2, x1, hn, p, wout, pg, wgate, wple)


def loss_fwd_bwd(x, fg, target, name):
    t, tm = x.shape[0], 1024

    def body(x_ref, fg_ref, t_ref, loss_ref, dx_ref, dfg_ref):
        def f(xv, gv):
            err = _rms(xv, gv) - t_ref[...]
            return 0.5 * jnp.sum(jnp.mean(err * err, axis=-1))

        val, vj = jax.vjp(f, x_ref[...], fg_ref[...])
        dx, dfg = vj(jnp.ones((), f32))
        dx_ref[...] = dx
        first = pl.program_id(0) == 0
        _acc(dfg_ref, dfg, first)
        _acc(loss_ref, jnp.full((8, LANE), val, f32), first)

    return pl.pallas_call(
        body, name=name, grid=(t // tm,),
        in_specs=[_rows(tm, D_MODEL), _full((1, D_MODEL)), _rows(tm, D_MODEL)],
        out_specs=[_full((8, LANE)), _rows(tm, D_MODEL), _full((1, D_MODEL))],
        out_shape=[_sds((8, LANE)), _sds((t, D_MODEL)), _sds((1, D_MODEL))],
        compiler_params=_cp(1),
    )(x, fg, target)


S5_PREPARED = 4
_S5_PARAM_SHAPES = ((S5_GROUP_ROWS, N_STATE), (S5_GROUP_ROWS, N_STATE), (D_SSM, N_STATE), (D_SSM, N_STATE),
                    (N_STATE, D_SSM), (N_STATE, D_SSM), (1, D_SSM), (D_SSM, D_SSM), (1, D_SSM))

def s5_prep_fwd(are, aim, ls, bre, bim, name):
    def body(are_ref, aim_ref, ls_ref, bre_ref, bim_ref, *outs):
        vals = _s5_prep(are_ref[...], aim_ref[...], ls_ref[...], bre_ref[...], bim_ref[...])
        for o, v in zip(outs, vals):
            o[...] = v

    return pl.pallas_call(body, name=name, out_shape=[_sds(s) for s in _S5_PARAM_SHAPES[:S5_PREPARED]])(
        are, aim, ls, bre, bim)


def s5_prep_bwd(are, aim, ls, bre, bim, cts, name):
    def body(are_ref, aim_ref, ls_ref, bre_ref, bim_ref, *rest):
        ct_refs, outs = rest[:S5_PREPARED], rest[S5_PREPARED:]
        _, vj = jax.vjp(_s5_prep, are_ref[...], aim_ref[...], ls_ref[...], bre_ref[...], bim_ref[...])
        for o, v in zip(outs, vj(tuple(r[...] for r in ct_refs))):
            o[...] = v

    shapes = [(1, N_STATE)] * 3 + [(D_SSM, N_STATE)] * 2
    return pl.pallas_call(body, name=name, out_shape=[_sds(s) for s in shapes])(are, aim, ls, bre, bim, *cts)


def _step_major(ref, cols):
    x = ref[:, cols]
    n, w = x.shape
    return jnp.swapaxes(x.reshape(n // S5_GROUP_ROWS, S5_GROUP_ROWS, w), 0, 1).reshape(n, w)


def _store_step_major(ref, cols, val):
    n, w = val.shape
    ref[:, cols] = jnp.swapaxes(val.reshape(S5_GROUP_ROWS, n // S5_GROUP_ROWS, w), 0, 1).reshape(n, w).astype(ref.dtype)


def s5_fwd(z, params, nb, name):
    t = z.shape[0]
    nc = t // nb // S5_CHUNK
    npar = len(_S5_PARAM_SHAPES)

    def body(z_ref, *rest):
        p_refs, (y_ref, hs_ref, hr_s, hi_s) = rest[:npar], rest[npar:]

        @pl.when(pl.program_id(1) == 0)
        def _():
            hr_s[...] = jnp.zeros_like(hr_s)
            hi_s[...] = jnp.zeros_like(hi_s)

        hr, hi = hr_s[...], hi_s[...]
        hs_ref[0, :, :N_STATE] = hr
        hs_ref[0, :, N_STATE:] = hi
        y, nhr, nhi = _s5_chunk(_step_major(z_ref, slice(0, D_SSM)), _step_major(z_ref, slice(D_SSM, 2 * D_SSM)),
                                hr, hi, *[r[...] for r in p_refs])
        _store_step_major(y_ref, slice(0, D_SSM), y)
        hr_s[...] = nhr
        hi_s[...] = nhi

    return pl.pallas_call(
        body, name=name, grid=(nb, nc),
        in_specs=[pl.BlockSpec((S5_CHUNK, 2 * D_SSM), lambda b, c: (b * nc + c, 0))]
        + [_full(s) for s in _S5_PARAM_SHAPES],
        out_specs=[pl.BlockSpec((S5_CHUNK, D_SSM), lambda b, c: (b * nc + c, 0)),
                   pl.BlockSpec((1, 1, 2 * N_STATE), lambda b, c: (b * nc + c, 0, 0))],
        out_shape=[_sds((t, D_SSM)), _sds((nb * nc, 1, 2 * N_STATE))],
        scratch_shapes=[pltpu.VMEM((1, N_STATE), f32), pltpu.VMEM((1, N_STATE), f32)],
        compiler_params=_cp(2, VMEM_BIG),
    )(z, *params)


def s5_bwd(z, params, hs, dy, nb, name):
    t = z.shape[0]
    nc = t // nb // S5_CHUNK
    npar = len(_S5_PARAM_SHAPES)

    def body(z_ref, hs_ref, dy_ref, *rest):
        p_refs, dz_ref, dp_refs, (dhr_s, dhi_s) = rest[:npar], rest[npar], rest[npar + 1:2 * npar + 1], rest[2 * npar + 1:]

        @pl.when(pl.program_id(1) == 0)
        def _():
            dhr_s[...] = jnp.zeros_like(dhr_s)
            dhi_s[...] = jnp.zeros_like(dhi_s)

        prim = (_step_major(z_ref, slice(0, D_SSM)), _step_major(z_ref, slice(D_SSM, 2 * D_SSM)),
                hs_ref[0, :, :N_STATE], hs_ref[0, :, N_STATE:]) + tuple(r[...] for r in p_refs)
        _, vj = jax.vjp(_s5_chunk, *prim)
        cts = vj((_step_major(dy_ref, slice(0, D_SSM)), dhr_s[...], dhi_s[...]))
        _store_step_major(dz_ref, slice(0, D_SSM), cts[0])
        _store_step_major(dz_ref, slice(D_SSM, 2 * D_SSM), cts[1])
        dhr_s[...] = cts[2]
        dhi_s[...] = cts[3]
        first = (pl.program_id(0) == 0) & (pl.program_id(1) == 0)
        for r, v in zip(dp_refs, cts[4:]):
            _acc(r, v, first)

    rev = lambda b, c: (b * nc + nc - 1 - c, 0)
    return pl.pallas_call(
        body, name=name, grid=(nb, nc),
        in_specs=[pl.BlockSpec((S5_CHUNK, 2 * D_SSM), rev),
                  pl.BlockSpec((1, 1, 2 * N_STATE), lambda b, c: (b * nc + nc - 1 - c, 0, 0)),
                  pl.BlockSpec((S5_CHUNK, D_SSM), rev)] + [_full(s) for s in _S5_PARAM_SHAPES],
        out_specs=[pl.BlockSpec((S5_CHUNK, 2 * D_SSM), rev)] + [_full(s) for s in _S5_PARAM_SHAPES],
        out_shape=[_sds((t, 2 * D_SSM), _MXU)] + [_sds(s) for s in _S5_PARAM_SHAPES],
        scratch_shapes=[pltpu.VMEM((1, N_STATE), f32), pltpu.VMEM((1, N_STATE), f32)],
        compiler_params=_cp(2, VMEM_BIG),
    )(z, hs, dy, *params)


DN_PRE_ROWS = 256
DN_COLS = 3 * D_DN // LANE


def dn_pre_bwd(zq, convw, dqkv, seq, name):
    t, tb = zq.shape[0], DN_PRE_ROWS
    nrow = t // tb
    per_seq = seq // tb

    def body(xc_ref, xp_ref, w_ref, d_ref, dx_ref, dw_ref, carry):
        step = pl.program_id(0)
        i = nrow - 1 - step

        @pl.when(step == 0)
        def _():
            carry[...] = jnp.zeros_like(carry)

        for j in range(DN_COLS):
            cols = slice(j * LANE, (j + 1) * LANE)
            fn = functools.partial(_dn_pre, is_start=i % per_seq == 0, col=j)
            _, vj = jax.vjp(fn, xc_ref[:, cols], xp_ref[:, cols], w_ref[0:1, cols], w_ref[1:2, cols],
                            w_ref[2:3, cols], w_ref[3:4, cols])
            dxc, dxp, dw0, dw1, dw2, dw3 = vj(d_ref[:, cols])
            dx_ref[:tb - DN_HALO, cols] = dxc[:tb - DN_HALO].astype(_MXU)
            dx_ref[tb - DN_HALO:, cols] = (dxc[tb - DN_HALO:] + carry[:, cols]).astype(_MXU)
            carry[:, cols] = dxp
            for k, dw in enumerate((dw0, dw1, dw2, dw3)):
                @pl.when(step == 0)
                def _():
                    dw_ref[k:k + 1, cols] = dw

                @pl.when(step != 0)
                def _():
                    dw_ref[k:k + 1, cols] += dw

    rev = lambda s: (nrow - 1 - s, 0)
    return pl.pallas_call(
        body, name=name, grid=(nrow,),
        in_specs=[pl.BlockSpec((tb, 3 * D_DN), rev),
                  pl.BlockSpec((DN_HALO, 3 * D_DN),
                               lambda s: (jnp.maximum((nrow - 1 - s) * (tb // DN_HALO) - 1, 0), 0)),
                  _full((DN_CONV, 3 * D_DN)), pl.BlockSpec((tb, 3 * D_DN), rev)],
        out_specs=[pl.BlockSpec((tb, 3 * D_DN), rev), _full((DN_CONV, 3 * D_DN))],
        out_shape=[_sds((t, 3 * D_DN), _MXU), _sds((DN_CONV, 3 * D_DN))],
        scratch_shapes=[pltpu.VMEM((DN_HALO, 3 * D_DN), f32)],
        compiler_params=_cp(1, VMEM_BIG),
    )(zq, zq, convw, dqkv)


DN_LOCAL_CHUNKS = 4
DN_ATTN = DN_HEADS * DN_CHUNK


def _dn_heads(ref, rows, base=0):
    return [ref[rows, base + h * DN_HEAD_DIM:base + (h + 1) * DN_HEAD_DIM] for h in range(DN_HEADS)]


def dn_front_fwd(zq, convw, ab, alog, dtb, seq, name):
    t = zq.shape[0]
    c, n = DN_CHUNK, DN_LOCAL_CHUNKS
    per_seq = seq // (n * c)

    def body(xc_ref, xp_ref, w_ref, ab_ref, alog_ref, dtb_ref,
             qkv_ref, val_ref, kcd_ref, attn_ref, qd_ref, kd_ref, el_ref, inv_ref):
        is_start = pl.program_id(0) % per_seq == 0
        blocks = []
        for j in range(DN_COLS):
            cols = slice(j * LANE, (j + 1) * LANE)
            blocks.append(_dn_pre(xc_ref[:, cols], xp_ref[:, cols], w_ref[0:1, cols], w_ref[1:2, cols],
                                  w_ref[2:3, cols], w_ref[3:4, cols], is_start, j))
            qkv_ref[:, cols] = blocks[-1]
        rows = [pl.ds(j * c, c) for j in range(n)]

        def heads(base, j):
            return [blocks[base + h][j * c:(j + 1) * c] for h in range(DN_HEADS)]

        vals, kcds, attns, qds, kds, els, invs = _dn_local(
            [heads(0, j) for j in range(n)], [heads(DN_HEADS, j) for j in range(n)],
            [heads(2 * DN_HEADS, j) for j in range(n)], [ab_ref[r, :] for r in rows], alog_ref[...], dtb_ref[...])
        for j, r in enumerate(rows):
            for h in range(DN_HEADS):
                lo, hi = h * DN_HEAD_DIM, (h + 1) * DN_HEAD_DIM
                val_ref[r, lo:hi] = vals[j][h]
                kcd_ref[r, lo:hi] = kcds[j][h].astype(_MXU)
                qd_ref[r, lo:hi] = qds[j][h].astype(_MXU)
                kd_ref[r, lo:hi] = kds[j][h].astype(_MXU)
                attn_ref[r, h * c:(h + 1) * c] = attns[j][h].astype(_MXU)
                inv_ref[r, h * c:(h + 1) * c] = invs[j][h]
            el_ref[j] = els[j]

    wide = _rows(n * c, D_DN)
    outs = pl.pallas_call(
        body, name=name, grid=(t // (n * c),),
        in_specs=[_rows(n * c, 3 * D_DN),
                  pl.BlockSpec((DN_HALO, 3 * D_DN), lambda i: (jnp.maximum(i * (n * c // DN_HALO) - 1, 0), 0)),
                  _full((DN_CONV, 3 * D_DN)), _rows(n * c, LANE), _full((1, LANE)), _full((1, LANE))],
        out_specs=[_rows(n * c, 3 * D_DN), wide, wide, _rows(n * c, DN_ATTN), wide, wide,
                   pl.BlockSpec((n, 1, LANE), lambda i: (i, 0, 0)), _rows(n * c, DN_ATTN)],
        out_shape=[_sds((t, 3 * D_DN)), _sds((t, D_DN)), _sds((t, D_DN), _MXU), _sds((t, DN_ATTN), _MXU),
                   _sds((t, D_DN), _MXU), _sds((t, D_DN), _MXU), _sds((t // c, 1, LANE)), _sds((t, DN_ATTN))],
        compiler_params=_cp(1, VMEM_BIG),
    )(zq, zq, convw, ab, alog, dtb)
    return outs[0], outs[1:7], outs[7]


def dn_local_bwd(qkv, ab, alog, dtb, inv, cts, name):
    t = qkv.shape[0]
    c, n = DN_CHUNK, DN_LOCAL_CHUNKS

    def body(qkv_ref, ab_ref, alog_ref, dtb_ref, inv_ref, dval_ref, dkcd_ref, dattn_ref, dqd_ref, dkd_ref, del_ref,
             dqkv_ref, dab_ref, dalog_ref, ddtb_ref):
        rows = [pl.ds(j * c, c) for j in range(n)]
        invs = [[inv_ref[r, h * c:(h + 1) * c] for h in range(DN_HEADS)] for r in rows]

        def local(qs, ks, vs, abs_, alog, dtb):
            return _dn_local(qs, ks, vs, abs_, alog, dtb, invs)[:6]

        _, vj = jax.vjp(local, [_dn_heads(qkv_ref, r) for r in rows], [_dn_heads(qkv_ref, r, D_DN) for r in rows],
                        [_dn_heads(qkv_ref, r, 2 * D_DN) for r in rows], [ab_ref[r, :] for r in rows], alog_ref[...],
                        dtb_ref[...])
        dattn = [[dattn_ref[r, h * c:(h + 1) * c] for h in range(DN_HEADS)] for r in rows]
        dq, dk, dv, dab, dalog, ddtb = vj(([_dn_heads(dval_ref, r) for r in rows], [_dn_heads(dkcd_ref, r) for r in rows],
                                           dattn, [_dn_heads(dqd_ref, r) for r in rows],
                                           [_dn_heads(dkd_ref, r) for r in rows], [del_ref[j] for j in range(n)]))
        for j, r in enumerate(rows):
            for h in range(DN_HEADS):
                lo, hi = h * DN_HEAD_DIM, (h + 1) * DN_HEAD_DIM
                dqkv_ref[r, lo:hi] = dq[j][h]
                dqkv_ref[r, D_DN + lo:D_DN + hi] = dk[j][h]
                dqkv_ref[r, 2 * D_DN + lo:2 * D_DN + hi] = dv[j][h]
            dab_ref[r, :] = dab[j].astype(_MXU)
        first = pl.program_id(0) == 0
        _acc(dalog_ref, dalog, first)
        _acc(ddtb_ref, ddtb, first)

    wide = _rows(n * c, D_DN)
    return pl.pallas_call(
        body, name=name, grid=(t // (n * c),),
        in_specs=[_rows(n * c, 3 * D_DN), _rows(n * c, LANE), _full((1, LANE)), _full((1, LANE)),
                  _rows(n * c, DN_ATTN), wide, wide, _rows(n * c, DN_ATTN), wide, wide,
                  pl.BlockSpec((n, 1, LANE), lambda i: (i, 0, 0))],
        out_specs=[_rows(n * c, 3 * D_DN), _rows(n * c, LANE), _full((1, LANE)), _full((1, LANE))],
        out_shape=[_sds((t, 3 * D_DN)), _sds((t, LANE), _MXU), _sds((1, LANE)), _sds((1, LANE))],
        compiler_params=_cp(1),
    )(qkv, ab, alog, dtb, inv, *cts)


def _seq_view(a, nb):
    return a.reshape((nb, a.shape[0] // nb) + a.shape[1:])


def _dn_chains(nb):
    return [(b, h) for b in range(nb) for h in range(DN_HEADS)]


DN_SCAN_CHUNKS = 4
DN_SCAN_FWD_CHUNKS = 8


def _dn_step_operands(val_ref, kcd_ref, attn_ref, qd_ref, kd_ref, el_ref, gg_ref, nb, j):
    chains = _dn_chains(nb)
    c = DN_CHUNK
    rows = pl.ds(j * c, c)

    def wide(ref):
        return [ref[b, rows, h * DN_HEAD_DIM:(h + 1) * DN_HEAD_DIM].astype(f32) for b, h in chains]

    attns = [attn_ref[b, rows, h * c:(h + 1) * c].astype(f32) for b, h in chains]
    return (wide(val_ref), wide(kcd_ref), attns, wide(qd_ref), wide(kd_ref),
            [el_ref[b, j, :, h:h + 1] for b, h in chains], wide(gg_ref))


def dn_scan_fwd(loc, gg, ng, nb, name):
    val, kcd, attn, qd, kd, el = loc
    t = val.shape[0]
    c, n = DN_CHUNK, DN_SCAN_FWD_CHUNKS
    nc = t // nb // c
    ns = nb * DN_HEADS

    def body(val_ref, kcd_ref, attn_ref, qd_ref, kd_ref, el_ref, gg_ref, ng_ref, y_ref, ss_ref, st):
        @pl.when(pl.program_id(0) == 0)
        def _():
            st[...] = jnp.zeros_like(st)

        sts = [st[i] for i in range(ns)]
        for j in range(n):
            for i in range(ns):
                ss_ref[j, i] = sts[i]
            ys, sts = _dn_step(*_dn_step_operands(val_ref, kcd_ref, attn_ref, qd_ref, kd_ref, el_ref, gg_ref, nb, j),
                               sts, ng_ref[...])
            for i, (b, h) in enumerate(_dn_chains(nb)):
                y_ref[b, pl.ds(j * c, c), h * DN_HEAD_DIM:(h + 1) * DN_HEAD_DIM] = ys[i]
        for i in range(ns):
            st[i] = sts[i]

    def blk(w):
        return pl.BlockSpec((nb, n * c, w), lambda k: (0, k, 0))

    el_spec = pl.BlockSpec((nb, n, 1, LANE), lambda k: (0, k, 0, 0))
    y, ss = pl.pallas_call(
        body, name=name, grid=(nc // n,),
        in_specs=[blk(D_DN), blk(D_DN), blk(DN_ATTN), blk(D_DN), blk(D_DN), el_spec, blk(D_DN), _full((1, LANE))],
        out_specs=[blk(D_DN), pl.BlockSpec((n, ns, DN_HEAD_DIM, DN_HEAD_DIM), lambda k: (k, 0, 0, 0))],
        out_shape=[_sds((nb, t // nb, D_DN)), _sds((nc, ns, DN_HEAD_DIM, DN_HEAD_DIM))],
        scratch_shapes=[pltpu.VMEM((ns, DN_HEAD_DIM, DN_HEAD_DIM), f32)],
        compiler_params=_cp(1, VMEM_BIG),
    )(_seq_view(val, nb), _seq_view(kcd, nb), _seq_view(attn, nb), _seq_view(qd, nb), _seq_view(kd, nb),
      el.reshape(nb, nc, 1, LANE), _seq_view(gg, nb), ng)
    return y.reshape(t, D_DN), ss


def dn_scan_bwd(loc, gg, ng, ss, dy, nb, name):
    val, kcd, attn, qd, kd, el = loc
    t = val.shape[0]
    c, n = DN_CHUNK, DN_SCAN_CHUNKS
    nc = t // nb // c
    ns = nb * DN_HEADS
    steps = nc // n

    def body(val_ref, kcd_ref, attn_ref, qd_ref, kd_ref, el_ref, gg_ref, ng_ref, ss_ref, dy_ref,
             dval_ref, dkcd_ref, dattn_ref, dqd_ref, dkd_ref, del_ref, dgg_ref, dng_ref, dst):
        @pl.when(pl.program_id(0) == 0)
        def _():
            dst[...] = jnp.zeros_like(dst)

        lane = lax.broadcasted_iota(jnp.int32, (1, LANE), 1)
        chains = _dn_chains(nb)
        ds = [dst[i] for i in range(ns)]
        dng_tot = jnp.zeros((1, LANE), f32)
        for j in reversed(range(n)):
            rows = pl.ds(j * c, c)
            _, vj = jax.vjp(_dn_step,
                            *_dn_step_operands(val_ref, kcd_ref, attn_ref, qd_ref, kd_ref, el_ref, gg_ref, nb, j),
                            [ss_ref[j, i] for i in range(ns)], ng_ref[...])
            dys = [dy_ref[b, rows, h * DN_HEAD_DIM:(h + 1) * DN_HEAD_DIM] for b, h in chains]
            dval, dkcd, dattn, dqd, dkd, dlast, dgg, ds, dng = vj((dys, ds))
            dng_tot = dng_tot + dng
            del_rows = [jnp.zeros((1, LANE), f32) for _ in range(nb)]
            for i, (b, h) in enumerate(chains):
                cols = slice(h * DN_HEAD_DIM, (h + 1) * DN_HEAD_DIM)
                dval_ref[b, rows, cols] = dval[i]
                dkcd_ref[b, rows, cols] = dkcd[i]
                dattn_ref[b, rows, h * c:(h + 1) * c] = dattn[i]
                dqd_ref[b, rows, cols] = dqd[i]
                dkd_ref[b, rows, cols] = dkd[i]
                dgg_ref[b, rows, cols] = dgg[i].astype(_MXU)
                del_rows[b] = del_rows[b] + jnp.where(lane == h, dlast[i], 0.0)
            for b in range(nb):
                del_ref[b, j] = del_rows[b]
        for i in range(ns):
            dst[i] = ds[i]
        _acc(dng_ref, dng_tot, pl.program_id(0) == 0)

    def blk(w):
        return pl.BlockSpec((nb, n * c, w), lambda k: (0, steps - 1 - k, 0))

    el_spec = pl.BlockSpec((nb, n, 1, LANE), lambda k: (0, steps - 1 - k, 0, 0))
    outs = pl.pallas_call(
        body, name=name, grid=(steps,),
        in_specs=[blk(D_DN), blk(D_DN), blk(DN_ATTN), blk(D_DN), blk(D_DN), el_spec, blk(D_DN), _full((1, LANE)),
                  pl.BlockSpec((n, ns, DN_HEAD_DIM, DN_HEAD_DIM), lambda k: (steps - 1 - k, 0, 0, 0)), blk(D_DN)],
        out_specs=[blk(D_DN), blk(D_DN), blk(DN_ATTN), blk(D_DN), blk(D_DN), el_spec, blk(D_DN), _full((1, LANE))],
        out_shape=[_sds((nb, t // nb, D_DN)), _sds((nb, t // nb, D_DN)), _sds((nb, t // nb, DN_ATTN)),
                   _sds((nb, t // nb, D_DN)), _sds((nb, t // nb, D_DN)), _sds((nb, nc, 1, LANE)),
                   _sds((nb, t // nb, D_DN), _MXU), _sds((1, LANE))],
        scratch_shapes=[pltpu.VMEM((ns, DN_HEAD_DIM, DN_HEAD_DIM), f32)],
        compiler_params=_cp(1, VMEM_BIG),
    )(_seq_view(val, nb), _seq_view(kcd, nb), _seq_view(attn, nb), _seq_view(qd, nb), _seq_view(kd, nb),
      el.reshape(nb, nc, 1, LANE), _seq_view(gg, nb), ng, ss, _seq_view(dy, nb))
    dloc = [o.reshape((t,) + o.shape[2:]) for o in outs[:5]] + [outs[5].reshape(t // c, 1, LANE)]
    return dloc, outs[6].reshape(t, D_DN), outs[7]


SG_ROWS = 1024


def sg_fwd(z, lng, lnb, w, bt, name):
    t = z.shape[0]

    def body(z_ref, lng_ref, lnb_ref, w_ref, bt_ref, y_ref):
        ws = [w_ref[h] for h in range(SG_HEADS)]
        for k in range(SG_ROWS // SG_CHUNK):
            r = pl.ds(k * SG_CHUNK, SG_CHUNK)
            y_ref[r, :] = _sg_chunk(z_ref[r, :D_SG], z_ref[r, D_SG:2 * D_SG], z_ref[r, 2 * D_SG:], lng_ref[...],
                                    lnb_ref[...], ws, bt_ref[...])

    return pl.pallas_call(
        body, name=name, grid=(t // SG_ROWS,),
        in_specs=[_rows(SG_ROWS, 3 * D_SG), _full((1, D_SG)), _full((1, D_SG)),
                  _full((SG_HEADS, SG_CHUNK, SG_CHUNK)), _full((SG_CHUNK, LANE))],
        out_specs=_rows(SG_ROWS, D_SG),
        out_shape=_sds((t, D_SG)),
        compiler_params=_cp(1),
    )(z, lng, lnb, w, bt)


def sg_bwd(z, lng, lnb, w, bt, dy, name):
    t = z.shape[0]

    def body(z_ref, lng_ref, lnb_ref, w_ref, bt_ref, dy_ref, dz_ref, dlng_ref, dlnb_ref, dw_ref, dbt_ref):
        ws = [w_ref[h] for h in range(SG_HEADS)]
        tot = None
        for k in range(SG_ROWS // SG_CHUNK):
            r = pl.ds(k * SG_CHUNK, SG_CHUNK)
            _, vj = jax.vjp(_sg_chunk, z_ref[r, :D_SG], z_ref[r, D_SG:2 * D_SG], z_ref[r, 2 * D_SG:], lng_ref[...],
                            lnb_ref[...], ws, bt_ref[...])
            du, dv, dgate, dlng, dlnb, dws, dbt = vj(dy_ref[r, :])
            dz_ref[r, :D_SG] = du.astype(_MXU)
            dz_ref[r, D_SG:2 * D_SG] = dv.astype(_MXU)
            dz_ref[r, 2 * D_SG:] = dgate.astype(_MXU)
            part = [dlng, dlnb, dbt] + list(dws)
            tot = part if tot is None else [a + b for a, b in zip(tot, part)]
        first = pl.program_id(0) == 0
        _acc(dlng_ref, tot[0], first)
        _acc(dlnb_ref, tot[1], first)
        _acc(dbt_ref, tot[2], first)
        for h in range(SG_HEADS):
            @pl.when(first)
            def _():
                dw_ref[h] = tot[3 + h]

            @pl.when(jnp.logical_not(first))
            def _():
                dw_ref[h] += tot[3 + h]

    return pl.pallas_call(
        body, name=name, grid=(t // SG_ROWS,),
        in_specs=[_rows(SG_ROWS, 3 * D_SG), _full((1, D_SG)), _full((1, D_SG)),
                  _full((SG_HEADS, SG_CHUNK, SG_CHUNK)), _full((SG_CHUNK, LANE)), _rows(SG_ROWS, D_SG)],
        out_specs=[_rows(SG_ROWS, 3 * D_SG), _full((1, D_SG)), _full((1, D_SG)),
                   _full((SG_HEADS, SG_CHUNK, SG_CHUNK)), _full((SG_CHUNK, LANE))],
        out_shape=[_sds((t, 3 * D_SG), _MXU), _sds((1, D_SG)), _sds((1, D_SG)), _sds((SG_HEADS, SG_CHUNK, SG_CHUNK)),
                   _sds((SG_CHUNK, LANE))],
        compiler_params=_cp(1),
    )(z, lng, lnb, w, bt, dy)


def add_pairs(a_list, b_list, name):
    n = len(a_list)

    def body(*refs):
        for a_ref, b_ref, o_ref in zip(refs[:n], refs[n:2 * n], refs[2 * n:]):
            o_ref[...] = (a_ref[...].astype(f32) + b_ref[...].astype(f32)).astype(o_ref.dtype)

    return pl.pallas_call(
        body, name=name, out_shape=[_sds(a.shape, a.dtype) for a in a_list],
        compiler_params=pltpu.CompilerParams(vmem_limit_bytes=VMEM_BIG),
    )(*a_list, *b_list)


def _adamw(g, w, m, v):
    nm = ADAM_B1 * m + (1.0 - ADAM_B1) * g
    nv = ADAM_B2 * v + (1.0 - ADAM_B2) * jnp.square(g)
    m_hat = nm / (1.0 - ADAM_B1 ** ADAM_STEP)
    v_hat = nv / (1.0 - ADAM_B2 ** ADAM_STEP)
    return -ADAM_LR * (m_hat / (jnp.sqrt(v_hat) + ADAM_EPS) + ADAM_WD * w), nm, nv


def sum_parts(half, recv, name):
    _, r, c = recv.shape
    tr = 256 if r % 256 == 0 else r

    def body(half_ref, recv_ref, g_ref):
        g = recv_ref[0].astype(f32)
        for k in range(1, N_CHIPS):
            g = g + recv_ref[k].astype(f32)
        g_ref[...] = g

    return pl.pallas_call(
        body, name=name,
        grid_spec=pltpu.PrefetchScalarGridSpec(
            num_scalar_prefetch=1, grid=(r // tr,),
            in_specs=[pl.BlockSpec((N_CHIPS, tr, c), lambda i, h: (0, i, 0))],
            out_specs=pl.BlockSpec((None, tr, c), lambda i, h: (h[0], i, 0))),
        out_shape=_sds((2, r, c)),
        compiler_params=_cp(1, VMEM_BIG),
    )(half, recv)


def adamw(g, w, m, v, name):
    _, r, c = w.shape
    tr = 256 if r % 256 == 0 else r

    def body(g_ref, w_ref, m_ref, v_ref, d_ref, nm_ref, nv_ref):
        d_ref[...], nm_ref[...], nv_ref[...] = _adamw(g_ref[...], w_ref[...], m_ref[...], v_ref[...])

    blk = pl.BlockSpec((None, tr, c), lambda l, i: (l, i, 0))
    return pl.pallas_call(
        body, name=name, grid=(2, r // tr), in_specs=[blk] * 4, out_specs=[blk] * 3, out_shape=[_sds((2, r, c))] * 3,
        compiler_params=_cp(2, VMEM_BIG),
    )(g, w, m, v)


def sum_parts_small(chip, parts, sums, name):
    n = len(sums)

    def body(chip_ref, *refs):
        for part, own, out in zip(refs[:n], refs[n:2 * n], refs[2 * n:]):
            g = jnp.where(chip_ref[0] == 0, own[...], part[0])
            for q in range(1, N_CHIPS):
                g = g + jnp.where(chip_ref[0] == q, own[...], part[q])
            out[...] = g

    vmem = pl.BlockSpec(memory_space=pltpu.VMEM)
    return pl.pallas_call(
        body, name=name, in_specs=[pl.BlockSpec(memory_space=pltpu.SMEM)] + [vmem] * (2 * n), out_specs=[vmem] * n,
        out_shape=[_sds(s.shape) for s in sums], compiler_params=pltpu.CompilerParams(vmem_limit_bytes=VMEM_BIG),
    )(chip, *parts, *sums)


def adamw_small(gs, ws, ms, vs, name):
    n = len(ws)

    def body(*refs):
        ins, outs = refs[:4 * n], refs[4 * n:]
        for k in range(n):
            outs[k][...], outs[n + k][...], outs[2 * n + k][...] = _adamw(
                ins[k][...], ins[n + k][...], ins[2 * n + k][...], ins[3 * n + k][...])

    outs = pl.pallas_call(
        body, name=name, out_shape=[_sds(w.shape) for w in ws] * 3,
        compiler_params=pltpu.CompilerParams(vmem_limit_bytes=VMEM_BIG),
    )(*gs, *ws, *ms, *vs)
    return [outs[j * n:(j + 1) * n] for j in range(3)]


_ANY = pl.BlockSpec(memory_space=pltpu.HBM)
_MESH = pl.DeviceIdType.MESH


def _flip(v, bit):
    return 1 - v if bit else v


_CHIP_RELS = ((1, 0), (0, 1), (1, 1))


def _piece(ref, kind, q):
    if kind[0] == "slot":
        return ref.at[q]
    if kind[0] == "all":
        return ref
    _, axis, n = kind
    return ref.at[(slice(None),) * axis + (pl.ds(q * n, n),)]


def _piece_shape(shape, kind):
    if kind[0] == "slot":
        return tuple(shape[1:])
    if kind[0] == "all":
        return tuple(shape)
    _, axis, n = kind
    return tuple(shape[:axis]) + (n,) + tuple(shape[axis + 1:])


def gather_weights(shards, kinds, name):
    n = len(shards)

    def out_shape(s, kind):
        if kind[0] == "slot":
            return (N_CHIPS,) + tuple(s.shape)
        _, axis, w = kind
        return tuple(s.shape[:axis + 1]) + (N_CHIPS * w,) + tuple(s.shape[axis + 2:])

    def place(o_ref, kind, q, layer):
        if kind[0] == "slot":
            return o_ref.at[q, layer]
        return _piece(o_ref.at[layer], kind, q)

    def body(*refs):
        s_refs, o_refs = refs[:n], refs[n:2 * n]
        send_sems, recv_sems, fwd_send_sems, fwd_recv_sems = refs[2 * n:]
        x, y, c = lax.axis_index("x"), lax.axis_index("y"), lax.axis_index("c")
        mine = 2 * x + y
        sends, arrivals, forwards, fwd_arrivals = [], [], [], []
        for r, (fx, fy) in enumerate(_CHIP_RELS):
            px, py = _flip(x, fx), _flip(y, fy)
            peer = 2 * px + py
            for k in range(n):
                s = r * n + k
                sends.append(pltpu.make_async_remote_copy(
                    src_ref=s_refs[k].at[c], dst_ref=place(o_refs[k], kinds[k], mine, c), send_sem=send_sems.at[s],
                    recv_sem=recv_sems.at[s], device_id=(px, py, c), device_id_type=_MESH))
                arrivals.append(pltpu.make_async_remote_copy(
                    src_ref=s_refs[k].at[c], dst_ref=place(o_refs[k], kinds[k], peer, c), send_sem=send_sems.at[s],
                    recv_sem=recv_sems.at[s], device_id=(px, py, c), device_id_type=_MESH))
                block = place(o_refs[k], kinds[k], peer, c)
                forwards.append(pltpu.make_async_remote_copy(
                    src_ref=block, dst_ref=block, send_sem=fwd_send_sems.at[s], recv_sem=fwd_recv_sems.at[s],
                    device_id=(x, y, 1 - c), device_id_type=_MESH))
                other = place(o_refs[k], kinds[k], peer, 1 - c)
                fwd_arrivals.append(pltpu.make_async_remote_copy(
                    src_ref=other, dst_ref=other, send_sem=fwd_send_sems.at[s], recv_sem=fwd_recv_sems.at[s],
                    device_id=(x, y, 1 - c), device_id_type=_MESH))
        for cp in sends:
            cp.start()
        for arrived, fwd in zip(arrivals, forwards):
            arrived.wait_recv()
            fwd.start()
        for cp in fwd_arrivals:
            cp.wait_recv()
        for cp in sends + forwards:
            cp.wait_send()

    m = len(_CHIP_RELS) * n
    return pl.pallas_call(
        body, name=name, in_specs=[_ANY] * n, out_specs=[_ANY] * n,
        out_shape=[_sds(out_shape(s, k), s.dtype) for s, k in zip(shards, kinds)],
        scratch_shapes=[pltpu.SemaphoreType.DMA((m,))] * 4,
    )(*shards)


def _owned_by(owners, side):
    return [k for k, o in enumerate(owners) if o == side]


def exchange_halves(gs, smalls, owners, name):
    n, ns = len(gs), len(smalls)

    def body(*refs):
        g_refs, s_refs = refs[:n], refs[n:n + ns]
        got_refs, sgot_refs = refs[n + ns:2 * n + ns], refs[2 * n + ns:2 * (n + ns)]
        send_sems, recv_sems = refs[2 * (n + ns):]
        x, y, c = lax.axis_index("x"), lax.axis_index("y"), lax.axis_index("c")
        sibling = (x, y, 1 - c)
        swaps = [pltpu.make_async_remote_copy(
            src_ref=g_refs[k].at[1 - c], dst_ref=got_refs[k], send_sem=send_sems.at[k], recv_sem=recv_sems.at[k],
            device_id=sibling, device_id_type=_MESH) for k in range(n)]
        gives = [pltpu.make_async_remote_copy(
            src_ref=s_refs[k], dst_ref=sgot_refs[k], send_sem=send_sems.at[n + k], recv_sem=recv_sems.at[n + k],
            device_id=sibling, device_id_type=_MESH) for k in range(ns)]
        for cp in swaps:
            cp.start()
        for side in (0, 1):
            @pl.when(c == 1 - side)
            def _():
                for k in _owned_by(owners, side):
                    gives[k].start()
        for cp in swaps:
            cp.wait()
        for side in (0, 1):
            @pl.when(c == 1 - side)
            def _():
                for k in _owned_by(owners, side):
                    gives[k].wait_send()

            @pl.when(c == side)
            def _():
                for k in _owned_by(owners, side):
                    gives[k].wait_recv()

    outs = pl.pallas_call(
        body, name=name, in_specs=[_ANY] * (n + ns), out_specs=[_ANY] * (n + ns),
        out_shape=[_sds(g.shape[1:], g.dtype) for g in gs] + [_sds(s.shape, s.dtype) for s in smalls],
        scratch_shapes=[pltpu.SemaphoreType.DMA((n + ns,)), pltpu.SemaphoreType.DMA((n + ns,))],
    )(*gs, *smalls)
    return outs[:n], outs[n:]


def reduce_to_chips(ts, kinds, smalls, owners, name):
    n, ns = len(ts), len(smalls)

    def body(*refs):
        t_refs, s_refs = refs[:n], refs[n:n + ns]
        o_refs, so_refs = refs[n + ns:2 * n + ns], refs[2 * n + ns:2 * (n + ns)]
        send_sems, recv_sems = refs[2 * (n + ns):]
        x, y, c = lax.axis_index("x"), lax.axis_index("y"), lax.axis_index("c")
        mine = 2 * x + y
        sends, arrivals, small_sends, small_arrivals = [], [], [], []
        for r, (fx, fy) in enumerate(_CHIP_RELS):
            px, py = _flip(x, fx), _flip(y, fy)
            peer = 2 * px + py
            for k in range(n + ns):
                s = r * (n + ns) + k
                if k < n:
                    src, dst = _piece(t_refs[k], kinds[k], peer), o_refs[k]
                else:
                    src, dst = s_refs[k - n], so_refs[k - n]
                go = pltpu.make_async_remote_copy(
                    src_ref=src, dst_ref=dst.at[mine], send_sem=send_sems.at[s], recv_sem=recv_sems.at[s],
                    device_id=(px, py, c), device_id_type=_MESH)
                come = pltpu.make_async_remote_copy(
                    src_ref=src, dst_ref=dst.at[peer], send_sem=send_sems.at[s], recv_sem=recv_sems.at[s],
                    device_id=(px, py, c), device_id_type=_MESH)
                (sends if k < n else small_sends).append(go)
                (arrivals if k < n else small_arrivals).append(come)

        def owned(copies, side):
            return [cp for j, cp in enumerate(copies) if owners[j % ns] == side]

        for cp in sends:
            cp.start()
        for side in (0, 1):
            @pl.when(c == side)
            def _():
                for cp in owned(small_sends, side):
                    cp.start()
        for cp in arrivals:
            cp.wait_recv()
        for cp in sends:
            cp.wait_send()
        for side in (0, 1):
            @pl.when(c == side)
            def _():
                for cp in owned(small_arrivals, side):
                    cp.wait_recv()
                for cp in owned(small_sends, side):
                    cp.wait_send()

    m = len(_CHIP_RELS) * (n + ns)
    outs = pl.pallas_call(
        body, name=name, in_specs=[_ANY] * (n + ns), out_specs=[_ANY] * (n + ns),
        out_shape=[_sds((N_CHIPS,) + _piece_shape(t.shape, k), t.dtype) for t, k in zip(ts, kinds)]
        + [_sds((N_CHIPS,) + s.shape, s.dtype) for s in smalls],
        scratch_shapes=[pltpu.SemaphoreType.DMA((m,)), pltpu.SemaphoreType.DMA((m,))],
    )(*ts, *smalls)
    return outs[:n], outs[n:]


def share_halves(rs, smalls, owners, name):
    n, ns = len(rs), len(smalls)

    def body(*refs):
        o_refs, so_refs = refs[n + ns:2 * n + ns], refs[2 * n + ns:2 * (n + ns)]
        send_sems, recv_sems = refs[2 * (n + ns):]
        x, y, c = lax.axis_index("x"), lax.axis_index("y"), lax.axis_index("c")
        sibling = (x, y, 1 - c)
        swaps = [pltpu.make_async_remote_copy(
            src_ref=o_refs[k].at[c], dst_ref=o_refs[k].at[c], send_sem=send_sems.at[k], recv_sem=recv_sems.at[k],
            device_id=sibling, device_id_type=_MESH) for k in range(n)]
        arrivals = [pltpu.make_async_remote_copy(
            src_ref=o_refs[k].at[c], dst_ref=o_refs[k].at[1 - c], send_sem=send_sems.at[k], recv_sem=recv_sems.at[k],
            device_id=sibling, device_id_type=_MESH) for k in range(n)]
        gives = [pltpu.make_async_remote_copy(
            src_ref=so_refs[k], dst_ref=so_refs[k], send_sem=send_sems.at[n + k], recv_sem=recv_sems.at[n + k],
            device_id=sibling, device_id_type=_MESH) for k in range(ns)]
        for cp in swaps:
            cp.start()
        for side in (0, 1):
            @pl.when(c == side)
            def _():
                for k in _owned_by(owners, side):
                    gives[k].start()
        for cp in arrivals:
            cp.wait_recv()
        for cp in swaps:
            cp.wait_send()
        for side in (0, 1):
            @pl.when(c == side)
            def _():
                for k in _owned_by(owners, side):
                    gives[k].wait_send()

            @pl.when(c == 1 - side)
            def _():
                for k in _owned_by(owners, side):
                    gives[k].wait_recv()

    outs = pl.pallas_call(
        body, name=name, in_specs=[_ANY] * (n + ns), out_specs=[_ANY] * (n + ns),
        out_shape=[_sds(r.shape, r.dtype) for r in list(rs) + list(smalls)],
        input_output_aliases={k: k for k in range(n + ns)},
        scratch_shapes=[pltpu.SemaphoreType.DMA((n + ns,)), pltpu.SemaphoreType.DMA((n + ns,))],
    )(*rs, *smalls)
    return outs[:n], outs[n:]


def _small_view(a):
    if a.size < 8 * LANE:
        return jnp.pad(a.reshape(-1), (0, 8 * LANE - a.size)).reshape(8, LANE)
    if a.ndim == 1:
        return a.reshape(1, a.shape[0])
    if a.ndim == 4 and a.shape[-1] < LANE:
        return a.reshape(a.shape[0], a.shape[1], a.shape[2] * a.shape[3])
    return a


def _permuted_from_shards(shards):
    parts = []
    for lo, hi in GROUP_COLS:
        for q in range(N_CHIPS):
            a, b = max(lo, q * SHARD_COLS), min(hi, (q + 1) * SHARD_COLS)
            if a < b:
                parts.append(shards[q][..., a - q * SHARD_COLS:b - q * SHARD_COLS])
    pad = jnp.zeros(shards[0].shape[:-1] + (D_IN_PAD - D_IN,), shards[0].dtype)
    return jnp.concatenate(parts + [pad], axis=-1)


def _shards_from_groups(groups):
    in_order = sorted(range(len(GROUP_COLS)), key=lambda j: GROUP_COLS[j][0])
    shards = []
    for q in range(N_CHIPS):
        parts = []
        for j in in_order:
            lo, hi = GROUP_COLS[j]
            a, b = max(lo, q * SHARD_COLS), min(hi, (q + 1) * SHARD_COLS)
            if a < b:
                parts.append(groups[j][..., a - lo:b - lo])
        shards.append(jnp.concatenate(parts, axis=-1))
    return shards


def _expand_b(b):
    eye = jnp.eye(SSM_GROUPS, dtype=b.dtype)
    return jnp.einsum("gnc,gh->gchn", b, eye).reshape(D_SSM, N_STATE)


def _extract_b(e):
    return jnp.einsum("gcgn->gnc", e.reshape(SSM_GROUPS, SSM_GROUP, SSM_GROUPS, SSM_STATE))


def _expand_c(c):
    eye = jnp.eye(SSM_GROUPS, dtype=c.dtype)
    return jnp.einsum("gcn,gh->gnhc", c, eye).reshape(N_STATE, D_SSM)


def _extract_c(e):
    return jnp.einsum("gngc->gcn", e.reshape(SSM_GROUPS, SSM_STATE, SSM_GROUPS, SSM_GROUP))


def _lane_row(v):
    return jnp.pad(v, (0, LANE - v.shape[0])).reshape(1, LANE)


def _layer_params(w, l):
    return dict(
        norm_g=w["norm_g"][l][None], win=w["w_in_perm"][l], wout=w["w_out"][l].astype(_MXU),
        pg=w["ple_norm_g"][l][None], wgate=w["w_ple_gate"][l].astype(_MXU), wple=w["w_ple"][l].astype(_MXU),
        are=w["ssm_a_re"][l].reshape(1, N_STATE), aim=w["ssm_a_im"][l].reshape(1, N_STATE),
        ls=jnp.repeat(w["ssm_log_step"][l], SSM_STATE).reshape(1, N_STATE),
        bre=_expand_b(w["ssm_b_re"][l]), bim=_expand_b(w["ssm_b_im"][l]),
        cr=_expand_c(w["ssm_c_re"][l]), ci=_expand_c(w["ssm_c_im"][l]),
        dr=w["ssm_d"][l].reshape(1, D_SSM), wglu=w["ssm_w_glu"][l].astype(f32), bglu=w["ssm_b_glu"][l][None],
        convw=w["dn_conv_w"][l], alog=_lane_row(w["dn_a_log"][l]), dtb=_lane_row(w["dn_dt_bias"][l]),
        ng=w["dn_norm_g"][l][None],
        lng=w["sg_ln_g"][l][None], lnb=w["sg_ln_b"][l][None], sgw=w["sg_w"][l],
        bt=jnp.pad(w["sg_b"][l].T, ((0, 0), (0, LANE - SG_HEADS))),
    )


def _layer_fwd(x, p, lp, nb, tag):
    seq = x.shape[0] // nb
    h, zs, zq, zg, zsg, zab = in_fwd(x, lp["norm_g"], lp["win"], f"in_fwd{tag}")
    prep = s5_prep_fwd(lp["are"], lp["aim"], lp["ls"], lp["bre"], lp["bim"], f"s5_prep_fwd{tag}")
    s5p = tuple(prep) + (lp["cr"], lp["ci"], lp["dr"], lp["wglu"], lp["bglu"])
    ys, hs = s5_fwd(zs, s5p, nb, f"s5_fwd{tag}")
    qkv, loc, inv = dn_front_fwd(zq, lp["convw"], zab, lp["alog"], lp["dtb"], seq, f"dn_front_fwd{tag}")
    yd, ss = dn_scan_fwd(loc, zg, lp["ng"], nb, f"dn_scan_fwd{tag}")
    yg = sg_fwd(zsg, lp["lng"], lp["lnb"], lp["sgw"], lp["bt"], f"sg_fwd{tag}")
    x2, x1, y, hn = post_fwd(x, ys, yd, yg, p, lp["wout"], lp["pg"], lp["wgate"], lp["wple"], f"post_fwd{tag}")
    saved = dict(x=x, h=h, zs=zs, zq=zq, zg=zg, zsg=zsg, zab=zab, s5p=s5p, hs=hs, qkv=qkv, loc=loc, inv=inv, ss=ss, x1=x1, y=y, hn=hn, p=p)
    return x2, saved


def _layer_bwd(dx2, sv, lp, nb, tag):
    seq = dx2.shape[0] // nb
    dx1, dgp, dpp, dys, dyd, dyg, dpg = post_bwd(dx2, sv["x1"], sv["hn"], sv["p"], lp["wout"], lp["pg"], lp["wgate"],
                                                 lp["wple"], f"post_bwd{tag}")
    g = {}
    g["w_out"] = wgrad(sv["y"], dx1, f"wgrad_out{tag}")
    g["w_ple_gate"] = wgrad(sv["hn"], dgp, f"wgrad_gate{tag}")
    g["w_ple"] = wgrad(sv["p"], dpp, f"wgrad_ple{tag}")
    g["ple_norm_g"] = dpg[0]
    dzsg, dlng, dlnb, dsgw, dbt = sg_bwd(sv["zsg"], lp["lng"], lp["lnb"], lp["sgw"], lp["bt"], dyg, f"sg_bwd{tag}")
    g["sg_ln_g"], g["sg_ln_b"], g["sg_w"], g["sg_b"] = dlng[0], dlnb[0], dsgw, dbt[:, :SG_HEADS].T
    dloc, dzg, dng = dn_scan_bwd(sv["loc"], sv["zg"], lp["ng"], sv["ss"], dyd, nb, f"dn_scan_bwd{tag}")
    dqkv, dzab, dalog, ddtb = dn_local_bwd(sv["qkv"], sv["zab"], lp["alog"], lp["dtb"], sv["inv"], dloc,
                                           f"dn_local_bwd{tag}")
    dzq, dconv = dn_pre_bwd(sv["zq"], lp["convw"], dqkv, seq, f"dn_pre_bwd{tag}")
    g["dn_conv_w"], g["dn_a_log"], g["dn_dt_bias"], g["dn_norm_g"] = dconv, dalog[0, :DN_HEADS], ddtb[0, :DN_HEADS], dng[0]
    s5out = s5_bwd(sv["zs"], sv["s5p"], sv["hs"], dys, nb, f"s5_bwd{tag}")
    dzs, dprep, (dcr, dci, ddr, dwglu, dbglu) = s5out[0], s5out[1:1 + S5_PREPARED], s5out[1 + S5_PREPARED:]
    dare, daim, dls, dbre, dbim = s5_prep_bwd(lp["are"], lp["aim"], lp["ls"], lp["bre"], lp["bim"], dprep,
                                              f"s5_prep_bwd{tag}")
    g["ssm_a_re"] = dare.reshape(SSM_GROUPS, SSM_STATE)
    g["ssm_a_im"] = daim.reshape(SSM_GROUPS, SSM_STATE)
    g["ssm_log_step"] = dls.reshape(SSM_GROUPS, SSM_STATE).sum(axis=1)
    g["ssm_b_re"], g["ssm_b_im"] = _extract_b(dbre), _extract_b(dbim)
    g["ssm_c_re"], g["ssm_c_im"] = _extract_c(dcr), _extract_c(dci)
    g["ssm_d"] = ddr.reshape(SSM_GROUPS, SSM_GROUP)
    g["ssm_w_glu"], g["ssm_b_glu"] = dwglu, dbglu[0]
    dzs_all = (dzs, dzq, dzg, dzsg, dzab)
    dx, dng_in = in_bwd(sv["x"], lp["norm_g"], lp["win"], dzs_all, dx1, f"in_bwd{tag}")
    g["w_in_pieces"] = [wgrad(sv["h"], dz, f"wgrad_in{k}{tag}") for k, dz in enumerate(dzs_all)]
    g["norm_g"] = dng_in[0]
    return dx, g


def _local_step(x, p, target, w, nb):
    lps = [_layer_params(w, l) for l in range(DEPTH)]
    saved = []
    for l in range(DEPTH):
        x, sv = _layer_fwd(x, p[l], lps[l], nb, f"_l{l}")
        saved.append(sv)
    loss_blk, dx, dfg = loss_fwd_bwd(x, w["final_norm_g"][None], target, "loss")
    grads = [None] * DEPTH
    for l in reversed(range(DEPTH)):
        dx, grads[l] = _layer_bwd(dx, saved[l], lps[l], nb, f"_l{l}")
    out = {k: jnp.stack([grads[l][k] for l in range(DEPTH)]) for k in grads[0] if k != "w_in_pieces"}
    out["w_in_pieces"] = [grads[l]["w_in_pieces"] for l in range(DEPTH)]
    out["final_norm_g"] = dfg[0]
    return loss_blk[0, 0], dx, out


def kernel(x, p, norm_g, w_in, ssm_a_re, ssm_a_im, ssm_b_re, ssm_b_im, ssm_c_re, ssm_c_im, ssm_d, ssm_log_step, ssm_w_glu, ssm_b_glu, dn_conv_w, dn_a_log, dn_dt_bias, dn_norm_g, sg_ln_g, sg_ln_b, sg_w, sg_b, w_out, ple_norm_g, w_ple_gate, w_ple, final_norm_g, loss_target, m_norm_g, m_w_in, m_ssm_a_re, m_ssm_a_im, m_ssm_b_re, m_ssm_b_im, m_ssm_c_re, m_ssm_c_im, m_ssm_d, m_ssm_log_step, m_ssm_w_glu, m_ssm_b_glu, m_dn_conv_w, m_dn_a_log, m_dn_dt_bias, m_dn_norm_g, m_sg_ln_g, m_sg_ln_b, m_sg_w, m_sg_b, m_w_out, m_ple_norm_g, m_w_ple_gate, m_w_ple, m_final_norm_g, v_norm_g, v_w_in, v_ssm_a_re, v_ssm_a_im, v_ssm_b_re, v_ssm_b_im, v_ssm_c_re, v_ssm_c_im, v_ssm_d, v_ssm_log_step, v_ssm_w_glu, v_ssm_b_glu, v_dn_conv_w, v_dn_a_log, v_dn_dt_bias, v_dn_norm_g, v_sg_ln_g, v_sg_ln_b, v_sg_w, v_sg_b, v_w_out, v_ple_norm_g, v_w_ple_gate, v_w_ple, v_final_norm_g):
    args = locals()
    w = {n: args[n] for n in WEIGHTS}
    m = {n: args["m_" + n] for n in WEIGHTS}
    v = {n: args["v_" + n] for n in WEIGHTS}
    nb, seq = x.shape[0], x.shape[1]
    t = nb * seq

    full = _gather_full(w)
    loss_local, dx, grads = _local_step(x.reshape(t, D_MODEL), p.reshape(DEPTH, t, D_PLE),
                                        loss_target.reshape(t, D_MODEL), full, nb)
    outs, loss = _reduce_and_update(grads, w, m, v, loss_local)
    return (loss, dx.reshape(nb, seq, D_MODEL), *[outs[0][n] for n in WEIGHTS], *[outs[1][n] for n in WEIGHTS],
            *[outs[2][n] for n in WEIGHTS], *[outs[3][n] for n in WEIGHTS])


def _gather_full(w):
    sh_names = [n for n, _ in SHARDED]
    shards = [w[n] if n == "dn_conv_w" else w[n].astype(_COMM) for n in sh_names]
    gathered = gather_weights(shards, [k for _, k in SHARDED], "gather_weights")
    chip = 2 * lax.axis_index("x") + lax.axis_index("y")
    full = {n: w[n] for n in REPLICATED}
    for (n, kind), shard, got in zip(SHARDED, shards, gathered):
        if kind[0] == "slot":
            full[n] = lax.dynamic_update_index_in_dim(got, shard, chip, 0)
        else:
            full[n] = lax.dynamic_update_slice_in_dim(got, shard, chip * kind[2], axis=kind[1] + 1)
    slots = full.pop("w_in")
    full["w_in_perm"] = _permuted_from_shards([slots[q] for q in range(N_CHIPS)]).astype(_MXU)
    return full


def _reduce_and_update(grads, w, m, v, loss_local):
    sh_names = [n for n, _ in SHARDED]
    sh_kinds = [k for _, k in SHARDED]
    owners = [SMALL_OWNER[n] for n in REPLICATED + ("loss",)]

    def small_views(d):
        return [_small_view(d[n]) for n in REPLICATED]

    grads["w_in"] = jnp.stack([jnp.stack(_shards_from_groups(pieces)) for pieces in grads["w_in_pieces"]])
    gs = [grads[n] if n == "dn_conv_w" else grads[n].astype(_COMM) for n in sh_names]
    sm = small_views(grads) + [_small_view(loss_local.reshape(1))]
    core = lax.axis_index("c")
    chip = 2 * lax.axis_index("x") + lax.axis_index("y")
    got, sm_got = exchange_halves(gs, sm, owners, "exchange_halves")
    sums = add_pairs([lax.dynamic_index_in_dim(g, core, 0, keepdims=False) for g in gs] + sm, list(got) + list(sm_got),
                     "add_halves")
    sums, sm_sums = sums[:len(gs)], sums[len(gs):]
    parts, sm_parts = reduce_to_chips(sums, sh_kinds, sm_sums, owners, "reduce_to_chips")
    parts = list(parts)
    for k, (kind, total) in enumerate(zip(sh_kinds, sums)):
        if kind[0] == "slot":
            own = lax.dynamic_index_in_dim(total, chip, 0, keepdims=False)
        else:
            own = lax.dynamic_slice_in_dim(total, chip * kind[2], kind[2], axis=kind[1])
        parts[k] = lax.dynamic_update_index_in_dim(parts[k], own, chip, 0)
    half = core.astype(jnp.int32).reshape(1)
    totals = [sum_parts(half, part, f"sum_{n}") for n, part in zip(sh_names, parts)]
    sm_totals = sum_parts_small(chip.astype(jnp.int32).reshape(1), sm_parts, sm_sums, "sum_replicated")
    g_big, g_small = share_halves(totals, sm_totals, owners, "share_halves")
    outs = [dict(zip(sh_names, g_big)), {}, {}, {}]
    for n, g in zip(sh_names, g_big):
        outs[1][n], outs[2][n], outs[3][n] = adamw(g, w[n], m[n], v[n], f"adamw_{n}")
    small_results = [g_small[:-1]] + adamw_small(g_small[:-1], small_views(w), small_views(m), small_views(v),
                                                 "adamw_replicated")
    for j in range(4):
        for n, r in zip(REPLICATED, small_results[j]):
            outs[j][n] = r.reshape(-1)[:w[n].size].reshape(w[n].shape)
    return outs, g_small[-1][0, 0]
```

```python
import functools

import jax
import jax.numpy as jnp
from jax import lax
from jax.experimental import pallas as pl
from jax.experimental.pallas import tpu as pltpu

f32 = jnp.float32
bf16 = jnp.bfloat16

_MXU = bf16
_COMM = bf16
HIGH = lax.Precision.HIGH

D_MODEL = 1024
DEPTH = 2
D_PLE = 256
D_SSM = 256
D_DN = 512
D_SG = 256
SSM_GROUPS = 16
SSM_GROUP = 16
SSM_STATE = 64
N_STATE = SSM_GROUPS * SSM_STATE
DN_HEADS = 4
DN_HEAD_DIM = 128
DN_CONV = 4
DN_HALO = 16
DN_CHUNK = 64
SG_HEADS = 4
SG_HEAD_DIM = 64
SG_CHUNK = 128
S5_CHUNK = 1024
S5_GROUP_ROWS = 8
EPS = 1e-6
D_IN = 3336
D_IN_PAD = 3456
LANE = 128

ADAM_LR = 0.001
ADAM_B1 = 0.9
ADAM_B2 = 0.999
ADAM_EPS = 1e-08
ADAM_WD = 0.01
ADAM_STEP = 10

N_CHIPS = 4

Z_COLS = ((0, 512), (512, 2048), (2048, 2560), (2560, 3328), (3328, 3456))

GROUP_COLS = ((0, 512), (512, 2048), (2056, 2568), (2568, 3336), (2048, 2056))
SHARD_COLS = D_IN // 4

SHARDED = (("w_in", ("slot",)), ("ssm_w_glu", ("win", 0, 64)), ("dn_conv_w", ("win", 1, 384)),
           ("w_out", ("win", 0, 256)), ("w_ple_gate", ("win", 0, 256)), ("w_ple", ("win", 1, 256)))
REPLICATED = ("norm_g", "ssm_a_re", "ssm_a_im", "ssm_b_re", "ssm_b_im", "ssm_c_re", "ssm_c_im", "ssm_d",
              "ssm_log_step", "ssm_b_glu", "dn_a_log", "dn_dt_bias", "dn_norm_g", "sg_ln_g", "sg_ln_b", "sg_w",
              "sg_b", "ple_norm_g", "final_norm_g")
SMALL_OWNER = {n: int(n.startswith("ssm_")) for n in REPLICATED + ("loss",)}
WEIGHTS = ("norm_g", "w_in", "ssm_a_re", "ssm_a_im", "ssm_b_re", "ssm_b_im", "ssm_c_re", "ssm_c_im", "ssm_d",
           "ssm_log_step", "ssm_w_glu", "ssm_b_glu", "dn_conv_w", "dn_a_log", "dn_dt_bias", "dn_norm_g", "sg_ln_g",
           "sg_ln_b", "sg_w", "sg_b", "w_out", "ple_norm_g", "w_ple_gate", "w_ple", "final_norm_g")

VMEM_BIG = 56 * 1024 * 1024


def _mm(a, b):
    return jnp.dot(a.astype(_MXU), b.astype(_MXU), preferred_element_type=f32)


def _mm_nt(a, b):
    return lax.dot_general(a.astype(_MXU), b.astype(_MXU), (((1,), (1,)), ((), ())), preferred_element_type=f32)


def _mm_tn(a, b):
    return lax.dot_general(a.astype(_MXU), b.astype(_MXU), (((0,), (0,)), ((), ())), preferred_element_type=f32)


@jax.custom_vjp
def bdot(a, b):
    return _mm(a, b)


def _bdot_fwd(a, b):
    return _mm(a, b), (a, b)


def _bdot_bwd(res, g):
    a, b = res
    return _mm_nt(g, b).astype(a.dtype), _mm_tn(a, g).astype(b.dtype)


bdot.defvjp(_bdot_fwd, _bdot_bwd)


@jax.custom_vjp
def bdot_nt(a, b):
    return _mm_nt(a, b)


def _bdot_nt_fwd(a, b):
    return _mm_nt(a, b), (a, b)


def _bdot_nt_bwd(res, g):
    a, b = res
    return _mm(g, b).astype(a.dtype), _mm_tn(g, a).astype(b.dtype)


bdot_nt.defvjp(_bdot_nt_fwd, _bdot_nt_bwd)


@jax.custom_vjp
def bdot_tn(a, b):
    return _mm_tn(a, b)


def _bdot_tn_fwd(a, b):
    return _mm_tn(a, b), (a, b)


def _bdot_tn_bwd(res, g):
    a, b = res
    return _mm_nt(b, g).astype(a.dtype), _mm(a, g).astype(b.dtype)


bdot_tn.defvjp(_bdot_tn_fwd, _bdot_tn_bwd)


def hdot(a, b):
    return jnp.dot(a, b, precision=HIGH, preferred_element_type=f32)


def _unit_lower_inverses(ms):
    n = ms[0].shape[0]
    eye = (lax.broadcasted_iota(jnp.int32, (n, n), 0) == lax.broadcasted_iota(jnp.int32, (n, n), 1)).astype(f32)
    pw = [-m for m in ms]
    inv = [eye + p for p in pw]
    for _ in range(n.bit_length() - 2):
        pw = [hdot(p, p) for p in pw]
        inv = [a + hdot(a, p) for a, p in zip(inv, pw)]
    return inv


@jax.custom_vjp
def solve_unit_lower(ms, rhs, inv):
    return [hdot(a, r) for a, r in zip(inv, rhs)]


def _solve_unit_lower_fwd(ms, rhs, inv):
    xs = [hdot(a, r) for a, r in zip(inv, rhs)]
    return xs, (inv, xs)


def _solve_unit_lower_bwd(res, gs):
    inv, xs = res
    d_rhs = [lax.dot_general(a, g, (((0,), (0,)), ((), ())), precision=HIGH, preferred_element_type=f32)
             for a, g in zip(inv, gs)]
    d_ms = [-lax.dot_general(d, x, (((1,), (1,)), ((), ())), precision=HIGH, preferred_element_type=f32)
            for d, x in zip(d_rhs, xs)]
    return d_ms, d_rhs, [jnp.zeros_like(a) for a in inv]


solve_unit_lower.defvjp(_solve_unit_lower_fwd, _solve_unit_lower_bwd)


@functools.partial(jax.custom_vjp, nondiff_argnums=(1,))
def roll_rows(x, k):
    return pltpu.roll(x, k, 0)


def _roll_rows_fwd(x, k):
    return pltpu.roll(x, k, 0), None


def _roll_rows_bwd(k, _, g):
    return (pltpu.roll(g, g.shape[0] - k, 0),)


roll_rows.defvjp(_roll_rows_fwd, _roll_rows_bwd)


def _row_ids(shape):
    return lax.broadcasted_iota(jnp.int32, shape, 0)


def _rms(x, g):
    return x * lax.rsqrt(jnp.mean(x * x, axis=-1, keepdims=True) + EPS) * g


def _layer_norm(x, g, b):
    mu = jnp.mean(x, axis=-1, keepdims=True)
    xc = x - mu
    return xc * lax.rsqrt(jnp.mean(xc * xc, axis=-1, keepdims=True) + EPS) * g + b


def _s5_prep(are, aim, ls, bre, bim):
    step = jnp.exp(ls)
    mag = jnp.exp(are * step)
    lr = mag * jnp.cos(aim * step)
    li = mag * jnp.sin(aim * step)
    den = are * are + aim * aim
    nr, ni = lr - 1.0, li
    fr = (nr * are + ni * aim) / den
    fi = (ni * are - nr * aim) / den
    bbr = fr * bre - fi * bim
    bbi = fr * bim + fi * bre
    pr = jnp.broadcast_to(lr, (S5_GROUP_ROWS, N_STATE))
    pi = jnp.broadcast_to(li, (S5_GROUP_ROWS, N_STATE))
    d = 1
    while d < S5_GROUP_ROWS:
        keep = _row_ids(pr.shape) >= d
        sr, si = roll_rows(pr, d), roll_rows(pi, d)
        pr, pi = jnp.where(keep, pr * sr - pi * si, pr), jnp.where(keep, pr * si + pi * sr, pi)
        d *= 2
    return pr, pi, bbr, bbi


def _s5_chunk(u, gate, hr, hi, pr, pi, bbr, bbi, cr, ci, dr, wglu, bglu):
    n, steps = u.shape[0], S5_GROUP_ROWS
    groups = n // steps
    xr = bdot(u, bbr)
    xi = bdot(u, bbi)
    lr, li = pr[0:1], pi[0:1]
    rs, ims = [xr[:groups]], [xi[:groups]]
    for t in range(1, steps):
        a, b = rs[-1], ims[-1]
        rs.append(xr[t * groups:(t + 1) * groups] + lr * a - li * b)
        ims.append(xi[t * groups:(t + 1) * groups] + lr * b + li * a)
    er, ei = rs[-1], ims[-1]
    mr, mi = pr[steps - 1:steps], pi[steps - 1:steps]
    gid = _row_ids(er.shape)
    er, ei = (er + jnp.where(gid == 0, mr * hr - mi * hi, 0.0), ei + jnp.where(gid == 0, mr * hi + mi * hr, 0.0))
    d = 1
    while d < groups:
        sr = jnp.where(gid >= d, roll_rows(er, d), 0.0)
        si = jnp.where(gid >= d, roll_rows(ei, d), 0.0)
        er, ei = er + mr * sr - mi * si, ei + mr * si + mi * sr
        mr, mi = mr * mr - mi * mi, 2.0 * mr * mi
        d *= 2
    before_r = jnp.where(gid == 0, hr, roll_rows(er, 1))
    before_i = jnp.where(gid == 0, hi, roll_rows(ei, 1))
    xr = jnp.concatenate([rs[t] + pr[t:t + 1] * before_r - pi[t:t + 1] * before_i for t in range(steps)], axis=0)
    xi = jnp.concatenate([ims[t] + pr[t:t + 1] * before_i + pi[t:t + 1] * before_r for t in range(steps)], axis=0)
    y = bdot(xr, cr) - bdot(xi, ci) + dr * u
    y = jax.nn.gelu(y)
    y = y * jax.nn.sigmoid(bdot(y, wglu) + bglu)
    return y * jax.nn.silu(gate), er[groups - 1:groups], ei[groups - 1:groups]


def _dn_pre(xc, xp, w0, w1, w2, w3, is_start, col):
    xp = jnp.where(is_start, 0.0, xp)
    halo_rows = _row_ids(xp.shape)
    acc = w3 * xc
    for d, w in ((1, w2), (2, w1), (3, w0)):
        r = roll_rows(xc, d)
        head = jnp.where(halo_rows >= d, r[:DN_HALO], roll_rows(xp, d))
        acc = acc + w * jnp.concatenate([head, r[DN_HALO:]], axis=0)
    y = jax.nn.silu(acc)
    if col >= 2 * DN_HEADS:
        return y
    nrm = y * lax.rsqrt(jnp.sum(y * y, axis=-1, keepdims=True) + EPS)
    return nrm * DN_HEAD_DIM ** -0.5 if col < DN_HEADS else nrm


def _dn_local(qs, ks, vs, abs_, alog, dtb, invs=None):
    c = DN_CHUNK
    ri = lax.broadcasted_iota(jnp.int32, (c, c), 0)
    ci = lax.broadcasted_iota(jnp.int32, (c, c), 1)
    causal, strict = ri >= ci, ri > ci
    tril = causal.astype(f32)
    gcums = [hdot(tril, -jnp.exp(alog) * jax.nn.softplus(ab + dtb)) for ab in abs_]
    gcum_ts = [g.T for g in gcums]
    sigs = [jax.nn.sigmoid(ab) for ab in abs_]
    chains = [(j, h) for j in range(len(abs_)) for h in range(DN_HEADS)]
    gc = [gcums[j][:, h:h + 1] for j, h in chains]
    decay = [jnp.where(causal, jnp.exp(jnp.where(causal, gc[n] - gcum_ts[j][h:h + 1, :], 0.0)), 0.0)
             for n, (j, h) in enumerate(chains)]
    beta = [sigs[j][:, DN_HEADS + h:DN_HEADS + h + 1] for j, h in chains]
    kb = [ks[j][h] * beta[n] for n, (j, h) in enumerate(chains)]
    ms = [jnp.where(strict, bdot_nt(kb[n], ks[j][h]) * decay[n], 0.0) for n, (j, h) in enumerate(chains)]
    egc = [jnp.exp(g) for g in gc]
    rhs = [jnp.concatenate([vs[j][h] * beta[n], kb[n] * egc[n]], axis=1) for n, (j, h) in enumerate(chains)]
    inv = _unit_lower_inverses(ms) if invs is None else [invs[j][h] for j, h in chains]
    sol = solve_unit_lower(ms, rhs, inv)
    values = [s[:, :DN_HEAD_DIM] for s in sol]
    k_cds = [s[:, DN_HEAD_DIM:] for s in sol]
    attns = [bdot_nt(qs[j][h], ks[j][h]) * decay[n] for n, (j, h) in enumerate(chains)]
    q_decs = [qs[j][h] * egc[n] for n, (j, h) in enumerate(chains)]
    k_decs = [ks[j][h] * jnp.exp(gc[n][c - 1:c, :] - gc[n]) for n, (j, h) in enumerate(chains)]

    def nest(flat):
        return [flat[j * DN_HEADS:(j + 1) * DN_HEADS] for j in range(len(abs_))]

    lasts = [jnp.exp(g[c - 1:c, :]) for g in gcums]
    return nest(values), nest(k_cds), nest(attns), nest(q_decs), nest(k_decs), lasts, nest(inv)


def _dn_step(values, k_cds, attns, q_decs, k_decs, lasts, ggs, sts, ng):
    v_new = [v - bdot(kc, st) for v, kc, st in zip(values, k_cds, sts)]
    o = [bdot(qd, st) for qd, st in zip(q_decs, sts)]
    o = [a + bdot(at, vn) for a, at, vn in zip(o, attns, v_new)]
    new = [st * la + bdot_tn(kd, vn) for st, la, kd, vn in zip(sts, lasts, k_decs, v_new)]
    return [_rms(a, ng) * jax.nn.silu(g) for a, g in zip(o, ggs)], new


def _sg_chunk(u, v, gate, lng, lnb, ws, bt):
    n = SG_CHUNK
    ug = jax.nn.gelu(u)
    vn = _layer_norm(jax.nn.gelu(v), lng, lnb)
    causal = lax.broadcasted_iota(jnp.int32, (n, n), 0) >= lax.broadcasted_iota(jnp.int32, (n, n), 1)
    lane = lax.broadcasted_iota(jnp.int32, (n, D_SG), 1)
    s = jnp.zeros((n, D_SG), f32)
    for h in range(SG_HEADS):
        t = bdot(jnp.where(causal, ws[h], 0.0), vn) + bt[:, h:h + 1]
        s = s + jnp.where((lane >= h * SG_HEAD_DIM) & (lane < (h + 1) * SG_HEAD_DIM), t, 0.0)
    return ug * s * jax.nn.silu(gate)


def _cp(n_grid, vmem=None):
    return pltpu.CompilerParams(dimension_semantics=("arbitrary",) * n_grid, vmem_limit_bytes=vmem)


def _full(shape):
    nd = len(shape)
    return pl.BlockSpec(tuple(shape), lambda *_: (0,) * nd)


def _rows(tm, ncol):
    return pl.BlockSpec((tm, ncol), lambda i: (i, 0))


def _sds(shape, dtype=f32):
    return jax.ShapeDtypeStruct(tuple(shape), dtype)


def _acc(ref, val, first):
    @pl.when(first)
    def _():
        ref[...] = val

    @pl.when(jnp.logical_not(first))
    def _():
        ref[...] += val


def in_fwd(x, g, w, name):
    t, tm = x.shape[0], 512

    def body(x_ref, g_ref, w_ref, h_ref, *z_refs):
        h = _rms(x_ref[...], g_ref[...]).astype(_MXU)
        h_ref[...] = h
        for z_ref, (a, b) in zip(z_refs, Z_COLS):
            z_ref[...] = jnp.dot(h, w_ref[:, a:b], preferred_element_type=f32)

    widths = [b - a for a, b in Z_COLS]
    return pl.pallas_call(
        body, name=name, grid=(t // tm,),
        in_specs=[_rows(tm, D_MODEL), _full((1, D_MODEL)), _full((D_MODEL, D_IN_PAD))],
        out_specs=[_rows(tm, D_MODEL)] + [_rows(tm, n) for n in widths],
        out_shape=[_sds((t, D_MODEL), _MXU)] + [_sds((t, n)) for n in widths],
        compiler_params=_cp(1, VMEM_BIG),
    )(x, g, w)


def in_bwd(x, g, w, dzs, dres, name):
    t, tm = x.shape[0], 512
    widths = [b - a for a, b in Z_COLS]

    def body(x_ref, g_ref, w_ref, dres_ref, *rest):
        dz_refs, (dx_ref, dg_ref) = rest[:5], rest[5:]
        dh = jnp.zeros((tm, D_MODEL), f32)
        for dz_ref, (a, b) in zip(dz_refs, Z_COLS):
            dh = dh + _mm_nt(dz_ref[...], w_ref[:, a:b])
        _, vj = jax.vjp(_rms, x_ref[...], g_ref[...])
        dx, dg = vj(dh)
        dx_ref[...] = dres_ref[...] + dx
        _acc(dg_ref, dg, pl.program_id(0) == 0)

    return pl.pallas_call(
        body, name=name, grid=(t // tm,),
        in_specs=[_rows(tm, D_MODEL), _full((1, D_MODEL)), _full((D_MODEL, D_IN_PAD)), _rows(tm, D_MODEL)]
        + [_rows(tm, n) for n in widths],
        out_specs=[_rows(tm, D_MODEL), _full((1, D_MODEL))],
        out_shape=[_sds((t, D_MODEL)), _sds((1, D_MODEL))],
        compiler_params=_cp(1, VMEM_BIG),
    )(x, g, w, dres, *dzs)


def wgrad(a, g, name):
    t, k = a.shape
    n = g.shape[1]
    tm = min(t, 2048)
    tn = n if n <= 1536 else n // 2
    steps = t // tm

    def body(a_ref, g_ref, o_ref, acc):
        i = pl.program_id(1)
        _acc(acc, _mm_tn(a_ref[...], g_ref[...]), i == 0)

        @pl.when(i == steps - 1)
        def _():
            o_ref[...] = acc[...].astype(o_ref.dtype)

    return pl.pallas_call(
        body, name=name, grid=(n // tn, steps),
        in_specs=[pl.BlockSpec((tm, k), lambda j, i: (i, 0)), pl.BlockSpec((tm, tn), lambda j, i: (i, j))],
        out_specs=pl.BlockSpec((k, tn), lambda j, i: (0, j)),
        out_shape=_sds((k, n), _COMM),
        scratch_shapes=[pltpu.VMEM((k, tn), f32)],
        compiler_params=_cp(2, VMEM_BIG),
    )(a, g)


def post_fwd(x, ys, yd, yg, p, wout, pg, wgate, wple, name):
    t, tm = x.shape[0], 512

    def body(x_ref, ys_ref, yd_ref, yg_ref, p_ref, wout_ref, pg_ref, wgate_ref, wple_ref,
             x2_ref, x1_ref, y_ref, hn_ref):
        y = jnp.concatenate([ys_ref[...], yd_ref[...], yg_ref[...]], axis=1).astype(_MXU)
        y_ref[...] = y
        x1 = x_ref[...] + jnp.dot(y, wout_ref[...], preferred_element_type=f32)
        x1_ref[...] = x1
        hn = _rms(x1, pg_ref[...]).astype(_MXU)
        hn_ref[...] = hn
        gp = jnp.dot(hn, wgate_ref[...], preferred_element_type=f32)
        pp = _mm(p_ref[...], wple_ref[...])
        x2_ref[...] = x1 + jax.nn.sigmoid(gp) * pp

    return pl.pallas_call(
        body, name=name, grid=(t // tm,),
        in_specs=[_rows(tm, D_MODEL), _rows(tm, D_SSM), _rows(tm, D_DN), _rows(tm, D_SG), _rows(tm, D_PLE),
                  _full((D_MODEL, D_MODEL)), _full((1, D_MODEL)), _full((D_MODEL, D_MODEL)), _full((D_PLE, D_MODEL))],
        out_specs=[_rows(tm, D_MODEL)] * 4,
        out_shape=[_sds((t, D_MODEL)), _sds((t, D_MODEL)), _sds((t, D_MODEL), _MXU), _sds((t, D_MODEL), _MXU)],
        compiler_params=_cp(1, VMEM_BIG),
    )(x, ys, yd, yg, p, wout, pg, wgate, wple)


def post_bwd(dx2, x1, hn, p, wout, pg, wgate, wple, name):
    t, tm = dx2.shape[0], 512

    def body(dx2_ref, x1_ref, hn_ref, p_ref, wout_ref, pg_ref, wgate_ref, wple_ref,
             dx1_ref, dgp_ref, dpp_ref, dys_ref, dyd_ref, dyg_ref, dpg_ref):
        dx2 = dx2_ref[...]
        gp = jnp.dot(hn_ref[...], wgate_ref[...], preferred_element_type=f32)
        pp = _mm(p_ref[...], wple_ref[...])
        sg = jax.nn.sigmoid(gp)
        dpp_ref[...] = (dx2 * sg).astype(_MXU)
        dgp = (dx2 * pp * sg * (1.0 - sg)).astype(_MXU)
        dgp_ref[...] = dgp
        dhn = _mm_nt(dgp, wgate_ref[...])
        _, vj = jax.vjp(_rms, x1_ref[...], pg_ref[...])
        dx1n, dpg = vj(dhn)
        dx1 = dx2 + dx1n
        dx1_ref[...] = dx1
        dy = _mm_nt(dx1, wout_ref[...])
        dys_ref[...] = dy[:, :D_SSM]
        dyd_ref[...] = dy[:, D_SSM:D_SSM + D_DN]
        dyg_ref[...] = dy[:, D_SSM + D_DN:]
        _acc(dpg_ref, dpg, pl.program_id(0) == 0)

    return pl.pallas_call(
        body, name=name, grid=(t // tm,),
        in_specs=[_rows(tm, D_MODEL), _rows(tm, D_MODEL), _rows(tm, D_MODEL), _rows(tm, D_PLE),
                  _full((D_MODEL, D_MODEL)), _full((1, D_MODEL)), _full((D_MODEL, D_MODEL)), _full((D_PLE, D_MODEL))],
        out_specs=[_rows(tm, D_MODEL), _rows(tm, D_MODEL), _rows(tm, D_MODEL), _rows(tm, D_SSM), _rows(tm, D_DN),
                   _rows(tm, D_SG), _full((1, D_MODEL))],
        out_shape=[_sds((t, D_MODEL)), _sds((t, D_MODEL), _MXU), _sds((t, D_MODEL), _MXU), _sds((t, D_SSM)),
                   _sds((t, D_DN)), _sds((t, D_SG)), _sds((1, D_MODEL))],
        compiler_params=_cp(1, VMEM_BIG),
    )(dx2, x1, hn, p, wout, pg, wgate, wple)


def loss_fwd_bwd(x, fg, target, name):
    t, tm = x.shape[0], 1024

    def body(x_ref, fg_ref, t_ref, loss_ref, dx_ref, dfg_ref):
        def f(xv, gv):
            err = _rms(xv, gv) - t_ref[...]
            return 0.5 * jnp.sum(jnp.mean(err * err, axis=-1))

        val, vj = jax.vjp(f, x_ref[...], fg_ref[...])
        dx, dfg = vj(jnp.ones((), f32))
        dx_ref[...] = dx
        first = pl.program_id(0) == 0
        _acc(dfg_ref, dfg, first)
        _acc(loss_ref, jnp.full((8, LANE), val, f32), first)

    return pl.pallas_call(
        body, name=name, grid=(t // tm,),
        in_specs=[_rows(tm, D_MODEL), _full((1, D_MODEL)), _rows(tm, D_MODEL)],
        out_specs=[_full((8, LANE)), _rows(tm, D_MODEL), _full((1, D_MODEL))],
        out_shape=[_sds((8, LANE)), _sds((t, D_MODEL)), _sds((1, D_MODEL))],
        compiler_params=_cp(1),
    )(x, fg, target)


S5_PREPARED = 4
_S5_PARAM_SHAPES = ((S5_GROUP_ROWS, N_STATE), (S5_GROUP_ROWS, N_STATE), (D_SSM, N_STATE), (D_SSM, N_STATE),
                    (N_STATE, D_SSM), (N_STATE, D_SSM), (1, D_SSM), (D_SSM, D_SSM), (1, D_SSM))

def s5_prep_fwd(are, aim, ls, bre, bim, name):
    def body(are_ref, aim_ref, ls_ref, bre_ref, bim_ref, *outs):
        vals = _s5_prep(are_ref[...], aim_ref[...], ls_ref[...], bre_ref[...], bim_ref[...])
        for o, v in zip(outs, vals):
            o[...] = v

    return pl.pallas_call(body, name=name, out_shape=[_sds(s) for s in _S5_PARAM_SHAPES[:S5_PREPARED]])(
        are, aim, ls, bre, bim)


def s5_prep_bwd(are, aim, ls, bre, bim, cts, name):
    def body(are_ref, aim_ref, ls_ref, bre_ref, bim_ref, *rest):
        ct_refs, outs = rest[:S5_PREPARED], rest[S5_PREPARED:]
        _, vj = jax.vjp(_s5_prep, are_ref[...], aim_ref[...], ls_ref[...], bre_ref[...], bim_ref[...])
        for o, v in zip(outs, vj(tuple(r[...] for r in ct_refs))):
            o[...] = v

    shapes = [(1, N_STATE)] * 3 + [(D_SSM, N_STATE)] * 2
    return pl.pallas_call(body, name=name, out_shape=[_sds(s) for s in shapes])(are, aim, ls, bre, bim, *cts)


def _step_major(ref, cols):
    x = ref[:, cols]
    n, w = x.shape
    return jnp.swapaxes(x.reshape(n // S5_GROUP_ROWS, S5_GROUP_ROWS, w), 0, 1).reshape(n, w)


def _store_step_major(ref, cols, val):
    n, w = val.shape
    ref[:, cols] = jnp.swapaxes(val.reshape(S5_GROUP_ROWS, n // S5_GROUP_ROWS, w), 0, 1).reshape(n, w).astype(ref.dtype)


def s5_fwd(z, params, nb, name):
    t = z.shape[0]
    nc = t // nb // S5_CHUNK
    npar = len(_S5_PARAM_SHAPES)

    def body(z_ref, *rest):
        p_refs, (y_ref, hs_ref, hr_s, hi_s) = rest[:npar], rest[npar:]

        @pl.when(pl.program_id(1) == 0)
        def _():
            hr_s[...] = jnp.zeros_like(hr_s)
            hi_s[...] = jnp.zeros_like(hi_s)

        hr, hi = hr_s[...], hi_s[...]
        hs_ref[0, :, :N_STATE] = hr
        hs_ref[0, :, N_STATE:] = hi
        y, nhr, nhi = _s5_chunk(_step_major(z_ref, slice(0, D_SSM)), _step_major(z_ref, slice(D_SSM, 2 * D_SSM)),
                                hr, hi, *[r[...] for r in p_refs])
        _store_step_major(y_ref, slice(0, D_SSM), y)
        hr_s[...] = nhr
        hi_s[...] = nhi

    return pl.pallas_call(
        body, name=name, grid=(nb, nc),
        in_specs=[pl.BlockSpec((S5_CHUNK, 2 * D_SSM), lambda b, c: (b * nc + c, 0))]
        + [_full(s) for s in _S5_PARAM_SHAPES],
        out_specs=[pl.BlockSpec((S5_CHUNK, D_SSM), lambda b, c: (b * nc + c, 0)),
                   pl.BlockSpec((1, 1, 2 * N_STATE), lambda b, c: (b * nc + c, 0, 0))],
        out_shape=[_sds((t, D_SSM)), _sds((nb * nc, 1, 2 * N_STATE))],
        scratch_shapes=[pltpu.VMEM((1, N_STATE), f32), pltpu.VMEM((1, N_STATE), f32)],
        compiler_params=_cp(2, VMEM_BIG),
    )(z, *params)


def s5_bwd(z, params, hs, dy, nb, name):
    t = z.shape[0]
    nc = t // nb // S5_CHUNK
    npar = len(_S5_PARAM_SHAPES)

    def body(z_ref, hs_ref, dy_ref, *rest):
        p_refs, dz_ref, dp_refs, (dhr_s, dhi_s) = rest[:npar], rest[npar], rest[npar + 1:2 * npar + 1], rest[2 * npar + 1:]

        @pl.when(pl.program_id(1) == 0)
        def _():
            dhr_s[...] = jnp.zeros_like(dhr_s)
            dhi_s[...] = jnp.zeros_like(dhi_s)

        prim = (_step_major(z_ref, slice(0, D_SSM)), _step_major(z_ref, slice(D_SSM, 2 * D_SSM)),
                hs_ref[0, :, :N_STATE], hs_ref[0, :, N_STATE:]) + tuple(r[...] for r in p_refs)
        _, vj = jax.vjp(_s5_chunk, *prim)
        cts = vj((_step_major(dy_ref, slice(0, D_SSM)), dhr_s[...], dhi_s[...]))
        _store_step_major(dz_ref, slice(0, D_SSM), cts[0])
        _store_step_major(dz_ref, slice(D_SSM, 2 * D_SSM), cts[1])
        dhr_s[...] = cts[2]
        dhi_s[...] = cts[3]
        first = (pl.program_id(0) == 0) & (pl.program_id(1) == 0)
        for r, v in zip(dp_refs, cts[4:]):
            _acc(r, v, first)

    rev = lambda b, c: (b * nc + nc - 1 - c, 0)
    return pl.pallas_call(
        body, name=name, grid=(nb, nc),
        in_specs=[pl.BlockSpec((S5_CHUNK, 2 * D_SSM), rev),
                  pl.BlockSpec((1, 1, 2 * N_STATE), lambda b, c: (b * nc + nc - 1 - c, 0, 0)),
                  pl.BlockSpec((S5_CHUNK, D_SSM), rev)] + [_full(s) for s in _S5_PARAM_SHAPES],
        out_specs=[pl.BlockSpec((S5_CHUNK, 2 * D_SSM), rev)] + [_full(s) for s in _S5_PARAM_SHAPES],
        out_shape=[_sds((t, 2 * D_SSM), _MXU)] + [_sds(s) for s in _S5_PARAM_SHAPES],
        scratch_shapes=[pltpu.VMEM((1, N_STATE), f32), pltpu.VMEM((1, N_STATE), f32)],
        compiler_params=_cp(2, VMEM_BIG),
    )(z, hs, dy, *params)


DN_PRE_ROWS = 512
DN_COLS = 3 * D_DN // LANE


def dn_pre_bwd(zq, convw, dqkv, seq, name):
    t, tb = zq.shape[0], DN_PRE_ROWS
    nrow = t // tb
    per_seq = seq // tb

    def body(xc_ref, xp_ref, w_ref, d_ref, dx_ref, dw_ref, carry):
        step = pl.program_id(0)
        i = nrow - 1 - step

        @pl.when(step == 0)
        def _():
            carry[...] = jnp.zeros_like(carry)

        for j in range(DN_COLS):
            cols = slice(j * LANE, (j + 1) * LANE)
            fn = functools.partial(_dn_pre, is_start=i % per_seq == 0, col=j)
            _, vj = jax.vjp(fn, xc_ref[:, cols], xp_ref[:, cols], w_ref[0:1, cols], w_ref[1:2, cols],
                            w_ref[2:3, cols], w_ref[3:4, cols])
            dxc, dxp, dw0, dw1, dw2, dw3 = vj(d_ref[:, cols])
            dx_ref[:tb - DN_HALO, cols] = dxc[:tb - DN_HALO].astype(_MXU)
            dx_ref[tb - DN_HALO:, cols] = (dxc[tb - DN_HALO:] + carry[:, cols]).astype(_MXU)
            carry[:, cols] = dxp
            for k, dw in enumerate((dw0, dw1, dw2, dw3)):
                @pl.when(step == 0)
                def _():
                    dw_ref[k:k + 1, cols] = dw

                @pl.when(step != 0)
                def _():
                    dw_ref[k:k + 1, cols] += dw

    rev = lambda s: (nrow - 1 - s, 0)
    return pl.pallas_call(
        body, name=name, grid=(nrow,),
        in_specs=[pl.BlockSpec((tb, 3 * D_DN), rev),
                  pl.BlockSpec((DN_HALO, 3 * D_DN),
                               lambda s: (jnp.maximum((nrow - 1 - s) * (tb // DN_HALO) - 1, 0), 0)),
                  _full((DN_CONV, 3 * D_DN)), pl.BlockSpec((tb, 3 * D_DN), rev)],
        out_specs=[pl.BlockSpec((tb, 3 * D_DN), rev), _full((DN_CONV, 3 * D_DN))],
        out_shape=[_sds((t, 3 * D_DN), _MXU), _sds((DN_CONV, 3 * D_DN))],
        scratch_shapes=[pltpu.VMEM((DN_HALO, 3 * D_DN), f32)],
        compiler_params=_cp(1, VMEM_BIG),
    )(zq, zq, convw, dqkv)


DN_LOCAL_CHUNKS = 4
DN_FRONT_CHUNKS = 8
DN_ATTN = DN_HEADS * DN_CHUNK


def _dn_heads(ref, rows, base=0):
    return [ref[rows, base + h * DN_HEAD_DIM:base + (h + 1) * DN_HEAD_DIM] for h in range(DN_HEADS)]


def dn_front_fwd(zq, convw, ab, alog, dtb, seq, name):
    t = zq.shape[0]
    c, n = DN_CHUNK, DN_FRONT_CHUNKS
    per_seq = seq // (n * c)

    def body(xc_ref, xp_ref, w_ref, ab_ref, alog_ref, dtb_ref,
             qkv_ref, val_ref, kcd_ref, attn_ref, qd_ref, kd_ref, el_ref, inv_ref):
        is_start = pl.program_id(0) % per_seq == 0
        blocks = []
        for j in range(DN_COLS):
            cols = slice(j * LANE, (j + 1) * LANE)
            blocks.append(_dn_pre(xc_ref[:, cols], xp_ref[:, cols], w_ref[0:1, cols], w_ref[1:2, cols],
                                  w_ref[2:3, cols], w_ref[3:4, cols], is_start, j))
            qkv_ref[:, cols] = blocks[-1]
        rows = [pl.ds(j * c, c) for j in range(n)]

        def heads(base, j):
            return [blocks[base + h][j * c:(j + 1) * c] for h in range(DN_HEADS)]

        vals, kcds, attns, qds, kds, els, invs = _dn_local(
            [heads(0, j) for j in range(n)], [heads(DN_HEADS, j) for j in range(n)],
            [heads(2 * DN_HEADS, j) for j in range(n)], [ab_ref[r, :] for r in rows], alog_ref[...], dtb_ref[...])
        for j, r in enumerate(rows):
            for h in range(DN_HEADS):
                lo, hi = h * DN_HEAD_DIM, (h + 1) * DN_HEAD_DIM
                val_ref[r, lo:hi] = vals[j][h]
                kcd_ref[r, lo:hi] = kcds[j][h].astype(_MXU)
                qd_ref[r, lo:hi] = qds[j][h].astype(_MXU)
                kd_ref[r, lo:hi] = kds[j][h].astype(_MXU)
                attn_ref[r, h * c:(h + 1) * c] = attns[j][h].astype(_MXU)
                inv_ref[r, h * c:(h + 1) * c] = invs[j][h]
            el_ref[j] = els[j]

    wide = _rows(n * c, D_DN)
    outs = pl.pallas_call(
        body, name=name, grid=(t // (n * c),),
        in_specs=[_rows(n * c, 3 * D_DN),
                  pl.BlockSpec((DN_HALO, 3 * D_DN), lambda i: (jnp.maximum(i * (n * c // DN_HALO) - 1, 0), 0)),
                  _full((DN_CONV, 3 * D_DN)), _rows(n * c, LANE), _full((1, LANE)), _full((1, LANE))],
        out_specs=[_rows(n * c, 3 * D_DN), wide, wide, _rows(n * c, DN_ATTN), wide, wide,
                   pl.BlockSpec((n, 1, LANE), lambda i: (i, 0, 0)), _rows(n * c, DN_ATTN)],
        out_shape=[_sds((t, 3 * D_DN)), _sds((t, D_DN)), _sds((t, D_DN), _MXU), _sds((t, DN_ATTN), _MXU),
                   _sds((t, D_DN), _MXU), _sds((t, D_DN), _MXU), _sds((t // c, 1, LANE)), _sds((t, DN_ATTN))],
        compiler_params=_cp(1, VMEM_BIG),
    )(zq, zq, convw, ab, alog, dtb)
    return outs[0], outs[1:7], outs[7]


def dn_local_bwd(qkv, ab, alog, dtb, inv, cts, name):
    t = qkv.shape[0]
    c, n = DN_CHUNK, DN_LOCAL_CHUNKS

    def body(qkv_ref, ab_ref, alog_ref, dtb_ref, inv_ref, dval_ref, dkcd_ref, dattn_ref, dqd_ref, dkd_ref, del_ref,
             dqkv_ref, dab_ref, dalog_ref, ddtb_ref):
        rows = [pl.ds(j * c, c) for j in range(n)]
        invs = [[inv_ref[r, h * c:(h + 1) * c] for h in range(DN_HEADS)] for r in rows]

        def local(qs, ks, vs, abs_, alog, dtb):
            return _dn_local(qs, ks, vs, abs_, alog, dtb, invs)[:6]

        _, vj = jax.vjp(local, [_dn_heads(qkv_ref, r) for r in rows], [_dn_heads(qkv_ref, r, D_DN) for r in rows],
                        [_dn_heads(qkv_ref, r, 2 * D_DN) for r in rows], [ab_ref[r, :] for r in rows], alog_ref[...],
                        dtb_ref[...])
        dattn = [[dattn_ref[r, h * c:(h + 1) * c] for h in range(DN_HEADS)] for r in rows]
        dq, dk, dv, dab, dalog, ddtb = vj(([_dn_heads(dval_ref, r) for r in rows], [_dn_heads(dkcd_ref, r) for r in rows],
                                           dattn, [_dn_heads(dqd_ref, r) for r in rows],
                                           [_dn_heads(dkd_ref, r) for r in rows], [del_ref[j] for j in range(n)]))
        for j, r in enumerate(rows):
            for h in range(DN_HEADS):
                lo, hi = h * DN_HEAD_DIM, (h + 1) * DN_HEAD_DIM
                dqkv_ref[r, lo:hi] = dq[j][h]
                dqkv_ref[r, D_DN + lo:D_DN + hi] = dk[j][h]
                dqkv_ref[r, 2 * D_DN + lo:2 * D_DN + hi] = dv[j][h]
            dab_ref[r, :] = dab[j].astype(_MXU)
        first = pl.program_id(0) == 0
        _acc(dalog_ref, dalog, first)
        _acc(ddtb_ref, ddtb, first)

    wide = _rows(n * c, D_DN)
    return pl.pallas_call(
        body, name=name, grid=(t // (n * c),),
        in_specs=[_rows(n * c, 3 * D_DN), _rows(n * c, LANE), _full((1, LANE)), _full((1, LANE)),
                  _rows(n * c, DN_ATTN), wide, wide, _rows(n * c, DN_ATTN), wide, wide,
                  pl.BlockSpec((n, 1, LANE), lambda i: (i, 0, 0))],
        out_specs=[_rows(n * c, 3 * D_DN), _rows(n * c, LANE), _full((1, LANE)), _full((1, LANE))],
        out_shape=[_sds((t, 3 * D_DN)), _sds((t, LANE), _MXU), _sds((1, LANE)), _sds((1, LANE))],
        compiler_params=_cp(1),
    )(qkv, ab, alog, dtb, inv, *cts)


def _seq_view(a, nb):
    return a.reshape((nb, a.shape[0] // nb) + a.shape[1:])


def _dn_chains(nb):
    return [(b, h) for b in range(nb) for h in range(DN_HEADS)]


DN_SCAN_CHUNKS = 4
DN_SCAN_FWD_CHUNKS = 8


def _dn_step_operands(val_ref, kcd_ref, attn_ref, qd_ref, kd_ref, el_ref, gg_ref, nb, j):
    chains = _dn_chains(nb)
    c = DN_CHUNK
    rows = pl.ds(j * c, c)

    def wide(ref):
        return [ref[b, rows, h * DN_HEAD_DIM:(h + 1) * DN_HEAD_DIM].astype(f32) for b, h in chains]

    attns = [attn_ref[b, rows, h * c:(h + 1) * c].astype(f32) for b, h in chains]
    return (wide(val_ref), wide(kcd_ref), attns, wide(qd_ref), wide(kd_ref),
            [el_ref[b, j, :, h:h + 1] for b, h in chains], wide(gg_ref))


def dn_scan_fwd(loc, gg, ng, nb, name):
    val, kcd, attn, qd, kd, el = loc
    t = val.shape[0]
    c, n = DN_CHUNK, DN_SCAN_FWD_CHUNKS
    nc = t // nb // c
    ns = nb * DN_HEADS

    def body(val_ref, kcd_ref, attn_ref, qd_ref, kd_ref, el_ref, gg_ref, ng_ref, y_ref, ss_ref, st):
        @pl.when(pl.program_id(0) == 0)
        def _():
            st[...] = jnp.zeros_like(st)

        sts = [st[i] for i in range(ns)]
        for j in range(n):
            for i in range(ns):
                ss_ref[j, i] = sts[i]
            ys, sts = _dn_step(*_dn_step_operands(val_ref, kcd_ref, attn_ref, qd_ref, kd_ref, el_ref, gg_ref, nb, j),
                               sts, ng_ref[...])
            for i, (b, h) in enumerate(_dn_chains(nb)):
                y_ref[b, pl.ds(j * c, c), h * DN_HEAD_DIM:(h + 1) * DN_HEAD_DIM] = ys[i]
        for i in range(ns):
            st[i] = sts[i]

    def blk(w):
        return pl.BlockSpec((nb, n * c, w), lambda k: (0, k, 0))

    el_spec = pl.BlockSpec((nb, n, 1, LANE), lambda k: (0, k, 0, 0))
    y, ss = pl.pallas_call(
        body, name=name, grid=(nc // n,),
        in_specs=[blk(D_DN), blk(D_DN), blk(DN_ATTN), blk(D_DN), blk(D_DN), el_spec, blk(D_DN), _full((1, LANE))],
        out_specs=[blk(D_DN), pl.BlockSpec((n, ns, DN_HEAD_DIM, DN_HEAD_DIM), lambda k: (k, 0, 0, 0))],
        out_shape=[_sds((nb, t // nb, D_DN)), _sds((nc, ns, DN_HEAD_DIM, DN_HEAD_DIM))],
        scratch_shapes=[pltpu.VMEM((ns, DN_HEAD_DIM, DN_HEAD_DIM), f32)],
        compiler_params=_cp(1, VMEM_BIG),
    )(_seq_view(val, nb), _seq_view(kcd, nb), _seq_view(attn, nb), _seq_view(qd, nb), _seq_view(kd, nb),
      el.reshape(nb, nc, 1, LANE), _seq_view(gg, nb), ng)
    return y.reshape(t, D_DN), ss


def dn_scan_bwd(loc, gg, ng, ss, dy, nb, name):
    val, kcd, attn, qd, kd, el = loc
    t = val.shape[0]
    c, n = DN_CHUNK, DN_SCAN_CHUNKS
    nc = t // nb // c
    ns = nb * DN_HEADS
    steps = nc // n

    def body(val_ref, kcd_ref, attn_ref, qd_ref, kd_ref, el_ref, gg_ref, ng_ref, ss_ref, dy_ref,
             dval_ref, dkcd_ref, dattn_ref, dqd_ref, dkd_ref, del_ref, dgg_ref, dng_ref, dst):
        @pl.when(pl.program_id(0) == 0)
        def _():
            dst[...] = jnp.zeros_like(dst)

        lane = lax.broadcasted_iota(jnp.int32, (1, LANE), 1)
        chains = _dn_chains(nb)
        ds = [dst[i] for i in range(ns)]
        dng_tot = jnp.zeros((1, LANE), f32)
        for j in reversed(range(n)):
            rows = pl.ds(j * c, c)
            _, vj = jax.vjp(_dn_step,
                            *_dn_step_operands(val_ref, kcd_ref, attn_ref, qd_ref, kd_ref, el_ref, gg_ref, nb, j),
                            [ss_ref[j, i] for i in range(ns)], ng_ref[...])
            dys = [dy_ref[b, rows, h * DN_HEAD_DIM:(h + 1) * DN_HEAD_DIM] for b, h in chains]
            dval, dkcd, dattn, dqd, dkd, dlast, dgg, ds, dng = vj((dys, ds))
            dng_tot = dng_tot + dng
            del_rows = [jnp.zeros((1, LANE), f32) for _ in range(nb)]
            for i, (b, h) in enumerate(chains):
                cols = slice(h * DN_HEAD_DIM, (h + 1) * DN_HEAD_DIM)
                dval_ref[b, rows, cols] = dval[i]
                dkcd_ref[b, rows, cols] = dkcd[i]
                dattn_ref[b, rows, h * c:(h + 1) * c] = dattn[i]
                dqd_ref[b, rows, cols] = dqd[i]
                dkd_ref[b, rows, cols] = dkd[i]
                dgg_ref[b, rows, cols] = dgg[i].astype(_MXU)
                del_rows[b] = del_rows[b] + jnp.where(lane == h, dlast[i], 0.0)
            for b in range(nb):
                del_ref[b, j] = del_rows[b]
        for i in range(ns):
            dst[i] = ds[i]
        _acc(dng_ref, dng_tot, pl.program_id(0) == 0)

    def blk(w):
        return pl.BlockSpec((nb, n * c, w), lambda k: (0, steps - 1 - k, 0))

    el_spec = pl.BlockSpec((nb, n, 1, LANE), lambda k: (0, steps - 1 - k, 0, 0))
    outs = pl.pallas_call(
        body, name=name, grid=(steps,),
        in_specs=[blk(D_DN), blk(D_DN), blk(DN_ATTN), blk(D_DN), blk(D_DN), el_spec, blk(D_DN), _full((1, LANE)),
                  pl.BlockSpec((n, ns, DN_HEAD_DIM, DN_HEAD_DIM), lambda k: (steps - 1 - k, 0, 0, 0)), blk(D_DN)],
        out_specs=[blk(D_DN), blk(D_DN), blk(DN_ATTN), blk(D_DN), blk(D_DN), el_spec, blk(D_DN), _full((1, LANE))],
        out_shape=[_sds((nb, t // nb, D_DN)), _sds((nb, t // nb, D_DN)), _sds((nb, t // nb, DN_ATTN)),
                   _sds((nb, t // nb, D_DN)), _sds((nb, t // nb, D_DN)), _sds((nb, nc, 1, LANE)),
                   _sds((nb, t // nb, D_DN), _MXU), _sds((1, LANE))],
        scratch_shapes=[pltpu.VMEM((ns, DN_HEAD_DIM, DN_HEAD_DIM), f32)],
        compiler_params=_cp(1, VMEM_BIG),
    )(_seq_view(val, nb), _seq_view(kcd, nb), _seq_view(attn, nb), _seq_view(qd, nb), _seq_view(kd, nb),
      el.reshape(nb, nc, 1, LANE), _seq_view(gg, nb), ng, ss, _seq_view(dy, nb))
    dloc = [o.reshape((t,) + o.shape[2:]) for o in outs[:5]] + [outs[5].reshape(t // c, 1, LANE)]
    return dloc, outs[6].reshape(t, D_DN), outs[7]


SG_ROWS = 1024


def sg_fwd(z, lng, lnb, w, bt, name):
    t = z.shape[0]

    def body(z_ref, lng_ref, lnb_ref, w_ref, bt_ref, y_ref):
        ws = [w_ref[h] for h in range(SG_HEADS)]
        for k in range(SG_ROWS // SG_CHUNK):
            r = pl.ds(k * SG_CHUNK, SG_CHUNK)
            y_ref[r, :] = _sg_chunk(z_ref[r, :D_SG], z_ref[r, D_SG:2 * D_SG], z_ref[r, 2 * D_SG:], lng_ref[...],
                                    lnb_ref[...], ws, bt_ref[...])

    return pl.pallas_call(
        body, name=name, grid=(t // SG_ROWS,),
        in_specs=[_rows(SG_ROWS, 3 * D_SG), _full((1, D_SG)), _full((1, D_SG)),
                  _full((SG_HEADS, SG_CHUNK, SG_CHUNK)), _full((SG_CHUNK, LANE))],
        out_specs=_rows(SG_ROWS, D_SG),
        out_shape=_sds((t, D_SG)),
        compiler_params=_cp(1),
    )(z, lng, lnb, w, bt)


def sg_bwd(z, lng, lnb, w, bt, dy, name):
    t = z.shape[0]

    def body(z_ref, lng_ref, lnb_ref, w_ref, bt_ref, dy_ref, dz_ref, dlng_ref, dlnb_ref, dw_ref, dbt_ref):
        ws = [w_ref[h] for h in range(SG_HEADS)]
        tot = None
        for k in range(SG_ROWS // SG_CHUNK):
            r = pl.ds(k * SG_CHUNK, SG_CHUNK)
            _, vj = jax.vjp(_sg_chunk, z_ref[r, :D_SG], z_ref[r, D_SG:2 * D_SG], z_ref[r, 2 * D_SG:], lng_ref[...],
                            lnb_ref[...], ws, bt_ref[...])
            du, dv, dgate, dlng, dlnb, dws, dbt = vj(dy_ref[r, :])
            dz_ref[r, :D_SG] = du.astype(_MXU)
            dz_ref[r, D_SG:2 * D_SG] = dv.astype(_MXU)
            dz_ref[r, 2 * D_SG:] = dgate.astype(_MXU)
            part = [dlng, dlnb, dbt] + list(dws)
            tot = part if tot is None else [a + b for a, b in zip(tot, part)]
        first = pl.program_id(0) == 0
        _acc(dlng_ref, tot[0], first)
        _acc(dlnb_ref, tot[1], first)
        _acc(dbt_ref, tot[2], first)
        for h in range(SG_HEADS):
            @pl.when(first)
            def _():
                dw_ref[h] = tot[3 + h]

            @pl.when(jnp.logical_not(first))
            def _():
                dw_ref[h] += tot[3 + h]

    return pl.pallas_call(
        body, name=name, grid=(t // SG_ROWS,),
        in_specs=[_rows(SG_ROWS, 3 * D_SG), _full((1, D_SG)), _full((1, D_SG)),
                  _full((SG_HEADS, SG_CHUNK, SG_CHUNK)), _full((SG_CHUNK, LANE)), _rows(SG_ROWS, D_SG)],
        out_specs=[_rows(SG_ROWS, 3 * D_SG), _full((1, D_SG)), _full((1, D_SG)),
                   _full((SG_HEADS, SG_CHUNK, SG_CHUNK)), _full((SG_CHUNK, LANE))],
        out_shape=[_sds((t, 3 * D_SG), _MXU), _sds((1, D_SG)), _sds((1, D_SG)), _sds((SG_HEADS, SG_CHUNK, SG_CHUNK)),
                   _sds((SG_CHUNK, LANE))],
        compiler_params=_cp(1),
    )(z, lng, lnb, w, bt, dy)


def add_pairs(a_list, b_list, name):
    n = len(a_list)

    def body(*refs):
        for a_ref, b_ref, o_ref in zip(refs[:n], refs[n:2 * n], refs[2 * n:]):
            o_ref[...] = (a_ref[...].astype(f32) + b_ref[...].astype(f32)).astype(o_ref.dtype)

    return pl.pallas_call(
        body, name=name, out_shape=[_sds(a.shape, a.dtype) for a in a_list],
        compiler_params=pltpu.CompilerParams(vmem_limit_bytes=VMEM_BIG),
    )(*a_list, *b_list)


def _adamw(g, w, m, v):
    nm = ADAM_B1 * m + (1.0 - ADAM_B1) * g
    nv = ADAM_B2 * v + (1.0 - ADAM_B2) * jnp.square(g)
    m_hat = nm / (1.0 - ADAM_B1 ** ADAM_STEP)
    v_hat = nv / (1.0 - ADAM_B2 ** ADAM_STEP)
    return -ADAM_LR * (m_hat / (jnp.sqrt(v_hat) + ADAM_EPS) + ADAM_WD * w), nm, nv


def sum_parts(half, recv, name):
    _, r, c = recv.shape
    tr = 256 if r % 256 == 0 else r

    def body(half_ref, recv_ref, g_ref):
        g = recv_ref[0].astype(f32)
        for k in range(1, N_CHIPS):
            g = g + recv_ref[k].astype(f32)
        g_ref[...] = g

    return pl.pallas_call(
        body, name=name,
        grid_spec=pltpu.PrefetchScalarGridSpec(
            num_scalar_prefetch=1, grid=(r // tr,),
            in_specs=[pl.BlockSpec((N_CHIPS, tr, c), lambda i, h: (0, i, 0))],
            out_specs=pl.BlockSpec((None, tr, c), lambda i, h: (h[0], i, 0))),
        out_shape=_sds((2, r, c)),
        compiler_params=_cp(1, VMEM_BIG),
    )(half, recv)


def adamw(g, w, m, v, name):
    _, r, c = w.shape
    tr = 256 if r % 256 == 0 else r

    def body(g_ref, w_ref, m_ref, v_ref, d_ref, nm_ref, nv_ref):
        d_ref[...], nm_ref[...], nv_ref[...] = _adamw(g_ref[...], w_ref[...], m_ref[...], v_ref[...])

    blk = pl.BlockSpec((None, tr, c), lambda l, i: (l, i, 0))
    return pl.pallas_call(
        body, name=name, grid=(2, r // tr), in_specs=[blk] * 4, out_specs=[blk] * 3, out_shape=[_sds((2, r, c))] * 3,
        compiler_params=_cp(2, VMEM_BIG),
    )(g, w, m, v)


def sum_parts_small(chip, parts, sums, name):
    n = len(sums)

    def body(chip_ref, *refs):
        for part, own, out in zip(refs[:n], refs[n:2 * n], refs[2 * n:]):
            g = jnp.where(chip_ref[0] == 0, own[...], part[0])
            for q in range(1, N_CHIPS):
                g = g + jnp.where(chip_ref[0] == q, own[...], part[q])
            out[...] = g

    vmem = pl.BlockSpec(memory_space=pltpu.VMEM)
    return pl.pallas_call(
        body, name=name, in_specs=[pl.BlockSpec(memory_space=pltpu.SMEM)] + [vmem] * (2 * n), out_specs=[vmem] * n,
        out_shape=[_sds(s.shape) for s in sums], compiler_params=pltpu.CompilerParams(vmem_limit_bytes=VMEM_BIG),
    )(chip, *parts, *sums)


def adamw_small(gs, ws, ms, vs, name):
    n = len(ws)

    def body(*refs):
        ins, outs = refs[:4 * n], refs[4 * n:]
        for k in range(n):
            outs[k][...], outs[n + k][...], outs[2 * n + k][...] = _adamw(
                ins[k][...], ins[n + k][...], ins[2 * n + k][...], ins[3 * n + k][...])

    outs = pl.pallas_call(
        body, name=name, out_shape=[_sds(w.shape) for w in ws] * 3,
        compiler_params=pltpu.CompilerParams(vmem_limit_bytes=VMEM_BIG),
    )(*gs, *ws, *ms, *vs)
    return [outs[j * n:(j + 1) * n] for j in range(3)]


_ANY = pl.BlockSpec(memory_space=pltpu.HBM)
_MESH = pl.DeviceIdType.MESH


def _flip(v, bit):
    return 1 - v if bit else v


_CHIP_RELS = ((1, 0), (0, 1), (1, 1))


def _piece(ref, kind, q):
    if kind[0] == "slot":
        return ref.at[q]
    if kind[0] == "all":
        return ref
    _, axis, n = kind
    return ref.at[(slice(None),) * axis + (pl.ds(q * n, n),)]


def _piece_shape(shape, kind):
    if kind[0] == "slot":
        return tuple(shape[1:])
    if kind[0] == "all":
        return tuple(shape)
    _, axis, n = kind
    return tuple(shape[:axis]) + (n,) + tuple(shape[axis + 1:])


def gather_weights(shards, kinds, name):
    n = len(shards)

    def out_shape(s, kind):
        if kind[0] == "slot":
            return (N_CHIPS,) + tuple(s.shape)
        _, axis, w = kind
        return tuple(s.shape[:axis + 1]) + (N_CHIPS * w,) + tuple(s.shape[axis + 2:])

    def place(o_ref, kind, q, layer):
        if kind[0] == "slot":
            return o_ref.at[q, layer]
        return _piece(o_ref.at[layer], kind, q)

    def body(*refs):
        s_refs, o_refs = refs[:n], refs[n:2 * n]
        send_sems, recv_sems, fwd_send_sems, fwd_recv_sems = refs[2 * n:]
        x, y, c = lax.axis_index("x"), lax.axis_index("y"), lax.axis_index("c")
        mine = 2 * x + y
        sends, arrivals, forwards, fwd_arrivals = [], [], [], []
        for r, (fx, fy) in enumerate(_CHIP_RELS):
            px, py = _flip(x, fx), _flip(y, fy)
            peer = 2 * px + py
            for k in range(n):
                s = r * n + k
                sends.append(pltpu.make_async_remote_copy(
                    src_ref=s_refs[k].at[c], dst_ref=place(o_refs[k], kinds[k], mine, c), send_sem=send_sems.at[s],
                    recv_sem=recv_sems.at[s], device_id=(px, py, c), device_id_type=_MESH))
                arrivals.append(pltpu.make_async_remote_copy(
                    src_ref=s_refs[k].at[c], dst_ref=place(o_refs[k], kinds[k], peer, c), send_sem=send_sems.at[s],
                    recv_sem=recv_sems.at[s], device_id=(px, py, c), device_id_type=_MESH))
                block = place(o_refs[k], kinds[k], peer, c)
                forwards.append(pltpu.make_async_remote_copy(
                    src_ref=block, dst_ref=block, send_sem=fwd_send_sems.at[s], recv_sem=fwd_recv_sems.at[s],
                    device_id=(x, y, 1 - c), device_id_type=_MESH))
                other = place(o_refs[k], kinds[k], peer, 1 - c)
                fwd_arrivals.append(pltpu.make_async_remote_copy(
                    src_ref=other, dst_ref=other, send_sem=fwd_send_sems.at[s], recv_sem=fwd_recv_sems.at[s],
                    device_id=(x, y, 1 - c), device_id_type=_MESH))
        for cp in sends:
            cp.start()
        for arrived, fwd in zip(arrivals, forwards):
            arrived.wait_recv()
            fwd.start()
        for cp in fwd_arrivals:
            cp.wait_recv()
        for cp in sends + forwards:
            cp.wait_send()

    m = len(_CHIP_RELS) * n
    return pl.pallas_call(
        body, name=name, in_specs=[_ANY] * n, out_specs=[_ANY] * n,
        out_shape=[_sds(out_shape(s, k), s.dtype) for s, k in zip(shards, kinds)],
        scratch_shapes=[pltpu.SemaphoreType.DMA((m,))] * 4,
    )(*shards)


def _owned_by(owners, side):
    return [k for k, o in enumerate(owners) if o == side]


def exchange_halves(gs, smalls, owners, name):
    n, ns = len(gs), len(smalls)

    def body(*refs):
        g_refs, s_refs = refs[:n], refs[n:n + ns]
        got_refs, sgot_refs = refs[n + ns:2 * n + ns], refs[2 * n + ns:2 * (n + ns)]
        send_sems, recv_sems = refs[2 * (n + ns):]
        x, y, c = lax.axis_index("x"), lax.axis_index("y"), lax.axis_index("c")
        sibling = (x, y, 1 - c)
        swaps = [pltpu.make_async_remote_copy(
            src_ref=g_refs[k].at[1 - c], dst_ref=got_refs[k], send_sem=send_sems.at[k], recv_sem=recv_sems.at[k],
            device_id=sibling, device_id_type=_MESH) for k in range(n)]
        gives = [pltpu.make_async_remote_copy(
            src_ref=s_refs[k], dst_ref=sgot_refs[k], send_sem=send_sems.at[n + k], recv_sem=recv_sems.at[n + k],
            device_id=sibling, device_id_type=_MESH) for k in range(ns)]
        for cp in swaps:
            cp.start()
        for side in (0, 1):
            @pl.when(c == 1 - side)
            def _():
                for k in _owned_by(owners, side):
                    gives[k].start()
        for cp in swaps:
            cp.wait()
        for side in (0, 1):
            @pl.when(c == 1 - side)
            def _():
                for k in _owned_by(owners, side):
                    gives[k].wait_send()

            @pl.when(c == side)
            def _():
                for k in _owned_by(owners, side):
                    gives[k].wait_recv()

    outs = pl.pallas_call(
        body, name=name, in_specs=[_ANY] * (n + ns), out_specs=[_ANY] * (n + ns),
        out_shape=[_sds(g.shape[1:], g.dtype) for g in gs] + [_sds(s.shape, s.dtype) for s in smalls],
        scratch_shapes=[pltpu.SemaphoreType.DMA((n + ns,)), pltpu.SemaphoreType.DMA((n + ns,))],
    )(*gs, *smalls)
    return outs[:n], outs[n:]


def reduce_to_chips(ts, kinds, smalls, owners, name):
    n, ns = len(ts), len(smalls)

    def body(*refs):
        t_refs, s_refs = refs[:n], refs[n:n + ns]
        o_refs, so_refs = refs[n + ns:2 * n + ns], refs[2 * n + ns:2 * (n + ns)]
        send_sems, recv_sems = refs[2 * (n + ns):]
        x, y, c = lax.axis_index("x"), lax.axis_index("y"), lax.axis_index("c")
        mine = 2 * x + y
        sends, arrivals, small_sends, small_arrivals = [], [], [], []
        for r, (fx, fy) in enumerate(_CHIP_RELS):
            px, py = _flip(x, fx), _flip(y, fy)
            peer = 2 * px + py
            for k in range(n + ns):
                s = r * (n + ns) + k
                if k < n:
                    src, dst = _piece(t_refs[k], kinds[k], peer), o_refs[k]
                else:
                    src, dst = s_refs[k - n], so_refs[k - n]
                go = pltpu.make_async_remote_copy(
                    src_ref=src, dst_ref=dst.at[mine], send_sem=send_sems.at[s], recv_sem=recv_sems.at[s],
                    device_id=(px, py, c), device_id_type=_MESH)
                come = pltpu.make_async_remote_copy(
                    src_ref=src, dst_ref=dst.at[peer], send_sem=send_sems.at[s], recv_sem=recv_sems.at[s],
                    device_id=(px, py, c), device_id_type=_MESH)
                (sends if k < n else small_sends).append(go)
                (arrivals if k < n else small_arrivals).append(come)

        def owned(copies, side):
            return [cp for j, cp in enumerate(copies) if owners[j % ns] == side]

        for cp in sends:
            cp.start()
        for side in (0, 1):
            @pl.when(c == side)
            def _():
                for cp in owned(small_sends, side):
                    cp.start()
        for cp in arrivals:
            cp.wait_recv()
        for cp in sends:
            cp.wait_send()
        for side in (0, 1):
            @pl.when(c == side)
            def _():
                for cp in owned(small_arrivals, side):
                    cp.wait_recv()
                for cp in owned(small_sends, side):
                    cp.wait_send()

    m = len(_CHIP_RELS) * (n + ns)
    outs = pl.pallas_call(
        body, name=name, in_specs=[_ANY] * (n + ns), out_specs=[_ANY] * (n + ns),
        out_shape=[_sds((N_CHIPS,) + _piece_shape(t.shape, k), t.dtype) for t, k in zip(ts, kinds)]
        + [_sds((N_CHIPS,) + s.shape, s.dtype) for s in smalls],
        scratch_shapes=[pltpu.SemaphoreType.DMA((m,)), pltpu.SemaphoreType.DMA((m,))],
    )(*ts, *smalls)
    return outs[:n], outs[n:]


def share_halves(rs, smalls, owners, name):
    n, ns = len(rs), len(smalls)

    def body(*refs):
        o_refs, so_refs = refs[n + ns:2 * n + ns], refs[2 * n + ns:2 * (n + ns)]
        send_sems, recv_sems = refs[2 * (n + ns):]
        x, y, c = lax.axis_index("x"), lax.axis_index("y"), lax.axis_index("c")
        sibling = (x, y, 1 - c)
        swaps = [pltpu.make_async_remote_copy(
            src_ref=o_refs[k].at[c], dst_ref=o_refs[k].at[c], send_sem=send_sems.at[k], recv_sem=recv_sems.at[k],
            device_id=sibling, device_id_type=_MESH) for k in range(n)]
        arrivals = [pltpu.make_async_remote_copy(
            src_ref=o_refs[k].at[c], dst_ref=o_refs[k].at[1 - c], send_sem=send_sems.at[k], recv_sem=recv_sems.at[k],
            device_id=sibling, device_id_type=_MESH) for k in range(n)]
        gives = [pltpu.make_async_remote_copy(
            src_ref=so_refs[k], dst_ref=so_refs[k], send_sem=send_sems.at[n + k], recv_sem=recv_sems.at[n + k],
            device_id=sibling, device_id_type=_MESH) for k in range(ns)]
        for cp in swaps:
            cp.start()
        for side in (0, 1):
            @pl.when(c == side)
            def _():
                for k in _owned_by(owners, side):
                    gives[k].start()
        for cp in arrivals:
            cp.wait_recv()
        for cp in swaps:
            cp.wait_send()
        for side in (0, 1):
            @pl.when(c == side)
            def _():
                for k in _owned_by(owners, side):
                    gives[k].wait_send()

            @pl.when(c == 1 - side)
            def _():
                for k in _owned_by(owners, side):
                    gives[k].wait_recv()

    outs = pl.pallas_call(
        body, name=name, in_specs=[_ANY] * (n + ns), out_specs=[_ANY] * (n + ns),
        out_shape=[_sds(r.shape, r.dtype) for r in list(rs) + list(smalls)],
        input_output_aliases={k: k for k in range(n + ns)},
        scratch_shapes=[pltpu.SemaphoreType.DMA((n + ns,)), pltpu.SemaphoreType.DMA((n + ns,))],
    )(*rs, *smalls)
    return outs[:n], outs[n:]


def _small_view(a):
    if a.size < 8 * LANE:
        return jnp.pad(a.reshape(-1), (0, 8 * LANE - a.size)).reshape(8, LANE)
    if a.ndim == 1:
        return a.reshape(1, a.shape[0])
    if a.ndim == 4 and a.shape[-1] < LANE:
        return a.reshape(a.shape[0], a.shape[1], a.shape[2] * a.shape[3])
    return a


def _permuted_from_shards(shards):
    parts = []
    for lo, hi in GROUP_COLS:
        for q in range(N_CHIPS):
            a, b = max(lo, q * SHARD_COLS), min(hi, (q + 1) * SHARD_COLS)
            if a < b:
                parts.append(shards[q][..., a - q * SHARD_COLS:b - q * SHARD_COLS])
    pad = jnp.zeros(shards[0].shape[:-1] + (D_IN_PAD - D_IN,), shards[0].dtype)
    return jnp.concatenate(parts + [pad], axis=-1)


def _shards_from_groups(groups):
    in_order = sorted(range(len(GROUP_COLS)), key=lambda j: GROUP_COLS[j][0])
    shards = []
    for q in range(N_CHIPS):
        parts = []
        for j in in_order:
            lo, hi = GROUP_COLS[j]
            a, b = max(lo, q * SHARD_COLS), min(hi, (q + 1) * SHARD_COLS)
            if a < b:
                parts.append(groups[j][..., a - lo:b - lo])
        shards.append(jnp.concatenate(parts, axis=-1))
    return shards


def _expand_b(b):
    eye = jnp.eye(SSM_GROUPS, dtype=b.dtype)
    return jnp.einsum("gnc,gh->gchn", b, eye).reshape(D_SSM, N_STATE)


def _extract_b(e):
    return jnp.einsum("gcgn->gnc", e.reshape(SSM_GROUPS, SSM_GROUP, SSM_GROUPS, SSM_STATE))


def _expand_c(c):
    eye = jnp.eye(SSM_GROUPS, dtype=c.dtype)
    return jnp.einsum("gcn,gh->gnhc", c, eye).reshape(N_STATE, D_SSM)


def _extract_c(e):
    return jnp.einsum("gngc->gcn", e.reshape(SSM_GROUPS, SSM_STATE, SSM_GROUPS, SSM_GROUP))


def _lane_row(v):
    return jnp.pad(v, (0, LANE - v.shape[0])).reshape(1, LANE)


def _layer_params(w, l):
    return dict(
        norm_g=w["norm_g"][l][None], win=w["w_in_perm"][l], wout=w["w_out"][l].astype(_MXU),
        pg=w["ple_norm_g"][l][None], wgate=w["w_ple_gate"][l].astype(_MXU), wple=w["w_ple"][l].astype(_MXU),
        are=w["ssm_a_re"][l].reshape(1, N_STATE), aim=w["ssm_a_im"][l].reshape(1, N_STATE),
        ls=jnp.repeat(w["ssm_log_step"][l], SSM_STATE).reshape(1, N_STATE),
        bre=_expand_b(w["ssm_b_re"][l]), bim=_expand_b(w["ssm_b_im"][l]),
        cr=_expand_c(w["ssm_c_re"][l]), ci=_expand_c(w["ssm_c_im"][l]),
        dr=w["ssm_d"][l].reshape(1, D_SSM), wglu=w["ssm_w_glu"][l].astype(f32), bglu=w["ssm_b_glu"][l][None],
        convw=w["dn_conv_w"][l], alog=_lane_row(w["dn_a_log"][l]), dtb=_lane_row(w["dn_dt_bias"][l]),
        ng=w["dn_norm_g"][l][None],
        lng=w["sg_ln_g"][l][None], lnb=w["sg_ln_b"][l][None], sgw=w["sg_w"][l],
        bt=jnp.pad(w["sg_b"][l].T, ((0, 0), (0, LANE - SG_HEADS))),
    )


def _layer_fwd(x, p, lp, nb, tag):
    seq = x.shape[0] // nb
    h, zs, zq, zg, zsg, zab = in_fwd(x, lp["norm_g"], lp["win"], f"in_fwd{tag}")
    prep = s5_prep_fwd(lp["are"], lp["aim"], lp["ls"], lp["bre"], lp["bim"], f"s5_prep_fwd{tag}")
    s5p = tuple(prep) + (lp["cr"], lp["ci"], lp["dr"], lp["wglu"], lp["bglu"])
    ys, hs = s5_fwd(zs, s5p, nb, f"s5_fwd{tag}")
    qkv, loc, inv = dn_front_fwd(zq, lp["convw"], zab, lp["alog"], lp["dtb"], seq, f"dn_front_fwd{tag}")
    yd, ss = dn_scan_fwd(loc, zg, lp["ng"], nb, f"dn_scan_fwd{tag}")
    yg = sg_fwd(zsg, lp["lng"], lp["lnb"], lp["sgw"], lp["bt"], f"sg_fwd{tag}")
    x2, x1, y, hn = post_fwd(x, ys, yd, yg, p, lp["wout"], lp["pg"], lp["wgate"], lp["wple"], f"post_fwd{tag}")
    saved = dict(x=x, h=h, zs=zs, zq=zq, zg=zg, zsg=zsg, zab=zab, s5p=s5p, hs=hs, qkv=qkv, loc=loc, inv=inv, ss=ss, x1=x1, y=y, hn=hn, p=p)
    return x2, saved


def _layer_bwd(dx2, sv, lp, nb, tag):
    seq = dx2.shape[0] // nb
    dx1, dgp, dpp, dys, dyd, dyg, dpg = post_bwd(dx2, sv["x1"], sv["hn"], sv["p"], lp["wout"], lp["pg"], lp["wgate"],
                                                 lp["wple"], f"post_bwd{tag}")
    g = {}
    g["w_out"] = wgrad(sv["y"], dx1, f"wgrad_out{tag}")
    g["w_ple_gate"] = wgrad(sv["hn"], dgp, f"wgrad_gate{tag}")
    g["w_ple"] = wgrad(sv["p"], dpp, f"wgrad_ple{tag}")
    g["ple_norm_g"] = dpg[0]
    dzsg, dlng, dlnb, dsgw, dbt = sg_bwd(sv["zsg"], lp["lng"], lp["lnb"], lp["sgw"], lp["bt"], dyg, f"sg_bwd{tag}")
    g["sg_ln_g"], g["sg_ln_b"], g["sg_w"], g["sg_b"] = dlng[0], dlnb[0], dsgw, dbt[:, :SG_HEADS].T
    dloc, dzg, dng = dn_scan_bwd(sv["loc"], sv["zg"], lp["ng"], sv["ss"], dyd, nb, f"dn_scan_bwd{tag}")
    dqkv, dzab, dalog, ddtb = dn_local_bwd(sv["qkv"], sv["zab"], lp["alog"], lp["dtb"], sv["inv"], dloc,
                                           f"dn_local_bwd{tag}")
    dzq, dconv = dn_pre_bwd(sv["zq"], lp["convw"], dqkv, seq, f"dn_pre_bwd{tag}")
    g["dn_conv_w"], g["dn_a_log"], g["dn_dt_bias"], g["dn_norm_g"] = dconv, dalog[0, :DN_HEADS], ddtb[0, :DN_HEADS], dng[0]
    s5out = s5_bwd(sv["zs"], sv["s5p"], sv["hs"], dys, nb, f"s5_bwd{tag}")
    dzs, dprep, (dcr, dci, ddr, dwglu, dbglu) = s5out[0], s5out[1:1 + S5_PREPARED], s5out[1 + S5_PREPARED:]
    dare, daim, dls, dbre, dbim = s5_prep_bwd(lp["are"], lp["aim"], lp["ls"], lp["bre"], lp["bim"], dprep,
                                              f"s5_prep_bwd{tag}")
    g["ssm_a_re"] = dare.reshape(SSM_GROUPS, SSM_STATE)
    g["ssm_a_im"] = daim.reshape(SSM_GROUPS, SSM_STATE)
    g["ssm_log_step"] = dls.reshape(SSM_GROUPS, SSM_STATE).sum(axis=1)
    g["ssm_b_re"], g["ssm_b_im"] = _extract_b(dbre), _extract_b(dbim)
    g["ssm_c_re"], g["ssm_c_im"] = _extract_c(dcr), _extract_c(dci)
    g["ssm_d"] = ddr.reshape(SSM_GROUPS, SSM_GROUP)
    g["ssm_w_glu"], g["ssm_b_glu"] = dwglu, dbglu[0]
    dzs_all = (dzs, dzq, dzg, dzsg, dzab)
    dx, dng_in = in_bwd(sv["x"], lp["norm_g"], lp["win"], dzs_all, dx1, f"in_bwd{tag}")
    g["w_in_pieces"] = [wgrad(sv["h"], dz, f"wgrad_in{k}{tag}") for k, dz in enumerate(dzs_all)]
    g["norm_g"] = dng_in[0]
    return dx, g


def _local_step(x, p, target, w, nb):
    lps = [_layer_params(w, l) for l in range(DEPTH)]
    saved = []
    for l in range(DEPTH):
        x, sv = _layer_fwd(x, p[l], lps[l], nb, f"_l{l}")
        saved.append(sv)
    loss_blk, dx, dfg = loss_fwd_bwd(x, w["final_norm_g"][None], target, "loss")
    grads = [None] * DEPTH
    for l in reversed(range(DEPTH)):
        dx, grads[l] = _layer_bwd(dx, saved[l], lps[l], nb, f"_l{l}")
    out = {k: jnp.stack([grads[l][k] for l in range(DEPTH)]) for k in grads[0] if k != "w_in_pieces"}
    out["w_in_pieces"] = [grads[l]["w_in_pieces"] for l in range(DEPTH)]
    out["final_norm_g"] = dfg[0]
    return loss_blk[0, 0], dx, out


def kernel(x, p, norm_g, w_in, ssm_a_re, ssm_a_im, ssm_b_re, ssm_b_im, ssm_c_re, ssm_c_im, ssm_d, ssm_log_step, ssm_w_glu, ssm_b_glu, dn_conv_w, dn_a_log, dn_dt_bias, dn_norm_g, sg_ln_g, sg_ln_b, sg_w, sg_b, w_out, ple_norm_g, w_ple_gate, w_ple, final_norm_g, loss_target, m_norm_g, m_w_in, m_ssm_a_re, m_ssm_a_im, m_ssm_b_re, m_ssm_b_im, m_ssm_c_re, m_ssm_c_im, m_ssm_d, m_ssm_log_step, m_ssm_w_glu, m_ssm_b_glu, m_dn_conv_w, m_dn_a_log, m_dn_dt_bias, m_dn_norm_g, m_sg_ln_g, m_sg_ln_b, m_sg_w, m_sg_b, m_w_out, m_ple_norm_g, m_w_ple_gate, m_w_ple, m_final_norm_g, v_norm_g, v_w_in, v_ssm_a_re, v_ssm_a_im, v_ssm_b_re, v_ssm_b_im, v_ssm_c_re, v_ssm_c_im, v_ssm_d, v_ssm_log_step, v_ssm_w_glu, v_ssm_b_glu, v_dn_conv_w, v_dn_a_log, v_dn_dt_bias, v_dn_norm_g, v_sg_ln_g, v_sg_ln_b, v_sg_w, v_sg_b, v_w_out, v_ple_norm_g, v_w_ple_gate, v_w_ple, v_final_norm_g):
    args = locals()
    w = {n: args[n] for n in WEIGHTS}
    m = {n: args["m_" + n] for n in WEIGHTS}
    v = {n: args["v_" + n] for n in WEIGHTS}
    nb, seq = x.shape[0], x.shape[1]
    t = nb * seq

    full = _gather_full(w)
    loss_local, dx, grads = _local_step(x.reshape(t, D_MODEL), p.reshape(DEPTH, t, D_PLE),
                                        loss_target.reshape(t, D_MODEL), full, nb)
    outs, loss = _reduce_and_update(grads, w, m, v, loss_local)
    return (loss, dx.reshape(nb, seq, D_MODEL), *[outs[0][n] for n in WEIGHTS], *[outs[1][n] for n in WEIGHTS],
            *[outs[2][n] for n in WEIGHTS], *[outs[3][n] for n in WEIGHTS])


def _gather_full(w):
    sh_names = [n for n, _ in SHARDED]
    shards = [w[n] if n == "dn_conv_w" else w[n].astype(_COMM) for n in sh_names]
    gathered = gather_weights(shards, [k for _, k in SHARDED], "gather_weights")
    chip = 2 * lax.axis_index("x") + lax.axis_index("y")
    full = {n: w[n] for n in REPLICATED}
    for (n, kind), shard, got in zip(SHARDED, shards, gathered):
        if kind[0] == "slot":
            full[n] = lax.dynamic_update_index_in_dim(got, shard, chip, 0)
        else:
            full[n] = lax.dynamic_update_slice_in_dim(got, shard, chip * kind[2], axis=kind[1] + 1)
    slots = full.pop("w_in")
    full["w_in_perm"] = _permuted_from_shards([slots[q] for q in range(N_CHIPS)]).astype(_MXU)
    return full


def _reduce_and_update(grads, w, m, v, loss_local):
    sh_names = [n for n, _ in SHARDED]
    sh_kinds = [k for _, k in SHARDED]
    owners = [SMALL_OWNER[n] for n in REPLICATED + ("loss",)]

    def small_views(d):
        return [_small_view(d[n]) for n in REPLICATED]

    grads["w_in"] = jnp.stack([jnp.stack(_shards_from_groups(pieces)) for pieces in grads["w_in_pieces"]])
    gs = [grads[n] if n == "dn_conv_w" else grads[n].astype(_COMM) for n in sh_names]
    sm = small_views(grads) + [_small_view(loss_local.reshape(1))]
    core = lax.axis_index("c")
    chip = 2 * lax.axis_index("x") + lax.axis_index("y")
    got, sm_got = exchange_halves(gs, sm, owners, "exchange_halves")
    sums = add_pairs([lax.dynamic_index_in_dim(g, core, 0, keepdims=False) for g in gs] + sm, list(got) + list(sm_got),
                     "add_halves")
    sums, sm_sums = sums[:len(gs)], sums[len(gs):]
    parts, sm_parts = reduce_to_chips(sums, sh_kinds, sm_sums, owners, "reduce_to_chips")
    parts = list(parts)
    for k, (kind, total) in enumerate(zip(sh_kinds, sums)):
        if kind[0] == "slot":
            own = lax.dynamic_index_in_dim(total, chip, 0, keepdims=False)
        else:
            own = lax.dynamic_slice_in_dim(total, chip * kind[2], kind[2], axis=kind[1])
        parts[k] = lax.dynamic_update_index_in_dim(parts[k], own, chip, 0)
    half = core.astype(jnp.int32).reshape(1)
    totals = [sum_parts(half, part, f"sum_{n}") for n, part in zip(sh_names, parts)]
    sm_totals = sum_parts_small(chip.astype(jnp.int32).reshape(1), sm_parts, sm_sums, "sum_replicated")
    g_big, g_small = share_halves(totals, sm_totals, owners, "share_halves")
    outs = [dict(zip(sh_names, g_big)), {}, {}, {}]
    for n, g in zip(sh_names, g_big):
        outs[1][n], outs[2][n], outs[3][n] = adamw(g, w[n], m[n], v[n], f"adamw_{n}")
    small_results = [g_small[:-1]] + adamw_small(g_small[:-1], small_views(w), small_views(m), small_views(v),
                                                 "adamw_replicated")
    for j in range(4):
        for n, r in zip(REPLICATED, small_results[j]):
            outs[j][n] = r.reshape(-1)[:w[n].size].reshape(w[n].shape)
    return outs, g_small[-1][0, 0]
```

```python
import functools

import jax
import jax.numpy as jnp
from jax import lax
from jax.experimental import pallas as pl
from jax.experimental.pallas import tpu as pltpu

f32 = jnp.float32
bf16 = jnp.bfloat16

_MXU = bf16
_COMM = bf16
HIGH = lax.Precision.HIGH

D_MODEL = 1024
DEPTH = 2
D_PLE = 256
D_SSM = 256
D_DN = 512
D_SG = 256
SSM_GROUPS = 16
SSM_GROUP = 16
SSM_STATE = 64
N_STATE = SSM_GROUPS * SSM_STATE
DN_HEADS = 4
DN_HEAD_DIM = 128
DN_CONV = 4
DN_HALO = 16
DN_CHUNK = 64
SG_HEADS = 4
SG_HEAD_DIM = 64
SG_CHUNK = 128
S5_CHUNK = 1024
S5_GROUP_ROWS = 8
EPS = 1e-6
D_IN = 3336
D_IN_PAD = 3456
LANE = 128

ADAM_LR = 0.001
ADAM_B1 = 0.9
ADAM_B2 = 0.999
ADAM_EPS = 1e-08
ADAM_WD = 0.01
ADAM_STEP = 10

N_CHIPS = 4

Z_COLS = ((0, 512), (512, 2048), (2048, 2560), (2560, 3328), (3328, 3456))

GROUP_COLS = ((0, 512), (512, 2048), (2056, 2568), (2568, 3336), (2048, 2056))
SHARD_COLS = D_IN // 4

SHARDED = (("w_in", ("slot",)), ("ssm_w_glu", ("win", 0, 64)), ("dn_conv_w", ("win", 1, 384)),
           ("w_out", ("win", 0, 256)), ("w_ple_gate", ("win", 0, 256)), ("w_ple", ("win", 1, 256)))
REPLICATED = ("norm_g", "ssm_a_re", "ssm_a_im", "ssm_b_re", "ssm_b_im", "ssm_c_re", "ssm_c_im", "ssm_d",
              "ssm_log_step", "ssm_b_glu", "dn_a_log", "dn_dt_bias", "dn_norm_g", "sg_ln_g", "sg_ln_b", "sg_w",
              "sg_b", "ple_norm_g", "final_norm_g")
SMALL_OWNER = {n: int(n.startswith("ssm_")) for n in REPLICATED + ("loss",)}
WEIGHTS = ("norm_g", "w_in", "ssm_a_re", "ssm_a_im", "ssm_b_re", "ssm_b_im", "ssm_c_re", "ssm_c_im", "ssm_d",
           "ssm_log_step", "ssm_w_glu", "ssm_b_glu", "dn_conv_w", "dn_a_log", "dn_dt_bias", "dn_norm_g", "sg_ln_g",
           "sg_ln_b", "sg_w", "sg_b", "w_out", "ple_norm_g", "w_ple_gate", "w_ple", "final_norm_g")

VMEM_BIG = 56 * 1024 * 1024


def _mm(a, b):
    return jnp.dot(a.astype(_MXU), b.astype(_MXU), preferred_element_type=f32)


def _mm_nt(a, b):
    return lax.dot_general(a.astype(_MXU), b.astype(_MXU), (((1,), (1,)), ((), ())), preferred_element_type=f32)


def _mm_tn(a, b):
    return lax.dot_general(a.astype(_MXU), b.astype(_MXU), (((0,), (0,)), ((), ())), preferred_element_type=f32)


@jax.custom_vjp
def bdot(a, b):
    return _mm(a, b)


def _bdot_fwd(a, b):
    return _mm(a, b), (a, b)


def _bdot_bwd(res, g):
    a, b = res
    return _mm_nt(g, b).astype(a.dtype), _mm_tn(a, g).astype(b.dtype)


bdot.defvjp(_bdot_fwd, _bdot_bwd)


@jax.custom_vjp
def bdot_nt(a, b):
    return _mm_nt(a, b)


def _bdot_nt_fwd(a, b):
    return _mm_nt(a, b), (a, b)


def _bdot_nt_bwd(res, g):
    a, b = res
    return _mm(g, b).astype(a.dtype), _mm_tn(g, a).astype(b.dtype)


bdot_nt.defvjp(_bdot_nt_fwd, _bdot_nt_bwd)


@jax.custom_vjp
def bdot_tn(a, b):
    return _mm_tn(a, b)


def _bdot_tn_fwd(a, b):
    return _mm_tn(a, b), (a, b)


def _bdot_tn_bwd(res, g):
    a, b = res
    return _mm_nt(b, g).astype(a.dtype), _mm(a, g).astype(b.dtype)


bdot_tn.defvjp(_bdot_tn_fwd, _bdot_tn_bwd)


def hdot(a, b):
    return jnp.dot(a, b, precision=HIGH, preferred_element_type=f32)


def _unit_lower_inverses(ms):
    n = ms[0].shape[0]
    eye = (lax.broadcasted_iota(jnp.int32, (n, n), 0) == lax.broadcasted_iota(jnp.int32, (n, n), 1)).astype(f32)
    pw = [-m for m in ms]
    inv = [eye + p for p in pw]
    for _ in range(n.bit_length() - 2):
        pw = [hdot(p, p) for p in pw]
        inv = [a + hdot(a, p) for a, p in zip(inv, pw)]
    return inv


@jax.custom_vjp
def solve_unit_lower(ms, rhs, inv):
    return [hdot(a, r) for a, r in zip(inv, rhs)]


def _solve_unit_lower_fwd(ms, rhs, inv):
    xs = [hdot(a, r) for a, r in zip(inv, rhs)]
    return xs, (inv, xs)


def _solve_unit_lower_bwd(res, gs):
    inv, xs = res
    d_rhs = [lax.dot_general(a, g, (((0,), (0,)), ((), ())), precision=HIGH, preferred_element_type=f32)
             for a, g in zip(inv, gs)]
    d_ms = [-lax.dot_general(d, x, (((1,), (1,)), ((), ())), precision=HIGH, preferred_element_type=f32)
            for d, x in zip(d_rhs, xs)]
    return d_ms, d_rhs, [jnp.zeros_like(a) for a in inv]


solve_unit_lower.defvjp(_solve_unit_lower_fwd, _solve_unit_lower_bwd)


@functools.partial(jax.custom_vjp, nondiff_argnums=(1,))
def roll_rows(x, k):
    return pltpu.roll(x, k, 0)


def _roll_rows_fwd(x, k):
    return pltpu.roll(x, k, 0), None


def _roll_rows_bwd(k, _, g):
    return (pltpu.roll(g, g.shape[0] - k, 0),)


roll_rows.defvjp(_roll_rows_fwd, _roll_rows_bwd)


def _row_ids(shape):
    return lax.broadcasted_iota(jnp.int32, shape, 0)


def _rms(x, g):
    return x * lax.rsqrt(jnp.mean(x * x, axis=-1, keepdims=True) + EPS) * g


def _layer_norm(x, g, b):
    mu = jnp.mean(x, axis=-1, keepdims=True)
    xc = x - mu
    return xc * lax.rsqrt(jnp.mean(xc * xc, axis=-1, keepdims=True) + EPS) * g + b


def _s5_prep(are, aim, ls, bre, bim):
    step = jnp.exp(ls)
    mag = jnp.exp(are * step)
    lr = mag * jnp.cos(aim * step)
    li = mag * jnp.sin(aim * step)
    den = are * are + aim * aim
    nr, ni = lr - 1.0, li
    fr = (nr * are + ni * aim) / den
    fi = (ni * are - nr * aim) / den
    bbr = fr * bre - fi * bim
    bbi = fr * bim + fi * bre
    pr = jnp.broadcast_to(lr, (S5_GROUP_ROWS, N_STATE))
    pi = jnp.broadcast_to(li, (S5_GROUP_ROWS, N_STATE))
    d = 1
    while d < S5_GROUP_ROWS:
        keep = _row_ids(pr.shape) >= d
        sr, si = roll_rows(pr, d), roll_rows(pi, d)
        pr, pi = jnp.where(keep, pr * sr - pi * si, pr), jnp.where(keep, pr * si + pi * sr, pi)
        d *= 2
    return pr, pi, bbr, bbi


def _s5_chunk(u, gate, hr, hi, pr, pi, bbr, bbi, cr, ci, dr, wglu, bglu):
    n, steps = u.shape[0], S5_GROUP_ROWS
    groups = n // steps
    xr = bdot(u, bbr)
    xi = bdot(u, bbi)
    lr, li = pr[0:1], pi[0:1]
    rs, ims = [xr[:groups]], [xi[:groups]]
    for t in range(1, steps):
        a, b = rs[-1], ims[-1]
        rs.append(xr[t * groups:(t + 1) * groups] + lr * a - li * b)
        ims.append(xi[t * groups:(t + 1) * groups] + lr * b + li * a)
    er, ei = rs[-1], ims[-1]
    mr, mi = pr[steps - 1:steps], pi[steps - 1:steps]
    gid = _row_ids(er.shape)
    er, ei = (er + jnp.where(gid == 0, mr * hr - mi * hi, 0.0), ei + jnp.where(gid == 0, mr * hi + mi * hr, 0.0))
    d = 1
    while d < groups:
        sr = jnp.where(gid >= d, roll_rows(er, d), 0.0)
        si = jnp.where(gid >= d, roll_rows(ei, d), 0.0)
        er, ei = er + mr * sr - mi * si, ei + mr * si + mi * sr
        mr, mi = mr * mr - mi * mi, 2.0 * mr * mi
        d *= 2
    before_r = jnp.where(gid == 0, hr, roll_rows(er, 1))
    before_i = jnp.where(gid == 0, hi, roll_rows(ei, 1))
    xr = jnp.concatenate([rs[t] + pr[t:t + 1] * before_r - pi[t:t + 1] * before_i for t in range(steps)], axis=0)
    xi = jnp.concatenate([ims[t] + pr[t:t + 1] * before_i + pi[t:t + 1] * before_r for t in range(steps)], axis=0)
    y = bdot(xr, cr) - bdot(xi, ci) + dr * u
    y = jax.nn.gelu(y)
    y = y * jax.nn.sigmoid(bdot(y, wglu) + bglu)
    return y * jax.nn.silu(gate), er[groups - 1:groups], ei[groups - 1:groups]


def _dn_pre(xc, xp, w0, w1, w2, w3, is_start, col):
    xp = jnp.where(is_start, 0.0, xp)
    halo_rows = _row_ids(xp.shape)
    acc = w3 * xc
    for d, w in ((1, w2), (2, w1), (3, w0)):
        r = roll_rows(xc, d)
        head = jnp.where(halo_rows >= d, r[:DN_HALO], roll_rows(xp, d))
        acc = acc + w * jnp.concatenate([head, r[DN_HALO:]], axis=0)
    y = jax.nn.silu(acc)
    if col >= 2 * DN_HEADS:
        return y
    nrm = y * lax.rsqrt(jnp.sum(y * y, axis=-1, keepdims=True) + EPS)
    return nrm * DN_HEAD_DIM ** -0.5 if col < DN_HEADS else nrm


def _dn_local(qs, ks, vs, abs_, alog, dtb, invs=None):
    c = DN_CHUNK
    ri = lax.broadcasted_iota(jnp.int32, (c, c), 0)
    ci = lax.broadcasted_iota(jnp.int32, (c, c), 1)
    causal, strict = ri >= ci, ri > ci
    tril = causal.astype(f32)
    gcums = [hdot(tril, -jnp.exp(alog) * jax.nn.softplus(ab + dtb)) for ab in abs_]
    gcum_ts = [g.T for g in gcums]
    sigs = [jax.nn.sigmoid(ab) for ab in abs_]
    chains = [(j, h) for j in range(len(abs_)) for h in range(DN_HEADS)]
    gc = [gcums[j][:, h:h + 1] for j, h in chains]
    decay = [jnp.where(causal, jnp.exp(jnp.where(causal, gc[n] - gcum_ts[j][h:h + 1, :], 0.0)), 0.0)
             for n, (j, h) in enumerate(chains)]
    beta = [sigs[j][:, DN_HEADS + h:DN_HEADS + h + 1] for j, h in chains]
    kb = [ks[j][h] * beta[n] for n, (j, h) in enumerate(chains)]
    ms = [jnp.where(strict, bdot_nt(kb[n], ks[j][h]) * decay[n], 0.0) for n, (j, h) in enumerate(chains)]
    egc = [jnp.exp(g) for g in gc]
    rhs = [jnp.concatenate([vs[j][h] * beta[n], kb[n] * egc[n]], axis=1) for n, (j, h) in enumerate(chains)]
    inv = _unit_lower_inverses(ms) if invs is None else [invs[j][h] for j, h in chains]
    sol = solve_unit_lower(ms, rhs, inv)
    values = [s[:, :DN_HEAD_DIM] for s in sol]
    k_cds = [s[:, DN_HEAD_DIM:] for s in sol]
    attns = [bdot_nt(qs[j][h], ks[j][h]) * decay[n] for n, (j, h) in enumerate(chains)]
    q_decs = [qs[j][h] * egc[n] for n, (j, h) in enumerate(chains)]
    k_decs = [ks[j][h] * jnp.exp(gc[n][c - 1:c, :] - gc[n]) for n, (j, h) in enumerate(chains)]

    def nest(flat):
        return [flat[j * DN_HEADS:(j + 1) * DN_HEADS] for j in range(len(abs_))]

    lasts = [jnp.exp(g[c - 1:c, :]) for g in gcums]
    return nest(values), nest(k_cds), nest(attns), nest(q_decs), nest(k_decs), lasts, nest(inv)


def _dn_step(values, k_cds, attns, q_decs, k_decs, lasts, ggs, sts, ng):
    v_new = [v - bdot(kc, st) for v, kc, st in zip(values, k_cds, sts)]
    o = [bdot(qd, st) for qd, st in zip(q_decs, sts)]
    o = [a + bdot(at, vn) for a, at, vn in zip(o, attns, v_new)]
    new = [st * la + bdot_tn(kd, vn) for st, la, kd, vn in zip(sts, lasts, k_decs, v_new)]
    return [_rms(a, ng) * jax.nn.silu(g) for a, g in zip(o, ggs)], new


def _sg_chunk(u, v, gate, lng, lnb, ws, bt):
    n = SG_CHUNK
    ug = jax.nn.gelu(u)
    vn = _layer_norm(jax.nn.gelu(v), lng, lnb)
    causal = lax.broadcasted_iota(jnp.int32, (n, n), 0) >= lax.broadcasted_iota(jnp.int32, (n, n), 1)
    lane = lax.broadcasted_iota(jnp.int32, (n, D_SG), 1)
    s = jnp.zeros((n, D_SG), f32)
    for h in range(SG_HEADS):
        t = bdot(jnp.where(causal, ws[h], 0.0), vn) + bt[:, h:h + 1]
        s = s + jnp.where((lane >= h * SG_HEAD_DIM) & (lane < (h + 1) * SG_HEAD_DIM), t, 0.0)
    return ug * s * jax.nn.silu(gate)


def _cp(n_grid, vmem=None):
    return pltpu.CompilerParams(dimension_semantics=("arbitrary",) * n_grid, vmem_limit_bytes=vmem)


def _full(shape):
    nd = len(shape)
    return pl.BlockSpec(tuple(shape), lambda *_: (0,) * nd)


def _rows(tm, ncol):
    return pl.BlockSpec((tm, ncol), lambda i: (i, 0))


def _sds(shape, dtype=f32):
    return jax.ShapeDtypeStruct(tuple(shape), dtype)


def _acc(ref, val, first):
    @pl.when(first)
    def _():
        ref[...] = val

    @pl.when(jnp.logical_not(first))
    def _():
        ref[...] += val


def in_fwd(x, g, w, name):
    t, tm = x.shape[0], 512

    def body(x_ref, g_ref, w_ref, h_ref, *z_refs):
        h = _rms(x_ref[...], g_ref[...]).astype(_MXU)
        h_ref[...] = h
        for z_ref, (a, b) in zip(z_refs, Z_COLS):
            z_ref[...] = jnp.dot(h, w_ref[:, a:b], preferred_element_type=f32)

    widths = [b - a for a, b in Z_COLS]
    return pl.pallas_call(
        body, name=name, grid=(t // tm,),
        in_specs=[_rows(tm, D_MODEL), _full((1, D_MODEL)), _full((D_MODEL, D_IN_PAD))],
        out_specs=[_rows(tm, D_MODEL)] + [_rows(tm, n) for n in widths],
        out_shape=[_sds((t, D_MODEL), _MXU)] + [_sds((t, n)) for n in widths],
        compiler_params=_cp(1, VMEM_BIG),
    )(x, g, w)


def in_bwd(x, g, w, dzs, dres, name):
    t, tm = x.shape[0], 512
    widths = [b - a for a, b in Z_COLS]

    def body(x_ref, g_ref, w_ref, dres_ref, *rest):
        dz_refs, (dx_ref, dg_ref) = rest[:5], rest[5:]
        dh = jnp.zeros((tm, D_MODEL), f32)
        for dz_ref, (a, b) in zip(dz_refs, Z_COLS):
            dh = dh + _mm_nt(dz_ref[...], w_ref[:, a:b])
        _, vj = jax.vjp(_rms, x_ref[...], g_ref[...])
        dx, dg = vj(dh)
        dx_ref[...] = dres_ref[...] + dx
        _acc(dg_ref, dg, pl.program_id(0) == 0)

    return pl.pallas_call(
        body, name=name, grid=(t // tm,),
        in_specs=[_rows(tm, D_MODEL), _full((1, D_MODEL)), _full((D_MODEL, D_IN_PAD)), _rows(tm, D_MODEL)]
        + [_rows(tm, n) for n in widths],
        out_specs=[_rows(tm, D_MODEL), _full((1, D_MODEL))],
        out_shape=[_sds((t, D_MODEL)), _sds((1, D_MODEL))],
        compiler_params=_cp(1, VMEM_BIG),
    )(x, g, w, dres, *dzs)


def wgrad(a, g, name):
    t, k = a.shape
    n = g.shape[1]
    tm = min(t, 4096 if n <= 768 else 2048)
    tn = n if n <= 1536 else n // 2
    steps = t // tm

    def body(a_ref, g_ref, o_ref, acc):
        i = pl.program_id(1)
        _acc(acc, _mm_tn(a_ref[...], g_ref[...]), i == 0)

        @pl.when(i == steps - 1)
        def _():
            o_ref[...] = acc[...].astype(o_ref.dtype)

    return pl.pallas_call(
        body, name=name, grid=(n // tn, steps),
        in_specs=[pl.BlockSpec((tm, k), lambda j, i: (i, 0)), pl.BlockSpec((tm, tn), lambda j, i: (i, j))],
        out_specs=pl.BlockSpec((k, tn), lambda j, i: (0, j)),
        out_shape=_sds((k, n), _COMM),
        scratch_shapes=[pltpu.VMEM((k, tn), f32)],
        compiler_params=_cp(2, VMEM_BIG),
    )(a, g)


def post_fwd(x, ys, yd, yg, p, wout, pg, wgate, wple, name):
    t, tm = x.shape[0], 512

    def body(x_ref, ys_ref, yd_ref, yg_ref, p_ref, wout_ref, pg_ref, wgate_ref, wple_ref,
             x2_ref, x1_ref, y_ref, hn_ref):
        y = jnp.concatenate([ys_ref[...], yd_ref[...], yg_ref[...]], axis=1).astype(_MXU)
        y_ref[...] = y
        x1 = x_ref[...] + jnp.dot(y, wout_ref[...], preferred_element_type=f32)
        x1_ref[...] = x1
        hn = _rms(x1, pg_ref[...]).astype(_MXU)
        hn_ref[...] = hn
        gp = jnp.dot(hn, wgate_ref[...], preferred_element_type=f32)
        pp = _mm(p_ref[...], wple_ref[...])
        x2_ref[...] = x1 + jax.nn.sigmoid(gp) * pp

    return pl.pallas_call(
        body, name=name, grid=(t // tm,),
        in_specs=[_rows(tm, D_MODEL), _rows(tm, D_SSM), _rows(tm, D_DN), _rows(tm, D_SG), _rows(tm, D_PLE),
                  _full((D_MODEL, D_MODEL)), _full((1, D_MODEL)), _full((D_MODEL, D_MODEL)), _full((D_PLE, D_MODEL))],
        out_specs=[_rows(tm, D_MODEL)] * 4,
        out_shape=[_sds((t, D_MODEL)), _sds((t, D_MODEL)), _sds((t, D_MODEL), _MXU), _sds((t, D_MODEL), _MXU)],
        compiler_params=_cp(1, VMEM_BIG),
    )(x, ys, yd, yg, p, wout, pg, wgate, wple)


def post_bwd(dx2, x1, hn, p, wout, pg, wgate, wple, name):
    t, tm = dx2.shape[0], 512

    def body(dx2_ref, x1_ref, hn_ref, p_ref, wout_ref, pg_ref, wgate_ref, wple_ref,
             dx1_ref, dgp_ref, dpp_ref, dys_ref, dyd_ref, dyg_ref, dpg_ref):
        dx2 = dx2_ref[...]
        gp = jnp.dot(hn_ref[...], wgate_ref[...], preferred_element_type=f32)
        pp = _mm(p_ref[...], wple_ref[...])
        sg = jax.nn.sigmoid(gp)
        dpp_ref[...] = (dx2 * sg).astype(_MXU)
        dgp = (dx2 * pp * sg * (1.0 - sg)).astype(_MXU)
        dgp_ref[...] = dgp
        dhn = _mm_nt(dgp, wgate_ref[...])
        _, vj = jax.vjp(_rms, x1_ref[...], pg_ref[...])
        dx1n, dpg = vj(dhn)
        dx1 = dx2 + dx1n
        dx1_ref[...] = dx1
        dy = _mm_nt(dx1, wout_ref[...])
        dys_ref[...] = dy[:, :D_SSM]
        dyd_ref[...] = dy[:, D_SSM:D_SSM + D_DN]
        dyg_ref[...] = dy[:, D_SSM + D_DN:]
        _acc(dpg_ref, dpg, pl.program_id(0) == 0)

    return pl.pallas_call(
        body, name=name, grid=(t // tm,),
        in_specs=[_rows(tm, D_MODEL), _rows(tm, D_MODEL), _rows(tm, D_MODEL), _rows(tm, D_PLE),
                  _full((D_MODEL, D_MODEL)), _full((1, D_MODEL)), _full((D_MODEL, D_MODEL)), _full((D_PLE, D_MODEL))],
        out_specs=[_rows(tm, D_MODEL), _rows(tm, D_MODEL), _rows(tm, D_MODEL), _rows(tm, D_SSM), _rows(tm, D_DN),
                   _rows(tm, D_SG), _full((1, D_MODEL))],
        out_shape=[_sds((t, D_MODEL)), _sds((t, D_MODEL), _MXU), _sds((t, D_MODEL), _MXU), _sds((t, D_SSM)),
                   _sds((t, D_DN)), _sds((t, D_SG)), _sds((1, D_MODEL))],
        compiler_params=_cp(1, VMEM_BIG),
    )(dx2, x1, hn, p, wout, pg, wgate, wple)


def loss_fwd_bwd(x, fg, target, name):
    t, tm = x.shape[0], 1024

    def body(x_ref, fg_ref, t_ref, loss_ref, dx_ref, dfg_ref):
        def f(xv, gv):
            err = _rms(xv, gv) - t_ref[...]
            return 0.5 * jnp.sum(jnp.mean(err * err, axis=-1))

        val, vj = jax.vjp(f, x_ref[...], fg_ref[...])
        dx, dfg = vj(jnp.ones((), f32))
        dx_ref[...] = dx
        first = pl.program_id(0) == 0
        _acc(dfg_ref, dfg, first)
        _acc(loss_ref, jnp.full((8, LANE), val, f32), first)

    return pl.pallas_call(
        body, name=name, grid=(t // tm,),
        in_specs=[_rows(tm, D_MODEL), _full((1, D_MODEL)), _rows(tm, D_MODEL)],
        out_specs=[_full((8, LANE)), _rows(tm, D_MODEL), _full((1, D_MODEL))],
        out_shape=[_sds((8, LANE)), _sds((t, D_MODEL)), _sds((1, D_MODEL))],
        compiler_params=_cp(1),
    )(x, fg, target)


S5_PREPARED = 4
_S5_PARAM_SHAPES = ((S5_GROUP_ROWS, N_STATE), (S5_GROUP_ROWS, N_STATE), (D_SSM, N_STATE), (D_SSM, N_STATE),
                    (N_STATE, D_SSM), (N_STATE, D_SSM), (1, D_SSM), (D_SSM, D_SSM), (1, D_SSM))

def s5_prep_fwd(are, aim, ls, bre, bim, name):
    def body(are_ref, aim_ref, ls_ref, bre_ref, bim_ref, *outs):
        vals = _s5_prep(are_ref[...], aim_ref[...], ls_ref[...], bre_ref[...], bim_ref[...])
        for o, v in zip(outs, vals):
            o[...] = v

    return pl.pallas_call(body, name=name, out_shape=[_sds(s) for s in _S5_PARAM_SHAPES[:S5_PREPARED]])(
        are, aim, ls, bre, bim)


def s5_prep_bwd(are, aim, ls, bre, bim, cts, name):
    def body(are_ref, aim_ref, ls_ref, bre_ref, bim_ref, *rest):
        ct_refs, outs = rest[:S5_PREPARED], rest[S5_PREPARED:]
        _, vj = jax.vjp(_s5_prep, are_ref[...], aim_ref[...], ls_ref[...], bre_ref[...], bim_ref[...])
        for o, v in zip(outs, vj(tuple(r[...] for r in ct_refs))):
            o[...] = v

    shapes = [(1, N_STATE)] * 3 + [(D_SSM, N_STATE)] * 2
    return pl.pallas_call(body, name=name, out_shape=[_sds(s) for s in shapes])(are, aim, ls, bre, bim, *cts)


def _step_major(ref, cols):
    x = ref[:, cols]
    n, w = x.shape
    return jnp.swapaxes(x.reshape(n // S5_GROUP_ROWS, S5_GROUP_ROWS, w), 0, 1).reshape(n, w)


def _store_step_major(ref, cols, val):
    n, w = val.shape
    ref[:, cols] = jnp.swapaxes(val.reshape(S5_GROUP_ROWS, n // S5_GROUP_ROWS, w), 0, 1).reshape(n, w).astype(ref.dtype)


def s5_fwd(z, params, nb, name):
    t = z.shape[0]
    nc = t // nb // S5_CHUNK
    npar = len(_S5_PARAM_SHAPES)

    def body(z_ref, *rest):
        p_refs, (y_ref, hs_ref, hr_s, hi_s) = rest[:npar], rest[npar:]

        @pl.when(pl.program_id(1) == 0)
        def _():
            hr_s[...] = jnp.zeros_like(hr_s)
            hi_s[...] = jnp.zeros_like(hi_s)

        hr, hi = hr_s[...], hi_s[...]
        hs_ref[0, :, :N_STATE] = hr
        hs_ref[0, :, N_STATE:] = hi
        y, nhr, nhi = _s5_chunk(_step_major(z_ref, slice(0, D_SSM)), _step_major(z_ref, slice(D_SSM, 2 * D_SSM)),
                                hr, hi, *[r[...] for r in p_refs])
        _store_step_major(y_ref, slice(0, D_SSM), y)
        hr_s[...] = nhr
        hi_s[...] = nhi

    return pl.pallas_call(
        body, name=name, grid=(nb, nc),
        in_specs=[pl.BlockSpec((S5_CHUNK, 2 * D_SSM), lambda b, c: (b * nc + c, 0))]
        + [_full(s) for s in _S5_PARAM_SHAPES],
        out_specs=[pl.BlockSpec((S5_CHUNK, D_SSM), lambda b, c: (b * nc + c, 0)),
                   pl.BlockSpec((1, 1, 2 * N_STATE), lambda b, c: (b * nc + c, 0, 0))],
        out_shape=[_sds((t, D_SSM)), _sds((nb * nc, 1, 2 * N_STATE))],
        scratch_shapes=[pltpu.VMEM((1, N_STATE), f32), pltpu.VMEM((1, N_STATE), f32)],
        compiler_params=_cp(2, VMEM_BIG),
    )(z, *params)


def s5_bwd(z, params, hs, dy, nb, name):
    t = z.shape[0]
    nc = t // nb // S5_CHUNK
    npar = len(_S5_PARAM_SHAPES)

    def body(z_ref, hs_ref, dy_ref, *rest):
        p_refs, dz_ref, dp_refs, (dhr_s, dhi_s) = rest[:npar], rest[npar], rest[npar + 1:2 * npar + 1], rest[2 * npar + 1:]

        @pl.when(pl.program_id(1) == 0)
        def _():
            dhr_s[...] = jnp.zeros_like(dhr_s)
            dhi_s[...] = jnp.zeros_like(dhi_s)

        prim = (_step_major(z_ref, slice(0, D_SSM)), _step_major(z_ref, slice(D_SSM, 2 * D_SSM)),
                hs_ref[0, :, :N_STATE], hs_ref[0, :, N_STATE:]) + tuple(r[...] for r in p_refs)
        _, vj = jax.vjp(_s5_chunk, *prim)
        cts = vj((_step_major(dy_ref, slice(0, D_SSM)), dhr_s[...], dhi_s[...]))
        _store_step_major(dz_ref, slice(0, D_SSM), cts[0])
        _store_step_major(dz_ref, slice(D_SSM, 2 * D_SSM), cts[1])
        dhr_s[...] = cts[2]
        dhi_s[...] = cts[3]
        first = (pl.program_id(0) == 0) & (pl.program_id(1) == 0)
        for r, v in zip(dp_refs, cts[4:]):
            _acc(r, v, first)

    rev = lambda b, c: (b * nc + nc - 1 - c, 0)
    return pl.pallas_call(
        body, name=name, grid=(nb, nc),
        in_specs=[pl.BlockSpec((S5_CHUNK, 2 * D_SSM), rev),
                  pl.BlockSpec((1, 1, 2 * N_STATE), lambda b, c: (b * nc + nc - 1 - c, 0, 0)),
                  pl.BlockSpec((S5_CHUNK, D_SSM), rev)] + [_full(s) for s in _S5_PARAM_SHAPES],
        out_specs=[pl.BlockSpec((S5_CHUNK, 2 * D_SSM), rev)] + [_full(s) for s in _S5_PARAM_SHAPES],
        out_shape=[_sds((t, 2 * D_SSM), _MXU)] + [_sds(s) for s in _S5_PARAM_SHAPES],
        scratch_shapes=[pltpu.VMEM((1, N_STATE), f32), pltpu.VMEM((1, N_STATE), f32)],
        compiler_params=_cp(2, VMEM_BIG),
    )(z, hs, dy, *params)


DN_PRE_ROWS = 512
DN_COLS = 3 * D_DN // LANE


def dn_pre_bwd(zq, convw, dqkv, seq, name):
    t, tb = zq.shape[0], DN_PRE_ROWS
    nrow = t // tb
    per_seq = seq // tb

    def body(xc_ref, xp_ref, w_ref, d_ref, dx_ref, dw_ref, carry):
        step = pl.program_id(0)
        i = nrow - 1 - step

        @pl.when(step == 0)
        def _():
            carry[...] = jnp.zeros_like(carry)

        for j in range(DN_COLS):
            cols = slice(j * LANE, (j + 1) * LANE)
            fn = functools.partial(_dn_pre, is_start=i % per_seq == 0, col=j)
            _, vj = jax.vjp(fn, xc_ref[:, cols], xp_ref[:, cols], w_ref[0:1, cols], w_ref[1:2, cols],
                            w_ref[2:3, cols], w_ref[3:4, cols])
            dxc, dxp, dw0, dw1, dw2, dw3 = vj(d_ref[:, cols])
            dx_ref[:tb - DN_HALO, cols] = dxc[:tb - DN_HALO].astype(_MXU)
            dx_ref[tb - DN_HALO:, cols] = (dxc[tb - DN_HALO:] + carry[:, cols]).astype(_MXU)
            carry[:, cols] = dxp
            for k, dw in enumerate((dw0, dw1, dw2, dw3)):
                @pl.when(step == 0)
                def _():
                    dw_ref[k:k + 1, cols] = dw

                @pl.when(step != 0)
                def _():
                    dw_ref[k:k + 1, cols] += dw

    rev = lambda s: (nrow - 1 - s, 0)
    return pl.pallas_call(
        body, name=name, grid=(nrow,),
        in_specs=[pl.BlockSpec((tb, 3 * D_DN), rev),
                  pl.BlockSpec((DN_HALO, 3 * D_DN),
                               lambda s: (jnp.maximum((nrow - 1 - s) * (tb // DN_HALO) - 1, 0), 0)),
                  _full((DN_CONV, 3 * D_DN)), pl.BlockSpec((tb, 3 * D_DN), rev)],
        out_specs=[pl.BlockSpec((tb, 3 * D_DN), rev), _full((DN_CONV, 3 * D_DN))],
        out_shape=[_sds((t, 3 * D_DN), _MXU), _sds((DN_CONV, 3 * D_DN))],
        scratch_shapes=[pltpu.VMEM((DN_HALO, 3 * D_DN), f32)],
        compiler_params=_cp(1, VMEM_BIG),
    )(zq, zq, convw, dqkv)


DN_LOCAL_CHUNKS = 4
DN_FRONT_CHUNKS = 8
DN_ATTN = DN_HEADS * DN_CHUNK


def _dn_heads(ref, rows, base=0):
    return [ref[rows, base + h * DN_HEAD_DIM:base + (h + 1) * DN_HEAD_DIM] for h in range(DN_HEADS)]


def dn_front_fwd(zq, convw, ab, alog, dtb, seq, name):
    t = zq.shape[0]
    c, n = DN_CHUNK, DN_FRONT_CHUNKS
    per_seq = seq // (n * c)

    def body(xc_ref, xp_ref, w_ref, ab_ref, alog_ref, dtb_ref,
             qkv_ref, val_ref, kcd_ref, attn_ref, qd_ref, kd_ref, el_ref, inv_ref):
        is_start = pl.program_id(0) % per_seq == 0
        blocks = []
        for j in range(DN_COLS):
            cols = slice(j * LANE, (j + 1) * LANE)
            blocks.append(_dn_pre(xc_ref[:, cols], xp_ref[:, cols], w_ref[0:1, cols], w_ref[1:2, cols],
                                  w_ref[2:3, cols], w_ref[3:4, cols], is_start, j))
            qkv_ref[:, cols] = blocks[-1]
        rows = [pl.ds(j * c, c) for j in range(n)]

        def heads(base, j):
            return [blocks[base + h][j * c:(j + 1) * c] for h in range(DN_HEADS)]

        vals, kcds, attns, qds, kds, els, invs = _dn_local(
            [heads(0, j) for j in range(n)], [heads(DN_HEADS, j) for j in range(n)],
            [heads(2 * DN_HEADS, j) for j in range(n)], [ab_ref[r, :] for r in rows], alog_ref[...], dtb_ref[...])
        for j, r in enumerate(rows):
            for h in range(DN_HEADS):
                lo, hi = h * DN_HEAD_DIM, (h + 1) * DN_HEAD_DIM
                val_ref[r, lo:hi] = vals[j][h]
                kcd_ref[r, lo:hi] = kcds[j][h].astype(_MXU)
                qd_ref[r, lo:hi] = qds[j][h].astype(_MXU)
                kd_ref[r, lo:hi] = kds[j][h].astype(_MXU)
                attn_ref[r, h * c:(h + 1) * c] = attns[j][h].astype(_MXU)
                inv_ref[r, h * c:(h + 1) * c] = invs[j][h]
            el_ref[j] = els[j]

    wide = _rows(n * c, D_DN)
    outs = pl.pallas_call(
        body, name=name, grid=(t // (n * c),),
        in_specs=[_rows(n * c, 3 * D_DN),
                  pl.BlockSpec((DN_HALO, 3 * D_DN), lambda i: (jnp.maximum(i * (n * c // DN_HALO) - 1, 0), 0)),
                  _full((DN_CONV, 3 * D_DN)), _rows(n * c, LANE), _full((1, LANE)), _full((1, LANE))],
        out_specs=[_rows(n * c, 3 * D_DN), wide, wide, _rows(n * c, DN_ATTN), wide, wide,
                   pl.BlockSpec((n, 1, LANE), lambda i: (i, 0, 0)), _rows(n * c, DN_ATTN)],
        out_shape=[_sds((t, 3 * D_DN)), _sds((t, D_DN)), _sds((t, D_DN), _MXU), _sds((t, DN_ATTN), _MXU),
                   _sds((t, D_DN), _MXU), _sds((t, D_DN), _MXU), _sds((t // c, 1, LANE)), _sds((t, DN_ATTN))],
        compiler_params=_cp(1, VMEM_BIG),
    )(zq, zq, convw, ab, alog, dtb)
    return outs[0], outs[1:7], outs[7]


def dn_local_bwd(qkv, ab, alog, dtb, inv, cts, name):
    t = qkv.shape[0]
    c, n = DN_CHUNK, DN_LOCAL_CHUNKS

    def body(qkv_ref, ab_ref, alog_ref, dtb_ref, inv_ref, dval_ref, dkcd_ref, dattn_ref, dqd_ref, dkd_ref, del_ref,
             dqkv_ref, dab_ref, dalog_ref, ddtb_ref):
        rows = [pl.ds(j * c, c) for j in range(n)]
        invs = [[inv_ref[r, h * c:(h + 1) * c] for h in range(DN_HEADS)] for r in rows]

        def local(qs, ks, vs, abs_, alog, dtb):
            return _dn_local(qs, ks, vs, abs_, alog, dtb, invs)[:6]

        _, vj = jax.vjp(local, [_dn_heads(qkv_ref, r) for r in rows], [_dn_heads(qkv_ref, r, D_DN) for r in rows],
                        [_dn_heads(qkv_ref, r, 2 * D_DN) for r in rows], [ab_ref[r, :] for r in rows], alog_ref[...],
                        dtb_ref[...])
        dattn = [[dattn_ref[r, h * c:(h + 1) * c] for h in range(DN_HEADS)] for r in rows]
        dq, dk, dv, dab, dalog, ddtb = vj(([_dn_heads(dval_ref, r) for r in rows], [_dn_heads(dkcd_ref, r) for r in rows],
                                           dattn, [_dn_heads(dqd_ref, r) for r in rows],
                                           [_dn_heads(dkd_ref, r) for r in rows], [del_ref[j] for j in range(n)]))
        for j, r in enumerate(rows):
            for h in range(DN_HEADS):
                lo, hi = h * DN_HEAD_DIM, (h + 1) * DN_HEAD_DIM
                dqkv_ref[r, lo:hi] = dq[j][h]
                dqkv_ref[r, D_DN + lo:D_DN + hi] = dk[j][h]
                dqkv_ref[r, 2 * D_DN + lo:2 * D_DN + hi] = dv[j][h]
            dab_ref[r, :] = dab[j].astype(_MXU)
        first = pl.program_id(0) == 0
        _acc(dalog_ref, dalog, first)
        _acc(ddtb_ref, ddtb, first)

    wide = _rows(n * c, D_DN)
    return pl.pallas_call(
        body, name=name, grid=(t // (n * c),),
        in_specs=[_rows(n * c, 3 * D_DN), _rows(n * c, LANE), _full((1, LANE)), _full((1, LANE)),
                  _rows(n * c, DN_ATTN), wide, wide, _rows(n * c, DN_ATTN), wide, wide,
                  pl.BlockSpec((n, 1, LANE), lambda i: (i, 0, 0))],
        out_specs=[_rows(n * c, 3 * D_DN), _rows(n * c, LANE), _full((1, LANE)), _full((1, LANE))],
        out_shape=[_sds((t, 3 * D_DN)), _sds((t, LANE), _MXU), _sds((1, LANE)), _sds((1, LANE))],
        compiler_params=_cp(1),
    )(qkv, ab, alog, dtb, inv, *cts)


def _seq_view(a, nb):
    return a.reshape((nb, a.shape[0] // nb) + a.shape[1:])


def _dn_chains(nb):
    return [(b, h) for b in range(nb) for h in range(DN_HEADS)]


DN_SCAN_CHUNKS = 4
DN_SCAN_FWD_CHUNKS = 8


def _dn_step_operands(val_ref, kcd_ref, attn_ref, qd_ref, kd_ref, el_ref, gg_ref, nb, j):
    chains = _dn_chains(nb)
    c = DN_CHUNK
    rows = pl.ds(j * c, c)

    def wide(ref):
        return [ref[b, rows, h * DN_HEAD_DIM:(h + 1) * DN_HEAD_DIM].astype(f32) for b, h in chains]

    attns = [attn_ref[b, rows, h * c:(h + 1) * c].astype(f32) for b, h in chains]
    return (wide(val_ref), wide(kcd_ref), attns, wide(qd_ref), wide(kd_ref),
            [el_ref[b, j, :, h:h + 1] for b, h in chains], wide(gg_ref))


def dn_scan_fwd(loc, gg, ng, nb, name):
    val, kcd, attn, qd, kd, el = loc
    t = val.shape[0]
    c, n = DN_CHUNK, DN_SCAN_FWD_CHUNKS
    nc = t // nb // c
    ns = nb * DN_HEADS

    def body(val_ref, kcd_ref, attn_ref, qd_ref, kd_ref, el_ref, gg_ref, ng_ref, y_ref, ss_ref, st):
        @pl.when(pl.program_id(0) == 0)
        def _():
            st[...] = jnp.zeros_like(st)

        sts = [st[i] for i in range(ns)]
        for j in range(n):
            for i in range(ns):
                ss_ref[j, i] = sts[i]
            ys, sts = _dn_step(*_dn_step_operands(val_ref, kcd_ref, attn_ref, qd_ref, kd_ref, el_ref, gg_ref, nb, j),
                               sts, ng_ref[...])
            for i, (b, h) in enumerate(_dn_chains(nb)):
                y_ref[b, pl.ds(j * c, c), h * DN_HEAD_DIM:(h + 1) * DN_HEAD_DIM] = ys[i]
        for i in range(ns):
            st[i] = sts[i]

    def blk(w):
        return pl.BlockSpec((nb, n * c, w), lambda k: (0, k, 0))

    el_spec = pl.BlockSpec((nb, n, 1, LANE), lambda k: (0, k, 0, 0))
    y, ss = pl.pallas_call(
        body, name=name, grid=(nc // n,),
        in_specs=[blk(D_DN), blk(D_DN), blk(DN_ATTN), blk(D_DN), blk(D_DN), el_spec, blk(D_DN), _full((1, LANE))],
        out_specs=[blk(D_DN), pl.BlockSpec((n, ns, DN_HEAD_DIM, DN_HEAD_DIM), lambda k: (k, 0, 0, 0))],
        out_shape=[_sds((nb, t // nb, D_DN)), _sds((nc, ns, DN_HEAD_DIM, DN_HEAD_DIM))],
        scratch_shapes=[pltpu.VMEM((ns, DN_HEAD_DIM, DN_HEAD_DIM), f32)],
        compiler_params=_cp(1, VMEM_BIG),
    )(_seq_view(val, nb), _seq_view(kcd, nb), _seq_view(attn, nb), _seq_view(qd, nb), _seq_view(kd, nb),
      el.reshape(nb, nc, 1, LANE), _seq_view(gg, nb), ng)
    return y.reshape(t, D_DN), ss


def dn_scan_bwd(loc, gg, ng, ss, dy, nb, name):
    val, kcd, attn, qd, kd, el = loc
    t = val.shape[0]
    c, n = DN_CHUNK, DN_SCAN_CHUNKS
    nc = t // nb // c
    ns = nb * DN_HEADS
    steps = nc // n

    def body(val_ref, kcd_ref, attn_ref, qd_ref, kd_ref, el_ref, gg_ref, ng_ref, ss_ref, dy_ref,
             dval_ref, dkcd_ref, dattn_ref, dqd_ref, dkd_ref, del_ref, dgg_ref, dng_ref, dst):
        @pl.when(pl.program_id(0) == 0)
        def _():
            dst[...] = jnp.zeros_like(dst)

        lane = lax.broadcasted_iota(jnp.int32, (1, LANE), 1)
        chains = _dn_chains(nb)
        ds = [dst[i] for i in range(ns)]
        dng_tot = jnp.zeros((1, LANE), f32)
        for j in reversed(range(n)):
            rows = pl.ds(j * c, c)
            _, vj = jax.vjp(_dn_step,
                            *_dn_step_operands(val_ref, kcd_ref, attn_ref, qd_ref, kd_ref, el_ref, gg_ref, nb, j),
                            [ss_ref[j, i] for i in range(ns)], ng_ref[...])
            dys = [dy_ref[b, rows, h * DN_HEAD_DIM:(h + 1) * DN_HEAD_DIM] for b, h in chains]
            dval, dkcd, dattn, dqd, dkd, dlast, dgg, ds, dng = vj((dys, ds))
            dng_tot = dng_tot + dng
            del_rows = [jnp.zeros((1, LANE), f32) for _ in range(nb)]
            for i, (b, h) in enumerate(chains):
                cols = slice(h * DN_HEAD_DIM, (h + 1) * DN_HEAD_DIM)
                dval_ref[b, rows, cols] = dval[i]
                dkcd_ref[b, rows, cols] = dkcd[i]
                dattn_ref[b, rows, h * c:(h + 1) * c] = dattn[i]
                dqd_ref[b, rows, cols] = dqd[i]
                dkd_ref[b, rows, cols] = dkd[i]
                dgg_ref[b, rows, cols] = dgg[i].astype(_MXU)
                del_rows[b] = del_rows[b] + jnp.where(lane == h, dlast[i], 0.0)
            for b in range(nb):
                del_ref[b, j] = del_rows[b]
        for i in range(ns):
            dst[i] = ds[i]
        _acc(dng_ref, dng_tot, pl.program_id(0) == 0)

    def blk(w):
        return pl.BlockSpec((nb, n * c, w), lambda k: (0, steps - 1 - k, 0))

    el_spec = pl.BlockSpec((nb, n, 1, LANE), lambda k: (0, steps - 1 - k, 0, 0))
    outs = pl.pallas_call(
        body, name=name, grid=(steps,),
        in_specs=[blk(D_DN), blk(D_DN), blk(DN_ATTN), blk(D_DN), blk(D_DN), el_spec, blk(D_DN), _full((1, LANE)),
                  pl.BlockSpec((n, ns, DN_HEAD_DIM, DN_HEAD_DIM), lambda k: (steps - 1 - k, 0, 0, 0)), blk(D_DN)],
        out_specs=[blk(D_DN), blk(D_DN), blk(DN_ATTN), blk(D_DN), blk(D_DN), el_spec, blk(D_DN), _full((1, LANE))],
        out_shape=[_sds((nb, t // nb, D_DN)), _sds((nb, t // nb, D_DN)), _sds((nb, t // nb, DN_ATTN)),
                   _sds((nb, t // nb, D_DN)), _sds((nb, t // nb, D_DN)), _sds((nb, nc, 1, LANE)),
                   _sds((nb, t // nb, D_DN), _MXU), _sds((1, LANE))],
        scratch_shapes=[pltpu.VMEM((ns, DN_HEAD_DIM, DN_HEAD_DIM), f32)],
        compiler_params=_cp(1, VMEM_BIG),
    )(_seq_view(val, nb), _seq_view(kcd, nb), _seq_view(attn, nb), _seq_view(qd, nb), _seq_view(kd, nb),
      el.reshape(nb, nc, 1, LANE), _seq_view(gg, nb), ng, ss, _seq_view(dy, nb))
    dloc = [o.reshape((t,) + o.shape[2:]) for o in outs[:5]] + [outs[5].reshape(t // c, 1, LANE)]
    return dloc, outs[6].reshape(t, D_DN), outs[7]


SG_ROWS = 1024


def sg_fwd(z, lng, lnb, w, bt, name):
    t = z.shape[0]

    def body(z_ref, lng_ref, lnb_ref, w_ref, bt_ref, y_ref):
        ws = [w_ref[h] for h in range(SG_HEADS)]
        for k in range(SG_ROWS // SG_CHUNK):
            r = pl.ds(k * SG_CHUNK, SG_CHUNK)
            y_ref[r, :] = _sg_chunk(z_ref[r, :D_SG], z_ref[r, D_SG:2 * D_SG], z_ref[r, 2 * D_SG:], lng_ref[...],
                                    lnb_ref[...], ws, bt_ref[...])

    return pl.pallas_call(
        body, name=name, grid=(t // SG_ROWS,),
        in_specs=[_rows(SG_ROWS, 3 * D_SG), _full((1, D_SG)), _full((1, D_SG)),
                  _full((SG_HEADS, SG_CHUNK, SG_CHUNK)), _full((SG_CHUNK, LANE))],
        out_specs=_rows(SG_ROWS, D_SG),
        out_shape=_sds((t, D_SG)),
        compiler_params=_cp(1),
    )(z, lng, lnb, w, bt)


def sg_bwd(z, lng, lnb, w, bt, dy, name):
    t = z.shape[0]

    def body(z_ref, lng_ref, lnb_ref, w_ref, bt_ref, dy_ref, dz_ref, dlng_ref, dlnb_ref, dw_ref, dbt_ref):
        ws = [w_ref[h] for h in range(SG_HEADS)]
        tot = None
        for k in range(SG_ROWS // SG_CHUNK):
            r = pl.ds(k * SG_CHUNK, SG_CHUNK)
            _, vj = jax.vjp(_sg_chunk, z_ref[r, :D_SG], z_ref[r, D_SG:2 * D_SG], z_ref[r, 2 * D_SG:], lng_ref[...],
                            lnb_ref[...], ws, bt_ref[...])
            du, dv, dgate, dlng, dlnb, dws, dbt = vj(dy_ref[r, :])
            dz_ref[r, :D_SG] = du.astype(_MXU)
            dz_ref[r, D_SG:2 * D_SG] = dv.astype(_MXU)
            dz_ref[r, 2 * D_SG:] = dgate.astype(_MXU)
            part = [dlng, dlnb, dbt] + list(dws)
            tot = part if tot is None else [a + b for a, b in zip(tot, part)]
        first = pl.program_id(0) == 0
        _acc(dlng_ref, tot[0], first)
        _acc(dlnb_ref, tot[1], first)
        _acc(dbt_ref, tot[2], first)
        for h in range(SG_HEADS):
            @pl.when(first)
            def _():
                dw_ref[h] = tot[3 + h]

            @pl.when(jnp.logical_not(first))
            def _():
                dw_ref[h] += tot[3 + h]

    return pl.pallas_call(
        body, name=name, grid=(t // SG_ROWS,),
        in_specs=[_rows(SG_ROWS, 3 * D_SG), _full((1, D_SG)), _full((1, D_SG)),
                  _full((SG_HEADS, SG_CHUNK, SG_CHUNK)), _full((SG_CHUNK, LANE)), _rows(SG_ROWS, D_SG)],
        out_specs=[_rows(SG_ROWS, 3 * D_SG), _full((1, D_SG)), _full((1, D_SG)),
                   _full((SG_HEADS, SG_CHUNK, SG_CHUNK)), _full((SG_CHUNK, LANE))],
        out_shape=[_sds((t, 3 * D_SG), _MXU), _sds((1, D_SG)), _sds((1, D_SG)), _sds((SG_HEADS, SG_CHUNK, SG_CHUNK)),
                   _sds((SG_CHUNK, LANE))],
        compiler_params=_cp(1),
    )(z, lng, lnb, w, bt, dy)


def add_pairs(a_list, b_list, name):
    n = len(a_list)

    def body(*refs):
        for a_ref, b_ref, o_ref in zip(refs[:n], refs[n:2 * n], refs[2 * n:]):
            o_ref[...] = (a_ref[...].astype(f32) + b_ref[...].astype(f32)).astype(o_ref.dtype)

    return pl.pallas_call(
        body, name=name, out_shape=[_sds(a.shape, a.dtype) for a in a_list],
        compiler_params=pltpu.CompilerParams(vmem_limit_bytes=VMEM_BIG),
    )(*a_list, *b_list)


def _adamw(g, w, m, v):
    nm = ADAM_B1 * m + (1.0 - ADAM_B1) * g
    nv = ADAM_B2 * v + (1.0 - ADAM_B2) * jnp.square(g)
    m_hat = nm / (1.0 - ADAM_B1 ** ADAM_STEP)
    v_hat = nv / (1.0 - ADAM_B2 ** ADAM_STEP)
    return -ADAM_LR * (m_hat / (jnp.sqrt(v_hat) + ADAM_EPS) + ADAM_WD * w), nm, nv


def sum_parts(half, recv, name):
    _, r, c = recv.shape
    tr = 256 if r % 256 == 0 else r

    def body(half_ref, recv_ref, g_ref):
        g = recv_ref[0].astype(f32)
        for k in range(1, N_CHIPS):
            g = g + recv_ref[k].astype(f32)
        g_ref[...] = g

    return pl.pallas_call(
        body, name=name,
        grid_spec=pltpu.PrefetchScalarGridSpec(
            num_scalar_prefetch=1, grid=(r // tr,),
            in_specs=[pl.BlockSpec((N_CHIPS, tr, c), lambda i, h: (0, i, 0))],
            out_specs=pl.BlockSpec((None, tr, c), lambda i, h: (h[0], i, 0))),
        out_shape=_sds((2, r, c)),
        compiler_params=_cp(1, VMEM_BIG),
    )(half, recv)


def adamw(g, w, m, v, name):
    _, r, c = w.shape
    tr = 256 if r % 256 == 0 else r

    def body(g_ref, w_ref, m_ref, v_ref, d_ref, nm_ref, nv_ref):
        d_ref[...], nm_ref[...], nv_ref[...] = _adamw(g_ref[...], w_ref[...], m_ref[...], v_ref[...])

    blk = pl.BlockSpec((None, tr, c), lambda l, i: (l, i, 0))
    return pl.pallas_call(
        body, name=name, grid=(2, r // tr), in_specs=[blk] * 4, out_specs=[blk] * 3, out_shape=[_sds((2, r, c))] * 3,
        compiler_params=_cp(2, VMEM_BIG),
    )(g, w, m, v)


def sum_parts_small(chip, parts, sums, name):
    n = len(sums)

    def body(chip_ref, *refs):
        for part, own, out in zip(refs[:n], refs[n:2 * n], refs[2 * n:]):
            g = jnp.where(chip_ref[0] == 0, own[...], part[0])
            for q in range(1, N_CHIPS):
                g = g + jnp.where(chip_ref[0] == q, own[...], part[q])
            out[...] = g

    vmem = pl.BlockSpec(memory_space=pltpu.VMEM)
    return pl.pallas_call(
        body, name=name, in_specs=[pl.BlockSpec(memory_space=pltpu.SMEM)] + [vmem] * (2 * n), out_specs=[vmem] * n,
        out_shape=[_sds(s.shape) for s in sums], compiler_params=pltpu.CompilerParams(vmem_limit_bytes=VMEM_BIG),
    )(chip, *parts, *sums)


def adamw_small(gs, ws, ms, vs, name):
    n = len(ws)

    def body(*refs):
        ins, outs = refs[:4 * n], refs[4 * n:]
        for k in range(n):
            outs[k][...], outs[n + k][...], outs[2 * n + k][...] = _adamw(
                ins[k][...], ins[n + k][...], ins[2 * n + k][...], ins[3 * n + k][...])

    outs = pl.pallas_call(
        body, name=name, out_shape=[_sds(w.shape) for w in ws] * 3,
        compiler_params=pltpu.CompilerParams(vmem_limit_bytes=VMEM_BIG),
    )(*gs, *ws, *ms, *vs)
    return [outs[j * n:(j + 1) * n] for j in range(3)]


_ANY = pl.BlockSpec(memory_space=pltpu.HBM)
_MESH = pl.DeviceIdType.MESH


def _flip(v, bit):
    return 1 - v if bit else v


_CHIP_RELS = ((1, 0), (0, 1), (1, 1))


def _piece(ref, kind, q):
    if kind[0] == "slot":
        return ref.at[q]
    if kind[0] == "all":
        return ref
    _, axis, n = kind
    return ref.at[(slice(None),) * axis + (pl.ds(q * n, n),)]


def _piece_shape(shape, kind):
    if kind[0] == "slot":
        return tuple(shape[1:])
    if kind[0] == "all":
        return tuple(shape)
    _, axis, n = kind
    return tuple(shape[:axis]) + (n,) + tuple(shape[axis + 1:])


def gather_weights(shards, kinds, name):
    n = len(shards)

    def out_shape(s, kind):
        if kind[0] == "slot":
            return (N_CHIPS,) + tuple(s.shape)
        _, axis, w = kind
        return tuple(s.shape[:axis + 1]) + (N_CHIPS * w,) + tuple(s.shape[axis + 2:])

    def place(o_ref, kind, q, layer):
        if kind[0] == "slot":
            return o_ref.at[q, layer]
        return _piece(o_ref.at[layer], kind, q)

    def body(*refs):
        s_refs, o_refs = refs[:n], refs[n:2 * n]
        send_sems, recv_sems, fwd_send_sems, fwd_recv_sems = refs[2 * n:]
        x, y, c = lax.axis_index("x"), lax.axis_index("y"), lax.axis_index("c")
        mine = 2 * x + y
        sends, arrivals, forwards, fwd_arrivals = [], [], [], []
        for r, (fx, fy) in enumerate(_CHIP_RELS):
            px, py = _flip(x, fx), _flip(y, fy)
            peer = 2 * px + py
            for k in range(n):
                s = r * n + k
                sends.append(pltpu.make_async_remote_copy(
                    src_ref=s_refs[k].at[c], dst_ref=place(o_refs[k], kinds[k], mine, c), send_sem=send_sems.at[s],
                    recv_sem=recv_sems.at[s], device_id=(px, py, c), device_id_type=_MESH))
                arrivals.append(pltpu.make_async_remote_copy(
                    src_ref=s_refs[k].at[c], dst_ref=place(o_refs[k], kinds[k], peer, c), send_sem=send_sems.at[s],
                    recv_sem=recv_sems.at[s], device_id=(px, py, c), device_id_type=_MESH))
                block = place(o_refs[k], kinds[k], peer, c)
                forwards.append(pltpu.make_async_remote_copy(
                    src_ref=block, dst_ref=block, send_sem=fwd_send_sems.at[s], recv_sem=fwd_recv_sems.at[s],
                    device_id=(x, y, 1 - c), device_id_type=_MESH))
                other = place(o_refs[k], kinds[k], peer, 1 - c)
                fwd_arrivals.append(pltpu.make_async_remote_copy(
                    src_ref=other, dst_ref=other, send_sem=fwd_send_sems.at[s], recv_sem=fwd_recv_sems.at[s],
                    device_id=(x, y, 1 - c), device_id_type=_MESH))
        for cp in sends:
            cp.start()
        for arrived, fwd in zip(arrivals, forwards):
            arrived.wait_recv()
            fwd.start()
        for cp in fwd_arrivals:
            cp.wait_recv()
        for cp in sends + forwards:
            cp.wait_send()

    m = len(_CHIP_RELS) * n
    return pl.pallas_call(
        body, name=name, in_specs=[_ANY] * n, out_specs=[_ANY] * n,
        out_shape=[_sds(out_shape(s, k), s.dtype) for s, k in zip(shards, kinds)],
        scratch_shapes=[pltpu.SemaphoreType.DMA((m,))] * 4,
    )(*shards)


def _owned_by(owners, side):
    return [k for k, o in enumerate(owners) if o == side]


def exchange_halves(gs, smalls, owners, name):
    n, ns = len(gs), len(smalls)

    def body(*refs):
        g_refs, s_refs = refs[:n], refs[n:n + ns]
        got_refs, sgot_refs = refs[n + ns:2 * n + ns], refs[2 * n + ns:2 * (n + ns)]
        send_sems, recv_sems = refs[2 * (n + ns):]
        x, y, c = lax.axis_index("x"), lax.axis_index("y"), lax.axis_index("c")
        sibling = (x, y, 1 - c)
        swaps = [pltpu.make_async_remote_copy(
            src_ref=g_refs[k].at[1 - c], dst_ref=got_refs[k], send_sem=send_sems.at[k], recv_sem=recv_sems.at[k],
            device_id=sibling, device_id_type=_MESH) for k in range(n)]
        gives = [pltpu.make_async_remote_copy(
            src_ref=s_refs[k], dst_ref=sgot_refs[k], send_sem=send_sems.at[n + k], recv_sem=recv_sems.at[n + k],
            device_id=sibling, device_id_type=_MESH) for k in range(ns)]
        for cp in swaps:
            cp.start()
        for side in (0, 1):
            @pl.when(c == 1 - side)
            def _():
                for k in _owned_by(owners, side):
                    gives[k].start()
        for cp in swaps:
            cp.wait()
        for side in (0, 1):
            @pl.when(c == 1 - side)
            def _():
                for k in _owned_by(owners, side):
                    gives[k].wait_send()

            @pl.when(c == side)
            def _():
                for k in _owned_by(owners, side):
                    gives[k].wait_recv()

    outs = pl.pallas_call(
        body, name=name, in_specs=[_ANY] * (n + ns), out_specs=[_ANY] * (n + ns),
        out_shape=[_sds(g.shape[1:], g.dtype) for g in gs] + [_sds(s.shape, s.dtype) for s in smalls],
        scratch_shapes=[pltpu.SemaphoreType.DMA((n + ns,)), pltpu.SemaphoreType.DMA((n + ns,))],
    )(*gs, *smalls)
    return outs[:n], outs[n:]


def reduce_to_chips(ts, kinds, smalls, owners, name):
    n, ns = len(ts), len(smalls)

    def body(*refs):
        t_refs, s_refs = refs[:n], refs[n:n + ns]
        o_refs, so_refs = refs[n + ns:2 * n + ns], refs[2 * n + ns:2 * (n + ns)]
        send_sems, recv_sems = refs[2 * (n + ns):]
        x, y, c = lax.axis_index("x"), lax.axis_index("y"), lax.axis_index("c")
        mine = 2 * x + y
        sends, arrivals, small_sends, small_arrivals = [], [], [], []
        for r, (fx, fy) in enumerate(_CHIP_RELS):
            px, py = _flip(x, fx), _flip(y, fy)
            peer = 2 * px + py
            for k in range(n + ns):
                s = r * (n + ns) + k
                if k < n:
                    src, dst = _piece(t_refs[k], kinds[k], peer), o_refs[k]
                else:
                    src, dst = s_refs[k - n], so_refs[k - n]
                go = pltpu.make_async_remote_copy(
                    src_ref=src, dst_ref=dst.at[mine], send_sem=send_sems.at[s], recv_sem=recv_sems.at[s],
                    device_id=(px, py, c), device_id_type=_MESH)
                come = pltpu.make_async_remote_copy(
                    src_ref=src, dst_ref=dst.at[peer], send_sem=send_sems.at[s], recv_sem=recv_sems.at[s],
                    device_id=(px, py, c), device_id_type=_MESH)
                (sends if k < n else small_sends).append(go)
                (arrivals if k < n else small_arrivals).append(come)

        def owned(copies, side):
            return [cp for j, cp in enumerate(copies) if owners[j % ns] == side]

        for cp in sends:
            cp.start()
        for side in (0, 1):
            @pl.when(c == side)
            def _():
                for cp in owned(small_sends, side):
                    cp.start()
        for cp in arrivals:
            cp.wait_recv()
        for cp in sends:
            cp.wait_send()
        for side in (0, 1):
            @pl.when(c == side)
            def _():
                for cp in owned(small_arrivals, side):
                    cp.wait_recv()
                for cp in owned(small_sends, side):
                    cp.wait_send()

    m = len(_CHIP_RELS) * (n + ns)
    outs = pl.pallas_call(
        body, name=name, in_specs=[_ANY] * (n + ns), out_specs=[_ANY] * (n + ns),
        out_shape=[_sds((N_CHIPS,) + _piece_shape(t.shape, k), t.dtype) for t, k in zip(ts, kinds)]
        + [_sds((N_CHIPS,) + s.shape, s.dtype) for s in smalls],
        scratch_shapes=[pltpu.SemaphoreType.DMA((m,)), pltpu.SemaphoreType.DMA((m,))],
    )(*ts, *smalls)
    return outs[:n], outs[n:]


def share_halves(rs, smalls, owners, name):
    n, ns = len(rs), len(smalls)

    def body(*refs):
        o_refs, so_refs = refs[n + ns:2 * n + ns], refs[2 * n + ns:2 * (n + ns)]
        send_sems, recv_sems = refs[2 * (n + ns):]
        x, y, c = lax.axis_index("x"), lax.axis_index("y"), lax.axis_index("c")
        sibling = (x, y, 1 - c)
        swaps = [pltpu.make_async_remote_copy(
            src_ref=o_refs[k].at[c], dst_ref=o_refs[k].at[c], send_sem=send_sems.at[k], recv_sem=recv_sems.at[k],
            device_id=sibling, device_id_type=_MESH) for k in range(n)]
        arrivals = [pltpu.make_async_remote_copy(
            src_ref=o_refs[k].at[c], dst_ref=o_refs[k].at[1 - c], send_sem=send_sems.at[k], recv_sem=recv_sems.at[k],
            device_id=sibling, device_id_type=_MESH) for k in range(n)]
        gives = [pltpu.make_async_remote_copy(
            src_ref=so_refs[k], dst_ref=so_refs[k], send_sem=send_sems.at[n + k], recv_sem=recv_sems.at[n + k],
            device_id=sibling, device_id_type=_MESH) for k in range(ns)]
        for cp in swaps:
            cp.start()
        for side in (0, 1):
            @pl.when(c == side)
            def _():
                for k in _owned_by(owners, side):
                    gives[k].start()
        for cp in arrivals:
            cp.wait_recv()
        for cp in swaps:
            cp.wait_send()
        for side in (0, 1):
            @pl.when(c == side)
            def _():
                for k in _owned_by(owners, side):
                    gives[k].wait_send()

            @pl.when(c == 1 - side)
            def _():
                for k in _owned_by(owners, side):
                    gives[k].wait_recv()

    outs = pl.pallas_call(
        body, name=name, in_specs=[_ANY] * (n + ns), out_specs=[_ANY] * (n + ns),
        out_shape=[_sds(r.shape, r.dtype) for r in list(rs) + list(smalls)],
        input_output_aliases={k: k for k in range(n + ns)},
        scratch_shapes=[pltpu.SemaphoreType.DMA((n + ns,)), pltpu.SemaphoreType.DMA((n + ns,))],
    )(*rs, *smalls)
    return outs[:n], outs[n:]


def _small_view(a):
    if a.size < 8 * LANE:
        return jnp.pad(a.reshape(-1), (0, 8 * LANE - a.size)).reshape(8, LANE)
    if a.ndim == 1:
        return a.reshape(1, a.shape[0])
    if a.ndim == 4 and a.shape[-1] < LANE:
        return a.reshape(a.shape[0], a.shape[1], a.shape[2] * a.shape[3])
    return a


def _permuted_from_shards(shards):
    parts = []
    for lo, hi in GROUP_COLS:
        for q in range(N_CHIPS):
            a, b = max(lo, q * SHARD_COLS), min(hi, (q + 1) * SHARD_COLS)
            if a < b:
                parts.append(shards[q][..., a - q * SHARD_COLS:b - q * SHARD_COLS])
    pad = jnp.zeros(shards[0].shape[:-1] + (D_IN_PAD - D_IN,), shards[0].dtype)
    return jnp.concatenate(parts + [pad], axis=-1)


def _shards_from_groups(groups):
    in_order = sorted(range(len(GROUP_COLS)), key=lambda j: GROUP_COLS[j][0])
    shards = []
    for q in range(N_CHIPS):
        parts = []
        for j in in_order:
            lo, hi = GROUP_COLS[j]
            a, b = max(lo, q * SHARD_COLS), min(hi, (q + 1) * SHARD_COLS)
            if a < b:
                parts.append(groups[j][..., a - lo:b - lo])
        shards.append(jnp.concatenate(parts, axis=-1))
    return shards


def _expand_b(b):
    eye = jnp.eye(SSM_GROUPS, dtype=b.dtype)
    return jnp.einsum("gnc,gh->gchn", b, eye).reshape(D_SSM, N_STATE)


def _extract_b(e):
    return jnp.einsum("gcgn->gnc", e.reshape(SSM_GROUPS, SSM_GROUP, SSM_GROUPS, SSM_STATE))


def _expand_c(c):
    eye = jnp.eye(SSM_GROUPS, dtype=c.dtype)
    return jnp.einsum("gcn,gh->gnhc", c, eye).reshape(N_STATE, D_SSM)


def _extract_c(e):
    return jnp.einsum("gngc->gcn", e.reshape(SSM_GROUPS, SSM_STATE, SSM_GROUPS, SSM_GROUP))


def _lane_row(v):
    return jnp.pad(v, (0, LANE - v.shape[0])).reshape(1, LANE)


def _layer_params(w, l):
    return dict(
        norm_g=w["norm_g"][l][None], win=w["w_in_perm"][l], wout=w["w_out"][l].astype(_MXU),
        pg=w["ple_norm_g"][l][None], wgate=w["w_ple_gate"][l].astype(_MXU), wple=w["w_ple"][l].astype(_MXU),
        are=w["ssm_a_re"][l].reshape(1, N_STATE), aim=w["ssm_a_im"][l].reshape(1, N_STATE),
        ls=jnp.repeat(w["ssm_log_step"][l], SSM_STATE).reshape(1, N_STATE),
        bre=_expand_b(w["ssm_b_re"][l]), bim=_expand_b(w["ssm_b_im"][l]),
        cr=_expand_c(w["ssm_c_re"][l]), ci=_expand_c(w["ssm_c_im"][l]),
        dr=w["ssm_d"][l].reshape(1, D_SSM), wglu=w["ssm_w_glu"][l].astype(f32), bglu=w["ssm_b_glu"][l][None],
        convw=w["dn_conv_w"][l], alog=_lane_row(w["dn_a_log"][l]), dtb=_lane_row(w["dn_dt_bias"][l]),
        ng=w["dn_norm_g"][l][None],
        lng=w["sg_ln_g"][l][None], lnb=w["sg_ln_b"][l][None], sgw=w["sg_w"][l],
        bt=jnp.pad(w["sg_b"][l].T, ((0, 0), (0, LANE - SG_HEADS))),
    )


def _layer_fwd(x, p, lp, nb, tag):
    seq = x.shape[0] // nb
    h, zs, zq, zg, zsg, zab = in_fwd(x, lp["norm_g"], lp["win"], f"in_fwd{tag}")
    prep = s5_prep_fwd(lp["are"], lp["aim"], lp["ls"], lp["bre"], lp["bim"], f"s5_prep_fwd{tag}")
    s5p = tuple(prep) + (lp["cr"], lp["ci"], lp["dr"], lp["wglu"], lp["bglu"])
    ys, hs = s5_fwd(zs, s5p, nb, f"s5_fwd{tag}")
    qkv, loc, inv = dn_front_fwd(zq, lp["convw"], zab, lp["alog"], lp["dtb"], seq, f"dn_front_fwd{tag}")
    yd, ss = dn_scan_fwd(loc, zg, lp["ng"], nb, f"dn_scan_fwd{tag}")
    yg = sg_fwd(zsg, lp["lng"], lp["lnb"], lp["sgw"], lp["bt"], f"sg_fwd{tag}")
    x2, x1, y, hn = post_fwd(x, ys, yd, yg, p, lp["wout"], lp["pg"], lp["wgate"], lp["wple"], f"post_fwd{tag}")
    saved = dict(x=x, h=h, zs=zs, zq=zq, zg=zg, zsg=zsg, zab=zab, s5p=s5p, hs=hs, qkv=qkv, loc=loc, inv=inv, ss=ss, x1=x1, y=y, hn=hn, p=p)
    return x2, saved


def _layer_bwd(dx2, sv, lp, nb, tag):
    seq = dx2.shape[0] // nb
    dx1, dgp, dpp, dys, dyd, dyg, dpg = post_bwd(dx2, sv["x1"], sv["hn"], sv["p"], lp["wout"], lp["pg"], lp["wgate"],
                                                 lp["wple"], f"post_bwd{tag}")
    g = {}
    g["w_out"] = wgrad(sv["y"], dx1, f"wgrad_out{tag}")
    g["w_ple_gate"] = wgrad(sv["hn"], dgp, f"wgrad_gate{tag}")
    g["w_ple"] = wgrad(sv["p"], dpp, f"wgrad_ple{tag}")
    g["ple_norm_g"] = dpg[0]
    dzsg, dlng, dlnb, dsgw, dbt = sg_bwd(sv["zsg"], lp["lng"], lp["lnb"], lp["sgw"], lp["bt"], dyg, f"sg_bwd{tag}")
    g["sg_ln_g"], g["sg_ln_b"], g["sg_w"], g["sg_b"] = dlng[0], dlnb[0], dsgw, dbt[:, :SG_HEADS].T
    dloc, dzg, dng = dn_scan_bwd(sv["loc"], sv["zg"], lp["ng"], sv["ss"], dyd, nb, f"dn_scan_bwd{tag}")
    dqkv, dzab, dalog, ddtb = dn_local_bwd(sv["qkv"], sv["zab"], lp["alog"], lp["dtb"], sv["inv"], dloc,
                                           f"dn_local_bwd{tag}")
    dzq, dconv = dn_pre_bwd(sv["zq"], lp["convw"], dqkv, seq, f"dn_pre_bwd{tag}")
    g["dn_conv_w"], g["dn_a_log"], g["dn_dt_bias"], g["dn_norm_g"] = dconv, dalog[0, :DN_HEADS], ddtb[0, :DN_HEADS], dng[0]
    s5out = s5_bwd(sv["zs"], sv["s5p"], sv["hs"], dys, nb, f"s5_bwd{tag}")
    dzs, dprep, (dcr, dci, ddr, dwglu, dbglu) = s5out[0], s5out[1:1 + S5_PREPARED], s5out[1 + S5_PREPARED:]
    dare, daim, dls, dbre, dbim = s5_prep_bwd(lp["are"], lp["aim"], lp["ls"], lp["bre"], lp["bim"], dprep,
                                              f"s5_prep_bwd{tag}")
    g["ssm_a_re"] = dare.reshape(SSM_GROUPS, SSM_STATE)
    g["ssm_a_im"] = daim.reshape(SSM_GROUPS, SSM_STATE)
    g["ssm_log_step"] = dls.reshape(SSM_GROUPS, SSM_STATE).sum(axis=1)
    g["ssm_b_re"], g["ssm_b_im"] = _extract_b(dbre), _extract_b(dbim)
    g["ssm_c_re"], g["ssm_c_im"] = _extract_c(dcr), _extract_c(dci)
    g["ssm_d"] = ddr.reshape(SSM_GROUPS, SSM_GROUP)
    g["ssm_w_glu"], g["ssm_b_glu"] = dwglu, dbglu[0]
    dzs_all = (dzs, dzq, dzg, dzsg, dzab)
    dx, dng_in = in_bwd(sv["x"], lp["norm_g"], lp["win"], dzs_all, dx1, f"in_bwd{tag}")
    g["w_in_pieces"] = [wgrad(sv["h"], dz, f"wgrad_in{k}{tag}") for k, dz in enumerate(dzs_all)]
    g["norm_g"] = dng_in[0]
    return dx, g


def _local_step(x, p, target, w, nb):
    lps = [_layer_params(w, l) for l in range(DEPTH)]
    saved = []
    for l in range(DEPTH):
        x, sv = _layer_fwd(x, p[l], lps[l], nb, f"_l{l}")
        saved.append(sv)
    loss_blk, dx, dfg = loss_fwd_bwd(x, w["final_norm_g"][None], target, "loss")
    grads = [None] * DEPTH
    for l in reversed(range(DEPTH)):
        dx, grads[l] = _layer_bwd(dx, saved[l], lps[l], nb, f"_l{l}")
    out = {k: jnp.stack([grads[l][k] for l in range(DEPTH)]) for k in grads[0] if k != "w_in_pieces"}
    out["w_in_pieces"] = [grads[l]["w_in_pieces"] for l in range(DEPTH)]
    out["final_norm_g"] = dfg[0]
    return loss_blk[0, 0], dx, out


def kernel(x, p, norm_g, w_in, ssm_a_re, ssm_a_im, ssm_b_re, ssm_b_im, ssm_c_re, ssm_c_im, ssm_d, ssm_log_step, ssm_w_glu, ssm_b_glu, dn_conv_w, dn_a_log, dn_dt_bias, dn_norm_g, sg_ln_g, sg_ln_b, sg_w, sg_b, w_out, ple_norm_g, w_ple_gate, w_ple, final_norm_g, loss_target, m_norm_g, m_w_in, m_ssm_a_re, m_ssm_a_im, m_ssm_b_re, m_ssm_b_im, m_ssm_c_re, m_ssm_c_im, m_ssm_d, m_ssm_log_step, m_ssm_w_glu, m_ssm_b_glu, m_dn_conv_w, m_dn_a_log, m_dn_dt_bias, m_dn_norm_g, m_sg_ln_g, m_sg_ln_b, m_sg_w, m_sg_b, m_w_out, m_ple_norm_g, m_w_ple_gate, m_w_ple, m_final_norm_g, v_norm_g, v_w_in, v_ssm_a_re, v_ssm_a_im, v_ssm_b_re, v_ssm_b_im, v_ssm_c_re, v_ssm_c_im, v_ssm_d, v_ssm_log_step, v_ssm_w_glu, v_ssm_b_glu, v_dn_conv_w, v_dn_a_log, v_dn_dt_bias, v_dn_norm_g, v_sg_ln_g, v_sg_ln_b, v_sg_w, v_sg_b, v_w_out, v_ple_norm_g, v_w_ple_gate, v_w_ple, v_final_norm_g):
    args = locals()
    w = {n: args[n] for n in WEIGHTS}
    m = {n: args["m_" + n] for n in WEIGHTS}
    v = {n: args["v_" + n] for n in WEIGHTS}
    nb, seq = x.shape[0], x.shape[1]
    t = nb * seq

    full = _gather_full(w)
    loss_local, dx, grads = _local_step(x.reshape(t, D_MODEL), p.reshape(DEPTH, t, D_PLE),
                                        loss_target.reshape(t, D_MODEL), full, nb)
    outs, loss = _reduce_and_update(grads, w, m, v, loss_local)
    return (loss, dx.reshape(nb, seq, D_MODEL), *[outs[0][n] for n in WEIGHTS], *[outs[1][n] for n in WEIGHTS],
            *[outs[2][n] for n in WEIGHTS], *[outs[3][n] for n in WEIGHTS])


def _gather_full(w):
    sh_names = [n for n, _ in SHARDED]
    shards = [w[n] if n == "dn_conv_w" else w[n].astype(_COMM) for n in sh_names]
    gathered = gather_weights(shards, [k for _, k in SHARDED], "gather_weights")
    chip = 2 * lax.axis_index("x") + lax.axis_index("y")
    full = {n: w[n] for n in REPLICATED}
    for (n, kind), shard, got in zip(SHARDED, shards, gathered):
        if kind[0] == "slot":
            full[n] = lax.dynamic_update_index_in_dim(got, shard, chip, 0)
        else:
            full[n] = lax.dynamic_update_slice_in_dim(got, shard, chip * kind[2], axis=kind[1] + 1)
    slots = full.pop("w_in")
    full["w_in_perm"] = _permuted_from_shards([slots[q] for q in range(N_CHIPS)]).astype(_MXU)
    return full


def _reduce_and_update(grads, w, m, v, loss_local):
    sh_names = [n for n, _ in SHARDED]
    sh_kinds = [k for _, k in SHARDED]
    owners = [SMALL_OWNER[n] for n in REPLICATED + ("loss",)]

    def small_views(d):
        return [_small_view(d[n]) for n in REPLICATED]

    grads["w_in"] = jnp.stack([jnp.stack(_shards_from_groups(pieces)) for pieces in grads["w_in_pieces"]])
    gs = [grads[n] if n == "dn_conv_w" else grads[n].astype(_COMM) for n in sh_names]
    sm = small_views(grads) + [_small_view(loss_local.reshape(1))]
    core = lax.axis_index("c")
    chip = 2 * lax.axis_index("x") + lax.axis_index("y")
    got, sm_got = exchange_halves(gs, sm, owners, "exchange_halves")
    sums = add_pairs([lax.dynamic_index_in_dim(g, core, 0, keepdims=False) for g in gs] + sm, list(got) + list(sm_got),
                     "add_halves")
    sums, sm_sums = sums[:len(gs)], sums[len(gs):]
    parts, sm_parts = reduce_to_chips(sums, sh_kinds, sm_sums, owners, "reduce_to_chips")
    parts = list(parts)
    for k, (kind, total) in enumerate(zip(sh_kinds, sums)):
        if kind[0] == "slot":
            own = lax.dynamic_index_in_dim(total, chip, 0, keepdims=False)
        else:
            own = lax.dynamic_slice_in_dim(total, chip * kind[2], kind[2], axis=kind[1])
        parts[k] = lax.dynamic_update_index_in_dim(parts[k], own, chip, 0)
    half = core.astype(jnp.int32).reshape(1)
    totals = [sum_parts(half, part, f"sum_{n}") for n, part in zip(sh_names, parts)]
    sm_totals = sum_parts_small(chip.astype(jnp.int32).reshape(1), sm_parts, sm_sums, "sum_replicated")
    g_big, g_small = share_halves(totals, sm_totals, owners, "share_halves")
    outs = [dict(zip(sh_names, g_big)), {}, {}, {}]
    for n, g in zip(sh_names, g_big):
        outs[1][n], outs[2][n], outs[3][n] = adamw(g, w[n], m[n], v[n], f"adamw_{n}")
    small_results = [g_small[:-1]] + adamw_small(g_small[:-1], small_views(w), small_views(m), small_views(v),
                                                 "adamw_replicated")
    for j in range(4):
        for n, r in zip(REPLICATED, small_results[j]):
            outs[j][n] = r.reshape(-1)[:w[n].size].reshape(w[n].shape)
    return outs, g_small[-1][0, 0]
```

```python
import functools

import jax
import jax.numpy as jnp
from jax import lax
from jax.experimental import pallas as pl
from jax.experimental.pallas import tpu as pltpu

f32 = jnp.float32
bf16 = jnp.bfloat16

_MXU = bf16
_COMM = bf16
HIGH = lax.Precision.HIGH

D_MODEL = 1024
DEPTH = 2
D_PLE = 256
D_SSM = 256
D_DN = 512
D_SG = 256
SSM_GROUPS = 16
SSM_GROUP = 16
SSM_STATE = 64
N_STATE = SSM_GROUPS * SSM_STATE
DN_HEADS = 4
DN_HEAD_DIM = 128
DN_CONV = 4
DN_HALO = 16
DN_CHUNK = 64
SG_HEADS = 4
SG_HEAD_DIM = 64
SG_CHUNK = 128
S5_CHUNK = 1024
S5_GROUP_ROWS = 8
EPS = 1e-6
D_IN = 3336
D_IN_PAD = 3456
LANE = 128

ADAM_LR = 0.001
ADAM_B1 = 0.9
ADAM_B2 = 0.999
ADAM_EPS = 1e-08
ADAM_WD = 0.01
ADAM_STEP = 10

N_CHIPS = 4

Z_COLS = ((0, 512), (512, 2048), (2048, 2560), (2560, 3328), (3328, 3456))

GROUP_COLS = ((0, 512), (512, 2048), (2056, 2568), (2568, 3336), (2048, 2056))
SHARD_COLS = D_IN // 4

SHARDED = (("w_in", ("slot",)), ("ssm_w_glu", ("win", 0, 64)), ("dn_conv_w", ("win", 1, 384)),
           ("w_out", ("win", 0, 256)), ("w_ple_gate", ("win", 0, 256)), ("w_ple", ("win", 1, 256)))
REPLICATED = ("norm_g", "ssm_a_re", "ssm_a_im", "ssm_b_re", "ssm_b_im", "ssm_c_re", "ssm_c_im", "ssm_d",
              "ssm_log_step", "ssm_b_glu", "dn_a_log", "dn_dt_bias", "dn_norm_g", "sg_ln_g", "sg_ln_b", "sg_w",
              "sg_b", "ple_norm_g", "final_norm_g")
SMALL_OWNER = {n: int(n.startswith("ssm_")) for n in REPLICATED + ("loss",)}
WEIGHTS = ("norm_g", "w_in", "ssm_a_re", "ssm_a_im", "ssm_b_re", "ssm_b_im", "ssm_c_re", "ssm_c_im", "ssm_d",
           "ssm_log_step", "ssm_w_glu", "ssm_b_glu", "dn_conv_w", "dn_a_log", "dn_dt_bias", "dn_norm_g", "sg_ln_g",
           "sg_ln_b", "sg_w", "sg_b", "w_out", "ple_norm_g", "w_ple_gate", "w_ple", "final_norm_g")

VMEM_BIG = 56 * 1024 * 1024


def _mm(a, b):
    return jnp.dot(a.astype(_MXU), b.astype(_MXU), preferred_element_type=f32)


def _mm_nt(a, b):
    return lax.dot_general(a.astype(_MXU), b.astype(_MXU), (((1,), (1,)), ((), ())), preferred_element_type=f32)


def _mm_tn(a, b):
    return lax.dot_general(a.astype(_MXU), b.astype(_MXU), (((0,), (0,)), ((), ())), preferred_element_type=f32)


@jax.custom_vjp
def bdot(a, b):
    return _mm(a, b)


def _bdot_fwd(a, b):
    return _mm(a, b), (a, b)


def _bdot_bwd(res, g):
    a, b = res
    return _mm_nt(g, b).astype(a.dtype), _mm_tn(a, g).astype(b.dtype)


bdot.defvjp(_bdot_fwd, _bdot_bwd)


@jax.custom_vjp
def bdot_nt(a, b):
    return _mm_nt(a, b)


def _bdot_nt_fwd(a, b):
    return _mm_nt(a, b), (a, b)


def _bdot_nt_bwd(res, g):
    a, b = res
    return _mm(g, b).astype(a.dtype), _mm_tn(g, a).astype(b.dtype)


bdot_nt.defvjp(_bdot_nt_fwd, _bdot_nt_bwd)


@jax.custom_vjp
def bdot_tn(a, b):
    return _mm_tn(a, b)


def _bdot_tn_fwd(a, b):
    return _mm_tn(a, b), (a, b)


def _bdot_tn_bwd(res, g):
    a, b = res
    return _mm_nt(b, g).astype(a.dtype), _mm(a, g).astype(b.dtype)


bdot_tn.defvjp(_bdot_tn_fwd, _bdot_tn_bwd)


def hdot(a, b):
    return jnp.dot(a, b, precision=HIGH, preferred_element_type=f32)


def _unit_lower_inverses(ms):
    n = ms[0].shape[0]
    eye = (lax.broadcasted_iota(jnp.int32, (n, n), 0) == lax.broadcasted_iota(jnp.int32, (n, n), 1)).astype(f32)
    pw = [-m for m in ms]
    inv = [eye + p for p in pw]
    for _ in range(n.bit_length() - 2):
        pw = [hdot(p, p) for p in pw]
        inv = [a + hdot(a, p) for a, p in zip(inv, pw)]
    return inv


@jax.custom_vjp
def solve_unit_lower(ms, rhs, inv):
    return [hdot(a, r) for a, r in zip(inv, rhs)]


def _solve_unit_lower_fwd(ms, rhs, inv):
    xs = [hdot(a, r) for a, r in zip(inv, rhs)]
    return xs, (inv, xs)


def _solve_unit_lower_bwd(res, gs):
    inv, xs = res
    d_rhs = [lax.dot_general(a, g, (((0,), (0,)), ((), ())), precision=HIGH, preferred_element_type=f32)
             for a, g in zip(inv, gs)]
    d_ms = [-lax.dot_general(d, x, (((1,), (1,)), ((), ())), precision=HIGH, preferred_element_type=f32)
            for d, x in zip(d_rhs, xs)]
    return d_ms, d_rhs, [jnp.zeros_like(a) for a in inv]


solve_unit_lower.defvjp(_solve_unit_lower_fwd, _solve_unit_lower_bwd)


@functools.partial(jax.custom_vjp, nondiff_argnums=(1,))
def roll_rows(x, k):
    return pltpu.roll(x, k, 0)


def _roll_rows_fwd(x, k):
    return pltpu.roll(x, k, 0), None


def _roll_rows_bwd(k, _, g):
    return (pltpu.roll(g, g.shape[0] - k, 0),)


roll_rows.defvjp(_roll_rows_fwd, _roll_rows_bwd)


def _row_ids(shape):
    return lax.broadcasted_iota(jnp.int32, shape, 0)


def _rms(x, g):
    return x * lax.rsqrt(jnp.mean(x * x, axis=-1, keepdims=True) + EPS) * g


def _layer_norm(x, g, b):
    mu = jnp.mean(x, axis=-1, keepdims=True)
    xc = x - mu
    return xc * lax.rsqrt(jnp.mean(xc * xc, axis=-1, keepdims=True) + EPS) * g + b


def _s5_prep(are, aim, ls, bre, bim):
    step = jnp.exp(ls)
    mag = jnp.exp(are * step)
    lr = mag * jnp.cos(aim * step)
    li = mag * jnp.sin(aim * step)
    den = are * are + aim * aim
    nr, ni = lr - 1.0, li
    fr = (nr * are + ni * aim) / den
    fi = (ni * are - nr * aim) / den
    bbr = fr * bre - fi * bim
    bbi = fr * bim + fi * bre
    pr = jnp.broadcast_to(lr, (S5_GROUP_ROWS, N_STATE))
    pi = jnp.broadcast_to(li, (S5_GROUP_ROWS, N_STATE))
    d = 1
    while d < S5_GROUP_ROWS:
        keep = _row_ids(pr.shape) >= d
        sr, si = roll_rows(pr, d), roll_rows(pi, d)
        pr, pi = jnp.where(keep, pr * sr - pi * si, pr), jnp.where(keep, pr * si + pi * sr, pi)
        d *= 2
    return pr, pi, bbr, bbi


def _s5_chunk(u, gate, hr, hi, pr, pi, bbr, bbi, cr, ci, dr, wglu, bglu):
    n, steps = u.shape[0], S5_GROUP_ROWS
    groups = n // steps
    xr = bdot(u, bbr)
    xi = bdot(u, bbi)
    lr, li = pr[0:1], pi[0:1]
    rs, ims = [xr[:groups]], [xi[:groups]]
    for t in range(1, steps):
        a, b = rs[-1], ims[-1]
        rs.append(xr[t * groups:(t + 1) * groups] + lr * a - li * b)
        ims.append(xi[t * groups:(t + 1) * groups] + lr * b + li * a)
    er, ei = rs[-1], ims[-1]
    mr, mi = pr[steps - 1:steps], pi[steps - 1:steps]
    gid = _row_ids(er.shape)
    er, ei = (er + jnp.where(gid == 0, mr * hr - mi * hi, 0.0), ei + jnp.where(gid == 0, mr * hi + mi * hr, 0.0))
    d = 1
    while d < groups:
        sr = jnp.where(gid >= d, roll_rows(er, d), 0.0)
        si = jnp.where(gid >= d, roll_rows(ei, d), 0.0)
        er, ei = er + mr * sr - mi * si, ei + mr * si + mi * sr
        mr, mi = mr * mr - mi * mi, 2.0 * mr * mi
        d *= 2
    before_r = jnp.where(gid == 0, hr, roll_rows(er, 1))
    before_i = jnp.where(gid == 0, hi, roll_rows(ei, 1))
    xr = jnp.concatenate([rs[t] + pr[t:t + 1] * before_r - pi[t:t + 1] * before_i for t in range(steps)], axis=0)
    xi = jnp.concatenate([ims[t] + pr[t:t + 1] * before_i + pi[t:t + 1] * before_r for t in range(steps)], axis=0)
    y = bdot(xr, cr) - bdot(xi, ci) + dr * u
    y = jax.nn.gelu(y)
    y = y * jax.nn.sigmoid(bdot(y, wglu) + bglu)
    return y * jax.nn.silu(gate), er[groups - 1:groups], ei[groups - 1:groups]


def _dn_pre(xc, xp, w0, w1, w2, w3, is_start, col):
    xp = jnp.where(is_start, 0.0, xp)
    halo_rows = _row_ids(xp.shape)
    acc = w3 * xc
    for d, w in ((1, w2), (2, w1), (3, w0)):
        r = roll_rows(xc, d)
        head = jnp.where(halo_rows >= d, r[:DN_HALO], roll_rows(xp, d))
        acc = acc + w * jnp.concatenate([head, r[DN_HALO:]], axis=0)
    y = jax.nn.silu(acc)
    if col >= 2 * DN_HEADS:
        return y
    nrm = y * lax.rsqrt(jnp.sum(y * y, axis=-1, keepdims=True) + EPS)
    return nrm * DN_HEAD_DIM ** -0.5 if col < DN_HEADS else nrm


def _dn_local(qs, ks, vs, abs_, alog, dtb, invs=None):
    c = DN_CHUNK
    ri = lax.broadcasted_iota(jnp.int32, (c, c), 0)
    ci = lax.broadcasted_iota(jnp.int32, (c, c), 1)
    causal, strict = ri >= ci, ri > ci
    tril = causal.astype(f32)
    gcums = [hdot(tril, -jnp.exp(alog) * jax.nn.softplus(ab + dtb)) for ab in abs_]
    gcum_ts = [g.T for g in gcums]
    sigs = [jax.nn.sigmoid(ab) for ab in abs_]
    chains = [(j, h) for j in range(len(abs_)) for h in range(DN_HEADS)]
    gc = [gcums[j][:, h:h + 1] for j, h in chains]
    decay = [jnp.where(causal, jnp.exp(jnp.where(causal, gc[n] - gcum_ts[j][h:h + 1, :], 0.0)), 0.0)
             for n, (j, h) in enumerate(chains)]
    beta = [sigs[j][:, DN_HEADS + h:DN_HEADS + h + 1] for j, h in chains]
    kb = [ks[j][h] * beta[n] for n, (j, h) in enumerate(chains)]
    ms = [jnp.where(strict, bdot_nt(kb[n], ks[j][h]) * decay[n], 0.0) for n, (j, h) in enumerate(chains)]
    egc = [jnp.exp(g) for g in gc]
    rhs = [jnp.concatenate([vs[j][h] * beta[n], kb[n] * egc[n]], axis=1) for n, (j, h) in enumerate(chains)]
    inv = _unit_lower_inverses(ms) if invs is None else [invs[j][h] for j, h in chains]
    sol = solve_unit_lower(ms, rhs, inv)
    values = [s[:, :DN_HEAD_DIM] for s in sol]
    k_cds = [s[:, DN_HEAD_DIM:] for s in sol]
    attns = [bdot_nt(qs[j][h], ks[j][h]) * decay[n] for n, (j, h) in enumerate(chains)]
    q_decs = [qs[j][h] * egc[n] for n, (j, h) in enumerate(chains)]
    k_decs = [ks[j][h] * jnp.exp(gc[n][c - 1:c, :] - gc[n]) for n, (j, h) in enumerate(chains)]

    def nest(flat):
        return [flat[j * DN_HEADS:(j + 1) * DN_HEADS] for j in range(len(abs_))]

    lasts = [jnp.exp(g[c - 1:c, :]) for g in gcums]
    return nest(values), nest(k_cds), nest(attns), nest(q_decs), nest(k_decs), lasts, nest(inv)


def _dn_step(values, k_cds, attns, q_decs, k_decs, lasts, ggs, sts, ng):
    v_new = [v - bdot(kc, st) for v, kc, st in zip(values, k_cds, sts)]
    o = [bdot(qd, st) for qd, st in zip(q_decs, sts)]
    o = [a + bdot(at, vn) for a, at, vn in zip(o, attns, v_new)]
    new = [st * la + bdot_tn(kd, vn) for st, la, kd, vn in zip(sts, lasts, k_decs, v_new)]
    return [_rms(a, ng) * jax.nn.silu(g) for a, g in zip(o, ggs)], new


def _sg_chunk(u, v, gate, lng, lnb, ws, bt):
    n = SG_CHUNK
    ug = jax.nn.gelu(u)
    vn = _layer_norm(jax.nn.gelu(v), lng, lnb)
    causal = lax.broadcasted_iota(jnp.int32, (n, n), 0) >= lax.broadcasted_iota(jnp.int32, (n, n), 1)
    lane = lax.broadcasted_iota(jnp.int32, (n, D_SG), 1)
    s = jnp.zeros((n, D_SG), f32)
    for h in range(SG_HEADS):
        t = bdot(jnp.where(causal, ws[h], 0.0), vn) + bt[:, h:h + 1]
        s = s + jnp.where((lane >= h * SG_HEAD_DIM) & (lane < (h + 1) * SG_HEAD_DIM), t, 0.0)
    return ug * s * jax.nn.silu(gate)


def _cp(n_grid, vmem=None):
    return pltpu.CompilerParams(dimension_semantics=("arbitrary",) * n_grid, vmem_limit_bytes=vmem)


def _full(shape):
    nd = len(shape)
    return pl.BlockSpec(tuple(shape), lambda *_: (0,) * nd)


def _rows(tm, ncol):
    return pl.BlockSpec((tm, ncol), lambda i: (i, 0))


def _sds(shape, dtype=f32):
    return jax.ShapeDtypeStruct(tuple(shape), dtype)


def _acc(ref, val, first):
    @pl.when(first)
    def _():
        ref[...] = val

    @pl.when(jnp.logical_not(first))
    def _():
        ref[...] += val


def in_fwd(x, g, w, name):
    t, tm = x.shape[0], 512

    def body(x_ref, g_ref, w_ref, h_ref, *z_refs):
        h = _rms(x_ref[...], g_ref[...]).astype(_MXU)
        h_ref[...] = h
        for z_ref, (a, b) in zip(z_refs, Z_COLS):
            z_ref[...] = jnp.dot(h, w_ref[:, a:b], preferred_element_type=f32)

    widths = [b - a for a, b in Z_COLS]
    return pl.pallas_call(
        body, name=name, grid=(t // tm,),
        in_specs=[_rows(tm, D_MODEL), _full((1, D_MODEL)), _full((D_MODEL, D_IN_PAD))],
        out_specs=[_rows(tm, D_MODEL)] + [_rows(tm, n) for n in widths],
        out_shape=[_sds((t, D_MODEL), _MXU)] + [_sds((t, n)) for n in widths],
        compiler_params=_cp(1, VMEM_BIG),
    )(x, g, w)


def in_bwd(x, g, w, dzs, dres, name):
    t, tm = x.shape[0], 512
    widths = [b - a for a, b in Z_COLS]

    def body(x_ref, g_ref, w_ref, dres_ref, *rest):
        dz_refs, (dx_ref, dg_ref) = rest[:5], rest[5:]
        dh = jnp.zeros((tm, D_MODEL), f32)
        for dz_ref, (a, b) in zip(dz_refs, Z_COLS):
            dh = dh + _mm_nt(dz_ref[...], w_ref[:, a:b])
        _, vj = jax.vjp(_rms, x_ref[...], g_ref[...])
        dx, dg = vj(dh)
        dx_ref[...] = dres_ref[...] + dx
        _acc(dg_ref, dg, pl.program_id(0) == 0)

    return pl.pallas_call(
        body, name=name, grid=(t // tm,),
        in_specs=[_rows(tm, D_MODEL), _full((1, D_MODEL)), _full((D_MODEL, D_IN_PAD)), _rows(tm, D_MODEL)]
        + [_rows(tm, n) for n in widths],
        out_specs=[_rows(tm, D_MODEL), _full((1, D_MODEL))],
        out_shape=[_sds((t, D_MODEL)), _sds((1, D_MODEL))],
        compiler_params=_cp(1, VMEM_BIG),
    )(x, g, w, dres, *dzs)


def wgrad(a, g, name):
    t, k = a.shape
    n = g.shape[1]
    tm = min(t, 2048)
    tn = n if n <= 1536 else n // 2
    steps = t // tm

    def body(a_ref, g_ref, o_ref, acc):
        i = pl.program_id(1)
        _acc(acc, _mm_tn(a_ref[...], g_ref[...]), i == 0)

        @pl.when(i == steps - 1)
        def _():
            o_ref[...] = acc[...].astype(o_ref.dtype)

    return pl.pallas_call(
        body, name=name, grid=(n // tn, steps),
        in_specs=[pl.BlockSpec((tm, k), lambda j, i: (i, 0)), pl.BlockSpec((tm, tn), lambda j, i: (i, j))],
        out_specs=pl.BlockSpec((k, tn), lambda j, i: (0, j)),
        out_shape=_sds((k, n), _COMM),
        scratch_shapes=[pltpu.VMEM((k, tn), f32)],
        compiler_params=_cp(2, VMEM_BIG),
    )(a, g)


def post_fwd(x, ys, yd, yg, p, wout, pg, wgate, wple, name):
    t, tm = x.shape[0], 512

    def body(x_ref, ys_ref, yd_ref, yg_ref, p_ref, wout_ref, pg_ref, wgate_ref, wple_ref,
             x2_ref, x1_ref, y_ref, hn_ref):
        y = jnp.concatenate([ys_ref[...], yd_ref[...], yg_ref[...]], axis=1).astype(_MXU)
        y_ref[...] = y
        x1 = x_ref[...] + jnp.dot(y, wout_ref[...], preferred_element_type=f32)
        x1_ref[...] = x1
        hn = _rms(x1, pg_ref[...]).astype(_MXU)
        hn_ref[...] = hn
        gp = jnp.dot(hn, wgate_ref[...], preferred_element_type=f32)
        pp = _mm(p_ref[...], wple_ref[...])
        x2_ref[...] = x1 + jax.nn.sigmoid(gp) * pp

    return pl.pallas_call(
        body, name=name, grid=(t // tm,),
        in_specs=[_rows(tm, D_MODEL), _rows(tm, D_SSM), _rows(tm, D_DN), _rows(tm, D_SG), _rows(tm, D_PLE),
                  _full((D_MODEL, D_MODEL)), _full((1, D_MODEL)), _full((D_MODEL, D_MODEL)), _full((D_PLE, D_MODEL))],
        out_specs=[_rows(tm, D_MODEL)] * 4,
        out_shape=[_sds((t, D_MODEL)), _sds((t, D_MODEL)), _sds((t, D_MODEL), _MXU), _sds((t, D_MODEL), _MXU)],
        compiler_params=_cp(1, VMEM_BIG),
    )(x, ys, yd, yg, p, wout, pg, wgate, wple)


def post_bwd(dx2, x1, hn, p, wout, pg, wgate, wple, name):
    t, tm = dx2.shape[0], 512

    def body(dx2_ref, x1_ref, hn_ref, p_ref, wout_ref, pg_ref, wgate_ref, wple_ref,
             dx1_ref, dgp_ref, dpp_ref, dys_ref, dyd_ref, dyg_ref, dpg_ref):
        dx2 = dx2_ref[...]
        gp = jnp.dot(hn_ref[...], wgate_ref[...], preferred_element_type=f32)
        pp = _mm(p_ref[...], wple_ref[...])
        sg = jax.nn.sigmoid(gp)
        dpp_ref[...] = (dx2 * sg).astype(_MXU)
        dgp = (dx2 * pp * sg * (1.0 - sg)).astype(_MXU)
        dgp_ref[...] = dgp
        dhn = _mm_nt(dgp, wgate_ref[...])
        _, vj = jax.vjp(_rms, x1_ref[...], pg_ref[...])
        dx1n, dpg = vj(dhn)
        dx1 = dx2 + dx1n
        dx1_ref[...] = dx1
        dy = _mm_nt(dx1, wout_ref[...])
        dys_ref[...] = dy[:, :D_SSM]
        dyd_ref[...] = dy[:, D_SSM:D_SSM + D_DN]
        dyg_ref[...] = dy[:, D_SSM + D_DN:]
        _acc(dpg_ref, dpg, pl.program_id(0) == 0)

    return pl.pallas_call(
        body, name=name, grid=(t // tm,),
        in_specs=[_rows(tm, D_MODEL), _rows(tm, D_MODEL), _rows(tm, D_MODEL), _rows(tm, D_PLE),
                  _full((D_MODEL, D_MODEL)), _full((1, D_MODEL)), _full((D_MODEL, D_MODEL)), _full((D_PLE, D_MODEL))],
        out_specs=[_rows(tm, D_MODEL), _rows(tm, D_MODEL), _rows(tm, D_MODEL), _rows(tm, D_SSM), _rows(tm, D_DN),
                   _rows(tm, D_SG), _full((1, D_MODEL))],
        out_shape=[_sds((t, D_MODEL)), _sds((t, D_MODEL), _MXU), _sds((t, D_MODEL), _MXU), _sds((t, D_SSM)),
                   _sds((t, D_DN)), _sds((t, D_SG)), _sds((1, D_MODEL))],
        compiler_params=_cp(1, VMEM_BIG),
    )(dx2, x1, hn, p, wout, pg, wgate, wple)


def loss_fwd_bwd(x, fg, target, name):
    t, tm = x.shape[0], 1024

    def body(x_ref, fg_ref, t_ref, loss_ref, dx_ref, dfg_ref):
        def f(xv, gv):
            err = _rms(xv, gv) - t_ref[...]
            return 0.5 * jnp.sum(jnp.mean(err * err, axis=-1))

        val, vj = jax.vjp(f, x_ref[...], fg_ref[...])
        dx, dfg = vj(jnp.ones((), f32))
        dx_ref[...] = dx
        first = pl.program_id(0) == 0
        _acc(dfg_ref, dfg, first)
        _acc(loss_ref, jnp.full((8, LANE), val, f32), first)

    return pl.pallas_call(
        body, name=name, grid=(t // tm,),
        in_specs=[_rows(tm, D_MODEL), _full((1, D_MODEL)), _rows(tm, D_MODEL)],
        out_specs=[_full((8, LANE)), _rows(tm, D_MODEL), _full((1, D_MODEL))],
        out_shape=[_sds((8, LANE)), _sds((t, D_MODEL)), _sds((1, D_MODEL))],
        compiler_params=_cp(1),
    )(x, fg, target)


S5_PREPARED = 4
_S5_PARAM_SHAPES = ((S5_GROUP_ROWS, N_STATE), (S5_GROUP_ROWS, N_STATE), (D_SSM, N_STATE), (D_SSM, N_STATE),
                    (N_STATE, D_SSM), (N_STATE, D_SSM), (1, D_SSM), (D_SSM, D_SSM), (1, D_SSM))

def s5_prep_fwd(are, aim, ls, bre, bim, name):
    def body(are_ref, aim_ref, ls_ref, bre_ref, bim_ref, *outs):
        vals = _s5_prep(are_ref[...], aim_ref[...], ls_ref[...], bre_ref[...], bim_ref[...])
        for o, v in zip(outs, vals):
            o[...] = v

    return pl.pallas_call(body, name=name, out_shape=[_sds(s) for s in _S5_PARAM_SHAPES[:S5_PREPARED]])(
        are, aim, ls, bre, bim)


def s5_prep_bwd(are, aim, ls, bre, bim, cts, name):
    def body(are_ref, aim_ref, ls_ref, bre_ref, bim_ref, *rest):
        ct_refs, outs = rest[:S5_PREPARED], rest[S5_PREPARED:]
        _, vj = jax.vjp(_s5_prep, are_ref[...], aim_ref[...], ls_ref[...], bre_ref[...], bim_ref[...])
        for o, v in zip(outs, vj(tuple(r[...] for r in ct_refs))):
            o[...] = v

    shapes = [(1, N_STATE)] * 3 + [(D_SSM, N_STATE)] * 2
    return pl.pallas_call(body, name=name, out_shape=[_sds(s) for s in shapes])(are, aim, ls, bre, bim, *cts)


def _step_major(ref, cols):
    x = ref[:, cols]
    n, w = x.shape
    return jnp.swapaxes(x.reshape(n // S5_GROUP_ROWS, S5_GROUP_ROWS, w), 0, 1).reshape(n, w)


def _store_step_major(ref, cols, val):
    n, w = val.shape
    ref[:, cols] = jnp.swapaxes(val.reshape(S5_GROUP_ROWS, n // S5_GROUP_ROWS, w), 0, 1).reshape(n, w).astype(ref.dtype)


def s5_fwd(z, params, nb, name):
    t = z.shape[0]
    nc = t // nb // S5_CHUNK
    npar = len(_S5_PARAM_SHAPES)

    def body(z_ref, *rest):
        p_refs, (y_ref, hs_ref, hr_s, hi_s) = rest[:npar], rest[npar:]

        @pl.when(pl.program_id(1) == 0)
        def _():
            hr_s[...] = jnp.zeros_like(hr_s)
            hi_s[...] = jnp.zeros_like(hi_s)

        hr, hi = hr_s[...], hi_s[...]
        hs_ref[0, :, :N_STATE] = hr
        hs_ref[0, :, N_STATE:] = hi
        y, nhr, nhi = _s5_chunk(_step_major(z_ref, slice(0, D_SSM)), _step_major(z_ref, slice(D_SSM, 2 * D_SSM)),
                                hr, hi, *[r[...] for r in p_refs])
        _store_step_major(y_ref, slice(0, D_SSM), y)
        hr_s[...] = nhr
        hi_s[...] = nhi

    return pl.pallas_call(
        body, name=name, grid=(nb, nc),
        in_specs=[pl.BlockSpec((S5_CHUNK, 2 * D_SSM), lambda b, c: (b * nc + c, 0))]
        + [_full(s) for s in _S5_PARAM_SHAPES],
        out_specs=[pl.BlockSpec((S5_CHUNK, D_SSM), lambda b, c: (b * nc + c, 0)),
                   pl.BlockSpec((1, 1, 2 * N_STATE), lambda b, c: (b * nc + c, 0, 0))],
        out_shape=[_sds((t, D_SSM)), _sds((nb * nc, 1, 2 * N_STATE))],
        scratch_shapes=[pltpu.VMEM((1, N_STATE), f32), pltpu.VMEM((1, N_STATE), f32)],
        compiler_params=_cp(2, VMEM_BIG),
    )(z, *params)


def s5_bwd(z, params, hs, dy, nb, name):
    t = z.shape[0]
    nc = t // nb // S5_CHUNK
    npar = len(_S5_PARAM_SHAPES)

    def body(z_ref, hs_ref, dy_ref, *rest):
        p_refs, dz_ref, dp_refs, (dhr_s, dhi_s) = rest[:npar], rest[npar], rest[npar + 1:2 * npar + 1], rest[2 * npar + 1:]

        @pl.when(pl.program_id(1) == 0)
        def _():
            dhr_s[...] = jnp.zeros_like(dhr_s)
            dhi_s[...] = jnp.zeros_like(dhi_s)

        prim = (_step_major(z_ref, slice(0, D_SSM)), _step_major(z_ref, slice(D_SSM, 2 * D_SSM)),
                hs_ref[0, :, :N_STATE], hs_ref[0, :, N_STATE:]) + tuple(r[...] for r in p_refs)
        _, vj = jax.vjp(_s5_chunk, *prim)
        cts = vj((_step_major(dy_ref, slice(0, D_SSM)), dhr_s[...], dhi_s[...]))
        _store_step_major(dz_ref, slice(0, D_SSM), cts[0])
        _store_step_major(dz_ref, slice(D_SSM, 2 * D_SSM), cts[1])
        dhr_s[...] = cts[2]
        dhi_s[...] = cts[3]
        first = (pl.program_id(0) == 0) & (pl.program_id(1) == 0)
        for r, v in zip(dp_refs, cts[4:]):
            _acc(r, v, first)

    rev = lambda b, c: (b * nc + nc - 1 - c, 0)
    return pl.pallas_call(
        body, name=name, grid=(nb, nc),
        in_specs=[pl.BlockSpec((S5_CHUNK, 2 * D_SSM), rev),
                  pl.BlockSpec((1, 1, 2 * N_STATE), lambda b, c: (b * nc + nc - 1 - c, 0, 0)),
                  pl.BlockSpec((S5_CHUNK, D_SSM), rev)] + [_full(s) for s in _S5_PARAM_SHAPES],
        out_specs=[pl.BlockSpec((S5_CHUNK, 2 * D_SSM), rev)] + [_full(s) for s in _S5_PARAM_SHAPES],
        out_shape=[_sds((t, 2 * D_SSM), _MXU)] + [_sds(s) for s in _S5_PARAM_SHAPES],
        scratch_shapes=[pltpu.VMEM((1, N_STATE), f32), pltpu.VMEM((1, N_STATE), f32)],
        compiler_params=_cp(2, VMEM_BIG),
    )(z, hs, dy, *params)


DN_PRE_ROWS = 512
DN_COLS = 3 * D_DN // LANE


def dn_pre_bwd(zq, convw, dqkv, seq, name):
    t, tb = zq.shape[0], DN_PRE_ROWS
    nrow = t // tb
    per_seq = seq // tb

    def body(xc_ref, xp_ref, w_ref, d_ref, dx_ref, dw_ref, carry):
        step = pl.program_id(0)
        i = nrow - 1 - step

        @pl.when(step == 0)
        def _():
            carry[...] = jnp.zeros_like(carry)

        for j in range(DN_COLS):
            cols = slice(j * LANE, (j + 1) * LANE)
            fn = functools.partial(_dn_pre, is_start=i % per_seq == 0, col=j)
            _, vj = jax.vjp(fn, xc_ref[:, cols], xp_ref[:, cols], w_ref[0:1, cols], w_ref[1:2, cols],
                            w_ref[2:3, cols], w_ref[3:4, cols])
            dxc, dxp, dw0, dw1, dw2, dw3 = vj(d_ref[:, cols])
            dx_ref[:tb - DN_HALO, cols] = dxc[:tb - DN_HALO].astype(_MXU)
            dx_ref[tb - DN_HALO:, cols] = (dxc[tb - DN_HALO:] + carry[:, cols]).astype(_MXU)
            carry[:, cols] = dxp
            for k, dw in enumerate((dw0, dw1, dw2, dw3)):
                @pl.when(step == 0)
                def _():
                    dw_ref[k:k + 1, cols] = dw

                @pl.when(step != 0)
                def _():
                    dw_ref[k:k + 1, cols] += dw

    rev = lambda s: (nrow - 1 - s, 0)
    return pl.pallas_call(
        body, name=name, grid=(nrow,),
        in_specs=[pl.BlockSpec((tb, 3 * D_DN), rev),
                  pl.BlockSpec((DN_HALO, 3 * D_DN),
                               lambda s: (jnp.maximum((nrow - 1 - s) * (tb // DN_HALO) - 1, 0), 0)),
                  _full((DN_CONV, 3 * D_DN)), pl.BlockSpec((tb, 3 * D_DN), rev)],
        out_specs=[pl.BlockSpec((tb, 3 * D_DN), rev), _full((DN_CONV, 3 * D_DN))],
        out_shape=[_sds((t, 3 * D_DN), _MXU), _sds((DN_CONV, 3 * D_DN))],
        scratch_shapes=[pltpu.VMEM((DN_HALO, 3 * D_DN), f32)],
        compiler_params=_cp(1, VMEM_BIG),
    )(zq, zq, convw, dqkv)


DN_LOCAL_CHUNKS = 4
DN_FRONT_CHUNKS = 8
DN_ATTN = DN_HEADS * DN_CHUNK


def _dn_heads(ref, rows, base=0):
    return [ref[rows, base + h * DN_HEAD_DIM:base + (h + 1) * DN_HEAD_DIM] for h in range(DN_HEADS)]


def dn_front_fwd(zq, convw, ab, alog, dtb, seq, name):
    t = zq.shape[0]
    c, n = DN_CHUNK, DN_FRONT_CHUNKS
    per_seq = seq // (n * c)

    def body(xc_ref, xp_ref, w_ref, ab_ref, alog_ref, dtb_ref,
             qkv_ref, val_ref, kcd_ref, attn_ref, qd_ref, kd_ref, el_ref, inv_ref):
        is_start = pl.program_id(0) % per_seq == 0
        blocks = []
        for j in range(DN_COLS):
            cols = slice(j * LANE, (j + 1) * LANE)
            blocks.append(_dn_pre(xc_ref[:, cols], xp_ref[:, cols], w_ref[0:1, cols], w_ref[1:2, cols],
                                  w_ref[2:3, cols], w_ref[3:4, cols], is_start, j))
            qkv_ref[:, cols] = blocks[-1]
        rows = [pl.ds(j * c, c) for j in range(n)]

        def heads(base, j):
            return [blocks[base + h][j * c:(j + 1) * c] for h in range(DN_HEADS)]

        vals, kcds, attns, qds, kds, els, invs = _dn_local(
            [heads(0, j) for j in range(n)], [heads(DN_HEADS, j) for j in range(n)],
            [heads(2 * DN_HEADS, j) for j in range(n)], [ab_ref[r, :] for r in rows], alog_ref[...], dtb_ref[...])
        for j, r in enumerate(rows):
            for h in range(DN_HEADS):
                lo, hi = h * DN_HEAD_DIM, (h + 1) * DN_HEAD_DIM
                val_ref[r, lo:hi] = vals[j][h]
                kcd_ref[r, lo:hi] = kcds[j][h].astype(_MXU)
                qd_ref[r, lo:hi] = qds[j][h].astype(_MXU)
                kd_ref[r, lo:hi] = kds[j][h].astype(_MXU)
                attn_ref[r, h * c:(h + 1) * c] = attns[j][h].astype(_MXU)
                inv_ref[r, h * c:(h + 1) * c] = invs[j][h]
            el_ref[j] = els[j]

    wide = _rows(n * c, D_DN)
    outs = pl.pallas_call(
        body, name=name, grid=(t // (n * c),),
        in_specs=[_rows(n * c, 3 * D_DN),
                  pl.BlockSpec((DN_HALO, 3 * D_DN), lambda i: (jnp.maximum(i * (n * c // DN_HALO) - 1, 0), 0)),
                  _full((DN_CONV, 3 * D_DN)), _rows(n * c, LANE), _full((1, LANE)), _full((1, LANE))],
        out_specs=[_rows(n * c, 3 * D_DN), wide, wide, _rows(n * c, DN_ATTN), wide, wide,
                   pl.BlockSpec((n, 1, LANE), lambda i: (i, 0, 0)), _rows(n * c, DN_ATTN)],
        out_shape=[_sds((t, 3 * D_DN)), _sds((t, D_DN)), _sds((t, D_DN), _MXU), _sds((t, DN_ATTN), _MXU),
                   _sds((t, D_DN), _MXU), _sds((t, D_DN), _MXU), _sds((t // c, 1, LANE)), _sds((t, DN_ATTN))],
        compiler_params=_cp(1, VMEM_BIG),
    )(zq, zq, convw, ab, alog, dtb)
    return outs[0], outs[1:7], outs[7]


def dn_local_bwd(qkv, ab, alog, dtb, inv, cts, name):
    t = qkv.shape[0]
    c, n = DN_CHUNK, DN_LOCAL_CHUNKS

    def body(qkv_ref, ab_ref, alog_ref, dtb_ref, inv_ref, dval_ref, dkcd_ref, dattn_ref, dqd_ref, dkd_ref, del_ref,
             dqkv_ref, dab_ref, dalog_ref, ddtb_ref):
        rows = [pl.ds(j * c, c) for j in range(n)]
        invs = [[inv_ref[r, h * c:(h + 1) * c] for h in range(DN_HEADS)] for r in rows]

        def local(qs, ks, vs, abs_, alog, dtb):
            return _dn_local(qs, ks, vs, abs_, alog, dtb, invs)[:6]

        _, vj = jax.vjp(local, [_dn_heads(qkv_ref, r) for r in rows], [_dn_heads(qkv_ref, r, D_DN) for r in rows],
                        [_dn_heads(qkv_ref, r, 2 * D_DN) for r in rows], [ab_ref[r, :] for r in rows], alog_ref[...],
                        dtb_ref[...])
        dattn = [[dattn_ref[r, h * c:(h + 1) * c] for h in range(DN_HEADS)] for r in rows]
        dq, dk, dv, dab, dalog, ddtb = vj(([_dn_heads(dval_ref, r) for r in rows], [_dn_heads(dkcd_ref, r) for r in rows],
                                           dattn, [_dn_heads(dqd_ref, r) for r in rows],
                                           [_dn_heads(dkd_ref, r) for r in rows], [del_ref[j] for j in range(n)]))
        for j, r in enumerate(rows):
            for h in range(DN_HEADS):
                lo, hi = h * DN_HEAD_DIM, (h + 1) * DN_HEAD_DIM
                dqkv_ref[r, lo:hi] = dq[j][h]
                dqkv_ref[r, D_DN + lo:D_DN + hi] = dk[j][h]
                dqkv_ref[r, 2 * D_DN + lo:2 * D_DN + hi] = dv[j][h]
            dab_ref[r, :] = dab[j].astype(_MXU)
        first = pl.program_id(0) == 0
        _acc(dalog_ref, dalog, first)
        _acc(ddtb_ref, ddtb, first)

    wide = _rows(n * c, D_DN)
    return pl.pallas_call(
        body, name=name, grid=(t // (n * c),),
        in_specs=[_rows(n * c, 3 * D_DN), _rows(n * c, LANE), _full((1, LANE)), _full((1, LANE)),
                  _rows(n * c, DN_ATTN), wide, wide, _rows(n * c, DN_ATTN), wide, wide,
                  pl.BlockSpec((n, 1, LANE), lambda i: (i, 0, 0))],
        out_specs=[_rows(n * c, 3 * D_DN), _rows(n * c, LANE), _full((1, LANE)), _full((1, LANE))],
        out_shape=[_sds((t, 3 * D_DN)), _sds((t, LANE), _MXU), _sds((1, LANE)), _sds((1, LANE))],
        compiler_params=_cp(1),
    )(qkv, ab, alog, dtb, inv, *cts)


def _seq_view(a, nb):
    return a.reshape((nb, a.shape[0] // nb) + a.shape[1:])


def _dn_chains(nb):
    return [(b, h) for b in range(nb) for h in range(DN_HEADS)]


DN_SCAN_CHUNKS = 4
DN_SCAN_FWD_CHUNKS = 8


def _dn_step_operands(val_ref, kcd_ref, attn_ref, qd_ref, kd_ref, el_ref, gg_ref, nb, j):
    chains = _dn_chains(nb)
    c = DN_CHUNK
    rows = pl.ds(j * c, c)

    def wide(ref):
        return [ref[b, rows, h * DN_HEAD_DIM:(h + 1) * DN_HEAD_DIM].astype(f32) for b, h in chains]

    attns = [attn_ref[b, rows, h * c:(h + 1) * c].astype(f32) for b, h in chains]
    return (wide(val_ref), wide(kcd_ref), attns, wide(qd_ref), wide(kd_ref),
            [el_ref[b, j, :, h:h + 1] for b, h in chains], wide(gg_ref))


def dn_scan_fwd(loc, gg, ng, nb, name):
    val, kcd, attn, qd, kd, el = loc
    t = val.shape[0]
    c, n = DN_CHUNK, DN_SCAN_FWD_CHUNKS
    nc = t // nb // c
    ns = nb * DN_HEADS

    steps = nc // n
    depth = 3

    def body(val_hbm, kcd_ref, attn_ref, qd_ref, kd_ref, el_ref, gg_hbm, ng_ref, y_ref, ss_ref, st, vbuf, gbuf, sems):
        k = pl.program_id(0)

        def loads(step, slot):
            rows = pl.ds(step * (n * c), n * c)
            return (pltpu.make_async_copy(val_hbm.at[:, rows, :], vbuf.at[slot], sems.at[0, slot]),
                    pltpu.make_async_copy(gg_hbm.at[:, rows, :], gbuf.at[slot], sems.at[1, slot]))

        @pl.when(k == 0)
        def _():
            st[...] = jnp.zeros_like(st)
            for s in range(min(depth, steps)):
                for cp in loads(s, s):
                    cp.start()

        slot = k % depth
        for cp in loads(k, slot):
            cp.wait()
        val_ref, gg_ref = vbuf.at[slot], gbuf.at[slot]
        sts = [st[i] for i in range(ns)]
        for j in range(n):
            for i in range(ns):
                ss_ref[j, i] = sts[i]
            ys, sts = _dn_step(*_dn_step_operands(val_ref, kcd_ref, attn_ref, qd_ref, kd_ref, el_ref, gg_ref, nb, j),
                               sts, ng_ref[...])
            for i, (b, h) in enumerate(_dn_chains(nb)):
                y_ref[b, pl.ds(j * c, c), h * DN_HEAD_DIM:(h + 1) * DN_HEAD_DIM] = ys[i]
        for i in range(ns):
            st[i] = sts[i]

        @pl.when(k + depth < steps)
        def _():
            for cp in loads(k + depth, slot):
                cp.start()

    def blk(w):
        return pl.BlockSpec((nb, n * c, w), lambda k: (0, k, 0))

    hbm = pl.BlockSpec(memory_space=pl.ANY)
    el_spec = pl.BlockSpec((nb, n, 1, LANE), lambda k: (0, k, 0, 0))
    y, ss = pl.pallas_call(
        body, name=name, grid=(steps,),
        in_specs=[hbm, blk(D_DN), blk(DN_ATTN), blk(D_DN), blk(D_DN), el_spec, hbm, _full((1, LANE))],
        out_specs=[blk(D_DN), pl.BlockSpec((n, ns, DN_HEAD_DIM, DN_HEAD_DIM), lambda k: (k, 0, 0, 0))],
        out_shape=[_sds((nb, t // nb, D_DN)), _sds((nc, ns, DN_HEAD_DIM, DN_HEAD_DIM))],
        scratch_shapes=[pltpu.VMEM((ns, DN_HEAD_DIM, DN_HEAD_DIM), f32), pltpu.VMEM((depth, nb, n * c, D_DN), f32),
                        pltpu.VMEM((depth, nb, n * c, D_DN), f32), pltpu.SemaphoreType.DMA((2, depth))],
        compiler_params=_cp(1, VMEM_BIG),
    )(_seq_view(val, nb), _seq_view(kcd, nb), _seq_view(attn, nb), _seq_view(qd, nb), _seq_view(kd, nb),
      el.reshape(nb, nc, 1, LANE), _seq_view(gg, nb), ng)
    return y.reshape(t, D_DN), ss


def dn_scan_bwd(loc, gg, ng, ss, dy, nb, name):
    val, kcd, attn, qd, kd, el = loc
    t = val.shape[0]
    c, n = DN_CHUNK, DN_SCAN_CHUNKS
    nc = t // nb // c
    ns = nb * DN_HEADS
    steps = nc // n

    def body(val_ref, kcd_ref, attn_ref, qd_ref, kd_ref, el_ref, gg_ref, ng_ref, ss_ref, dy_ref,
             dval_ref, dkcd_ref, dattn_ref, dqd_ref, dkd_ref, del_ref, dgg_ref, dng_ref, dst):
        @pl.when(pl.program_id(0) == 0)
        def _():
            dst[...] = jnp.zeros_like(dst)

        lane = lax.broadcasted_iota(jnp.int32, (1, LANE), 1)
        chains = _dn_chains(nb)
        ds = [dst[i] for i in range(ns)]
        dng_tot = jnp.zeros((1, LANE), f32)
        for j in reversed(range(n)):
            rows = pl.ds(j * c, c)
            _, vj = jax.vjp(_dn_step,
                            *_dn_step_operands(val_ref, kcd_ref, attn_ref, qd_ref, kd_ref, el_ref, gg_ref, nb, j),
                            [ss_ref[j, i] for i in range(ns)], ng_ref[...])
            dys = [dy_ref[b, rows, h * DN_HEAD_DIM:(h + 1) * DN_HEAD_DIM] for b, h in chains]
            dval, dkcd, dattn, dqd, dkd, dlast, dgg, ds, dng = vj((dys, ds))
            dng_tot = dng_tot + dng
            del_rows = [jnp.zeros((1, LANE), f32) for _ in range(nb)]
            for i, (b, h) in enumerate(chains):
                cols = slice(h * DN_HEAD_DIM, (h + 1) * DN_HEAD_DIM)
                dval_ref[b, rows, cols] = dval[i]
                dkcd_ref[b, rows, cols] = dkcd[i]
                dattn_ref[b, rows, h * c:(h + 1) * c] = dattn[i]
                dqd_ref[b, rows, cols] = dqd[i]
                dkd_ref[b, rows, cols] = dkd[i]
                dgg_ref[b, rows, cols] = dgg[i].astype(_MXU)
                del_rows[b] = del_rows[b] + jnp.where(lane == h, dlast[i], 0.0)
            for b in range(nb):
                del_ref[b, j] = del_rows[b]
        for i in range(ns):
            dst[i] = ds[i]
        _acc(dng_ref, dng_tot, pl.program_id(0) == 0)

    def blk(w):
        return pl.BlockSpec((nb, n * c, w), lambda k: (0, steps - 1 - k, 0))

    el_spec = pl.BlockSpec((nb, n, 1, LANE), lambda k: (0, steps - 1 - k, 0, 0))
    outs = pl.pallas_call(
        body, name=name, grid=(steps,),
        in_specs=[blk(D_DN), blk(D_DN), blk(DN_ATTN), blk(D_DN), blk(D_DN), el_spec, blk(D_DN), _full((1, LANE)),
                  pl.BlockSpec((n, ns, DN_HEAD_DIM, DN_HEAD_DIM), lambda k: (steps - 1 - k, 0, 0, 0)), blk(D_DN)],
        out_specs=[blk(D_DN), blk(D_DN), blk(DN_ATTN), blk(D_DN), blk(D_DN), el_spec, blk(D_DN), _full((1, LANE))],
        out_shape=[_sds((nb, t // nb, D_DN)), _sds((nb, t // nb, D_DN)), _sds((nb, t // nb, DN_ATTN)),
                   _sds((nb, t // nb, D_DN)), _sds((nb, t // nb, D_DN)), _sds((nb, nc, 1, LANE)),
                   _sds((nb, t // nb, D_DN), _MXU), _sds((1, LANE))],
        scratch_shapes=[pltpu.VMEM((ns, DN_HEAD_DIM, DN_HEAD_DIM), f32)],
        compiler_params=_cp(1, VMEM_BIG),
    )(_seq_view(val, nb), _seq_view(kcd, nb), _seq_view(attn, nb), _seq_view(qd, nb), _seq_view(kd, nb),
      el.reshape(nb, nc, 1, LANE), _seq_view(gg, nb), ng, ss, _seq_view(dy, nb))
    dloc = [o.reshape((t,) + o.shape[2:]) for o in outs[:5]] + [outs[5].reshape(t // c, 1, LANE)]
    return dloc, outs[6].reshape(t, D_DN), outs[7]


SG_ROWS = 1024


def sg_fwd(z, lng, lnb, w, bt, name):
    t = z.shape[0]

    def body(z_ref, lng_ref, lnb_ref, w_ref, bt_ref, y_ref):
        ws = [w_ref[h] for h in range(SG_HEADS)]
        for k in range(SG_ROWS // SG_CHUNK):
            r = pl.ds(k * SG_CHUNK, SG_CHUNK)
            y_ref[r, :] = _sg_chunk(z_ref[r, :D_SG], z_ref[r, D_SG:2 * D_SG], z_ref[r, 2 * D_SG:], lng_ref[...],
                                    lnb_ref[...], ws, bt_ref[...])

    return pl.pallas_call(
        body, name=name, grid=(t // SG_ROWS,),
        in_specs=[_rows(SG_ROWS, 3 * D_SG), _full((1, D_SG)), _full((1, D_SG)),
                  _full((SG_HEADS, SG_CHUNK, SG_CHUNK)), _full((SG_CHUNK, LANE))],
        out_specs=_rows(SG_ROWS, D_SG),
        out_shape=_sds((t, D_SG)),
        compiler_params=_cp(1),
    )(z, lng, lnb, w, bt)


def sg_bwd(z, lng, lnb, w, bt, dy, name):
    t = z.shape[0]

    def body(z_ref, lng_ref, lnb_ref, w_ref, bt_ref, dy_ref, dz_ref, dlng_ref, dlnb_ref, dw_ref, dbt_ref):
        ws = [w_ref[h] for h in range(SG_HEADS)]
        tot = None
        for k in range(SG_ROWS // SG_CHUNK):
            r = pl.ds(k * SG_CHUNK, SG_CHUNK)
            _, vj = jax.vjp(_sg_chunk, z_ref[r, :D_SG], z_ref[r, D_SG:2 * D_SG], z_ref[r, 2 * D_SG:], lng_ref[...],
                            lnb_ref[...], ws, bt_ref[...])
            du, dv, dgate, dlng, dlnb, dws, dbt = vj(dy_ref[r, :])
            dz_ref[r, :D_SG] = du.astype(_MXU)
            dz_ref[r, D_SG:2 * D_SG] = dv.astype(_MXU)
            dz_ref[r, 2 * D_SG:] = dgate.astype(_MXU)
            part = [dlng, dlnb, dbt] + list(dws)
            tot = part if tot is None else [a + b for a, b in zip(tot, part)]
        first = pl.program_id(0) == 0
        _acc(dlng_ref, tot[0], first)
        _acc(dlnb_ref, tot[1], first)
        _acc(dbt_ref, tot[2], first)
        for h in range(SG_HEADS):
            @pl.when(first)
            def _():
                dw_ref[h] = tot[3 + h]

            @pl.when(jnp.logical_not(first))
            def _():
                dw_ref[h] += tot[3 + h]

    return pl.pallas_call(
        body, name=name, grid=(t // SG_ROWS,),
        in_specs=[_rows(SG_ROWS, 3 * D_SG), _full((1, D_SG)), _full((1, D_SG)),
                  _full((SG_HEADS, SG_CHUNK, SG_CHUNK)), _full((SG_CHUNK, LANE)), _rows(SG_ROWS, D_SG)],
        out_specs=[_rows(SG_ROWS, 3 * D_SG), _full((1, D_SG)), _full((1, D_SG)),
                   _full((SG_HEADS, SG_CHUNK, SG_CHUNK)), _full((SG_CHUNK, LANE))],
        out_shape=[_sds((t, 3 * D_SG), _MXU), _sds((1, D_SG)), _sds((1, D_SG)), _sds((SG_HEADS, SG_CHUNK, SG_CHUNK)),
                   _sds((SG_CHUNK, LANE))],
        compiler_params=_cp(1),
    )(z, lng, lnb, w, bt, dy)


def add_pairs(a_list, b_list, name):
    n = len(a_list)

    def body(*refs):
        for a_ref, b_ref, o_ref in zip(refs[:n], refs[n:2 * n], refs[2 * n:]):
            o_ref[...] = (a_ref[...].astype(f32) + b_ref[...].astype(f32)).astype(o_ref.dtype)

    return pl.pallas_call(
        body, name=name, out_shape=[_sds(a.shape, a.dtype) for a in a_list],
        compiler_params=pltpu.CompilerParams(vmem_limit_bytes=VMEM_BIG),
    )(*a_list, *b_list)


def _adamw(g, w, m, v):
    nm = ADAM_B1 * m + (1.0 - ADAM_B1) * g
    nv = ADAM_B2 * v + (1.0 - ADAM_B2) * jnp.square(g)
    m_hat = nm / (1.0 - ADAM_B1 ** ADAM_STEP)
    v_hat = nv / (1.0 - ADAM_B2 ** ADAM_STEP)
    return -ADAM_LR * (m_hat / (jnp.sqrt(v_hat) + ADAM_EPS) + ADAM_WD * w), nm, nv


def sum_parts(half, recv, name):
    _, r, c = recv.shape
    tr = 256 if r % 256 == 0 else r

    def body(half_ref, recv_ref, g_ref):
        g = recv_ref[0].astype(f32)
        for k in range(1, N_CHIPS):
            g = g + recv_ref[k].astype(f32)
        g_ref[...] = g

    return pl.pallas_call(
        body, name=name,
        grid_spec=pltpu.PrefetchScalarGridSpec(
            num_scalar_prefetch=1, grid=(r // tr,),
            in_specs=[pl.BlockSpec((N_CHIPS, tr, c), lambda i, h: (0, i, 0))],
            out_specs=pl.BlockSpec((None, tr, c), lambda i, h: (h[0], i, 0))),
        out_shape=_sds((2, r, c)),
        compiler_params=_cp(1, VMEM_BIG),
    )(half, recv)


def adamw(g, w, m, v, name):
    _, r, c = w.shape
    tr = 256 if r % 256 == 0 else r

    def body(g_ref, w_ref, m_ref, v_ref, d_ref, nm_ref, nv_ref):
        d_ref[...], nm_ref[...], nv_ref[...] = _adamw(g_ref[...], w_ref[...], m_ref[...], v_ref[...])

    blk = pl.BlockSpec((None, tr, c), lambda l, i: (l, i, 0))
    return pl.pallas_call(
        body, name=name, grid=(2, r // tr), in_specs=[blk] * 4, out_specs=[blk] * 3, out_shape=[_sds((2, r, c))] * 3,
        compiler_params=_cp(2, VMEM_BIG),
    )(g, w, m, v)


def sum_parts_small(chip, parts, sums, name):
    n = len(sums)

    def body(chip_ref, *refs):
        for part, own, out in zip(refs[:n], refs[n:2 * n], refs[2 * n:]):
            g = jnp.where(chip_ref[0] == 0, own[...], part[0])
            for q in range(1, N_CHIPS):
                g = g + jnp.where(chip_ref[0] == q, own[...], part[q])
            out[...] = g

    vmem = pl.BlockSpec(memory_space=pltpu.VMEM)
    return pl.pallas_call(
        body, name=name, in_specs=[pl.BlockSpec(memory_space=pltpu.SMEM)] + [vmem] * (2 * n), out_specs=[vmem] * n,
        out_shape=[_sds(s.shape) for s in sums], compiler_params=pltpu.CompilerParams(vmem_limit_bytes=VMEM_BIG),
    )(chip, *parts, *sums)


def adamw_small(gs, ws, ms, vs, name):
    n = len(ws)

    def body(*refs):
        ins, outs = refs[:4 * n], refs[4 * n:]
        for k in range(n):
            outs[k][...], outs[n + k][...], outs[2 * n + k][...] = _adamw(
                ins[k][...], ins[n + k][...], ins[2 * n + k][...], ins[3 * n + k][...])

    outs = pl.pallas_call(
        body, name=name, out_shape=[_sds(w.shape) for w in ws] * 3,
        compiler_params=pltpu.CompilerParams(vmem_limit_bytes=VMEM_BIG),
    )(*gs, *ws, *ms, *vs)
    return [outs[j * n:(j + 1) * n] for j in range(3)]


_ANY = pl.BlockSpec(memory_space=pltpu.HBM)
_MESH = pl.DeviceIdType.MESH


def _flip(v, bit):
    return 1 - v if bit else v


_CHIP_RELS = ((1, 0), (0, 1), (1, 1))


def _piece(ref, kind, q):
    if kind[0] == "slot":
        return ref.at[q]
    if kind[0] == "all":
        return ref
    _, axis, n = kind
    return ref.at[(slice(None),) * axis + (pl.ds(q * n, n),)]


def _piece_shape(shape, kind):
    if kind[0] == "slot":
        return tuple(shape[1:])
    if kind[0] == "all":
        return tuple(shape)
    _, axis, n = kind
    return tuple(shape[:axis]) + (n,) + tuple(shape[axis + 1:])


def gather_weights(shards, kinds, name):
    n = len(shards)

    def out_shape(s, kind):
        if kind[0] == "slot":
            return (N_CHIPS,) + tuple(s.shape)
        _, axis, w = kind
        return tuple(s.shape[:axis + 1]) + (N_CHIPS * w,) + tuple(s.shape[axis + 2:])

    def place(o_ref, kind, q, layer):
        if kind[0] == "slot":
            return o_ref.at[q, layer]
        return _piece(o_ref.at[layer], kind, q)

    def body(*refs):
        s_refs, o_refs = refs[:n], refs[n:2 * n]
        send_sems, recv_sems, fwd_send_sems, fwd_recv_sems = refs[2 * n:]
        x, y, c = lax.axis_index("x"), lax.axis_index("y"), lax.axis_index("c")
        mine = 2 * x + y
        sends, arrivals, forwards, fwd_arrivals = [], [], [], []
        for r, (fx, fy) in enumerate(_CHIP_RELS):
            px, py = _flip(x, fx), _flip(y, fy)
            peer = 2 * px + py
            for k in range(n):
                s = r * n + k
                sends.append(pltpu.make_async_remote_copy(
                    src_ref=s_refs[k].at[c], dst_ref=place(o_refs[k], kinds[k], mine, c), send_sem=send_sems.at[s],
                    recv_sem=recv_sems.at[s], device_id=(px, py, c), device_id_type=_MESH))
                arrivals.append(pltpu.make_async_remote_copy(
                    src_ref=s_refs[k].at[c], dst_ref=place(o_refs[k], kinds[k], peer, c), send_sem=send_sems.at[s],
                    recv_sem=recv_sems.at[s], device_id=(px, py, c), device_id_type=_MESH))
                block = place(o_refs[k], kinds[k], peer, c)
                forwards.append(pltpu.make_async_remote_copy(
                    src_ref=block, dst_ref=block, send_sem=fwd_send_sems.at[s], recv_sem=fwd_recv_sems.at[s],
                    device_id=(x, y, 1 - c), device_id_type=_MESH))
                other = place(o_refs[k], kinds[k], peer, 1 - c)
                fwd_arrivals.append(pltpu.make_async_remote_copy(
                    src_ref=other, dst_ref=other, send_sem=fwd_send_sems.at[s], recv_sem=fwd_recv_sems.at[s],
                    device_id=(x, y, 1 - c), device_id_type=_MESH))
        for cp in sends:
            cp.start()
        for arrived, fwd in zip(arrivals, forwards):
            arrived.wait_recv()
            fwd.start()
        for cp in fwd_arrivals:
            cp.wait_recv()
        for cp in sends + forwards:
            cp.wait_send()

    m = len(_CHIP_RELS) * n
    return pl.pallas_call(
        body, name=name, in_specs=[_ANY] * n, out_specs=[_ANY] * n,
        out_shape=[_sds(out_shape(s, k), s.dtype) for s, k in zip(shards, kinds)],
        scratch_shapes=[pltpu.SemaphoreType.DMA((m,))] * 4,
    )(*shards)


def _owned_by(owners, side):
    return [k for k, o in enumerate(owners) if o == side]


def exchange_halves(gs, smalls, owners, name):
    n, ns = len(gs), len(smalls)

    def body(*refs):
        g_refs, s_refs = refs[:n], refs[n:n + ns]
        got_refs, sgot_refs = refs[n + ns:2 * n + ns], refs[2 * n + ns:2 * (n + ns)]
        send_sems, recv_sems = refs[2 * (n + ns):]
        x, y, c = lax.axis_index("x"), lax.axis_index("y"), lax.axis_index("c")
        sibling = (x, y, 1 - c)
        swaps = [pltpu.make_async_remote_copy(
            src_ref=g_refs[k].at[1 - c], dst_ref=got_refs[k], send_sem=send_sems.at[k], recv_sem=recv_sems.at[k],
            device_id=sibling, device_id_type=_MESH) for k in range(n)]
        gives = [pltpu.make_async_remote_copy(
            src_ref=s_refs[k], dst_ref=sgot_refs[k], send_sem=send_sems.at[n + k], recv_sem=recv_sems.at[n + k],
            device_id=sibling, device_id_type=_MESH) for k in range(ns)]
        for cp in swaps:
            cp.start()
        for side in (0, 1):
            @pl.when(c == 1 - side)
            def _():
                for k in _owned_by(owners, side):
                    gives[k].start()
        for cp in swaps:
            cp.wait()
        for side in (0, 1):
            @pl.when(c == 1 - side)
            def _():
                for k in _owned_by(owners, side):
                    gives[k].wait_send()

            @pl.when(c == side)
            def _():
                for k in _owned_by(owners, side):
                    gives[k].wait_recv()

    outs = pl.pallas_call(
        body, name=name, in_specs=[_ANY] * (n + ns), out_specs=[_ANY] * (n + ns),
        out_shape=[_sds(g.shape[1:], g.dtype) for g in gs] + [_sds(s.shape, s.dtype) for s in smalls],
        scratch_shapes=[pltpu.SemaphoreType.DMA((n + ns,)), pltpu.SemaphoreType.DMA((n + ns,))],
    )(*gs, *smalls)
    return outs[:n], outs[n:]


def reduce_to_chips(ts, kinds, smalls, owners, name):
    n, ns = len(ts), len(smalls)

    def body(*refs):
        t_refs, s_refs = refs[:n], refs[n:n + ns]
        o_refs, so_refs = refs[n + ns:2 * n + ns], refs[2 * n + ns:2 * (n + ns)]
        send_sems, recv_sems = refs[2 * (n + ns):]
        x, y, c = lax.axis_index("x"), lax.axis_index("y"), lax.axis_index("c")
        mine = 2 * x + y
        sends, arrivals, small_sends, small_arrivals = [], [], [], []
        for r, (fx, fy) in enumerate(_CHIP_RELS):
            px, py = _flip(x, fx), _flip(y, fy)
            peer = 2 * px + py
            for k in range(n + ns):
                s = r * (n + ns) + k
                if k < n:
                    src, dst = _piece(t_refs[k], kinds[k], peer), o_refs[k]
                else:
                    src, dst = s_refs[k - n], so_refs[k - n]
                go = pltpu.make_async_remote_copy(
                    src_ref=src, dst_ref=dst.at[mine], send_sem=send_sems.at[s], recv_sem=recv_sems.at[s],
                    device_id=(px, py, c), device_id_type=_MESH)
                come = pltpu.make_async_remote_copy(
                    src_ref=src, dst_ref=dst.at[peer], send_sem=send_sems.at[s], recv_sem=recv_sems.at[s],
                    device_id=(px, py, c), device_id_type=_MESH)
                (sends if k < n else small_sends).append(go)
                (arrivals if k < n else small_arrivals).append(come)

        def owned(copies, side):
            return [cp for j, cp in enumerate(copies) if owners[j % ns] == side]

        for cp in sends:
            cp.start()
        for side in (0, 1):
            @pl.when(c == side)
            def _():
                for cp in owned(small_sends, side):
                    cp.start()
        for cp in arrivals:
            cp.wait_recv()
        for cp in sends:
            cp.wait_send()
        for side in (0, 1):
            @pl.when(c == side)
            def _():
                for cp in owned(small_arrivals, side):
                    cp.wait_recv()
                for cp in owned(small_sends, side):
                    cp.wait_send()

    m = len(_CHIP_RELS) * (n + ns)
    outs = pl.pallas_call(
        body, name=name, in_specs=[_ANY] * (n + ns), out_specs=[_ANY] * (n + ns),
        out_shape=[_sds((N_CHIPS,) + _piece_shape(t.shape, k), t.dtype) for t, k in zip(ts, kinds)]
        + [_sds((N_CHIPS,) + s.shape, s.dtype) for s in smalls],
        scratch_shapes=[pltpu.SemaphoreType.DMA((m,)), pltpu.SemaphoreType.DMA((m,))],
    )(*ts, *smalls)
    return outs[:n], outs[n:]


def share_halves(rs, smalls, owners, name):
    n, ns = len(rs), len(smalls)

    def body(*refs):
        o_refs, so_refs = refs[n + ns:2 * n + ns], refs[2 * n + ns:2 * (n + ns)]
        send_sems, recv_sems = refs[2 * (n + ns):]
        x, y, c = lax.axis_index("x"), lax.axis_index("y"), lax.axis_index("c")
        sibling = (x, y, 1 - c)
        swaps = [pltpu.make_async_remote_copy(
            src_ref=o_refs[k].at[c], dst_ref=o_refs[k].at[c], send_sem=send_sems.at[k], recv_sem=recv_sems.at[k],
            device_id=sibling, device_id_type=_MESH) for k in range(n)]
        arrivals = [pltpu.make_async_remote_copy(
            src_ref=o_refs[k].at[c], dst_ref=o_refs[k].at[1 - c], send_sem=send_sems.at[k], recv_sem=recv_sems.at[k],
            device_id=sibling, device_id_type=_MESH) for k in range(n)]
        gives = [pltpu.make_async_remote_copy(
            src_ref=so_refs[k], dst_ref=so_refs[k], send_sem=send_sems.at[n + k], recv_sem=recv_sems.at[n + k],
            device_id=sibling, device_id_type=_MESH) for k in range(ns)]
        for cp in swaps:
            cp.start()
        for side in (0, 1):
            @pl.when(c == side)
            def _():
                for k in _owned_by(owners, side):
                    gives[k].start()
        for cp in arrivals:
            cp.wait_recv()
        for cp in swaps:
            cp.wait_send()
        for side in (0, 1):
            @pl.when(c == side)
            def _():
                for k in _owned_by(owners, side):
                    gives[k].wait_send()

            @pl.when(c == 1 - side)
            def _():
                for k in _owned_by(owners, side):
                    gives[k].wait_recv()

    outs = pl.pallas_call(
        body, name=name, in_specs=[_ANY] * (n + ns), out_specs=[_ANY] * (n + ns),
        out_shape=[_sds(r.shape, r.dtype) for r in list(rs) + list(smalls)],
        input_output_aliases={k: k for k in range(n + ns)},
        scratch_shapes=[pltpu.SemaphoreType.DMA((n + ns,)), pltpu.SemaphoreType.DMA((n + ns,))],
    )(*rs, *smalls)
    return outs[:n], outs[n:]


def _small_view(a):
    if a.size < 8 * LANE:
        return jnp.pad(a.reshape(-1), (0, 8 * LANE - a.size)).reshape(8, LANE)
    if a.ndim == 1:
        return a.reshape(1, a.shape[0])
    if a.ndim == 4 and a.shape[-1] < LANE:
        return a.reshape(a.shape[0], a.shape[1], a.shape[2] * a.shape[3])
    return a


def _permuted_from_shards(shards):
    parts = []
    for lo, hi in GROUP_COLS:
        for q in range(N_CHIPS):
            a, b = max(lo, q * SHARD_COLS), min(hi, (q + 1) * SHARD_COLS)
            if a < b:
                parts.append(shards[q][..., a - q * SHARD_COLS:b - q * SHARD_COLS])
    pad = jnp.zeros(shards[0].shape[:-1] + (D_IN_PAD - D_IN,), shards[0].dtype)
    return jnp.concatenate(parts + [pad], axis=-1)


def _shards_from_groups(groups):
    in_order = sorted(range(len(GROUP_COLS)), key=lambda j: GROUP_COLS[j][0])
    shards = []
    for q in range(N_CHIPS):
        parts = []
        for j in in_order:
            lo, hi = GROUP_COLS[j]
            a, b = max(lo, q * SHARD_COLS), min(hi, (q + 1) * SHARD_COLS)
            if a < b:
                parts.append(groups[j][..., a - lo:b - lo])
        shards.append(jnp.concatenate(parts, axis=-1))
    return shards


def _expand_b(b):
    eye = jnp.eye(SSM_GROUPS, dtype=b.dtype)
    return jnp.einsum("gnc,gh->gchn", b, eye).reshape(D_SSM, N_STATE)


def _extract_b(e):
    return jnp.einsum("gcgn->gnc", e.reshape(SSM_GROUPS, SSM_GROUP, SSM_GROUPS, SSM_STATE))


def _expand_c(c):
    eye = jnp.eye(SSM_GROUPS, dtype=c.dtype)
    return jnp.einsum("gcn,gh->gnhc", c, eye).reshape(N_STATE, D_SSM)


def _extract_c(e):
    return jnp.einsum("gngc->gcn", e.reshape(SSM_GROUPS, SSM_STATE, SSM_GROUPS, SSM_GROUP))


def _lane_row(v):
    return jnp.pad(v, (0, LANE - v.shape[0])).reshape(1, LANE)


def _layer_params(w, l):
    return dict(
        norm_g=w["norm_g"][l][None], win=w["w_in_perm"][l], wout=w["w_out"][l].astype(_MXU),
        pg=w["ple_norm_g"][l][None], wgate=w["w_ple_gate"][l].astype(_MXU), wple=w["w_ple"][l].astype(_MXU),
        are=w["ssm_a_re"][l].reshape(1, N_STATE), aim=w["ssm_a_im"][l].reshape(1, N_STATE),
        ls=jnp.repeat(w["ssm_log_step"][l], SSM_STATE).reshape(1, N_STATE),
        bre=_expand_b(w["ssm_b_re"][l]), bim=_expand_b(w["ssm_b_im"][l]),
        cr=_expand_c(w["ssm_c_re"][l]), ci=_expand_c(w["ssm_c_im"][l]),
        dr=w["ssm_d"][l].reshape(1, D_SSM), wglu=w["ssm_w_glu"][l].astype(f32), bglu=w["ssm_b_glu"][l][None],
        convw=w["dn_conv_w"][l], alog=_lane_row(w["dn_a_log"][l]), dtb=_lane_row(w["dn_dt_bias"][l]),
        ng=w["dn_norm_g"][l][None],
        lng=w["sg_ln_g"][l][None], lnb=w["sg_ln_b"][l][None], sgw=w["sg_w"][l],
        bt=jnp.pad(w["sg_b"][l].T, ((0, 0), (0, LANE - SG_HEADS))),
    )


def _layer_fwd(x, p, lp, nb, tag):
    seq = x.shape[0] // nb
    h, zs, zq, zg, zsg, zab = in_fwd(x, lp["norm_g"], lp["win"], f"in_fwd{tag}")
    prep = s5_prep_fwd(lp["are"], lp["aim"], lp["ls"], lp["bre"], lp["bim"], f"s5_prep_fwd{tag}")
    s5p = tuple(prep) + (lp["cr"], lp["ci"], lp["dr"], lp["wglu"], lp["bglu"])
    ys, hs = s5_fwd(zs, s5p, nb, f"s5_fwd{tag}")
    qkv, loc, inv = dn_front_fwd(zq, lp["convw"], zab, lp["alog"], lp["dtb"], seq, f"dn_front_fwd{tag}")
    yd, ss = dn_scan_fwd(loc, zg, lp["ng"], nb, f"dn_scan_fwd{tag}")
    yg = sg_fwd(zsg, lp["lng"], lp["lnb"], lp["sgw"], lp["bt"], f"sg_fwd{tag}")
    x2, x1, y, hn = post_fwd(x, ys, yd, yg, p, lp["wout"], lp["pg"], lp["wgate"], lp["wple"], f"post_fwd{tag}")
    saved = dict(x=x, h=h, zs=zs, zq=zq, zg=zg, zsg=zsg, zab=zab, s5p=s5p, hs=hs, qkv=qkv, loc=loc, inv=inv, ss=ss, x1=x1, y=y, hn=hn, p=p)
    return x2, saved


def _layer_bwd(dx2, sv, lp, nb, tag):
    seq = dx2.shape[0] // nb
    dx1, dgp, dpp, dys, dyd, dyg, dpg = post_bwd(dx2, sv["x1"], sv["hn"], sv["p"], lp["wout"], lp["pg"], lp["wgate"],
                                                 lp["wple"], f"post_bwd{tag}")
    g = {}
    g["w_out"] = wgrad(sv["y"], dx1, f"wgrad_out{tag}")
    g["w_ple_gate"] = wgrad(sv["hn"], dgp, f"wgrad_gate{tag}")
    g["w_ple"] = wgrad(sv["p"], dpp, f"wgrad_ple{tag}")
    g["ple_norm_g"] = dpg[0]
    dzsg, dlng, dlnb, dsgw, dbt = sg_bwd(sv["zsg"], lp["lng"], lp["lnb"], lp["sgw"], lp["bt"], dyg, f"sg_bwd{tag}")
    g["sg_ln_g"], g["sg_ln_b"], g["sg_w"], g["sg_b"] = dlng[0], dlnb[0], dsgw, dbt[:, :SG_HEADS].T
    dloc, dzg, dng = dn_scan_bwd(sv["loc"], sv["zg"], lp["ng"], sv["ss"], dyd, nb, f"dn_scan_bwd{tag}")
    dqkv, dzab, dalog, ddtb = dn_local_bwd(sv["qkv"], sv["zab"], lp["alog"], lp["dtb"], sv["inv"], dloc,
                                           f"dn_local_bwd{tag}")
    dzq, dconv = dn_pre_bwd(sv["zq"], lp["convw"], dqkv, seq, f"dn_pre_bwd{tag}")
    g["dn_conv_w"], g["dn_a_log"], g["dn_dt_bias"], g["dn_norm_g"] = dconv, dalog[0, :DN_HEADS], ddtb[0, :DN_HEADS], dng[0]
    s5out = s5_bwd(sv["zs"], sv["s5p"], sv["hs"], dys, nb, f"s5_bwd{tag}")
    dzs, dprep, (dcr, dci, ddr, dwglu, dbglu) = s5out[0], s5out[1:1 + S5_PREPARED], s5out[1 + S5_PREPARED:]
    dare, daim, dls, dbre, dbim = s5_prep_bwd(lp["are"], lp["aim"], lp["ls"], lp["bre"], lp["bim"], dprep,
                                              f"s5_prep_bwd{tag}")
    g["ssm_a_re"] = dare.reshape(SSM_GROUPS, SSM_STATE)
    g["ssm_a_im"] = daim.reshape(SSM_GROUPS, SSM_STATE)
    g["ssm_log_step"] = dls.reshape(SSM_GROUPS, SSM_STATE).sum(axis=1)
    g["ssm_b_re"], g["ssm_b_im"] = _extract_b(dbre), _extract_b(dbim)
    g["ssm_c_re"], g["ssm_c_im"] = _extract_c(dcr), _extract_c(dci)
    g["ssm_d"] = ddr.reshape(SSM_GROUPS, SSM_GROUP)
    g["ssm_w_glu"], g["ssm_b_glu"] = dwglu, dbglu[0]
    dzs_all = (dzs, dzq, dzg, dzsg, dzab)
    dx, dng_in = in_bwd(sv["x"], lp["norm_g"], lp["win"], dzs_all, dx1, f"in_bwd{tag}")
    g["w_in_pieces"] = [wgrad(sv["h"], dz, f"wgrad_in{k}{tag}") for k, dz in enumerate(dzs_all)]
    g["norm_g"] = dng_in[0]
    return dx, g


def _local_step(x, p, target, w, nb):
    lps = [_layer_params(w, l) for l in range(DEPTH)]
    saved = []
    for l in range(DEPTH):
        x, sv = _layer_fwd(x, p[l], lps[l], nb, f"_l{l}")
        saved.append(sv)
    loss_blk, dx, dfg = loss_fwd_bwd(x, w["final_norm_g"][None], target, "loss")
    grads = [None] * DEPTH
    for l in reversed(range(DEPTH)):
        dx, grads[l] = _layer_bwd(dx, saved[l], lps[l], nb, f"_l{l}")
    out = {k: jnp.stack([grads[l][k] for l in range(DEPTH)]) for k in grads[0] if k != "w_in_pieces"}
    out["w_in_pieces"] = [grads[l]["w_in_pieces"] for l in range(DEPTH)]
    out["final_norm_g"] = dfg[0]
    return loss_blk[0, 0], dx, out


def kernel(x, p, norm_g, w_in, ssm_a_re, ssm_a_im, ssm_b_re, ssm_b_im, ssm_c_re, ssm_c_im, ssm_d, ssm_log_step, ssm_w_glu, ssm_b_glu, dn_conv_w, dn_a_log, dn_dt_bias, dn_norm_g, sg_ln_g, sg_ln_b, sg_w, sg_b, w_out, ple_norm_g, w_ple_gate, w_ple, final_norm_g, loss_target, m_norm_g, m_w_in, m_ssm_a_re, m_ssm_a_im, m_ssm_b_re, m_ssm_b_im, m_ssm_c_re, m_ssm_c_im, m_ssm_d, m_ssm_log_step, m_ssm_w_glu, m_ssm_b_glu, m_dn_conv_w, m_dn_a_log, m_dn_dt_bias, m_dn_norm_g, m_sg_ln_g, m_sg_ln_b, m_sg_w, m_sg_b, m_w_out, m_ple_norm_g, m_w_ple_gate, m_w_ple, m_final_norm_g, v_norm_g, v_w_in, v_ssm_a_re, v_ssm_a_im, v_ssm_b_re, v_ssm_b_im, v_ssm_c_re, v_ssm_c_im, v_ssm_d, v_ssm_log_step, v_ssm_w_glu, v_ssm_b_glu, v_dn_conv_w, v_dn_a_log, v_dn_dt_bias, v_dn_norm_g, v_sg_ln_g, v_sg_ln_b, v_sg_w, v_sg_b, v_w_out, v_ple_norm_g, v_w_ple_gate, v_w_ple, v_final_norm_g):
    args = locals()
    w = {n: args[n] for n in WEIGHTS}
    m = {n: args["m_" + n] for n in WEIGHTS}
    v = {n: args["v_" + n] for n in WEIGHTS}
    nb, seq = x.shape[0], x.shape[1]
    t = nb * seq

    full = _gather_full(w)
    loss_local, dx, grads = _local_step(x.reshape(t, D_MODEL), p.reshape(DEPTH, t, D_PLE),
                                        loss_target.reshape(t, D_MODEL), full, nb)
    outs, loss = _reduce_and_update(grads, w, m, v, loss_local)
    return (loss, dx.reshape(nb, seq, D_MODEL), *[outs[0][n] for n in WEIGHTS], *[outs[1][n] for n in WEIGHTS],
            *[outs[2][n] for n in WEIGHTS], *[outs[3][n] for n in WEIGHTS])


def _gather_full(w):
    sh_names = [n for n, _ in SHARDED]
    shards = [w[n] if n == "dn_conv_w" else w[n].astype(_COMM) for n in sh_names]
    gathered = gather_weights(shards, [k for _, k in SHARDED], "gather_weights")
    chip = 2 * lax.axis_index("x") + lax.axis_index("y")
    full = {n: w[n] for n in REPLICATED}
    for (n, kind), shard, got in zip(SHARDED, shards, gathered):
        if kind[0] == "slot":
            full[n] = lax.dynamic_update_index_in_dim(got, shard, chip, 0)
        else:
            full[n] = lax.dynamic_update_slice_in_dim(got, shard, chip * kind[2], axis=kind[1] + 1)
    slots = full.pop("w_in")
    full["w_in_perm"] = _permuted_from_shards([slots[q] for q in range(N_CHIPS)]).astype(_MXU)
    return full


def _reduce_and_update(grads, w, m, v, loss_local):
    sh_names = [n for n, _ in SHARDED]
    sh_kinds = [k for _, k in SHARDED]
    owners = [SMALL_OWNER[n] for n in REPLICATED + ("loss",)]

    def small_views(d):
        return [_small_view(d[n]) for n in REPLICATED]

    grads["w_in"] = jnp.stack([jnp.stack(_shards_from_groups(pieces)) for pieces in grads["w_in_pieces"]])
    gs = [grads[n] if n == "dn_conv_w" else grads[n].astype(_COMM) for n in sh_names]
    sm = small_views(grads) + [_small_view(loss_local.reshape(1))]
    core = lax.axis_index("c")
    chip = 2 * lax.axis_index("x") + lax.axis_index("y")
    got, sm_got = exchange_halves(gs, sm, owners, "exchange_halves")
    sums = add_pairs([lax.dynamic_index_in_dim(g, core, 0, keepdims=False) for g in gs] + sm, list(got) + list(sm_got),
                     "add_halves")
    sums, sm_sums = sums[:len(gs)], sums[len(gs):]
    parts, sm_parts = reduce_to_chips(sums, sh_kinds, sm_sums, owners, "reduce_to_chips")
    parts = list(parts)
    for k, (kind, total) in enumerate(zip(sh_kinds, sums)):
        if kind[0] == "slot":
            own = lax.dynamic_index_in_dim(total, chip, 0, keepdims=False)
        else:
            own = lax.dynamic_slice_in_dim(total, chip * kind[2], kind[2], axis=kind[1])
        parts[k] = lax.dynamic_update_index_in_dim(parts[k], own, chip, 0)
    half = core.astype(jnp.int32).reshape(1)
    totals = [sum_parts(half, part, f"sum_{n}") for n, part in zip(sh_names, parts)]
    sm_totals = sum_parts_small(chip.astype(jnp.int32).reshape(1), sm_parts, sm_sums, "sum_replicated")
    g_big, g_small = share_halves(totals, sm_totals, owners, "share_halves")
    outs = [dict(zip(sh_names, g_big)), {}, {}, {}]
    for n, g in zip(sh_names, g_big):
        outs[1][n], outs[2][n], outs[3][n] = adamw(g, w[n], m[n], v[n], f"adamw_{n}")
    small_results = [g_small[:-1]] + adamw_small(g_small[:-1], small_views(w), small_views(m), small_views(v),
                                                 "adamw_replicated")
    for j in range(4):
        for n, r in zip(REPLICATED, small_results[j]):
            outs[j][n] = r.reshape(-1)[:w[n].size].reshape(w[n].shape)
    return outs, g_small[-1][0, 0]
```

```python
import functools

import jax
import jax.numpy as jnp
from jax import lax
from jax.experimental import pallas as pl
from jax.experimental.pallas import tpu as pltpu

f32 = jnp.float32
bf16 = jnp.bfloat16

_MXU = bf16
_COMM = bf16
HIGH = lax.Precision.HIGH

D_MODEL = 1024
DEPTH = 2
D_PLE = 256
D_SSM = 256
D_DN = 512
D_SG = 256
SSM_GROUPS = 16
SSM_GROUP = 16
SSM_STATE = 64
N_STATE = SSM_GROUPS * SSM_STATE
DN_HEADS = 4
DN_HEAD_DIM = 128
DN_CONV = 4
DN_HALO = 16
DN_CHUNK = 64
SG_HEADS = 4
SG_HEAD_DIM = 64
SG_CHUNK = 128
S5_CHUNK = 1024
S5_GROUP_ROWS = 8
EPS = 1e-6
D_IN = 3336
D_IN_PAD = 3456
LANE = 128

ADAM_LR = 0.001
ADAM_B1 = 0.9
ADAM_B2 = 0.999
ADAM_EPS = 1e-08
ADAM_WD = 0.01
ADAM_STEP = 10

N_CHIPS = 4

Z_COLS = ((0, 512), (512, 2048), (2048, 2560), (2560, 3328), (3328, 3456))

GROUP_COLS = ((0, 512), (512, 2048), (2056, 2568), (2568, 3336), (2048, 2056))
SHARD_COLS = D_IN // 4

SHARDED = (("w_in", ("slot",)), ("ssm_w_glu", ("win", 0, 64)), ("dn_conv_w", ("win", 1, 384)),
           ("w_out", ("win", 0, 256)), ("w_ple_gate", ("win", 0, 256)), ("w_ple", ("win", 1, 256)))
REPLICATED = ("norm_g", "ssm_a_re", "ssm_a_im", "ssm_b_re", "ssm_b_im", "ssm_c_re", "ssm_c_im", "ssm_d",
              "ssm_log_step", "ssm_b_glu", "dn_a_log", "dn_dt_bias", "dn_norm_g", "sg_ln_g", "sg_ln_b", "sg_w",
              "sg_b", "ple_norm_g", "final_norm_g")
SMALL_OWNER = {n: int(n.startswith("ssm_")) for n in REPLICATED + ("loss",)}
WEIGHTS = ("norm_g", "w_in", "ssm_a_re", "ssm_a_im", "ssm_b_re", "ssm_b_im", "ssm_c_re", "ssm_c_im", "ssm_d",
           "ssm_log_step", "ssm_w_glu", "ssm_b_glu", "dn_conv_w", "dn_a_log", "dn_dt_bias", "dn_norm_g", "sg_ln_g",
           "sg_ln_b", "sg_w", "sg_b", "w_out", "ple_norm_g", "w_ple_gate", "w_ple", "final_norm_g")

VMEM_BIG = 56 * 1024 * 1024


def _mm(a, b):
    return jnp.dot(a.astype(_MXU), b.astype(_MXU), preferred_element_type=f32)


def _mm_nt(a, b):
    return lax.dot_general(a.astype(_MXU), b.astype(_MXU), (((1,), (1,)), ((), ())), preferred_element_type=f32)


def _mm_tn(a, b):
    return lax.dot_general(a.astype(_MXU), b.astype(_MXU), (((0,), (0,)), ((), ())), preferred_element_type=f32)


@jax.custom_vjp
def bdot(a, b):
    return _mm(a, b)


def _bdot_fwd(a, b):
    return _mm(a, b), (a, b)


def _bdot_bwd(res, g):
    a, b = res
    return _mm_nt(g, b).astype(a.dtype), _mm_tn(a, g).astype(b.dtype)


bdot.defvjp(_bdot_fwd, _bdot_bwd)


@jax.custom_vjp
def bdot_nt(a, b):
    return _mm_nt(a, b)


def _bdot_nt_fwd(a, b):
    return _mm_nt(a, b), (a, b)


def _bdot_nt_bwd(res, g):
    a, b = res
    return _mm(g, b).astype(a.dtype), _mm_tn(g, a).astype(b.dtype)


bdot_nt.defvjp(_bdot_nt_fwd, _bdot_nt_bwd)


@jax.custom_vjp
def bdot_tn(a, b):
    return _mm_tn(a, b)


def _bdot_tn_fwd(a, b):
    return _mm_tn(a, b), (a, b)


def _bdot_tn_bwd(res, g):
    a, b = res
    return _mm_nt(b, g).astype(a.dtype), _mm(a, g).astype(b.dtype)


bdot_tn.defvjp(_bdot_tn_fwd, _bdot_tn_bwd)


def hdot(a, b):
    return jnp.dot(a, b, precision=HIGH, preferred_element_type=f32)


def _unit_lower_inverses(ms):
    n = ms[0].shape[0]
    eye = (lax.broadcasted_iota(jnp.int32, (n, n), 0) == lax.broadcasted_iota(jnp.int32, (n, n), 1)).astype(f32)
    pw = [-m for m in ms]
    inv = [eye + p for p in pw]
    for _ in range(n.bit_length() - 2):
        pw = [hdot(p, p) for p in pw]
        inv = [a + hdot(a, p) for a, p in zip(inv, pw)]
    return inv


@jax.custom_vjp
def solve_unit_lower(ms, rhs, inv):
    return [hdot(a, r) for a, r in zip(inv, rhs)]


def _solve_unit_lower_fwd(ms, rhs, inv):
    xs = [hdot(a, r) for a, r in zip(inv, rhs)]
    return xs, (inv, xs)


def _solve_unit_lower_bwd(res, gs):
    inv, xs = res
    d_rhs = [lax.dot_general(a, g, (((0,), (0,)), ((), ())), precision=HIGH, preferred_element_type=f32)
             for a, g in zip(inv, gs)]
    d_ms = [-lax.dot_general(d, x, (((1,), (1,)), ((), ())), precision=HIGH, preferred_element_type=f32)
            for d, x in zip(d_rhs, xs)]
    return d_ms, d_rhs, [jnp.zeros_like(a) for a in inv]


solve_unit_lower.defvjp(_solve_unit_lower_fwd, _solve_unit_lower_bwd)


@functools.partial(jax.custom_vjp, nondiff_argnums=(1,))
def roll_rows(x, k):
    return pltpu.roll(x, k, 0)


def _roll_rows_fwd(x, k):
    return pltpu.roll(x, k, 0), None


def _roll_rows_bwd(k, _, g):
    return (pltpu.roll(g, g.shape[0] - k, 0),)


roll_rows.defvjp(_roll_rows_fwd, _roll_rows_bwd)


def _row_ids(shape):
    return lax.broadcasted_iota(jnp.int32, shape, 0)


def _rms(x, g):
    return x * lax.rsqrt(jnp.mean(x * x, axis=-1, keepdims=True) + EPS) * g


def _layer_norm(x, g, b):
    mu = jnp.mean(x, axis=-1, keepdims=True)
    xc = x - mu
    return xc * lax.rsqrt(jnp.mean(xc * xc, axis=-1, keepdims=True) + EPS) * g + b


def _s5_prep(are, aim, ls, bre, bim):
    step = jnp.exp(ls)
    mag = jnp.exp(are * step)
    lr = mag * jnp.cos(aim * step)
    li = mag * jnp.sin(aim * step)
    den = are * are + aim * aim
    nr, ni = lr - 1.0, li
    fr = (nr * are + ni * aim) / den
    fi = (ni * are - nr * aim) / den
    bbr = fr * bre - fi * bim
    bbi = fr * bim + fi * bre
    pr = jnp.broadcast_to(lr, (S5_GROUP_ROWS, N_STATE))
    pi = jnp.broadcast_to(li, (S5_GROUP_ROWS, N_STATE))
    d = 1
    while d < S5_GROUP_ROWS:
        keep = _row_ids(pr.shape) >= d
        sr, si = roll_rows(pr, d), roll_rows(pi, d)
        pr, pi = jnp.where(keep, pr * sr - pi * si, pr), jnp.where(keep, pr * si + pi * sr, pi)
        d *= 2
    return pr, pi, bbr, bbi


def _s5_chunk(u, gate, hr, hi, pr, pi, bbr, bbi, cr, ci, dr, wglu, bglu):
    n, steps = u.shape[0], S5_GROUP_ROWS
    groups = n // steps
    xr = bdot(u, bbr)
    xi = bdot(u, bbi)
    lr, li = pr[0:1], pi[0:1]
    rs, ims = [xr[:groups]], [xi[:groups]]
    for t in range(1, steps):
        a, b = rs[-1], ims[-1]
        rs.append(xr[t * groups:(t + 1) * groups] + lr * a - li * b)
        ims.append(xi[t * groups:(t + 1) * groups] + lr * b + li * a)
    er, ei = rs[-1], ims[-1]
    mr, mi = pr[steps - 1:steps], pi[steps - 1:steps]
    gid = _row_ids(er.shape)
    er, ei = (er + jnp.where(gid == 0, mr * hr - mi * hi, 0.0), ei + jnp.where(gid == 0, mr * hi + mi * hr, 0.0))
    d = 1
    while d < groups:
        sr = jnp.where(gid >= d, roll_rows(er, d), 0.0)
        si = jnp.where(gid >= d, roll_rows(ei, d), 0.0)
        er, ei = er + mr * sr - mi * si, ei + mr * si + mi * sr
        mr, mi = mr * mr - mi * mi, 2.0 * mr * mi
        d *= 2
    before_r = jnp.where(gid == 0, hr, roll_rows(er, 1))
    before_i = jnp.where(gid == 0, hi, roll_rows(ei, 1))
    xr = jnp.concatenate([rs[t] + pr[t:t + 1] * before_r - pi[t:t + 1] * before_i for t in range(steps)], axis=0)
    xi = jnp.concatenate([ims[t] + pr[t:t + 1] * before_i + pi[t:t + 1] * before_r for t in range(steps)], axis=0)
    y = bdot(xr, cr) - bdot(xi, ci) + dr * u
    y = jax.nn.gelu(y)
    y = y * jax.nn.sigmoid(bdot(y, wglu) + bglu)
    return y * jax.nn.silu(gate), er[groups - 1:groups], ei[groups - 1:groups]


def _dn_pre(xc, xp, w0, w1, w2, w3, is_start, col):
    xp = jnp.where(is_start, 0.0, xp)
    halo_rows = _row_ids(xp.shape)
    acc = w3 * xc
    for d, w in ((1, w2), (2, w1), (3, w0)):
        r = roll_rows(xc, d)
        head = jnp.where(halo_rows >= d, r[:DN_HALO], roll_rows(xp, d))
        acc = acc + w * jnp.concatenate([head, r[DN_HALO:]], axis=0)
    y = jax.nn.silu(acc)
    if col >= 2 * DN_HEADS:
        return y
    nrm = y * lax.rsqrt(jnp.sum(y * y, axis=-1, keepdims=True) + EPS)
    return nrm * DN_HEAD_DIM ** -0.5 if col < DN_HEADS else nrm


def _dn_local(qs, ks, vs, abs_, alog, dtb, invs=None):
    c = DN_CHUNK
    ri = lax.broadcasted_iota(jnp.int32, (c, c), 0)
    ci = lax.broadcasted_iota(jnp.int32, (c, c), 1)
    causal, strict = ri >= ci, ri > ci
    tril = causal.astype(f32)
    gcums = [hdot(tril, -jnp.exp(alog) * jax.nn.softplus(ab + dtb)) for ab in abs_]
    gcum_ts = [g.T for g in gcums]
    sigs = [jax.nn.sigmoid(ab) for ab in abs_]
    chains = [(j, h) for j in range(len(abs_)) for h in range(DN_HEADS)]
    gc = [gcums[j][:, h:h + 1] for j, h in chains]
    decay = [jnp.where(causal, jnp.exp(jnp.where(causal, gc[n] - gcum_ts[j][h:h + 1, :], 0.0)), 0.0)
             for n, (j, h) in enumerate(chains)]
    beta = [sigs[j][:, DN_HEADS + h:DN_HEADS + h + 1] for j, h in chains]
    kb = [ks[j][h] * beta[n] for n, (j, h) in enumerate(chains)]
    ms = [jnp.where(strict, bdot_nt(kb[n], ks[j][h]) * decay[n], 0.0) for n, (j, h) in enumerate(chains)]
    egc = [jnp.exp(g) for g in gc]
    rhs = [jnp.concatenate([vs[j][h] * beta[n], kb[n] * egc[n]], axis=1) for n, (j, h) in enumerate(chains)]
    inv = _unit_lower_inverses(ms) if invs is None else [invs[j][h] for j, h in chains]
    sol = solve_unit_lower(ms, rhs, inv)
    values = [s[:, :DN_HEAD_DIM] for s in sol]
    k_cds = [s[:, DN_HEAD_DIM:] for s in sol]
    attns = [bdot_nt(qs[j][h], ks[j][h]) * decay[n] for n, (j, h) in enumerate(chains)]
    q_decs = [qs[j][h] * egc[n] for n, (j, h) in enumerate(chains)]
    k_decs = [ks[j][h] * jnp.exp(gc[n][c - 1:c, :] - gc[n]) for n, (j, h) in enumerate(chains)]

    def nest(flat):
        return [flat[j * DN_HEADS:(j + 1) * DN_HEADS] for j in range(len(abs_))]

    lasts = [jnp.exp(g[c - 1:c, :]) for g in gcums]
    return nest(values), nest(k_cds), nest(attns), nest(q_decs), nest(k_decs), lasts, nest(inv)


def _dn_step(values, k_cds, attns, q_decs, k_decs, lasts, ggs, sts, ng):
    v_new = [v - bdot(kc, st) for v, kc, st in zip(values, k_cds, sts)]
    o = [bdot(qd, st) for qd, st in zip(q_decs, sts)]
    o = [a + bdot(at, vn) for a, at, vn in zip(o, attns, v_new)]
    new = [st * la + bdot_tn(kd, vn) for st, la, kd, vn in zip(sts, lasts, k_decs, v_new)]
    return [_rms(a, ng) * jax.nn.silu(g) for a, g in zip(o, ggs)], new


def _sg_chunk(u, v, gate, lng, lnb, ws, bt):
    n = SG_CHUNK
    ug = jax.nn.gelu(u)
    vn = _layer_norm(jax.nn.gelu(v), lng, lnb)
    causal = lax.broadcasted_iota(jnp.int32, (n, n), 0) >= lax.broadcasted_iota(jnp.int32, (n, n), 1)
    lane = lax.broadcasted_iota(jnp.int32, (n, D_SG), 1)
    s = jnp.zeros((n, D_SG), f32)
    for h in range(SG_HEADS):
        t = bdot(jnp.where(causal, ws[h], 0.0), vn) + bt[:, h:h + 1]
        s = s + jnp.where((lane >= h * SG_HEAD_DIM) & (lane < (h + 1) * SG_HEAD_DIM), t, 0.0)
    return ug * s * jax.nn.silu(gate)


def _cp(n_grid, vmem=None):
    return pltpu.CompilerParams(dimension_semantics=("arbitrary",) * n_grid, vmem_limit_bytes=vmem)


def _full(shape):
    nd = len(shape)
    return pl.BlockSpec(tuple(shape), lambda *_: (0,) * nd)


def _rows(tm, ncol):
    return pl.BlockSpec((tm, ncol), lambda i: (i, 0))


def _sds(shape, dtype=f32):
    return jax.ShapeDtypeStruct(tuple(shape), dtype)


def _acc(ref, val, first):
    @pl.when(first)
    def _():
        ref[...] = val

    @pl.when(jnp.logical_not(first))
    def _():
        ref[...] += val


def in_fwd(x, g, w, name):
    t, tm = x.shape[0], 512

    def body(x_ref, g_ref, w_ref, h_ref, *z_refs):
        h = _rms(x_ref[...], g_ref[...]).astype(_MXU)
        h_ref[...] = h
        for z_ref, (a, b) in zip(z_refs, Z_COLS):
            z_ref[...] = jnp.dot(h, w_ref[:, a:b], preferred_element_type=f32)

    widths = [b - a for a, b in Z_COLS]
    return pl.pallas_call(
        body, name=name, grid=(t // tm,),
        in_specs=[_rows(tm, D_MODEL), _full((1, D_MODEL)), _full((D_MODEL, D_IN_PAD))],
        out_specs=[_rows(tm, D_MODEL)] + [_rows(tm, n) for n in widths],
        out_shape=[_sds((t, D_MODEL), _MXU)] + [_sds((t, n)) for n in widths],
        compiler_params=_cp(1, VMEM_BIG),
    )(x, g, w)


def in_bwd(x, g, w, dzs, dres, name):
    t, tm = x.shape[0], 512
    widths = [b - a for a, b in Z_COLS]

    def body(x_ref, g_ref, w_ref, dres_ref, *rest):
        dz_refs, (dx_ref, dg_ref) = rest[:5], rest[5:]
        dh = jnp.zeros((tm, D_MODEL), f32)
        for dz_ref, (a, b) in zip(dz_refs, Z_COLS):
            dh = dh + _mm_nt(dz_ref[...], w_ref[:, a:b])
        _, vj = jax.vjp(_rms, x_ref[...], g_ref[...])
        dx, dg = vj(dh)
        dx_ref[...] = dres_ref[...] + dx
        _acc(dg_ref, dg, pl.program_id(0) == 0)

    return pl.pallas_call(
        body, name=name, grid=(t // tm,),
        in_specs=[_rows(tm, D_MODEL), _full((1, D_MODEL)), _full((D_MODEL, D_IN_PAD)), _rows(tm, D_MODEL)]
        + [_rows(tm, n) for n in widths],
        out_specs=[_rows(tm, D_MODEL), _full((1, D_MODEL))],
        out_shape=[_sds((t, D_MODEL)), _sds((1, D_MODEL))],
        compiler_params=_cp(1, VMEM_BIG),
    )(x, g, w, dres, *dzs)


def wgrad(a, g, name):
    t, k = a.shape
    n = g.shape[1]
    tm = min(t, 2048)
    tn = n if n <= 1536 else n // 2
    steps = t // tm

    def body(a_ref, g_ref, o_ref, acc):
        i = pl.program_id(1)
        _acc(acc, _mm_tn(a_ref[...], g_ref[...]), i == 0)

        @pl.when(i == steps - 1)
        def _():
            o_ref[...] = acc[...].astype(o_ref.dtype)

    return pl.pallas_call(
        body, name=name, grid=(n // tn, steps),
        in_specs=[pl.BlockSpec((tm, k), lambda j, i: (i, 0)), pl.BlockSpec((tm, tn), lambda j, i: (i, j))],
        out_specs=pl.BlockSpec((k, tn), lambda j, i: (0, j)),
        out_shape=_sds((k, n), _COMM),
        scratch_shapes=[pltpu.VMEM((k, tn), f32)],
        compiler_params=_cp(2, VMEM_BIG),
    )(a, g)


def post_fwd(x, ys, yd, yg, p, wout, pg, wgate, wple, name):
    t, tm = x.shape[0], 512

    def body(x_ref, ys_ref, yd_ref, yg_ref, p_ref, wout_ref, pg_ref, wgate_ref, wple_ref,
             x2_ref, x1_ref, y_ref, hn_ref):
        y = jnp.concatenate([ys_ref[...], yd_ref[...], yg_ref[...]], axis=1).astype(_MXU)
        y_ref[...] = y
        x1 = x_ref[...] + jnp.dot(y, wout_ref[...], preferred_element_type=f32)
        x1_ref[...] = x1
        hn = _rms(x1, pg_ref[...]).astype(_MXU)
        hn_ref[...] = hn
        gp = jnp.dot(hn, wgate_ref[...], preferred_element_type=f32)
        pp = _mm(p_ref[...], wple_ref[...])
        x2_ref[...] = x1 + jax.nn.sigmoid(gp) * pp

    return pl.pallas_call(
        body, name=name, grid=(t // tm,),
        in_specs=[_rows(tm, D_MODEL), _rows(tm, D_SSM), _rows(tm, D_DN), _rows(tm, D_SG), _rows(tm, D_PLE),
                  _full((D_MODEL, D_MODEL)), _full((1, D_MODEL)), _full((D_MODEL, D_MODEL)), _full((D_PLE, D_MODEL))],
        out_specs=[_rows(tm, D_MODEL)] * 4,
        out_shape=[_sds((t, D_MODEL)), _sds((t, D_MODEL)), _sds((t, D_MODEL), _MXU), _sds((t, D_MODEL), _MXU)],
        compiler_params=_cp(1, VMEM_BIG),
    )(x, ys, yd, yg, p, wout, pg, wgate, wple)


def post_bwd(dx2, x1, hn, p, wout, pg, wgate, wple, name):
    t, tm = dx2.shape[0], 512

    def body(dx2_ref, x1_ref, hn_ref, p_ref, wout_ref, pg_ref, wgate_ref, wple_ref,
             dx1_ref, dgp_ref, dpp_ref, dys_ref, dyd_ref, dyg_ref, dpg_ref):
        dx2 = dx2_ref[...]
        gp = jnp.dot(hn_ref[...], wgate_ref[...], preferred_element_type=f32)
        pp = _mm(p_ref[...], wple_ref[...])
        sg = jax.nn.sigmoid(gp)
        dpp_ref[...] = (dx2 * sg).astype(_MXU)
        dgp = (dx2 * pp * sg * (1.0 - sg)).astype(_MXU)
        dgp_ref[...] = dgp
        dhn = _mm_nt(dgp, wgate_ref[...])
        _, vj = jax.vjp(_rms, x1_ref[...], pg_ref[...])
        dx1n, dpg = vj(dhn)
        dx1 = dx2 + dx1n
        dx1_ref[...] = dx1
        dy = _mm_nt(dx1, wout_ref[...])
        dys_ref[...] = dy[:, :D_SSM]
        dyd_ref[...] = dy[:, D_SSM:D_SSM + D_DN]
        dyg_ref[...] = dy[:, D_SSM + D_DN:]
        _acc(dpg_ref, dpg, pl.program_id(0) == 0)

    return pl.pallas_call(
        body, name=name, grid=(t // tm,),
        in_specs=[_rows(tm, D_MODEL), _rows(tm, D_MODEL), _rows(tm, D_MODEL), _rows(tm, D_PLE),
                  _full((D_MODEL, D_MODEL)), _full((1, D_MODEL)), _full((D_MODEL, D_MODEL)), _full((D_PLE, D_MODEL))],
        out_specs=[_rows(tm, D_MODEL), _rows(tm, D_MODEL), _rows(tm, D_MODEL), _rows(tm, D_SSM), _rows(tm, D_DN),
                   _rows(tm, D_SG), _full((1, D_MODEL))],
        out_shape=[_sds((t, D_MODEL)), _sds((t, D_MODEL), _MXU), _sds((t, D_MODEL), _MXU), _sds((t, D_SSM)),
                   _sds((t, D_DN)), _sds((t, D_SG)), _sds((1, D_MODEL))],
        compiler_params=_cp(1, VMEM_BIG),
    )(dx2, x1, hn, p, wout, pg, wgate, wple)


def loss_fwd_bwd(x, fg, target, name):
    t, tm = x.shape[0], 1024

    def body(x_ref, fg_ref, t_ref, loss_ref, dx_ref, dfg_ref):
        def f(xv, gv):
            err = _rms(xv, gv) - t_ref[...]
            return 0.5 * jnp.sum(jnp.mean(err * err, axis=-1))

        val, vj = jax.vjp(f, x_ref[...], fg_ref[...])
        dx, dfg = vj(jnp.ones((), f32))
        dx_ref[...] = dx
        first = pl.program_id(0) == 0
        _acc(dfg_ref, dfg, first)
        _acc(loss_ref, jnp.full((8, LANE), val, f32), first)

    return pl.pallas_call(
        body, name=name, grid=(t // tm,),
        in_specs=[_rows(tm, D_MODEL), _full((1, D_MODEL)), _rows(tm, D_MODEL)],
        out_specs=[_full((8, LANE)), _rows(tm, D_MODEL), _full((1, D_MODEL))],
        out_shape=[_sds((8, LANE)), _sds((t, D_MODEL)), _sds((1, D_MODEL))],
        compiler_params=_cp(1),
    )(x, fg, target)


S5_PREPARED = 4
_S5_PARAM_SHAPES = ((S5_GROUP_ROWS, N_STATE), (S5_GROUP_ROWS, N_STATE), (D_SSM, N_STATE), (D_SSM, N_STATE),
                    (N_STATE, D_SSM), (N_STATE, D_SSM), (1, D_SSM), (D_SSM, D_SSM), (1, D_SSM))

def s5_prep_fwd(are, aim, ls, bre, bim, name):
    def body(are_ref, aim_ref, ls_ref, bre_ref, bim_ref, *outs):
        vals = _s5_prep(are_ref[...], aim_ref[...], ls_ref[...], bre_ref[...], bim_ref[...])
        for o, v in zip(outs, vals):
            o[...] = v

    return pl.pallas_call(body, name=name, out_shape=[_sds(s) for s in _S5_PARAM_SHAPES[:S5_PREPARED]])(
        are, aim, ls, bre, bim)


def s5_prep_bwd(are, aim, ls, bre, bim, cts, name):
    def body(are_ref, aim_ref, ls_ref, bre_ref, bim_ref, *rest):
        ct_refs, outs = rest[:S5_PREPARED], rest[S5_PREPARED:]
        _, vj = jax.vjp(_s5_prep, are_ref[...], aim_ref[...], ls_ref[...], bre_ref[...], bim_ref[...])
        for o, v in zip(outs, vj(tuple(r[...] for r in ct_refs))):
            o[...] = v

    shapes = [(1, N_STATE)] * 3 + [(D_SSM, N_STATE)] * 2
    return pl.pallas_call(body, name=name, out_shape=[_sds(s) for s in shapes])(are, aim, ls, bre, bim, *cts)


def _step_major(ref, cols):
    x = ref[:, cols]
    n, w = x.shape
    return jnp.swapaxes(x.reshape(n // S5_GROUP_ROWS, S5_GROUP_ROWS, w), 0, 1).reshape(n, w)


def _store_step_major(ref, cols, val):
    n, w = val.shape
    ref[:, cols] = jnp.swapaxes(val.reshape(S5_GROUP_ROWS, n // S5_GROUP_ROWS, w), 0, 1).reshape(n, w).astype(ref.dtype)


def s5_fwd(z, params, nb, name):
    t = z.shape[0]
    nc = t // nb // S5_CHUNK
    npar = len(_S5_PARAM_SHAPES)

    def body(z_ref, *rest):
        p_refs, (y_ref, hs_ref, hr_s, hi_s) = rest[:npar], rest[npar:]

        @pl.when(pl.program_id(1) == 0)
        def _():
            hr_s[...] = jnp.zeros_like(hr_s)
            hi_s[...] = jnp.zeros_like(hi_s)

        hr, hi = hr_s[...], hi_s[...]
        hs_ref[0, :, :N_STATE] = hr
        hs_ref[0, :, N_STATE:] = hi
        y, nhr, nhi = _s5_chunk(_step_major(z_ref, slice(0, D_SSM)), _step_major(z_ref, slice(D_SSM, 2 * D_SSM)),
                                hr, hi, *[r[...] for r in p_refs])
        _store_step_major(y_ref, slice(0, D_SSM), y)
        hr_s[...] = nhr
        hi_s[...] = nhi

    return pl.pallas_call(
        body, name=name, grid=(nb, nc),
        in_specs=[pl.BlockSpec((S5_CHUNK, 2 * D_SSM), lambda b, c: (b * nc + c, 0))]
        + [_full(s) for s in _S5_PARAM_SHAPES],
        out_specs=[pl.BlockSpec((S5_CHUNK, D_SSM), lambda b, c: (b * nc + c, 0)),
                   pl.BlockSpec((1, 1, 2 * N_STATE), lambda b, c: (b * nc + c, 0, 0))],
        out_shape=[_sds((t, D_SSM)), _sds((nb * nc, 1, 2 * N_STATE))],
        scratch_shapes=[pltpu.VMEM((1, N_STATE), f32), pltpu.VMEM((1, N_STATE), f32)],
        compiler_params=_cp(2, VMEM_BIG),
    )(z, *params)


def s5_bwd(z, params, hs, dy, nb, name):
    t = z.shape[0]
    nc = t // nb // S5_CHUNK
    npar = len(_S5_PARAM_SHAPES)

    def body(z_ref, hs_ref, dy_ref, *rest):
        p_refs, dz_ref, dp_refs, (dhr_s, dhi_s) = rest[:npar], rest[npar], rest[npar + 1:2 * npar + 1], rest[2 * npar + 1:]

        @pl.when(pl.program_id(1) == 0)
        def _():
            dhr_s[...] = jnp.zeros_like(dhr_s)
            dhi_s[...] = jnp.zeros_like(dhi_s)

        prim = (_step_major(z_ref, slice(0, D_SSM)), _step_major(z_ref, slice(D_SSM, 2 * D_SSM)),
                hs_ref[0, :, :N_STATE], hs_ref[0, :, N_STATE:]) + tuple(r[...] for r in p_refs)
        _, vj = jax.vjp(_s5_chunk, *prim)
        cts = vj((_step_major(dy_ref, slice(0, D_SSM)), dhr_s[...], dhi_s[...]))
        _store_step_major(dz_ref, slice(0, D_SSM), cts[0])
        _store_step_major(dz_ref, slice(D_SSM, 2 * D_SSM), cts[1])
        dhr_s[...] = cts[2]
        dhi_s[...] = cts[3]
        first = (pl.program_id(0) == 0) & (pl.program_id(1) == 0)
        for r, v in zip(dp_refs, cts[4:]):
            _acc(r, v, first)

    rev = lambda b, c: (b * nc + nc - 1 - c, 0)
    return pl.pallas_call(
        body, name=name, grid=(nb, nc),
        in_specs=[pl.BlockSpec((S5_CHUNK, 2 * D_SSM), rev),
                  pl.BlockSpec((1, 1, 2 * N_STATE), lambda b, c: (b * nc + nc - 1 - c, 0, 0)),
                  pl.BlockSpec((S5_CHUNK, D_SSM), rev)] + [_full(s) for s in _S5_PARAM_SHAPES],
        out_specs=[pl.BlockSpec((S5_CHUNK, 2 * D_SSM), rev)] + [_full(s) for s in _S5_PARAM_SHAPES],
        out_shape=[_sds((t, 2 * D_SSM), _MXU)] + [_sds(s) for s in _S5_PARAM_SHAPES],
        scratch_shapes=[pltpu.VMEM((1, N_STATE), f32), pltpu.VMEM((1, N_STATE), f32)],
        compiler_params=_cp(2, VMEM_BIG),
    )(z, hs, dy, *params)


DN_PRE_ROWS = 512
DN_COLS = 3 * D_DN // LANE


def dn_pre_bwd(zq, convw, dqkv, seq, name):
    t, tb = zq.shape[0], DN_PRE_ROWS
    nrow = t // tb
    per_seq = seq // tb

    def body(xc_ref, xp_ref, w_ref, d_ref, dx_ref, dw_ref, carry):
        step = pl.program_id(0)
        i = nrow - 1 - step

        @pl.when(step == 0)
        def _():
            carry[...] = jnp.zeros_like(carry)

        for j in range(DN_COLS):
            cols = slice(j * LANE, (j + 1) * LANE)
            fn = functools.partial(_dn_pre, is_start=i % per_seq == 0, col=j)
            _, vj = jax.vjp(fn, xc_ref[:, cols], xp_ref[:, cols], w_ref[0:1, cols], w_ref[1:2, cols],
                            w_ref[2:3, cols], w_ref[3:4, cols])
            dxc, dxp, dw0, dw1, dw2, dw3 = vj(d_ref[:, cols])
            dx_ref[:tb - DN_HALO, cols] = dxc[:tb - DN_HALO].astype(_MXU)
            dx_ref[tb - DN_HALO:, cols] = (dxc[tb - DN_HALO:] + carry[:, cols]).astype(_MXU)
            carry[:, cols] = dxp
            for k, dw in enumerate((dw0, dw1, dw2, dw3)):
                @pl.when(step == 0)
                def _():
                    dw_ref[k:k + 1, cols] = dw

                @pl.when(step != 0)
                def _():
                    dw_ref[k:k + 1, cols] += dw

    rev = lambda s: (nrow - 1 - s, 0)
    return pl.pallas_call(
        body, name=name, grid=(nrow,),
        in_specs=[pl.BlockSpec((tb, 3 * D_DN), rev),
                  pl.BlockSpec((DN_HALO, 3 * D_DN),
                               lambda s: (jnp.maximum((nrow - 1 - s) * (tb // DN_HALO) - 1, 0), 0)),
                  _full((DN_CONV, 3 * D_DN)), pl.BlockSpec((tb, 3 * D_DN), rev)],
        out_specs=[pl.BlockSpec((tb, 3 * D_DN), rev), _full((DN_CONV, 3 * D_DN))],
        out_shape=[_sds((t, 3 * D_DN), _MXU), _sds((DN_CONV, 3 * D_DN))],
        scratch_shapes=[pltpu.VMEM((DN_HALO, 3 * D_DN), f32)],
        compiler_params=_cp(1, VMEM_BIG),
    )(zq, zq, convw, dqkv)


DN_LOCAL_CHUNKS = 4
DN_FRONT_CHUNKS = 8
DN_ATTN = DN_HEADS * DN_CHUNK


def _dn_heads(ref, rows, base=0):
    return [ref[rows, base + h * DN_HEAD_DIM:base + (h + 1) * DN_HEAD_DIM] for h in range(DN_HEADS)]


def dn_front_fwd(zq, convw, ab, alog, dtb, seq, name):
    t = zq.shape[0]
    c, n = DN_CHUNK, DN_FRONT_CHUNKS
    per_seq = seq // (n * c)

    def body(xc_ref, xp_ref, w_ref, ab_ref, alog_ref, dtb_ref,
             qkv_ref, val_ref, kcd_ref, attn_ref, qd_ref, kd_ref, el_ref, inv_ref):
        is_start = pl.program_id(0) % per_seq == 0
        blocks = []
        for j in range(DN_COLS):
            cols = slice(j * LANE, (j + 1) * LANE)
            blocks.append(_dn_pre(xc_ref[:, cols], xp_ref[:, cols], w_ref[0:1, cols], w_ref[1:2, cols],
                                  w_ref[2:3, cols], w_ref[3:4, cols], is_start, j))
            qkv_ref[:, cols] = blocks[-1]
        rows = [pl.ds(j * c, c) for j in range(n)]

        def heads(base, j):
            return [blocks[base + h][j * c:(j + 1) * c] for h in range(DN_HEADS)]

        vals, kcds, attns, qds, kds, els, invs = _dn_local(
            [heads(0, j) for j in range(n)], [heads(DN_HEADS, j) for j in range(n)],
            [heads(2 * DN_HEADS, j) for j in range(n)], [ab_ref[r, :] for r in rows], alog_ref[...], dtb_ref[...])
        for j, r in enumerate(rows):
            for h in range(DN_HEADS):
                lo, hi = h * DN_HEAD_DIM, (h + 1) * DN_HEAD_DIM
                val_ref[r, lo:hi] = vals[j][h]
                kcd_ref[r, lo:hi] = kcds[j][h].astype(_MXU)
                qd_ref[r, lo:hi] = qds[j][h].astype(_MXU)
                kd_ref[r, lo:hi] = kds[j][h].astype(_MXU)
                attn_ref[r, h * c:(h + 1) * c] = attns[j][h].astype(_MXU)
                inv_ref[r, h * c:(h + 1) * c] = invs[j][h]
            el_ref[j] = els[j]

    wide = _rows(n * c, D_DN)
    outs = pl.pallas_call(
        body, name=name, grid=(t // (n * c),),
        in_specs=[_rows(n * c, 3 * D_DN),
                  pl.BlockSpec((DN_HALO, 3 * D_DN), lambda i: (jnp.maximum(i * (n * c // DN_HALO) - 1, 0), 0)),
                  _full((DN_CONV, 3 * D_DN)), _rows(n * c, LANE), _full((1, LANE)), _full((1, LANE))],
        out_specs=[_rows(n * c, 3 * D_DN), wide, wide, _rows(n * c, DN_ATTN), wide, wide,
                   pl.BlockSpec((n, 1, LANE), lambda i: (i, 0, 0)), _rows(n * c, DN_ATTN)],
        out_shape=[_sds((t, 3 * D_DN)), _sds((t, D_DN)), _sds((t, D_DN), _MXU), _sds((t, DN_ATTN), _MXU),
                   _sds((t, D_DN), _MXU), _sds((t, D_DN), _MXU), _sds((t // c, 1, LANE)), _sds((t, DN_ATTN))],
        compiler_params=_cp(1, VMEM_BIG),
    )(zq, zq, convw, ab, alog, dtb)
    return outs[0], outs[1:7], outs[7]


def dn_local_bwd(qkv, ab, alog, dtb, inv, cts, name):
    t = qkv.shape[0]
    c, n = DN_CHUNK, DN_LOCAL_CHUNKS

    def body(qkv_ref, ab_ref, alog_ref, dtb_ref, inv_ref, dval_ref, dkcd_ref, dattn_ref, dqd_ref, dkd_ref, del_ref,
             dqkv_ref, dab_ref, dalog_ref, ddtb_ref):
        rows = [pl.ds(j * c, c) for j in range(n)]
        invs = [[inv_ref[r, h * c:(h + 1) * c] for h in range(DN_HEADS)] for r in rows]

        def local(qs, ks, vs, abs_, alog, dtb):
            return _dn_local(qs, ks, vs, abs_, alog, dtb, invs)[:6]

        _, vj = jax.vjp(local, [_dn_heads(qkv_ref, r) for r in rows], [_dn_heads(qkv_ref, r, D_DN) for r in rows],
                        [_dn_heads(qkv_ref, r, 2 * D_DN) for r in rows], [ab_ref[r, :] for r in rows], alog_ref[...],
                        dtb_ref[...])
        dattn = [[dattn_ref[r, h * c:(h + 1) * c] for h in range(DN_HEADS)] for r in rows]
        dq, dk, dv, dab, dalog, ddtb = vj(([_dn_heads(dval_ref, r) for r in rows], [_dn_heads(dkcd_ref, r) for r in rows],
                                           dattn, [_dn_heads(dqd_ref, r) for r in rows],
                                           [_dn_heads(dkd_ref, r) for r in rows], [del_ref[j] for j in range(n)]))
        for j, r in enumerate(rows):
            for h in range(DN_HEADS):
                lo, hi = h * DN_HEAD_DIM, (h + 1) * DN_HEAD_DIM
                dqkv_ref[r, lo:hi] = dq[j][h]
                dqkv_ref[r, D_DN + lo:D_DN + hi] = dk[j][h]
                dqkv_ref[r, 2 * D_DN + lo:2 * D_DN + hi] = dv[j][h]
            dab_ref[r, :] = dab[j].astype(_MXU)
        first = pl.program_id(0) == 0
        _acc(dalog_ref, dalog, first)
        _acc(ddtb_ref, ddtb, first)

    wide = _rows(n * c, D_DN)
    return pl.pallas_call(
        body, name=name, grid=(t // (n * c),),
        in_specs=[_rows(n * c, 3 * D_DN), _rows(n * c, LANE), _full((1, LANE)), _full((1, LANE)),
                  _rows(n * c, DN_ATTN), wide, wide, _rows(n * c, DN_ATTN), wide, wide,
                  pl.BlockSpec((n, 1, LANE), lambda i: (i, 0, 0))],
        out_specs=[_rows(n * c, 3 * D_DN), _rows(n * c, LANE), _full((1, LANE)), _full((1, LANE))],
        out_shape=[_sds((t, 3 * D_DN)), _sds((t, LANE), _MXU), _sds((1, LANE)), _sds((1, LANE))],
        compiler_params=_cp(1),
    )(qkv, ab, alog, dtb, inv, *cts)


def _seq_view(a, nb):
    return a.reshape((nb, a.shape[0] // nb) + a.shape[1:])


def _dn_chains(nb):
    return [(b, h) for b in range(nb) for h in range(DN_HEADS)]


DN_SCAN_CHUNKS = 4
DN_SCAN_FWD_CHUNKS = 8


def _dn_step_operands(val_ref, kcd_ref, attn_ref, qd_ref, kd_ref, el_ref, gg_ref, nb, j):
    chains = _dn_chains(nb)
    c = DN_CHUNK
    rows = pl.ds(j * c, c)

    def wide(ref):
        return [ref[b, rows, h * DN_HEAD_DIM:(h + 1) * DN_HEAD_DIM].astype(f32) for b, h in chains]

    attns = [attn_ref[b, rows, h * c:(h + 1) * c].astype(f32) for b, h in chains]
    return (wide(val_ref), wide(kcd_ref), attns, wide(qd_ref), wide(kd_ref),
            [el_ref[b, j, :, h:h + 1] for b, h in chains], wide(gg_ref))


def dn_scan_fwd(loc, gg, ng, nb, name):
    val, kcd, attn, qd, kd, el = loc
    t = val.shape[0]
    c, n = DN_CHUNK, DN_SCAN_FWD_CHUNKS
    nc = t // nb // c
    ns = nb * DN_HEADS

    def body(val_ref, kcd_ref, attn_ref, qd_ref, kd_ref, el_ref, gg_ref, ng_ref, y_ref, ss_ref, st):
        @pl.when(pl.program_id(0) == 0)
        def _():
            st[...] = jnp.zeros_like(st)

        sts = [st[i] for i in range(ns)]
        for j in range(n):
            for i in range(ns):
                ss_ref[j, i] = sts[i]
            ys, sts = _dn_step(*_dn_step_operands(val_ref, kcd_ref, attn_ref, qd_ref, kd_ref, el_ref, gg_ref, nb, j),
                               sts, ng_ref[...])
            for i, (b, h) in enumerate(_dn_chains(nb)):
                y_ref[b, pl.ds(j * c, c), h * DN_HEAD_DIM:(h + 1) * DN_HEAD_DIM] = ys[i]
        for i in range(ns):
            st[i] = sts[i]

    def blk(w):
        return pl.BlockSpec((nb, n * c, w), lambda k: (0, k, 0))

    el_spec = pl.BlockSpec((nb, n, 1, LANE), lambda k: (0, k, 0, 0))
    y, ss = pl.pallas_call(
        body, name=name, grid=(nc // n,),
        in_specs=[blk(D_DN), blk(D_DN), blk(DN_ATTN), blk(D_DN), blk(D_DN), el_spec, blk(D_DN), _full((1, LANE))],
        out_specs=[blk(D_DN), pl.BlockSpec((n, ns, DN_HEAD_DIM, DN_HEAD_DIM), lambda k: (k, 0, 0, 0))],
        out_shape=[_sds((nb, t // nb, D_DN)), _sds((nc, ns, DN_HEAD_DIM, DN_HEAD_DIM))],
        scratch_shapes=[pltpu.VMEM((ns, DN_HEAD_DIM, DN_HEAD_DIM), f32)],
        compiler_params=_cp(1, VMEM_BIG),
    )(_seq_view(val, nb), _seq_view(kcd, nb), _seq_view(attn, nb), _seq_view(qd, nb), _seq_view(kd, nb),
      el.reshape(nb, nc, 1, LANE), _seq_view(gg, nb), ng)
    return y.reshape(t, D_DN), ss


def dn_scan_bwd(loc, gg, ng, ss, dy, nb, name):
    val, kcd, attn, qd, kd, el = loc
    t = val.shape[0]
    c, n = DN_CHUNK, DN_SCAN_CHUNKS
    nc = t // nb // c
    ns = nb * DN_HEADS
    steps = nc // n

    def body(val_ref, kcd_ref, attn_ref, qd_ref, kd_ref, el_ref, gg_ref, ng_ref, ss_ref, dy_ref,
             dval_ref, dkcd_ref, dattn_ref, dqd_ref, dkd_ref, del_ref, dgg_ref, dng_ref, dst):
        @pl.when(pl.program_id(0) == 0)
        def _():
            dst[...] = jnp.zeros_like(dst)

        lane = lax.broadcasted_iota(jnp.int32, (1, LANE), 1)
        chains = _dn_chains(nb)
        ds = [dst[i] for i in range(ns)]
        dng_tot = jnp.zeros((1, LANE), f32)
        for j in reversed(range(n)):
            rows = pl.ds(j * c, c)
            _, vj = jax.vjp(_dn_step,
                            *_dn_step_operands(val_ref, kcd_ref, attn_ref, qd_ref, kd_ref, el_ref, gg_ref, nb, j),
                            [ss_ref[j, i] for i in range(ns)], ng_ref[...])
            dys = [dy_ref[b, rows, h * DN_HEAD_DIM:(h + 1) * DN_HEAD_DIM] for b, h in chains]
            dval, dkcd, dattn, dqd, dkd, dlast, dgg, ds, dng = vj((dys, ds))
            dng_tot = dng_tot + dng
            del_rows = [jnp.zeros((1, LANE), f32) for _ in range(nb)]
            for i, (b, h) in enumerate(chains):
                cols = slice(h * DN_HEAD_DIM, (h + 1) * DN_HEAD_DIM)
                dval_ref[b, rows, cols] = dval[i]
                dkcd_ref[b, rows, cols] = dkcd[i]
                dattn_ref[b, rows, h * c:(h + 1) * c] = dattn[i]
                dqd_ref[b, rows, cols] = dqd[i]
                dkd_ref[b, rows, cols] = dkd[i]
                dgg_ref[b, rows, cols] = dgg[i].astype(_MXU)
                del_rows[b] = del_rows[b] + jnp.where(lane == h, dlast[i], 0.0)
            for b in range(nb):
                del_ref[b, j] = del_rows[b]
        for i in range(ns):
            dst[i] = ds[i]
        _acc(dng_ref, dng_tot, pl.program_id(0) == 0)

    def blk(w):
        return pl.BlockSpec((nb, n * c, w), lambda k: (0, steps - 1 - k, 0))

    el_spec = pl.BlockSpec((nb, n, 1, LANE), lambda k: (0, steps - 1 - k, 0, 0))
    outs = pl.pallas_call(
        body, name=name, grid=(steps,),
        in_specs=[blk(D_DN), blk(D_DN), blk(DN_ATTN), blk(D_DN), blk(D_DN), el_spec, blk(D_DN), _full((1, LANE)),
                  pl.BlockSpec((n, ns, DN_HEAD_DIM, DN_HEAD_DIM), lambda k: (steps - 1 - k, 0, 0, 0)), blk(D_DN)],
        out_specs=[blk(D_DN), blk(D_DN), blk(DN_ATTN), blk(D_DN), blk(D_DN), el_spec, blk(D_DN), _full((1, LANE))],
        out_shape=[_sds((nb, t // nb, D_DN)), _sds((nb, t // nb, D_DN)), _sds((nb, t // nb, DN_ATTN)),
                   _sds((nb, t // nb, D_DN)), _sds((nb, t // nb, D_DN)), _sds((nb, nc, 1, LANE)),
                   _sds((nb, t // nb, D_DN), _MXU), _sds((1, LANE))],
        scratch_shapes=[pltpu.VMEM((ns, DN_HEAD_DIM, DN_HEAD_DIM), f32)],
        compiler_params=_cp(1, VMEM_BIG),
    )(_seq_view(val, nb), _seq_view(kcd, nb), _seq_view(attn, nb), _seq_view(qd, nb), _seq_view(kd, nb),
      el.reshape(nb, nc, 1, LANE), _seq_view(gg, nb), ng, ss, _seq_view(dy, nb))
    dloc = [o.reshape((t,) + o.shape[2:]) for o in outs[:5]] + [outs[5].reshape(t // c, 1, LANE)]
    return dloc, outs[6].reshape(t, D_DN), outs[7]


SG_ROWS = 1024


def sg_fwd(z, lng, lnb, w, bt, name):
    t = z.shape[0]

    def body(z_ref, lng_ref, lnb_ref, w_ref, bt_ref, y_ref):
        ws = [w_ref[h] for h in range(SG_HEADS)]
        for k in range(SG_ROWS // SG_CHUNK):
            r = pl.ds(k * SG_CHUNK, SG_CHUNK)
            y_ref[r, :] = _sg_chunk(z_ref[r, :D_SG], z_ref[r, D_SG:2 * D_SG], z_ref[r, 2 * D_SG:], lng_ref[...],
                                    lnb_ref[...], ws, bt_ref[...])

    return pl.pallas_call(
        body, name=name, grid=(t // SG_ROWS,),
        in_specs=[_rows(SG_ROWS, 3 * D_SG), _full((1, D_SG)), _full((1, D_SG)),
                  _full((SG_HEADS, SG_CHUNK, SG_CHUNK)), _full((SG_CHUNK, LANE))],
        out_specs=_rows(SG_ROWS, D_SG),
        out_shape=_sds((t, D_SG)),
        compiler_params=_cp(1),
    )(z, lng, lnb, w, bt)


def sg_bwd(z, lng, lnb, w, bt, dy, name):
    t = z.shape[0]

    def body(z_ref, lng_ref, lnb_ref, w_ref, bt_ref, dy_ref, dz_ref, dlng_ref, dlnb_ref, dw_ref, dbt_ref):
        ws = [w_ref[h] for h in range(SG_HEADS)]
        tot = None
        for k in range(SG_ROWS // SG_CHUNK):
            r = pl.ds(k * SG_CHUNK, SG_CHUNK)
            _, vj = jax.vjp(_sg_chunk, z_ref[r, :D_SG], z_ref[r, D_SG:2 * D_SG], z_ref[r, 2 * D_SG:], lng_ref[...],
                            lnb_ref[...], ws, bt_ref[...])
            du, dv, dgate, dlng, dlnb, dws, dbt = vj(dy_ref[r, :])
            dz_ref[r, :D_SG] = du.astype(_MXU)
            dz_ref[r, D_SG:2 * D_SG] = dv.astype(_MXU)
            dz_ref[r, 2 * D_SG:] = dgate.astype(_MXU)
            part = [dlng, dlnb, dbt] + list(dws)
            tot = part if tot is None else [a + b for a, b in zip(tot, part)]
        first = pl.program_id(0) == 0
        _acc(dlng_ref, tot[0], first)
        _acc(dlnb_ref, tot[1], first)
        _acc(dbt_ref, tot[2], first)
        for h in range(SG_HEADS):
            @pl.when(first)
            def _():
                dw_ref[h] = tot[3 + h]

            @pl.when(jnp.logical_not(first))
            def _():
                dw_ref[h] += tot[3 + h]

    return pl.pallas_call(
        body, name=name, grid=(t // SG_ROWS,),
        in_specs=[_rows(SG_ROWS, 3 * D_SG), _full((1, D_SG)), _full((1, D_SG)),
                  _full((SG_HEADS, SG_CHUNK, SG_CHUNK)), _full((SG_CHUNK, LANE)), _rows(SG_ROWS, D_SG)],
        out_specs=[_rows(SG_ROWS, 3 * D_SG), _full((1, D_SG)), _full((1, D_SG)),
                   _full((SG_HEADS, SG_CHUNK, SG_CHUNK)), _full((SG_CHUNK, LANE))],
        out_shape=[_sds((t, 3 * D_SG), _MXU), _sds((1, D_SG)), _sds((1, D_SG)), _sds((SG_HEADS, SG_CHUNK, SG_CHUNK)),
                   _sds((SG_CHUNK, LANE))],
        compiler_params=_cp(1),
    )(z, lng, lnb, w, bt, dy)


def add_pairs(a_list, b_list, name):
    n = len(a_list)

    def body(*refs):
        for a_ref, b_ref, o_ref in zip(refs[:n], refs[n:2 * n], refs[2 * n:]):
            o_ref[...] = (a_ref[...].astype(f32) + b_ref[...].astype(f32)).astype(o_ref.dtype)

    return pl.pallas_call(
        body, name=name, out_shape=[_sds(a.shape, a.dtype) for a in a_list],
        compiler_params=pltpu.CompilerParams(vmem_limit_bytes=VMEM_BIG),
    )(*a_list, *b_list)


def _adamw(g, w, m, v):
    nm = ADAM_B1 * m + (1.0 - ADAM_B1) * g
    nv = ADAM_B2 * v + (1.0 - ADAM_B2) * jnp.square(g)
    m_hat = nm / (1.0 - ADAM_B1 ** ADAM_STEP)
    v_hat = nv / (1.0 - ADAM_B2 ** ADAM_STEP)
    return -ADAM_LR * (m_hat / (jnp.sqrt(v_hat) + ADAM_EPS) + ADAM_WD * w), nm, nv


def sum_parts(half, recv, name):
    _, r, c = recv.shape
    tr = 256 if r % 256 == 0 else r

    def body(half_ref, recv_ref, g_ref):
        g = recv_ref[0].astype(f32)
        for k in range(1, N_CHIPS):
            g = g + recv_ref[k].astype(f32)
        g_ref[...] = g

    return pl.pallas_call(
        body, name=name,
        grid_spec=pltpu.PrefetchScalarGridSpec(
            num_scalar_prefetch=1, grid=(r // tr,),
            in_specs=[pl.BlockSpec((N_CHIPS, tr, c), lambda i, h: (0, i, 0))],
            out_specs=pl.BlockSpec((None, tr, c), lambda i, h: (h[0], i, 0))),
        out_shape=_sds((2, r, c)),
        compiler_params=_cp(1, VMEM_BIG),
    )(half, recv)


def adamw(g, w, m, v, name):
    _, r, c = w.shape
    tr = 256 if r % 256 == 0 else r

    def body(g_ref, w_ref, m_ref, v_ref, d_ref, nm_ref, nv_ref):
        d_ref[...], nm_ref[...], nv_ref[...] = _adamw(g_ref[...], w_ref[...], m_ref[...], v_ref[...])

    blk = pl.BlockSpec((None, tr, c), lambda l, i: (l, i, 0))
    return pl.pallas_call(
        body, name=name, grid=(2, r // tr), in_specs=[blk] * 4, out_specs=[blk] * 3, out_shape=[_sds((2, r, c))] * 3,
        compiler_params=_cp(2, VMEM_BIG),
    )(g, w, m, v)


def sum_parts_small(chip, parts, sums, name):
    n = len(sums)

    def body(chip_ref, *refs):
        for part, own, out in zip(refs[:n], refs[n:2 * n], refs[2 * n:]):
            g = jnp.where(chip_ref[0] == 0, own[...], part[0])
            for q in range(1, N_CHIPS):
                g = g + jnp.where(chip_ref[0] == q, own[...], part[q])
            out[...] = g

    vmem = pl.BlockSpec(memory_space=pltpu.VMEM)
    return pl.pallas_call(
        body, name=name, in_specs=[pl.BlockSpec(memory_space=pltpu.SMEM)] + [vmem] * (2 * n), out_specs=[vmem] * n,
        out_shape=[_sds(s.shape) for s in sums], compiler_params=pltpu.CompilerParams(vmem_limit_bytes=VMEM_BIG),
    )(chip, *parts, *sums)


def adamw_small(gs, ws, ms, vs, name):
    n = len(ws)

    def body(*refs):
        ins, outs = refs[:4 * n], refs[4 * n:]
        for k in range(n):
            outs[k][...], outs[n + k][...], outs[2 * n + k][...] = _adamw(
                ins[k][...], ins[n + k][...], ins[2 * n + k][...], ins[3 * n + k][...])

    outs = pl.pallas_call(
        body, name=name, out_shape=[_sds(w.shape) for w in ws] * 3,
        compiler_params=pltpu.CompilerParams(vmem_limit_bytes=VMEM_BIG),
    )(*gs, *ws, *ms, *vs)
    return [outs[j * n:(j + 1) * n] for j in range(3)]


_ANY = pl.BlockSpec(memory_space=pltpu.HBM)
_MESH = pl.DeviceIdType.MESH


def _flip(v, bit):
    return 1 - v if bit else v


_CHIP_RELS = ((1, 0), (0, 1), (1, 1))


def _piece(ref, kind, q):
    if kind[0] == "slot":
        return ref.at[q]
    if kind[0] == "all":
        return ref
    _, axis, n = kind
    return ref.at[(slice(None),) * axis + (pl.ds(q * n, n),)]


def _piece_shape(shape, kind):
    if kind[0] == "slot":
        return tuple(shape[1:])
    if kind[0] == "all":
        return tuple(shape)
    _, axis, n = kind
    return tuple(shape[:axis]) + (n,) + tuple(shape[axis + 1:])


def gather_weights(shards, kinds, name):
    n = len(shards)

    def out_shape(s, kind):
        if kind[0] == "slot":
            return (N_CHIPS,) + tuple(s.shape)
        _, axis, w = kind
        return tuple(s.shape[:axis + 1]) + (N_CHIPS * w,) + tuple(s.shape[axis + 2:])

    def place(o_ref, kind, q, layer):
        if kind[0] == "slot":
            return o_ref.at[q, layer]
        return _piece(o_ref.at[layer], kind, q)

    def body(*refs):
        s_refs, o_refs = refs[:n], refs[n:2 * n]
        send_sems, recv_sems, fwd_send_sems, fwd_recv_sems = refs[2 * n:]
        x, y, c = lax.axis_index("x"), lax.axis_index("y"), lax.axis_index("c")
        mine = 2 * x + y
        sends, arrivals, forwards, fwd_arrivals = [], [], [], []
        for r, (fx, fy) in enumerate(_CHIP_RELS):
            px, py = _flip(x, fx), _flip(y, fy)
            peer = 2 * px + py
            for k in range(n):
                s = r * n + k
                sends.append(pltpu.make_async_remote_copy(
                    src_ref=s_refs[k].at[c], dst_ref=place(o_refs[k], kinds[k], mine, c), send_sem=send_sems.at[s],
                    recv_sem=recv_sems.at[s], device_id=(px, py, c), device_id_type=_MESH))
                arrivals.append(pltpu.make_async_remote_copy(
                    src_ref=s_refs[k].at[c], dst_ref=place(o_refs[k], kinds[k], peer, c), send_sem=send_sems.at[s],
                    recv_sem=recv_sems.at[s], device_id=(px, py, c), device_id_type=_MESH))
                block = place(o_refs[k], kinds[k], peer, c)
                forwards.append(pltpu.make_async_remote_copy(
                    src_ref=block, dst_ref=block, send_sem=fwd_send_sems.at[s], recv_sem=fwd_recv_sems.at[s],
                    device_id=(x, y, 1 - c), device_id_type=_MESH))
                other = place(o_refs[k], kinds[k], peer, 1 - c)
                fwd_arrivals.append(pltpu.make_async_remote_copy(
                    src_ref=other, dst_ref=other, send_sem=fwd_send_sems.at[s], recv_sem=fwd_recv_sems.at[s],
                    device_id=(x, y, 1 - c), device_id_type=_MESH))
        for cp in sends:
            cp.start()
        for arrived, fwd in zip(arrivals, forwards):
            arrived.wait_recv()
            fwd.start()
        for cp in fwd_arrivals:
            cp.wait_recv()
        for cp in sends + forwards:
            cp.wait_send()

    m = len(_CHIP_RELS) * n
    return pl.pallas_call(
        body, name=name, in_specs=[_ANY] * n, out_specs=[_ANY] * n,
        out_shape=[_sds(out_shape(s, k), s.dtype) for s, k in zip(shards, kinds)],
        scratch_shapes=[pltpu.SemaphoreType.DMA((m,))] * 4,
    )(*shards)


def _owned_by(owners, side):
    return [k for k, o in enumerate(owners) if o == side]


def exchange_halves(gs, smalls, owners, name):
    n, ns = len(gs), len(smalls)

    def body(*refs):
        g_refs, s_refs = refs[:n], refs[n:n + ns]
        got_refs, sgot_refs = refs[n + ns:2 * n + ns], refs[2 * n + ns:2 * (n + ns)]
        send_sems, recv_sems = refs[2 * (n + ns):]
        x, y, c = lax.axis_index("x"), lax.axis_index("y"), lax.axis_index("c")
        sibling = (x, y, 1 - c)
        swaps = [pltpu.make_async_remote_copy(
            src_ref=g_refs[k].at[1 - c], dst_ref=got_refs[k], send_sem=send_sems.at[k], recv_sem=recv_sems.at[k],
            device_id=sibling, device_id_type=_MESH) for k in range(n)]
        gives = [pltpu.make_async_remote_copy(
            src_ref=s_refs[k], dst_ref=sgot_refs[k], send_sem=send_sems.at[n + k], recv_sem=recv_sems.at[n + k],
            device_id=sibling, device_id_type=_MESH) for k in range(ns)]
        for cp in swaps:
            cp.start()
        for side in (0, 1):
            @pl.when(c == 1 - side)
            def _():
                for k in _owned_by(owners, side):
                    gives[k].start()
        for cp in swaps:
            cp.wait()
        for side in (0, 1):
            @pl.when(c == 1 - side)
            def _():
                for k in _owned_by(owners, side):
                    gives[k].wait_send()

            @pl.when(c == side)
            def _():
                for k in _owned_by(owners, side):
                    gives[k].wait_recv()

    outs = pl.pallas_call(
        body, name=name, in_specs=[_ANY] * (n + ns), out_specs=[_ANY] * (n + ns),
        out_shape=[_sds(g.shape[1:], g.dtype) for g in gs] + [_sds(s.shape, s.dtype) for s in smalls],
        scratch_shapes=[pltpu.SemaphoreType.DMA((n + ns,)), pltpu.SemaphoreType.DMA((n + ns,))],
    )(*gs, *smalls)
    return outs[:n], outs[n:]


def reduce_to_chips(ts, kinds, smalls, owners, name):
    n, ns = len(ts), len(smalls)

    def body(*refs):
        t_refs, s_refs = refs[:n], refs[n:n + ns]
        o_refs, so_refs = refs[n + ns:2 * n + ns], refs[2 * n + ns:2 * (n + ns)]
        send_sems, recv_sems = refs[2 * (n + ns):]
        x, y, c = lax.axis_index("x"), lax.axis_index("y"), lax.axis_index("c")
        mine = 2 * x + y
        sends, arrivals, small_sends, small_arrivals = [], [], [], []
        for r, (fx, fy) in enumerate(_CHIP_RELS):
            px, py = _flip(x, fx), _flip(y, fy)
            peer = 2 * px + py
            for k in range(n + ns):
                s = r * (n + ns) + k
                if k < n:
                    src, dst = _piece(t_refs[k], kinds[k], peer), o_refs[k]
                else:
                    src, dst = s_refs[k - n], so_refs[k - n]
                go = pltpu.make_async_remote_copy(
                    src_ref=src, dst_ref=dst.at[mine], send_sem=send_sems.at[s], recv_sem=recv_sems.at[s],
                    device_id=(px, py, c), device_id_type=_MESH)
                come = pltpu.make_async_remote_copy(
                    src_ref=src, dst_ref=dst.at[peer], send_sem=send_sems.at[s], recv_sem=recv_sems.at[s],
                    device_id=(px, py, c), device_id_type=_MESH)
                (sends if k < n else small_sends).append(go)
                (arrivals if k < n else small_arrivals).append(come)

        def owned(copies, side):
            return [cp for j, cp in enumerate(copies) if owners[j % ns] == side]

        for cp in sends:
            cp.start()
        for side in (0, 1):
            @pl.when(c == side)
            def _():
                for cp in owned(small_sends, side):
                    cp.start()
        for cp in arrivals:
            cp.wait_recv()
        for cp in sends:
            cp.wait_send()
        for side in (0, 1):
            @pl.when(c == side)
            def _():
                for cp in owned(small_arrivals, side):
                    cp.wait_recv()
                for cp in owned(small_sends, side):
                    cp.wait_send()

    m = len(_CHIP_RELS) * (n + ns)
    outs = pl.pallas_call(
        body, name=name, in_specs=[_ANY] * (n + ns), out_specs=[_ANY] * (n + ns),
        out_shape=[_sds((N_CHIPS,) + _piece_shape(t.shape, k), t.dtype) for t, k in zip(ts, kinds)]
        + [_sds((N_CHIPS,) + s.shape, s.dtype) for s in smalls],
        scratch_shapes=[pltpu.SemaphoreType.DMA((m,)), pltpu.SemaphoreType.DMA((m,))],
    )(*ts, *smalls)
    return outs[:n], outs[n:]


def share_halves(rs, smalls, owners, name):
    n, ns = len(rs), len(smalls)

    def body(*refs):
        o_refs, so_refs = refs[n + ns:2 * n + ns], refs[2 * n + ns:2 * (n + ns)]
        send_sems, recv_sems = refs[2 * (n + ns):]
        x, y, c = lax.axis_index("x"), lax.axis_index("y"), lax.axis_index("c")
        sibling = (x, y, 1 - c)
        swaps = [pltpu.make_async_remote_copy(
            src_ref=o_refs[k].at[c], dst_ref=o_refs[k].at[c], send_sem=send_sems.at[k], recv_sem=recv_sems.at[k],
            device_id=sibling, device_id_type=_MESH) for k in range(n)]
        arrivals = [pltpu.make_async_remote_copy(
            src_ref=o_refs[k].at[c], dst_ref=o_refs[k].at[1 - c], send_sem=send_sems.at[k], recv_sem=recv_sems.at[k],
            device_id=sibling, device_id_type=_MESH) for k in range(n)]
        gives = [pltpu.make_async_remote_copy(
            src_ref=so_refs[k], dst_ref=so_refs[k], send_sem=send_sems.at[n + k], recv_sem=recv_sems.at[n + k],
            device_id=sibling, device_id_type=_MESH) for k in range(ns)]
        for cp in swaps:
            cp.start()
        for side in (0, 1):
            @pl.when(c == side)
            def _():
                for k in _owned_by(owners, side):
                    gives[k].start()
        for cp in arrivals:
            cp.wait_recv()
        for cp in swaps:
            cp.wait_send()
        for side in (0, 1):
            @pl.when(c == side)
            def _():
                for k in _owned_by(owners, side):
                    gives[k].wait_send()

            @pl.when(c == 1 - side)
            def _():
                for k in _owned_by(owners, side):
                    gives[k].wait_recv()

    outs = pl.pallas_call(
        body, name=name, in_specs=[_ANY] * (n + ns), out_specs=[_ANY] * (n + ns),
        out_shape=[_sds(r.shape, r.dtype) for r in list(rs) + list(smalls)],
        input_output_aliases={k: k for k in range(n + ns)},
        scratch_shapes=[pltpu.SemaphoreType.DMA((n + ns,)), pltpu.SemaphoreType.DMA((n + ns,))],
    )(*rs, *smalls)
    return outs[:n], outs[n:]


def _small_view(a):
    if a.size < 8 * LANE:
        return jnp.pad(a.reshape(-1), (0, 8 * LANE - a.size)).reshape(8, LANE)
    if a.ndim == 1:
        return a.reshape(1, a.shape[0])
    if a.ndim == 4 and a.shape[-1] < LANE:
        return a.reshape(a.shape[0], a.shape[1], a.shape[2] * a.shape[3])
    return a


def _permuted_from_shards(shards):
    parts = []
    for lo, hi in GROUP_COLS:
        for q in range(N_CHIPS):
            a, b = max(lo, q * SHARD_COLS), min(hi, (q + 1) * SHARD_COLS)
            if a < b:
                parts.append(shards[q][..., a - q * SHARD_COLS:b - q * SHARD_COLS])
    pad = jnp.zeros(shards[0].shape[:-1] + (D_IN_PAD - D_IN,), shards[0].dtype)
    return jnp.concatenate(parts + [pad], axis=-1)


def _shards_from_groups(groups):
    in_order = sorted(range(len(GROUP_COLS)), key=lambda j: GROUP_COLS[j][0])
    shards = []
    for q in range(N_CHIPS):
        parts = []
        for j in in_order:
            lo, hi = GROUP_COLS[j]
            a, b = max(lo, q * SHARD_COLS), min(hi, (q + 1) * SHARD_COLS)
            if a < b:
                parts.append(groups[j][..., a - lo:b - lo])
        shards.append(jnp.concatenate(parts, axis=-1))
    return shards


def _expand_b(b):
    eye = jnp.eye(SSM_GROUPS, dtype=b.dtype)
    return jnp.einsum("gnc,gh->gchn", b, eye).reshape(D_SSM, N_STATE)


def _extract_b(e):
    return jnp.einsum("gcgn->gnc", e.reshape(SSM_GROUPS, SSM_GROUP, SSM_GROUPS, SSM_STATE))


def _expand_c(c):
    eye = jnp.eye(SSM_GROUPS, dtype=c.dtype)
    return jnp.einsum("gcn,gh->gnhc", c, eye).reshape(N_STATE, D_SSM)


def _extract_c(e):
    return jnp.einsum("gngc->gcn", e.reshape(SSM_GROUPS, SSM_STATE, SSM_GROUPS, SSM_GROUP))


def _lane_row(v):
    return jnp.pad(v, (0, LANE - v.shape[0])).reshape(1, LANE)


def _layer_params(w, l):
    return dict(
        norm_g=w["norm_g"][l][None], win=w["w_in_perm"][l], wout=w["w_out"][l].astype(_MXU),
        pg=w["ple_norm_g"][l][None], wgate=w["w_ple_gate"][l].astype(_MXU), wple=w["w_ple"][l].astype(_MXU),
        are=w["ssm_a_re"][l].reshape(1, N_STATE), aim=w["ssm_a_im"][l].reshape(1, N_STATE),
        ls=jnp.repeat(w["ssm_log_step"][l], SSM_STATE).reshape(1, N_STATE),
        bre=_expand_b(w["ssm_b_re"][l]), bim=_expand_b(w["ssm_b_im"][l]),
        cr=_expand_c(w["ssm_c_re"][l]), ci=_expand_c(w["ssm_c_im"][l]),
        dr=w["ssm_d"][l].reshape(1, D_SSM), wglu=w["ssm_w_glu"][l].astype(f32), bglu=w["ssm_b_glu"][l][None],
        convw=w["dn_conv_w"][l], alog=_lane_row(w["dn_a_log"][l]), dtb=_lane_row(w["dn_dt_bias"][l]),
        ng=w["dn_norm_g"][l][None],
        lng=w["sg_ln_g"][l][None], lnb=w["sg_ln_b"][l][None], sgw=w["sg_w"][l],
        bt=jnp.pad(w["sg_b"][l].T, ((0, 0), (0, LANE - SG_HEADS))),
    )


def _layer_fwd(x, p, lp, nb, tag):
    seq = x.shape[0] // nb
    h, zs, zq, zg, zsg, zab = in_fwd(x, lp["norm_g"], lp["win"], f"in_fwd{tag}")
    prep = s5_prep_fwd(lp["are"], lp["aim"], lp["ls"], lp["bre"], lp["bim"], f"s5_prep_fwd{tag}")
    s5p = tuple(prep) + (lp["cr"], lp["ci"], lp["dr"], lp["wglu"], lp["bglu"])
    ys, hs = s5_fwd(zs, s5p, nb, f"s5_fwd{tag}")
    qkv, loc, inv = dn_front_fwd(zq, lp["convw"], zab, lp["alog"], lp["dtb"], seq, f"dn_front_fwd{tag}")
    yd, ss = dn_scan_fwd(loc, zg, lp["ng"], nb, f"dn_scan_fwd{tag}")
    yg = sg_fwd(zsg, lp["lng"], lp["lnb"], lp["sgw"], lp["bt"], f"sg_fwd{tag}")
    x2, x1, y, hn = post_fwd(x, ys, yd, yg, p, lp["wout"], lp["pg"], lp["wgate"], lp["wple"], f"post_fwd{tag}")
    saved = dict(x=x, h=h, zs=zs, zq=zq, zg=zg, zsg=zsg, zab=zab, s5p=s5p, hs=hs, qkv=qkv, loc=loc, inv=inv, ss=ss, x1=x1, y=y, hn=hn, p=p)
    return x2, saved


def _layer_bwd(dx2, sv, lp, nb, tag):
    seq = dx2.shape[0] // nb
    dx1, dgp, dpp, dys, dyd, dyg, dpg = post_bwd(dx2, sv["x1"], sv["hn"], sv["p"], lp["wout"], lp["pg"], lp["wgate"],
                                                 lp["wple"], f"post_bwd{tag}")
    g = {}
    g["w_out"] = wgrad(sv["y"], dx1, f"wgrad_out{tag}")
    g["w_ple_gate"] = wgrad(sv["hn"], dgp, f"wgrad_gate{tag}")
    g["w_ple"] = wgrad(sv["p"], dpp, f"wgrad_ple{tag}")
    g["ple_norm_g"] = dpg[0]
    dzsg, dlng, dlnb, dsgw, dbt = sg_bwd(sv["zsg"], lp["lng"], lp["lnb"], lp["sgw"], lp["bt"], dyg, f"sg_bwd{tag}")
    g["sg_ln_g"], g["sg_ln_b"], g["sg_w"], g["sg_b"] = dlng[0], dlnb[0], dsgw, dbt[:, :SG_HEADS].T
    dloc, dzg, dng = dn_scan_bwd(sv["loc"], sv["zg"], lp["ng"], sv["ss"], dyd, nb, f"dn_scan_bwd{tag}")
    dqkv, dzab, dalog, ddtb = dn_local_bwd(sv["qkv"], sv["zab"], lp["alog"], lp["dtb"], sv["inv"], dloc,
                                           f"dn_local_bwd{tag}")
    dzq, dconv = dn_pre_bwd(sv["zq"], lp["convw"], dqkv, seq, f"dn_pre_bwd{tag}")
    g["dn_conv_w"], g["dn_a_log"], g["dn_dt_bias"], g["dn_norm_g"] = dconv, dalog[0, :DN_HEADS], ddtb[0, :DN_HEADS], dng[0]
    s5out = s5_bwd(sv["zs"], sv["s5p"], sv["hs"], dys, nb, f"s5_bwd{tag}")
    dzs, dprep, (dcr, dci, ddr, dwglu, dbglu) = s5out[0], s5out[1:1 + S5_PREPARED], s5out[1 + S5_PREPARED:]
    dare, daim, dls, dbre, dbim = s5_prep_bwd(lp["are"], lp["aim"], lp["ls"], lp["bre"], lp["bim"], dprep,
                                              f"s5_prep_bwd{tag}")
    g["ssm_a_re"] = dare.reshape(SSM_GROUPS, SSM_STATE)
    g["ssm_a_im"] = daim.reshape(SSM_GROUPS, SSM_STATE)
    g["ssm_log_step"] = dls.reshape(SSM_GROUPS, SSM_STATE).sum(axis=1)
    g["ssm_b_re"], g["ssm_b_im"] = _extract_b(dbre), _extract_b(dbim)
    g["ssm_c_re"], g["ssm_c_im"] = _extract_c(dcr), _extract_c(dci)
    g["ssm_d"] = ddr.reshape(SSM_GROUPS, SSM_GROUP)
    g["ssm_w_glu"], g["ssm_b_glu"] = dwglu, dbglu[0]
    dzs_all = (dzs, dzq, dzg, dzsg, dzab)
    dx, dng_in = in_bwd(sv["x"], lp["norm_g"], lp["win"], dzs_all, dx1, f"in_bwd{tag}")
    g["w_in_pieces"] = [wgrad(sv["h"], dz, f"wgrad_in{k}{tag}") for k, dz in enumerate(dzs_all)]
    g["norm_g"] = dng_in[0]
    return dx, g


def _local_step(x, p, target, w, nb):
    lps = [_layer_params(w, l) for l in range(DEPTH)]
    saved = []
    for l in range(DEPTH):
        x, sv = _layer_fwd(x, p[l], lps[l], nb, f"_l{l}")
        saved.append(sv)
    loss_blk, dx, dfg = loss_fwd_bwd(x, w["final_norm_g"][None], target, "loss")
    grads = [None] * DEPTH
    for l in reversed(range(DEPTH)):
        dx, grads[l] = _layer_bwd(dx, saved[l], lps[l], nb, f"_l{l}")
    out = {k: jnp.stack([grads[l][k] for l in range(DEPTH)]) for k in grads[0] if k != "w_in_pieces"}
    out["w_in_pieces"] = [grads[l]["w_in_pieces"] for l in range(DEPTH)]
    out["final_norm_g"] = dfg[0]
    return loss_blk[0, 0], dx, out


def kernel(x, p, norm_g, w_in, ssm_a_re, ssm_a_im, ssm_b_re, ssm_b_im, ssm_c_re, ssm_c_im, ssm_d, ssm_log_step, ssm_w_glu, ssm_b_glu, dn_conv_w, dn_a_log, dn_dt_bias, dn_norm_g, sg_ln_g, sg_ln_b, sg_w, sg_b, w_out, ple_norm_g, w_ple_gate, w_ple, final_norm_g, loss_target, m_norm_g, m_w_in, m_ssm_a_re, m_ssm_a_im, m_ssm_b_re, m_ssm_b_im, m_ssm_c_re, m_ssm_c_im, m_ssm_d, m_ssm_log_step, m_ssm_w_glu, m_ssm_b_glu, m_dn_conv_w, m_dn_a_log, m_dn_dt_bias, m_dn_norm_g, m_sg_ln_g, m_sg_ln_b, m_sg_w, m_sg_b, m_w_out, m_ple_norm_g, m_w_ple_gate, m_w_ple, m_final_norm_g, v_norm_g, v_w_in, v_ssm_a_re, v_ssm_a_im, v_ssm_b_re, v_ssm_b_im, v_ssm_c_re, v_ssm_c_im, v_ssm_d, v_ssm_log_step, v_ssm_w_glu, v_ssm_b_glu, v_dn_conv_w, v_dn_a_log, v_dn_dt_bias, v_dn_norm_g, v_sg_ln_g, v_sg_ln_b, v_sg_w, v_sg_b, v_w_out, v_ple_norm_g, v_w_ple_gate, v_w_ple, v_final_norm_g):
    args = locals()
    w = {n: args[n] for n in WEIGHTS}
    m = {n: args["m_" + n] for n in WEIGHTS}
    v = {n: args["v_" + n] for n in WEIGHTS}
    nb, seq = x.shape[0], x.shape[1]
    t = nb * seq

    full = _gather_full(w)
    loss_local, dx, grads = _local_step(x.reshape(t, D_MODEL), p.reshape(DEPTH, t, D_PLE).astype(_MXU),
                                        loss_target.reshape(t, D_MODEL), full, nb)
    outs, loss = _reduce_and_update(grads, w, m, v, loss_local)
    return (loss, dx.reshape(nb, seq, D_MODEL), *[outs[0][n] for n in WEIGHTS], *[outs[1][n] for n in WEIGHTS],
            *[outs[2][n] for n in WEIGHTS], *[outs[3][n] for n in WEIGHTS])


def _gather_full(w):
    sh_names = [n for n, _ in SHARDED]
    shards = [w[n] if n == "dn_conv_w" else w[n].astype(_COMM) for n in sh_names]
    gathered = gather_weights(shards, [k for _, k in SHARDED], "gather_weights")
    chip = 2 * lax.axis_index("x") + lax.axis_index("y")
    full = {n: w[n] for n in REPLICATED}
    for (n, kind), shard, got in zip(SHARDED, shards, gathered):
        if kind[0] == "slot":
            full[n] = lax.dynamic_update_index_in_dim(got, shard, chip, 0)
        else:
            full[n] = lax.dynamic_update_slice_in_dim(got, shard, chip * kind[2], axis=kind[1] + 1)
    slots = full.pop("w_in")
    full["w_in_perm"] = _permuted_from_shards([slots[q] for q in range(N_CHIPS)]).astype(_MXU)
    return full


def _reduce_and_update(grads, w, m, v, loss_local):
    sh_names = [n for n, _ in SHARDED]
    sh_kinds = [k for _, k in SHARDED]
    owners = [SMALL_OWNER[n] for n in REPLICATED + ("loss",)]

    def small_views(d):
        return [_small_view(d[n]) for n in REPLICATED]

    grads["w_in"] = jnp.stack([jnp.stack(_shards_from_groups(pieces)) for pieces in grads["w_in_pieces"]])
    gs = [grads[n] if n == "dn_conv_w" else grads[n].astype(_COMM) for n in sh_names]
    sm = small_views(grads) + [_small_view(loss_local.reshape(1))]
    core = lax.axis_index("c")
    chip = 2 * lax.axis_index("x") + lax.axis_index("y")
    got, sm_got = exchange_halves(gs, sm, owners, "exchange_halves")
    sums = add_pairs([lax.dynamic_index_in_dim(g, core, 0, keepdims=False) for g in gs] + sm, list(got) + list(sm_got),
                     "add_halves")
    sums, sm_sums = sums[:len(gs)], sums[len(gs):]
    parts, sm_parts = reduce_to_chips(sums, sh_kinds, sm_sums, owners, "reduce_to_chips")
    parts = list(parts)
    for k, (kind, total) in enumerate(zip(sh_kinds, sums)):
        if kind[0] == "slot":
            own = lax.dynamic_index_in_dim(total, chip, 0, keepdims=False)
        else:
            own = lax.dynamic_slice_in_dim(total, chip * kind[2], kind[2], axis=kind[1])
        parts[k] = lax.dynamic_update_index_in_dim(parts[k], own, chip, 0)
    half = core.astype(jnp.int32).reshape(1)
    totals = [sum_parts(half, part, f"sum_{n}") for n, part in zip(sh_names, parts)]
    sm_totals = sum_parts_small(chip.astype(jnp.int32).reshape(1), sm_parts, sm_sums, "sum_replicated")
    g_big, g_small = share_halves(totals, sm_totals, owners, "share_halves")
    outs = [dict(zip(sh_names, g_big)), {}, {}, {}]
    for n, g in zip(sh_names, g_big):
        outs[1][n], outs[2][n], outs[3][n] = adamw(g, w[n], m[n], v[n], f"adamw_{n}")
    small_results = [g_small[:-1]] + adamw_small(g_small[:-1], small_views(w), small_views(m), small_views(v),
                                                 "adamw_replicated")
    for j in range(4):
        for n, r in zip(REPLICATED, small_results[j]):
            outs[j][n] = r.reshape(-1)[:w[n].size].reshape(w[n].shape)
    return outs, g_small[-1][0, 0]
```
